```python
import jax, jax.numpy as jnp
from jax import lax
import numpy as np

D_MODEL = 1024
BATCH = 32
SEQ = 2048
DEPTH = 2

GRID_W = 64
CTX_LEN = 256
HEAD_DIM = 64
AXIS_DIM = HEAD_DIM // 2
ROPE_THETA = 10000.0
A_HEADS = 6
A_KV = 2
A_GROUP = A_HEADS // A_KV
C_HEADS = 6
C_KV = 2
C_GROUP = C_HEADS // C_KV
POOL_GROUPS = 4
POOL_CH = 64
POOL_WIDTH = POOL_GROUPS * POOL_CH
POOL_WINDOWS = (2, 4, 8, 16)
WINDOW = 128
Q_BLOCK = 128
BAND = Q_BLOCK + 2 * WINDOW
D_FF = 4 * D_MODEL
N_BRANCH = 3
A_QW = A_HEADS * HEAD_DIM
A_KVW = A_KV * HEAD_DIM
C_QW = C_HEADS * HEAD_DIM
C_KVW = C_KV * HEAD_DIM
IN_SPLITS = (A_QW, A_KVW, A_KVW, C_QW, C_KVW, C_KVW, POOL_WIDTH, D_MODEL, D_MODEL, D_MODEL)
IN_WIDTH = A_QW + 2 * A_KVW + C_QW + 2 * C_KVW + POOL_WIDTH + N_BRANCH * D_MODEL
EPS = 1e-6
NEG = -1e30

kernel_name = "hybrid_prefix_gqa_pool_window_block"


def rmsnorm(x, g):
    xf = x.astype(jnp.float32)
    y = xf * lax.rsqrt(jnp.mean(xf * xf, axis=-1, keepdims=True) + EPS)
    return (y * g.astype(jnp.float32)).astype(x.dtype)


def modulate(x, g, shift, scale):
    return rmsnorm(x, g) * (1 + scale) + shift


def adaln(v, w, b):
    m = jax.nn.silu(v) @ w + b
    return jnp.split(m, 6, axis=-1)


def split_in(z):
    idx = []
    acc = 0
    for s in IN_SPLITS[:-1]:
        acc += s
        idx.append(acc)
    return jnp.split(z, idx, axis=-1)


def heads_q(t, n_kv, n_group):
    b, l, _ = t.shape
    return t.reshape(b, l, n_kv, n_group, HEAD_DIM)


def heads_kv(t, n_kv):
    b, l, _ = t.shape
    return t.reshape(b, l, n_kv, HEAD_DIM)


def rope_tables(n_tok):
    rows = n_tok // GRID_W
    r = jnp.repeat(jnp.arange(rows, dtype=jnp.float32), GRID_W)
    col = jnp.tile(jnp.arange(GRID_W, dtype=jnp.float32), rows)
    inv = 1.0 / (ROPE_THETA ** (jnp.arange(0, AXIS_DIM, 2, dtype=jnp.float32) / AXIS_DIM))
    ang = jnp.concatenate([r[:, None] * inv, col[:, None] * inv], axis=-1)
    return jnp.cos(ang), jnp.sin(ang)


def apply_rope(x, cos, sin):
    shp = x.shape
    xr = x.reshape(shp[:-1] + (shp[-1] // 2, 2))
    x0, x1 = xr[..., 0], xr[..., 1]
    bshape = (shp[1],) + (1,) * (x.ndim - 3) + (shp[-1] // 2,)
    cs = cos.reshape(bshape).astype(x.dtype)
    sn = sin.reshape(bshape).astype(x.dtype)
    return jnp.stack([x0 * cs - x1 * sn, x0 * sn + x1 * cs], axis=-1).reshape(shp)


def global_attn(q, k, v):
    b, s, hk, g, dh = q.shape
    nblk = s // Q_BLOCK
    qb = q.reshape(b, nblk, Q_BLOCK, hk, g, dh).transpose(1, 0, 2, 3, 4, 5)
    scale = dh ** -0.5

    def one_block(qi):
        sc = jnp.einsum('bqhgd,bkhd->bhgqk', qi, k).astype(jnp.float32) * scale
        p = jax.nn.softmax(sc, axis=-1).astype(v.dtype)
        return jnp.einsum('bhgqk,bkhd->bqhgd', p, v)

    o = lax.map(one_block, qb)
    return o.transpose(1, 0, 2, 3, 4, 5).reshape(b, s, hk * g * dh)


def window_attn(q, k, v, kc, vc, sink):
    b, s, hk, g, dh = q.shape
    nctx = kc.shape[1]
    nblk = s // Q_BLOCK
    scale = dh ** -0.5
    k_pad = jnp.pad(k, ((0, 0), (WINDOW, WINDOW), (0, 0), (0, 0)))
    v_pad = jnp.pad(v, ((0, 0), (WINDOW, WINDOW), (0, 0), (0, 0)))
    qb = q.reshape(b, nblk, Q_BLOCK, hk, g, dh).transpose(1, 0, 2, 3, 4, 5)
    sink_f = sink.astype(jnp.float32).reshape(1, hk, g, 1, 1)

    def one_block(args):
        qi, bi = args
        start = bi * Q_BLOCK
        kb = lax.dynamic_slice_in_dim(k_pad, start, BAND, axis=1)
        vb = lax.dynamic_slice_in_dim(v_pad, start, BAND, axis=1)
        qpos = start + jnp.arange(Q_BLOCK)
        kpos = start - WINDOW + jnp.arange(BAND)
        valid = (jnp.abs(qpos[:, None] - kpos[None, :]) <= WINDOW) & (kpos[None, :] >= 0) & (kpos[None, :] < s)
        s_loc = jnp.einsum('bqhgd,bkhd->bhgqk', qi, kb).astype(jnp.float32) * scale
        s_loc = jnp.where(valid, s_loc, NEG)
        s_ctx = jnp.einsum('bqhgd,bkhd->bhgqk', qi, kc).astype(jnp.float32) * scale
        s_sink = jnp.broadcast_to(sink_f, s_ctx.shape[:-1] + (1,))
        p = jax.nn.softmax(jnp.concatenate([s_ctx, s_loc, s_sink], axis=-1), axis=-1).astype(v.dtype)
        o = jnp.einsum('bhgqk,bkhd->bqhgd', p[..., :nctx], vc)
        return o + jnp.einsum('bhgqk,bkhd->bqhgd', p[..., nctx:nctx + BAND], vb)

    o = lax.map(one_block, (qb, jnp.arange(nblk)))
    return o.transpose(1, 0, 2, 3, 4, 5).reshape(b, s, hk * g * dh)


def ctx_attn(q, k, v, sink):
    b, n, hk, g, dh = q.shape
    sc = jnp.einsum('bqhgd,bkhd->bhgqk', q, k).astype(jnp.float32) * (dh ** -0.5)
    if sink is not None:
        sk = jnp.broadcast_to(sink.astype(jnp.float32).reshape(1, hk, g, 1, 1), sc.shape[:-1] + (1,))
        p = jax.nn.softmax(jnp.concatenate([sc, sk], axis=-1), axis=-1)[..., :-1]
    else:
        p = jax.nn.softmax(sc, axis=-1)
    o = jnp.einsum('bhgqk,bkhd->bqhgd', p.astype(v.dtype), v)
    return o.reshape(b, n, hk * g * dh)


def pool_mix(u, w_pool, pool_scale):
    b, n, _ = u.shape
    uf = u.astype(jnp.float32).reshape(b, n, POOL_GROUPS, POOL_CH)
    cs = jnp.concatenate([jnp.zeros((b, 1, POOL_GROUPS, POOL_CH), jnp.float32), jnp.cumsum(uf, axis=1)], axis=1)
    t = jnp.arange(n)
    outs = []
    for gi, w in enumerate(POOL_WINDOWS):
        lo = jnp.clip(t - w // 2, 0, n)
        hi = jnp.clip(t + w - w // 2, 0, n)
        tot = cs[:, hi, gi] - cs[:, lo, gi]
        cnt = (hi - lo).astype(jnp.float32)
        outs.append(tot / cnt[None, :, None] - uf[:, :, gi])
    pooled = jnp.stack(outs, axis=2).astype(u.dtype)
    mixed = jnp.einsum('blgc,gcd->blgd', pooled, w_pool).reshape(b, n, POOL_WIDTH)
    return mixed * pool_scale


def project(h, w_in, qn_a, kn_a, qn_c, kn_c):
    qa, ka, va, qc, kc, vc, u, ga, gb, gc = split_in(h @ w_in)
    qa = rmsnorm(heads_q(qa, A_KV, A_GROUP), qn_a)
    ka = rmsnorm(heads_kv(ka, A_KV), kn_a)
    va = heads_kv(va, A_KV)
    qc = rmsnorm(heads_q(qc, C_KV, C_GROUP), qn_c)
    kc = rmsnorm(heads_kv(kc, C_KV), kn_c)
    vc = heads_kv(vc, C_KV)
    return qa, ka, va, qc, kc, vc, u, ga, gb, gc


def merge(oa, ob, oc, ga, gb, gc, w_br_a, w_br_b, w_br_c, w_out):
    y = (jax.nn.sigmoid(ga) * (oa @ w_br_a)
         + jax.nn.sigmoid(gb) * (ob @ w_br_b)
         + jax.nn.sigmoid(gc) * (oc @ w_br_c))
    return y @ w_out


def mlp(h, w1, w2):
    return jnp.square(jax.nn.relu(h @ w1)) @ w2


def _fwd_setup_inputs(seed: int = 0) -> dict:
    key = jax.random.key(seed)
    ks = jax.random.split(key, 24)
    f32 = jnp.float32
    nrm = lambda k, shp, s: jax.random.normal(k, shp, f32) * s
    return {
        "x": nrm(ks[0], (BATCH, SEQ, D_MODEL), 1.0),
        "c": nrm(ks[1], (BATCH, D_MODEL), 1.0),
        "ctx": nrm(ks[2], (BATCH, CTX_LEN, D_MODEL), 1.0),
        "c_ctx": nrm(ks[3], (D_MODEL,), 1.0),
        "w_ada": nrm(ks[4], (DEPTH, D_MODEL, 6 * D_MODEL), 0.5 * D_MODEL ** -0.5),
        "b_ada": nrm(ks[5], (DEPTH, 6 * D_MODEL), 0.02),
        "norm1": 1.0 + nrm(ks[6], (DEPTH, D_MODEL), 0.1),
        "norm2": 1.0 + nrm(ks[7], (DEPTH, D_MODEL), 0.1),
        "w_in": nrm(ks[8], (DEPTH, D_MODEL, IN_WIDTH), D_MODEL ** -0.5),
        "q_norm_a": 1.0 + nrm(ks[9], (DEPTH, HEAD_DIM), 0.1),
        "k_norm_a": 1.0 + nrm(ks[10], (DEPTH, HEAD_DIM), 0.1),
        "q_norm_c": 1.0 + nrm(ks[11], (DEPTH, HEAD_DIM), 0.1),
        "k_norm_c": 1.0 + nrm(ks[12], (DEPTH, HEAD_DIM), 0.1),
        "sink_c": nrm(ks[13], (DEPTH, C_HEADS), 0.5),
        "w_pool": nrm(ks[14], (DEPTH, POOL_GROUPS, POOL_CH, POOL_CH), POOL_CH ** -0.5),
        "pool_scale": 1.0 + nrm(ks[15], (DEPTH, POOL_WIDTH), 0.1),
        "w_br_a": nrm(ks[16], (DEPTH, A_QW, D_MODEL), A_QW ** -0.5),
        "w_br_b": nrm(ks[17], (DEPTH, POOL_WIDTH, D_MODEL), POOL_WIDTH ** -0.5),
        "w_br_c": nrm(ks[18], (DEPTH, C_QW, D_MODEL), C_QW ** -0.5),
        "w_out": nrm(ks[19], (DEPTH, D_MODEL, D_MODEL), D_MODEL ** -0.5),
        "w_mlp1": nrm(ks[20], (DEPTH, D_MODEL, D_FF), D_MODEL ** -0.5),
        "w_mlp2": nrm(ks[21], (DEPTH, D_FF, D_MODEL), D_FF ** -0.5),
    }


def _fwd_reference(x, c, ctx, c_ctx, w_ada, b_ada, norm1, norm2, w_in, q_norm_a, k_norm_a, q_norm_c, k_norm_c,
              sink_c, w_pool, pool_scale, w_br_a, w_br_b, w_br_c, w_out, w_mlp1, w_mlp2):
    n_tok = x.shape[1]
    cos, sin = rope_tables(n_tok)
    xc = ctx
    for l in range(DEPTH):
        last = l == DEPTH - 1
        sh1, sc1, g1, sh2, sc2, g2 = [m[:, None, :] for m in adaln(c, w_ada[l], b_ada[l])]
        csh1, csc1, cg1, csh2, csc2, cg2 = adaln(c_ctx, w_ada[l], b_ada[l])

        hc = modulate(xc, norm1[l], csh1, csc1)
        cqa, cka, cva, cqc, ckc, cvc, cu, cga, cgb, cgc = project(
            hc, w_in[l], q_norm_a[l], k_norm_a[l], q_norm_c[l], k_norm_c[l])

        h = modulate(x, norm1[l], sh1, sc1)
        qa, ka, va, qc, kc, vc, u, ga, gb, gc = project(
            h, w_in[l], q_norm_a[l], k_norm_a[l], q_norm_c[l], k_norm_c[l])
        qa, ka = apply_rope(qa, cos, sin), apply_rope(ka, cos, sin)
        qc, kc = apply_rope(qc, cos, sin), apply_rope(kc, cos, sin)

        oa = global_attn(qa, jnp.concatenate([cka, ka], axis=1), jnp.concatenate([cva, va], axis=1))
        ob = pool_mix(u, w_pool[l], pool_scale[l])
        oc = window_attn(qc, kc, vc, ckc, cvc, sink_c[l])
        x = x + g1 * merge(oa, ob, oc, ga, gb, gc, w_br_a[l], w_br_b[l], w_br_c[l], w_out[l])
        x = x + g2 * mlp(modulate(x, norm2[l], sh2, sc2), w_mlp1[l], w_mlp2[l])

        if not last:
            coa = ctx_attn(cqa, cka, cva, None)
            cob = pool_mix(cu, w_pool[l], pool_scale[l])
            coc = ctx_attn(cqc, ckc, cvc, sink_c[l])
            xc = xc + cg1 * merge(coa, cob, coc, cga, cgb, cgc, w_br_a[l], w_br_b[l], w_br_c[l], w_out[l])
            xc = xc + cg2 * mlp(modulate(xc, norm2[l], csh2, csc2), w_mlp1[l], w_mlp2[l])
    return x


import jax as _jax
import jax.numpy as _jnp

TWIN_FORMAT = 'train_step'
FWD_PARAMS = ['x', 'c', 'ctx', 'c_ctx', 'w_ada', 'b_ada', 'norm1', 'norm2', 'w_in', 'q_norm_a', 'k_norm_a', 'q_norm_c', 'k_norm_c', 'sink_c', 'w_pool', 'pool_scale', 'w_br_a', 'w_br_b', 'w_br_c', 'w_out', 'w_mlp1', 'w_mlp2']
TWIN_WEIGHTS = ['c_ctx', 'w_ada', 'b_ada', 'norm1', 'norm2', 'w_in', 'q_norm_a', 'k_norm_a', 'q_norm_c', 'k_norm_c', 'sink_c', 'w_pool', 'pool_scale', 'w_br_a', 'w_br_b', 'w_br_c', 'w_out', 'w_mlp1', 'w_mlp2']
TWIN_DIFF_INPUT = 'x'
TWIN_INPUTS = ['x', 'c', 'ctx', 'c_ctx', 'w_ada', 'b_ada', 'norm1', 'norm2', 'w_in', 'q_norm_a', 'k_norm_a', 'q_norm_c', 'k_norm_c', 'sink_c', 'w_pool', 'pool_scale', 'w_br_a', 'w_br_b', 'w_br_c', 'w_out', 'w_mlp1', 'w_mlp2', 'loss_target', 'm_c_ctx', 'm_w_ada', 'm_b_ada', 'm_norm1', 'm_norm2', 'm_w_in', 'm_q_norm_a', 'm_k_norm_a', 'm_q_norm_c', 'm_k_norm_c', 'm_sink_c', 'm_w_pool', 'm_pool_scale', 'm_w_br_a', 'm_w_br_b', 'm_w_br_c', 'm_w_out', 'm_w_mlp1', 'm_w_mlp2', 'v_c_ctx', 'v_w_ada', 'v_b_ada', 'v_norm1', 'v_norm2', 'v_w_in', 'v_q_norm_a', 'v_k_norm_a', 'v_q_norm_c', 'v_k_norm_c', 'v_sink_c', 'v_w_pool', 'v_pool_scale', 'v_w_br_a', 'v_w_br_b', 'v_w_br_c', 'v_w_out', 'v_w_mlp1', 'v_w_mlp2']
TWIN_OUTPUTS = ['loss', 'grad_x', 'grad_c_ctx', 'grad_w_ada', 'grad_b_ada', 'grad_norm1', 'grad_norm2', 'grad_w_in', 'grad_q_norm_a', 'grad_k_norm_a', 'grad_q_norm_c', 'grad_k_norm_c', 'grad_sink_c', 'grad_w_pool', 'grad_pool_scale', 'grad_w_br_a', 'grad_w_br_b', 'grad_w_br_c', 'grad_w_out', 'grad_w_mlp1', 'grad_w_mlp2', 'delta_c_ctx', 'delta_w_ada', 'delta_b_ada', 'delta_norm1', 'delta_norm2', 'delta_w_in', 'delta_q_norm_a', 'delta_k_norm_a', 'delta_q_norm_c', 'delta_k_norm_c', 'delta_sink_c', 'delta_w_pool', 'delta_pool_scale', 'delta_w_br_a', 'delta_w_br_b', 'delta_w_br_c', 'delta_w_out', 'delta_w_mlp1', 'delta_w_mlp2', 'new_m_c_ctx', 'new_m_w_ada', 'new_m_b_ada', 'new_m_norm1', 'new_m_norm2', 'new_m_w_in', 'new_m_q_norm_a', 'new_m_k_norm_a', 'new_m_q_norm_c', 'new_m_k_norm_c', 'new_m_sink_c', 'new_m_w_pool', 'new_m_pool_scale', 'new_m_w_br_a', 'new_m_w_br_b', 'new_m_w_br_c', 'new_m_w_out', 'new_m_w_mlp1', 'new_m_w_mlp2', 'new_v_c_ctx', 'new_v_w_ada', 'new_v_b_ada', 'new_v_norm1', 'new_v_norm2', 'new_v_w_in', 'new_v_q_norm_a', 'new_v_k_norm_a', 'new_v_q_norm_c', 'new_v_k_norm_c', 'new_v_sink_c', 'new_v_w_pool', 'new_v_pool_scale', 'new_v_w_br_a', 'new_v_w_br_b', 'new_v_w_br_c', 'new_v_w_out', 'new_v_w_mlp1', 'new_v_w_mlp2']
TWIN_LEAF_KINDS = {'loss': 'loss', 'grad_x': 'grad_x', 'grad_c_ctx': 'grad_w', 'grad_w_ada': 'grad_w', 'grad_b_ada': 'grad_w', 'grad_norm1': 'grad_w', 'grad_norm2': 'grad_w', 'grad_w_in': 'grad_w', 'grad_q_norm_a': 'grad_w', 'grad_k_norm_a': 'grad_w', 'grad_q_norm_c': 'grad_w', 'grad_k_norm_c': 'grad_w', 'grad_sink_c': 'grad_w', 'grad_w_pool': 'grad_w', 'grad_pool_scale': 'grad_w', 'grad_w_br_a': 'grad_w', 'grad_w_br_b': 'grad_w', 'grad_w_br_c': 'grad_w', 'grad_w_out': 'grad_w', 'grad_w_mlp1': 'grad_w', 'grad_w_mlp2': 'grad_w', 'delta_c_ctx': 'delta_w', 'delta_w_ada': 'delta_w', 'delta_b_ada': 'delta_w', 'delta_norm1': 'delta_w', 'delta_norm2': 'delta_w', 'delta_w_in': 'delta_w', 'delta_q_norm_a': 'delta_w', 'delta_k_norm_a': 'delta_w', 'delta_q_norm_c': 'delta_w', 'delta_k_norm_c': 'delta_w', 'delta_sink_c': 'delta_w', 'delta_w_pool': 'delta_w', 'delta_pool_scale': 'delta_w', 'delta_w_br_a': 'delta_w', 'delta_w_br_b': 'delta_w', 'delta_w_br_c': 'delta_w', 'delta_w_out': 'delta_w', 'delta_w_mlp1': 'delta_w', 'delta_w_mlp2': 'delta_w', 'new_m_c_ctx': 'new_m', 'new_m_w_ada': 'new_m', 'new_m_b_ada': 'new_m', 'new_m_norm1': 'new_m', 'new_m_norm2': 'new_m', 'new_m_w_in': 'new_m', 'new_m_q_norm_a': 'new_m', 'new_m_k_norm_a': 'new_m', 'new_m_q_norm_c': 'new_m', 'new_m_k_norm_c': 'new_m', 'new_m_sink_c': 'new_m', 'new_m_w_pool': 'new_m', 'new_m_pool_scale': 'new_m', 'new_m_w_br_a': 'new_m', 'new_m_w_br_b': 'new_m', 'new_m_w_br_c': 'new_m', 'new_m_w_out': 'new_m', 'new_m_w_mlp1': 'new_m', 'new_m_w_mlp2': 'new_m', 'new_v_c_ctx': 'new_v', 'new_v_w_ada': 'new_v', 'new_v_b_ada': 'new_v', 'new_v_norm1': 'new_v', 'new_v_norm2': 'new_v', 'new_v_w_in': 'new_v', 'new_v_q_norm_a': 'new_v', 'new_v_k_norm_a': 'new_v', 'new_v_q_norm_c': 'new_v', 'new_v_k_norm_c': 'new_v', 'new_v_sink_c': 'new_v', 'new_v_w_pool': 'new_v', 'new_v_pool_scale': 'new_v', 'new_v_w_br_a': 'new_v', 'new_v_w_br_b': 'new_v', 'new_v_w_br_c': 'new_v', 'new_v_w_out': 'new_v', 'new_v_w_mlp1': 'new_v', 'new_v_w_mlp2': 'new_v'}


def _forward(args):
    return _fwd_reference(*[args[k] for k in FWD_PARAMS])


def _output_shape():
    out = _jax.eval_shape(lambda: _forward(_fwd_setup_inputs(0)))
    return out.shape, out.dtype

N_MICROBATCH = 1
ADAM_LR = 0.001
ADAM_B1 = 0.9
ADAM_B2 = 0.999
ADAM_EPS = 1e-08
ADAM_WD = 0.01
ADAM_STEP = 10
PER_EXAMPLE_BATCH_AXIS = {'x': 0, 'c': 0, 'ctx': 0, 'loss_target': 0}
SHARED_INPUTS = []
_WEIGHT_DTYPES = {'c_ctx': _jnp.float32, 'w_ada': _jnp.float32, 'b_ada': _jnp.float32, 'norm1': _jnp.float32, 'norm2': _jnp.float32, 'w_in': _jnp.float32, 'q_norm_a': _jnp.float32, 'k_norm_a': _jnp.float32, 'q_norm_c': _jnp.float32, 'k_norm_c': _jnp.float32, 'sink_c': _jnp.float32, 'w_pool': _jnp.float32, 'pool_scale': _jnp.float32, 'w_br_a': _jnp.float32, 'w_br_b': _jnp.float32, 'w_br_c': _jnp.float32, 'w_out': _jnp.float32, 'w_mlp1': _jnp.float32, 'w_mlp2': _jnp.float32}
MOMENT_SCALE = {'c_ctx': 3.603215e-01, 'w_ada': 7.371397e+00, 'b_ada': 1.556553e+01, 'norm1': 1.790324e+00, 'norm2': 2.641115e+01, 'w_in': 3.607746e-01, 'q_norm_a': 4.926178e-02, 'k_norm_a': 5.010779e-02, 'q_norm_c': 2.091288e-01, 'k_norm_c': 2.071992e-01, 'sink_c': 3.538918e-02, 'w_pool': 6.107401e-01, 'pool_scale': 6.063314e+00, 'w_br_a': 5.699407e-01, 'w_br_b': 1.640220e-01, 'w_br_c': 5.129322e-01, 'w_out': 7.175394e-01, 'w_mlp1': 8.739332e-01, 'w_mlp2': 3.694587e+00}


def _to_microbatches(a, axis):
    t = _jnp.moveaxis(a, axis, 0)
    t = t.reshape((N_MICROBATCH, t.shape[0] // N_MICROBATCH) + t.shape[1:])
    return _jnp.moveaxis(t, 1, axis + 1)


def setup_inputs(seed: int = 0) -> dict:
    inp = _fwd_setup_inputs(seed)
    key = _jax.random.fold_in(_jax.random.key(seed), 7919)
    shape, _ = _output_shape()
    out = dict(inp)
    out["loss_target"] = _jax.random.normal(_jax.random.fold_in(key, 0), shape, _jnp.float32)
    for i, name in enumerate(TWIN_WEIGHTS):
        w = inp[name].astype(_jnp.float32)
        if MOMENT_SCALE is None:
            s = _jnp.sqrt(_jnp.mean(_jnp.square(w)) + 1e-30)
        else:
            s = MOMENT_SCALE[name]
        km, kv = _jax.random.split(_jax.random.fold_in(key, i + 1))
        out[name] = w
        out["m_" + name] = s * _jax.random.normal(km, w.shape, _jnp.float32)
        out["v_" + name] = (s * s) * _jax.random.uniform(kv, w.shape, _jnp.float32, 0.5, 1.5)
    if N_MICROBATCH > 1:
        for name, axis in PER_EXAMPLE_BATCH_AXIS.items():
            out[name] = _to_microbatches(out[name], axis)
    return {'x': out['x'], 'c': out['c'], 'ctx': out['ctx'], 'c_ctx': out['c_ctx'], 'w_ada': out['w_ada'], 'b_ada': out['b_ada'], 'norm1': out['norm1'], 'norm2': out['norm2'], 'w_in': out['w_in'], 'q_norm_a': out['q_norm_a'], 'k_norm_a': out['k_norm_a'], 'q_norm_c': out['q_norm_c'], 'k_norm_c': out['k_norm_c'], 'sink_c': out['sink_c'], 'w_pool': out['w_pool'], 'pool_scale': out['pool_scale'], 'w_br_a': out['w_br_a'], 'w_br_b': out['w_br_b'], 'w_br_c': out['w_br_c'], 'w_out': out['w_out'], 'w_mlp1': out['w_mlp1'], 'w_mlp2': out['w_mlp2'], 'loss_target': out['loss_target'], 'm_c_ctx': out['m_c_ctx'], 'm_w_ada': out['m_w_ada'], 'm_b_ada': out['m_b_ada'], 'm_norm1': out['m_norm1'], 'm_norm2': out['m_norm2'], 'm_w_in': out['m_w_in'], 'm_q_norm_a': out['m_q_norm_a'], 'm_k_norm_a': out['m_k_norm_a'], 'm_q_norm_c': out['m_q_norm_c'], 'm_k_norm_c': out['m_k_norm_c'], 'm_sink_c': out['m_sink_c'], 'm_w_pool': out['m_w_pool'], 'm_pool_scale': out['m_pool_scale'], 'm_w_br_a': out['m_w_br_a'], 'm_w_br_b': out['m_w_br_b'], 'm_w_br_c': out['m_w_br_c'], 'm_w_out': out['m_w_out'], 'm_w_mlp1': out['m_w_mlp1'], 'm_w_mlp2': out['m_w_mlp2'], 'v_c_ctx': out['v_c_ctx'], 'v_w_ada': out['v_w_ada'], 'v_b_ada': out['v_b_ada'], 'v_norm1': out['v_norm1'], 'v_norm2': out['v_norm2'], 'v_w_in': out['v_w_in'], 'v_q_norm_a': out['v_q_norm_a'], 'v_k_norm_a': out['v_k_norm_a'], 'v_q_norm_c': out['v_q_norm_c'], 'v_k_norm_c': out['v_k_norm_c'], 'v_sink_c': out['v_sink_c'], 'v_w_pool': out['v_w_pool'], 'v_pool_scale': out['v_pool_scale'], 'v_w_br_a': out['v_w_br_a'], 'v_w_br_b': out['v_w_br_b'], 'v_w_br_c': out['v_w_br_c'], 'v_w_out': out['v_w_out'], 'v_w_mlp1': out['v_w_mlp1'], 'v_w_mlp2': out['v_w_mlp2']}


def _loss(weights, diff, rest, loss_target):
    with _jax.named_scope("forward"):
        args = {**rest, TWIN_DIFF_INPUT: diff, **{k: w.astype(_WEIGHT_DTYPES[k]) for k, w in weights.items()}}
        y = _forward(args)
    with _jax.named_scope("loss_head"):
        err = _jnp.square(y.astype(_jnp.float32) - loss_target)
        return 0.5 * _jnp.sum(_jnp.mean(err, axis=-1)) if err.ndim else 0.5 * err


def _adamw(w, g, m, v):
    m = ADAM_B1 * m + (1.0 - ADAM_B1) * g
    v = ADAM_B2 * v + (1.0 - ADAM_B2) * _jnp.square(g)
    m_hat = m / (1.0 - ADAM_B1 ** ADAM_STEP)
    v_hat = v / (1.0 - ADAM_B2 ** ADAM_STEP)
    delta = -ADAM_LR * (m_hat / (_jnp.sqrt(v_hat) + ADAM_EPS) + ADAM_WD * w)
    return delta, m, v


def reference(x, c, ctx, c_ctx, w_ada, b_ada, norm1, norm2, w_in, q_norm_a, k_norm_a, q_norm_c, k_norm_c, sink_c, w_pool, pool_scale, w_br_a, w_br_b, w_br_c, w_out, w_mlp1, w_mlp2, loss_target, m_c_ctx, m_w_ada, m_b_ada, m_norm1, m_norm2, m_w_in, m_q_norm_a, m_k_norm_a, m_q_norm_c, m_k_norm_c, m_sink_c, m_w_pool, m_pool_scale, m_w_br_a, m_w_br_b, m_w_br_c, m_w_out, m_w_mlp1, m_w_mlp2, v_c_ctx, v_w_ada, v_b_ada, v_norm1, v_norm2, v_w_in, v_q_norm_a, v_k_norm_a, v_q_norm_c, v_k_norm_c, v_sink_c, v_w_pool, v_pool_scale, v_w_br_a, v_w_br_b, v_w_br_c, v_w_out, v_w_mlp1, v_w_mlp2):
    given = dict(x=x, c=c, ctx=ctx, c_ctx=c_ctx, w_ada=w_ada, b_ada=b_ada, norm1=norm1, norm2=norm2, w_in=w_in, q_norm_a=q_norm_a, k_norm_a=k_norm_a, q_norm_c=q_norm_c, k_norm_c=k_norm_c, sink_c=sink_c, w_pool=w_pool, pool_scale=pool_scale, w_br_a=w_br_a, w_br_b=w_br_b, w_br_c=w_br_c, w_out=w_out, w_mlp1=w_mlp1, w_mlp2=w_mlp2, loss_target=loss_target, m_c_ctx=m_c_ctx, m_w_ada=m_w_ada, m_b_ada=m_b_ada, m_norm1=m_norm1, m_norm2=m_norm2, m_w_in=m_w_in, m_q_norm_a=m_q_norm_a, m_k_norm_a=m_k_norm_a, m_q_norm_c=m_q_norm_c, m_k_norm_c=m_k_norm_c, m_sink_c=m_sink_c, m_w_pool=m_w_pool, m_pool_scale=m_pool_scale, m_w_br_a=m_w_br_a, m_w_br_b=m_w_br_b, m_w_br_c=m_w_br_c, m_w_out=m_w_out, m_w_mlp1=m_w_mlp1, m_w_mlp2=m_w_mlp2, v_c_ctx=v_c_ctx, v_w_ada=v_w_ada, v_b_ada=v_b_ada, v_norm1=v_norm1, v_norm2=v_norm2, v_w_in=v_w_in, v_q_norm_a=v_q_norm_a, v_k_norm_a=v_k_norm_a, v_q_norm_c=v_q_norm_c, v_k_norm_c=v_k_norm_c, v_sink_c=v_sink_c, v_w_pool=v_w_pool, v_pool_scale=v_pool_scale, v_w_br_a=v_w_br_a, v_w_br_b=v_w_br_b, v_w_br_c=v_w_br_c, v_w_out=v_w_out, v_w_mlp1=v_w_mlp1, v_w_mlp2=v_w_mlp2)
    weights = {n: given[n] for n in TWIN_WEIGHTS}
    shared = {n: given[n] for n in SHARED_INPUTS}
    per_example = {n: given[n] for n in ['x', 'c', 'ctx']}
    grad_fn = _jax.value_and_grad(_loss, argnums=(0, 1))

    def one_microbatch(ex, loss_target):
        ex = dict(ex)
        diff = ex.pop(TWIN_DIFF_INPUT)
        return grad_fn(weights, diff, {**shared, **ex}, loss_target)

    if N_MICROBATCH == 1:
        loss, (grad_w, grad_x) = one_microbatch(per_example, given["loss_target"])
    else:
        def body(carry, xs):
            loss_sum, grad_sum = carry
            l_k, (gw_k, gx_k) = one_microbatch(xs[0], xs[1])
            with _jax.named_scope("update"):
                return (loss_sum + l_k, _jax.tree.map(_jnp.add, grad_sum, gw_k)), gx_k

        init = (_jnp.zeros((), _jnp.float32), _jax.tree.map(_jnp.zeros_like, weights))
        (loss, grad_w), grad_x = _jax.lax.scan(body, init, (per_example, given["loss_target"]))
    with _jax.named_scope("update"):
        delta_w, new_m, new_v = {}, {}, {}
        for n in TWIN_WEIGHTS:
            delta_w[n], new_m[n], new_v[n] = _adamw(weights[n], grad_w[n], given["m_" + n], given["v_" + n])
    return (loss, grad_x, *[grad_w[n] for n in TWIN_WEIGHTS], *[delta_w[n] for n in TWIN_WEIGHTS],
            *[new_m[n] for n in TWIN_WEIGHTS], *[new_v[n] for n in TWIN_WEIGHTS])
```

```python
import functools

import jax
import jax.numpy as jnp
from jax import lax
from jax.experimental import pallas as pl
from jax.experimental.pallas import tpu as pltpu

F32 = jnp.float32
BF16 = jnp.bfloat16

HEAD_DIM = 64
GRID_W = 64
AXIS_DIM = HEAD_DIM // 2
ROPE_THETA = 10000.0
N_HEADS = 6
N_KV = 2
N_GROUP = N_HEADS // N_KV
POOL_CH = 64
POOL_WIDTH = 256
POOL_WINDOWS = (2, 4, 8, 16)
WINDOW = 128
Q_BLOCK = 128
Q_WIDTH = N_HEADS * HEAD_DIM
KV_WIDTH = N_KV * HEAD_DIM
GATE_COL = 2 * (Q_WIDTH + 2 * KV_WIDTH) + POOL_WIDTH
U_COL = 2 * (Q_WIDTH + 2 * KV_WIDTH)
EPS = 1e-6
NEG = -1e30
ADAM_LR = 0.001
ADAM_B1 = 0.9
ADAM_B2 = 0.999
ADAM_EPS = 1e-08
ADAM_WD = 0.01
ADAM_STEP = 10

N_CHIPS = 4
LANES = 128
POOL_PAD = 16
VMEM_LIMIT = 48 * 1024 * 1024
MESH = pl.DeviceIdType.MESH
ANY = pl.BlockSpec(memory_space=pl.ANY)


def _params(sem):
    return pltpu.CompilerParams(dimension_semantics=sem, vmem_limit_bytes=VMEM_LIMIT)


def _sds(shape, dtype):
    return jax.ShapeDtypeStruct(tuple(shape), dtype)


class _Opnd:
    def __init__(self, arr, kind="plain", layer=None):
        self.arr, self.kind, self.layer = arr, kind, layer

    @property
    def shape(self):
        a = self.arr
        if self.kind == "plain":
            return a.shape
        if self.kind == "bcols":
            return (a.shape[2], N_CHIPS * a.shape[3])
        return (N_CHIPS * a.shape[2], a.shape[3])

    def spec(self, tr, tc, fn):
        a, layer = self.arr, self.layer
        if self.kind == "plain":
            return pl.BlockSpec((tr, tc), lambda *g: fn(*g))
        if self.kind == "bcols":
            assert a.shape[3] % tc == 0, (a.shape, tc)
            per = a.shape[3] // tc

            def im(*g):
                ri, ci = fn(*g)
                return (ci // per, layer, ri, ci % per)
            return pl.BlockSpec((None, None, tr, tc), im)
        assert a.shape[2] % tr == 0, (a.shape, tr)
        per = a.shape[2] // tr

        def im(*g):
            ri, ci = fn(*g)
            return (ri // per, layer, ri % per, ci)
        return pl.BlockSpec((None, None, tr, tc), im)


def _matmul(a, b, mode, *, tm, tn, tk, name, out_dtypes=(F32,), epilogue=None, extras=(), out_blocked=False):
    if not isinstance(a, _Opnd):
        a = _Opnd(a)
    if not isinstance(b, _Opnd):
        b = _Opnd(b)
    if mode == "nn":
        (M, K), (K2, N) = a.shape, b.shape
        a_spec = a.spec(tm, tk, lambda m, n, k: (m, k))
        b_spec = b.spec(tk, tn, lambda m, n, k: (k, n))
        dims = (((1,), (0,)), ((), ()))
    elif mode == "nt":
        (M, K), (N, K2) = a.shape, b.shape
        a_spec = a.spec(tm, tk, lambda m, n, k: (m, k))
        b_spec = b.spec(tn, tk, lambda m, n, k: (n, k))
        dims = (((1,), (1,)), ((), ()))
    else:
        (K, M), (K2, N) = a.shape, b.shape
        a_spec = a.spec(tk, tm, lambda m, n, k: (k, m))
        b_spec = b.spec(tk, tn, lambda m, n, k: (k, n))
        dims = (((0,), (0,)), ((), ()))
    assert K == K2 and M % tm == 0 and N % tn == 0 and K % tk == 0, (name, M, N, K, K2, tm, tn, tk)
    nk = K // tk
    n_extra = len(extras)
    n_out = len(out_dtypes)
    extra_specs = [pl.BlockSpec(bs, functools.partial(lambda m, n, k, f: f(m, n), f=f)) for (_, bs, f) in extras]
    if out_blocked:
        assert (N // N_CHIPS) % tn == 0
        per = (N // N_CHIPS) // tn
        out_shape = [_sds((N_CHIPS, M, N // N_CHIPS), dt) for dt in out_dtypes]
        out_specs = [pl.BlockSpec((None, tm, tn), lambda m, n, k: (n // per, m, n % per)) for _ in out_dtypes]
    else:
        out_shape = [_sds((M, N), dt) for dt in out_dtypes]
        out_specs = [pl.BlockSpec((tm, tn), lambda m, n, k: (m, n)) for _ in out_dtypes]

    def body(*refs):
        a_ref, b_ref = refs[0], refs[1]
        extra_refs = refs[2:2 + n_extra]
        out_refs = refs[2 + n_extra:2 + n_extra + n_out]
        acc_ref = refs[2 + n_extra + n_out]
        k = pl.program_id(2)
        prod = lax.dot_general(a_ref[...].astype(BF16), b_ref[...].astype(BF16), dims, preferred_element_type=F32)

        def finish(acc):
            outs = epilogue(acc, *[r[...] for r in extra_refs]) if epilogue is not None else (acc,)
            for o_ref, o in zip(out_refs, outs):
                o_ref[...] = o.astype(o_ref.dtype)

        if nk == 1:
            finish(prod)
        else:
            @pl.when(k == 0)
            def _():
                acc_ref[...] = prod

            @pl.when(k > 0)
            def _():
                acc_ref[...] += prod

            @pl.when(k == nk - 1)
            def _():
                finish(acc_ref[...])

    outs = pl.pallas_call(
        body, name=name, grid=(M // tm, N // tn, nk),
        in_specs=[a_spec, b_spec] + extra_specs, out_specs=out_specs, out_shape=out_shape,
        scratch_shapes=[pltpu.VMEM((tm, tn), F32)],
        compiler_params=_params(("parallel", "parallel", "arbitrary")),
    )(a.arr, b.arr, *[e[0] for e in extras])
    return outs[0] if n_out == 1 else outs


def _tile(n, cands):
    for t in cands:
        if n % t == 0:
            return t
    return n


def _grp(i, P):
    return 2 * (i // P) + jnp.minimum(i % P, 1)


def _mod_spec(D, P, part):
    return pl.BlockSpec((1, 1, D), lambda i: (_grp(i, P), 0, part))


def _res_norm(x, pending, modtab, shift_part, scale_part, gain, *, TR, P, name):
    T, D = x.shape
    row = pl.BlockSpec((TR, D), lambda i: (i, 0))
    has_branch = pending is not None
    ins, specs = [x], [row]
    if has_branch:
        branch, gate_tab, gate_part = pending
        ins += [branch, gate_tab]
        specs += [row, _mod_spec(D, P, gate_part)]
    ins += [modtab, modtab, gain]
    specs += [_mod_spec(D, P, shift_part), _mod_spec(D, P, scale_part), pl.BlockSpec((1, D), lambda i: (0, 0))]

    def body(*refs):
        if has_branch:
            x_ref, br_ref, g_ref, sh_ref, sc_ref, gn_ref, xo_ref, h_ref = refs
            xv = x_ref[...] + g_ref[0] * br_ref[...]
        else:
            x_ref, sh_ref, sc_ref, gn_ref, xo_ref, h_ref = refs
            xv = x_ref[...]
        xo_ref[...] = xv
        y = xv * lax.rsqrt(jnp.mean(xv * xv, axis=-1, keepdims=True) + EPS) * gn_ref[...]
        h_ref[...] = (y * (1.0 + sc_ref[0]) + sh_ref[0]).astype(BF16)

    return pl.pallas_call(
        body, name=name, grid=(T // TR,), in_specs=specs, out_specs=[row, row],
        out_shape=[_sds((T, D), F32), _sds((T, D), BF16)], compiler_params=_params(("parallel",)),
    )(*ins)


def _gate_bwd(dx, branch, modtab, gate_part, *, TR, P, name):
    T, D = dx.shape
    G = modtab.shape[0]
    row = pl.BlockSpec((TR, D), lambda i: (i, 0))
    acc = pl.BlockSpec((1, 1, D), lambda i: (_grp(i, P), 0, 0))

    def body(dx_ref, br_ref, g_ref, db_ref, dg_ref):
        r = pl.program_id(0) % P
        dxv = dx_ref[...]
        db_ref[...] = (dxv * g_ref[0]).astype(BF16)
        part = jnp.sum(dxv * br_ref[...], axis=0, keepdims=True)

        @pl.when(r <= 1)
        def _():
            dg_ref[0] = part

        @pl.when(r > 1)
        def _():
            dg_ref[0] += part

    return pl.pallas_call(
        body, name=name, grid=(T // TR,), in_specs=[row, row, _mod_spec(D, P, gate_part)], out_specs=[row, acc],
        out_shape=[_sds((T, D), BF16), _sds((G, 1, D), F32)], compiler_params=_params(("arbitrary",)),
    )(dx, branch, modtab)


def _norm_bwd(x, dh, dres, modtab, scale_part, gain, *, TR, P, name):
    T, D = x.shape
    G = modtab.shape[0]
    row = pl.BlockSpec((TR, D), lambda i: (i, 0))
    acc = pl.BlockSpec((1, 1, D), lambda i: (_grp(i, P), 0, 0))

    def body(x_ref, dh_ref, dres_ref, sc_ref, gn_ref, dx_ref, dsh_ref, dsc_ref, dgn_ref):
        r = pl.program_id(0) % P
        xv, dhv, gn = x_ref[...], dh_ref[...], gn_ref[...]
        rstd = lax.rsqrt(jnp.mean(xv * xv, axis=-1, keepdims=True) + EPS)
        xhat = xv * rstd
        dn = dhv * (1.0 + sc_ref[0])
        dxhat = dn * gn
        dx_ref[...] = dres_ref[...] + rstd * (dxhat - xhat * jnp.mean(dxhat * xhat, axis=-1, keepdims=True))
        p_sh = jnp.sum(dhv, axis=0, keepdims=True)
        p_sc = jnp.sum(dhv * (xhat * gn), axis=0, keepdims=True)
        p_gn = jnp.sum(dn * xhat, axis=0, keepdims=True)

        @pl.when(r <= 1)
        def _():
            dsh_ref[0] = p_sh
            dsc_ref[0] = p_sc
            dgn_ref[0] = p_gn

        @pl.when(r > 1)
        def _():
            dsh_ref[0] += p_sh
            dsc_ref[0] += p_sc
            dgn_ref[0] += p_gn

    return pl.pallas_call(
        body, name=name, grid=(T // TR,),
        in_specs=[row, row, row, _mod_spec(D, P, scale_part), pl.BlockSpec((1, D), lambda i: (0, 0))],
        out_specs=[row, acc, acc, acc],
        out_shape=[_sds((T, D), F32)] + [_sds((G, 1, D), F32)] * 3, compiler_params=_params(("arbitrary",)),
    )(x, dh, dres, modtab, gain)


def _loss_head(x, branch, modtab, gate_part, target, *, TR, P, name):
    T, D = x.shape
    row = pl.BlockSpec((TR, D), lambda i: (i, 0))
    tgt = pl.BlockSpec((TR, D), lambda i: ((i // P) * (P - 1) + jnp.maximum(i % P - 1, 0), 0))
    one = pl.BlockSpec((1, LANES), lambda i: (0, 0))

    def body(x_ref, br_ref, g_ref, t_ref, dy_ref, loss_ref):
        i = pl.program_id(0)
        r = i % P

        @pl.when(i == 0)
        def _():
            loss_ref[...] = jnp.zeros_like(loss_ref)

        @pl.when(r == 0)
        def _():
            dy_ref[...] = jnp.zeros_like(dy_ref)

        @pl.when(r > 0)
        def _():
            err = x_ref[...] + g_ref[0] * br_ref[...] - t_ref[...]
            dy_ref[...] = err / D
            per_tok = jnp.mean(err * err, axis=-1, keepdims=True)
            loss_ref[...] += 0.5 * jnp.sum(per_tok, axis=0, keepdims=True)

    return pl.pallas_call(
        body, name=name, grid=(T // TR,), in_specs=[row, row, _mod_spec(D, P, gate_part), tgt], out_specs=[row, one],
        out_shape=[_sds((T, D), F32), _sds((1, LANES), F32)], compiler_params=_params(("arbitrary",)),
    )(x, branch, modtab, target)


QK_CHUNKS = 8


def _qk_col(j):
    return j + jnp.where(j >= 4, 1, 0)


def _seg_mean(v):
    lane = lax.broadcasted_iota(jnp.int32, v.shape, 1)
    lo = lane < HEAD_DIM
    s0 = jnp.sum(jnp.where(lo, v, 0.0), axis=-1, keepdims=True)
    s1 = jnp.sum(jnp.where(lo, 0.0, v), axis=-1, keepdims=True)
    return jnp.where(lo, s0, s1) * (1.0 / HEAD_DIM)


def _pair_swap(v):
    lane = lax.broadcasted_iota(jnp.int32, v.shape, 1)
    return jnp.where((lane & 1) == 0, pltpu.roll(v, LANES - 1, 1), pltpu.roll(v, 1, 1))


def _qk_prep(z, gains, cos, sin, *, TR, P, name):
    T = z.shape[0]

    def body(z_ref, g_ref, c_ref, s_ref, o_ref):
        xv = z_ref[...]
        y = xv * lax.rsqrt(_seg_mean(xv * xv) + EPS) * g_ref[0]
        o_ref[...] = (y * c_ref[...] + _pair_swap(y) * s_ref[...]).astype(BF16)

    return pl.pallas_call(
        body, name=name, grid=(T // TR, QK_CHUNKS),
        in_specs=[pl.BlockSpec((TR, LANES), lambda i, j: (i, _qk_col(j))),
                  pl.BlockSpec((1, 1, LANES), lambda i, j: (j, 0, 0)),
                  pl.BlockSpec((TR, LANES), lambda i, j: (i % P, 0)),
                  pl.BlockSpec((TR, LANES), lambda i, j: (i % P, 0))],
        out_specs=pl.BlockSpec((TR, LANES), lambda i, j: (i, j)),
        out_shape=_sds((T, QK_CHUNKS * LANES), BF16), compiler_params=_params(("parallel", "parallel")),
    )(z, gains, cos, sin)


def _qk_prep_bwd(z, dqk, gains, cos, sin, *, TR, P, name):
    T = z.shape[0]
    nt = T // TR

    def body(z_ref, d_ref, g_ref, c_ref, s_ref, dz_ref, dg_ref):
        i = pl.program_id(1)
        xv, dout, g = z_ref[...], d_ref[...], g_ref[0]
        dy = dout * c_ref[...] + _pair_swap(dout * s_ref[...])
        rstd = lax.rsqrt(_seg_mean(xv * xv) + EPS)
        xhat = xv * rstd
        dxhat = dy * g
        dz_ref[...] = (rstd * (dxhat - xhat * _seg_mean(dxhat * xhat))).astype(BF16)
        part = jnp.sum(dy * xhat, axis=0, keepdims=True)

        @pl.when(i == 0)
        def _():
            dg_ref[0] = part

        @pl.when(i > 0)
        def _():
            dg_ref[0] += part

    return pl.pallas_call(
        body, name=name, grid=(QK_CHUNKS, nt),
        in_specs=[pl.BlockSpec((TR, LANES), lambda j, i: (i, _qk_col(j))),
                  pl.BlockSpec((TR, LANES), lambda j, i: (i, j)),
                  pl.BlockSpec((1, 1, LANES), lambda j, i: (j, 0, 0)),
                  pl.BlockSpec((TR, LANES), lambda j, i: (i % P, 0)),
                  pl.BlockSpec((TR, LANES), lambda j, i: (i % P, 0))],
        out_specs=[pl.BlockSpec((TR, LANES), lambda j, i: (i, j)), pl.BlockSpec((1, 1, LANES), lambda j, i: (j, 0, 0))],
        out_shape=[_sds((T, QK_CHUNKS * LANES), BF16), _sds((QK_CHUNKS, 1, LANES), F32)],
        compiler_params=_params(("parallel", "arbitrary")),
    )(z, dqk, gains, cos, sin)


NT_DIMS = (((1,), (1,)), ((), ()))
TN_DIMS = (((0,), (0,)), ((), ()))
QROWS = N_GROUP * Q_BLOCK


def _attn_parts(q, k_ref, i, *, n_ctx, seq, t_all, window):
    scale = HEAD_DIM ** -0.5

    def scores(start, size):
        kk = k_ref[0, 0, pl.ds(start, size), :]
        return lax.dot_general(q, kk, NT_DIMS, preferred_element_type=F32) * scale

    def ctx_case():
        return [(0, n_ctx, scores(0, n_ctx))]

    def latent_case():
        if not window:
            return [(0, t_all, scores(0, t_all))]
        band = Q_BLOCK + 2 * WINDOW
        start = pl.multiple_of((i - 1) * Q_BLOCK, Q_BLOCK)
        s_loc = scores(start, band)
        rr = lax.broadcasted_iota(jnp.int32, (QROWS, band), 0) & (Q_BLOCK - 1)
        jj = lax.broadcasted_iota(jnp.int32, (QROWS, band), 1)
        kpos = (i - n_ctx // Q_BLOCK - 1) * Q_BLOCK + jj
        valid = (jj - rr >= 0) & (jj - rr <= 2 * WINDOW) & (kpos >= 0) & (kpos < seq)
        return [(0, n_ctx, scores(0, n_ctx)), (start, band, jnp.where(valid, s_loc, NEG))]

    return ctx_case, latent_case


def _softmax_parts(parts, sink_col):
    m = functools.reduce(jnp.maximum, [jnp.max(s, axis=-1, keepdims=True) for (_, _, s) in parts])
    if sink_col is not None:
        m = jnp.maximum(m, sink_col)
    es = [jnp.exp(s - m) for (_, _, s) in parts]
    l = functools.reduce(jnp.add, [jnp.sum(e, axis=-1, keepdims=True) for e in es])
    if sink_col is not None:
        e_sink = jnp.exp(sink_col - m)
        l = l + e_sink
    inv = 1.0 / l
    return [e * inv for e in es], (e_sink * inv if sink_col is not None else None)


def _sink_column(sink_ref, j):
    r = lax.broadcasted_iota(jnp.int32, (QROWS, 1), 0)
    s0, s1, s2 = sink_ref[j * N_GROUP], sink_ref[j * N_GROUP + 1], sink_ref[j * N_GROUP + 2]
    return jnp.where(r < Q_BLOCK, s0, jnp.where(r < 2 * Q_BLOCK, s1, s2))


def _attn_specs(Tp, Tk):
    q_spec = pl.BlockSpec((1, N_GROUP, Q_BLOCK, HEAD_DIM), lambda b, j, i: (b, j, i, 0))
    kv_spec = pl.BlockSpec((1, 1, Tk, HEAD_DIM), lambda b, j, i: (b, j, 0, 0))
    return q_spec, kv_spec


def _attn_fwd(q, k, v, sink, *, n_ctx, window, name):
    B, _, Tp, _ = q.shape
    Tk = k.shape[2]
    seq = Tp - n_ctx
    has_sink = sink is not None
    q_spec, kv_spec = _attn_specs(Tp, Tk)

    def body(*refs):
        if has_sink:
            sink_ref, q_ref, k_ref, v_ref, o_ref = refs
        else:
            q_ref, k_ref, v_ref, o_ref = refs
        j, i = pl.program_id(1), pl.program_id(2)
        qv = q_ref[0].reshape(QROWS, HEAD_DIM)
        sink_col = _sink_column(sink_ref, j) if has_sink else None
        ctx_case, latent_case = _attn_parts(qv, k_ref, i, n_ctx=n_ctx, seq=seq, t_all=Tp, window=window)

        def run(case):
            parts = case()
            probs, _ = _softmax_parts(parts, sink_col)
            o = None
            for (start, size, _), p in zip(parts, probs):
                t = jnp.dot(p.astype(BF16), v_ref[0, 0, pl.ds(start, size), :], preferred_element_type=F32)
                o = t if o is None else o + t
            o_ref[0] = o.reshape(N_GROUP, Q_BLOCK, HEAD_DIM).astype(BF16)

        @pl.when(i < n_ctx // Q_BLOCK)
        def _():
            run(ctx_case)

        @pl.when(i >= n_ctx // Q_BLOCK)
        def _():
            run(latent_case)

    ins, specs = [q, k, v], [q_spec, kv_spec, kv_spec]
    if has_sink:
        ins, specs = [sink] + ins, [pl.BlockSpec(memory_space=pltpu.SMEM)] + specs
    return pl.pallas_call(
        body, name=name, grid=(B, N_KV, Tp // Q_BLOCK), in_specs=specs, out_specs=q_spec,
        out_shape=_sds(q.shape, BF16), compiler_params=_params(("parallel", "parallel", "parallel")),
    )(*ins)


def _attn_bwd(q, k, v, do, sink, *, n_ctx, window, name):
    B, _, Tp, _ = q.shape
    Tk = k.shape[2]
    seq = Tp - n_ctx
    has_sink = sink is not None
    q_spec, kv_spec = _attn_specs(Tp, Tk)
    sink_spec = pl.BlockSpec((1, 1, 8, LANES), lambda b, j, i: (b, j, 0, 0))
    scale = HEAD_DIM ** -0.5

    def body(*refs):
        if has_sink:
            sink_ref, q_ref, k_ref, v_ref, do_ref, dq_ref, dk_ref, dv_ref, ds_ref = refs
        else:
            q_ref, k_ref, v_ref, do_ref, dq_ref, dk_ref, dv_ref = refs
        j, i = pl.program_id(1), pl.program_id(2)

        @pl.when(i == 0)
        def _():
            dk_ref[...] = jnp.zeros_like(dk_ref)
            dv_ref[...] = jnp.zeros_like(dv_ref)
            if has_sink:
                ds_ref[...] = jnp.zeros_like(ds_ref)

        qv = q_ref[0].reshape(QROWS, HEAD_DIM)
        dov = do_ref[0].reshape(QROWS, HEAD_DIM)
        sink_col = _sink_column(sink_ref, j) if has_sink else None
        ctx_case, latent_case = _attn_parts(qv, k_ref, i, n_ctx=n_ctx, seq=seq, t_all=Tp, window=window)

        def run(case):
            parts = case()
            probs, p_sink = _softmax_parts(parts, sink_col)
            dps = [lax.dot_general(dov, v_ref[0, 0, pl.ds(start, size), :], NT_DIMS, preferred_element_type=F32)
                   for (start, size, _) in parts]
            delta = functools.reduce(jnp.add, [jnp.sum(p * dp, axis=-1, keepdims=True) for p, dp in zip(probs, dps)])
            dq = None
            for (start, size, _), p, dp in zip(parts, probs, dps):
                ds = (p * (dp - delta) * scale).astype(BF16)
                rows = pl.ds(start, size)
                t = jnp.dot(ds, k_ref[0, 0, rows, :], preferred_element_type=F32)
                dq = t if dq is None else dq + t
                dk_ref[0, 0, rows, :] += lax.dot_general(ds, qv, TN_DIMS, preferred_element_type=F32)
                dv_ref[0, 0, rows, :] += lax.dot_general(p.astype(BF16), dov, TN_DIMS, preferred_element_type=F32)
            dq_ref[0] = dq.reshape(N_GROUP, Q_BLOCK, HEAD_DIM)
            if has_sink:
                contrib = -(p_sink * delta)
                r = lax.broadcasted_iota(jnp.int32, (QROWS, 1), 0)
                row8 = lax.broadcasted_iota(jnp.int32, (8, LANES), 0)
                upd = jnp.zeros((8, LANES), F32)
                for h in range(N_GROUP):
                    in_head = (r >= h * Q_BLOCK) & (r < (h + 1) * Q_BLOCK)
                    tot = jnp.sum(jnp.where(in_head, contrib, 0.0), axis=0, keepdims=True)
                    upd = upd + jnp.where(row8 == h, tot, 0.0)
                ds_ref[0, 0] += upd

        @pl.when(i < n_ctx // Q_BLOCK)
        def _():
            run(ctx_case)

        @pl.when(i >= n_ctx // Q_BLOCK)
        def _():
            run(latent_case)

    ins, specs = [q, k, v, do], [q_spec, kv_spec, kv_spec, q_spec]
    out_specs = [q_spec, kv_spec, kv_spec]
    out_shape = [_sds(q.shape, F32), _sds(k.shape, F32), _sds(k.shape, F32)]
    if has_sink:
        ins, specs = [sink] + ins, [pl.BlockSpec(memory_space=pltpu.SMEM)] + specs
        out_specs.append(sink_spec)
        out_shape.append(_sds((B, N_KV, 8, LANES), F32))
    return pl.pallas_call(
        body, name=name, grid=(B, N_KV, Tp // Q_BLOCK), in_specs=specs, out_specs=out_specs, out_shape=out_shape,
        compiler_params=_params(("parallel", "parallel", "arbitrary")),
    )(*ins)


def _window_sums(xp):
    n = xp.shape[0]

    def ahead(a, k):
        return pltpu.roll(a, n - k, 0)
    a2 = xp + ahead(xp, 1)
    a4 = a2 + ahead(a2, 2)
    a8 = a4 + ahead(a4, 4)
    a16 = a8 + ahead(a8, 8)
    return (a2, a4, a8, a16)


def _by_group(vals):
    lane = lax.broadcasted_iota(jnp.int32, vals[0].shape, 1)
    return jnp.where(lane < POOL_CH, vals[0], jnp.where(lane < 2 * POOL_CH, vals[1],
                     jnp.where(lane < 3 * POOL_CH, vals[2], vals[3])))


def _pool_counts(n):
    t = lax.broadcasted_iota(jnp.int32, (n, POOL_WIDTH), 0)
    cnts = [(jnp.minimum(t + w // 2, n) - jnp.maximum(t - w // 2, 0)).astype(F32) for w in POOL_WINDOWS]
    return _by_group(cnts)


def _pad_rows(x):
    zeros = jnp.zeros((POOL_PAD, x.shape[1]), x.dtype)
    return jnp.concatenate([zeros, x, zeros], axis=0)


def _pool_stream(u):
    n = u.shape[0]
    sums = _window_sums(_pad_rows(u))
    tots = [pltpu.roll(a, w // 2, 0)[POOL_PAD:POOL_PAD + n] for a, w in zip(sums, POOL_WINDOWS)]
    return _by_group(tots) / _pool_counts(n) - u


def _pool_stream_t(dp):
    n = dp.shape[0]
    sums = _window_sums(_pad_rows(dp / _pool_counts(n)))
    tots = [pltpu.roll(a, w // 2 - 1, 0)[POOL_PAD:POOL_PAD + n] if w > 2 else a[POOL_PAD:POOL_PAD + n]
            for a, w in zip(sums, POOL_WINDOWS)]
    return _by_group(tots) - dp


def _pool_fwd(z, w_bd, scale, *, B, Tp, n_ctx, name):
    T = z.shape[0]
    blk = pl.BlockSpec((Tp, POOL_WIDTH), lambda b: (b, U_COL // POOL_WIDTH))
    out = pl.BlockSpec((Tp, POOL_WIDTH), lambda b: (b, 0))

    def body(u_ref, w_ref, s_ref, p_ref, o_ref):
        for lo, hi in ((0, n_ctx), (n_ctx, Tp)):
            pooled = _pool_stream(u_ref[lo:hi, :]).astype(BF16)
            p_ref[lo:hi, :] = pooled
            mixed = jnp.dot(pooled, w_ref[...], preferred_element_type=F32)
            o_ref[lo:hi, :] = (mixed * s_ref[...]).astype(BF16)

    return pl.pallas_call(
        body, name=name, grid=(B,),
        in_specs=[blk, pl.BlockSpec((POOL_WIDTH, POOL_WIDTH), lambda b: (0, 0)), pl.BlockSpec((1, POOL_WIDTH), lambda b: (0, 0))],
        out_specs=[out, out], out_shape=[_sds((T, POOL_WIDTH), BF16)] * 2, compiler_params=_params(("parallel",)),
    )(z, w_bd, scale)


def _pool_bwd(d_ob, pooled, w_bd, scale, *, B, Tp, n_ctx, name):
    T = d_ob.shape[0]
    blk = pl.BlockSpec((Tp, POOL_WIDTH), lambda b: (b, 0))
    wsp = pl.BlockSpec((POOL_WIDTH, POOL_WIDTH), lambda b: (0, 0))
    ssp = pl.BlockSpec((1, POOL_WIDTH), lambda b: (0, 0))

    def body(d_ref, p_ref, w_ref, s_ref, du_ref, dw_ref, dsc_ref):
        @pl.when(pl.program_id(0) == 0)
        def _():
            dw_ref[...] = jnp.zeros_like(dw_ref)
            dsc_ref[...] = jnp.zeros_like(dsc_ref)

        dv, pv, wv = d_ref[...], p_ref[...], w_ref[...]
        mixed = jnp.dot(pv, wv, preferred_element_type=F32)
        dsc_ref[...] += jnp.sum(dv * mixed, axis=0, keepdims=True)
        dmixed = (dv * s_ref[...]).astype(BF16)
        dw_ref[...] += lax.dot_general(pv, dmixed, TN_DIMS, preferred_element_type=F32)
        dpooled = lax.dot_general(dmixed, wv, NT_DIMS, preferred_element_type=F32)
        for lo, hi in ((0, n_ctx), (n_ctx, Tp)):
            du_ref[lo:hi, :] = _pool_stream_t(dpooled[lo:hi, :]).astype(BF16)

    return pl.pallas_call(
        body, name=name, grid=(B,), in_specs=[blk, blk, wsp, ssp], out_specs=[blk, wsp, ssp],
        out_shape=[_sds((T, POOL_WIDTH), BF16), _sds((POOL_WIDTH, POOL_WIDTH), F32), _sds((1, POOL_WIDTH), F32)],
        compiler_params=_params(("arbitrary",)),
    )(d_ob, pooled, w_bd, scale)


def _merge_specs(z, D, TR, tc, wa, wb, wc):
    def act(width):
        return pl.BlockSpec((TR, width), lambda i, n: (i, 0))

    def gate(part):
        return pl.BlockSpec((TR, tc), lambda i, n: (i, (GATE_COL + part * D) // tc + n))
    w_specs = [w.spec(w.shape[0], tc, lambda i, n: (0, n)) for w in (wa, wb, wc)]
    return [act(Q_WIDTH), act(POOL_WIDTH), act(Q_WIDTH), gate(0), gate(1), gate(2)] + w_specs


def _merge_fwd(oa, ob, oc, z, wa, wb, wc, *, D, TR, name):
    T = oa.shape[0]
    tc = D // N_CHIPS

    def body(oa_ref, ob_ref, oc_ref, ga_ref, gb_ref, gc_ref, wa_ref, wb_ref, wc_ref, y_ref):
        acc = jax.nn.sigmoid(ga_ref[...]) * jnp.dot(oa_ref[...], wa_ref[...], preferred_element_type=F32)
        acc += jax.nn.sigmoid(gb_ref[...]) * jnp.dot(ob_ref[...], wb_ref[...], preferred_element_type=F32)
        acc += jax.nn.sigmoid(gc_ref[...]) * jnp.dot(oc_ref[...], wc_ref[...], preferred_element_type=F32)
        y_ref[...] = acc.astype(BF16)

    return pl.pallas_call(
        body, name=name, grid=(T // TR, D // tc), in_specs=_merge_specs(z, D, TR, tc, wa, wb, wc),
        out_specs=pl.BlockSpec((TR, tc), lambda i, n: (i, n)), out_shape=_sds((T, D), BF16),
        compiler_params=_params(("parallel", "parallel")),
    )(oa, ob, oc, z, z, z, wa.arr, wb.arr, wc.arr)


def _merge_bwd(dy, oa, ob, oc, z, wa, wb, wc, *, D, TR, name):
    T = oa.shape[0]
    tc = D // N_CHIPS
    out = pl.BlockSpec((TR, tc), lambda i, n: (i, n))

    def body(dy_ref, oa_ref, ob_ref, oc_ref, ga_ref, gb_ref, gc_ref, wa_ref, wb_ref, wc_ref,
             dpa_ref, dpb_ref, dpc_ref, dga_ref, dgb_ref, dgc_ref):
        dyv = dy_ref[...]
        for o_ref, g_ref, w_ref, dp_ref, dg_ref in ((oa_ref, ga_ref, wa_ref, dpa_ref, dga_ref),
                                                    (ob_ref, gb_ref, wb_ref, dpb_ref, dgb_ref),
                                                    (oc_ref, gc_ref, wc_ref, dpc_ref, dgc_ref)):
            s = jax.nn.sigmoid(g_ref[...])
            proj = jnp.dot(o_ref[...], w_ref[...], preferred_element_type=F32)
            dp_ref[...] = (dyv * s).astype(BF16)
            dg_ref[...] = (dyv * proj * (s * (1.0 - s))).astype(BF16)

    return pl.pallas_call(
        body, name=name, grid=(T // TR, D // tc), in_specs=[out] + _merge_specs(z, D, TR, tc, wa, wb, wc),
        out_specs=[out] * 6, out_shape=[_sds((T, D), BF16)] * 6, compiler_params=_params(("parallel", "parallel")),
    )(dy, oa, ob, oc, z, z, z, wa.arr, wb.arr, wc.arr)


def _silu_rows(cc, name):
    def body(c_ref, s_ref):
        v = c_ref[...]
        s_ref[...] = (v * jax.nn.sigmoid(v)).astype(BF16)
    return pl.pallas_call(body, name=name, out_shape=_sds(cc.shape, BF16))(cc)


def _ada_bwd_rows(dm, ds, cc, name):
    def body(dm_ref, ds_ref, c_ref, db_ref, dc_ref):
        db_ref[...] = jnp.sum(dm_ref[...], axis=0, keepdims=True)
        v = c_ref[...]
        s = jax.nn.sigmoid(v)
        dc_ref[...] = ds_ref[...] * (s * (1.0 + v * (1.0 - s)))
    return pl.pallas_call(body, name=name, out_shape=[_sds((1, dm.shape[1]), F32), _sds(cc.shape, F32)])(dm, ds, cc)


def _row_tile(rows, cols):
    for t in (512, 256, 128, 64, 32, 16, 8):
        if rows % t == 0 and t * cols * 4 <= (1 << 20):
            return t
    return rows


def _add_own_layer(stacked, landed, core, name):
    _, R, C = stacked.shape
    tr = _row_tile(R, C)

    def body(c_ref, a_ref, b_ref, o_ref):
        o_ref[...] = a_ref[...] + b_ref[...]

    grid_spec = pltpu.PrefetchScalarGridSpec(
        num_scalar_prefetch=1, grid=(R // tr,),
        in_specs=[pl.BlockSpec((None, tr, C), lambda i, c: (c[0], i, 0)), pl.BlockSpec((tr, C), lambda i, c: (i, 0))],
        out_specs=pl.BlockSpec((tr, C), lambda i, c: (i, 0)))
    return pl.pallas_call(body, name=name, grid_spec=grid_spec, out_shape=_sds((R, C), F32),
                          compiler_params=_params(("parallel",)))(core, stacked, landed)


def _sum_chips(own, landed, chip, name):
    _, R, C = own.shape
    tr = _row_tile(R, C)

    def body(k_ref, a_ref, b_ref, o_ref):
        o_ref[...] = ((a_ref[...] + b_ref[0]) + b_ref[1]) + b_ref[2]

    grid_spec = pltpu.PrefetchScalarGridSpec(
        num_scalar_prefetch=1, grid=(R // tr,),
        in_specs=[pl.BlockSpec((None, tr, C), lambda i, k: (k[0], i, 0)), pl.BlockSpec((3, tr, C), lambda i, k: (0, i, 0))],
        out_specs=pl.BlockSpec((tr, C), lambda i, k: (i, 0)))
    return pl.pallas_call(body, name=name, grid_spec=grid_spec, out_shape=_sds((R, C), F32),
                          compiler_params=_params(("parallel",)))(chip, own, landed)


def _adam_math(w, g, m, v):
    m = ADAM_B1 * m + (1.0 - ADAM_B1) * g
    v = ADAM_B2 * v + (1.0 - ADAM_B2) * (g * g)
    m_hat = m / (1.0 - ADAM_B1 ** ADAM_STEP)
    v_hat = v / (1.0 - ADAM_B2 ** ADAM_STEP)
    delta = -ADAM_LR * (m_hat / (jnp.sqrt(v_hat) + ADAM_EPS) + ADAM_WD * w)
    return delta, m, v


def _adamw(w, g, m, v, name):
    R, C = w.shape
    tr = _row_tile(R, C)
    row = pl.BlockSpec((tr, C), lambda i: (i, 0))

    def body(w_ref, g_ref, m_ref, v_ref, d_ref, mo_ref, vo_ref):
        d, mn, vn = _adam_math(w_ref[...], g_ref[...], m_ref[...], v_ref[...])
        d_ref[...] = d
        mo_ref[...] = mn
        vo_ref[...] = vn

    return pl.pallas_call(body, name=name, grid=(R // tr,), in_specs=[row] * 4, out_specs=[row] * 3,
                          out_shape=[_sds((R, C), F32)] * 3, compiler_params=_params(("parallel",)))(w, g, m, v)


def _adamw_small(w, parts, m, v, name):
    R, C = w.shape

    def body(w_ref, p_ref, m_ref, v_ref, g_ref, d_ref, mo_ref, vo_ref):
        g = p_ref[0]
        for dev in range(1, 8):
            g = g + p_ref[dev]
        d, mn, vn = _adam_math(w_ref[...], g, m_ref[...], v_ref[...])
        g_ref[...] = g
        d_ref[...] = d
        mo_ref[...] = mn
        vo_ref[...] = vn

    return pl.pallas_call(body, name=name, out_shape=[_sds((R, C), F32)] * 4)(w, parts, m, v)


def _place():
    return lax.axis_index("x"), lax.axis_index("y"), lax.axis_index("c")


def _other_chips(x, y):
    return [(1 - x, y), (x, 1 - y), (1 - x, 1 - y)]


def _rcopy(src, dst, ssem, rsem, dev):
    return pltpu.make_async_remote_copy(src_ref=src, dst_ref=dst, send_sem=ssem, recv_sem=rsem,
                                        device_id=dev, device_id_type=MESH)


def _gather_weights(shards, name):
    n = len(shards)

    def body(*refs):
        src, out = refs[:n], refs[n:2 * n]
        send_sems, recv_sems, local_sems = refs[2 * n:]
        x, y, c = _place()
        sibling = (x, y, 1 - c)
        chips = _other_chips(x, y)
        mine = 2 * x + y
        local = [pltpu.make_async_copy(src[w], out[w].at[mine], local_sems.at[w]) for w in range(n)]
        for cp in local:
            cp.start()
        first = [_rcopy(src[w].at[c], out[w].at[mine, c], send_sems.at[w, j], recv_sems.at[w, j], (*chip, c))
                 for w in range(n) for j, chip in enumerate(chips)]
        for cp in first:
            cp.start()
        passed = []
        for w in range(n):
            for j, (px, py) in enumerate(chips):
                landed = out[w].at[2 * px + py, c]
                _rcopy(landed, landed, send_sems.at[w, j], recv_sems.at[w, j], (px, py, c)).wait_recv()
                cp = _rcopy(landed, landed, send_sems.at[w, 3 + j], recv_sems.at[w, 3 + j], sibling)
                cp.start()
                passed.append(cp)
        for w in range(n):
            for j, (px, py) in enumerate(chips):
                landed = out[w].at[2 * px + py, 1 - c]
                _rcopy(landed, landed, send_sems.at[w, 3 + j], recv_sems.at[w, 3 + j], sibling).wait_recv()
        for cp in first + passed:
            cp.wait_send()
        for cp in local:
            cp.wait()

    return pl.pallas_call(
        body, name=name, in_specs=[ANY] * n, out_specs=[ANY] * n,
        out_shape=[_sds((N_CHIPS,) + s.shape, s.dtype) for s in shards],
        scratch_shapes=[pltpu.SemaphoreType.DMA((n, 6)), pltpu.SemaphoreType.DMA((n, 6)), pltpu.SemaphoreType.DMA((n,))],
    )(*shards)


def _send_other_layer(stacked, name):
    n = len(stacked)

    def body(*refs):
        src, out = refs[:n], refs[n:2 * n]
        send_sems, recv_sems = refs[2 * n:]
        x, y, c = _place()
        cps = [_rcopy(src[w].at[1 - c], out[w], send_sems.at[w], recv_sems.at[w], (x, y, 1 - c)) for w in range(n)]
        for cp in cps:
            cp.start()
        for cp in cps:
            cp.wait_recv()
        for cp in cps:
            cp.wait_send()

    return pl.pallas_call(
        body, name=name, in_specs=[ANY] * n, out_specs=[ANY] * n,
        out_shape=[_sds(s.shape[1:], s.dtype) for s in stacked],
        scratch_shapes=[pltpu.SemaphoreType.DMA((n,)), pltpu.SemaphoreType.DMA((n,))],
    )(*stacked)


def _send_chip_blocks(blocked, name):
    n = len(blocked)

    def body(*refs):
        src, out = refs[:n], refs[n:2 * n]
        send_sems, recv_sems = refs[2 * n:]
        x, y, c = _place()
        cps = [_rcopy(src[w].at[2 * px + py], out[w].at[j], send_sems.at[w, j], recv_sems.at[w, j], (px, py, c))
               for w in range(n) for j, (px, py) in enumerate(_other_chips(x, y))]
        for cp in cps:
            cp.start()
        for cp in cps:
            cp.wait_recv()
        for cp in cps:
            cp.wait_send()

    return pl.pallas_call(
        body, name=name, in_specs=[ANY] * n, out_specs=[ANY] * n,
        out_shape=[_sds((3,) + s.shape[1:], s.dtype) for s in blocked],
        scratch_shapes=[pltpu.SemaphoreType.DMA((n, 3)), pltpu.SemaphoreType.DMA((n, 3))],
    )(*blocked)


def _share_layers(reduced, name):
    n = len(reduced)

    def body(*refs):
        src, out = refs[:n], refs[n:2 * n]
        send_sems, recv_sems, local_sems = refs[2 * n:]
        x, y, c = _place()
        local = [pltpu.make_async_copy(src[w], out[w].at[c], local_sems.at[w]) for w in range(n)]
        for cp in local:
            cp.start()
        cps = [_rcopy(src[w], out[w].at[c], send_sems.at[w], recv_sems.at[w], (x, y, 1 - c)) for w in range(n)]
        for cp in cps:
            cp.start()
        for w in range(n):
            _rcopy(src[w], out[w].at[1 - c], send_sems.at[w], recv_sems.at[w], (x, y, 1 - c)).wait_recv()
        for cp in cps:
            cp.wait_send()
        for cp in local:
            cp.wait()

    return pl.pallas_call(
        body, name=name, in_specs=[ANY] * n, out_specs=[ANY] * n,
        out_shape=[_sds((2,) + s.shape, s.dtype) for s in reduced],
        scratch_shapes=[pltpu.SemaphoreType.DMA((n,)), pltpu.SemaphoreType.DMA((n,)), pltpu.SemaphoreType.DMA((n,))],
    )(*reduced)


def _gather_small(block, name):
    m_per, n = block.shape

    def body(x_ref, out_ref, send_sems, recv_sems, local_sem):
        x, y, c = _place()
        me, sibling = (x, y, c), (x, y, 1 - c)
        chips = _other_chips(x, y)

        def rows(px, py, pc):
            return out_ref.at[pl.ds((4 * px + 2 * py + pc) * m_per, m_per), :]

        def copy(k, blk, to, src=None):
            return _rcopy(rows(*blk) if src is None else src, rows(*blk), send_sems.at[k], recv_sems.at[k], to)

        mine = pltpu.make_async_copy(x_ref, rows(*me), local_sem)
        mine.start()
        first = [copy(0, me, sibling, src=x_ref)]
        first += [copy(1 + j, me, (*chip, c), src=x_ref) for j, chip in enumerate(chips)]
        for cp in first:
            cp.start()
        passed = [copy(4 + j, (*chip, c), sibling) for j, chip in enumerate(chips)]
        for j, chip in enumerate(chips):
            copy(1 + j, (*chip, c), me).wait_recv()
            passed[j].start()
        copy(0, sibling, me).wait_recv()
        for j, chip in enumerate(chips):
            copy(4 + j, (*chip, 1 - c), me).wait_recv()
        for cp in first + passed:
            cp.wait_send()
        mine.wait()

    return pl.pallas_call(
        body, name=name, out_shape=_sds((8 * m_per, n), block.dtype),
        in_specs=[pl.BlockSpec(memory_space=pltpu.VMEM)], out_specs=pl.BlockSpec(memory_space=pltpu.VMEM),
        scratch_shapes=[pltpu.SemaphoreType.DMA((7,)), pltpu.SemaphoreType.DMA((7,)), pltpu.SemaphoreType.DMA],
    )(block)


def _rope_tables(n_ctx, seq):
    rows = seq // GRID_W
    r = jnp.repeat(jnp.arange(rows, dtype=F32), GRID_W)
    col = jnp.tile(jnp.arange(GRID_W, dtype=F32), rows)
    inv = 1.0 / (ROPE_THETA ** (jnp.arange(0, AXIS_DIM, 2, dtype=F32) / AXIS_DIM))
    ang = jnp.concatenate([r[:, None] * inv, col[:, None] * inv], axis=-1)
    cos = jnp.repeat(jnp.cos(ang), 2, axis=-1)
    sin = jnp.repeat(jnp.sin(ang), 2, axis=-1) * jnp.tile(jnp.array([-1.0, 1.0], F32), HEAD_DIM // 2)
    cos = jnp.concatenate([jnp.ones((n_ctx, HEAD_DIM), F32), cos], axis=0)
    sin = jnp.concatenate([jnp.zeros((n_ctx, HEAD_DIM), F32), sin], axis=0)
    return jnp.tile(cos, (1, 2)), jnp.tile(sin, (1, 2))


def _to_heads(t, B, Tp):
    return t.reshape(B, Tp, t.shape[1] // HEAD_DIM, HEAD_DIM).transpose(0, 2, 1, 3)


def _from_heads(t):
    B, H, Tp, _ = t.shape
    return t.transpose(0, 2, 1, 3).reshape(B * Tp, H * HEAD_DIM)


def _pad_keys(t):
    return jnp.pad(t, ((0, 0), (0, 0), (0, Q_BLOCK), (0, 0)))


def _block_diag(w_pool_l):
    out = jnp.zeros((POOL_WIDTH, POOL_WIDTH), w_pool_l.dtype)
    for g in range(w_pool_l.shape[0]):
        out = out.at[g * POOL_CH:(g + 1) * POOL_CH, g * POOL_CH:(g + 1) * POOL_CH].set(w_pool_l[g])
    return out


def _local_step(x, c, ctx, c_ctx, small, gw, target):
    B, S, D = x.shape
    N = ctx.shape[1]
    L = small["norm1"].shape[0]
    Tp = N + S
    T = B * Tp
    TR = N
    P = Tp // N
    rows16 = 16
    assert N % Q_BLOCK == 0 and S % N == 0 and B + 1 <= rows16
    TM = _tile(T, (1024, 768, 512, 384, 256, 128))

    X = jnp.concatenate([ctx, x], axis=1).reshape(T, D)
    cc = jnp.zeros((rows16, D), F32).at[:B].set(c).at[B].set(c_ctx)
    s_rows = _silu_rows(cc, "silu_rows")
    cos, sin = _rope_tables(N, S)

    def weights(l):
        return dict(
            ada=_Opnd(gw["w_ada"], "bcols", l), w_in=_Opnd(gw["w_in"], "bcols", l),
            a=_Opnd(gw["w_br_a"], "bcols", l), b=_Opnd(gw["w_br_b"], "bcols", l), c=_Opnd(gw["w_br_c"], "bcols", l),
            out=_Opnd(gw["w_out"], "brows", l), mlp1=_Opnd(gw["w_mlp1"], "bcols", l), mlp2=_Opnd(gw["w_mlp2"], "brows", l))

    IN = weights(0)["w_in"].shape[1]
    DFF = weights(0)["mlp1"].shape[1]
    tn_in = _tile(IN // N_CHIPS, (1152, 768, 512, 384, 256, 128))
    tn_ff = _tile(DFF // N_CHIPS, (1024, 512, 256, 128))
    tn_ada = _tile(6 * D // N_CHIPS, (1536, 768, 512, 256, 128))
    tn_d = D // N_CHIPS
    tk_d = _tile(D, (1024, 512))
    tk_tok = _tile(T, (512, 384, 256))

    saved = []
    xin, pending = X, None
    for l in range(L):
        W = weights(l)
        b_ada = small["b_ada"][l].reshape(1, 6 * D)
        mod = _matmul(s_rows, W["ada"], "nn", tm=rows16, tn=tn_ada, tk=D, name=f"ada_fwd{l}",
                      epilogue=lambda acc, b: (acc + b,), extras=[(b_ada, (1, tn_ada), lambda m, n: (0, n))])
        modtab = jnp.stack([jnp.broadcast_to(mod[B], (B, 6 * D)), mod[:B]], axis=1).reshape(2 * B, 1, 6 * D)
        gains = jnp.concatenate(
            [jnp.tile(small[k][l], 2)[None] for k in ("q_norm_a",) * 3 + ("k_norm_a",) + ("q_norm_c",) * 3 + ("k_norm_c",)],
            axis=0).reshape(QK_CHUNKS, 1, LANES)
        w_bd = _block_diag(small["w_pool"][l]).astype(BF16)
        p_scale = small["pool_scale"][l].reshape(1, POOL_WIDTH)
        sink = small["sink_c"][l]

        x0, h1 = _res_norm(xin, pending, modtab, 0, 1, small["norm1"][l][None], TR=TR, P=P, name=f"norm1_fwd{l}")
        z = _matmul(h1, W["w_in"], "nn", tm=TM, tn=tn_in, tk=D, name=f"in_proj{l}")
        qk = _qk_prep(z, gains, cos, sin, TR=TR, P=P, name=f"qk_prep{l}")
        qa, ka = _to_heads(qk[:, 0:Q_WIDTH], B, Tp), _to_heads(qk[:, Q_WIDTH:Q_WIDTH + KV_WIDTH], B, Tp)
        qc = _to_heads(qk[:, 512:512 + Q_WIDTH], B, Tp)
        kc = _pad_keys(_to_heads(qk[:, 512 + Q_WIDTH:1024], B, Tp))
        va = _to_heads(z[:, 512:640].astype(BF16), B, Tp)
        vc = _pad_keys(_to_heads(z[:, 1152:1280].astype(BF16), B, Tp))
        oa = _from_heads(_attn_fwd(qa, ka, va, None, n_ctx=N, window=False, name=f"attn_a_fwd{l}"))
        oc = _from_heads(_attn_fwd(qc, kc, vc, sink, n_ctx=N, window=True, name=f"attn_c_fwd{l}"))
        pooled, ob = _pool_fwd(z, w_bd, p_scale, B=B, Tp=Tp, n_ctx=N, name=f"pool_fwd{l}")
        y = _merge_fwd(oa, ob, oc, z, W["a"], W["b"], W["c"], D=D, TR=TR, name=f"merge_fwd{l}")
        ao = _matmul(y, W["out"], "nn", tm=TM, tn=D, tk=tn_d, name=f"out_proj{l}")
        x1, h2 = _res_norm(x0, (ao, modtab, 2), modtab, 3, 4, small["norm2"][l][None], TR=TR, P=P, name=f"norm2_fwd{l}")
        a_pre, r_act = _matmul(h2, W["mlp1"], "nn", tm=TM, tn=tn_ff, tk=D, name=f"mlp1_fwd{l}", out_dtypes=(F32, BF16),
                               epilogue=lambda acc: (acc, jnp.square(jnp.maximum(acc, 0.0))))
        mo = _matmul(r_act, W["mlp2"], "nn", tm=TM, tn=D, tk=tn_ff, name=f"mlp2_fwd{l}")
        saved.append(dict(modtab=modtab, gains=gains, w_bd=w_bd, p_scale=p_scale, sink=sink, x0=x0, h1=h1, z=z,
                          qa=qa, ka=ka, va=va, qc=qc, kc=kc, vc=vc, oa=oa, ob=ob, oc=oc, pooled=pooled, y=y, ao=ao,
                          x1=x1, h2=h2, a_pre=a_pre, r_act=r_act, mo=mo))
        xin, pending = x1, (mo, modtab, 5)

    dxo, loss = _loss_head(xin, pending[0], pending[1], 5, target.reshape(B * S, D), TR=TR, P=P, name="loss_head")

    big = {k: [None] * L for k in gw}
    sm = {k: [None] * L for k in ("b_ada", "norm1", "norm2", "q_norm_a", "k_norm_a", "q_norm_c", "k_norm_c",
                                   "sink_c", "w_pool", "pool_scale")}
    d_cctx = jnp.zeros((D,), F32)
    for l in reversed(range(L)):
        W, sv = weights(l), saved[l]
        modtab = sv["modtab"]
        d_mo, dg2 = _gate_bwd(dxo, sv["mo"], modtab, 5, TR=TR, P=P, name=f"gate2_bwd{l}")
        d_a = _matmul(d_mo, W["mlp2"], "nt", tm=TM, tn=tn_ff, tk=D, name=f"mlp2_bwd{l}", out_dtypes=(BF16,),
                      epilogue=lambda acc, a: (acc * (2.0 * jnp.maximum(a, 0.0)),),
                      extras=[(sv["a_pre"], (TM, tn_ff), lambda m, n: (m, n))])
        big["w_mlp2"][l] = _matmul(sv["r_act"], d_mo, "tn", tm=tn_ff, tn=D, tk=tk_tok,
                                   name=f"mlp2_dw{l}").reshape(N_CHIPS, DFF // N_CHIPS, D)
        d_h2 = _matmul(d_a, W["mlp1"], "nt", tm=TM, tn=D, tk=tn_ff, name=f"mlp1_bwd{l}")
        big["w_mlp1"][l] = _matmul(sv["h2"], d_a, "tn", tm=tk_d, tn=tn_ff, tk=tk_tok, name=f"mlp1_dw{l}", out_blocked=True)
        dx1, dsh2, dsc2, dn2 = _norm_bwd(sv["x1"], d_h2, dxo, modtab, 4, small["norm2"][l][None], TR=TR, P=P,
                                         name=f"norm2_bwd{l}")
        d_ao, dg1 = _gate_bwd(dx1, sv["ao"], modtab, 2, TR=TR, P=P, name=f"gate1_bwd{l}")
        d_y = _matmul(d_ao, W["out"], "nt", tm=TM, tn=tn_d, tk=D, name=f"out_bwd{l}")
        big["w_out"][l] = _matmul(sv["y"], d_ao, "tn", tm=tk_d, tn=D, tk=tk_tok,
                                  name=f"out_dw{l}").reshape(N_CHIPS, D // N_CHIPS, D)
        d_pa, d_pb, d_pc, d_ga, d_gb, d_gc = _merge_bwd(d_y, sv["oa"], sv["ob"], sv["oc"], sv["z"], W["a"], W["b"], W["c"],
                                                        D=D, TR=TR, name=f"merge_bwd{l}")
        d_oa = _matmul(d_pa, W["a"], "nt", tm=TM, tn=Q_WIDTH, tk=tn_d, name=f"br_a_bwd{l}", out_dtypes=(BF16,))
        d_ob = _matmul(d_pb, W["b"], "nt", tm=TM, tn=POOL_WIDTH, tk=tn_d, name=f"br_b_bwd{l}")
        d_oc = _matmul(d_pc, W["c"], "nt", tm=TM, tn=Q_WIDTH, tk=tn_d, name=f"br_c_bwd{l}", out_dtypes=(BF16,))
        big["w_br_a"][l] = _matmul(sv["oa"], d_pa, "tn", tm=Q_WIDTH, tn=tn_d, tk=tk_tok, name=f"br_a_dw{l}", out_blocked=True)
        big["w_br_b"][l] = _matmul(sv["ob"], d_pb, "tn", tm=POOL_WIDTH, tn=tn_d, tk=tk_tok, name=f"br_b_dw{l}", out_blocked=True)
        big["w_br_c"][l] = _matmul(sv["oc"], d_pc, "tn", tm=Q_WIDTH, tn=tn_d, tk=tk_tok, name=f"br_c_dw{l}", out_blocked=True)
        d_u, d_wbd, d_ps = _pool_bwd(d_ob, sv["pooled"], sv["w_bd"], sv["p_scale"], B=B, Tp=Tp, n_ctx=N, name=f"pool_bwd{l}")
        dqa, dka, dva = _attn_bwd(sv["qa"], sv["ka"], sv["va"], _to_heads(d_oa, B, Tp), None, n_ctx=N, window=False,
                                  name=f"attn_a_bwd{l}")
        dqc, dkc, dvc, dsink = _attn_bwd(sv["qc"], sv["kc"], sv["vc"], _to_heads(d_oc, B, Tp), sv["sink"], n_ctx=N,
                                         window=True, name=f"attn_c_bwd{l}")
        dqk = jnp.concatenate([_from_heads(dqa), _from_heads(dka), _from_heads(dqc), _from_heads(dkc[:, :, :Tp])], axis=1)
        dz_qk, dgains = _qk_prep_bwd(sv["z"], dqk, sv["gains"], cos, sin, TR=TR, P=P, name=f"qk_prep_bwd{l}")
        dz = jnp.concatenate([dz_qk[:, 0:512], _from_heads(dva).astype(BF16), dz_qk[:, 512:1024],
                              _from_heads(dvc[:, :, :Tp]).astype(BF16), d_u, d_ga, d_gb, d_gc], axis=1)
        d_h1 = _matmul(dz, W["w_in"], "nt", tm=TM, tn=D, tk=tn_in, name=f"in_bwd{l}")
        big["w_in"][l] = _matmul(sv["h1"], dz, "tn", tm=tk_d, tn=tn_in, tk=tk_tok, name=f"in_dw{l}", out_blocked=True)
        dx0, dsh1, dsc1, dn1 = _norm_bwd(sv["x0"], d_h1, dx1, modtab, 1, small["norm1"][l][None], TR=TR, P=P,
                                         name=f"norm1_bwd{l}")

        dm_groups = jnp.concatenate([dsh1, dsc1, dg1, dsh2, dsc2, dg2], axis=-1).reshape(B, 2, 6 * D)
        dm = jnp.zeros((rows16, 6 * D), F32).at[:B].set(dm_groups[:, 1]).at[B].set(jnp.sum(dm_groups[:, 0], axis=0))
        dm_bf = dm.astype(BF16)
        d_s = _matmul(dm_bf, W["ada"], "nt", tm=rows16, tn=D, tk=tn_ada, name=f"ada_bwd{l}")
        big["w_ada"][l] = _matmul(s_rows, dm_bf, "tn", tm=tk_d, tn=tn_ada, tk=rows16, name=f"ada_dw{l}", out_blocked=True)
        db_ada, dcc = _ada_bwd_rows(dm, d_s, cc, f"ada_rows_bwd{l}")
        d_cctx = d_cctx + dcc[B]

        sm["b_ada"][l] = db_ada[0]
        sm["norm1"][l] = jnp.sum(dn1, axis=(0, 1))
        sm["norm2"][l] = jnp.sum(dn2, axis=(0, 1))
        dgh = dgains.reshape(QK_CHUNKS, 2, HEAD_DIM).sum(axis=1)
        sm["q_norm_a"][l] = dgh[0] + dgh[1] + dgh[2]
        sm["k_norm_a"][l] = dgh[3]
        sm["q_norm_c"][l] = dgh[4] + dgh[5] + dgh[6]
        sm["k_norm_c"][l] = dgh[7]
        sm["sink_c"][l] = jnp.sum(dsink[:, :, :N_GROUP, 0], axis=0).reshape(N_HEADS)
        sm["w_pool"][l] = jnp.stack([d_wbd[g * POOL_CH:(g + 1) * POOL_CH, g * POOL_CH:(g + 1) * POOL_CH]
                                     for g in range(POOL_WIDTH // POOL_CH)])
        sm["pool_scale"][l] = d_ps[0]
        dxo = dx0

    grad_x = dxo.reshape(B, Tp, D)[:, N:]
    small_grads = {k: jnp.stack(v) for k, v in sm.items()}
    small_grads["c_ctx"] = d_cctx
    return loss, grad_x, small_grads, big


SMALL_NAMES = ("c_ctx", "b_ada", "norm1", "norm2", "q_norm_a", "k_norm_a", "q_norm_c", "k_norm_c", "sink_c",
               "w_pool", "pool_scale")
BIG_NAMES = ("w_ada", "w_in", "w_br_a", "w_br_b", "w_br_c", "w_out", "w_mlp1", "w_mlp2")
WEIGHT_NAMES = ("c_ctx", "w_ada", "b_ada", "norm1", "norm2", "w_in", "q_norm_a", "k_norm_a", "q_norm_c", "k_norm_c",
                "sink_c", "w_pool", "pool_scale", "w_br_a", "w_br_b", "w_br_c", "w_out", "w_mlp1", "w_mlp2")


def _pack(parts, rows):
    flat = jnp.concatenate([p.reshape(-1).astype(F32) for p in parts])
    return jnp.pad(flat, (0, rows * LANES - flat.shape[0])).reshape(rows, LANES)


def _unpack(packed, like):
    flat, out, at = packed.reshape(-1), [], 0
    for p in like:
        out.append(flat[at:at + p.size].reshape(p.shape))
        at += p.size
    return out


def _reduce_big(stacked):
    names = list(stacked)
    x, y, c = _place()
    core = c.astype(jnp.int32).reshape(1)
    chip = (2 * x + y).astype(jnp.int32).reshape(1)
    flat = [stacked[k].reshape(stacked[k].shape[0], -1, stacked[k].shape[-1]) for k in names]
    landed = _send_other_layer(flat, "grads_to_sibling")
    in_chip = [_add_own_layer(f, r, core, f"grads_add_sibling_{k}") for k, f, r in zip(names, flat, landed)]
    blocked = [h.reshape(stacked[k].shape[1:]) for k, h in zip(names, in_chip)]
    from_chips = _send_chip_blocks(blocked, "grads_to_chips")
    reduced = [_sum_chips(h, r, chip, f"grads_sum_chips_{k}") for k, h, r in zip(names, blocked, from_chips)]
    shared = _share_layers(reduced, "grads_share_layers")
    return dict(zip(names, shared))


def kernel(x, c, ctx, c_ctx, w_ada, b_ada, norm1, norm2, w_in, q_norm_a, k_norm_a, q_norm_c, k_norm_c, sink_c, w_pool, pool_scale, w_br_a, w_br_b, w_br_c, w_out, w_mlp1, w_mlp2, loss_target, m_c_ctx, m_w_ada, m_b_ada, m_norm1, m_norm2, m_w_in, m_q_norm_a, m_k_norm_a, m_q_norm_c, m_k_norm_c, m_sink_c, m_w_pool, m_pool_scale, m_w_br_a, m_w_br_b, m_w_br_c, m_w_out, m_w_mlp1, m_w_mlp2, v_c_ctx, v_w_ada, v_b_ada, v_norm1, v_norm2, v_w_in, v_q_norm_a, v_k_norm_a, v_q_norm_c, v_k_norm_c, v_sink_c, v_w_pool, v_pool_scale, v_w_br_a, v_w_br_b, v_w_br_c, v_w_out, v_w_mlp1, v_w_mlp2):
    given = dict(locals())
    w = {k: given[k] for k in WEIGHT_NAMES}
    m = {k: given["m_" + k] for k in WEIGHT_NAMES}
    v = {k: given["v_" + k] for k in WEIGHT_NAMES}

    gathered = _gather_weights([w[k].astype(BF16) for k in BIG_NAMES], "gather_weights")
    gw = dict(zip(BIG_NAMES, gathered))
    small = {k: w[k] for k in SMALL_NAMES}
    loss_part, grad_x, small_grads, big_grads = _local_step(x, c, ctx, c_ctx, small, gw, loss_target)

    reduced = _reduce_big({k: jnp.stack(big_grads[k]) for k in BIG_NAMES})
    grads, deltas, new_m, new_v = {}, {}, {}, {}
    for k in BIG_NAMES:
        shape = w[k].shape
        g2 = reduced[k].reshape(-1, shape[-1])
        d, mn, vn = _adamw(w[k].reshape(g2.shape), g2, m[k].reshape(g2.shape), v[k].reshape(g2.shape), f"adamw_{k}")
        grads[k], deltas[k], new_m[k], new_v[k] = (t.reshape(shape) for t in (g2, d, mn, vn))

    sizes = sum(w[k].size for k in SMALL_NAMES) + LANES
    rows = -(-sizes // (8 * LANES)) * 8
    parts = _gather_small(_pack([small_grads[k] for k in SMALL_NAMES] + [loss_part[0]], rows), "gather_small")
    zero = jnp.zeros((LANES,), F32)
    packed = [_pack([t[k] for k in SMALL_NAMES] + [zero], rows) for t in (w, m, v)]
    outs = _adamw_small(packed[0], parts.reshape(8, rows, LANES), packed[1], packed[2], "adamw_small")
    like = [w[k] for k in SMALL_NAMES] + [zero]
    for store, packed_out in zip((grads, deltas, new_m, new_v), outs):
        pieces = _unpack(packed_out, like)
        for k, piece in zip(SMALL_NAMES, pieces):
            store[k] = piece
        if store is grads:
            loss = pieces[-1][0]

    return (loss, grad_x, *[grads[k] for k in WEIGHT_NAMES], *[deltas[k] for k in WEIGHT_NAMES],
            *[new_m[k] for k in WEIGHT_NAMES], *[new_v[k] for k in WEIGHT_NAMES])
```

```python
import functools

import jax
import jax.numpy as jnp
from jax import lax
from jax.experimental import pallas as pl
from jax.experimental.pallas import tpu as pltpu

F32 = jnp.float32
BF16 = jnp.bfloat16

HEAD_DIM = 64
GRID_W = 64
AXIS_DIM = HEAD_DIM // 2
ROPE_THETA = 10000.0
N_HEADS = 6
N_KV = 2
N_GROUP = N_HEADS // N_KV
POOL_CH = 64
POOL_WIDTH = 256
POOL_WINDOWS = (2, 4, 8, 16)
WINDOW = 128
Q_BLOCK = 128
Q_WIDTH = N_HEADS * HEAD_DIM
KV_WIDTH = N_KV * HEAD_DIM
GATE_COL = 2 * (Q_WIDTH + 2 * KV_WIDTH) + POOL_WIDTH
U_COL = 2 * (Q_WIDTH + 2 * KV_WIDTH)
EPS = 1e-6
NEG = -1e30
ADAM_LR = 0.001
ADAM_B1 = 0.9
ADAM_B2 = 0.999
ADAM_EPS = 1e-08
ADAM_WD = 0.01
ADAM_STEP = 10

N_CHIPS = 4
LANES = 128
POOL_PAD = 16
VMEM_LIMIT = 48 * 1024 * 1024
MESH = pl.DeviceIdType.MESH
ANY = pl.BlockSpec(memory_space=pl.ANY)


def _params(sem):
    return pltpu.CompilerParams(dimension_semantics=sem, vmem_limit_bytes=VMEM_LIMIT)


def _sds(shape, dtype):
    return jax.ShapeDtypeStruct(tuple(shape), dtype)


class _Opnd:
    def __init__(self, arr, kind="plain", layer=None):
        self.arr, self.kind, self.layer = arr, kind, layer

    @property
    def shape(self):
        a = self.arr
        if self.kind == "plain":
            return a.shape
        if self.kind == "bcols":
            return (a.shape[2], N_CHIPS * a.shape[3])
        return (N_CHIPS * a.shape[2], a.shape[3])

    def spec(self, tr, tc, fn):
        a, layer = self.arr, self.layer
        if self.kind == "plain":
            return pl.BlockSpec((tr, tc), lambda *g: fn(*g))
        if self.kind == "bcols":
            assert a.shape[3] % tc == 0, (a.shape, tc)
            per = a.shape[3] // tc

            def im(*g):
                ri, ci = fn(*g)
                return (ci // per, layer, ri, ci % per)
            return pl.BlockSpec((None, None, tr, tc), im)
        assert a.shape[2] % tr == 0, (a.shape, tr)
        per = a.shape[2] // tr

        def im(*g):
            ri, ci = fn(*g)
            return (ri // per, layer, ri % per, ci)
        return pl.BlockSpec((None, None, tr, tc), im)


def _matmul(a, b, mode, *, tm, tn, tk, name, out_dtypes=(F32,), epilogue=None, extras=(), out_blocked=False):
    if not isinstance(a, _Opnd):
        a = _Opnd(a)
    if not isinstance(b, _Opnd):
        b = _Opnd(b)
    if mode == "nn":
        (M, K), (K2, N) = a.shape, b.shape
        a_spec = a.spec(tm, tk, lambda m, n, k: (m, k))
        b_spec = b.spec(tk, tn, lambda m, n, k: (k, n))
        dims = (((1,), (0,)), ((), ()))
    elif mode == "nt":
        (M, K), (N, K2) = a.shape, b.shape
        a_spec = a.spec(tm, tk, lambda m, n, k: (m, k))
        b_spec = b.spec(tn, tk, lambda m, n, k: (n, k))
        dims = (((1,), (1,)), ((), ()))
    else:
        (K, M), (K2, N) = a.shape, b.shape
        a_spec = a.spec(tk, tm, lambda m, n, k: (k, m))
        b_spec = b.spec(tk, tn, lambda m, n, k: (k, n))
        dims = (((0,), (0,)), ((), ()))
    assert K == K2 and M % tm == 0 and N % tn == 0 and K % tk == 0, (name, M, N, K, K2, tm, tn, tk)
    nk = K // tk
    n_extra = len(extras)
    n_out = len(out_dtypes)
    extra_specs = [pl.BlockSpec(bs, functools.partial(lambda m, n, k, f: f(m, n), f=f)) for (_, bs, f) in extras]
    if out_blocked:
        assert (N // N_CHIPS) % tn == 0
        per = (N // N_CHIPS) // tn
        out_shape = [_sds((N_CHIPS, M, N // N_CHIPS), dt) for dt in out_dtypes]
        out_specs = [pl.BlockSpec((None, tm, tn), lambda m, n, k: (n // per, m, n % per)) for _ in out_dtypes]
    else:
        out_shape = [_sds((M, N), dt) for dt in out_dtypes]
        out_specs = [pl.BlockSpec((tm, tn), lambda m, n, k: (m, n)) for _ in out_dtypes]

    def body(*refs):
        a_ref, b_ref = refs[0], refs[1]
        extra_refs = refs[2:2 + n_extra]
        out_refs = refs[2 + n_extra:2 + n_extra + n_out]
        acc_ref = refs[2 + n_extra + n_out]
        k = pl.program_id(2)
        prod = lax.dot_general(a_ref[...].astype(BF16), b_ref[...].astype(BF16), dims, preferred_element_type=F32)

        def finish(acc):
            outs = epilogue(acc, *[r[...] for r in extra_refs]) if epilogue is not None else (acc,)
            for o_ref, o in zip(out_refs, outs):
                o_ref[...] = o.astype(o_ref.dtype)

        if nk == 1:
            finish(prod)
        else:
            @pl.when(k == 0)
            def _():
                acc_ref[...] = prod

            @pl.when(k > 0)
            def _():
                acc_ref[...] += prod

            @pl.when(k == nk - 1)
            def _():
                finish(acc_ref[...])

    outs = pl.pallas_call(
        body, name=name, grid=(M // tm, N // tn, nk),
        in_specs=[a_spec, b_spec] + extra_specs, out_specs=out_specs, out_shape=out_shape,
        scratch_shapes=[pltpu.VMEM((tm, tn), F32)],
        compiler_params=_params(("parallel", "parallel", "arbitrary")),
    )(a.arr, b.arr, *[e[0] for e in extras])
    return outs[0] if n_out == 1 else outs


def _tile(n, cands):
    for t in cands:
        if n % t == 0:
            return t
    return n


def _grp(i, P):
    return 2 * (i // P) + jnp.minimum(i % P, 1)


def _mod_spec(D, P, part):
    return pl.BlockSpec((1, 1, D), lambda i: (_grp(i, P), 0, part))


def _res_norm(x, pending, modtab, shift_part, scale_part, gain, *, TR, P, name):
    T, D = x.shape
    row = pl.BlockSpec((TR, D), lambda i: (i, 0))
    has_branch = pending is not None
    ins, specs = [x], [row]
    if has_branch:
        branch, gate_tab, gate_part = pending
        ins += [branch, gate_tab]
        specs += [row, _mod_spec(D, P, gate_part)]
    ins += [modtab, modtab, gain]
    specs += [_mod_spec(D, P, shift_part), _mod_spec(D, P, scale_part), pl.BlockSpec((1, D), lambda i: (0, 0))]

    def body(*refs):
        if has_branch:
            x_ref, br_ref, g_ref, sh_ref, sc_ref, gn_ref, xo_ref, h_ref = refs
            xv = x_ref[...] + g_ref[0] * br_ref[...]
        else:
            x_ref, sh_ref, sc_ref, gn_ref, xo_ref, h_ref = refs
            xv = x_ref[...]
        xo_ref[...] = xv
        y = xv * lax.rsqrt(jnp.mean(xv * xv, axis=-1, keepdims=True) + EPS) * gn_ref[...]
        h_ref[...] = (y * (1.0 + sc_ref[0]) + sh_ref[0]).astype(BF16)

    return pl.pallas_call(
        body, name=name, grid=(T // TR,), in_specs=specs, out_specs=[row, row],
        out_shape=[_sds((T, D), F32), _sds((T, D), BF16)], compiler_params=_params(("parallel",)),
    )(*ins)


def _gate_bwd(dx, branch, modtab, gate_part, *, TR, P, name):
    T, D = dx.shape
    G = modtab.shape[0]
    row = pl.BlockSpec((TR, D), lambda i: (i, 0))
    acc = pl.BlockSpec((1, 1, D), lambda i: (_grp(i, P), 0, 0))

    def body(dx_ref, br_ref, g_ref, db_ref, dg_ref):
        r = pl.program_id(0) % P
        dxv = dx_ref[...]
        db_ref[...] = (dxv * g_ref[0]).astype(BF16)
        part = jnp.sum(dxv * br_ref[...], axis=0, keepdims=True)

        @pl.when(r <= 1)
        def _():
            dg_ref[0] = part

        @pl.when(r > 1)
        def _():
            dg_ref[0] += part

    return pl.pallas_call(
        body, name=name, grid=(T // TR,), in_specs=[row, row, _mod_spec(D, P, gate_part)], out_specs=[row, acc],
        out_shape=[_sds((T, D), BF16), _sds((G, 1, D), F32)], compiler_params=_params(("arbitrary",)),
    )(dx, branch, modtab)


def _norm_bwd(x, dh, dres, modtab, scale_part, gain, *, TR, P, name):
    T, D = x.shape
    G = modtab.shape[0]
    row = pl.BlockSpec((TR, D), lambda i: (i, 0))
    acc = pl.BlockSpec((1, 1, D), lambda i: (_grp(i, P), 0, 0))

    def body(x_ref, dh_ref, dres_ref, sc_ref, gn_ref, dx_ref, dsh_ref, dsc_ref, dgn_ref):
        r = pl.program_id(0) % P
        xv, dhv, gn = x_ref[...], dh_ref[...], gn_ref[...]
        rstd = lax.rsqrt(jnp.mean(xv * xv, axis=-1, keepdims=True) + EPS)
        xhat = xv * rstd
        dn = dhv * (1.0 + sc_ref[0])
        dxhat = dn * gn
        dx_ref[...] = dres_ref[...] + rstd * (dxhat - xhat * jnp.mean(dxhat * xhat, axis=-1, keepdims=True))
        p_sh = jnp.sum(dhv, axis=0, keepdims=True)
        p_sc = jnp.sum(dhv * (xhat * gn), axis=0, keepdims=True)
        p_gn = jnp.sum(dn * xhat, axis=0, keepdims=True)

        @pl.when(r <= 1)
        def _():
            dsh_ref[0] = p_sh
            dsc_ref[0] = p_sc
            dgn_ref[0] = p_gn

        @pl.when(r > 1)
        def _():
            dsh_ref[0] += p_sh
            dsc_ref[0] += p_sc
            dgn_ref[0] += p_gn

    return pl.pallas_call(
        body, name=name, grid=(T // TR,),
        in_specs=[row, row, row, _mod_spec(D, P, scale_part), pl.BlockSpec((1, D), lambda i: (0, 0))],
        out_specs=[row, acc, acc, acc],
        out_shape=[_sds((T, D), F32)] + [_sds((G, 1, D), F32)] * 3, compiler_params=_params(("arbitrary",)),
    )(x, dh, dres, modtab, gain)


def _loss_head(x, branch, modtab, gate_part, target, *, TR, P, name):
    T, D = x.shape
    row = pl.BlockSpec((TR, D), lambda i: (i, 0))
    tgt = pl.BlockSpec((TR, D), lambda i: ((i // P) * (P - 1) + jnp.maximum(i % P - 1, 0), 0))
    one = pl.BlockSpec((1, LANES), lambda i: (0, 0))

    def body(x_ref, br_ref, g_ref, t_ref, dy_ref, loss_ref):
        i = pl.program_id(0)
        r = i % P

        @pl.when(i == 0)
        def _():
            loss_ref[...] = jnp.zeros_like(loss_ref)

        @pl.when(r == 0)
        def _():
            dy_ref[...] = jnp.zeros_like(dy_ref)

        @pl.when(r > 0)
        def _():
            err = x_ref[...] + g_ref[0] * br_ref[...] - t_ref[...]
            dy_ref[...] = err / D
            per_tok = jnp.mean(err * err, axis=-1, keepdims=True)
            loss_ref[...] += 0.5 * jnp.sum(per_tok, axis=0, keepdims=True)

    return pl.pallas_call(
        body, name=name, grid=(T // TR,), in_specs=[row, row, _mod_spec(D, P, gate_part), tgt], out_specs=[row, one],
        out_shape=[_sds((T, D), F32), _sds((1, LANES), F32)], compiler_params=_params(("arbitrary",)),
    )(x, branch, modtab, target)


QKV_WIDTH = Q_WIDTH + 2 * KV_WIDTH
QK_NORMED = 4


def _seg_mean(v):
    lane = lax.broadcasted_iota(jnp.int32, v.shape, 1)
    lo = lane < HEAD_DIM
    s0 = jnp.sum(jnp.where(lo, v, 0.0), axis=-1, keepdims=True)
    s1 = jnp.sum(jnp.where(lo, 0.0, v), axis=-1, keepdims=True)
    return jnp.where(lo, s0, s1) * (1.0 / HEAD_DIM)


def _pair_swap(v):
    lane = lax.broadcasted_iota(jnp.int32, v.shape, 1)
    return jnp.where((lane & 1) == 0, pltpu.roll(v, LANES - 1, 1), pltpu.roll(v, 1, 1))


def _chunk(c):
    return slice(c * LANES, (c + 1) * LANES)


def _qk_prep(z, gains, cos, sin, *, TR, P, name):
    T = z.shape[0]

    def body(z_ref, g_ref, c_ref, s_ref, o_ref):
        cs, sn = c_ref[...], s_ref[...]
        for ch in range(QK_NORMED):
            xv = z_ref[:, _chunk(ch)]
            y = xv * lax.rsqrt(_seg_mean(xv * xv) + EPS) * g_ref[0, :, _chunk(ch)]
            o_ref[:, _chunk(ch)] = (y * cs + _pair_swap(y) * sn).astype(BF16)
        o_ref[:, _chunk(QK_NORMED)] = z_ref[:, _chunk(QK_NORMED)].astype(BF16)

    return pl.pallas_call(
        body, name=name, grid=(T // TR, 2),
        in_specs=[pl.BlockSpec((TR, QKV_WIDTH), lambda i, j: (i, j)),
                  pl.BlockSpec((1, 1, QKV_WIDTH), lambda i, j: (j, 0, 0)),
                  pl.BlockSpec((TR, LANES), lambda i, j: (i % P, 0)),
                  pl.BlockSpec((TR, LANES), lambda i, j: (i % P, 0))],
        out_specs=pl.BlockSpec((TR, QKV_WIDTH), lambda i, j: (i, j)),
        out_shape=_sds((T, 2 * QKV_WIDTH), BF16), compiler_params=_params(("parallel", "parallel")),
    )(z, gains, cos, sin)


def _qk_prep_bwd(z, dqkv, gains, cos, sin, *, TR, P, name):
    T = z.shape[0]
    nt = T // TR

    def body(z_ref, d_ref, g_ref, c_ref, s_ref, dz_ref, dg_ref):
        i = pl.program_id(1)
        cs, sn = c_ref[...], s_ref[...]
        parts = []
        for ch in range(QK_NORMED):
            xv, dout, g = z_ref[:, _chunk(ch)], d_ref[:, _chunk(ch)], g_ref[0, :, _chunk(ch)]
            dy = dout * cs + _pair_swap(dout * sn)
            rstd = lax.rsqrt(_seg_mean(xv * xv) + EPS)
            xhat = xv * rstd
            dxhat = dy * g
            dz_ref[:, _chunk(ch)] = (rstd * (dxhat - xhat * _seg_mean(dxhat * xhat))).astype(BF16)
            parts.append(jnp.sum(dy * xhat, axis=0, keepdims=True))
        dz_ref[:, _chunk(QK_NORMED)] = d_ref[:, _chunk(QK_NORMED)].astype(BF16)
        parts.append(jnp.zeros((1, LANES), F32))
        part = jnp.concatenate(parts, axis=1)

        @pl.when(i == 0)
        def _():
            dg_ref[0] = part

        @pl.when(i > 0)
        def _():
            dg_ref[0] += part

    def blk():
        return pl.BlockSpec((TR, QKV_WIDTH), lambda j, i: (i, j))
    return pl.pallas_call(
        body, name=name, grid=(2, nt),
        in_specs=[blk(), blk(),
                  pl.BlockSpec((1, 1, QKV_WIDTH), lambda j, i: (j, 0, 0)),
                  pl.BlockSpec((TR, LANES), lambda j, i: (i % P, 0)),
                  pl.BlockSpec((TR, LANES), lambda j, i: (i % P, 0))],
        out_specs=[blk(), pl.BlockSpec((1, 1, QKV_WIDTH), lambda j, i: (j, 0, 0))],
        out_shape=[_sds((T, 2 * QKV_WIDTH), BF16), _sds((2, 1, QKV_WIDTH), F32)],
        compiler_params=_params(("parallel", "arbitrary")),
    )(z, dqkv, gains, cos, sin)


NT_DIMS = (((1,), (1,)), ((), ()))
TN_DIMS = (((0,), (0,)), ((), ()))
QROWS = N_GROUP * Q_BLOCK
SCORE_SCALE = HEAD_DIM ** -0.5


def _attn_parts(q, k_ref, i, *, n_ctx, seq, t_all, window):
    def scores(start, size):
        kk = k_ref[0, 0, pl.ds(start, size), :]
        return lax.dot_general(q, kk, NT_DIMS, preferred_element_type=F32)

    def ctx_case():
        return [(0, n_ctx, scores(0, n_ctx))]

    def latent_case():
        if not window:
            return [(0, t_all, scores(0, t_all))]
        band = Q_BLOCK + 2 * WINDOW
        start = pl.multiple_of((i - 1) * Q_BLOCK, Q_BLOCK)
        s_loc = scores(start, band)
        rr = lax.broadcasted_iota(jnp.int32, (QROWS, band), 0) & (Q_BLOCK - 1)
        jj = lax.broadcasted_iota(jnp.int32, (QROWS, band), 1)
        kpos = (i - n_ctx // Q_BLOCK - 1) * Q_BLOCK + jj
        valid = (jj - rr >= 0) & (jj - rr <= 2 * WINDOW) & (kpos >= 0) & (kpos < seq)
        return [(0, n_ctx, scores(0, n_ctx)), (start, band, jnp.where(valid, s_loc, NEG))]

    return ctx_case, latent_case


def _softmax_parts(parts, sink_col):
    m = functools.reduce(jnp.maximum, [jnp.max(s, axis=-1, keepdims=True) for (_, _, s) in parts])
    if sink_col is not None:
        m = jnp.maximum(m, sink_col)
    es = [jnp.exp(s - m) for (_, _, s) in parts]
    l = functools.reduce(jnp.add, [jnp.sum(e, axis=-1, keepdims=True) for e in es])
    if sink_col is not None:
        e_sink = jnp.exp(sink_col - m)
        l = l + e_sink
    inv = 1.0 / l
    return [e * inv for e in es], (e_sink * inv if sink_col is not None else None)


def _sink_column(sink_ref, j):
    r = lax.broadcasted_iota(jnp.int32, (QROWS, 1), 0)
    s0, s1, s2 = sink_ref[j * N_GROUP], sink_ref[j * N_GROUP + 1], sink_ref[j * N_GROUP + 2]
    return jnp.where(r < Q_BLOCK, s0, jnp.where(r < 2 * Q_BLOCK, s1, s2))


def _attn_specs(Tp, Tk):
    q_spec = pl.BlockSpec((1, N_GROUP, Q_BLOCK, HEAD_DIM), lambda b, j, i: (b, j, i, 0))
    kv_spec = pl.BlockSpec((1, 1, Tk, HEAD_DIM), lambda b, j, i: (b, j, 0, 0))
    return q_spec, kv_spec


def _attn_fwd(q, k, v, sink, *, n_ctx, window, name):
    B, _, Tp, _ = q.shape
    Tk = k.shape[2]
    seq = Tp - n_ctx
    has_sink = sink is not None
    q_spec, kv_spec = _attn_specs(Tp, Tk)

    def body(*refs):
        if has_sink:
            sink_ref, q_ref, k_ref, v_ref, o_ref = refs
        else:
            q_ref, k_ref, v_ref, o_ref = refs
        j, i = pl.program_id(1), pl.program_id(2)
        qv = q_ref[0].reshape(QROWS, HEAD_DIM) * SCORE_SCALE
        sink_col = _sink_column(sink_ref, j) if has_sink else None
        ctx_case, latent_case = _attn_parts(qv, k_ref, i, n_ctx=n_ctx, seq=seq, t_all=Tp, window=window)

        def run(case):
            parts = case()
            probs, _ = _softmax_parts(parts, sink_col)
            o = None
            for (start, size, _), p in zip(parts, probs):
                t = jnp.dot(p.astype(BF16), v_ref[0, 0, pl.ds(start, size), :], preferred_element_type=F32)
                o = t if o is None else o + t
            o_ref[0] = o.reshape(N_GROUP, Q_BLOCK, HEAD_DIM).astype(BF16)

        @pl.when(i < n_ctx // Q_BLOCK)
        def _():
            run(ctx_case)

        @pl.when(i >= n_ctx // Q_BLOCK)
        def _():
            run(latent_case)

    ins, specs = [q, k, v], [q_spec, kv_spec, kv_spec]
    if has_sink:
        ins, specs = [sink] + ins, [pl.BlockSpec(memory_space=pltpu.SMEM)] + specs
    return pl.pallas_call(
        body, name=name, grid=(B, N_KV, Tp // Q_BLOCK), in_specs=specs, out_specs=q_spec,
        out_shape=_sds(q.shape, BF16), compiler_params=_params(("parallel", "parallel", "parallel")),
    )(*ins)


def _attn_bwd(q, k, v, do, sink, *, n_ctx, window, name):
    B, _, Tp, _ = q.shape
    Tk = k.shape[2]
    seq = Tp - n_ctx
    has_sink = sink is not None
    q_spec, kv_spec = _attn_specs(Tp, Tk)
    sink_spec = pl.BlockSpec((1, 1, 8, LANES), lambda b, j, i: (b, j, 0, 0))
    def body(*refs):
        if has_sink:
            sink_ref, q_ref, k_ref, v_ref, do_ref, dq_ref, dk_ref, dv_ref, ds_ref = refs
        else:
            q_ref, k_ref, v_ref, do_ref, dq_ref, dk_ref, dv_ref = refs
        j, i = pl.program_id(1), pl.program_id(2)

        @pl.when(i == 0)
        def _():
            dk_ref[...] = jnp.zeros_like(dk_ref)
            dv_ref[...] = jnp.zeros_like(dv_ref)
            if has_sink:
                ds_ref[...] = jnp.zeros_like(ds_ref)

        qv = q_ref[0].reshape(QROWS, HEAD_DIM) * SCORE_SCALE
        dov = do_ref[0].reshape(QROWS, HEAD_DIM)
        sink_col = _sink_column(sink_ref, j) if has_sink else None
        ctx_case, latent_case = _attn_parts(qv, k_ref, i, n_ctx=n_ctx, seq=seq, t_all=Tp, window=window)

        def run(case):
            parts = case()
            probs, p_sink = _softmax_parts(parts, sink_col)
            dps = [lax.dot_general(dov, v_ref[0, 0, pl.ds(start, size), :], NT_DIMS, preferred_element_type=F32)
                   for (start, size, _) in parts]
            delta = functools.reduce(jnp.add, [jnp.sum(p * dp, axis=-1, keepdims=True) for p, dp in zip(probs, dps)])
            dq = None
            for (start, size, _), p, dp in zip(parts, probs, dps):
                ds = (p * (dp - delta)).astype(BF16)
                rows = pl.ds(start, size)
                t = jnp.dot(ds, k_ref[0, 0, rows, :], preferred_element_type=F32)
                dq = t if dq is None else dq + t
                dk_ref[0, 0, rows, :] += lax.dot_general(ds, qv, TN_DIMS, preferred_element_type=F32)
                dv_ref[0, 0, rows, :] += lax.dot_general(p.astype(BF16), dov, TN_DIMS, preferred_element_type=F32)
            dq_ref[0] = (dq * SCORE_SCALE).reshape(N_GROUP, Q_BLOCK, HEAD_DIM)
            if has_sink:
                contrib = -(p_sink * delta)
                r = lax.broadcasted_iota(jnp.int32, (QROWS, 1), 0)
                row8 = lax.broadcasted_iota(jnp.int32, (8, LANES), 0)
                upd = jnp.zeros((8, LANES), F32)
                for h in range(N_GROUP):
                    in_head = (r >= h * Q_BLOCK) & (r < (h + 1) * Q_BLOCK)
                    tot = jnp.sum(jnp.where(in_head, contrib, 0.0), axis=0, keepdims=True)
                    upd = upd + jnp.where(row8 == h, tot, 0.0)
                ds_ref[0, 0] += upd

        @pl.when(i < n_ctx // Q_BLOCK)
        def _():
            run(ctx_case)

        @pl.when(i >= n_ctx // Q_BLOCK)
        def _():
            run(latent_case)

    ins, specs = [q, k, v, do], [q_spec, kv_spec, kv_spec, q_spec]
    out_specs = [q_spec, kv_spec, kv_spec]
    out_shape = [_sds(q.shape, F32), _sds(k.shape, F32), _sds(k.shape, F32)]
    if has_sink:
        ins, specs = [sink] + ins, [pl.BlockSpec(memory_space=pltpu.SMEM)] + specs
        out_specs.append(sink_spec)
        out_shape.append(_sds((B, N_KV, 8, LANES), F32))
    return pl.pallas_call(
        body, name=name, grid=(B, N_KV, Tp // Q_BLOCK), in_specs=specs, out_specs=out_specs, out_shape=out_shape,
        compiler_params=_params(("parallel", "parallel", "arbitrary")),
    )(*ins)


def _window_sums(xp):
    n = xp.shape[0]

    def ahead(a, k):
        return pltpu.roll(a, n - k, 0)
    a2 = xp + ahead(xp, 1)
    a4 = a2 + ahead(a2, 2)
    a8 = a4 + ahead(a4, 4)
    a16 = a8 + ahead(a8, 8)
    return (a2, a4, a8, a16)


def _by_group(vals):
    lane = lax.broadcasted_iota(jnp.int32, vals[0].shape, 1)
    return jnp.where(lane < POOL_CH, vals[0], jnp.where(lane < 2 * POOL_CH, vals[1],
                     jnp.where(lane < 3 * POOL_CH, vals[2], vals[3])))


def _pool_counts(n):
    t = lax.broadcasted_iota(jnp.int32, (n, POOL_WIDTH), 0)
    cnts = [(jnp.minimum(t + w // 2, n) - jnp.maximum(t - w // 2, 0)).astype(F32) for w in POOL_WINDOWS]
    return _by_group(cnts)


def _pad_rows(x):
    zeros = jnp.zeros((POOL_PAD, x.shape[1]), x.dtype)
    return jnp.concatenate([zeros, x, zeros], axis=0)


def _pool_stream(u):
    n = u.shape[0]
    sums = _window_sums(_pad_rows(u))
    tots = [pltpu.roll(a, w // 2, 0)[POOL_PAD:POOL_PAD + n] for a, w in zip(sums, POOL_WINDOWS)]
    return _by_group(tots) / _pool_counts(n) - u


def _pool_stream_t(dp):
    n = dp.shape[0]
    sums = _window_sums(_pad_rows(dp / _pool_counts(n)))
    tots = [pltpu.roll(a, w // 2 - 1, 0)[POOL_PAD:POOL_PAD + n] if w > 2 else a[POOL_PAD:POOL_PAD + n]
            for a, w in zip(sums, POOL_WINDOWS)]
    return _by_group(tots) - dp


def _pool_fwd(z, w_bd, scale, *, B, Tp, n_ctx, name):
    T = z.shape[0]
    blk = pl.BlockSpec((Tp, POOL_WIDTH), lambda b: (b, U_COL // POOL_WIDTH))
    out = pl.BlockSpec((Tp, POOL_WIDTH), lambda b: (b, 0))

    def body(u_ref, w_ref, s_ref, p_ref, o_ref):
        for lo, hi in ((0, n_ctx), (n_ctx, Tp)):
            pooled = _pool_stream(u_ref[lo:hi, :]).astype(BF16)
            p_ref[lo:hi, :] = pooled
            mixed = jnp.dot(pooled, w_ref[...], preferred_element_type=F32)
            o_ref[lo:hi, :] = (mixed * s_ref[...]).astype(BF16)

    return pl.pallas_call(
        body, name=name, grid=(B,),
        in_specs=[blk, pl.BlockSpec((POOL_WIDTH, POOL_WIDTH), lambda b: (0, 0)), pl.BlockSpec((1, POOL_WIDTH), lambda b: (0, 0))],
        out_specs=[out, out], out_shape=[_sds((T, POOL_WIDTH), BF16)] * 2, compiler_params=_params(("parallel",)),
    )(z, w_bd, scale)


def _pool_bwd(d_ob, pooled, w_bd, scale, *, B, Tp, n_ctx, name):
    T = d_ob.shape[0]
    blk = pl.BlockSpec((Tp, POOL_WIDTH), lambda b: (b, 0))
    wsp = pl.BlockSpec((POOL_WIDTH, POOL_WIDTH), lambda b: (0, 0))
    ssp = pl.BlockSpec((1, POOL_WIDTH), lambda b: (0, 0))

    def body(d_ref, p_ref, w_ref, s_ref, du_ref, dw_ref, dsc_ref):
        @pl.when(pl.program_id(0) == 0)
        def _():
            dw_ref[...] = jnp.zeros_like(dw_ref)
            dsc_ref[...] = jnp.zeros_like(dsc_ref)

        dv, pv, wv = d_ref[...], p_ref[...], w_ref[...]
        mixed = jnp.dot(pv, wv, preferred_element_type=F32)
        dsc_ref[...] += jnp.sum(dv * mixed, axis=0, keepdims=True)
        dmixed = (dv * s_ref[...]).astype(BF16)
        dw_ref[...] += lax.dot_general(pv, dmixed, TN_DIMS, preferred_element_type=F32)
        dpooled = lax.dot_general(dmixed, wv, NT_DIMS, preferred_element_type=F32)
        for lo, hi in ((0, n_ctx), (n_ctx, Tp)):
            du_ref[lo:hi, :] = _pool_stream_t(dpooled[lo:hi, :]).astype(BF16)

    return pl.pallas_call(
        body, name=name, grid=(B,), in_specs=[blk, blk, wsp, ssp], out_specs=[blk, wsp, ssp],
        out_shape=[_sds((T, POOL_WIDTH), BF16), _sds((POOL_WIDTH, POOL_WIDTH), F32), _sds((1, POOL_WIDTH), F32)],
        compiler_params=_params(("arbitrary",)),
    )(d_ob, pooled, w_bd, scale)


def _merge_specs(z, D, TR, tc, wa, wb, wc):
    def act(width):
        return pl.BlockSpec((TR, width), lambda i, n: (i, 0))

    def gate(part):
        return pl.BlockSpec((TR, tc), lambda i, n: (i, (GATE_COL + part * D) // tc + n))
    w_specs = [w.spec(w.shape[0], tc, lambda i, n: (0, n)) for w in (wa, wb, wc)]
    return [act(Q_WIDTH), act(POOL_WIDTH), act(Q_WIDTH), gate(0), gate(1), gate(2)] + w_specs


def _merge_fwd(oa, ob, oc, z, wa, wb, wc, *, D, TR, name):
    T = oa.shape[0]
    tc = D // N_CHIPS

    def body(oa_ref, ob_ref, oc_ref, ga_ref, gb_ref, gc_ref, wa_ref, wb_ref, wc_ref, y_ref):
        acc = jax.nn.sigmoid(ga_ref[...]) * jnp.dot(oa_ref[...], wa_ref[...], preferred_element_type=F32)
        acc += jax.nn.sigmoid(gb_ref[...]) * jnp.dot(ob_ref[...], wb_ref[...], preferred_element_type=F32)
        acc += jax.nn.sigmoid(gc_ref[...]) * jnp.dot(oc_ref[...], wc_ref[...], preferred_element_type=F32)
        y_ref[...] = acc.astype(BF16)

    return pl.pallas_call(
        body, name=name, grid=(T // TR, D // tc), in_specs=_merge_specs(z, D, TR, tc, wa, wb, wc),
        out_specs=pl.BlockSpec((TR, tc), lambda i, n: (i, n)), out_shape=_sds((T, D), BF16),
        compiler_params=_params(("parallel", "parallel")),
    )(oa, ob, oc, z, z, z, wa.arr, wb.arr, wc.arr)


def _merge_bwd(dy, oa, ob, oc, z, wa, wb, wc, *, D, TR, name):
    T = oa.shape[0]
    tc = D // N_CHIPS
    out = pl.BlockSpec((TR, tc), lambda i, n: (i, n))

    def body(dy_ref, oa_ref, ob_ref, oc_ref, ga_ref, gb_ref, gc_ref, wa_ref, wb_ref, wc_ref,
             dpa_ref, dpb_ref, dpc_ref, dga_ref, dgb_ref, dgc_ref):
        dyv = dy_ref[...]
        for o_ref, g_ref, w_ref, dp_ref, dg_ref in ((oa_ref, ga_ref, wa_ref, dpa_ref, dga_ref),
                                                    (ob_ref, gb_ref, wb_ref, dpb_ref, dgb_ref),
                                                    (oc_ref, gc_ref, wc_ref, dpc_ref, dgc_ref)):
            s = jax.nn.sigmoid(g_ref[...])
            proj = jnp.dot(o_ref[...], w_ref[...], preferred_element_type=F32)
            dp_ref[...] = (dyv * s).astype(BF16)
            dg_ref[...] = (dyv * proj * (s * (1.0 - s))).astype(BF16)

    return pl.pallas_call(
        body, name=name, grid=(T // TR, D // tc), in_specs=[out] + _merge_specs(z, D, TR, tc, wa, wb, wc),
        out_specs=[out] * 6, out_shape=[_sds((T, D), BF16)] * 6, compiler_params=_params(("parallel", "parallel")),
    )(dy, oa, ob, oc, z, z, z, wa.arr, wb.arr, wc.arr)


def _silu_rows(cc, name):
    def body(c_ref, s_ref):
        v = c_ref[...]
        s_ref[...] = (v * jax.nn.sigmoid(v)).astype(BF16)
    return pl.pallas_call(body, name=name, out_shape=_sds(cc.shape, BF16))(cc)


def _ada_bwd_rows(dm, ds, cc, name):
    def body(dm_ref, ds_ref, c_ref, db_ref, dc_ref):
        db_ref[...] = jnp.sum(dm_ref[...], axis=0, keepdims=True)
        v = c_ref[...]
        s = jax.nn.sigmoid(v)
        dc_ref[...] = ds_ref[...] * (s * (1.0 + v * (1.0 - s)))
    return pl.pallas_call(body, name=name, out_shape=[_sds((1, dm.shape[1]), F32), _sds(cc.shape, F32)])(dm, ds, cc)


def _row_tile(rows, cols):
    for t in (512, 256, 128, 64, 32, 16, 8):
        if rows % t == 0 and t * cols * 4 <= (1 << 20):
            return t
    return rows


def _add_own_layer(stacked, landed, core, name):
    _, R, C = stacked.shape
    tr = _row_tile(R, C)

    def body(c_ref, a_ref, b_ref, o_ref, o16_ref):
        tot = a_ref[...] + b_ref[...]
        o_ref[...] = tot
        o16_ref[...] = tot.astype(BF16)

    row = pl.BlockSpec((tr, C), lambda i, c: (i, 0))
    grid_spec = pltpu.PrefetchScalarGridSpec(
        num_scalar_prefetch=1, grid=(R // tr,),
        in_specs=[pl.BlockSpec((None, tr, C), lambda i, c: (c[0], i, 0)), row], out_specs=[row, row])
    return pl.pallas_call(body, name=name, grid_spec=grid_spec, out_shape=[_sds((R, C), F32), _sds((R, C), BF16)],
                          compiler_params=_params(("parallel",)))(core, stacked, landed)


def _sum_chips(own, landed, chip, name):
    _, R, C = own.shape
    tr = _row_tile(R, C)

    def body(k_ref, a_ref, b_ref, o_ref):
        o_ref[...] = ((a_ref[...] + b_ref[0].astype(F32)) + b_ref[1].astype(F32)) + b_ref[2].astype(F32)

    grid_spec = pltpu.PrefetchScalarGridSpec(
        num_scalar_prefetch=1, grid=(R // tr,),
        in_specs=[pl.BlockSpec((None, tr, C), lambda i, k: (k[0], i, 0)), pl.BlockSpec((3, tr, C), lambda i, k: (0, i, 0))],
        out_specs=pl.BlockSpec((tr, C), lambda i, k: (i, 0)))
    return pl.pallas_call(body, name=name, grid_spec=grid_spec, out_shape=_sds((R, C), F32),
                          compiler_params=_params(("parallel",)))(chip, own, landed)


def _adam_math(w, g, m, v):
    m = ADAM_B1 * m + (1.0 - ADAM_B1) * g
    v = ADAM_B2 * v + (1.0 - ADAM_B2) * (g * g)
    m_hat = m / (1.0 - ADAM_B1 ** ADAM_STEP)
    v_hat = v / (1.0 - ADAM_B2 ** ADAM_STEP)
    delta = -ADAM_LR * (m_hat / (jnp.sqrt(v_hat) + ADAM_EPS) + ADAM_WD * w)
    return delta, m, v


def _adamw(w, g, m, v, name):
    R, C = w.shape
    tr = _row_tile(R, C)
    row = pl.BlockSpec((tr, C), lambda i: (i, 0))

    def body(w_ref, g_ref, m_ref, v_ref, d_ref, mo_ref, vo_ref):
        d, mn, vn = _adam_math(w_ref[...], g_ref[...], m_ref[...], v_ref[...])
        d_ref[...] = d
        mo_ref[...] = mn
        vo_ref[...] = vn

    return pl.pallas_call(body, name=name, grid=(R // tr,), in_specs=[row] * 4, out_specs=[row] * 3,
                          out_shape=[_sds((R, C), F32)] * 3, compiler_params=_params(("parallel",)))(w, g, m, v)


def _adamw_small(w, parts, m, v, name):
    R, C = w.shape

    def body(w_ref, p_ref, m_ref, v_ref, g_ref, d_ref, mo_ref, vo_ref):
        g = p_ref[0]
        for dev in range(1, 8):
            g = g + p_ref[dev]
        d, mn, vn = _adam_math(w_ref[...], g, m_ref[...], v_ref[...])
        g_ref[...] = g
        d_ref[...] = d
        mo_ref[...] = mn
        vo_ref[...] = vn

    return pl.pallas_call(body, name=name, out_shape=[_sds((R, C), F32)] * 4)(w, parts, m, v)


def _place():
    return lax.axis_index("x"), lax.axis_index("y"), lax.axis_index("c")


def _other_chips(x, y):
    return [(1 - x, y), (x, 1 - y), (1 - x, 1 - y)]


def _rcopy(src, dst, ssem, rsem, dev):
    return pltpu.make_async_remote_copy(src_ref=src, dst_ref=dst, send_sem=ssem, recv_sem=rsem,
                                        device_id=dev, device_id_type=MESH)


def _gather_weights(shards, name):
    n = len(shards)

    def body(*refs):
        src, out = refs[:n], refs[n:2 * n]
        send_sems, recv_sems, local_sems = refs[2 * n:]
        x, y, c = _place()
        sibling = (x, y, 1 - c)
        chips = _other_chips(x, y)
        mine = 2 * x + y
        local = [pltpu.make_async_copy(src[w], out[w].at[mine], local_sems.at[w]) for w in range(n)]
        for cp in local:
            cp.start()
        first = [_rcopy(src[w].at[c], out[w].at[mine, c], send_sems.at[w, j], recv_sems.at[w, j], (*chip, c))
                 for w in range(n) for j, chip in enumerate(chips)]
        for cp in first:
            cp.start()
        passed = []
        for w in range(n):
            for j, (px, py) in enumerate(chips):
                landed = out[w].at[2 * px + py, c]
                _rcopy(landed, landed, send_sems.at[w, j], recv_sems.at[w, j], (px, py, c)).wait_recv()
                cp = _rcopy(landed, landed, send_sems.at[w, 3 + j], recv_sems.at[w, 3 + j], sibling)
                cp.start()
                passed.append(cp)
        for w in range(n):
            for j, (px, py) in enumerate(chips):
                landed = out[w].at[2 * px + py, 1 - c]
                _rcopy(landed, landed, send_sems.at[w, 3 + j], recv_sems.at[w, 3 + j], sibling).wait_recv()
        for cp in first + passed:
            cp.wait_send()
        for cp in local:
            cp.wait()

    return pl.pallas_call(
        body, name=name, in_specs=[ANY] * n, out_specs=[ANY] * n,
        out_shape=[_sds((N_CHIPS,) + s.shape, s.dtype) for s in shards],
        scratch_shapes=[pltpu.SemaphoreType.DMA((n, 6)), pltpu.SemaphoreType.DMA((n, 6)), pltpu.SemaphoreType.DMA((n,))],
    )(*shards)


def _send_other_layer(stacked, name):
    n = len(stacked)

    def body(*refs):
        src, out = refs[:n], refs[n:2 * n]
        send_sems, recv_sems = refs[2 * n:]
        x, y, c = _place()
        cps = [_rcopy(src[w].at[1 - c], out[w], send_sems.at[w], recv_sems.at[w], (x, y, 1 - c)) for w in range(n)]
        for cp in cps:
            cp.start()
        for cp in cps:
            cp.wait_recv()
        for cp in cps:
            cp.wait_send()

    return pl.pallas_call(
        body, name=name, in_specs=[ANY] * n, out_specs=[ANY] * n,
        out_shape=[_sds(s.shape[1:], s.dtype) for s in stacked],
        scratch_shapes=[pltpu.SemaphoreType.DMA((n,)), pltpu.SemaphoreType.DMA((n,))],
    )(*stacked)


def _send_chip_blocks(blocked, name):
    n = len(blocked)

    def body(*refs):
        src, out = refs[:n], refs[n:2 * n]
        send_sems, recv_sems = refs[2 * n:]
        x, y, c = _place()
        cps = [_rcopy(src[w].at[2 * px + py], out[w].at[j], send_sems.at[w, j], recv_sems.at[w, j], (px, py, c))
               for w in range(n) for j, (px, py) in enumerate(_other_chips(x, y))]
        for cp in cps:
            cp.start()
        for cp in cps:
            cp.wait_recv()
        for cp in cps:
            cp.wait_send()

    return pl.pallas_call(
        body, name=name, in_specs=[ANY] * n, out_specs=[ANY] * n,
        out_shape=[_sds((3,) + s.shape[1:], s.dtype) for s in blocked],
        scratch_shapes=[pltpu.SemaphoreType.DMA((n, 3)), pltpu.SemaphoreType.DMA((n, 3))],
    )(*blocked)


def _share_layers(reduced, name):
    n = len(reduced)

    def body(*refs):
        src, out = refs[:n], refs[n:2 * n]
        send_sems, recv_sems, local_sems = refs[2 * n:]
        x, y, c = _place()
        local = [pltpu.make_async_copy(src[w], out[w].at[c], local_sems.at[w]) for w in range(n)]
        for cp in local:
            cp.start()
        cps = [_rcopy(src[w], out[w].at[c], send_sems.at[w], recv_sems.at[w], (x, y, 1 - c)) for w in range(n)]
        for cp in cps:
            cp.start()
        for w in range(n):
            _rcopy(src[w], out[w].at[1 - c], send_sems.at[w], recv_sems.at[w], (x, y, 1 - c)).wait_recv()
        for cp in cps:
            cp.wait_send()
        for cp in local:
            cp.wait()

    return pl.pallas_call(
        body, name=name, in_specs=[ANY] * n, out_specs=[ANY] * n,
        out_shape=[_sds((2,) + s.shape, s.dtype) for s in reduced],
        scratch_shapes=[pltpu.SemaphoreType.DMA((n,)), pltpu.SemaphoreType.DMA((n,)), pltpu.SemaphoreType.DMA((n,))],
    )(*reduced)


def _gather_small(block, name):
    m_per, n = block.shape

    def body(x_ref, out_ref, send_sems, recv_sems, local_sem):
        x, y, c = _place()
        me, sibling = (x, y, c), (x, y, 1 - c)
        chips = _other_chips(x, y)

        def rows(px, py, pc):
            return out_ref.at[pl.ds((4 * px + 2 * py + pc) * m_per, m_per), :]

        def copy(k, blk, to, src=None):
            return _rcopy(rows(*blk) if src is None else src, rows(*blk), send_sems.at[k], recv_sems.at[k], to)

        mine = pltpu.make_async_copy(x_ref, rows(*me), local_sem)
        mine.start()
        first = [copy(0, me, sibling, src=x_ref)]
        first += [copy(1 + j, me, (*chip, c), src=x_ref) for j, chip in enumerate(chips)]
        for cp in first:
            cp.start()
        passed = [copy(4 + j, (*chip, c), sibling) for j, chip in enumerate(chips)]
        for j, chip in enumerate(chips):
            copy(1 + j, (*chip, c), me).wait_recv()
            passed[j].start()
        copy(0, sibling, me).wait_recv()
        for j, chip in enumerate(chips):
            copy(4 + j, (*chip, 1 - c), me).wait_recv()
        for cp in first + passed:
            cp.wait_send()
        mine.wait()

    return pl.pallas_call(
        body, name=name, out_shape=_sds((8 * m_per, n), block.dtype),
        in_specs=[pl.BlockSpec(memory_space=pltpu.VMEM)], out_specs=pl.BlockSpec(memory_space=pltpu.VMEM),
        scratch_shapes=[pltpu.SemaphoreType.DMA((7,)), pltpu.SemaphoreType.DMA((7,)), pltpu.SemaphoreType.DMA],
    )(block)


def _rope_tables(n_ctx, seq):
    rows = seq // GRID_W
    r = jnp.repeat(jnp.arange(rows, dtype=F32), GRID_W)
    col = jnp.tile(jnp.arange(GRID_W, dtype=F32), rows)
    inv = 1.0 / (ROPE_THETA ** (jnp.arange(0, AXIS_DIM, 2, dtype=F32) / AXIS_DIM))
    ang = jnp.concatenate([r[:, None] * inv, col[:, None] * inv], axis=-1)
    cos = jnp.repeat(jnp.cos(ang), 2, axis=-1)
    sin = jnp.repeat(jnp.sin(ang), 2, axis=-1) * jnp.tile(jnp.array([-1.0, 1.0], F32), HEAD_DIM // 2)
    cos = jnp.concatenate([jnp.ones((n_ctx, HEAD_DIM), F32), cos], axis=0)
    sin = jnp.concatenate([jnp.zeros((n_ctx, HEAD_DIM), F32), sin], axis=0)
    return jnp.tile(cos, (1, 2)), jnp.tile(sin, (1, 2))


def _to_heads(t, B, Tp):
    return t.reshape(B, Tp, t.shape[1] // HEAD_DIM, HEAD_DIM).transpose(0, 2, 1, 3)


def _from_heads(t):
    B, H, Tp, _ = t.shape
    return t.transpose(0, 2, 1, 3).reshape(B * Tp, H * HEAD_DIM)


def _pad_keys(t):
    return jnp.pad(t, ((0, 0), (0, 0), (0, Q_BLOCK), (0, 0)))


def _block_diag(w_pool_l):
    out = jnp.zeros((POOL_WIDTH, POOL_WIDTH), w_pool_l.dtype)
    for g in range(w_pool_l.shape[0]):
        out = out.at[g * POOL_CH:(g + 1) * POOL_CH, g * POOL_CH:(g + 1) * POOL_CH].set(w_pool_l[g])
    return out


def _local_step(x, c, ctx, c_ctx, small, gw, target):
    B, S, D = x.shape
    N = ctx.shape[1]
    L = small["norm1"].shape[0]
    Tp = N + S
    T = B * Tp
    TR = N
    P = Tp // N
    rows16 = 16
    assert N % Q_BLOCK == 0 and S % N == 0 and B + 1 <= rows16
    TM = _tile(T, (1024, 768, 512, 384, 256, 128))
    TMG = _tile(T, (512, 384, 256, 128))

    X = jnp.concatenate([ctx, x], axis=1).reshape(T, D)
    cc = jnp.zeros((rows16, D), F32).at[:B].set(c).at[B].set(c_ctx)
    s_rows = _silu_rows(cc, "silu_rows")
    cos, sin = _rope_tables(N, S)

    def weights(l):
        return dict(
            ada=_Opnd(gw["w_ada"], "bcols", l), w_in=_Opnd(gw["w_in"], "bcols", l),
            a=_Opnd(gw["w_br_a"], "bcols", l), b=_Opnd(gw["w_br_b"], "bcols", l), c=_Opnd(gw["w_br_c"], "bcols", l),
            out=_Opnd(gw["w_out"], "brows", l), mlp1=_Opnd(gw["w_mlp1"], "bcols", l), mlp2=_Opnd(gw["w_mlp2"], "brows", l))

    IN = weights(0)["w_in"].shape[1]
    DFF = weights(0)["mlp1"].shape[1]
    tn_in = _tile(IN // N_CHIPS, (1152, 768, 512, 384, 256, 128))
    tn_ff = _tile(DFF // N_CHIPS, (1024, 512, 256, 128))
    tn_ada = _tile(6 * D // N_CHIPS, (1536, 768, 512, 256, 128))
    tn_d = D // N_CHIPS
    tk_d = _tile(D, (1024, 512))
    tk_tok = _tile(T, (512, 384, 256))

    saved = []
    xin, pending = X, None
    for l in range(L):
        W = weights(l)
        b_ada = small["b_ada"][l].reshape(1, 6 * D)
        mod = _matmul(s_rows, W["ada"], "nn", tm=rows16, tn=tn_ada, tk=D, name=f"ada_fwd{l}",
                      epilogue=lambda acc, b: (acc + b,), extras=[(b_ada, (1, tn_ada), lambda m, n: (0, n))])
        modtab = jnp.stack([jnp.broadcast_to(mod[B], (B, 6 * D)), mod[:B]], axis=1).reshape(2 * B, 1, 6 * D)
        ones = jnp.ones((LANES,), F32)
        gains = jnp.stack([
            jnp.concatenate([jnp.tile(small[q][l], 2)] * 3 + [jnp.tile(small[k][l], 2), ones])
            for q, k in (("q_norm_a", "k_norm_a"), ("q_norm_c", "k_norm_c"))]).reshape(2, 1, QKV_WIDTH)
        w_bd = _block_diag(small["w_pool"][l]).astype(BF16)
        p_scale = small["pool_scale"][l].reshape(1, POOL_WIDTH)
        sink = small["sink_c"][l]

        x0, h1 = _res_norm(xin, pending, modtab, 0, 1, small["norm1"][l][None], TR=TR, P=P, name=f"norm1_fwd{l}")
        z = _matmul(h1, W["w_in"], "nn", tm=TM, tn=tn_in, tk=D, name=f"in_proj{l}")
        qkv = _qk_prep(z, gains, cos, sin, TR=TR, P=P, name=f"qk_prep{l}")
        cuts = ((0, Q_WIDTH), (Q_WIDTH, Q_WIDTH + KV_WIDTH), (Q_WIDTH + KV_WIDTH, QKV_WIDTH))
        qa, ka, va = (_to_heads(qkv[:, a:b], B, Tp) for a, b in cuts)
        qc, kc, vc = (_to_heads(qkv[:, QKV_WIDTH + a:QKV_WIDTH + b], B, Tp) for a, b in cuts)
        kc, vc = _pad_keys(kc), _pad_keys(vc)
        oa = _from_heads(_attn_fwd(qa, ka, va, None, n_ctx=N, window=False, name=f"attn_a_fwd{l}"))
        oc = _from_heads(_attn_fwd(qc, kc, vc, sink, n_ctx=N, window=True, name=f"attn_c_fwd{l}"))
        pooled, ob = _pool_fwd(z, w_bd, p_scale, B=B, Tp=Tp, n_ctx=N, name=f"pool_fwd{l}")
        y = _merge_fwd(oa, ob, oc, z, W["a"], W["b"], W["c"], D=D, TR=TMG, name=f"merge_fwd{l}")
        ao = _matmul(y, W["out"], "nn", tm=TM, tn=D, tk=tn_d, name=f"out_proj{l}")
        x1, h2 = _res_norm(x0, (ao, modtab, 2), modtab, 3, 4, small["norm2"][l][None], TR=TR, P=P, name=f"norm2_fwd{l}")
        a_pre, r_act = _matmul(h2, W["mlp1"], "nn", tm=TM, tn=tn_ff, tk=D, name=f"mlp1_fwd{l}", out_dtypes=(F32, BF16),
                               epilogue=lambda acc: (acc, jnp.square(jnp.maximum(acc, 0.0))))
        mo = _matmul(r_act, W["mlp2"], "nn", tm=TM, tn=D, tk=tn_ff, name=f"mlp2_fwd{l}")
        saved.append(dict(modtab=modtab, gains=gains, w_bd=w_bd, p_scale=p_scale, sink=sink, x0=x0, h1=h1, z=z,
                          qa=qa, ka=ka, va=va, qc=qc, kc=kc, vc=vc, oa=oa, ob=ob, oc=oc, pooled=pooled, y=y, ao=ao,
                          x1=x1, h2=h2, a_pre=a_pre, r_act=r_act, mo=mo))
        xin, pending = x1, (mo, modtab, 5)

    dxo, loss = _loss_head(xin, pending[0], pending[1], 5, target.reshape(B * S, D), TR=TR, P=P, name="loss_head")

    big = {k: [None] * L for k in gw}
    sm = {k: [None] * L for k in ("b_ada", "norm1", "norm2", "q_norm_a", "k_norm_a", "q_norm_c", "k_norm_c",
                                   "sink_c", "w_pool", "pool_scale")}
    d_cctx = jnp.zeros((D,), F32)
    for l in reversed(range(L)):
        W, sv = weights(l), saved[l]
        modtab = sv["modtab"]
        d_mo, dg2 = _gate_bwd(dxo, sv["mo"], modtab, 5, TR=TR, P=P, name=f"gate2_bwd{l}")
        d_a = _matmul(d_mo, W["mlp2"], "nt", tm=TM, tn=tn_ff, tk=D, name=f"mlp2_bwd{l}", out_dtypes=(BF16,),
                      epilogue=lambda acc, a: (acc * (2.0 * jnp.maximum(a, 0.0)),),
                      extras=[(sv["a_pre"], (TM, tn_ff), lambda m, n: (m, n))])
        big["w_mlp2"][l] = _matmul(sv["r_act"], d_mo, "tn", tm=tn_ff, tn=D, tk=tk_tok,
                                   name=f"mlp2_dw{l}").reshape(N_CHIPS, DFF // N_CHIPS, D)
        d_h2 = _matmul(d_a, W["mlp1"], "nt", tm=TM, tn=D, tk=tn_ff, name=f"mlp1_bwd{l}")
        big["w_mlp1"][l] = _matmul(sv["h2"], d_a, "tn", tm=tk_d, tn=tn_ff, tk=tk_tok, name=f"mlp1_dw{l}", out_blocked=True)
        dx1, dsh2, dsc2, dn2 = _norm_bwd(sv["x1"], d_h2, dxo, modtab, 4, small["norm2"][l][None], TR=TR, P=P,
                                         name=f"norm2_bwd{l}")
        d_ao, dg1 = _gate_bwd(dx1, sv["ao"], modtab, 2, TR=TR, P=P, name=f"gate1_bwd{l}")
        d_y = _matmul(d_ao, W["out"], "nt", tm=TM, tn=tn_d, tk=D, name=f"out_bwd{l}")
        big["w_out"][l] = _matmul(sv["y"], d_ao, "tn", tm=tk_d, tn=D, tk=tk_tok,
                                  name=f"out_dw{l}").reshape(N_CHIPS, D // N_CHIPS, D)
        d_pa, d_pb, d_pc, d_ga, d_gb, d_gc = _merge_bwd(d_y, sv["oa"], sv["ob"], sv["oc"], sv["z"], W["a"], W["b"], W["c"],
                                                        D=D, TR=TMG, name=f"merge_bwd{l}")
        d_oa = _matmul(d_pa, W["a"], "nt", tm=TM, tn=Q_WIDTH, tk=tn_d, name=f"br_a_bwd{l}", out_dtypes=(BF16,))
        d_ob = _matmul(d_pb, W["b"], "nt", tm=TM, tn=POOL_WIDTH, tk=tn_d, name=f"br_b_bwd{l}")
        d_oc = _matmul(d_pc, W["c"], "nt", tm=TM, tn=Q_WIDTH, tk=tn_d, name=f"br_c_bwd{l}", out_dtypes=(BF16,))
        big["w_br_a"][l] = _matmul(sv["oa"], d_pa, "tn", tm=Q_WIDTH, tn=tn_d, tk=tk_tok, name=f"br_a_dw{l}", out_blocked=True)
        big["w_br_b"][l] = _matmul(sv["ob"], d_pb, "tn", tm=POOL_WIDTH, tn=tn_d, tk=tk_tok, name=f"br_b_dw{l}", out_blocked=True)
        big["w_br_c"][l] = _matmul(sv["oc"], d_pc, "tn", tm=Q_WIDTH, tn=tn_d, tk=tk_tok, name=f"br_c_dw{l}", out_blocked=True)
        d_u, d_wbd, d_ps = _pool_bwd(d_ob, sv["pooled"], sv["w_bd"], sv["p_scale"], B=B, Tp=Tp, n_ctx=N, name=f"pool_bwd{l}")
        dqa, dka, dva = _attn_bwd(sv["qa"], sv["ka"], sv["va"], _to_heads(d_oa, B, Tp), None, n_ctx=N, window=False,
                                  name=f"attn_a_bwd{l}")
        dqc, dkc, dvc, dsink = _attn_bwd(sv["qc"], sv["kc"], sv["vc"], _to_heads(d_oc, B, Tp), sv["sink"], n_ctx=N,
                                         window=True, name=f"attn_c_bwd{l}")
        dqkv = jnp.concatenate([_from_heads(t) for t in (dqa, dka, dva, dqc, dkc[:, :, :Tp], dvc[:, :, :Tp])], axis=1)
        dz_qkv, dgains = _qk_prep_bwd(sv["z"], dqkv, sv["gains"], cos, sin, TR=TR, P=P, name=f"qk_prep_bwd{l}")
        dz = jnp.concatenate([dz_qkv, d_u, d_ga, d_gb, d_gc], axis=1)
        d_h1 = _matmul(dz, W["w_in"], "nt", tm=TM, tn=D, tk=tn_in, name=f"in_bwd{l}")
        big["w_in"][l] = _matmul(sv["h1"], dz, "tn", tm=tk_d, tn=tn_in, tk=tk_tok, name=f"in_dw{l}", out_blocked=True)
        dx0, dsh1, dsc1, dn1 = _norm_bwd(sv["x0"], d_h1, dx1, modtab, 1, small["norm1"][l][None], TR=TR, P=P,
                                         name=f"norm1_bwd{l}")

        dm_groups = jnp.concatenate([dsh1, dsc1, dg1, dsh2, dsc2, dg2], axis=-1).reshape(B, 2, 6 * D)
        dm = jnp.zeros((rows16, 6 * D), F32).at[:B].set(dm_groups[:, 1]).at[B].set(jnp.sum(dm_groups[:, 0], axis=0))
        dm_bf = dm.astype(BF16)
        d_s = _matmul(dm_bf, W["ada"], "nt", tm=rows16, tn=D, tk=tn_ada, name=f"ada_bwd{l}")
        big["w_ada"][l] = _matmul(s_rows, dm_bf, "tn", tm=tk_d, tn=tn_ada, tk=rows16, name=f"ada_dw{l}", out_blocked=True)
        db_ada, dcc = _ada_bwd_rows(dm, d_s, cc, f"ada_rows_bwd{l}")
        d_cctx = d_cctx + dcc[B]

        sm["b_ada"][l] = db_ada[0]
        sm["norm1"][l] = jnp.sum(dn1, axis=(0, 1))
        sm["norm2"][l] = jnp.sum(dn2, axis=(0, 1))
        dgh = dgains.reshape(2, QKV_WIDTH // HEAD_DIM, HEAD_DIM)
        sm["q_norm_a"][l] = jnp.sum(dgh[0, :N_HEADS], axis=0)
        sm["k_norm_a"][l] = jnp.sum(dgh[0, N_HEADS:N_HEADS + N_KV], axis=0)
        sm["q_norm_c"][l] = jnp.sum(dgh[1, :N_HEADS], axis=0)
        sm["k_norm_c"][l] = jnp.sum(dgh[1, N_HEADS:N_HEADS + N_KV], axis=0)
        sm["sink_c"][l] = jnp.sum(dsink[:, :, :N_GROUP, 0], axis=0).reshape(N_HEADS)
        sm["w_pool"][l] = jnp.stack([d_wbd[g * POOL_CH:(g + 1) * POOL_CH, g * POOL_CH:(g + 1) * POOL_CH]
                                     for g in range(POOL_WIDTH // POOL_CH)])
        sm["pool_scale"][l] = d_ps[0]
        dxo = dx0

    grad_x = dxo.reshape(B, Tp, D)[:, N:]
    small_grads = {k: jnp.stack(v) for k, v in sm.items()}
    small_grads["c_ctx"] = d_cctx
    return loss, grad_x, small_grads, big


SMALL_NAMES = ("c_ctx", "b_ada", "norm1", "norm2", "q_norm_a", "k_norm_a", "q_norm_c", "k_norm_c", "sink_c",
               "w_pool", "pool_scale")
BIG_NAMES = ("w_ada", "w_in", "w_br_a", "w_br_b", "w_br_c", "w_out", "w_mlp1", "w_mlp2")
WEIGHT_NAMES = ("c_ctx", "w_ada", "b_ada", "norm1", "norm2", "w_in", "q_norm_a", "k_norm_a", "q_norm_c", "k_norm_c",
                "sink_c", "w_pool", "pool_scale", "w_br_a", "w_br_b", "w_br_c", "w_out", "w_mlp1", "w_mlp2")


def _pack(parts, rows):
    flat = jnp.concatenate([p.reshape(-1).astype(F32) for p in parts])
    return jnp.pad(flat, (0, rows * LANES - flat.shape[0])).reshape(rows, LANES)


def _unpack(packed, like):
    flat, out, at = packed.reshape(-1), [], 0
    for p in like:
        out.append(flat[at:at + p.size].reshape(p.shape))
        at += p.size
    return out


def _reduce_big(stacked):
    names = list(stacked)
    x, y, c = _place()
    core = c.astype(jnp.int32).reshape(1)
    chip = (2 * x + y).astype(jnp.int32).reshape(1)
    flat = [stacked[k].reshape(stacked[k].shape[0], -1, stacked[k].shape[-1]) for k in names]
    landed = _send_other_layer(flat, "grads_to_sibling")
    in_chip = [_add_own_layer(f, r, core, f"grads_add_sibling_{k}") for k, f, r in zip(names, flat, landed)]
    blocked = [h.reshape(stacked[k].shape[1:]) for k, (h, _) in zip(names, in_chip)]
    blocked16 = [h.reshape(stacked[k].shape[1:]) for k, (_, h) in zip(names, in_chip)]
    from_chips = _send_chip_blocks(blocked16, "grads_to_chips")
    reduced = [_sum_chips(h, r, chip, f"grads_sum_chips_{k}") for k, h, r in zip(names, blocked, from_chips)]
    shared = _share_layers(reduced, "grads_share_layers")
    return dict(zip(names, shared))


def kernel(x, c, ctx, c_ctx, w_ada, b_ada, norm1, norm2, w_in, q_norm_a, k_norm_a, q_norm_c, k_norm_c, sink_c, w_pool, pool_scale, w_br_a, w_br_b, w_br_c, w_out, w_mlp1, w_mlp2, loss_target, m_c_ctx, m_w_ada, m_b_ada, m_norm1, m_norm2, m_w_in, m_q_norm_a, m_k_norm_a, m_q_norm_c, m_k_norm_c, m_sink_c, m_w_pool, m_pool_scale, m_w_br_a, m_w_br_b, m_w_br_c, m_w_out, m_w_mlp1, m_w_mlp2, v_c_ctx, v_w_ada, v_b_ada, v_norm1, v_norm2, v_w_in, v_q_norm_a, v_k_norm_a, v_q_norm_c, v_k_norm_c, v_sink_c, v_w_pool, v_pool_scale, v_w_br_a, v_w_br_b, v_w_br_c, v_w_out, v_w_mlp1, v_w_mlp2):
    given = dict(locals())
    w = {k: given[k] for k in WEIGHT_NAMES}
    m = {k: given["m_" + k] for k in WEIGHT_NAMES}
    v = {k: given["v_" + k] for k in WEIGHT_NAMES}

    gathered = _gather_weights([w[k].astype(BF16) for k in BIG_NAMES], "gather_weights")
    gw = dict(zip(BIG_NAMES, gathered))
    small = {k: w[k] for k in SMALL_NAMES}
    loss_part, grad_x, small_grads, big_grads = _local_step(x, c, ctx, c_ctx, small, gw, loss_target)

    reduced = _reduce_big({k: jnp.stack(big_grads[k]) for k in BIG_NAMES})
    grads, deltas, new_m, new_v = {}, {}, {}, {}
    for k in BIG_NAMES:
        shape = w[k].shape
        g2 = reduced[k].reshape(-1, shape[-1])
        d, mn, vn = _adamw(w[k].reshape(g2.shape), g2, m[k].reshape(g2.shape), v[k].reshape(g2.shape), f"adamw_{k}")
        grads[k], deltas[k], new_m[k], new_v[k] = (t.reshape(shape) for t in (g2, d, mn, vn))

    sizes = sum(w[k].size for k in SMALL_NAMES) + LANES
    rows = -(-sizes // (8 * LANES)) * 8
    parts = _gather_small(_pack([small_grads[k] for k in SMALL_NAMES] + [loss_part[0]], rows), "gather_small")
    zero = jnp.zeros((LANES,), F32)
    packed = [_pack([t[k] for k in SMALL_NAMES] + [zero], rows) for t in (w, m, v)]
    outs = _adamw_small(packed[0], parts.reshape(8, rows, LANES), packed[1], packed[2], "adamw_small")
    like = [w[k] for k in SMALL_NAMES] + [zero]
    for store, packed_out in zip((grads, deltas, new_m, new_v), outs):
        pieces = _unpack(packed_out, like)
        for k, piece in zip(SMALL_NAMES, pieces):
            store[k] = piece
        if store is grads:
            loss = pieces[-1][0]

    return (loss, grad_x, *[grads[k] for k in WEIGHT_NAMES], *[deltas[k] for k in WEIGHT_NAMES],
            *[new_m[k] for k in WEIGHT_NAMES], *[new_v[k] for k in WEIGHT_NAMES])
```

```python
import functools

import jax
import jax.numpy as jnp
from jax import lax
from jax.experimental import pallas as pl
from jax.experimental.pallas import tpu as pltpu

F32 = jnp.float32
BF16 = jnp.bfloat16

HEAD_DIM = 64
GRID_W = 64
AXIS_DIM = HEAD_DIM // 2
ROPE_THETA = 10000.0
N_HEADS = 6
N_KV = 2
N_GROUP = N_HEADS // N_KV
POOL_CH = 64
POOL_WIDTH = 256
POOL_WINDOWS = (2, 4, 8, 16)
WINDOW = 128
Q_BLOCK = 128
Q_WIDTH = N_HEADS * HEAD_DIM
KV_WIDTH = N_KV * HEAD_DIM
GATE_COL = 2 * (Q_WIDTH + 2 * KV_WIDTH) + POOL_WIDTH
U_COL = 2 * (Q_WIDTH + 2 * KV_WIDTH)
EPS = 1e-6
NEG = -1e30
ADAM_LR = 0.001
ADAM_B1 = 0.9
ADAM_B2 = 0.999
ADAM_EPS = 1e-08
ADAM_WD = 0.01
ADAM_STEP = 10

N_CHIPS = 4
LANES = 128
POOL_PAD = 16
VMEM_LIMIT = 48 * 1024 * 1024
MESH = pl.DeviceIdType.MESH
ANY = pl.BlockSpec(memory_space=pl.ANY)


def _params(sem):
    return pltpu.CompilerParams(dimension_semantics=sem, vmem_limit_bytes=VMEM_LIMIT)


def _sds(shape, dtype):
    return jax.ShapeDtypeStruct(tuple(shape), dtype)


class _Opnd:
    def __init__(self, arr, kind="plain", layer=None):
        self.arr, self.kind, self.layer = arr, kind, layer

    @property
    def shape(self):
        a = self.arr
        if self.kind == "plain":
            return a.shape
        if self.kind == "bcols":
            return (a.shape[2], N_CHIPS * a.shape[3])
        return (N_CHIPS * a.shape[2], a.shape[3])

    def spec(self, tr, tc, fn):
        a, layer = self.arr, self.layer
        if self.kind == "plain":
            return pl.BlockSpec((tr, tc), lambda *g: fn(*g))
        if self.kind == "bcols":
            assert a.shape[3] % tc == 0, (a.shape, tc)
            per = a.shape[3] // tc

            def im(*g):
                ri, ci = fn(*g)
                return (ci // per, layer, ri, ci % per)
            return pl.BlockSpec((None, None, tr, tc), im)
        assert a.shape[2] % tr == 0, (a.shape, tr)
        per = a.shape[2] // tr

        def im(*g):
            ri, ci = fn(*g)
            return (ri // per, layer, ri % per, ci)
        return pl.BlockSpec((None, None, tr, tc), im)


def _matmul(a, b, mode, *, tm, tn, tk, name, out_dtypes=(F32,), epilogue=None, extras=(), out_blocked=False):
    if not isinstance(a, _Opnd):
        a = _Opnd(a)
    if not isinstance(b, _Opnd):
        b = _Opnd(b)
    if mode == "nn":
        (M, K), (K2, N) = a.shape, b.shape
        a_spec = a.spec(tm, tk, lambda m, n, k: (m, k))
        b_spec = b.spec(tk, tn, lambda m, n, k: (k, n))
        dims = (((1,), (0,)), ((), ()))
    elif mode == "nt":
        (M, K), (N, K2) = a.shape, b.shape
        a_spec = a.spec(tm, tk, lambda m, n, k: (m, k))
        b_spec = b.spec(tn, tk, lambda m, n, k: (n, k))
        dims = (((1,), (1,)), ((), ()))
    else:
        (K, M), (K2, N) = a.shape, b.shape
        a_spec = a.spec(tk, tm, lambda m, n, k: (k, m))
        b_spec = b.spec(tk, tn, lambda m, n, k: (k, n))
        dims = (((0,), (0,)), ((), ()))
    assert K == K2 and M % tm == 0 and N % tn == 0 and K % tk == 0, (name, M, N, K, K2, tm, tn, tk)
    nk = K // tk
    n_extra = len(extras)
    n_out = len(out_dtypes)
    extra_specs = [pl.BlockSpec(bs, functools.partial(lambda m, n, k, f: f(m, n), f=f)) for (_, bs, f) in extras]
    if out_blocked:
        assert (N // N_CHIPS) % tn == 0
        per = (N // N_CHIPS) // tn
        out_shape = [_sds((N_CHIPS, M, N // N_CHIPS), dt) for dt in out_dtypes]
        out_specs = [pl.BlockSpec((None, tm, tn), lambda m, n, k: (n // per, m, n % per)) for _ in out_dtypes]
    else:
        out_shape = [_sds((M, N), dt) for dt in out_dtypes]
        out_specs = [pl.BlockSpec((tm, tn), lambda m, n, k: (m, n)) for _ in out_dtypes]

    def body(*refs):
        a_ref, b_ref = refs[0], refs[1]
        extra_refs = refs[2:2 + n_extra]
        out_refs = refs[2 + n_extra:2 + n_extra + n_out]
        acc_ref = refs[2 + n_extra + n_out]
        k = pl.program_id(2)
        prod = lax.dot_general(a_ref[...].astype(BF16), b_ref[...].astype(BF16), dims, preferred_element_type=F32)

        def finish(acc):
            outs = epilogue(acc, *[r[...] for r in extra_refs]) if epilogue is not None else (acc,)
            for o_ref, o in zip(out_refs, outs):
                o_ref[...] = o.astype(o_ref.dtype)

        if nk == 1:
            finish(prod)
        else:
            @pl.when(k == 0)
            def _():
                acc_ref[...] = prod

            @pl.when(k > 0)
            def _():
                acc_ref[...] += prod

            @pl.when(k == nk - 1)
            def _():
                finish(acc_ref[...])

    outs = pl.pallas_call(
        body, name=name, grid=(M // tm, N // tn, nk),
        in_specs=[a_spec, b_spec] + extra_specs, out_specs=out_specs, out_shape=out_shape,
        scratch_shapes=[pltpu.VMEM((tm, tn), F32)],
        compiler_params=_params(("parallel", "parallel", "arbitrary")),
    )(a.arr, b.arr, *[e[0] for e in extras])
    return outs[0] if n_out == 1 else outs


def _tile(n, cands):
    for t in cands:
        if n % t == 0:
            return t
    return n


def _grp(i, P):
    return 2 * (i // P) + jnp.minimum(i % P, 1)


def _mod_spec(D, P, part):
    return pl.BlockSpec((1, 1, D), lambda i: (_grp(i, P), 0, part))


def _res_norm(x, pending, modtab, shift_part, scale_part, gain, *, TR, P, name):
    T, D = x.shape
    row = pl.BlockSpec((TR, D), lambda i: (i, 0))
    has_branch = pending is not None
    ins, specs = [x], [row]
    if has_branch:
        branch, gate_tab, gate_part = pending
        ins += [branch, gate_tab]
        specs += [row, _mod_spec(D, P, gate_part)]
    ins += [modtab, modtab, gain]
    specs += [_mod_spec(D, P, shift_part), _mod_spec(D, P, scale_part), pl.BlockSpec((1, D), lambda i: (0, 0))]

    def body(*refs):
        if has_branch:
            x_ref, br_ref, g_ref, sh_ref, sc_ref, gn_ref, xo_ref, h_ref = refs
            xv = x_ref[...] + g_ref[0] * br_ref[...]
        else:
            x_ref, sh_ref, sc_ref, gn_ref, xo_ref, h_ref = refs
            xv = x_ref[...]
        xo_ref[...] = xv
        y = xv * lax.rsqrt(jnp.mean(xv * xv, axis=-1, keepdims=True) + EPS) * gn_ref[...]
        h_ref[...] = (y * (1.0 + sc_ref[0]) + sh_ref[0]).astype(BF16)

    return pl.pallas_call(
        body, name=name, grid=(T // TR,), in_specs=specs, out_specs=[row, row],
        out_shape=[_sds((T, D), F32), _sds((T, D), BF16)], compiler_params=_params(("parallel",)),
    )(*ins)


def _gate_bwd(dx, branch, modtab, gate_part, *, TR, P, name):
    T, D = dx.shape
    G = modtab.shape[0]
    row = pl.BlockSpec((TR, D), lambda i: (i, 0))
    acc = pl.BlockSpec((1, 1, D), lambda i: (_grp(i, P), 0, 0))

    def body(dx_ref, br_ref, g_ref, db_ref, dg_ref):
        r = pl.program_id(0) % P
        dxv = dx_ref[...]
        db_ref[...] = (dxv * g_ref[0]).astype(BF16)
        part = jnp.sum(dxv * br_ref[...], axis=0, keepdims=True)

        @pl.when(r <= 1)
        def _():
            dg_ref[0] = part

        @pl.when(r > 1)
        def _():
            dg_ref[0] += part

    return pl.pallas_call(
        body, name=name, grid=(T // TR,), in_specs=[row, row, _mod_spec(D, P, gate_part)], out_specs=[row, acc],
        out_shape=[_sds((T, D), BF16), _sds((G, 1, D), F32)], compiler_params=_params(("arbitrary",)),
    )(dx, branch, modtab)


def _norm_bwd(x, dh, dres, modtab, scale_part, gain, *, TR, P, name):
    T, D = x.shape
    G = modtab.shape[0]
    row = pl.BlockSpec((TR, D), lambda i: (i, 0))
    acc = pl.BlockSpec((1, 1, D), lambda i: (_grp(i, P), 0, 0))

    def body(x_ref, dh_ref, dres_ref, sc_ref, gn_ref, dx_ref, dsh_ref, dsc_ref, dgn_ref):
        r = pl.program_id(0) % P
        xv, dhv, gn = x_ref[...], dh_ref[...], gn_ref[...]
        rstd = lax.rsqrt(jnp.mean(xv * xv, axis=-1, keepdims=True) + EPS)
        xhat = xv * rstd
        dn = dhv * (1.0 + sc_ref[0])
        dxhat = dn * gn
        dx_ref[...] = dres_ref[...] + rstd * (dxhat - xhat * jnp.mean(dxhat * xhat, axis=-1, keepdims=True))
        p_sh = jnp.sum(dhv, axis=0, keepdims=True)
        p_sc = jnp.sum(dhv * (xhat * gn), axis=0, keepdims=True)
        p_gn = jnp.sum(dn * xhat, axis=0, keepdims=True)

        @pl.when(r <= 1)
        def _():
            dsh_ref[0] = p_sh
            dsc_ref[0] = p_sc
            dgn_ref[0] = p_gn

        @pl.when(r > 1)
        def _():
            dsh_ref[0] += p_sh
            dsc_ref[0] += p_sc
            dgn_ref[0] += p_gn

    return pl.pallas_call(
        body, name=name, grid=(T // TR,),
        in_specs=[row, row, row, _mod_spec(D, P, scale_part), pl.BlockSpec((1, D), lambda i: (0, 0))],
        out_specs=[row, acc, acc, acc],
        out_shape=[_sds((T, D), F32)] + [_sds((G, 1, D), F32)] * 3, compiler_params=_params(("arbitrary",)),
    )(x, dh, dres, modtab, gain)


def _loss_head(x, branch, modtab, gate_part, target, *, TR, P, name):
    T, D = x.shape
    row = pl.BlockSpec((TR, D), lambda i: (i, 0))
    tgt = pl.BlockSpec((TR, D), lambda i: ((i // P) * (P - 1) + jnp.maximum(i % P - 1, 0), 0))
    one = pl.BlockSpec((1, LANES), lambda i: (0, 0))

    def body(x_ref, br_ref, g_ref, t_ref, dy_ref, loss_ref):
        i = pl.program_id(0)
        r = i % P

        @pl.when(i == 0)
        def _():
            loss_ref[...] = jnp.zeros_like(loss_ref)

        @pl.when(r == 0)
        def _():
            dy_ref[...] = jnp.zeros_like(dy_ref)

        @pl.when(r > 0)
        def _():
            err = x_ref[...] + g_ref[0] * br_ref[...] - t_ref[...]
            dy_ref[...] = err / D
            per_tok = jnp.mean(err * err, axis=-1, keepdims=True)
            loss_ref[...] += 0.5 * jnp.sum(per_tok, axis=0, keepdims=True)

    return pl.pallas_call(
        body, name=name, grid=(T // TR,), in_specs=[row, row, _mod_spec(D, P, gate_part), tgt], out_specs=[row, one],
        out_shape=[_sds((T, D), F32), _sds((1, LANES), F32)], compiler_params=_params(("arbitrary",)),
    )(x, branch, modtab, target)


QKV_WIDTH = Q_WIDTH + 2 * KV_WIDTH
QK_NORMED = 4


def _seg_mean(v):
    lane = lax.broadcasted_iota(jnp.int32, v.shape, 1)
    lo = lane < HEAD_DIM
    s0 = jnp.sum(jnp.where(lo, v, 0.0), axis=-1, keepdims=True)
    s1 = jnp.sum(jnp.where(lo, 0.0, v), axis=-1, keepdims=True)
    return jnp.where(lo, s0, s1) * (1.0 / HEAD_DIM)


def _pair_swap(v):
    lane = lax.broadcasted_iota(jnp.int32, v.shape, 1)
    return jnp.where((lane & 1) == 0, pltpu.roll(v, LANES - 1, 1), pltpu.roll(v, 1, 1))


def _chunk(c):
    return slice(c * LANES, (c + 1) * LANES)


def _qk_prep(z, gains, cos, sin, *, TR, P, name):
    T = z.shape[0]

    def body(z_ref, g_ref, c_ref, s_ref, q_ref, k_ref, v_ref):
        cs, sn = c_ref[...], s_ref[...]
        for ch in range(QK_NORMED):
            xv = z_ref[:, _chunk(ch)]
            y = xv * lax.rsqrt(_seg_mean(xv * xv) + EPS) * g_ref[0, :, _chunk(ch)]
            out = (y * cs + _pair_swap(y) * sn).astype(BF16)
            if ch < QK_NORMED - 1:
                q_ref[:, _chunk(ch)] = out
            else:
                k_ref[...] = out
        v_ref[...] = z_ref[:, _chunk(QK_NORMED)].astype(BF16)

    def out(width):
        return pl.BlockSpec((None, TR, width), lambda i, j: (j, i, 0))
    return pl.pallas_call(
        body, name=name, grid=(T // TR, 2),
        in_specs=[pl.BlockSpec((TR, QKV_WIDTH), lambda i, j: (i, j)),
                  pl.BlockSpec((1, 1, QKV_WIDTH), lambda i, j: (j, 0, 0)),
                  pl.BlockSpec((TR, LANES), lambda i, j: (i % P, 0)),
                  pl.BlockSpec((TR, LANES), lambda i, j: (i % P, 0))],
        out_specs=[out(Q_WIDTH), out(KV_WIDTH), out(KV_WIDTH)],
        out_shape=[_sds((2, T, Q_WIDTH), BF16), _sds((2, T, KV_WIDTH), BF16), _sds((2, T, KV_WIDTH), BF16)],
        compiler_params=_params(("parallel", "parallel")),
    )(z, gains, cos, sin)


def _qk_prep_bwd(z, dq, dk, dv, gains, cos, sin, *, branch, TR, P, name):
    T = z.shape[0]
    nt = T // TR

    def body(z_ref, dq_ref, dk_ref, dv_ref, g_ref, c_ref, s_ref, dz_ref, dg_ref):
        i = pl.program_id(0)
        cs, sn = c_ref[...], s_ref[...]
        parts = []
        for ch in range(QK_NORMED):
            xv, g = z_ref[:, _chunk(ch)], g_ref[0, :, _chunk(ch)]
            dout = dq_ref[:, _chunk(ch)] if ch < QK_NORMED - 1 else dk_ref[...]
            dy = dout * cs + _pair_swap(dout * sn)
            rstd = lax.rsqrt(_seg_mean(xv * xv) + EPS)
            xhat = xv * rstd
            dxhat = dy * g
            dz_ref[:, _chunk(ch)] = (rstd * (dxhat - xhat * _seg_mean(dxhat * xhat))).astype(BF16)
            parts.append(jnp.sum(dy * xhat, axis=0, keepdims=True))
        dz_ref[:, _chunk(QK_NORMED)] = dv_ref[...].astype(BF16)
        parts.append(jnp.zeros((1, LANES), F32))
        part = jnp.concatenate(parts, axis=1)

        @pl.when(i == 0)
        def _():
            dg_ref[0] = part

        @pl.when(i > 0)
        def _():
            dg_ref[0] += part

    def rows(width, col=0):
        return pl.BlockSpec((TR, width), lambda i: (i, col))
    return pl.pallas_call(
        body, name=name, grid=(nt,),
        in_specs=[rows(QKV_WIDTH, branch), rows(Q_WIDTH), rows(KV_WIDTH), rows(KV_WIDTH),
                  pl.BlockSpec((1, 1, QKV_WIDTH), lambda i: (branch, 0, 0)),
                  pl.BlockSpec((TR, LANES), lambda i: (i % P, 0)),
                  pl.BlockSpec((TR, LANES), lambda i: (i % P, 0))],
        out_specs=[rows(QKV_WIDTH), pl.BlockSpec((1, 1, QKV_WIDTH), lambda i: (0, 0, 0))],
        out_shape=[_sds((T, QKV_WIDTH), BF16), _sds((1, 1, QKV_WIDTH), F32)],
        compiler_params=_params(("arbitrary",)),
    )(z, dq, dk, dv, gains, cos, sin)


NT_DIMS = (((1,), (1,)), ((), ()))
TN_DIMS = (((0,), (0,)), ((), ()))
QROWS = N_GROUP * Q_BLOCK
SCORE_SCALE = HEAD_DIM ** -0.5
BAND = Q_BLOCK + 2 * WINDOW


def _move_head(block, half_from, half_to):
    lane = lax.broadcasted_iota(jnp.int32, block.shape, 1)
    src = block if half_from == half_to else pltpu.roll(block, HEAD_DIM, 1)
    keep = (lane < HEAD_DIM) if half_to == 0 else (lane >= HEAD_DIM)
    return jnp.where(keep, src, 0.0)


def _stack_heads(ref, j):
    pieces = []
    for h in range(N_GROUP * j, N_GROUP * (j + 1)):
        blk = ref[:, (h // 2) * LANES:(h // 2 + 1) * LANES].astype(F32)
        pieces.append(_move_head(blk, h % 2, j))
    return jnp.concatenate(pieces, axis=0)


def _unstack_heads(stacked, ref):
    heads = []
    for h in range(N_HEADS):
        j, r = h // N_GROUP, h % N_GROUP
        heads.append(_move_head(stacked[j][r * Q_BLOCK:(r + 1) * Q_BLOCK], j, h % 2))
    for m in range(N_HEADS // 2):
        ref[:, m * LANES:(m + 1) * LANES] = (heads[2 * m] + heads[2 * m + 1]).astype(ref.dtype)


def _attn_parts(q, k_ref, i, *, n_ctx, seq, t_all, window):
    def scores(start, size):
        return lax.dot_general(q, k_ref[pl.ds(start, size), :], NT_DIMS, preferred_element_type=F32)

    def ctx_case():
        return [(0, n_ctx, scores(0, n_ctx))]

    def latent_case():
        if not window:
            return [(0, t_all, scores(0, t_all))]
        start = pl.multiple_of(jnp.minimum((i - 1) * Q_BLOCK, t_all - BAND), Q_BLOCK)
        s_loc = scores(start, BAND)
        qpos = (i * Q_BLOCK - n_ctx) + (lax.broadcasted_iota(jnp.int32, (QROWS, BAND), 0) & (Q_BLOCK - 1))
        kpos = (start - n_ctx) + lax.broadcasted_iota(jnp.int32, (QROWS, BAND), 1)
        valid = (kpos - qpos <= WINDOW) & (qpos - kpos <= WINDOW) & (kpos >= 0)
        return [(0, n_ctx, scores(0, n_ctx)), (start, BAND, jnp.where(valid, s_loc, NEG))]

    return ctx_case, latent_case


def _softmax_parts(parts, sink_col):
    m = functools.reduce(jnp.maximum, [jnp.max(s, axis=-1, keepdims=True) for (_, _, s) in parts])
    if sink_col is not None:
        m = jnp.maximum(m, sink_col)
    es = [jnp.exp(s - m) for (_, _, s) in parts]
    l = functools.reduce(jnp.add, [jnp.sum(e, axis=-1, keepdims=True) for e in es])
    if sink_col is not None:
        e_sink = jnp.exp(sink_col - m)
        l = l + e_sink
    inv = 1.0 / l
    return [e * inv for e in es], (e_sink * inv if sink_col is not None else None)


def _sink_column(sink_ref, j):
    r = lax.broadcasted_iota(jnp.int32, (QROWS, 1), 0)
    s0, s1, s2 = sink_ref[j * N_GROUP], sink_ref[j * N_GROUP + 1], sink_ref[j * N_GROUP + 2]
    return jnp.where(r < Q_BLOCK, s0, jnp.where(r < 2 * Q_BLOCK, s1, s2))


def _attn_specs(Tp, branch):
    nq = Tp // Q_BLOCK
    q_in = pl.BlockSpec((None, Q_BLOCK, Q_WIDTH), lambda b, i: (branch, b * nq + i, 0))
    kv_in = pl.BlockSpec((None, Tp, KV_WIDTH), lambda b, i: (branch, b, 0))
    q_out = pl.BlockSpec((Q_BLOCK, Q_WIDTH), lambda b, i: (b * nq + i, 0))
    kv_out = pl.BlockSpec((Tp, KV_WIDTH), lambda b, i: (b, 0))
    return q_in, kv_in, q_out, kv_out


def _attn_fwd(q, k, v, sink, *, branch, B, n_ctx, window, name):
    T = q.shape[1]
    Tp = T // B
    seq = Tp - n_ctx
    has_sink = sink is not None
    q_in, kv_in, q_out, _ = _attn_specs(Tp, branch)

    def body(*refs):
        if has_sink:
            sink_ref, q_ref, k_ref, v_ref, o_ref = refs
        else:
            q_ref, k_ref, v_ref, o_ref = refs
        i = pl.program_id(1)

        def run(latent):
            outs = []
            for j in range(N_KV):
                qv = (_stack_heads(q_ref, j) * SCORE_SCALE).astype(BF16)
                sink_col = _sink_column(sink_ref, j) if has_sink else None
                cases = _attn_parts(qv, k_ref, i, n_ctx=n_ctx, seq=seq, t_all=Tp, window=window)
                parts = cases[latent]()
                probs, _ = _softmax_parts(parts, sink_col)
                o = None
                for (start, size, _), p in zip(parts, probs):
                    t = jnp.dot(p.astype(BF16), v_ref[pl.ds(start, size), :], preferred_element_type=F32)
                    o = t if o is None else o + t
                outs.append(o)
            _unstack_heads(outs, o_ref)

        @pl.when(i < n_ctx // Q_BLOCK)
        def _():
            run(0)

        @pl.when(i >= n_ctx // Q_BLOCK)
        def _():
            run(1)

    ins, specs = [q, k, v], [q_in, kv_in, kv_in]
    if has_sink:
        ins, specs = [sink] + ins, [pl.BlockSpec(memory_space=pltpu.SMEM)] + specs
    return pl.pallas_call(
        body, name=name, grid=(B, Tp // Q_BLOCK), in_specs=specs, out_specs=q_out,
        out_shape=_sds((T, Q_WIDTH), BF16), compiler_params=_params(("parallel", "parallel")),
    )(*ins)


def _attn_bwd(q, k, v, do, sink, *, branch, B, n_ctx, window, name):
    T = q.shape[1]
    Tp = T // B
    seq = Tp - n_ctx
    has_sink = sink is not None
    q_in, kv_in, q_out, kv_out = _attn_specs(Tp, branch)
    sink_spec = pl.BlockSpec((None, 8, LANES), lambda b, i: (b, 0, 0))

    def body(*refs):
        if has_sink:
            sink_ref, q_ref, k_ref, v_ref, do_ref, dq_ref, dk_ref, dv_ref, ds_ref = refs
        else:
            q_ref, k_ref, v_ref, do_ref, dq_ref, dk_ref, dv_ref = refs
        i = pl.program_id(1)

        @pl.when(i == 0)
        def _():
            dk_ref[...] = jnp.zeros_like(dk_ref)
            dv_ref[...] = jnp.zeros_like(dv_ref)
            if has_sink:
                ds_ref[...] = jnp.zeros_like(ds_ref)

        def run(latent):
            dqs = []
            upd = jnp.zeros((8, LANES), F32)
            for j in range(N_KV):
                qv = (_stack_heads(q_ref, j) * SCORE_SCALE).astype(BF16)
                dov = _stack_heads(do_ref, j).astype(BF16)
                sink_col = _sink_column(sink_ref, j) if has_sink else None
                cases = _attn_parts(qv, k_ref, i, n_ctx=n_ctx, seq=seq, t_all=Tp, window=window)
                parts = cases[latent]()
                probs, p_sink = _softmax_parts(parts, sink_col)
                dps = [lax.dot_general(dov, v_ref[pl.ds(start, size), :], NT_DIMS, preferred_element_type=F32)
                       for (start, size, _) in parts]
                delta = functools.reduce(jnp.add, [jnp.sum(p * dp, axis=-1, keepdims=True) for p, dp in zip(probs, dps)])
                dq = None
                for (start, size, _), p, dp in zip(parts, probs, dps):
                    ds = (p * (dp - delta)).astype(BF16)
                    rows = pl.ds(start, size)
                    t = jnp.dot(ds, k_ref[rows, :], preferred_element_type=F32)
                    dq = t if dq is None else dq + t
                    dk_ref[rows, :] += lax.dot_general(ds, qv, TN_DIMS, preferred_element_type=F32)
                    dv_ref[rows, :] += lax.dot_general(p.astype(BF16), dov, TN_DIMS, preferred_element_type=F32)
                dqs.append(dq * SCORE_SCALE)
                if has_sink:
                    contrib = -(p_sink * delta)
                    r = lax.broadcasted_iota(jnp.int32, (QROWS, 1), 0)
                    row8 = lax.broadcasted_iota(jnp.int32, (8, LANES), 0)
                    for h in range(N_GROUP):
                        in_head = (r >= h * Q_BLOCK) & (r < (h + 1) * Q_BLOCK)
                        tot = jnp.sum(jnp.where(in_head, contrib, 0.0), axis=0, keepdims=True)
                        upd = upd + jnp.where(row8 == j * N_GROUP + h, tot, 0.0)
            _unstack_heads(dqs, dq_ref)
            if has_sink:
                ds_ref[...] += upd

        @pl.when(i < n_ctx // Q_BLOCK)
        def _():
            run(0)

        @pl.when(i >= n_ctx // Q_BLOCK)
        def _():
            run(1)

    ins, specs = [q, k, v, do], [q_in, kv_in, kv_in, q_out]
    out_specs = [q_out, kv_out, kv_out]
    out_shape = [_sds((T, Q_WIDTH), F32), _sds((T, KV_WIDTH), F32), _sds((T, KV_WIDTH), F32)]
    if has_sink:
        ins, specs = [sink] + ins, [pl.BlockSpec(memory_space=pltpu.SMEM)] + specs
        out_specs.append(sink_spec)
        out_shape.append(_sds((B, 8, LANES), F32))
    return pl.pallas_call(
        body, name=name, grid=(B, Tp // Q_BLOCK), in_specs=specs, out_specs=out_specs, out_shape=out_shape,
        compiler_params=_params(("parallel", "arbitrary")),
    )(*ins)


def _window_sums(xp):
    n = xp.shape[0]

    def ahead(a, k):
        return pltpu.roll(a, n - k, 0)
    a2 = xp + ahead(xp, 1)
    a4 = a2 + ahead(a2, 2)
    a8 = a4 + ahead(a4, 4)
    a16 = a8 + ahead(a8, 8)
    return (a2, a4, a8, a16)


def _by_group(vals):
    lane = lax.broadcasted_iota(jnp.int32, vals[0].shape, 1)
    return jnp.where(lane < POOL_CH, vals[0], jnp.where(lane < 2 * POOL_CH, vals[1],
                     jnp.where(lane < 3 * POOL_CH, vals[2], vals[3])))


def _pool_counts(n):
    t = lax.broadcasted_iota(jnp.int32, (n, POOL_WIDTH), 0)
    cnts = [(jnp.minimum(t + w // 2, n) - jnp.maximum(t - w // 2, 0)).astype(F32) for w in POOL_WINDOWS]
    return _by_group(cnts)


def _pad_rows(x):
    zeros = jnp.zeros((POOL_PAD, x.shape[1]), x.dtype)
    return jnp.concatenate([zeros, x, zeros], axis=0)


def _pool_stream(u):
    n = u.shape[0]
    sums = _window_sums(_pad_rows(u))
    tots = [pltpu.roll(a, w // 2, 0)[POOL_PAD:POOL_PAD + n] for a, w in zip(sums, POOL_WINDOWS)]
    return _by_group(tots) / _pool_counts(n) - u


def _pool_stream_t(dp):
    n = dp.shape[0]
    sums = _window_sums(_pad_rows(dp / _pool_counts(n)))
    tots = [pltpu.roll(a, w // 2 - 1, 0)[POOL_PAD:POOL_PAD + n] if w > 2 else a[POOL_PAD:POOL_PAD + n]
            for a, w in zip(sums, POOL_WINDOWS)]
    return _by_group(tots) - dp


def _pool_fwd(z, w_bd, scale, *, B, Tp, n_ctx, name):
    T = z.shape[0]
    blk = pl.BlockSpec((Tp, POOL_WIDTH), lambda b: (b, U_COL // POOL_WIDTH))
    out = pl.BlockSpec((Tp, POOL_WIDTH), lambda b: (b, 0))

    def body(u_ref, w_ref, s_ref, p_ref, o_ref):
        for lo, hi in ((0, n_ctx), (n_ctx, Tp)):
            pooled = _pool_stream(u_ref[lo:hi, :]).astype(BF16)
            p_ref[lo:hi, :] = pooled
            mixed = jnp.dot(pooled, w_ref[...], preferred_element_type=F32)
            o_ref[lo:hi, :] = (mixed * s_ref[...]).astype(BF16)

    return pl.pallas_call(
        body, name=name, grid=(B,),
        in_specs=[blk, pl.BlockSpec((POOL_WIDTH, POOL_WIDTH), lambda b: (0, 0)), pl.BlockSpec((1, POOL_WIDTH), lambda b: (0, 0))],
        out_specs=[out, out], out_shape=[_sds((T, POOL_WIDTH), BF16)] * 2, compiler_params=_params(("parallel",)),
    )(z, w_bd, scale)


def _pool_bwd(d_ob, pooled, w_bd, scale, *, B, Tp, n_ctx, name):
    T = d_ob.shape[0]
    blk = pl.BlockSpec((Tp, POOL_WIDTH), lambda b: (b, 0))
    wsp = pl.BlockSpec((POOL_WIDTH, POOL_WIDTH), lambda b: (0, 0))
    ssp = pl.BlockSpec((1, POOL_WIDTH), lambda b: (0, 0))

    def body(d_ref, p_ref, w_ref, s_ref, du_ref, dw_ref, dsc_ref):
        @pl.when(pl.program_id(0) == 0)
        def _():
            dw_ref[...] = jnp.zeros_like(dw_ref)
            dsc_ref[...] = jnp.zeros_like(dsc_ref)

        dv, pv, wv = d_ref[...], p_ref[...], w_ref[...]
        mixed = jnp.dot(pv, wv, preferred_element_type=F32)
        dsc_ref[...] += jnp.sum(dv * mixed, axis=0, keepdims=True)
        dmixed = (dv * s_ref[...]).astype(BF16)
        dw_ref[...] += lax.dot_general(pv, dmixed, TN_DIMS, preferred_element_type=F32)
        dpooled = lax.dot_general(dmixed, wv, NT_DIMS, preferred_element_type=F32)
        for lo, hi in ((0, n_ctx), (n_ctx, Tp)):
            du_ref[lo:hi, :] = _pool_stream_t(dpooled[lo:hi, :]).astype(BF16)

    return pl.pallas_call(
        body, name=name, grid=(B,), in_specs=[blk, blk, wsp, ssp], out_specs=[blk, wsp, ssp],
        out_shape=[_sds((T, POOL_WIDTH), BF16), _sds((POOL_WIDTH, POOL_WIDTH), F32), _sds((1, POOL_WIDTH), F32)],
        compiler_params=_params(("arbitrary",)),
    )(d_ob, pooled, w_bd, scale)


def _merge_specs(z, D, TR, tc, wa, wb, wc):
    def act(width):
        return pl.BlockSpec((TR, width), lambda i, n: (i, 0))

    def gate(part):
        return pl.BlockSpec((TR, tc), lambda i, n: (i, (GATE_COL + part * D) // tc + n))
    w_specs = [w.spec(w.shape[0], tc, lambda i, n: (0, n)) for w in (wa, wb, wc)]
    return [act(Q_WIDTH), act(POOL_WIDTH), act(Q_WIDTH), gate(0), gate(1), gate(2)] + w_specs


def _merge_fwd(oa, ob, oc, z, wa, wb, wc, *, D, TR, name):
    T = oa.shape[0]
    tc = D // N_CHIPS

    def body(oa_ref, ob_ref, oc_ref, ga_ref, gb_ref, gc_ref, wa_ref, wb_ref, wc_ref, y_ref):
        acc = jax.nn.sigmoid(ga_ref[...]) * jnp.dot(oa_ref[...], wa_ref[...], preferred_element_type=F32)
        acc += jax.nn.sigmoid(gb_ref[...]) * jnp.dot(ob_ref[...], wb_ref[...], preferred_element_type=F32)
        acc += jax.nn.sigmoid(gc_ref[...]) * jnp.dot(oc_ref[...], wc_ref[...], preferred_element_type=F32)
        y_ref[...] = acc.astype(BF16)

    return pl.pallas_call(
        body, name=name, grid=(T // TR, D // tc), in_specs=_merge_specs(z, D, TR, tc, wa, wb, wc),
        out_specs=pl.BlockSpec((TR, tc), lambda i, n: (i, n)), out_shape=_sds((T, D), BF16),
        compiler_params=_params(("parallel", "parallel")),
    )(oa, ob, oc, z, z, z, wa.arr, wb.arr, wc.arr)


def _merge_bwd(dy, oa, ob, oc, z, wa, wb, wc, *, D, TR, name):
    T = oa.shape[0]
    tc = D // N_CHIPS
    out = pl.BlockSpec((TR, tc), lambda i, n: (i, n))

    def body(dy_ref, oa_ref, ob_ref, oc_ref, ga_ref, gb_ref, gc_ref, wa_ref, wb_ref, wc_ref,
             dpa_ref, dpb_ref, dpc_ref, dga_ref, dgb_ref, dgc_ref):
        dyv = dy_ref[...]
        for o_ref, g_ref, w_ref, dp_ref, dg_ref in ((oa_ref, ga_ref, wa_ref, dpa_ref, dga_ref),
                                                    (ob_ref, gb_ref, wb_ref, dpb_ref, dgb_ref),
                                                    (oc_ref, gc_ref, wc_ref, dpc_ref, dgc_ref)):
            s = jax.nn.sigmoid(g_ref[...])
            proj = jnp.dot(o_ref[...], w_ref[...], preferred_element_type=F32)
            dp_ref[...] = (dyv * s).astype(BF16)
            dg_ref[...] = (dyv * proj * (s * (1.0 - s))).astype(BF16)

    return pl.pallas_call(
        body, name=name, grid=(T // TR, D // tc), in_specs=[out] + _merge_specs(z, D, TR, tc, wa, wb, wc),
        out_specs=[out] * 6, out_shape=[_sds((T, D), BF16)] * 6, compiler_params=_params(("parallel", "parallel")),
    )(dy, oa, ob, oc, z, z, z, wa.arr, wb.arr, wc.arr)


def _silu_rows(cc, name):
    def body(c_ref, s_ref):
        v = c_ref[...]
        s_ref[...] = (v * jax.nn.sigmoid(v)).astype(BF16)
    return pl.pallas_call(body, name=name, out_shape=_sds(cc.shape, BF16))(cc)


def _ada_bwd_rows(dm, ds, cc, name):
    def body(dm_ref, ds_ref, c_ref, db_ref, dc_ref):
        db_ref[...] = jnp.sum(dm_ref[...], axis=0, keepdims=True)
        v = c_ref[...]
        s = jax.nn.sigmoid(v)
        dc_ref[...] = ds_ref[...] * (s * (1.0 + v * (1.0 - s)))
    return pl.pallas_call(body, name=name, out_shape=[_sds((1, dm.shape[1]), F32), _sds(cc.shape, F32)])(dm, ds, cc)


def _row_tile(rows, cols):
    for t in (512, 256, 128, 64, 32, 16, 8):
        if rows % t == 0 and t * cols * 4 <= (1 << 20):
            return t
    return rows


def _add_own_layer(stacked, landed, core, name):
    _, R, C = stacked.shape
    tr = _row_tile(R, C)

    def body(c_ref, a_ref, b_ref, o_ref, o16_ref):
        tot = a_ref[...] + b_ref[...]
        o_ref[...] = tot
        o16_ref[...] = tot.astype(BF16)

    row = pl.BlockSpec((tr, C), lambda i, c: (i, 0))
    grid_spec = pltpu.PrefetchScalarGridSpec(
        num_scalar_prefetch=1, grid=(R // tr,),
        in_specs=[pl.BlockSpec((None, tr, C), lambda i, c: (c[0], i, 0)), row], out_specs=[row, row])
    return pl.pallas_call(body, name=name, grid_spec=grid_spec, out_shape=[_sds((R, C), F32), _sds((R, C), BF16)],
                          compiler_params=_params(("parallel",)))(core, stacked, landed)


def _sum_chips(own, landed, chip, name):
    _, R, C = own.shape
    tr = _row_tile(R, C)

    def body(k_ref, a_ref, b_ref, o_ref):
        o_ref[...] = ((a_ref[...] + b_ref[0].astype(F32)) + b_ref[1].astype(F32)) + b_ref[2].astype(F32)

    grid_spec = pltpu.PrefetchScalarGridSpec(
        num_scalar_prefetch=1, grid=(R // tr,),
        in_specs=[pl.BlockSpec((None, tr, C), lambda i, k: (k[0], i, 0)), pl.BlockSpec((3, tr, C), lambda i, k: (0, i, 0))],
        out_specs=pl.BlockSpec((tr, C), lambda i, k: (i, 0)))
    return pl.pallas_call(body, name=name, grid_spec=grid_spec, out_shape=_sds((R, C), F32),
                          compiler_params=_params(("parallel",)))(chip, own, landed)


def _adam_math(w, g, m, v):
    m = ADAM_B1 * m + (1.0 - ADAM_B1) * g
    v = ADAM_B2 * v + (1.0 - ADAM_B2) * (g * g)
    m_hat = m / (1.0 - ADAM_B1 ** ADAM_STEP)
    v_hat = v / (1.0 - ADAM_B2 ** ADAM_STEP)
    delta = -ADAM_LR * (m_hat / (jnp.sqrt(v_hat) + ADAM_EPS) + ADAM_WD * w)
    return delta, m, v


def _adamw(w, mine, other, m, v, core, name):
    L, R, C = w.shape
    tr = _row_tile(R, C)

    def body(c_ref, w_ref, a_ref, b_ref, m_ref, v_ref, g_ref, d_ref, mo_ref, vo_ref):
        def step(g):
            d, mn, vn = _adam_math(w_ref[...], g, m_ref[...], v_ref[...])
            g_ref[...] = g
            d_ref[...] = d
            mo_ref[...] = mn
            vo_ref[...] = vn

        @pl.when(pl.program_id(0) == c_ref[0])
        def _():
            step(a_ref[...])

        @pl.when(pl.program_id(0) != c_ref[0])
        def _():
            step(b_ref[...])

    lay = pl.BlockSpec((None, tr, C), lambda l, i, c: (l, i, 0))
    row = pl.BlockSpec((tr, C), lambda l, i, c: (i, 0))
    grid_spec = pltpu.PrefetchScalarGridSpec(num_scalar_prefetch=1, grid=(L, R // tr),
                                             in_specs=[lay, row, row, lay, lay], out_specs=[lay] * 4)
    return pl.pallas_call(body, name=name, grid_spec=grid_spec, out_shape=[_sds((L, R, C), F32)] * 4,
                          compiler_params=_params(("parallel", "parallel")))(core, w, mine, other, m, v)


def _adamw_small(w, parts, m, v, name):
    R, C = w.shape

    def body(w_ref, p_ref, m_ref, v_ref, g_ref, d_ref, mo_ref, vo_ref):
        g = p_ref[0]
        for dev in range(1, 8):
            g = g + p_ref[dev]
        d, mn, vn = _adam_math(w_ref[...], g, m_ref[...], v_ref[...])
        g_ref[...] = g
        d_ref[...] = d
        mo_ref[...] = mn
        vo_ref[...] = vn

    return pl.pallas_call(body, name=name, out_shape=[_sds((R, C), F32)] * 4)(w, parts, m, v)


def _place():
    return lax.axis_index("x"), lax.axis_index("y"), lax.axis_index("c")


def _other_chips(x, y):
    return [(1 - x, y), (x, 1 - y), (1 - x, 1 - y)]


def _rcopy(src, dst, ssem, rsem, dev):
    return pltpu.make_async_remote_copy(src_ref=src, dst_ref=dst, send_sem=ssem, recv_sem=rsem,
                                        device_id=dev, device_id_type=MESH)


def _gather_weights(shards, name):
    n = len(shards)

    def body(*refs):
        src, out = refs[:n], refs[n:2 * n]
        send_sems, recv_sems = refs[2 * n:]
        x, y, c = _place()
        sibling = (x, y, 1 - c)
        chips = _other_chips(x, y)
        mine = 2 * x + y
        first = [_rcopy(src[w].at[c], out[w].at[mine, c], send_sems.at[w, j], recv_sems.at[w, j], (*chip, c))
                 for w in range(n) for j, chip in enumerate(chips)]
        for cp in first:
            cp.start()
        passed = []
        for w in range(n):
            for j, (px, py) in enumerate(chips):
                landed = out[w].at[2 * px + py, c]
                _rcopy(landed, landed, send_sems.at[w, j], recv_sems.at[w, j], (px, py, c)).wait_recv()
                cp = _rcopy(landed, landed, send_sems.at[w, 3 + j], recv_sems.at[w, 3 + j], sibling)
                cp.start()
                passed.append(cp)
        for w in range(n):
            for j, (px, py) in enumerate(chips):
                landed = out[w].at[2 * px + py, 1 - c]
                _rcopy(landed, landed, send_sems.at[w, 3 + j], recv_sems.at[w, 3 + j], sibling).wait_recv()
        for cp in first + passed:
            cp.wait_send()

    landed = pl.pallas_call(
        body, name=name, in_specs=[ANY] * n, out_specs=[ANY] * n,
        out_shape=[_sds((N_CHIPS,) + s.shape, s.dtype) for s in shards],
        scratch_shapes=[pltpu.SemaphoreType.DMA((n, 6)), pltpu.SemaphoreType.DMA((n, 6))],
    )(*shards)
    mine = 2 * lax.axis_index("x") + lax.axis_index("y")
    return [lax.dynamic_update_index_in_dim(g, s, mine, 0) for g, s in zip(landed, shards)]


def _send_other_layer(stacked, name):
    n = len(stacked)

    def body(*refs):
        src, out = refs[:n], refs[n:2 * n]
        send_sems, recv_sems = refs[2 * n:]
        x, y, c = _place()
        cps = [_rcopy(src[w].at[1 - c], out[w], send_sems.at[w], recv_sems.at[w], (x, y, 1 - c)) for w in range(n)]
        for cp in cps:
            cp.start()
        for cp in cps:
            cp.wait_recv()
        for cp in cps:
            cp.wait_send()

    return pl.pallas_call(
        body, name=name, in_specs=[ANY] * n, out_specs=[ANY] * n,
        out_shape=[_sds(s.shape[1:], s.dtype) for s in stacked],
        scratch_shapes=[pltpu.SemaphoreType.DMA((n,)), pltpu.SemaphoreType.DMA((n,))],
    )(*stacked)


def _send_chip_blocks(blocked, name):
    n = len(blocked)

    def body(*refs):
        src, out = refs[:n], refs[n:2 * n]
        send_sems, recv_sems = refs[2 * n:]
        x, y, c = _place()
        cps = [_rcopy(src[w].at[2 * px + py], out[w].at[j], send_sems.at[w, j], recv_sems.at[w, j], (px, py, c))
               for w in range(n) for j, (px, py) in enumerate(_other_chips(x, y))]
        for cp in cps:
            cp.start()
        for cp in cps:
            cp.wait_recv()
        for cp in cps:
            cp.wait_send()

    return pl.pallas_call(
        body, name=name, in_specs=[ANY] * n, out_specs=[ANY] * n,
        out_shape=[_sds((3,) + s.shape[1:], s.dtype) for s in blocked],
        scratch_shapes=[pltpu.SemaphoreType.DMA((n, 3)), pltpu.SemaphoreType.DMA((n, 3))],
    )(*blocked)


def _share_layers(reduced, name):
    n = len(reduced)

    def body(*refs):
        src, out = refs[:n], refs[n:2 * n]
        send_sems, recv_sems = refs[2 * n:]
        x, y, c = _place()
        cps = [_rcopy(src[w], out[w], send_sems.at[w], recv_sems.at[w], (x, y, 1 - c)) for w in range(n)]
        for cp in cps:
            cp.start()
        for cp in cps:
            cp.wait_recv()
        for cp in cps:
            cp.wait_send()

    return pl.pallas_call(
        body, name=name, in_specs=[ANY] * n, out_specs=[ANY] * n,
        out_shape=[_sds(s.shape, s.dtype) for s in reduced],
        scratch_shapes=[pltpu.SemaphoreType.DMA((n,)), pltpu.SemaphoreType.DMA((n,))],
    )(*reduced)


def _gather_small(block, name):
    m_per, n = block.shape

    def body(x_ref, out_ref, send_sems, recv_sems, local_sem):
        x, y, c = _place()
        me, sibling = (x, y, c), (x, y, 1 - c)
        chips = _other_chips(x, y)

        def rows(px, py, pc):
            return out_ref.at[pl.ds((4 * px + 2 * py + pc) * m_per, m_per), :]

        def copy(k, blk, to, src=None):
            return _rcopy(rows(*blk) if src is None else src, rows(*blk), send_sems.at[k], recv_sems.at[k], to)

        mine = pltpu.make_async_copy(x_ref, rows(*me), local_sem)
        mine.start()
        first = [copy(0, me, sibling, src=x_ref)]
        first += [copy(1 + j, me, (*chip, c), src=x_ref) for j, chip in enumerate(chips)]
        for cp in first:
            cp.start()
        passed = [copy(4 + j, (*chip, c), sibling) for j, chip in enumerate(chips)]
        for j, chip in enumerate(chips):
            copy(1 + j, (*chip, c), me).wait_recv()
            passed[j].start()
        copy(0, sibling, me).wait_recv()
        for j, chip in enumerate(chips):
            copy(4 + j, (*chip, 1 - c), me).wait_recv()
        for cp in first + passed:
            cp.wait_send()
        mine.wait()

    return pl.pallas_call(
        body, name=name, out_shape=_sds((8 * m_per, n), block.dtype),
        in_specs=[pl.BlockSpec(memory_space=pltpu.VMEM)], out_specs=pl.BlockSpec(memory_space=pltpu.VMEM),
        scratch_shapes=[pltpu.SemaphoreType.DMA((7,)), pltpu.SemaphoreType.DMA((7,)), pltpu.SemaphoreType.DMA],
    )(block)


def _rope_tables(n_ctx, seq):
    rows = seq // GRID_W
    r = jnp.repeat(jnp.arange(rows, dtype=F32), GRID_W)
    col = jnp.tile(jnp.arange(GRID_W, dtype=F32), rows)
    inv = 1.0 / (ROPE_THETA ** (jnp.arange(0, AXIS_DIM, 2, dtype=F32) / AXIS_DIM))
    ang = jnp.concatenate([r[:, None] * inv, col[:, None] * inv], axis=-1)
    cos = jnp.repeat(jnp.cos(ang), 2, axis=-1)
    sin = jnp.repeat(jnp.sin(ang), 2, axis=-1) * jnp.tile(jnp.array([-1.0, 1.0], F32), HEAD_DIM // 2)
    cos = jnp.concatenate([jnp.ones((n_ctx, HEAD_DIM), F32), cos], axis=0)
    sin = jnp.concatenate([jnp.zeros((n_ctx, HEAD_DIM), F32), sin], axis=0)
    return jnp.tile(cos, (1, 2)), jnp.tile(sin, (1, 2))


def _block_diag(w_pool_l):
    out = jnp.zeros((POOL_WIDTH, POOL_WIDTH), w_pool_l.dtype)
    for g in range(w_pool_l.shape[0]):
        out = out.at[g * POOL_CH:(g + 1) * POOL_CH, g * POOL_CH:(g + 1) * POOL_CH].set(w_pool_l[g])
    return out


def _local_step(x, c, ctx, c_ctx, small, gw, target):
    B, S, D = x.shape
    N = ctx.shape[1]
    L = small["norm1"].shape[0]
    Tp = N + S
    T = B * Tp
    TR = N
    P = Tp // N
    rows16 = 16
    assert N % Q_BLOCK == 0 and S % N == 0 and B + 1 <= rows16
    TM = _tile(T, (1024, 768, 512, 384, 256, 128))
    TMG = _tile(T, (512, 384, 256, 128))

    X = jnp.concatenate([ctx, x], axis=1).reshape(T, D)
    cc = jnp.zeros((rows16, D), F32).at[:B].set(c).at[B].set(c_ctx)
    s_rows = _silu_rows(cc, "silu_rows")
    cos, sin = _rope_tables(N, S)

    def weights(l):
        return dict(
            ada=_Opnd(gw["w_ada"], "bcols", l), w_in=_Opnd(gw["w_in"], "bcols", l),
            a=_Opnd(gw["w_br_a"], "bcols", l), b=_Opnd(gw["w_br_b"], "bcols", l), c=_Opnd(gw["w_br_c"], "bcols", l),
            out=_Opnd(gw["w_out"], "brows", l), mlp1=_Opnd(gw["w_mlp1"], "bcols", l), mlp2=_Opnd(gw["w_mlp2"], "brows", l))

    IN = weights(0)["w_in"].shape[1]
    DFF = weights(0)["mlp1"].shape[1]
    tn_in = _tile(IN // N_CHIPS, (1152, 768, 512, 384, 256, 128))
    tn_ff = _tile(DFF // N_CHIPS, (1024, 512, 256, 128))
    tn_ada = _tile(6 * D // N_CHIPS, (1536, 768, 512, 256, 128))
    tn_d = D // N_CHIPS
    tk_d = _tile(D, (1024, 512))
    tk_tok = _tile(T, (512, 384, 256))

    saved = []
    xin, pending = X, None
    for l in range(L):
        W = weights(l)
        b_ada = small["b_ada"][l].reshape(1, 6 * D)
        mod = _matmul(s_rows, W["ada"], "nn", tm=rows16, tn=tn_ada, tk=D, name=f"ada_fwd{l}",
                      epilogue=lambda acc, b: (acc + b,), extras=[(b_ada, (1, tn_ada), lambda m, n: (0, n))])
        modtab = jnp.stack([jnp.broadcast_to(mod[B], (B, 6 * D)), mod[:B]], axis=1).reshape(2 * B, 1, 6 * D)
        ones = jnp.ones((LANES,), F32)
        gains = jnp.stack([
            jnp.concatenate([jnp.tile(small[q][l], 2)] * 3 + [jnp.tile(small[k][l], 2), ones])
            for q, k in (("q_norm_a", "k_norm_a"), ("q_norm_c", "k_norm_c"))]).reshape(2, 1, QKV_WIDTH)
        w_bd = _block_diag(small["w_pool"][l]).astype(BF16)
        p_scale = small["pool_scale"][l].reshape(1, POOL_WIDTH)
        sink = small["sink_c"][l]

        x0, h1 = _res_norm(xin, pending, modtab, 0, 1, small["norm1"][l][None], TR=TR, P=P, name=f"norm1_fwd{l}")
        z = _matmul(h1, W["w_in"], "nn", tm=TM, tn=tn_in, tk=D, name=f"in_proj{l}")
        q2, k2, v2 = _qk_prep(z, gains, cos, sin, TR=TR, P=P, name=f"qk_prep{l}")
        oa = _attn_fwd(q2, k2, v2, None, branch=0, B=B, n_ctx=N, window=False, name=f"attn_a_fwd{l}")
        oc = _attn_fwd(q2, k2, v2, sink, branch=1, B=B, n_ctx=N, window=True, name=f"attn_c_fwd{l}")
        pooled, ob = _pool_fwd(z, w_bd, p_scale, B=B, Tp=Tp, n_ctx=N, name=f"pool_fwd{l}")
        y = _merge_fwd(oa, ob, oc, z, W["a"], W["b"], W["c"], D=D, TR=TMG, name=f"merge_fwd{l}")
        ao = _matmul(y, W["out"], "nn", tm=TM, tn=D, tk=tn_d, name=f"out_proj{l}")
        x1, h2 = _res_norm(x0, (ao, modtab, 2), modtab, 3, 4, small["norm2"][l][None], TR=TR, P=P, name=f"norm2_fwd{l}")
        a_pre, r_act = _matmul(h2, W["mlp1"], "nn", tm=TM, tn=tn_ff, tk=D, name=f"mlp1_fwd{l}", out_dtypes=(F32, BF16),
                               epilogue=lambda acc: (acc, jnp.square(jnp.maximum(acc, 0.0))))
        mo = _matmul(r_act, W["mlp2"], "nn", tm=TM, tn=D, tk=tn_ff, name=f"mlp2_fwd{l}")
        saved.append(dict(modtab=modtab, gains=gains, w_bd=w_bd, p_scale=p_scale, sink=sink, x0=x0, h1=h1, z=z,
                          q2=q2, k2=k2, v2=v2, oa=oa, ob=ob, oc=oc, pooled=pooled, y=y, ao=ao,
                          x1=x1, h2=h2, a_pre=a_pre, r_act=r_act, mo=mo))
        xin, pending = x1, (mo, modtab, 5)

    dxo, loss = _loss_head(xin, pending[0], pending[1], 5, target.reshape(B * S, D), TR=TR, P=P, name="loss_head")

    big = {k: [None] * L for k in gw}
    sm = {k: [None] * L for k in ("b_ada", "norm1", "norm2", "q_norm_a", "k_norm_a", "q_norm_c", "k_norm_c",
                                   "sink_c", "w_pool", "pool_scale")}
    d_cctx = jnp.zeros((D,), F32)
    for l in reversed(range(L)):
        W, sv = weights(l), saved[l]
        modtab = sv["modtab"]
        d_mo, dg2 = _gate_bwd(dxo, sv["mo"], modtab, 5, TR=TR, P=P, name=f"gate2_bwd{l}")
        d_a = _matmul(d_mo, W["mlp2"], "nt", tm=TM, tn=tn_ff, tk=D, name=f"mlp2_bwd{l}", out_dtypes=(BF16,),
                      epilogue=lambda acc, a: (acc * (2.0 * jnp.maximum(a, 0.0)),),
                      extras=[(sv["a_pre"], (TM, tn_ff), lambda m, n: (m, n))])
        big["w_mlp2"][l] = _matmul(sv["r_act"], d_mo, "tn", tm=tn_ff, tn=D, tk=tk_tok,
                                   name=f"mlp2_dw{l}").reshape(N_CHIPS, DFF // N_CHIPS, D)
        d_h2 = _matmul(d_a, W["mlp1"], "nt", tm=TM, tn=D, tk=tn_ff, name=f"mlp1_bwd{l}")
        big["w_mlp1"][l] = _matmul(sv["h2"], d_a, "tn", tm=tk_d, tn=tn_ff, tk=tk_tok, name=f"mlp1_dw{l}", out_blocked=True)
        dx1, dsh2, dsc2, dn2 = _norm_bwd(sv["x1"], d_h2, dxo, modtab, 4, small["norm2"][l][None], TR=TR, P=P,
                                         name=f"norm2_bwd{l}")
        d_ao, dg1 = _gate_bwd(dx1, sv["ao"], modtab, 2, TR=TR, P=P, name=f"gate1_bwd{l}")
        d_y = _matmul(d_ao, W["out"], "nt", tm=TM, tn=tn_d, tk=D, name=f"out_bwd{l}")
        big["w_out"][l] = _matmul(sv["y"], d_ao, "tn", tm=tk_d, tn=D, tk=tk_tok,
                                  name=f"out_dw{l}").reshape(N_CHIPS, D // N_CHIPS, D)
        d_pa, d_pb, d_pc, d_ga, d_gb, d_gc = _merge_bwd(d_y, sv["oa"], sv["ob"], sv["oc"], sv["z"], W["a"], W["b"], W["c"],
                                                        D=D, TR=TMG, name=f"merge_bwd{l}")
        d_oa = _matmul(d_pa, W["a"], "nt", tm=TM, tn=Q_WIDTH, tk=tn_d, name=f"br_a_bwd{l}", out_dtypes=(BF16,))
        d_ob = _matmul(d_pb, W["b"], "nt", tm=TM, tn=POOL_WIDTH, tk=tn_d, name=f"br_b_bwd{l}")
        d_oc = _matmul(d_pc, W["c"], "nt", tm=TM, tn=Q_WIDTH, tk=tn_d, name=f"br_c_bwd{l}", out_dtypes=(BF16,))
        big["w_br_a"][l] = _matmul(sv["oa"], d_pa, "tn", tm=Q_WIDTH, tn=tn_d, tk=tk_tok, name=f"br_a_dw{l}", out_blocked=True)
        big["w_br_b"][l] = _matmul(sv["ob"], d_pb, "tn", tm=POOL_WIDTH, tn=tn_d, tk=tk_tok, name=f"br_b_dw{l}", out_blocked=True)
        big["w_br_c"][l] = _matmul(sv["oc"], d_pc, "tn", tm=Q_WIDTH, tn=tn_d, tk=tk_tok, name=f"br_c_dw{l}", out_blocked=True)
        d_u, d_wbd, d_ps = _pool_bwd(d_ob, sv["pooled"], sv["w_bd"], sv["p_scale"], B=B, Tp=Tp, n_ctx=N, name=f"pool_bwd{l}")
        dqa, dka, dva = _attn_bwd(sv["q2"], sv["k2"], sv["v2"], d_oa, None, branch=0, B=B, n_ctx=N, window=False,
                                  name=f"attn_a_bwd{l}")
        dqc, dkc, dvc, dsink = _attn_bwd(sv["q2"], sv["k2"], sv["v2"], d_oc, sv["sink"], branch=1, B=B, n_ctx=N,
                                         window=True, name=f"attn_c_bwd{l}")
        dz_a, dgains_a = _qk_prep_bwd(sv["z"], dqa, dka, dva, sv["gains"], cos, sin, branch=0, TR=TR, P=P,
                                      name=f"qk_prep_a_bwd{l}")
        dz_c, dgains_c = _qk_prep_bwd(sv["z"], dqc, dkc, dvc, sv["gains"], cos, sin, branch=1, TR=TR, P=P,
                                      name=f"qk_prep_c_bwd{l}")
        dz = jnp.concatenate([dz_a, dz_c, d_u, d_ga, d_gb, d_gc], axis=1)
        d_h1 = _matmul(dz, W["w_in"], "nt", tm=TM, tn=D, tk=tn_in, name=f"in_bwd{l}")
        big["w_in"][l] = _matmul(sv["h1"], dz, "tn", tm=tk_d, tn=tn_in, tk=tk_tok, name=f"in_dw{l}", out_blocked=True)
        dx0, dsh1, dsc1, dn1 = _norm_bwd(sv["x0"], d_h1, dx1, modtab, 1, small["norm1"][l][None], TR=TR, P=P,
                                         name=f"norm1_bwd{l}")

        dm_groups = jnp.concatenate([dsh1, dsc1, dg1, dsh2, dsc2, dg2], axis=-1).reshape(B, 2, 6 * D)
        dm = jnp.zeros((rows16, 6 * D), F32).at[:B].set(dm_groups[:, 1]).at[B].set(jnp.sum(dm_groups[:, 0], axis=0))
        dm_bf = dm.astype(BF16)
        d_s = _matmul(dm_bf, W["ada"], "nt", tm=rows16, tn=D, tk=tn_ada, name=f"ada_bwd{l}")
        big["w_ada"][l] = _matmul(s_rows, dm_bf, "tn", tm=tk_d, tn=tn_ada, tk=rows16, name=f"ada_dw{l}", out_blocked=True)
        db_ada, dcc = _ada_bwd_rows(dm, d_s, cc, f"ada_rows_bwd{l}")
        d_cctx = d_cctx + dcc[B]

        sm["b_ada"][l] = db_ada[0]
        sm["norm1"][l] = jnp.sum(dn1, axis=(0, 1))
        sm["norm2"][l] = jnp.sum(dn2, axis=(0, 1))
        dgh = jnp.stack([dgains_a, dgains_c]).reshape(2, QKV_WIDTH // HEAD_DIM, HEAD_DIM)
        sm["q_norm_a"][l] = jnp.sum(dgh[0, :N_HEADS], axis=0)
        sm["k_norm_a"][l] = jnp.sum(dgh[0, N_HEADS:N_HEADS + N_KV], axis=0)
        sm["q_norm_c"][l] = jnp.sum(dgh[1, :N_HEADS], axis=0)
        sm["k_norm_c"][l] = jnp.sum(dgh[1, N_HEADS:N_HEADS + N_KV], axis=0)
        sm["sink_c"][l] = jnp.sum(dsink[:, :N_HEADS, 0], axis=0)
        sm["w_pool"][l] = jnp.stack([d_wbd[g * POOL_CH:(g + 1) * POOL_CH, g * POOL_CH:(g + 1) * POOL_CH]
                                     for g in range(POOL_WIDTH // POOL_CH)])
        sm["pool_scale"][l] = d_ps[0]
        dxo = dx0

    grad_x = dxo.reshape(B, Tp, D)[:, N:]
    small_grads = {k: jnp.stack(v) for k, v in sm.items()}
    small_grads["c_ctx"] = d_cctx
    return loss, grad_x, small_grads, big


SMALL_NAMES = ("c_ctx", "b_ada", "norm1", "norm2", "q_norm_a", "k_norm_a", "q_norm_c", "k_norm_c", "sink_c",
               "w_pool", "pool_scale")
BIG_NAMES = ("w_ada", "w_in", "w_br_a", "w_br_b", "w_br_c", "w_out", "w_mlp1", "w_mlp2")
WEIGHT_NAMES = ("c_ctx", "w_ada", "b_ada", "norm1", "norm2", "w_in", "q_norm_a", "k_norm_a", "q_norm_c", "k_norm_c",
                "sink_c", "w_pool", "pool_scale", "w_br_a", "w_br_b", "w_br_c", "w_out", "w_mlp1", "w_mlp2")


def _pack(parts, rows):
    flat = jnp.concatenate([p.reshape(-1).astype(F32) for p in parts])
    return jnp.pad(flat, (0, rows * LANES - flat.shape[0])).reshape(rows, LANES)


def _unpack(packed, like):
    flat, out, at = packed.reshape(-1), [], 0
    for p in like:
        out.append(flat[at:at + p.size].reshape(p.shape))
        at += p.size
    return out


def _reduce_big(stacked):
    names = list(stacked)
    x, y, c = _place()
    core = c.astype(jnp.int32).reshape(1)
    chip = (2 * x + y).astype(jnp.int32).reshape(1)
    flat = [stacked[k].reshape(stacked[k].shape[0], -1, stacked[k].shape[-1]) for k in names]
    landed = _send_other_layer(flat, "grads_to_sibling")
    in_chip = [_add_own_layer(f, r, core, f"grads_add_sibling_{k}") for k, f, r in zip(names, flat, landed)]
    blocked = [h.reshape(stacked[k].shape[1:]) for k, (h, _) in zip(names, in_chip)]
    blocked16 = [h.reshape(stacked[k].shape[1:]) for k, (_, h) in zip(names, in_chip)]
    from_chips = _send_chip_blocks(blocked16, "grads_to_chips")
    reduced = [_sum_chips(h, r, chip, f"grads_sum_chips_{k}") for k, h, r in zip(names, blocked, from_chips)]
    shared = _share_layers(reduced, "grads_share_layers")
    return core, dict(zip(names, zip(reduced, shared)))


def kernel(x, c, ctx, c_ctx, w_ada, b_ada, norm1, norm2, w_in, q_norm_a, k_norm_a, q_norm_c, k_norm_c, sink_c, w_pool, pool_scale, w_br_a, w_br_b, w_br_c, w_out, w_mlp1, w_mlp2, loss_target, m_c_ctx, m_w_ada, m_b_ada, m_norm1, m_norm2, m_w_in, m_q_norm_a, m_k_norm_a, m_q_norm_c, m_k_norm_c, m_sink_c, m_w_pool, m_pool_scale, m_w_br_a, m_w_br_b, m_w_br_c, m_w_out, m_w_mlp1, m_w_mlp2, v_c_ctx, v_w_ada, v_b_ada, v_norm1, v_norm2, v_w_in, v_q_norm_a, v_k_norm_a, v_q_norm_c, v_k_norm_c, v_sink_c, v_w_pool, v_pool_scale, v_w_br_a, v_w_br_b, v_w_br_c, v_w_out, v_w_mlp1, v_w_mlp2):
    given = dict(locals())
    w = {k: given[k] for k in WEIGHT_NAMES}
    m = {k: given["m_" + k] for k in WEIGHT_NAMES}
    v = {k: given["v_" + k] for k in WEIGHT_NAMES}

    gathered = _gather_weights([w[k].astype(BF16) for k in BIG_NAMES], "gather_weights")
    gw = dict(zip(BIG_NAMES, gathered))
    small = {k: w[k] for k in SMALL_NAMES}
    loss_part, grad_x, small_grads, big_grads = _local_step(x, c, ctx, c_ctx, small, gw, loss_target)

    core, reduced = _reduce_big({k: jnp.stack(big_grads[k]) for k in BIG_NAMES})
    grads, deltas, new_m, new_v = {}, {}, {}, {}
    for k in BIG_NAMES:
        mine, other = reduced[k]
        grads[k], deltas[k], new_m[k], new_v[k] = _adamw(w[k], mine, other, m[k], v[k], core, f"adamw_{k}")

    sizes = sum(w[k].size for k in SMALL_NAMES) + LANES
    rows = -(-sizes // (8 * LANES)) * 8
    parts = _gather_small(_pack([small_grads[k] for k in SMALL_NAMES] + [loss_part[0]], rows), "gather_small")
    zero = jnp.zeros((LANES,), F32)
    packed = [_pack([t[k] for k in SMALL_NAMES] + [zero], rows) for t in (w, m, v)]
    outs = _adamw_small(packed[0], parts.reshape(8, rows, LANES), packed[1], packed[2], "adamw_small")
    like = [w[k] for k in SMALL_NAMES] + [zero]
    for store, packed_out in zip((grads, deltas, new_m, new_v), outs):
        pieces = _unpack(packed_out, like)
        for k, piece in zip(SMALL_NAMES, pieces):
            store[k] = piece
        if store is grads:
            loss = pieces[-1][0]

    return (loss, grad_x, *[grads[k] for k in WEIGHT_NAMES], *[deltas[k] for k in WEIGHT_NAMES],
            *[new_m[k] for k in WEIGHT_NAMES], *[new_v[k] for k in WEIGHT_NAMES])
```

```python
import functools

import jax
import jax.numpy as jnp
from jax import lax
from jax.experimental import pallas as pl
from jax.experimental.pallas import tpu as pltpu

F32 = jnp.float32
BF16 = jnp.bfloat16

HEAD_DIM = 64
GRID_W = 64
AXIS_DIM = HEAD_DIM // 2
ROPE_THETA = 10000.0
N_HEADS = 6
N_KV = 2
N_GROUP = N_HEADS // N_KV
POOL_CH = 64
POOL_WIDTH = 256
POOL_WINDOWS = (2, 4, 8, 16)
WINDOW = 128
Q_BLOCK = 128
Q_WIDTH = N_HEADS * HEAD_DIM
KV_WIDTH = N_KV * HEAD_DIM
GATE_COL = 2 * (Q_WIDTH + 2 * KV_WIDTH) + POOL_WIDTH
U_COL = 2 * (Q_WIDTH + 2 * KV_WIDTH)
EPS = 1e-6
NEG = -1e30
ADAM_LR = 0.001
ADAM_B1 = 0.9
ADAM_B2 = 0.999
ADAM_EPS = 1e-08
ADAM_WD = 0.01
ADAM_STEP = 10

N_CHIPS = 4
LANES = 128
POOL_PAD = 16
VMEM_LIMIT = 48 * 1024 * 1024
MESH = pl.DeviceIdType.MESH
ANY = pl.BlockSpec(memory_space=pl.ANY)


def _params(sem):
    return pltpu.CompilerParams(dimension_semantics=sem, vmem_limit_bytes=VMEM_LIMIT)


def _sds(shape, dtype):
    return jax.ShapeDtypeStruct(tuple(shape), dtype)


class _Opnd:
    def __init__(self, arr, kind="plain", layer=None):
        self.arr, self.kind, self.layer = arr, kind, layer

    @property
    def shape(self):
        a = self.arr
        if self.kind == "plain":
            return a.shape
        if self.kind == "bcols":
            return (a.shape[2], N_CHIPS * a.shape[3])
        return (N_CHIPS * a.shape[2], a.shape[3])

    def spec(self, tr, tc, fn):
        a, layer = self.arr, self.layer
        if self.kind == "plain":
            return pl.BlockSpec((tr, tc), lambda *g: fn(*g))
        if self.kind == "bcols":
            assert a.shape[3] % tc == 0, (a.shape, tc)
            per = a.shape[3] // tc

            def im(*g):
                ri, ci = fn(*g)
                return (ci // per, layer, ri, ci % per)
            return pl.BlockSpec((None, None, tr, tc), im)
        assert a.shape[2] % tr == 0, (a.shape, tr)
        per = a.shape[2] // tr

        def im(*g):
            ri, ci = fn(*g)
            return (ri // per, layer, ri % per, ci)
        return pl.BlockSpec((None, None, tr, tc), im)


def _matmul(a, b, mode, *, tm, tn, tk, name, out_dtypes=(F32,), epilogue=None, extras=(), out_blocked=False):
    if not isinstance(a, _Opnd):
        a = _Opnd(a)
    if not isinstance(b, _Opnd):
        b = _Opnd(b)
    if mode == "nn":
        (M, K), (K2, N) = a.shape, b.shape
        a_spec = a.spec(tm, tk, lambda m, n, k: (m, k))
        b_spec = b.spec(tk, tn, lambda m, n, k: (k, n))
        dims = (((1,), (0,)), ((), ()))
    elif mode == "nt":
        (M, K), (N, K2) = a.shape, b.shape
        a_spec = a.spec(tm, tk, lambda m, n, k: (m, k))
        b_spec = b.spec(tn, tk, lambda m, n, k: (n, k))
        dims = (((1,), (1,)), ((), ()))
    else:
        (K, M), (K2, N) = a.shape, b.shape
        a_spec = a.spec(tk, tm, lambda m, n, k: (k, m))
        b_spec = b.spec(tk, tn, lambda m, n, k: (k, n))
        dims = (((0,), (0,)), ((), ()))
    assert K == K2 and M % tm == 0 and N % tn == 0 and K % tk == 0, (name, M, N, K, K2, tm, tn, tk)
    nk = K // tk
    n_extra = len(extras)
    n_out = len(out_dtypes)
    extra_specs = [pl.BlockSpec(bs, functools.partial(lambda m, n, k, f: f(m, n), f=f)) for (_, bs, f) in extras]
    if out_blocked:
        assert (N // N_CHIPS) % tn == 0
        per = (N // N_CHIPS) // tn
        out_shape = [_sds((N_CHIPS, M, N // N_CHIPS), dt) for dt in out_dtypes]
        out_specs = [pl.BlockSpec((None, tm, tn), lambda m, n, k: (n // per, m, n % per)) for _ in out_dtypes]
    else:
        out_shape = [_sds((M, N), dt) for dt in out_dtypes]
        out_specs = [pl.BlockSpec((tm, tn), lambda m, n, k: (m, n)) for _ in out_dtypes]

    in_place = nk > 1 and epilogue is None and tuple(out_dtypes) == (F32,)

    def body(*refs):
        a_ref, b_ref = refs[0], refs[1]
        extra_refs = refs[2:2 + n_extra]
        out_refs = refs[2 + n_extra:2 + n_extra + n_out]
        acc_ref = out_refs[0] if in_place else (refs[2 + n_extra + n_out] if nk > 1 else None)
        k = pl.program_id(2)
        prod = lax.dot_general(a_ref[...].astype(BF16), b_ref[...].astype(BF16), dims, preferred_element_type=F32)

        def finish(acc):
            outs = epilogue(acc, *[r[...] for r in extra_refs]) if epilogue is not None else (acc,)
            for o_ref, o in zip(out_refs, outs):
                o_ref[...] = o.astype(o_ref.dtype)

        if nk == 1:
            finish(prod)
        elif in_place:
            @pl.when(k == 0)
            def _():
                acc_ref[...] = prod

            @pl.when(k > 0)
            def _():
                acc_ref[...] += prod
        else:
            @pl.when(k == 0)
            def _():
                acc_ref[...] = prod

            @pl.when(k > 0)
            def _():
                acc_ref[...] += prod

            @pl.when(k == nk - 1)
            def _():
                finish(acc_ref[...])

    outs = pl.pallas_call(
        body, name=name, grid=(M // tm, N // tn, nk),
        in_specs=[a_spec, b_spec] + extra_specs, out_specs=out_specs, out_shape=out_shape,
        scratch_shapes=[pltpu.VMEM((tm, tn), F32)] if nk > 1 and not in_place else [],
        compiler_params=_params(("parallel", "parallel", "arbitrary")),
    )(a.arr, b.arr, *[e[0] for e in extras])
    return outs[0] if n_out == 1 else outs


def _tile(n, cands):
    for t in cands:
        if n % t == 0:
            return t
    return n


def _grp(i, P):
    return 2 * (i // P) + jnp.minimum(i % P, 1)


def _mod_spec(D, P, part):
    return pl.BlockSpec((1, 1, D), lambda i: (_grp(i, P), 0, part))


def _res_norm(x, pending, modtab, shift_part, scale_part, gain, *, TR, P, name):
    T, D = x.shape
    row = pl.BlockSpec((TR, D), lambda i: (i, 0))
    has_branch = pending is not None
    ins, specs = [x], [row]
    if has_branch:
        branch, gate_tab, gate_part = pending
        ins += [branch, gate_tab]
        specs += [row, _mod_spec(D, P, gate_part)]
    ins += [modtab, modtab, gain]
    specs += [_mod_spec(D, P, shift_part), _mod_spec(D, P, scale_part), pl.BlockSpec((1, D), lambda i: (0, 0))]

    def body(*refs):
        if has_branch:
            x_ref, br_ref, g_ref, sh_ref, sc_ref, gn_ref, xo_ref, h_ref = refs
            xv = x_ref[...] + g_ref[0] * br_ref[...]
        else:
            x_ref, sh_ref, sc_ref, gn_ref, xo_ref, h_ref = refs
            xv = x_ref[...]
        xo_ref[...] = xv
        y = xv * lax.rsqrt(jnp.mean(xv * xv, axis=-1, keepdims=True) + EPS) * gn_ref[...]
        h_ref[...] = (y * (1.0 + sc_ref[0]) + sh_ref[0]).astype(BF16)

    return pl.pallas_call(
        body, name=name, grid=(T // TR,), in_specs=specs, out_specs=[row, row],
        out_shape=[_sds((T, D), F32), _sds((T, D), BF16)], compiler_params=_params(("parallel",)),
    )(*ins)


def _gate_bwd(dx, branch, modtab, gate_part, *, TR, P, name):
    T, D = dx.shape
    G = modtab.shape[0]
    row = pl.BlockSpec((TR, D), lambda i: (i, 0))
    acc = pl.BlockSpec((1, 1, D), lambda i: (_grp(i, P), 0, 0))

    def body(dx_ref, br_ref, g_ref, db_ref, dg_ref):
        r = pl.program_id(0) % P
        dxv = dx_ref[...]
        db_ref[...] = (dxv * g_ref[0]).astype(BF16)
        part = jnp.sum(dxv * br_ref[...], axis=0, keepdims=True)

        @pl.when(r <= 1)
        def _():
            dg_ref[0] = part

        @pl.when(r > 1)
        def _():
            dg_ref[0] += part

    return pl.pallas_call(
        body, name=name, grid=(T // TR,), in_specs=[row, row, _mod_spec(D, P, gate_part)], out_specs=[row, acc],
        out_shape=[_sds((T, D), BF16), _sds((G, 1, D), F32)], compiler_params=_params(("arbitrary",)),
    )(dx, branch, modtab)


def _norm_bwd(x, dh, dres, modtab, scale_part, gain, *, TR, P, name):
    T, D = x.shape
    G = modtab.shape[0]
    row = pl.BlockSpec((TR, D), lambda i: (i, 0))
    acc = pl.BlockSpec((1, 1, D), lambda i: (_grp(i, P), 0, 0))

    def body(x_ref, dh_ref, dres_ref, sc_ref, gn_ref, dx_ref, dsh_ref, dsc_ref, dgn_ref):
        r = pl.program_id(0) % P
        xv, dhv, gn = x_ref[...], dh_ref[...], gn_ref[...]
        rstd = lax.rsqrt(jnp.mean(xv * xv, axis=-1, keepdims=True) + EPS)
        xhat = xv * rstd
        dn = dhv * (1.0 + sc_ref[0])
        dxhat = dn * gn
        dx_ref[...] = dres_ref[...] + rstd * (dxhat - xhat * jnp.mean(dxhat * xhat, axis=-1, keepdims=True))
        p_sh = jnp.sum(dhv, axis=0, keepdims=True)
        p_sc = jnp.sum(dhv * (xhat * gn), axis=0, keepdims=True)
        p_gn = jnp.sum(dn * xhat, axis=0, keepdims=True)

        @pl.when(r <= 1)
        def _():
            dsh_ref[0] = p_sh
            dsc_ref[0] = p_sc
            dgn_ref[0] = p_gn

        @pl.when(r > 1)
        def _():
            dsh_ref[0] += p_sh
            dsc_ref[0] += p_sc
            dgn_ref[0] += p_gn

    return pl.pallas_call(
        body, name=name, grid=(T // TR,),
        in_specs=[row, row, row, _mod_spec(D, P, scale_part), pl.BlockSpec((1, D), lambda i: (0, 0))],
        out_specs=[row, acc, acc, acc],
        out_shape=[_sds((T, D), F32)] + [_sds((G, 1, D), F32)] * 3, compiler_params=_params(("arbitrary",)),
    )(x, dh, dres, modtab, gain)


def _loss_head(x, branch, modtab, gate_part, target, *, TR, P, name):
    T, D = x.shape
    row = pl.BlockSpec((TR, D), lambda i: (i, 0))
    tgt = pl.BlockSpec((TR, D), lambda i: ((i // P) * (P - 1) + jnp.maximum(i % P - 1, 0), 0))
    one = pl.BlockSpec((1, LANES), lambda i: (0, 0))

    def body(x_ref, br_ref, g_ref, t_ref, dy_ref, loss_ref):
        i = pl.program_id(0)
        r = i % P

        @pl.when(i == 0)
        def _():
            loss_ref[...] = jnp.zeros_like(loss_ref)

        @pl.when(r == 0)
        def _():
            dy_ref[...] = jnp.zeros_like(dy_ref)

        @pl.when(r > 0)
        def _():
            err = x_ref[...] + g_ref[0] * br_ref[...] - t_ref[...]
            dy_ref[...] = err / D
            per_tok = jnp.mean(err * err, axis=-1, keepdims=True)
            loss_ref[...] += 0.5 * jnp.sum(per_tok, axis=0, keepdims=True)

    return pl.pallas_call(
        body, name=name, grid=(T // TR,), in_specs=[row, row, _mod_spec(D, P, gate_part), tgt], out_specs=[row, one],
        out_shape=[_sds((T, D), F32), _sds((1, LANES), F32)], compiler_params=_params(("arbitrary",)),
    )(x, branch, modtab, target)


QKV_WIDTH = Q_WIDTH + 2 * KV_WIDTH
QK_NORMED = 4


def _seg_mean(v):
    lane = lax.broadcasted_iota(jnp.int32, v.shape, 1)
    lo = lane < HEAD_DIM
    s0 = jnp.sum(jnp.where(lo, v, 0.0), axis=-1, keepdims=True)
    s1 = jnp.sum(jnp.where(lo, 0.0, v), axis=-1, keepdims=True)
    return jnp.where(lo, s0, s1) * (1.0 / HEAD_DIM)


def _pair_swap(v):
    lane = lax.broadcasted_iota(jnp.int32, v.shape, 1)
    return jnp.where((lane & 1) == 0, pltpu.roll(v, LANES - 1, 1), pltpu.roll(v, 1, 1))


def _chunk(c):
    return slice(c * LANES, (c + 1) * LANES)


def _qk_prep(z, gains, cos, sin, *, TR, P, name):
    T = z.shape[0]

    def body(z_ref, g_ref, c_ref, s_ref, q_ref, k_ref, v_ref):
        cs, sn = c_ref[...], s_ref[...]
        for ch in range(QK_NORMED):
            xv = z_ref[:, _chunk(ch)]
            y = xv * lax.rsqrt(_seg_mean(xv * xv) + EPS) * g_ref[0, :, _chunk(ch)]
            out = (y * cs + _pair_swap(y) * sn).astype(BF16)
            if ch < QK_NORMED - 1:
                q_ref[:, _chunk(ch)] = out
            else:
                k_ref[...] = out
        v_ref[...] = z_ref[:, _chunk(QK_NORMED)].astype(BF16)

    def out(width):
        return pl.BlockSpec((None, TR, width), lambda i, j: (j, i, 0))
    return pl.pallas_call(
        body, name=name, grid=(T // TR, 2),
        in_specs=[pl.BlockSpec((TR, QKV_WIDTH), lambda i, j: (i, j)),
                  pl.BlockSpec((1, 1, QKV_WIDTH), lambda i, j: (j, 0, 0)),
                  pl.BlockSpec((TR, LANES), lambda i, j: (i % P, 0)),
                  pl.BlockSpec((TR, LANES), lambda i, j: (i % P, 0))],
        out_specs=[out(Q_WIDTH), out(KV_WIDTH), out(KV_WIDTH)],
        out_shape=[_sds((2, T, Q_WIDTH), BF16), _sds((2, T, KV_WIDTH), BF16), _sds((2, T, KV_WIDTH), BF16)],
        compiler_params=_params(("parallel", "parallel")),
    )(z, gains, cos, sin)


def _qk_prep_bwd(z, dq, dk, dv, gains, cos, sin, *, branch, TR, P, name):
    T = z.shape[0]
    nt = T // TR

    def body(z_ref, dq_ref, dk_ref, dv_ref, g_ref, c_ref, s_ref, dz_ref, dg_ref):
        i = pl.program_id(0)
        cs, sn = c_ref[...], s_ref[...]
        parts = []
        for ch in range(QK_NORMED):
            xv, g = z_ref[:, _chunk(ch)], g_ref[0, :, _chunk(ch)]
            dout = dq_ref[:, _chunk(ch)] if ch < QK_NORMED - 1 else dk_ref[...]
            dy = dout * cs + _pair_swap(dout * sn)
            rstd = lax.rsqrt(_seg_mean(xv * xv) + EPS)
            xhat = xv * rstd
            dxhat = dy * g
            dz_ref[:, _chunk(ch)] = (rstd * (dxhat - xhat * _seg_mean(dxhat * xhat))).astype(BF16)
            parts.append(jnp.sum(dy * xhat, axis=0, keepdims=True))
        dz_ref[:, _chunk(QK_NORMED)] = dv_ref[...].astype(BF16)
        parts.append(jnp.zeros((1, LANES), F32))
        part = jnp.concatenate(parts, axis=1)

        @pl.when(i == 0)
        def _():
            dg_ref[0] = part

        @pl.when(i > 0)
        def _():
            dg_ref[0] += part

    def rows(width, col=0):
        return pl.BlockSpec((TR, width), lambda i: (i, col))
    return pl.pallas_call(
        body, name=name, grid=(nt,),
        in_specs=[rows(QKV_WIDTH, branch), rows(Q_WIDTH), rows(KV_WIDTH), rows(KV_WIDTH),
                  pl.BlockSpec((1, 1, QKV_WIDTH), lambda i: (branch, 0, 0)),
                  pl.BlockSpec((TR, LANES), lambda i: (i % P, 0)),
                  pl.BlockSpec((TR, LANES), lambda i: (i % P, 0))],
        out_specs=[rows(QKV_WIDTH), pl.BlockSpec((1, 1, QKV_WIDTH), lambda i: (0, 0, 0))],
        out_shape=[_sds((T, QKV_WIDTH), BF16), _sds((1, 1, QKV_WIDTH), F32)],
        compiler_params=_params(("arbitrary",)),
    )(z, dq, dk, dv, gains, cos, sin)


NT_DIMS = (((1,), (1,)), ((), ()))
TN_DIMS = (((0,), (0,)), ((), ()))
QROWS = N_GROUP * Q_BLOCK
SCORE_SCALE = HEAD_DIM ** -0.5
BAND = Q_BLOCK + 2 * WINDOW


def _move_head(block, half_from, half_to):
    lane = lax.broadcasted_iota(jnp.int32, block.shape, 1)
    src = block if half_from == half_to else pltpu.roll(block, HEAD_DIM, 1)
    keep = (lane < HEAD_DIM) if half_to == 0 else (lane >= HEAD_DIM)
    return jnp.where(keep, src, 0.0)


def _stack_heads(ref, j):
    pieces = []
    for h in range(N_GROUP * j, N_GROUP * (j + 1)):
        blk = ref[:, (h // 2) * LANES:(h // 2 + 1) * LANES].astype(F32)
        pieces.append(_move_head(blk, h % 2, j))
    return jnp.concatenate(pieces, axis=0)


def _unstack_heads(stacked, ref):
    heads = []
    for h in range(N_HEADS):
        j, r = h // N_GROUP, h % N_GROUP
        heads.append(_move_head(stacked[j][r * Q_BLOCK:(r + 1) * Q_BLOCK], j, h % 2))
    for m in range(N_HEADS // 2):
        ref[:, m * LANES:(m + 1) * LANES] = (heads[2 * m] + heads[2 * m + 1]).astype(ref.dtype)


def _attn_parts(q, k_ref, i, *, n_ctx, seq, t_all, window):
    def scores(start, size):
        return lax.dot_general(q, k_ref[pl.ds(start, size), :], NT_DIMS, preferred_element_type=F32)

    def ctx_case():
        return [(0, n_ctx, scores(0, n_ctx))]

    def latent_case():
        if not window:
            return [(0, t_all, scores(0, t_all))]
        start = pl.multiple_of(jnp.minimum((i - 1) * Q_BLOCK, t_all - BAND), Q_BLOCK)
        s_loc = scores(start, BAND)
        qpos = (i * Q_BLOCK - n_ctx) + (lax.broadcasted_iota(jnp.int32, (QROWS, BAND), 0) & (Q_BLOCK - 1))
        kpos = (start - n_ctx) + lax.broadcasted_iota(jnp.int32, (QROWS, BAND), 1)
        valid = (kpos - qpos <= WINDOW) & (qpos - kpos <= WINDOW) & (kpos >= 0)
        return [(0, n_ctx, scores(0, n_ctx)), (start, BAND, jnp.where(valid, s_loc, NEG))]

    return ctx_case, latent_case


def _softmax_parts(parts, sink_col):
    m = functools.reduce(jnp.maximum, [jnp.max(s, axis=-1, keepdims=True) for (_, _, s) in parts])
    if sink_col is not None:
        m = jnp.maximum(m, sink_col)
    es = [jnp.exp(s - m) for (_, _, s) in parts]
    l = functools.reduce(jnp.add, [jnp.sum(e, axis=-1, keepdims=True) for e in es])
    if sink_col is not None:
        e_sink = jnp.exp(sink_col - m)
        l = l + e_sink
    inv = 1.0 / l
    return [e * inv for e in es], (e_sink * inv if sink_col is not None else None)


def _sink_column(sink_ref, j):
    r = lax.broadcasted_iota(jnp.int32, (QROWS, 1), 0)
    s0, s1, s2 = sink_ref[j * N_GROUP], sink_ref[j * N_GROUP + 1], sink_ref[j * N_GROUP + 2]
    return jnp.where(r < Q_BLOCK, s0, jnp.where(r < 2 * Q_BLOCK, s1, s2))


def _attn_specs(Tp, branch):
    nq = Tp // Q_BLOCK
    q_in = pl.BlockSpec((None, Q_BLOCK, Q_WIDTH), lambda b, i: (branch, b * nq + i, 0))
    kv_in = pl.BlockSpec((None, Tp, KV_WIDTH), lambda b, i: (branch, b, 0))
    q_out = pl.BlockSpec((Q_BLOCK, Q_WIDTH), lambda b, i: (b * nq + i, 0))
    kv_out = pl.BlockSpec((Tp, KV_WIDTH), lambda b, i: (b, 0))
    return q_in, kv_in, q_out, kv_out


def _attn_fwd(q, k, v, sink, *, branch, B, n_ctx, window, name):
    T = q.shape[1]
    Tp = T // B
    seq = Tp - n_ctx
    has_sink = sink is not None
    q_in, kv_in, q_out, _ = _attn_specs(Tp, branch)

    def body(*refs):
        if has_sink:
            sink_ref, q_ref, k_ref, v_ref, o_ref = refs
        else:
            q_ref, k_ref, v_ref, o_ref = refs
        i = pl.program_id(1)

        def run(latent):
            outs = []
            for j in range(N_KV):
                qv = (_stack_heads(q_ref, j) * SCORE_SCALE).astype(BF16)
                sink_col = _sink_column(sink_ref, j) if has_sink else None
                cases = _attn_parts(qv, k_ref, i, n_ctx=n_ctx, seq=seq, t_all=Tp, window=window)
                parts = cases[latent]()
                probs, _ = _softmax_parts(parts, sink_col)
                o = None
                for (start, size, _), p in zip(parts, probs):
                    t = jnp.dot(p.astype(BF16), v_ref[pl.ds(start, size), :], preferred_element_type=F32)
                    o = t if o is None else o + t
                outs.append(o)
            _unstack_heads(outs, o_ref)

        @pl.when(i < n_ctx // Q_BLOCK)
        def _():
            run(0)

        @pl.when(i >= n_ctx // Q_BLOCK)
        def _():
            run(1)

    ins, specs = [q, k, v], [q_in, kv_in, kv_in]
    if has_sink:
        ins, specs = [sink] + ins, [pl.BlockSpec(memory_space=pltpu.SMEM)] + specs
    return pl.pallas_call(
        body, name=name, grid=(B, Tp // Q_BLOCK), in_specs=specs, out_specs=q_out,
        out_shape=_sds((T, Q_WIDTH), BF16), compiler_params=_params(("parallel", "parallel")),
    )(*ins)


def _attn_bwd(q, k, v, do, sink, *, branch, B, n_ctx, window, name):
    T = q.shape[1]
    Tp = T // B
    seq = Tp - n_ctx
    has_sink = sink is not None
    q_in, kv_in, q_out, kv_out = _attn_specs(Tp, branch)
    sink_spec = pl.BlockSpec((None, 8, LANES), lambda b, i: (b, 0, 0))

    def body(*refs):
        if has_sink:
            sink_ref, q_ref, k_ref, v_ref, do_ref, dq_ref, dk_ref, dv_ref, ds_ref = refs
        else:
            q_ref, k_ref, v_ref, do_ref, dq_ref, dk_ref, dv_ref = refs
        i = pl.program_id(1)

        @pl.when(i == 0)
        def _():
            dk_ref[...] = jnp.zeros_like(dk_ref)
            dv_ref[...] = jnp.zeros_like(dv_ref)
            if has_sink:
                ds_ref[...] = jnp.zeros_like(ds_ref)

        def run(latent):
            dqs = []
            upd = jnp.zeros((8, LANES), F32)
            for j in range(N_KV):
                qv = (_stack_heads(q_ref, j) * SCORE_SCALE).astype(BF16)
                dov = _stack_heads(do_ref, j).astype(BF16)
                sink_col = _sink_column(sink_ref, j) if has_sink else None
                cases = _attn_parts(qv, k_ref, i, n_ctx=n_ctx, seq=seq, t_all=Tp, window=window)
                parts = cases[latent]()
                probs, p_sink = _softmax_parts(parts, sink_col)
                dps = [lax.dot_general(dov, v_ref[pl.ds(start, size), :], NT_DIMS, preferred_element_type=F32)
                       for (start, size, _) in parts]
                delta = functools.reduce(jnp.add, [jnp.sum(p * dp, axis=-1, keepdims=True) for p, dp in zip(probs, dps)])
                dq = None
                for (start, size, _), p, dp in zip(parts, probs, dps):
                    ds = (p * (dp - delta)).astype(BF16)
                    rows = pl.ds(start, size)
                    t = jnp.dot(ds, k_ref[rows, :], preferred_element_type=F32)
                    dq = t if dq is None else dq + t
                    dk_ref[rows, :] += lax.dot_general(ds, qv, TN_DIMS, preferred_element_type=F32)
                    dv_ref[rows, :] += lax.dot_general(p.astype(BF16), dov, TN_DIMS, preferred_element_type=F32)
                dqs.append(dq * SCORE_SCALE)
                if has_sink:
                    contrib = -(p_sink * delta)
                    r = lax.broadcasted_iota(jnp.int32, (QROWS, 1), 0)
                    row8 = lax.broadcasted_iota(jnp.int32, (8, LANES), 0)
                    for h in range(N_GROUP):
                        in_head = (r >= h * Q_BLOCK) & (r < (h + 1) * Q_BLOCK)
                        tot = jnp.sum(jnp.where(in_head, contrib, 0.0), axis=0, keepdims=True)
                        upd = upd + jnp.where(row8 == j * N_GROUP + h, tot, 0.0)
            _unstack_heads(dqs, dq_ref)
            if has_sink:
                ds_ref[...] += upd

        @pl.when(i < n_ctx // Q_BLOCK)
        def _():
            run(0)

        @pl.when(i >= n_ctx // Q_BLOCK)
        def _():
            run(1)

    ins, specs = [q, k, v, do], [q_in, kv_in, kv_in, q_out]
    out_specs = [q_out, kv_out, kv_out]
    out_shape = [_sds((T, Q_WIDTH), F32), _sds((T, KV_WIDTH), F32), _sds((T, KV_WIDTH), F32)]
    if has_sink:
        ins, specs = [sink] + ins, [pl.BlockSpec(memory_space=pltpu.SMEM)] + specs
        out_specs.append(sink_spec)
        out_shape.append(_sds((B, 8, LANES), F32))
    return pl.pallas_call(
        body, name=name, grid=(B, Tp // Q_BLOCK), in_specs=specs, out_specs=out_specs, out_shape=out_shape,
        compiler_params=_params(("parallel", "arbitrary")),
    )(*ins)


def _window_sums(xp):
    n = xp.shape[0]

    def ahead(a, k):
        return pltpu.roll(a, n - k, 0)
    a2 = xp + ahead(xp, 1)
    a4 = a2 + ahead(a2, 2)
    a8 = a4 + ahead(a4, 4)
    a16 = a8 + ahead(a8, 8)
    return (a2, a4, a8, a16)


def _by_group(vals):
    lane = lax.broadcasted_iota(jnp.int32, vals[0].shape, 1)
    return jnp.where(lane < POOL_CH, vals[0], jnp.where(lane < 2 * POOL_CH, vals[1],
                     jnp.where(lane < 3 * POOL_CH, vals[2], vals[3])))


def _pool_counts(n):
    t = lax.broadcasted_iota(jnp.int32, (n, POOL_WIDTH), 0)
    cnts = [(jnp.minimum(t + w // 2, n) - jnp.maximum(t - w // 2, 0)).astype(F32) for w in POOL_WINDOWS]
    return _by_group(cnts)


def _pad_rows(x):
    zeros = jnp.zeros((POOL_PAD, x.shape[1]), x.dtype)
    return jnp.concatenate([zeros, x, zeros], axis=0)


def _pool_stream(u):
    n = u.shape[0]
    sums = _window_sums(_pad_rows(u))
    tots = [pltpu.roll(a, w // 2, 0)[POOL_PAD:POOL_PAD + n] for a, w in zip(sums, POOL_WINDOWS)]
    return _by_group(tots) / _pool_counts(n) - u


def _pool_stream_t(dp):
    n = dp.shape[0]
    sums = _window_sums(_pad_rows(dp / _pool_counts(n)))
    tots = [pltpu.roll(a, w // 2 - 1, 0)[POOL_PAD:POOL_PAD + n] if w > 2 else a[POOL_PAD:POOL_PAD + n]
            for a, w in zip(sums, POOL_WINDOWS)]
    return _by_group(tots) - dp


def _pool_fwd(z, w_bd, scale, *, B, Tp, n_ctx, name):
    T = z.shape[0]
    blk = pl.BlockSpec((Tp, POOL_WIDTH), lambda b: (b, U_COL // POOL_WIDTH))
    out = pl.BlockSpec((Tp, POOL_WIDTH), lambda b: (b, 0))

    def body(u_ref, w_ref, s_ref, p_ref, o_ref):
        for lo, hi in ((0, n_ctx), (n_ctx, Tp)):
            pooled = _pool_stream(u_ref[lo:hi, :]).astype(BF16)
            p_ref[lo:hi, :] = pooled
            mixed = jnp.dot(pooled, w_ref[...], preferred_element_type=F32)
            o_ref[lo:hi, :] = (mixed * s_ref[...]).astype(BF16)

    return pl.pallas_call(
        body, name=name, grid=(B,),
        in_specs=[blk, pl.BlockSpec((POOL_WIDTH, POOL_WIDTH), lambda b: (0, 0)), pl.BlockSpec((1, POOL_WIDTH), lambda b: (0, 0))],
        out_specs=[out, out], out_shape=[_sds((T, POOL_WIDTH), BF16)] * 2, compiler_params=_params(("parallel",)),
    )(z, w_bd, scale)


def _pool_bwd(d_ob, pooled, w_bd, scale, *, B, Tp, n_ctx, name):
    T = d_ob.shape[0]
    blk = pl.BlockSpec((Tp, POOL_WIDTH), lambda b: (b, 0))
    wsp = pl.BlockSpec((POOL_WIDTH, POOL_WIDTH), lambda b: (0, 0))
    ssp = pl.BlockSpec((1, POOL_WIDTH), lambda b: (0, 0))

    def body(d_ref, p_ref, w_ref, s_ref, du_ref, dw_ref, dsc_ref):
        @pl.when(pl.program_id(0) == 0)
        def _():
            dw_ref[...] = jnp.zeros_like(dw_ref)
            dsc_ref[...] = jnp.zeros_like(dsc_ref)

        dv, pv, wv = d_ref[...], p_ref[...], w_ref[...]
        mixed = jnp.dot(pv, wv, preferred_element_type=F32)
        dsc_ref[...] += jnp.sum(dv * mixed, axis=0, keepdims=True)
        dmixed = (dv * s_ref[...]).astype(BF16)
        dw_ref[...] += lax.dot_general(pv, dmixed, TN_DIMS, preferred_element_type=F32)
        dpooled = lax.dot_general(dmixed, wv, NT_DIMS, preferred_element_type=F32)
        for lo, hi in ((0, n_ctx), (n_ctx, Tp)):
            du_ref[lo:hi, :] = _pool_stream_t(dpooled[lo:hi, :]).astype(BF16)

    return pl.pallas_call(
        body, name=name, grid=(B,), in_specs=[blk, blk, wsp, ssp], out_specs=[blk, wsp, ssp],
        out_shape=[_sds((T, POOL_WIDTH), BF16), _sds((POOL_WIDTH, POOL_WIDTH), F32), _sds((1, POOL_WIDTH), F32)],
        compiler_params=_params(("arbitrary",)),
    )(d_ob, pooled, w_bd, scale)


def _merge_specs(z, D, TR, tc, wa, wb, wc):
    def act(width):
        return pl.BlockSpec((TR, width), lambda i, n: (i, 0))

    def gate(part):
        return pl.BlockSpec((TR, tc), lambda i, n: (i, (GATE_COL + part * D) // tc + n))
    w_specs = [w.spec(w.shape[0], tc, lambda i, n: (0, n)) for w in (wa, wb, wc)]
    return [act(Q_WIDTH), act(POOL_WIDTH), act(Q_WIDTH), gate(0), gate(1), gate(2)] + w_specs


def _merge_fwd(oa, ob, oc, z, wa, wb, wc, *, D, TR, name):
    T = oa.shape[0]
    tc = D // N_CHIPS

    def body(oa_ref, ob_ref, oc_ref, ga_ref, gb_ref, gc_ref, wa_ref, wb_ref, wc_ref, y_ref):
        acc = jax.nn.sigmoid(ga_ref[...]) * jnp.dot(oa_ref[...], wa_ref[...], preferred_element_type=F32)
        acc += jax.nn.sigmoid(gb_ref[...]) * jnp.dot(ob_ref[...], wb_ref[...], preferred_element_type=F32)
        acc += jax.nn.sigmoid(gc_ref[...]) * jnp.dot(oc_ref[...], wc_ref[...], preferred_element_type=F32)
        y_ref[...] = acc.astype(BF16)

    return pl.pallas_call(
        body, name=name, grid=(T // TR, D // tc), in_specs=_merge_specs(z, D, TR, tc, wa, wb, wc),
        out_specs=pl.BlockSpec((TR, tc), lambda i, n: (i, n)), out_shape=_sds((T, D), BF16),
        compiler_params=_params(("parallel", "parallel")),
    )(oa, ob, oc, z, z, z, wa.arr, wb.arr, wc.arr)


def _merge_bwd(dy, oa, ob, oc, z, wa, wb, wc, *, D, TR, name):
    T = oa.shape[0]
    tc = D // N_CHIPS
    out = pl.BlockSpec((TR, tc), lambda i, n: (i, n))

    def body(dy_ref, oa_ref, ob_ref, oc_ref, ga_ref, gb_ref, gc_ref, wa_ref, wb_ref, wc_ref,
             dpa_ref, dpb_ref, dpc_ref, dga_ref, dgb_ref, dgc_ref):
        dyv = dy_ref[...]
        for o_ref, g_ref, w_ref, dp_ref, dg_ref in ((oa_ref, ga_ref, wa_ref, dpa_ref, dga_ref),
                                                    (ob_ref, gb_ref, wb_ref, dpb_ref, dgb_ref),
                                                    (oc_ref, gc_ref, wc_ref, dpc_ref, dgc_ref)):
            s = jax.nn.sigmoid(g_ref[...])
            proj = jnp.dot(o_ref[...], w_ref[...], preferred_element_type=F32)
            dp_ref[...] = (dyv * s).astype(BF16)
            dg_ref[...] = (dyv * proj * (s * (1.0 - s))).astype(BF16)

    return pl.pallas_call(
        body, name=name, grid=(T // TR, D // tc), in_specs=[out] + _merge_specs(z, D, TR, tc, wa, wb, wc),
        out_specs=[out] * 6, out_shape=[_sds((T, D), BF16)] * 6, compiler_params=_params(("parallel", "parallel")),
    )(dy, oa, ob, oc, z, z, z, wa.arr, wb.arr, wc.arr)


def _silu_rows(cc, name):
    def body(c_ref, s_ref):
        v = c_ref[...]
        s_ref[...] = (v * jax.nn.sigmoid(v)).astype(BF16)
    return pl.pallas_call(body, name=name, out_shape=_sds(cc.shape, BF16))(cc)


def _ada_bwd_rows(dm, ds, cc, name):
    def body(dm_ref, ds_ref, c_ref, db_ref, dc_ref):
        db_ref[...] = jnp.sum(dm_ref[...], axis=0, keepdims=True)
        v = c_ref[...]
        s = jax.nn.sigmoid(v)
        dc_ref[...] = ds_ref[...] * (s * (1.0 + v * (1.0 - s)))
    return pl.pallas_call(body, name=name, out_shape=[_sds((1, dm.shape[1]), F32), _sds(cc.shape, F32)])(dm, ds, cc)


def _row_tile(rows, cols):
    for t in (512, 256, 128, 64, 32, 16, 8):
        if rows % t == 0 and t * cols * 4 <= (1 << 20):
            return t
    return rows


def _add_own_layer(layers, landed, core, name):
    R, C = landed.shape
    tr = _row_tile(R, C)
    n_layers = len(layers)

    def body(c_ref, *refs):
        b_ref, o_ref, o16_ref = refs[n_layers:]
        for l in range(n_layers):
            @pl.when(c_ref[0] == l)
            def _(a_ref=refs[l]):
                tot = a_ref[...] + b_ref[...]
                o_ref[...] = tot
                o16_ref[...] = tot.astype(BF16)

    row = pl.BlockSpec((tr, C), lambda i, c: (i, 0))
    own = [pl.BlockSpec((tr, C), functools.partial(lambda i, c, l: (jnp.where(c[0] == l, i, 0), 0), l=l))
           for l in range(n_layers)]
    grid_spec = pltpu.PrefetchScalarGridSpec(num_scalar_prefetch=1, grid=(R // tr,),
                                             in_specs=own + [row], out_specs=[row, row])
    return pl.pallas_call(body, name=name, grid_spec=grid_spec, out_shape=[_sds((R, C), F32), _sds((R, C), BF16)],
                          compiler_params=_params(("arbitrary",)))(core, *layers, landed)


def _sum_chips(own, landed, chip, name):
    _, R, C = own.shape
    tr = _row_tile(R, C)

    def body(k_ref, a_ref, b_ref, o_ref):
        o_ref[...] = ((a_ref[...] + b_ref[0].astype(F32)) + b_ref[1].astype(F32)) + b_ref[2].astype(F32)

    grid_spec = pltpu.PrefetchScalarGridSpec(
        num_scalar_prefetch=1, grid=(R // tr,),
        in_specs=[pl.BlockSpec((None, tr, C), lambda i, k: (k[0], i, 0)), pl.BlockSpec((3, tr, C), lambda i, k: (0, i, 0))],
        out_specs=pl.BlockSpec((tr, C), lambda i, k: (i, 0)))
    return pl.pallas_call(body, name=name, grid_spec=grid_spec, out_shape=_sds((R, C), F32),
                          compiler_params=_params(("parallel",)))(chip, own, landed)


def _adam_math(w, g, m, v):
    m = ADAM_B1 * m + (1.0 - ADAM_B1) * g
    v = ADAM_B2 * v + (1.0 - ADAM_B2) * (g * g)
    m_hat = m / (1.0 - ADAM_B1 ** ADAM_STEP)
    v_hat = v / (1.0 - ADAM_B2 ** ADAM_STEP)
    delta = -ADAM_LR * (m_hat / (jnp.sqrt(v_hat) + ADAM_EPS) + ADAM_WD * w)
    return delta, m, v


def _adamw(w, mine, other, m, v, core, name):
    L, R, C = w.shape
    tr = _row_tile(R, C)

    def body(c_ref, w_ref, a_ref, b_ref, m_ref, v_ref, g_ref, d_ref, mo_ref, vo_ref):
        def step(g):
            d, mn, vn = _adam_math(w_ref[...], g, m_ref[...], v_ref[...])
            g_ref[...] = g
            d_ref[...] = d
            mo_ref[...] = mn
            vo_ref[...] = vn

        @pl.when(pl.program_id(0) == c_ref[0])
        def _():
            step(a_ref[...])

        @pl.when(pl.program_id(0) != c_ref[0])
        def _():
            step(b_ref[...])

    lay = pl.BlockSpec((None, tr, C), lambda l, i, c: (l, i, 0))
    row = pl.BlockSpec((tr, C), lambda l, i, c: (i, 0))
    grid_spec = pltpu.PrefetchScalarGridSpec(num_scalar_prefetch=1, grid=(L, R // tr),
                                             in_specs=[lay, row, row, lay, lay], out_specs=[lay] * 4)
    return pl.pallas_call(body, name=name, grid_spec=grid_spec, out_shape=[_sds((L, R, C), F32)] * 4,
                          compiler_params=_params(("parallel", "parallel")))(core, w, mine, other, m, v)


def _adamw_small(w, parts, m, v, name):
    R, C = w.shape

    def body(w_ref, p_ref, m_ref, v_ref, g_ref, d_ref, mo_ref, vo_ref):
        g = p_ref[0]
        for dev in range(1, 8):
            g = g + p_ref[dev]
        d, mn, vn = _adam_math(w_ref[...], g, m_ref[...], v_ref[...])
        g_ref[...] = g
        d_ref[...] = d
        mo_ref[...] = mn
        vo_ref[...] = vn

    return pl.pallas_call(body, name=name, out_shape=[_sds((R, C), F32)] * 4)(w, parts, m, v)


def _place():
    return lax.axis_index("x"), lax.axis_index("y"), lax.axis_index("c")


def _other_chips(x, y):
    return [(1 - x, y), (x, 1 - y), (1 - x, 1 - y)]


def _rcopy(src, dst, ssem, rsem, dev):
    return pltpu.make_async_remote_copy(src_ref=src, dst_ref=dst, send_sem=ssem, recv_sem=rsem,
                                        device_id=dev, device_id_type=MESH)


def _gather_weights(shards, name):
    n = len(shards)

    def body(*refs):
        src, out = refs[:n], refs[n:2 * n]
        send_sems, recv_sems = refs[2 * n:]
        x, y, c = _place()
        sibling = (x, y, 1 - c)
        chips = _other_chips(x, y)
        mine = 2 * x + y
        first = [_rcopy(src[w].at[c], out[w].at[mine, c], send_sems.at[w, j], recv_sems.at[w, j], (*chip, c))
                 for w in range(n) for j, chip in enumerate(chips)]
        for cp in first:
            cp.start()
        passed = []
        for w in range(n):
            for j, (px, py) in enumerate(chips):
                landed = out[w].at[2 * px + py, c]
                _rcopy(landed, landed, send_sems.at[w, j], recv_sems.at[w, j], (px, py, c)).wait_recv()
                cp = _rcopy(landed, landed, send_sems.at[w, 3 + j], recv_sems.at[w, 3 + j], sibling)
                cp.start()
                passed.append(cp)
        for w in range(n):
            for j, (px, py) in enumerate(chips):
                landed = out[w].at[2 * px + py, 1 - c]
                _rcopy(landed, landed, send_sems.at[w, 3 + j], recv_sems.at[w, 3 + j], sibling).wait_recv()
        for cp in first + passed:
            cp.wait_send()

    landed = pl.pallas_call(
        body, name=name, in_specs=[ANY] * n, out_specs=[ANY] * n,
        out_shape=[_sds((N_CHIPS,) + s.shape, s.dtype) for s in shards],
        scratch_shapes=[pltpu.SemaphoreType.DMA((n, 6)), pltpu.SemaphoreType.DMA((n, 6))],
    )(*shards)
    mine = 2 * lax.axis_index("x") + lax.axis_index("y")
    return [lax.dynamic_update_index_in_dim(g, s, mine, 0) for g, s in zip(landed, shards)]


def _send_other_layer(layer0, layer1, name):
    n = len(layer0)

    def body(*refs):
        src0, src1, out = refs[:n], refs[n:2 * n], refs[2 * n:3 * n]
        send_sems, recv_sems = refs[3 * n:]
        x, y, c = _place()

        def copies(src):
            return [_rcopy(src[w], out[w], send_sems.at[w], recv_sems.at[w], (x, y, 1 - c)) for w in range(n)]

        @pl.when(c == 0)
        def _():
            for cp in copies(src1):
                cp.start()

        @pl.when(c == 1)
        def _():
            for cp in copies(src0):
                cp.start()

        for cp in copies(src0):
            cp.wait_recv()
        for cp in copies(src0):
            cp.wait_send()

    return pl.pallas_call(
        body, name=name, in_specs=[ANY] * (2 * n), out_specs=[ANY] * n,
        out_shape=[_sds(s.shape, s.dtype) for s in layer0],
        scratch_shapes=[pltpu.SemaphoreType.DMA((n,)), pltpu.SemaphoreType.DMA((n,))],
    )(*layer0, *layer1)


def _send_chip_blocks(blocked, name):
    n = len(blocked)

    def body(*refs):
        src, out = refs[:n], refs[n:2 * n]
        send_sems, recv_sems = refs[2 * n:]
        x, y, c = _place()
        cps = [_rcopy(src[w].at[2 * px + py], out[w].at[j], send_sems.at[w, j], recv_sems.at[w, j], (px, py, c))
               for w in range(n) for j, (px, py) in enumerate(_other_chips(x, y))]
        for cp in cps:
            cp.start()
        for cp in cps:
            cp.wait_recv()
        for cp in cps:
            cp.wait_send()

    return pl.pallas_call(
        body, name=name, in_specs=[ANY] * n, out_specs=[ANY] * n,
        out_shape=[_sds((3,) + s.shape[1:], s.dtype) for s in blocked],
        scratch_shapes=[pltpu.SemaphoreType.DMA((n, 3)), pltpu.SemaphoreType.DMA((n, 3))],
    )(*blocked)


def _share_layers(reduced, name):
    n = len(reduced)

    def body(*refs):
        src, out = refs[:n], refs[n:2 * n]
        send_sems, recv_sems = refs[2 * n:]
        x, y, c = _place()
        cps = [_rcopy(src[w], out[w], send_sems.at[w], recv_sems.at[w], (x, y, 1 - c)) for w in range(n)]
        for cp in cps:
            cp.start()
        for cp in cps:
            cp.wait_recv()
        for cp in cps:
            cp.wait_send()

    return pl.pallas_call(
        body, name=name, in_specs=[ANY] * n, out_specs=[ANY] * n,
        out_shape=[_sds(s.shape, s.dtype) for s in reduced],
        scratch_shapes=[pltpu.SemaphoreType.DMA((n,)), pltpu.SemaphoreType.DMA((n,))],
    )(*reduced)


def _gather_small(block, name):
    m_per, n = block.shape

    def body(x_ref, out_ref, send_sems, recv_sems, local_sem):
        x, y, c = _place()
        me, sibling = (x, y, c), (x, y, 1 - c)
        chips = _other_chips(x, y)

        def rows(px, py, pc):
            return out_ref.at[pl.ds((4 * px + 2 * py + pc) * m_per, m_per), :]

        def copy(k, blk, to, src=None):
            return _rcopy(rows(*blk) if src is None else src, rows(*blk), send_sems.at[k], recv_sems.at[k], to)

        mine = pltpu.make_async_copy(x_ref, rows(*me), local_sem)
        mine.start()
        first = [copy(0, me, sibling, src=x_ref)]
        first += [copy(1 + j, me, (*chip, c), src=x_ref) for j, chip in enumerate(chips)]
        for cp in first:
            cp.start()
        passed = [copy(4 + j, (*chip, c), sibling) for j, chip in enumerate(chips)]
        for j, chip in enumerate(chips):
            copy(1 + j, (*chip, c), me).wait_recv()
            passed[j].start()
        copy(0, sibling, me).wait_recv()
        for j, chip in enumerate(chips):
            copy(4 + j, (*chip, 1 - c), me).wait_recv()
        for cp in first + passed:
            cp.wait_send()
        mine.wait()

    return pl.pallas_call(
        body, name=name, out_shape=_sds((8 * m_per, n), block.dtype),
        in_specs=[pl.BlockSpec(memory_space=pltpu.VMEM)], out_specs=pl.BlockSpec(memory_space=pltpu.VMEM),
        scratch_shapes=[pltpu.SemaphoreType.DMA((7,)), pltpu.SemaphoreType.DMA((7,)), pltpu.SemaphoreType.DMA],
    )(block)


def _rope_tables(n_ctx, seq):
    rows = seq // GRID_W
    r = jnp.repeat(jnp.arange(rows, dtype=F32), GRID_W)
    col = jnp.tile(jnp.arange(GRID_W, dtype=F32), rows)
    inv = 1.0 / (ROPE_THETA ** (jnp.arange(0, AXIS_DIM, 2, dtype=F32) / AXIS_DIM))
    ang = jnp.concatenate([r[:, None] * inv, col[:, None] * inv], axis=-1)
    cos = jnp.repeat(jnp.cos(ang), 2, axis=-1)
    sin = jnp.repeat(jnp.sin(ang), 2, axis=-1) * jnp.tile(jnp.array([-1.0, 1.0], F32), HEAD_DIM // 2)
    cos = jnp.concatenate([jnp.ones((n_ctx, HEAD_DIM), F32), cos], axis=0)
    sin = jnp.concatenate([jnp.zeros((n_ctx, HEAD_DIM), F32), sin], axis=0)
    return jnp.tile(cos, (1, 2)), jnp.tile(sin, (1, 2))


def _block_diag(w_pool_l):
    out = jnp.zeros((POOL_WIDTH, POOL_WIDTH), w_pool_l.dtype)
    for g in range(w_pool_l.shape[0]):
        out = out.at[g * POOL_CH:(g + 1) * POOL_CH, g * POOL_CH:(g + 1) * POOL_CH].set(w_pool_l[g])
    return out


def _local_step(x, c, ctx, c_ctx, small, gw, target):
    B, S, D = x.shape
    N = ctx.shape[1]
    L = small["norm1"].shape[0]
    Tp = N + S
    T = B * Tp
    TR = N
    P = Tp // N
    rows16 = 16
    assert N % Q_BLOCK == 0 and S % N == 0 and B + 1 <= rows16
    TM = _tile(T, (1024, 768, 512, 384, 256, 128))
    TMG = _tile(T, (512, 384, 256, 128))

    X = jnp.concatenate([ctx, x], axis=1).reshape(T, D)
    cc = jnp.zeros((rows16, D), F32).at[:B].set(c).at[B].set(c_ctx)
    s_rows = _silu_rows(cc, "silu_rows")
    cos, sin = _rope_tables(N, S)

    def weights(l):
        return dict(
            ada=_Opnd(gw["w_ada"], "bcols", l), w_in=_Opnd(gw["w_in"], "bcols", l),
            a=_Opnd(gw["w_br_a"], "bcols", l), b=_Opnd(gw["w_br_b"], "bcols", l), c=_Opnd(gw["w_br_c"], "bcols", l),
            out=_Opnd(gw["w_out"], "brows", l), mlp1=_Opnd(gw["w_mlp1"], "bcols", l), mlp2=_Opnd(gw["w_mlp2"], "brows", l))

    IN = weights(0)["w_in"].shape[1]
    DFF = weights(0)["mlp1"].shape[1]
    tn_in = _tile(IN // N_CHIPS, (1152, 768, 512, 384, 256, 128))
    tn_ff = _tile(DFF // N_CHIPS, (1024, 512, 256, 128))
    tn_ada = _tile(6 * D // N_CHIPS, (1536, 768, 512, 256, 128))
    tn_d = D // N_CHIPS
    tk_d = _tile(D, (512,))
    tk_tok = _tile(T, (2304, 1536, 1024, 768, 512, 384, 256))

    saved = []
    xin, pending = X, None
    for l in range(L):
        W = weights(l)
        b_ada = small["b_ada"][l].reshape(1, 6 * D)
        mod = _matmul(s_rows, W["ada"], "nn", tm=rows16, tn=tn_ada, tk=D, name=f"ada_fwd{l}",
                      epilogue=lambda acc, b: (acc + b,), extras=[(b_ada, (1, tn_ada), lambda m, n: (0, n))])
        modtab = jnp.stack([jnp.broadcast_to(mod[B], (B, 6 * D)), mod[:B]], axis=1).reshape(2 * B, 1, 6 * D)
        ones = jnp.ones((LANES,), F32)
        gains = jnp.stack([
            jnp.concatenate([jnp.tile(small[q][l], 2)] * 3 + [jnp.tile(small[k][l], 2), ones])
            for q, k in (("q_norm_a", "k_norm_a"), ("q_norm_c", "k_norm_c"))]).reshape(2, 1, QKV_WIDTH)
        w_bd = _block_diag(small["w_pool"][l]).astype(BF16)
        p_scale = small["pool_scale"][l].reshape(1, POOL_WIDTH)
        sink = small["sink_c"][l]

        x0, h1 = _res_norm(xin, pending, modtab, 0, 1, small["norm1"][l][None], TR=TR, P=P, name=f"norm1_fwd{l}")
        z = _matmul(h1, W["w_in"], "nn", tm=TM, tn=tn_in, tk=D, name=f"in_proj{l}")
        q2, k2, v2 = _qk_prep(z, gains, cos, sin, TR=TR, P=P, name=f"qk_prep{l}")
        oa = _attn_fwd(q2, k2, v2, None, branch=0, B=B, n_ctx=N, window=False, name=f"attn_a_fwd{l}")
        oc = _attn_fwd(q2, k2, v2, sink, branch=1, B=B, n_ctx=N, window=True, name=f"attn_c_fwd{l}")
        pooled, ob = _pool_fwd(z, w_bd, p_scale, B=B, Tp=Tp, n_ctx=N, name=f"pool_fwd{l}")
        y = _merge_fwd(oa, ob, oc, z, W["a"], W["b"], W["c"], D=D, TR=TMG, name=f"merge_fwd{l}")
        ao = _matmul(y, W["out"], "nn", tm=TM, tn=D, tk=tn_d, name=f"out_proj{l}")
        x1, h2 = _res_norm(x0, (ao, modtab, 2), modtab, 3, 4, small["norm2"][l][None], TR=TR, P=P, name=f"norm2_fwd{l}")
        a_pre, r_act = _matmul(h2, W["mlp1"], "nn", tm=TM, tn=tn_ff, tk=D, name=f"mlp1_fwd{l}", out_dtypes=(F32, BF16),
                               epilogue=lambda acc: (acc, jnp.square(jnp.maximum(acc, 0.0))))
        mo = _matmul(r_act, W["mlp2"], "nn", tm=TM, tn=D, tk=tn_ff, name=f"mlp2_fwd{l}")
        saved.append(dict(modtab=modtab, gains=gains, w_bd=w_bd, p_scale=p_scale, sink=sink, x0=x0, h1=h1, z=z,
                          q2=q2, k2=k2, v2=v2, oa=oa, ob=ob, oc=oc, pooled=pooled, y=y, ao=ao,
                          x1=x1, h2=h2, a_pre=a_pre, r_act=r_act, mo=mo))
        xin, pending = x1, (mo, modtab, 5)

    dxo, loss = _loss_head(xin, pending[0], pending[1], 5, target.reshape(B * S, D), TR=TR, P=P, name="loss_head")

    big = {k: [None] * L for k in gw}
    sm = {k: [None] * L for k in ("b_ada", "norm1", "norm2", "q_norm_a", "k_norm_a", "q_norm_c", "k_norm_c",
                                   "sink_c", "w_pool", "pool_scale")}
    d_cctx = jnp.zeros((D,), F32)
    for l in reversed(range(L)):
        W, sv = weights(l), saved[l]
        modtab = sv["modtab"]
        d_mo, dg2 = _gate_bwd(dxo, sv["mo"], modtab, 5, TR=TR, P=P, name=f"gate2_bwd{l}")
        d_a = _matmul(d_mo, W["mlp2"], "nt", tm=TM, tn=tn_ff, tk=D, name=f"mlp2_bwd{l}", out_dtypes=(BF16,),
                      epilogue=lambda acc, a: (acc * (2.0 * jnp.maximum(a, 0.0)),),
                      extras=[(sv["a_pre"], (TM, tn_ff), lambda m, n: (m, n))])
        big["w_mlp2"][l] = _matmul(sv["r_act"], d_mo, "tn", tm=tk_d, tn=D, tk=tk_tok,
                                   name=f"mlp2_dw{l}").reshape(N_CHIPS, DFF // N_CHIPS, D)
        d_h2 = _matmul(d_a, W["mlp1"], "nt", tm=TM, tn=D, tk=tn_ff, name=f"mlp1_bwd{l}")
        big["w_mlp1"][l] = _matmul(sv["h2"], d_a, "tn", tm=tk_d, tn=tn_ff, tk=tk_tok, name=f"mlp1_dw{l}", out_blocked=True)
        dx1, dsh2, dsc2, dn2 = _norm_bwd(sv["x1"], d_h2, dxo, modtab, 4, small["norm2"][l][None], TR=TR, P=P,
                                         name=f"norm2_bwd{l}")
        d_ao, dg1 = _gate_bwd(dx1, sv["ao"], modtab, 2, TR=TR, P=P, name=f"gate1_bwd{l}")
        d_y = _matmul(d_ao, W["out"], "nt", tm=TM, tn=tn_d, tk=D, name=f"out_bwd{l}")
        big["w_out"][l] = _matmul(sv["y"], d_ao, "tn", tm=tk_d, tn=D, tk=tk_tok,
                                  name=f"out_dw{l}").reshape(N_CHIPS, D // N_CHIPS, D)
        d_pa, d_pb, d_pc, d_ga, d_gb, d_gc = _merge_bwd(d_y, sv["oa"], sv["ob"], sv["oc"], sv["z"], W["a"], W["b"], W["c"],
                                                        D=D, TR=TMG, name=f"merge_bwd{l}")
        d_oa = _matmul(d_pa, W["a"], "nt", tm=TM, tn=Q_WIDTH, tk=tn_d, name=f"br_a_bwd{l}", out_dtypes=(BF16,))
        d_ob = _matmul(d_pb, W["b"], "nt", tm=TM, tn=POOL_WIDTH, tk=tn_d, name=f"br_b_bwd{l}")
        d_oc = _matmul(d_pc, W["c"], "nt", tm=TM, tn=Q_WIDTH, tk=tn_d, name=f"br_c_bwd{l}", out_dtypes=(BF16,))
        big["w_br_a"][l] = _matmul(sv["oa"], d_pa, "tn", tm=Q_WIDTH, tn=tn_d, tk=tk_tok, name=f"br_a_dw{l}", out_blocked=True)
        big["w_br_b"][l] = _matmul(sv["ob"], d_pb, "tn", tm=POOL_WIDTH, tn=tn_d, tk=tk_tok, name=f"br_b_dw{l}", out_blocked=True)
        big["w_br_c"][l] = _matmul(sv["oc"], d_pc, "tn", tm=Q_WIDTH, tn=tn_d, tk=tk_tok, name=f"br_c_dw{l}", out_blocked=True)
        d_u, d_wbd, d_ps = _pool_bwd(d_ob, sv["pooled"], sv["w_bd"], sv["p_scale"], B=B, Tp=Tp, n_ctx=N, name=f"pool_bwd{l}")
        dqa, dka, dva = _attn_bwd(sv["q2"], sv["k2"], sv["v2"], d_oa, None, branch=0, B=B, n_ctx=N, window=False,
                                  name=f"attn_a_bwd{l}")
        dqc, dkc, dvc, dsink = _attn_bwd(sv["q2"], sv["k2"], sv["v2"], d_oc, sv["sink"], branch=1, B=B, n_ctx=N,
                                         window=True, name=f"attn_c_bwd{l}")
        dz_a, dgains_a = _qk_prep_bwd(sv["z"], dqa, dka, dva, sv["gains"], cos, sin, branch=0, TR=TR, P=P,
                                      name=f"qk_prep_a_bwd{l}")
        dz_c, dgains_c = _qk_prep_bwd(sv["z"], dqc, dkc, dvc, sv["gains"], cos, sin, branch=1, TR=TR, P=P,
                                      name=f"qk_prep_c_bwd{l}")
        dz = jnp.concatenate([dz_a, dz_c, d_u, d_ga, d_gb, d_gc], axis=1)
        d_h1 = _matmul(dz, W["w_in"], "nt", tm=TM, tn=D, tk=tn_in, name=f"in_bwd{l}")
        big["w_in"][l] = _matmul(sv["h1"], dz, "tn", tm=tk_d, tn=tn_in, tk=tk_tok, name=f"in_dw{l}", out_blocked=True)
        dx0, dsh1, dsc1, dn1 = _norm_bwd(sv["x0"], d_h1, dx1, modtab, 1, small["norm1"][l][None], TR=TR, P=P,
                                         name=f"norm1_bwd{l}")

        dm_groups = jnp.concatenate([dsh1, dsc1, dg1, dsh2, dsc2, dg2], axis=-1).reshape(B, 2, 6 * D)
        dm = jnp.zeros((rows16, 6 * D), F32).at[:B].set(dm_groups[:, 1]).at[B].set(jnp.sum(dm_groups[:, 0], axis=0))
        dm_bf = dm.astype(BF16)
        d_s = _matmul(dm_bf, W["ada"], "nt", tm=rows16, tn=D, tk=tn_ada, name=f"ada_bwd{l}")
        big["w_ada"][l] = _matmul(s_rows, dm_bf, "tn", tm=tk_d, tn=tn_ada, tk=rows16, name=f"ada_dw{l}", out_blocked=True)
        db_ada, dcc = _ada_bwd_rows(dm, d_s, cc, f"ada_rows_bwd{l}")
        d_cctx = d_cctx + dcc[B]

        sm["b_ada"][l] = db_ada[0]
        sm["norm1"][l] = jnp.sum(dn1, axis=(0, 1))
        sm["norm2"][l] = jnp.sum(dn2, axis=(0, 1))
        dgh = jnp.stack([dgains_a, dgains_c]).reshape(2, QKV_WIDTH // HEAD_DIM, HEAD_DIM)
        sm["q_norm_a"][l] = jnp.sum(dgh[0, :N_HEADS], axis=0)
        sm["k_norm_a"][l] = jnp.sum(dgh[0, N_HEADS:N_HEADS + N_KV], axis=0)
        sm["q_norm_c"][l] = jnp.sum(dgh[1, :N_HEADS], axis=0)
        sm["k_norm_c"][l] = jnp.sum(dgh[1, N_HEADS:N_HEADS + N_KV], axis=0)
        sm["sink_c"][l] = jnp.sum(dsink[:, :N_HEADS, 0], axis=0)
        sm["w_pool"][l] = jnp.stack([d_wbd[g * POOL_CH:(g + 1) * POOL_CH, g * POOL_CH:(g + 1) * POOL_CH]
                                     for g in range(POOL_WIDTH // POOL_CH)])
        sm["pool_scale"][l] = d_ps[0]
        dxo = dx0

    grad_x = dxo.reshape(B, Tp, D)[:, N:]
    small_grads = {k: jnp.stack(v) for k, v in sm.items()}
    small_grads["c_ctx"] = d_cctx
    return loss, grad_x, small_grads, big


SMALL_NAMES = ("c_ctx", "b_ada", "norm1", "norm2", "q_norm_a", "k_norm_a", "q_norm_c", "k_norm_c", "sink_c",
               "w_pool", "pool_scale")
BIG_NAMES = ("w_ada", "w_in", "w_br_a", "w_br_b", "w_br_c", "w_out", "w_mlp1", "w_mlp2")
WEIGHT_NAMES = ("c_ctx", "w_ada", "b_ada", "norm1", "norm2", "w_in", "q_norm_a", "k_norm_a", "q_norm_c", "k_norm_c",
                "sink_c", "w_pool", "pool_scale", "w_br_a", "w_br_b", "w_br_c", "w_out", "w_mlp1", "w_mlp2")


def _pack(parts, rows):
    flat = jnp.concatenate([p.reshape(-1).astype(F32) for p in parts])
    return jnp.pad(flat, (0, rows * LANES - flat.shape[0])).reshape(rows, LANES)


def _unpack(packed, like):
    flat, out, at = packed.reshape(-1), [], 0
    for p in like:
        out.append(flat[at:at + p.size].reshape(p.shape))
        at += p.size
    return out


def _reduce_big(partials):
    names = list(partials)
    assert all(len(partials[k]) == 2 for k in names)
    x, y, c = _place()
    core = c.astype(jnp.int32).reshape(1)
    chip = (2 * x + y).astype(jnp.int32).reshape(1)
    shapes = [partials[k][0].shape for k in names]
    flat = [[g.reshape(-1, g.shape[-1]) for g in partials[k]] for k in names]
    landed = _send_other_layer([f[0] for f in flat], [f[1] for f in flat], "grads_to_sibling")
    in_chip = [_add_own_layer(f, r, core, f"grads_add_sibling_{k}") for k, f, r in zip(names, flat, landed)]
    blocked = [h.reshape(s) for s, (h, _) in zip(shapes, in_chip)]
    blocked16 = [h.reshape(s) for s, (_, h) in zip(shapes, in_chip)]
    from_chips = _send_chip_blocks(blocked16, "grads_to_chips")
    reduced = [_sum_chips(h, r, chip, f"grads_sum_chips_{k}") for k, h, r in zip(names, blocked, from_chips)]
    shared = _share_layers(reduced, "grads_share_layers")
    return core, dict(zip(names, zip(reduced, shared)))


def kernel(x, c, ctx, c_ctx, w_ada, b_ada, norm1, norm2, w_in, q_norm_a, k_norm_a, q_norm_c, k_norm_c, sink_c, w_pool, pool_scale, w_br_a, w_br_b, w_br_c, w_out, w_mlp1, w_mlp2, loss_target, m_c_ctx, m_w_ada, m_b_ada, m_norm1, m_norm2, m_w_in, m_q_norm_a, m_k_norm_a, m_q_norm_c, m_k_norm_c, m_sink_c, m_w_pool, m_pool_scale, m_w_br_a, m_w_br_b, m_w_br_c, m_w_out, m_w_mlp1, m_w_mlp2, v_c_ctx, v_w_ada, v_b_ada, v_norm1, v_norm2, v_w_in, v_q_norm_a, v_k_norm_a, v_q_norm_c, v_k_norm_c, v_sink_c, v_w_pool, v_pool_scale, v_w_br_a, v_w_br_b, v_w_br_c, v_w_out, v_w_mlp1, v_w_mlp2):
    given = dict(locals())
    w = {k: given[k] for k in WEIGHT_NAMES}
    m = {k: given["m_" + k] for k in WEIGHT_NAMES}
    v = {k: given["v_" + k] for k in WEIGHT_NAMES}

    gathered = _gather_weights([w[k].astype(BF16) for k in BIG_NAMES], "gather_weights")
    gw = dict(zip(BIG_NAMES, gathered))
    small = {k: w[k] for k in SMALL_NAMES}
    loss_part, grad_x, small_grads, big_grads = _local_step(x, c, ctx, c_ctx, small, gw, loss_target)

    core, reduced = _reduce_big({k: big_grads[k] for k in BIG_NAMES})
    grads, deltas, new_m, new_v = {}, {}, {}, {}
    for k in BIG_NAMES:
        mine, other = reduced[k]
        grads[k], deltas[k], new_m[k], new_v[k] = _adamw(w[k], mine, other, m[k], v[k], core, f"adamw_{k}")

    sizes = sum(w[k].size for k in SMALL_NAMES) + LANES
    rows = -(-sizes // (8 * LANES)) * 8
    parts = _gather_small(_pack([small_grads[k] for k in SMALL_NAMES] + [loss_part[0]], rows), "gather_small")
    zero = jnp.zeros((LANES,), F32)
    packed = [_pack([t[k] for k in SMALL_NAMES] + [zero], rows) for t in (w, m, v)]
    outs = _adamw_small(packed[0], parts.reshape(8, rows, LANES), packed[1], packed[2], "adamw_small")
    like = [w[k] for k in SMALL_NAMES] + [zero]
    for store, packed_out in zip((grads, deltas, new_m, new_v), outs):
        pieces = _unpack(packed_out, like)
        for k, piece in zip(SMALL_NAMES, pieces):
            store[k] = piece
        if store is grads:
            loss = pieces[-1][0]

    return (loss, grad_x, *[grads[k] for k in WEIGHT_NAMES], *[deltas[k] for k in WEIGHT_NAMES],
            *[new_m[k] for k in WEIGHT_NAMES], *[new_v[k] for k in WEIGHT_NAMES])
```

```python
import functools

import jax
import jax.numpy as jnp
from jax import lax
from jax.experimental import pallas as pl
from jax.experimental.pallas import tpu as pltpu

F32 = jnp.float32
BF16 = jnp.bfloat16

HEAD_DIM = 64
GRID_W = 64
AXIS_DIM = HEAD_DIM // 2
ROPE_THETA = 10000.0
N_HEADS = 6
N_KV = 2
N_GROUP = N_HEADS // N_KV
POOL_CH = 64
POOL_WIDTH = 256
POOL_WINDOWS = (2, 4, 8, 16)
WINDOW = 128
Q_BLOCK = 128
Q_WIDTH = N_HEADS * HEAD_DIM
KV_WIDTH = N_KV * HEAD_DIM
GATE_COL = 2 * (Q_WIDTH + 2 * KV_WIDTH) + POOL_WIDTH
U_COL = 2 * (Q_WIDTH + 2 * KV_WIDTH)
EPS = 1e-6
NEG = -1e30
ADAM_LR = 0.001
ADAM_B1 = 0.9
ADAM_B2 = 0.999
ADAM_EPS = 1e-08
ADAM_WD = 0.01
ADAM_STEP = 10

N_CHIPS = 4
LANES = 128
POOL_PAD = 16
VMEM_LIMIT = 48 * 1024 * 1024
MESH = pl.DeviceIdType.MESH
ANY = pl.BlockSpec(memory_space=pl.ANY)


def _params(sem):
    return pltpu.CompilerParams(dimension_semantics=sem, vmem_limit_bytes=VMEM_LIMIT)


def _sds(shape, dtype):
    return jax.ShapeDtypeStruct(tuple(shape), dtype)


class _Opnd:
    def __init__(self, arr, kind="plain", layer=None):
        self.arr, self.kind, self.layer = arr, kind, layer

    @property
    def shape(self):
        a = self.arr
        if self.kind == "plain":
            return a.shape
        if self.kind == "bcols":
            return (a.shape[2], N_CHIPS * a.shape[3])
        return (N_CHIPS * a.shape[2], a.shape[3])

    def spec(self, tr, tc, fn):
        a, layer = self.arr, self.layer
        if self.kind == "plain":
            return pl.BlockSpec((tr, tc), lambda *g: fn(*g))
        if self.kind == "bcols":
            assert a.shape[3] % tc == 0, (a.shape, tc)
            per = a.shape[3] // tc

            def im(*g):
                ri, ci = fn(*g)
                return (ci // per, layer, ri, ci % per)
            return pl.BlockSpec((None, None, tr, tc), im)
        assert a.shape[2] % tr == 0, (a.shape, tr)
        per = a.shape[2] // tr

        def im(*g):
            ri, ci = fn(*g)
            return (ri // per, layer, ri % per, ci)
        return pl.BlockSpec((None, None, tr, tc), im)


def _matmul(a, b, mode, *, tm, tn, tk, name, out_dtypes=(F32,), epilogue=None, extras=(), out_blocked=False):
    if not isinstance(a, _Opnd):
        a = _Opnd(a)
    if not isinstance(b, _Opnd):
        b = _Opnd(b)
    if mode == "nn":
        (M, K), (K2, N) = a.shape, b.shape
        a_spec = a.spec(tm, tk, lambda m, n, k: (m, k))
        b_spec = b.spec(tk, tn, lambda m, n, k: (k, n))
        dims = (((1,), (0,)), ((), ()))
    elif mode == "nt":
        (M, K), (N, K2) = a.shape, b.shape
        a_spec = a.spec(tm, tk, lambda m, n, k: (m, k))
        b_spec = b.spec(tn, tk, lambda m, n, k: (n, k))
        dims = (((1,), (1,)), ((), ()))
    else:
        (K, M), (K2, N) = a.shape, b.shape
        a_spec = a.spec(tk, tm, lambda m, n, k: (k, m))
        b_spec = b.spec(tk, tn, lambda m, n, k: (k, n))
        dims = (((0,), (0,)), ((), ()))
    assert K == K2 and M % tm == 0 and N % tn == 0 and K % tk == 0, (name, M, N, K, K2, tm, tn, tk)
    nk = K // tk
    n_extra = len(extras)
    n_out = len(out_dtypes)
    extra_specs = [pl.BlockSpec(bs, functools.partial(lambda m, n, k, f: f(m, n), f=f)) for (_, bs, f) in extras]
    if out_blocked:
        assert (N // N_CHIPS) % tn == 0
        per = (N // N_CHIPS) // tn
        out_shape = [_sds((N_CHIPS, M, N // N_CHIPS), dt) for dt in out_dtypes]
        out_specs = [pl.BlockSpec((None, tm, tn), lambda m, n, k: (n // per, m, n % per)) for _ in out_dtypes]
    else:
        out_shape = [_sds((M, N), dt) for dt in out_dtypes]
        out_specs = [pl.BlockSpec((tm, tn), lambda m, n, k: (m, n)) for _ in out_dtypes]

    in_place = nk > 1 and epilogue is None and tuple(out_dtypes) == (F32,)

    def body(*refs):
        a_ref, b_ref = refs[0], refs[1]
        extra_refs = refs[2:2 + n_extra]
        out_refs = refs[2 + n_extra:2 + n_extra + n_out]
        acc_ref = out_refs[0] if in_place else (refs[2 + n_extra + n_out] if nk > 1 else None)
        k = pl.program_id(2)
        prod = lax.dot_general(a_ref[...].astype(BF16), b_ref[...].astype(BF16), dims, preferred_element_type=F32)

        def finish(acc):
            outs = epilogue(acc, *[r[...] for r in extra_refs]) if epilogue is not None else (acc,)
            for o_ref, o in zip(out_refs, outs):
                o_ref[...] = o.astype(o_ref.dtype)

        if nk == 1:
            finish(prod)
        elif in_place:
            @pl.when(k == 0)
            def _():
                acc_ref[...] = prod

            @pl.when(k > 0)
            def _():
                acc_ref[...] += prod
        else:
            @pl.when(k == 0)
            def _():
                acc_ref[...] = prod

            @pl.when(k > 0)
            def _():
                acc_ref[...] += prod

            @pl.when(k == nk - 1)
            def _():
                finish(acc_ref[...])

    outs = pl.pallas_call(
        body, name=name, grid=(M // tm, N // tn, nk),
        in_specs=[a_spec, b_spec] + extra_specs, out_specs=out_specs, out_shape=out_shape,
        scratch_shapes=[pltpu.VMEM((tm, tn), F32)] if nk > 1 and not in_place else [],
        compiler_params=_params(("parallel", "parallel", "arbitrary")),
    )(a.arr, b.arr, *[e[0] for e in extras])
    return outs[0] if n_out == 1 else outs


def _tile(n, cands):
    for t in cands:
        if n % t == 0:
            return t
    return n


def _grp(i, P):
    return 2 * (i // P) + jnp.minimum(i % P, 1)


def _mod_spec(D, P, part):
    return pl.BlockSpec((1, 1, D), lambda i: (_grp(i, P), 0, part))


def _res_norm(x, pending, modtab, shift_part, scale_part, gain, *, TR, P, name):
    T, D = x.shape
    row = pl.BlockSpec((TR, D), lambda i: (i, 0))
    has_branch = pending is not None
    ins, specs = [x], [row]
    if has_branch:
        branch, gate_tab, gate_part = pending
        ins += [branch, gate_tab]
        specs += [row, _mod_spec(D, P, gate_part)]
    ins += [modtab, modtab, gain]
    specs += [_mod_spec(D, P, shift_part), _mod_spec(D, P, scale_part), pl.BlockSpec((1, D), lambda i: (0, 0))]

    def body(*refs):
        if has_branch:
            x_ref, br_ref, g_ref, sh_ref, sc_ref, gn_ref, xo_ref, h_ref = refs
            xv = x_ref[...] + g_ref[0] * br_ref[...]
        else:
            x_ref, sh_ref, sc_ref, gn_ref, xo_ref, h_ref = refs
            xv = x_ref[...]
        xo_ref[...] = xv
        y = xv * lax.rsqrt(jnp.mean(xv * xv, axis=-1, keepdims=True) + EPS) * gn_ref[...]
        h_ref[...] = (y * (1.0 + sc_ref[0]) + sh_ref[0]).astype(BF16)

    return pl.pallas_call(
        body, name=name, grid=(T // TR,), in_specs=specs, out_specs=[row, row],
        out_shape=[_sds((T, D), F32), _sds((T, D), BF16)], compiler_params=_params(("parallel",)),
    )(*ins)


def _gate_bwd(dx, branch, modtab, gate_part, *, TR, P, name):
    T, D = dx.shape
    G = modtab.shape[0]
    row = pl.BlockSpec((TR, D), lambda i: (i, 0))
    acc = pl.BlockSpec((1, 1, D), lambda i: (_grp(i, P), 0, 0))

    def body(dx_ref, br_ref, g_ref, db_ref, dg_ref):
        r = pl.program_id(0) % P
        dxv = dx_ref[...]
        db_ref[...] = (dxv * g_ref[0]).astype(BF16)
        part = jnp.sum(dxv * br_ref[...], axis=0, keepdims=True)

        @pl.when(r <= 1)
        def _():
            dg_ref[0] = part

        @pl.when(r > 1)
        def _():
            dg_ref[0] += part

    return pl.pallas_call(
        body, name=name, grid=(T // TR,), in_specs=[row, row, _mod_spec(D, P, gate_part)], out_specs=[row, acc],
        out_shape=[_sds((T, D), BF16), _sds((G, 1, D), F32)], compiler_params=_params(("arbitrary",)),
    )(dx, branch, modtab)


def _norm_bwd(x, dh, dres, modtab, scale_part, gain, *, TR, P, name):
    T, D = x.shape
    G = modtab.shape[0]
    row = pl.BlockSpec((TR, D), lambda i: (i, 0))
    acc = pl.BlockSpec((1, 1, D), lambda i: (_grp(i, P), 0, 0))

    def body(x_ref, dh_ref, dres_ref, sc_ref, gn_ref, dx_ref, dsh_ref, dsc_ref, dgn_ref):
        r = pl.program_id(0) % P
        xv, dhv, gn = x_ref[...], dh_ref[...], gn_ref[...]
        rstd = lax.rsqrt(jnp.mean(xv * xv, axis=-1, keepdims=True) + EPS)
        xhat = xv * rstd
        dn = dhv * (1.0 + sc_ref[0])
        dxhat = dn * gn
        dx_ref[...] = dres_ref[...] + rstd * (dxhat - xhat * jnp.mean(dxhat * xhat, axis=-1, keepdims=True))
        p_sh = jnp.sum(dhv, axis=0, keepdims=True)
        p_sc = jnp.sum(dhv * (xhat * gn), axis=0, keepdims=True)
        p_gn = jnp.sum(dn * xhat, axis=0, keepdims=True)

        @pl.when(r <= 1)
        def _():
            dsh_ref[0] = p_sh
            dsc_ref[0] = p_sc
            dgn_ref[0] = p_gn

        @pl.when(r > 1)
        def _():
            dsh_ref[0] += p_sh
            dsc_ref[0] += p_sc
            dgn_ref[0] += p_gn

    return pl.pallas_call(
        body, name=name, grid=(T // TR,),
        in_specs=[row, row, row, _mod_spec(D, P, scale_part), pl.BlockSpec((1, D), lambda i: (0, 0))],
        out_specs=[row, acc, acc, acc],
        out_shape=[_sds((T, D), F32)] + [_sds((G, 1, D), F32)] * 3, compiler_params=_params(("arbitrary",)),
    )(x, dh, dres, modtab, gain)


def _loss_head(x, branch, modtab, gate_part, target, *, TR, P, name):
    T, D = x.shape
    row = pl.BlockSpec((TR, D), lambda i: (i, 0))
    tgt = pl.BlockSpec((TR, D), lambda i: ((i // P) * (P - 1) + jnp.maximum(i % P - 1, 0), 0))
    one = pl.BlockSpec((1, LANES), lambda i: (0, 0))

    def body(x_ref, br_ref, g_ref, t_ref, dy_ref, loss_ref):
        i = pl.program_id(0)
        r = i % P

        @pl.when(i == 0)
        def _():
            loss_ref[...] = jnp.zeros_like(loss_ref)

        @pl.when(r == 0)
        def _():
            dy_ref[...] = jnp.zeros_like(dy_ref)

        @pl.when(r > 0)
        def _():
            err = x_ref[...] + g_ref[0] * br_ref[...] - t_ref[...]
            dy_ref[...] = err / D
            per_tok = jnp.mean(err * err, axis=-1, keepdims=True)
            loss_ref[...] += 0.5 * jnp.sum(per_tok, axis=0, keepdims=True)

    return pl.pallas_call(
        body, name=name, grid=(T // TR,), in_specs=[row, row, _mod_spec(D, P, gate_part), tgt], out_specs=[row, one],
        out_shape=[_sds((T, D), F32), _sds((1, LANES), F32)], compiler_params=_params(("arbitrary",)),
    )(x, branch, modtab, target)


QKV_WIDTH = Q_WIDTH + 2 * KV_WIDTH
QK_NORMED = 4


def _seg_mean(v):
    lane = lax.broadcasted_iota(jnp.int32, v.shape, 1)
    lo = lane < HEAD_DIM
    s0 = jnp.sum(jnp.where(lo, v, 0.0), axis=-1, keepdims=True)
    s1 = jnp.sum(jnp.where(lo, 0.0, v), axis=-1, keepdims=True)
    return jnp.where(lo, s0, s1) * (1.0 / HEAD_DIM)


def _pair_swap(v):
    lane = lax.broadcasted_iota(jnp.int32, v.shape, 1)
    return jnp.where((lane & 1) == 0, pltpu.roll(v, LANES - 1, 1), pltpu.roll(v, 1, 1))


def _chunk(c):
    return slice(c * LANES, (c + 1) * LANES)


def _qk_prep(z, gains, cos, sin, *, TR, P, name):
    T = z.shape[0]

    def body(z_ref, g_ref, c_ref, s_ref, q_ref, k_ref, v_ref):
        cs, sn = c_ref[...], s_ref[...]
        for ch in range(QK_NORMED):
            xv = z_ref[:, _chunk(ch)]
            y = xv * lax.rsqrt(_seg_mean(xv * xv) + EPS) * g_ref[0, :, _chunk(ch)]
            out = (y * cs + _pair_swap(y) * sn).astype(BF16)
            if ch < QK_NORMED - 1:
                q_ref[:, _chunk(ch)] = out
            else:
                k_ref[...] = out
        v_ref[...] = z_ref[:, _chunk(QK_NORMED)].astype(BF16)

    def out(width):
        return pl.BlockSpec((None, TR, width), lambda i, j: (j, i, 0))
    return pl.pallas_call(
        body, name=name, grid=(T // TR, 2),
        in_specs=[pl.BlockSpec((TR, QKV_WIDTH), lambda i, j: (i, j)),
                  pl.BlockSpec((1, 1, QKV_WIDTH), lambda i, j: (j, 0, 0)),
                  pl.BlockSpec((TR, LANES), lambda i, j: (i % P, 0)),
                  pl.BlockSpec((TR, LANES), lambda i, j: (i % P, 0))],
        out_specs=[out(Q_WIDTH), out(KV_WIDTH), out(KV_WIDTH)],
        out_shape=[_sds((2, T, Q_WIDTH), BF16), _sds((2, T, KV_WIDTH), BF16), _sds((2, T, KV_WIDTH), BF16)],
        compiler_params=_params(("parallel", "parallel")),
    )(z, gains, cos, sin)


def _qk_prep_bwd(z, dq, dk, dv, gains, cos, sin, *, branch, TR, P, name):
    T = z.shape[0]
    nt = T // TR

    def body(z_ref, dq_ref, dk_ref, dv_ref, g_ref, c_ref, s_ref, dz_ref, dg_ref):
        i = pl.program_id(0)
        cs, sn = c_ref[...], s_ref[...]
        parts = []
        for ch in range(QK_NORMED):
            xv, g = z_ref[:, _chunk(ch)], g_ref[0, :, _chunk(ch)]
            dout = dq_ref[:, _chunk(ch)] if ch < QK_NORMED - 1 else dk_ref[...]
            dy = dout * cs + _pair_swap(dout * sn)
            rstd = lax.rsqrt(_seg_mean(xv * xv) + EPS)
            xhat = xv * rstd
            dxhat = dy * g
            dz_ref[:, _chunk(ch)] = (rstd * (dxhat - xhat * _seg_mean(dxhat * xhat))).astype(BF16)
            parts.append(jnp.sum(dy * xhat, axis=0, keepdims=True))
        dz_ref[:, _chunk(QK_NORMED)] = dv_ref[...].astype(BF16)
        parts.append(jnp.zeros((1, LANES), F32))
        part = jnp.concatenate(parts, axis=1)

        @pl.when(i == 0)
        def _():
            dg_ref[0] = part

        @pl.when(i > 0)
        def _():
            dg_ref[0] += part

    def rows(width, col=0):
        return pl.BlockSpec((TR, width), lambda i: (i, col))
    return pl.pallas_call(
        body, name=name, grid=(nt,),
        in_specs=[rows(QKV_WIDTH, branch), rows(Q_WIDTH), rows(KV_WIDTH), rows(KV_WIDTH),
                  pl.BlockSpec((1, 1, QKV_WIDTH), lambda i: (branch, 0, 0)),
                  pl.BlockSpec((TR, LANES), lambda i: (i % P, 0)),
                  pl.BlockSpec((TR, LANES), lambda i: (i % P, 0))],
        out_specs=[rows(QKV_WIDTH), pl.BlockSpec((1, 1, QKV_WIDTH), lambda i: (0, 0, 0))],
        out_shape=[_sds((T, QKV_WIDTH), BF16), _sds((1, 1, QKV_WIDTH), F32)],
        compiler_params=_params(("arbitrary",)),
    )(z, dq, dk, dv, gains, cos, sin)


NT_DIMS = (((1,), (1,)), ((), ()))
TN_DIMS = (((0,), (0,)), ((), ()))
QROWS = N_GROUP * Q_BLOCK
SCORE_SCALE = HEAD_DIM ** -0.5
BAND = Q_BLOCK + 2 * WINDOW
FWD_LATENT_CHUNK = 256
BWD_LATENT_CHUNK = 2048


def _move_head(block, half_from, half_to):
    lane = lax.broadcasted_iota(jnp.int32, block.shape, 1)
    src = block if half_from == half_to else pltpu.roll(block, HEAD_DIM, 1)
    keep = (lane < HEAD_DIM) if half_to == 0 else (lane >= HEAD_DIM)
    return jnp.where(keep, src, 0.0)


def _stack_heads(lane_block, j):
    pieces = []
    for h in range(N_GROUP * j, N_GROUP * (j + 1)):
        pieces.append(_move_head(lane_block(h // 2), h % 2, j))
    return jnp.concatenate(pieces, axis=0)


def _lane_blocks(ref):
    return lambda m: ref[:, m * LANES:(m + 1) * LANES].astype(F32)


def _unstack_heads(stacked, ref):
    heads = []
    for h in range(N_HEADS):
        j, r = h // N_GROUP, h % N_GROUP
        heads.append(_move_head(stacked[j][r * Q_BLOCK:(r + 1) * Q_BLOCK], j, h % 2))
    for m in range(N_HEADS // 2):
        ref[:, m * LANES:(m + 1) * LANES] = (heads[2 * m] + heads[2 * m + 1]).astype(ref.dtype)


def _key_chunks(i, latent, *, n_ctx, t_all, window, chunk, latent_chunk):
    ctx = [(s, chunk, False) for s in range(0, n_ctx, chunk)]
    if not latent:
        return ctx
    if not window:
        wide = latent_chunk if (t_all - n_ctx) % latent_chunk == 0 else chunk
        return ctx + [(s, wide, False) for s in range(n_ctx, t_all, wide)]
    start = pl.multiple_of(jnp.minimum((i - 1) * Q_BLOCK, t_all - BAND), Q_BLOCK)
    band_chunk = BAND if latent_chunk >= BAND else (chunk if BAND % chunk == 0 else Q_BLOCK)
    return ctx + [(start + s, band_chunk, True) for s in range(0, BAND, band_chunk)]


def _scores(q, k_ref, i, start, size, masked, *, n_ctx):
    s = lax.dot_general(q, k_ref[pl.ds(start, size), :], NT_DIMS, preferred_element_type=F32)
    if masked:
        qpos = (i * Q_BLOCK - n_ctx) + (lax.broadcasted_iota(jnp.int32, (QROWS, size), 0) & (Q_BLOCK - 1))
        kpos = (start - n_ctx) + lax.broadcasted_iota(jnp.int32, (QROWS, size), 1)
        valid = (kpos - qpos <= WINDOW) & (qpos - kpos <= WINDOW) & (kpos >= 0)
        s = jnp.where(valid, s, NEG)
    return s


def _sink_column(sink_ref, j):
    r = lax.broadcasted_iota(jnp.int32, (QROWS, 1), 0)
    s0, s1, s2 = sink_ref[j * N_GROUP], sink_ref[j * N_GROUP + 1], sink_ref[j * N_GROUP + 2]
    return jnp.where(r < Q_BLOCK, s0, jnp.where(r < 2 * Q_BLOCK, s1, s2))


def _attn_specs(Tp, branch):
    nq = Tp // Q_BLOCK
    q_in = pl.BlockSpec((None, Q_BLOCK, Q_WIDTH), lambda b, i: (branch, b * nq + i, 0))
    kv_in = pl.BlockSpec((None, Tp, KV_WIDTH), lambda b, i: (branch, b, 0))
    q_out = pl.BlockSpec((Q_BLOCK, Q_WIDTH), lambda b, i: (b * nq + i, 0))
    kv_out = pl.BlockSpec((Tp, KV_WIDTH), lambda b, i: (b, 0))
    return q_in, kv_in, q_out, kv_out


def _attn_chunk(Tp):
    return 256 if Tp % 256 == 0 else Q_BLOCK


def _attn_fwd(q, k, v, sink, *, branch, B, n_ctx, window, name):
    T = q.shape[1]
    Tp = T // B
    nq = Tp // Q_BLOCK
    has_sink = sink is not None
    q_in, kv_in, q_out, _ = _attn_specs(Tp, branch)
    lse_spec = pl.BlockSpec((None, N_KV * QROWS, 1), lambda b, i: (b * nq + i, 0, 0))

    def body(*refs):
        if has_sink:
            sink_ref, q_ref, k_ref, v_ref, o_ref, o32_ref, lse_ref = refs
        else:
            q_ref, k_ref, v_ref, o_ref, o32_ref, lse_ref = refs
        i = pl.program_id(1)

        def run(latent):
            outs = []
            for j in range(N_KV):
                qv = (_stack_heads(_lane_blocks(q_ref), j) * SCORE_SCALE).astype(BF16)
                if has_sink:
                    m, l = _sink_column(sink_ref, j), jnp.ones((QROWS, 1), F32)
                else:
                    m, l = jnp.full((QROWS, 1), NEG, F32), jnp.zeros((QROWS, 1), F32)
                acc = jnp.zeros((QROWS, LANES), F32)
                for start, size, masked in _key_chunks(i, latent, n_ctx=n_ctx, t_all=Tp, window=window,
                                                       chunk=_attn_chunk(Tp), latent_chunk=FWD_LATENT_CHUNK):
                    s = _scores(qv, k_ref, i, start, size, masked, n_ctx=n_ctx)
                    m_new = jnp.maximum(m, jnp.max(s, axis=-1, keepdims=True))
                    alpha = jnp.exp(m - m_new)
                    p = jnp.exp(s - m_new)
                    l = l * alpha + jnp.sum(p, axis=-1, keepdims=True)
                    acc = acc * alpha + jnp.dot(p.astype(BF16), v_ref[pl.ds(start, size), :], preferred_element_type=F32)
                    m = m_new
                outs.append(acc * (1.0 / l))
                lse_ref[j * QROWS:(j + 1) * QROWS, :] = m + jnp.log(l)
            _unstack_heads(outs, o_ref)
            _unstack_heads(outs, o32_ref)

        @pl.when(i < n_ctx // Q_BLOCK)
        def _():
            run(False)

        @pl.when(i >= n_ctx // Q_BLOCK)
        def _():
            run(True)

    ins, specs = [q, k, v], [q_in, kv_in, kv_in]
    if has_sink:
        ins, specs = [sink] + ins, [pl.BlockSpec(memory_space=pltpu.SMEM)] + specs
    return pl.pallas_call(
        body, name=name, grid=(B, nq), in_specs=specs, out_specs=[q_out, q_out, lse_spec],
        out_shape=[_sds((T, Q_WIDTH), BF16), _sds((T, Q_WIDTH), F32), _sds((T // Q_BLOCK, N_KV * QROWS, 1), F32)],
        compiler_params=_params(("parallel", "parallel")),
    )(*ins)


def _attn_bwd(q, k, v, do, o32, lse, sink, *, branch, B, n_ctx, window, name):
    T = q.shape[1]
    Tp = T // B
    nq = Tp // Q_BLOCK
    has_sink = sink is not None
    q_in, kv_in, q_out, kv_out = _attn_specs(Tp, branch)
    lse_spec = pl.BlockSpec((None, N_KV * QROWS, 1), lambda b, i: (b * nq + i, 0, 0))
    sink_spec = pl.BlockSpec((None, 8, LANES), lambda b, i: (b, 0, 0))

    def body(*refs):
        if has_sink:
            sink_ref, q_ref, k_ref, v_ref, do_ref, o_ref, lse_ref, dq_ref, dk_ref, dv_ref, ds_ref = refs
        else:
            q_ref, k_ref, v_ref, do_ref, o_ref, lse_ref, dq_ref, dk_ref, dv_ref = refs
        i = pl.program_id(1)

        @pl.when(i == 0)
        def _():
            dk_ref[...] = jnp.zeros_like(dk_ref)
            dv_ref[...] = jnp.zeros_like(dv_ref)
            if has_sink:
                ds_ref[...] = jnp.zeros_like(ds_ref)

        def run(latent):
            dqs = []
            upd = jnp.zeros((8, LANES), F32)
            do_blocks, o_blocks = _lane_blocks(do_ref), _lane_blocks(o_ref)
            for j in range(N_KV):
                qv = (_stack_heads(_lane_blocks(q_ref), j) * SCORE_SCALE).astype(BF16)
                dov = _stack_heads(do_blocks, j).astype(BF16)
                delta = jnp.sum(_stack_heads(lambda m: do_blocks(m) * o_blocks(m), j), axis=-1, keepdims=True)
                lse_j = lse_ref[j * QROWS:(j + 1) * QROWS, :]
                dq = jnp.zeros((QROWS, LANES), F32)
                for start, size, masked in _key_chunks(i, latent, n_ctx=n_ctx, t_all=Tp, window=window,
                                                       chunk=_attn_chunk(Tp), latent_chunk=BWD_LATENT_CHUNK):
                    rows = pl.ds(start, size)
                    p = jnp.exp(_scores(qv, k_ref, i, start, size, masked, n_ctx=n_ctx) - lse_j)
                    dp = lax.dot_general(dov, v_ref[rows, :], NT_DIMS, preferred_element_type=F32)
                    ds = (p * (dp - delta)).astype(BF16)
                    dq = dq + jnp.dot(ds, k_ref[rows, :], preferred_element_type=F32)
                    dk_ref[rows, :] += lax.dot_general(ds, qv, TN_DIMS, preferred_element_type=F32)
                    dv_ref[rows, :] += lax.dot_general(p.astype(BF16), dov, TN_DIMS, preferred_element_type=F32)
                dqs.append(dq * SCORE_SCALE)
                if has_sink:
                    contrib = -(jnp.exp(_sink_column(sink_ref, j) - lse_j) * delta)
                    r = lax.broadcasted_iota(jnp.int32, (QROWS, 1), 0)
                    row8 = lax.broadcasted_iota(jnp.int32, (8, LANES), 0)
                    for h in range(N_GROUP):
                        in_head = (r >= h * Q_BLOCK) & (r < (h + 1) * Q_BLOCK)
                        tot = jnp.sum(jnp.where(in_head, contrib, 0.0), axis=0, keepdims=True)
                        upd = upd + jnp.where(row8 == j * N_GROUP + h, tot, 0.0)
            _unstack_heads(dqs, dq_ref)
            if has_sink:
                ds_ref[...] += upd

        @pl.when(i < n_ctx // Q_BLOCK)
        def _():
            run(False)

        @pl.when(i >= n_ctx // Q_BLOCK)
        def _():
            run(True)

    ins, specs = [q, k, v, do, o32, lse], [q_in, kv_in, kv_in, q_out, q_out, lse_spec]
    out_specs = [q_out, kv_out, kv_out]
    out_shape = [_sds((T, Q_WIDTH), F32), _sds((T, KV_WIDTH), F32), _sds((T, KV_WIDTH), F32)]
    if has_sink:
        ins, specs = [sink] + ins, [pl.BlockSpec(memory_space=pltpu.SMEM)] + specs
        out_specs.append(sink_spec)
        out_shape.append(_sds((B, 8, LANES), F32))
    return pl.pallas_call(
        body, name=name, grid=(B, Tp // Q_BLOCK), in_specs=specs, out_specs=out_specs, out_shape=out_shape,
        compiler_params=_params(("parallel", "arbitrary")),
    )(*ins)


def _window_sums(xp):
    n = xp.shape[0]

    def ahead(a, k):
        return pltpu.roll(a, n - k, 0)
    a2 = xp + ahead(xp, 1)
    a4 = a2 + ahead(a2, 2)
    a8 = a4 + ahead(a4, 4)
    a16 = a8 + ahead(a8, 8)
    return (a2, a4, a8, a16)


def _by_group(vals):
    lane = lax.broadcasted_iota(jnp.int32, vals[0].shape, 1)
    return jnp.where(lane < POOL_CH, vals[0], jnp.where(lane < 2 * POOL_CH, vals[1],
                     jnp.where(lane < 3 * POOL_CH, vals[2], vals[3])))


def _pool_counts(n):
    t = lax.broadcasted_iota(jnp.int32, (n, POOL_WIDTH), 0)
    cnts = [(jnp.minimum(t + w // 2, n) - jnp.maximum(t - w // 2, 0)).astype(F32) for w in POOL_WINDOWS]
    return _by_group(cnts)


def _pad_rows(x):
    zeros = jnp.zeros((POOL_PAD, x.shape[1]), x.dtype)
    return jnp.concatenate([zeros, x, zeros], axis=0)


def _pool_stream(u):
    n = u.shape[0]
    sums = _window_sums(_pad_rows(u))
    tots = [pltpu.roll(a, w // 2, 0)[POOL_PAD:POOL_PAD + n] for a, w in zip(sums, POOL_WINDOWS)]
    return _by_group(tots) / _pool_counts(n) - u


def _pool_stream_t(dp):
    n = dp.shape[0]
    sums = _window_sums(_pad_rows(dp / _pool_counts(n)))
    tots = [pltpu.roll(a, w // 2 - 1, 0)[POOL_PAD:POOL_PAD + n] if w > 2 else a[POOL_PAD:POOL_PAD + n]
            for a, w in zip(sums, POOL_WINDOWS)]
    return _by_group(tots) - dp


def _pool_fwd(z, w_bd, scale, *, B, Tp, n_ctx, name):
    T = z.shape[0]
    blk = pl.BlockSpec((Tp, POOL_WIDTH), lambda b: (b, U_COL // POOL_WIDTH))
    out = pl.BlockSpec((Tp, POOL_WIDTH), lambda b: (b, 0))

    def body(u_ref, w_ref, s_ref, p_ref, o_ref):
        for lo, hi in ((0, n_ctx), (n_ctx, Tp)):
            pooled = _pool_stream(u_ref[lo:hi, :]).astype(BF16)
            p_ref[lo:hi, :] = pooled
            mixed = jnp.dot(pooled, w_ref[...], preferred_element_type=F32)
            o_ref[lo:hi, :] = (mixed * s_ref[...]).astype(BF16)

    return pl.pallas_call(
        body, name=name, grid=(B,),
        in_specs=[blk, pl.BlockSpec((POOL_WIDTH, POOL_WIDTH), lambda b: (0, 0)), pl.BlockSpec((1, POOL_WIDTH), lambda b: (0, 0))],
        out_specs=[out, out], out_shape=[_sds((T, POOL_WIDTH), BF16)] * 2, compiler_params=_params(("parallel",)),
    )(z, w_bd, scale)


def _pool_bwd(d_ob, pooled, w_bd, scale, *, B, Tp, n_ctx, name):
    T = d_ob.shape[0]
    blk = pl.BlockSpec((Tp, POOL_WIDTH), lambda b: (b, 0))
    wsp = pl.BlockSpec((POOL_WIDTH, POOL_WIDTH), lambda b: (0, 0))
    ssp = pl.BlockSpec((1, POOL_WIDTH), lambda b: (0, 0))

    def body(d_ref, p_ref, w_ref, s_ref, du_ref, dw_ref, dsc_ref):
        @pl.when(pl.program_id(0) == 0)
        def _():
            dw_ref[...] = jnp.zeros_like(dw_ref)
            dsc_ref[...] = jnp.zeros_like(dsc_ref)

        dv, pv, wv = d_ref[...], p_ref[...], w_ref[...]
        mixed = jnp.dot(pv, wv, preferred_element_type=F32)
        dsc_ref[...] += jnp.sum(dv * mixed, axis=0, keepdims=True)
        dmixed = (dv * s_ref[...]).astype(BF16)
        dw_ref[...] += lax.dot_general(pv, dmixed, TN_DIMS, preferred_element_type=F32)
        dpooled = lax.dot_general(dmixed, wv, NT_DIMS, preferred_element_type=F32)
        for lo, hi in ((0, n_ctx), (n_ctx, Tp)):
            du_ref[lo:hi, :] = _pool_stream_t(dpooled[lo:hi, :]).astype(BF16)

    return pl.pallas_call(
        body, name=name, grid=(B,), in_specs=[blk, blk, wsp, ssp], out_specs=[blk, wsp, ssp],
        out_shape=[_sds((T, POOL_WIDTH), BF16), _sds((POOL_WIDTH, POOL_WIDTH), F32), _sds((1, POOL_WIDTH), F32)],
        compiler_params=_params(("arbitrary",)),
    )(d_ob, pooled, w_bd, scale)


def _merge_specs(z, D, TR, tc, wa, wb, wc):
    def act(width):
        return pl.BlockSpec((TR, width), lambda i, n: (i, 0))

    def gate(part):
        return pl.BlockSpec((TR, tc), lambda i, n: (i, (GATE_COL + part * D) // tc + n))
    w_specs = [w.spec(w.shape[0], tc, lambda i, n: (0, n)) for w in (wa, wb, wc)]
    return [act(Q_WIDTH), act(POOL_WIDTH), act(Q_WIDTH), gate(0), gate(1), gate(2)] + w_specs


def _merge_fwd(oa, ob, oc, z, wa, wb, wc, *, D, TR, name):
    T = oa.shape[0]
    tc = D // N_CHIPS

    def body(oa_ref, ob_ref, oc_ref, ga_ref, gb_ref, gc_ref, wa_ref, wb_ref, wc_ref, y_ref):
        acc = jax.nn.sigmoid(ga_ref[...]) * jnp.dot(oa_ref[...], wa_ref[...], preferred_element_type=F32)
        acc += jax.nn.sigmoid(gb_ref[...]) * jnp.dot(ob_ref[...], wb_ref[...], preferred_element_type=F32)
        acc += jax.nn.sigmoid(gc_ref[...]) * jnp.dot(oc_ref[...], wc_ref[...], preferred_element_type=F32)
        y_ref[...] = acc.astype(BF16)

    return pl.pallas_call(
        body, name=name, grid=(T // TR, D // tc), in_specs=_merge_specs(z, D, TR, tc, wa, wb, wc),
        out_specs=pl.BlockSpec((TR, tc), lambda i, n: (i, n)), out_shape=_sds((T, D), BF16),
        compiler_params=_params(("parallel", "parallel")),
    )(oa, ob, oc, z, z, z, wa.arr, wb.arr, wc.arr)


def _merge_bwd(dy, oa, ob, oc, z, wa, wb, wc, *, D, TR, name):
    T = oa.shape[0]
    tc = D // N_CHIPS
    out = pl.BlockSpec((TR, tc), lambda i, n: (i, n))

    def body(dy_ref, oa_ref, ob_ref, oc_ref, ga_ref, gb_ref, gc_ref, wa_ref, wb_ref, wc_ref,
             dpa_ref, dpb_ref, dpc_ref, dga_ref, dgb_ref, dgc_ref):
        dyv = dy_ref[...]
        for o_ref, g_ref, w_ref, dp_ref, dg_ref in ((oa_ref, ga_ref, wa_ref, dpa_ref, dga_ref),
                                                    (ob_ref, gb_ref, wb_ref, dpb_ref, dgb_ref),
                                                    (oc_ref, gc_ref, wc_ref, dpc_ref, dgc_ref)):
            s = jax.nn.sigmoid(g_ref[...])
            proj = jnp.dot(o_ref[...], w_ref[...], preferred_element_type=F32)
            dp_ref[...] = (dyv * s).astype(BF16)
            dg_ref[...] = (dyv * proj * (s * (1.0 - s))).astype(BF16)

    return pl.pallas_call(
        body, name=name, grid=(T // TR, D // tc), in_specs=[out] + _merge_specs(z, D, TR, tc, wa, wb, wc),
        out_specs=[out] * 6, out_shape=[_sds((T, D), BF16)] * 6, compiler_params=_params(("parallel", "parallel")),
    )(dy, oa, ob, oc, z, z, z, wa.arr, wb.arr, wc.arr)


def _silu_rows(cc, name):
    def body(c_ref, s_ref):
        v = c_ref[...]
        s_ref[...] = (v * jax.nn.sigmoid(v)).astype(BF16)
    return pl.pallas_call(body, name=name, out_shape=_sds(cc.shape, BF16))(cc)


def _ada_bwd_rows(dm, ds, cc, name):
    def body(dm_ref, ds_ref, c_ref, db_ref, dc_ref):
        db_ref[...] = jnp.sum(dm_ref[...], axis=0, keepdims=True)
        v = c_ref[...]
        s = jax.nn.sigmoid(v)
        dc_ref[...] = ds_ref[...] * (s * (1.0 + v * (1.0 - s)))
    return pl.pallas_call(body, name=name, out_shape=[_sds((1, dm.shape[1]), F32), _sds(cc.shape, F32)])(dm, ds, cc)


def _row_tile(rows, cols):
    for t in (512, 256, 128, 64, 32, 16, 8):
        if rows % t == 0 and t * cols * 4 <= (1 << 20):
            return t
    return rows


def _add_own_layer(layers, landed, core, name):
    R, C = landed.shape
    tr = _row_tile(R, C)
    n_layers = len(layers)

    def body(c_ref, *refs):
        b_ref, o_ref, o16_ref = refs[n_layers:]
        for l in range(n_layers):
            @pl.when(c_ref[0] == l)
            def _(a_ref=refs[l]):
                tot = a_ref[...] + b_ref[...]
                o_ref[...] = tot
                o16_ref[...] = tot.astype(BF16)

    row = pl.BlockSpec((tr, C), lambda i, c: (i, 0))
    own = [pl.BlockSpec((tr, C), functools.partial(lambda i, c, l: (jnp.where(c[0] == l, i, 0), 0), l=l))
           for l in range(n_layers)]
    grid_spec = pltpu.PrefetchScalarGridSpec(num_scalar_prefetch=1, grid=(R // tr,),
                                             in_specs=own + [row], out_specs=[row, row])
    return pl.pallas_call(body, name=name, grid_spec=grid_spec, out_shape=[_sds((R, C), F32), _sds((R, C), BF16)],
                          compiler_params=_params(("arbitrary",)))(core, *layers, landed)


def _sum_chips(own, landed, chip, name):
    _, R, C = own.shape
    tr = _row_tile(R, C)

    def body(k_ref, a_ref, b_ref, o_ref):
        o_ref[...] = ((a_ref[...] + b_ref[0].astype(F32)) + b_ref[1].astype(F32)) + b_ref[2].astype(F32)

    grid_spec = pltpu.PrefetchScalarGridSpec(
        num_scalar_prefetch=1, grid=(R // tr,),
        in_specs=[pl.BlockSpec((None, tr, C), lambda i, k: (k[0], i, 0)), pl.BlockSpec((3, tr, C), lambda i, k: (0, i, 0))],
        out_specs=pl.BlockSpec((tr, C), lambda i, k: (i, 0)))
    return pl.pallas_call(body, name=name, grid_spec=grid_spec, out_shape=_sds((R, C), F32),
                          compiler_params=_params(("parallel",)))(chip, own, landed)


def _adam_math(w, g, m, v):
    m = ADAM_B1 * m + (1.0 - ADAM_B1) * g
    v = ADAM_B2 * v + (1.0 - ADAM_B2) * (g * g)
    m_hat = m / (1.0 - ADAM_B1 ** ADAM_STEP)
    v_hat = v / (1.0 - ADAM_B2 ** ADAM_STEP)
    delta = -ADAM_LR * (m_hat / (jnp.sqrt(v_hat) + ADAM_EPS) + ADAM_WD * w)
    return delta, m, v


def _adamw(w, mine, other, m, v, core, name):
    L, R, C = w.shape
    tr = _row_tile(R, C)

    def body(c_ref, w_ref, a_ref, b_ref, m_ref, v_ref, g_ref, d_ref, mo_ref, vo_ref):
        def step(g):
            d, mn, vn = _adam_math(w_ref[...], g, m_ref[...], v_ref[...])
            g_ref[...] = g
            d_ref[...] = d
            mo_ref[...] = mn
            vo_ref[...] = vn

        @pl.when(pl.program_id(0) == c_ref[0])
        def _():
            step(a_ref[...])

        @pl.when(pl.program_id(0) != c_ref[0])
        def _():
            step(b_ref[...])

    lay = pl.BlockSpec((None, tr, C), lambda l, i, c: (l, i, 0))
    row = pl.BlockSpec((tr, C), lambda l, i, c: (i, 0))
    grid_spec = pltpu.PrefetchScalarGridSpec(num_scalar_prefetch=1, grid=(L, R // tr),
                                             in_specs=[lay, row, row, lay, lay], out_specs=[lay] * 4)
    return pl.pallas_call(body, name=name, grid_spec=grid_spec, out_shape=[_sds((L, R, C), F32)] * 4,
                          compiler_params=_params(("parallel", "parallel")))(core, w, mine, other, m, v)


def _adamw_small(w, parts, m, v, name):
    R, C = w.shape

    def body(w_ref, p_ref, m_ref, v_ref, g_ref, d_ref, mo_ref, vo_ref):
        g = p_ref[0]
        for dev in range(1, 8):
            g = g + p_ref[dev]
        d, mn, vn = _adam_math(w_ref[...], g, m_ref[...], v_ref[...])
        g_ref[...] = g
        d_ref[...] = d
        mo_ref[...] = mn
        vo_ref[...] = vn

    return pl.pallas_call(body, name=name, out_shape=[_sds((R, C), F32)] * 4)(w, parts, m, v)


def _place():
    return lax.axis_index("x"), lax.axis_index("y"), lax.axis_index("c")


def _other_chips(x, y):
    return [(1 - x, y), (x, 1 - y), (1 - x, 1 - y)]


def _rcopy(src, dst, ssem, rsem, dev):
    return pltpu.make_async_remote_copy(src_ref=src, dst_ref=dst, send_sem=ssem, recv_sem=rsem,
                                        device_id=dev, device_id_type=MESH)


def _gather_weights(shards, name):
    n = len(shards)

    def body(*refs):
        src, out = refs[:n], refs[n:2 * n]
        send_sems, recv_sems = refs[2 * n:]
        x, y, c = _place()
        sibling = (x, y, 1 - c)
        chips = _other_chips(x, y)
        mine = 2 * x + y
        first = [_rcopy(src[w].at[c], out[w].at[mine, c], send_sems.at[w, j], recv_sems.at[w, j], (*chip, c))
                 for w in range(n) for j, chip in enumerate(chips)]
        for cp in first:
            cp.start()
        passed = []
        for w in range(n):
            for j, (px, py) in enumerate(chips):
                landed = out[w].at[2 * px + py, c]
                _rcopy(landed, landed, send_sems.at[w, j], recv_sems.at[w, j], (px, py, c)).wait_recv()
                cp = _rcopy(landed, landed, send_sems.at[w, 3 + j], recv_sems.at[w, 3 + j], sibling)
                cp.start()
                passed.append(cp)
        for w in range(n):
            for j, (px, py) in enumerate(chips):
                landed = out[w].at[2 * px + py, 1 - c]
                _rcopy(landed, landed, send_sems.at[w, 3 + j], recv_sems.at[w, 3 + j], sibling).wait_recv()
        for cp in first + passed:
            cp.wait_send()

    landed = pl.pallas_call(
        body, name=name, in_specs=[ANY] * n, out_specs=[ANY] * n,
        out_shape=[_sds((N_CHIPS,) + s.shape, s.dtype) for s in shards],
        scratch_shapes=[pltpu.SemaphoreType.DMA((n, 6)), pltpu.SemaphoreType.DMA((n, 6))],
    )(*shards)
    mine = 2 * lax.axis_index("x") + lax.axis_index("y")
    return [lax.dynamic_update_index_in_dim(g, s, mine, 0) for g, s in zip(landed, shards)]


def _send_other_layer(layer0, layer1, name):
    n = len(layer0)

    def body(*refs):
        src0, src1, out = refs[:n], refs[n:2 * n], refs[2 * n:3 * n]
        send_sems, recv_sems = refs[3 * n:]
        x, y, c = _place()

        def copies(src):
            return [_rcopy(src[w], out[w], send_sems.at[w], recv_sems.at[w], (x, y, 1 - c)) for w in range(n)]

        @pl.when(c == 0)
        def _():
            for cp in copies(src1):
                cp.start()

        @pl.when(c == 1)
        def _():
            for cp in copies(src0):
                cp.start()

        for cp in copies(src0):
            cp.wait_recv()
        for cp in copies(src0):
            cp.wait_send()

    return pl.pallas_call(
        body, name=name, in_specs=[ANY] * (2 * n), out_specs=[ANY] * n,
        out_shape=[_sds(s.shape, s.dtype) for s in layer0],
        scratch_shapes=[pltpu.SemaphoreType.DMA((n,)), pltpu.SemaphoreType.DMA((n,))],
    )(*layer0, *layer1)


def _send_chip_blocks(blocked, name):
    n = len(blocked)

    def body(*refs):
        src, out = refs[:n], refs[n:2 * n]
        send_sems, recv_sems = refs[2 * n:]
        x, y, c = _place()
        cps = [_rcopy(src[w].at[2 * px + py], out[w].at[j], send_sems.at[w, j], recv_sems.at[w, j], (px, py, c))
               for w in range(n) for j, (px, py) in enumerate(_other_chips(x, y))]
        for cp in cps:
            cp.start()
        for cp in cps:
            cp.wait_recv()
        for cp in cps:
            cp.wait_send()

    return pl.pallas_call(
        body, name=name, in_specs=[ANY] * n, out_specs=[ANY] * n,
        out_shape=[_sds((3,) + s.shape[1:], s.dtype) for s in blocked],
        scratch_shapes=[pltpu.SemaphoreType.DMA((n, 3)), pltpu.SemaphoreType.DMA((n, 3))],
    )(*blocked)


def _share_layers(reduced, name):
    n = len(reduced)

    def body(*refs):
        src, out = refs[:n], refs[n:2 * n]
        send_sems, recv_sems = refs[2 * n:]
        x, y, c = _place()
        cps = [_rcopy(src[w], out[w], send_sems.at[w], recv_sems.at[w], (x, y, 1 - c)) for w in range(n)]
        for cp in cps:
            cp.start()
        for cp in cps:
            cp.wait_recv()
        for cp in cps:
            cp.wait_send()

    return pl.pallas_call(
        body, name=name, in_specs=[ANY] * n, out_specs=[ANY] * n,
        out_shape=[_sds(s.shape, s.dtype) for s in reduced],
        scratch_shapes=[pltpu.SemaphoreType.DMA((n,)), pltpu.SemaphoreType.DMA((n,))],
    )(*reduced)


def _gather_small(block, name):
    m_per, n = block.shape

    def body(x_ref, out_ref, send_sems, recv_sems, local_sem):
        x, y, c = _place()
        me, sibling = (x, y, c), (x, y, 1 - c)
        chips = _other_chips(x, y)

        def rows(px, py, pc):
            return out_ref.at[pl.ds((4 * px + 2 * py + pc) * m_per, m_per), :]

        def copy(k, blk, to, src=None):
            return _rcopy(rows(*blk) if src is None else src, rows(*blk), send_sems.at[k], recv_sems.at[k], to)

        mine = pltpu.make_async_copy(x_ref, rows(*me), local_sem)
        mine.start()
        first = [copy(0, me, sibling, src=x_ref)]
        first += [copy(1 + j, me, (*chip, c), src=x_ref) for j, chip in enumerate(chips)]
        for cp in first:
            cp.start()
        passed = [copy(4 + j, (*chip, c), sibling) for j, chip in enumerate(chips)]
        for j, chip in enumerate(chips):
            copy(1 + j, (*chip, c), me).wait_recv()
            passed[j].start()
        copy(0, sibling, me).wait_recv()
        for j, chip in enumerate(chips):
            copy(4 + j, (*chip, 1 - c), me).wait_recv()
        for cp in first + passed:
            cp.wait_send()
        mine.wait()

    return pl.pallas_call(
        body, name=name, out_shape=_sds((8 * m_per, n), block.dtype),
        in_specs=[pl.BlockSpec(memory_space=pltpu.VMEM)], out_specs=pl.BlockSpec(memory_space=pltpu.VMEM),
        scratch_shapes=[pltpu.SemaphoreType.DMA((7,)), pltpu.SemaphoreType.DMA((7,)), pltpu.SemaphoreType.DMA],
    )(block)


def _rope_tables(n_ctx, seq):
    rows = seq // GRID_W
    r = jnp.repeat(jnp.arange(rows, dtype=F32), GRID_W)
    col = jnp.tile(jnp.arange(GRID_W, dtype=F32), rows)
    inv = 1.0 / (ROPE_THETA ** (jnp.arange(0, AXIS_DIM, 2, dtype=F32) / AXIS_DIM))
    ang = jnp.concatenate([r[:, None] * inv, col[:, None] * inv], axis=-1)
    cos = jnp.repeat(jnp.cos(ang), 2, axis=-1)
    sin = jnp.repeat(jnp.sin(ang), 2, axis=-1) * jnp.tile(jnp.array([-1.0, 1.0], F32), HEAD_DIM // 2)
    cos = jnp.concatenate([jnp.ones((n_ctx, HEAD_DIM), F32), cos], axis=0)
    sin = jnp.concatenate([jnp.zeros((n_ctx, HEAD_DIM), F32), sin], axis=0)
    return jnp.tile(cos, (1, 2)), jnp.tile(sin, (1, 2))


def _block_diag(w_pool_l):
    out = jnp.zeros((POOL_WIDTH, POOL_WIDTH), w_pool_l.dtype)
    for g in range(w_pool_l.shape[0]):
        out = out.at[g * POOL_CH:(g + 1) * POOL_CH, g * POOL_CH:(g + 1) * POOL_CH].set(w_pool_l[g])
    return out


def _local_step(x, c, ctx, c_ctx, small, gw, target):
    B, S, D = x.shape
    N = ctx.shape[1]
    L = small["norm1"].shape[0]
    Tp = N + S
    T = B * Tp
    TR = N
    P = Tp // N
    rows16 = 16
    assert N % Q_BLOCK == 0 and S % N == 0 and B + 1 <= rows16
    TM = _tile(T, (1024, 768, 512, 384, 256, 128))
    TMG = _tile(T, (512, 384, 256, 128))

    X = jnp.concatenate([ctx, x], axis=1).reshape(T, D)
    cc = jnp.zeros((rows16, D), F32).at[:B].set(c).at[B].set(c_ctx)
    s_rows = _silu_rows(cc, "silu_rows")
    cos, sin = _rope_tables(N, S)

    def weights(l):
        return dict(
            ada=_Opnd(gw["w_ada"], "bcols", l), w_in=_Opnd(gw["w_in"], "bcols", l),
            a=_Opnd(gw["w_br_a"], "bcols", l), b=_Opnd(gw["w_br_b"], "bcols", l), c=_Opnd(gw["w_br_c"], "bcols", l),
            out=_Opnd(gw["w_out"], "brows", l), mlp1=_Opnd(gw["w_mlp1"], "bcols", l), mlp2=_Opnd(gw["w_mlp2"], "brows", l))

    IN = weights(0)["w_in"].shape[1]
    DFF = weights(0)["mlp1"].shape[1]
    tn_in = _tile(IN // N_CHIPS, (1152, 768, 512, 384, 256, 128))
    tn_ff = _tile(DFF // N_CHIPS, (1024, 512, 256, 128))
    tn_ada = _tile(6 * D // N_CHIPS, (1536, 768, 512, 256, 128))
    tn_d = D // N_CHIPS
    tk_d = _tile(D, (512,))
    tk_tok = _tile(T, (2304, 1536, 1024, 768, 512, 384, 256))

    saved = []
    xin, pending = X, None
    for l in range(L):
        W = weights(l)
        b_ada = small["b_ada"][l].reshape(1, 6 * D)
        mod = _matmul(s_rows, W["ada"], "nn", tm=rows16, tn=tn_ada, tk=D, name=f"ada_fwd{l}",
                      epilogue=lambda acc, b: (acc + b,), extras=[(b_ada, (1, tn_ada), lambda m, n: (0, n))])
        modtab = jnp.stack([jnp.broadcast_to(mod[B], (B, 6 * D)), mod[:B]], axis=1).reshape(2 * B, 1, 6 * D)
        ones = jnp.ones((LANES,), F32)
        gains = jnp.stack([
            jnp.concatenate([jnp.tile(small[q][l], 2)] * 3 + [jnp.tile(small[k][l], 2), ones])
            for q, k in (("q_norm_a", "k_norm_a"), ("q_norm_c", "k_norm_c"))]).reshape(2, 1, QKV_WIDTH)
        w_bd = _block_diag(small["w_pool"][l]).astype(BF16)
        p_scale = small["pool_scale"][l].reshape(1, POOL_WIDTH)
        sink = small["sink_c"][l]

        x0, h1 = _res_norm(xin, pending, modtab, 0, 1, small["norm1"][l][None], TR=TR, P=P, name=f"norm1_fwd{l}")
        z = _matmul(h1, W["w_in"], "nn", tm=TM, tn=tn_in, tk=D, name=f"in_proj{l}")
        q2, k2, v2 = _qk_prep(z, gains, cos, sin, TR=TR, P=P, name=f"qk_prep{l}")
        oa, oa32, lse_a = _attn_fwd(q2, k2, v2, None, branch=0, B=B, n_ctx=N, window=False, name=f"attn_a_fwd{l}")
        oc, oc32, lse_c = _attn_fwd(q2, k2, v2, sink, branch=1, B=B, n_ctx=N, window=True, name=f"attn_c_fwd{l}")
        pooled, ob = _pool_fwd(z, w_bd, p_scale, B=B, Tp=Tp, n_ctx=N, name=f"pool_fwd{l}")
        y = _merge_fwd(oa, ob, oc, z, W["a"], W["b"], W["c"], D=D, TR=TMG, name=f"merge_fwd{l}")
        ao = _matmul(y, W["out"], "nn", tm=TM, tn=D, tk=tn_d, name=f"out_proj{l}")
        x1, h2 = _res_norm(x0, (ao, modtab, 2), modtab, 3, 4, small["norm2"][l][None], TR=TR, P=P, name=f"norm2_fwd{l}")
        a_pre, r_act = _matmul(h2, W["mlp1"], "nn", tm=TM, tn=tn_ff, tk=D, name=f"mlp1_fwd{l}", out_dtypes=(F32, BF16),
                               epilogue=lambda acc: (acc, jnp.square(jnp.maximum(acc, 0.0))))
        mo = _matmul(r_act, W["mlp2"], "nn", tm=TM, tn=D, tk=tn_ff, name=f"mlp2_fwd{l}")
        saved.append(dict(modtab=modtab, gains=gains, w_bd=w_bd, p_scale=p_scale, sink=sink, x0=x0, h1=h1, z=z,
                          q2=q2, k2=k2, v2=v2, oa=oa, ob=ob, oc=oc, oa32=oa32, oc32=oc32, lse_a=lse_a, lse_c=lse_c,
                          pooled=pooled, y=y, ao=ao,
                          x1=x1, h2=h2, a_pre=a_pre, r_act=r_act, mo=mo))
        xin, pending = x1, (mo, modtab, 5)

    dxo, loss = _loss_head(xin, pending[0], pending[1], 5, target.reshape(B * S, D), TR=TR, P=P, name="loss_head")

    big = {k: [None] * L for k in gw}
    sm = {k: [None] * L for k in ("b_ada", "norm1", "norm2", "q_norm_a", "k_norm_a", "q_norm_c", "k_norm_c",
                                   "sink_c", "w_pool", "pool_scale")}
    d_cctx = jnp.zeros((D,), F32)
    for l in reversed(range(L)):
        W, sv = weights(l), saved[l]
        modtab = sv["modtab"]
        d_mo, dg2 = _gate_bwd(dxo, sv["mo"], modtab, 5, TR=TR, P=P, name=f"gate2_bwd{l}")
        d_a = _matmul(d_mo, W["mlp2"], "nt", tm=TM, tn=tn_ff, tk=D, name=f"mlp2_bwd{l}", out_dtypes=(BF16,),
                      epilogue=lambda acc, a: (acc * (2.0 * jnp.maximum(a, 0.0)),),
                      extras=[(sv["a_pre"], (TM, tn_ff), lambda m, n: (m, n))])
        big["w_mlp2"][l] = _matmul(sv["r_act"], d_mo, "tn", tm=tk_d, tn=D, tk=tk_tok,
                                   name=f"mlp2_dw{l}").reshape(N_CHIPS, DFF // N_CHIPS, D)
        d_h2 = _matmul(d_a, W["mlp1"], "nt", tm=TM, tn=D, tk=tn_ff, name=f"mlp1_bwd{l}")
        big["w_mlp1"][l] = _matmul(sv["h2"], d_a, "tn", tm=tk_d, tn=tn_ff, tk=tk_tok, name=f"mlp1_dw{l}", out_blocked=True)
        dx1, dsh2, dsc2, dn2 = _norm_bwd(sv["x1"], d_h2, dxo, modtab, 4, small["norm2"][l][None], TR=TR, P=P,
                                         name=f"norm2_bwd{l}")
        d_ao, dg1 = _gate_bwd(dx1, sv["ao"], modtab, 2, TR=TR, P=P, name=f"gate1_bwd{l}")
        d_y = _matmul(d_ao, W["out"], "nt", tm=TM, tn=tn_d, tk=D, name=f"out_bwd{l}")
        big["w_out"][l] = _matmul(sv["y"], d_ao, "tn", tm=tk_d, tn=D, tk=tk_tok,
                                  name=f"out_dw{l}").reshape(N_CHIPS, D // N_CHIPS, D)
        d_pa, d_pb, d_pc, d_ga, d_gb, d_gc = _merge_bwd(d_y, sv["oa"], sv["ob"], sv["oc"], sv["z"], W["a"], W["b"], W["c"],
                                                        D=D, TR=TMG, name=f"merge_bwd{l}")
        d_oa = _matmul(d_pa, W["a"], "nt", tm=TM, tn=Q_WIDTH, tk=tn_d, name=f"br_a_bwd{l}", out_dtypes=(BF16,))
        d_ob = _matmul(d_pb, W["b"], "nt", tm=TM, tn=POOL_WIDTH, tk=tn_d, name=f"br_b_bwd{l}")
        d_oc = _matmul(d_pc, W["c"], "nt", tm=TM, tn=Q_WIDTH, tk=tn_d, name=f"br_c_bwd{l}", out_dtypes=(BF16,))
        big["w_br_a"][l] = _matmul(sv["oa"], d_pa, "tn", tm=Q_WIDTH, tn=tn_d, tk=tk_tok, name=f"br_a_dw{l}", out_blocked=True)
        big["w_br_b"][l] = _matmul(sv["ob"], d_pb, "tn", tm=POOL_WIDTH, tn=tn_d, tk=tk_tok, name=f"br_b_dw{l}", out_blocked=True)
        big["w_br_c"][l] = _matmul(sv["oc"], d_pc, "tn", tm=Q_WIDTH, tn=tn_d, tk=tk_tok, name=f"br_c_dw{l}", out_blocked=True)
        d_u, d_wbd, d_ps = _pool_bwd(d_ob, sv["pooled"], sv["w_bd"], sv["p_scale"], B=B, Tp=Tp, n_ctx=N, name=f"pool_bwd{l}")
        dqa, dka, dva = _attn_bwd(sv["q2"], sv["k2"], sv["v2"], d_oa, sv["oa32"], sv["lse_a"], None, branch=0, B=B,
                                  n_ctx=N, window=False, name=f"attn_a_bwd{l}")
        dqc, dkc, dvc, dsink = _attn_bwd(sv["q2"], sv["k2"], sv["v2"], d_oc, sv["oc32"], sv["lse_c"], sv["sink"],
                                         branch=1, B=B, n_ctx=N, window=True, name=f"attn_c_bwd{l}")
        dz_a, dgains_a = _qk_prep_bwd(sv["z"], dqa, dka, dva, sv["gains"], cos, sin, branch=0, TR=TR, P=P,
                                      name=f"qk_prep_a_bwd{l}")
        dz_c, dgains_c = _qk_prep_bwd(sv["z"], dqc, dkc, dvc, sv["gains"], cos, sin, branch=1, TR=TR, P=P,
                                      name=f"qk_prep_c_bwd{l}")
        dz = jnp.concatenate([dz_a, dz_c, d_u, d_ga, d_gb, d_gc], axis=1)
        d_h1 = _matmul(dz, W["w_in"], "nt", tm=TM, tn=D, tk=tn_in, name=f"in_bwd{l}")
        big["w_in"][l] = _matmul(sv["h1"], dz, "tn", tm=tk_d, tn=tn_in, tk=tk_tok, name=f"in_dw{l}", out_blocked=True)
        dx0, dsh1, dsc1, dn1 = _norm_bwd(sv["x0"], d_h1, dx1, modtab, 1, small["norm1"][l][None], TR=TR, P=P,
                                         name=f"norm1_bwd{l}")

        dm_groups = jnp.concatenate([dsh1, dsc1, dg1, dsh2, dsc2, dg2], axis=-1).reshape(B, 2, 6 * D)
        dm = jnp.zeros((rows16, 6 * D), F32).at[:B].set(dm_groups[:, 1]).at[B].set(jnp.sum(dm_groups[:, 0], axis=0))
        dm_bf = dm.astype(BF16)
        d_s = _matmul(dm_bf, W["ada"], "nt", tm=rows16, tn=D, tk=tn_ada, name=f"ada_bwd{l}")
        big["w_ada"][l] = _matmul(s_rows, dm_bf, "tn", tm=tk_d, tn=tn_ada, tk=rows16, name=f"ada_dw{l}", out_blocked=True)
        db_ada, dcc = _ada_bwd_rows(dm, d_s, cc, f"ada_rows_bwd{l}")
        d_cctx = d_cctx + dcc[B]

        sm["b_ada"][l] = db_ada[0]
        sm["norm1"][l] = jnp.sum(dn1, axis=(0, 1))
        sm["norm2"][l] = jnp.sum(dn2, axis=(0, 1))
        dgh = jnp.stack([dgains_a, dgains_c]).reshape(2, QKV_WIDTH // HEAD_DIM, HEAD_DIM)
        sm["q_norm_a"][l] = jnp.sum(dgh[0, :N_HEADS], axis=0)
        sm["k_norm_a"][l] = jnp.sum(dgh[0, N_HEADS:N_HEADS + N_KV], axis=0)
        sm["q_norm_c"][l] = jnp.sum(dgh[1, :N_HEADS], axis=0)
        sm["k_norm_c"][l] = jnp.sum(dgh[1, N_HEADS:N_HEADS + N_KV], axis=0)
        sm["sink_c"][l] = jnp.sum(dsink[:, :N_HEADS, 0], axis=0)
        sm["w_pool"][l] = jnp.stack([d_wbd[g * POOL_CH:(g + 1) * POOL_CH, g * POOL_CH:(g + 1) * POOL_CH]
                                     for g in range(POOL_WIDTH // POOL_CH)])
        sm["pool_scale"][l] = d_ps[0]
        dxo = dx0

    grad_x = dxo.reshape(B, Tp, D)[:, N:]
    small_grads = {k: jnp.stack(v) for k, v in sm.items()}
    small_grads["c_ctx"] = d_cctx
    return loss, grad_x, small_grads, big


SMALL_NAMES = ("c_ctx", "b_ada", "norm1", "norm2", "q_norm_a", "k_norm_a", "q_norm_c", "k_norm_c", "sink_c",
               "w_pool", "pool_scale")
BIG_NAMES = ("w_ada", "w_in", "w_br_a", "w_br_b", "w_br_c", "w_out", "w_mlp1", "w_mlp2")
WEIGHT_NAMES = ("c_ctx", "w_ada", "b_ada", "norm1", "norm2", "w_in", "q_norm_a", "k_norm_a", "q_norm_c", "k_norm_c",
                "sink_c", "w_pool", "pool_scale", "w_br_a", "w_br_b", "w_br_c", "w_out", "w_mlp1", "w_mlp2")


def _pack(parts, rows):
    flat = jnp.concatenate([p.reshape(-1).astype(F32) for p in parts])
    return jnp.pad(flat, (0, rows * LANES - flat.shape[0])).reshape(rows, LANES)


def _unpack(packed, like):
    flat, out, at = packed.reshape(-1), [], 0
    for p in like:
        out.append(flat[at:at + p.size].reshape(p.shape))
        at += p.size
    return out


def _reduce_big(partials):
    names = list(partials)
    assert all(len(partials[k]) == 2 for k in names)
    x, y, c = _place()
    core = c.astype(jnp.int32).reshape(1)
    chip = (2 * x + y).astype(jnp.int32).reshape(1)
    shapes = [partials[k][0].shape for k in names]
    flat = [[g.reshape(-1, g.shape[-1]) for g in partials[k]] for k in names]
    landed = _send_other_layer([f[0] for f in flat], [f[1] for f in flat], "grads_to_sibling")
    in_chip = [_add_own_layer(f, r, core, f"grads_add_sibling_{k}") for k, f, r in zip(names, flat, landed)]
    blocked = [h.reshape(s) for s, (h, _) in zip(shapes, in_chip)]
    blocked16 = [h.reshape(s) for s, (_, h) in zip(shapes, in_chip)]
    from_chips = _send_chip_blocks(blocked16, "grads_to_chips")
    reduced = [_sum_chips(h, r, chip, f"grads_sum_chips_{k}") for k, h, r in zip(names, blocked, from_chips)]
    shared = _share_layers(reduced, "grads_share_layers")
    return core, dict(zip(names, zip(reduced, shared)))


def kernel(x, c, ctx, c_ctx, w_ada, b_ada, norm1, norm2, w_in, q_norm_a, k_norm_a, q_norm_c, k_norm_c, sink_c, w_pool, pool_scale, w_br_a, w_br_b, w_br_c, w_out, w_mlp1, w_mlp2, loss_target, m_c_ctx, m_w_ada, m_b_ada, m_norm1, m_norm2, m_w_in, m_q_norm_a, m_k_norm_a, m_q_norm_c, m_k_norm_c, m_sink_c, m_w_pool, m_pool_scale, m_w_br_a, m_w_br_b, m_w_br_c, m_w_out, m_w_mlp1, m_w_mlp2, v_c_ctx, v_w_ada, v_b_ada, v_norm1, v_norm2, v_w_in, v_q_norm_a, v_k_norm_a, v_q_norm_c, v_k_norm_c, v_sink_c, v_w_pool, v_pool_scale, v_w_br_a, v_w_br_b, v_w_br_c, v_w_out, v_w_mlp1, v_w_mlp2):
    given = dict(locals())
    w = {k: given[k] for k in WEIGHT_NAMES}
    m = {k: given["m_" + k] for k in WEIGHT_NAMES}
    v = {k: given["v_" + k] for k in WEIGHT_NAMES}

    gathered = _gather_weights([w[k].astype(BF16) for k in BIG_NAMES], "gather_weights")
    gw = dict(zip(BIG_NAMES, gathered))
    small = {k: w[k] for k in SMALL_NAMES}
    loss_part, grad_x, small_grads, big_grads = _local_step(x, c, ctx, c_ctx, small, gw, loss_target)

    core, reduced = _reduce_big({k: big_grads[k] for k in BIG_NAMES})
    grads, deltas, new_m, new_v = {}, {}, {}, {}
    for k in BIG_NAMES:
        mine, other = reduced[k]
        grads[k], deltas[k], new_m[k], new_v[k] = _adamw(w[k], mine, other, m[k], v[k], core, f"adamw_{k}")

    sizes = sum(w[k].size for k in SMALL_NAMES) + LANES
    rows = -(-sizes // (8 * LANES)) * 8
    parts = _gather_small(_pack([small_grads[k] for k in SMALL_NAMES] + [loss_part[0]], rows), "gather_small")
    zero = jnp.zeros((LANES,), F32)
    packed = [_pack([t[k] for k in SMALL_NAMES] + [zero], rows) for t in (w, m, v)]
    outs = _adamw_small(packed[0], parts.reshape(8, rows, LANES), packed[1], packed[2], "adamw_small")
    like = [w[k] for k in SMALL_NAMES] + [zero]
    for store, packed_out in zip((grads, deltas, new_m, new_v), outs):
        pieces = _unpack(packed_out, like)
        for k, piece in zip(SMALL_NAMES, pieces):
            store[k] = piece
        if store is grads:
            loss = pieces[-1][0]

    return (loss, grad_x, *[grads[k] for k in WEIGHT_NAMES], *[deltas[k] for k in WEIGHT_NAMES],
            *[new_m[k] for k in WEIGHT_NAMES], *[new_v[k] for k in WEIGHT_NAMES])
```

```python
import functools

import jax
import jax.numpy as jnp
from jax import lax
from jax.experimental import pallas as pl
from jax.experimental.pallas import tpu as pltpu

F32 = jnp.float32
BF16 = jnp.bfloat16

HEAD_DIM = 64
GRID_W = 64
AXIS_DIM = HEAD_DIM // 2
ROPE_THETA = 10000.0
N_HEADS = 6
N_KV = 2
N_GROUP = N_HEADS // N_KV
POOL_CH = 64
POOL_WIDTH = 256
POOL_WINDOWS = (2, 4, 8, 16)
WINDOW = 128
Q_BLOCK = 128
Q_WIDTH = N_HEADS * HEAD_DIM
KV_WIDTH = N_KV * HEAD_DIM
GATE_COL = 2 * (Q_WIDTH + 2 * KV_WIDTH) + POOL_WIDTH
U_COL = 2 * (Q_WIDTH + 2 * KV_WIDTH)
EPS = 1e-6
NEG = -1e30
ADAM_LR = 0.001
ADAM_B1 = 0.9
ADAM_B2 = 0.999
ADAM_EPS = 1e-08
ADAM_WD = 0.01
ADAM_STEP = 10

N_CHIPS = 4
LANES = 128
POOL_PAD = 16
VMEM_LIMIT = 48 * 1024 * 1024
MESH = pl.DeviceIdType.MESH
ANY = pl.BlockSpec(memory_space=pl.ANY)


def _params(sem):
    return pltpu.CompilerParams(dimension_semantics=sem, vmem_limit_bytes=VMEM_LIMIT)


def _sds(shape, dtype):
    return jax.ShapeDtypeStruct(tuple(shape), dtype)


class _Opnd:
    def __init__(self, arr, kind="plain", layer=None):
        self.arr, self.kind, self.layer = arr, kind, layer

    @property
    def shape(self):
        a = self.arr
        if self.kind == "plain":
            return a.shape
        if self.kind == "bcols":
            return (a.shape[2], N_CHIPS * a.shape[3])
        return (N_CHIPS * a.shape[2], a.shape[3])

    def spec(self, tr, tc, fn):
        a, layer = self.arr, self.layer
        if self.kind == "plain":
            return pl.BlockSpec((tr, tc), lambda *g: fn(*g))
        if self.kind == "bcols":
            assert a.shape[3] % tc == 0, (a.shape, tc)
            per = a.shape[3] // tc

            def im(*g):
                ri, ci = fn(*g)
                return (ci // per, layer, ri, ci % per)
            return pl.BlockSpec((None, None, tr, tc), im)
        assert a.shape[2] % tr == 0, (a.shape, tr)
        per = a.shape[2] // tr

        def im(*g):
            ri, ci = fn(*g)
            return (ri // per, layer, ri % per, ci)
        return pl.BlockSpec((None, None, tr, tc), im)


def _matmul(a, b, mode, *, tm, tn, tk, name, out_dtypes=(F32,), epilogue=None, extras=(), out_blocked=False):
    if not isinstance(a, _Opnd):
        a = _Opnd(a)
    if not isinstance(b, _Opnd):
        b = _Opnd(b)
    if mode == "nn":
        (M, K), (K2, N) = a.shape, b.shape
        a_spec = a.spec(tm, tk, lambda m, n, k: (m, k))
        b_spec = b.spec(tk, tn, lambda m, n, k: (k, n))
        dims = (((1,), (0,)), ((), ()))
    elif mode == "nt":
        (M, K), (N, K2) = a.shape, b.shape
        a_spec = a.spec(tm, tk, lambda m, n, k: (m, k))
        b_spec = b.spec(tn, tk, lambda m, n, k: (n, k))
        dims = (((1,), (1,)), ((), ()))
    else:
        (K, M), (K2, N) = a.shape, b.shape
        a_spec = a.spec(tk, tm, lambda m, n, k: (k, m))
        b_spec = b.spec(tk, tn, lambda m, n, k: (k, n))
        dims = (((0,), (0,)), ((), ()))
    assert K == K2 and M % tm == 0 and N % tn == 0 and K % tk == 0, (name, M, N, K, K2, tm, tn, tk)
    nk = K // tk
    n_extra = len(extras)
    n_out = len(out_dtypes)
    extra_specs = [pl.BlockSpec(bs, functools.partial(lambda m, n, k, f: f(m, n), f=f)) for (_, bs, f) in extras]
    if out_blocked:
        assert (N // N_CHIPS) % tn == 0
        per = (N // N_CHIPS) // tn
        out_shape = [_sds((N_CHIPS, M, N // N_CHIPS), dt) for dt in out_dtypes]
        out_specs = [pl.BlockSpec((None, tm, tn), lambda m, n, k: (n // per, m, n % per)) for _ in out_dtypes]
    else:
        out_shape = [_sds((M, N), dt) for dt in out_dtypes]
        out_specs = [pl.BlockSpec((tm, tn), lambda m, n, k: (m, n)) for _ in out_dtypes]

    in_place = nk > 1 and epilogue is None and out_dtypes[0] == F32

    def body(*refs):
        a_ref, b_ref = refs[0], refs[1]
        extra_refs = refs[2:2 + n_extra]
        out_refs = refs[2 + n_extra:2 + n_extra + n_out]
        acc_ref = out_refs[0] if in_place else (refs[2 + n_extra + n_out] if nk > 1 else None)
        k = pl.program_id(2)
        prod = lax.dot_general(a_ref[...].astype(BF16), b_ref[...].astype(BF16), dims, preferred_element_type=F32)

        def finish(acc):
            outs = epilogue(acc, *[r[...] for r in extra_refs]) if epilogue is not None else (acc,) * n_out
            for o_ref, o in zip(out_refs, outs):
                o_ref[...] = o.astype(o_ref.dtype)

        if nk == 1:
            finish(prod)
        elif in_place:
            @pl.when(k == 0)
            def _():
                acc_ref[...] = prod

            @pl.when(k > 0)
            def _():
                acc_ref[...] += prod

            if n_out > 1:
                @pl.when(k == nk - 1)
                def _():
                    for o_ref in out_refs[1:]:
                        o_ref[...] = acc_ref[...].astype(o_ref.dtype)
        else:
            @pl.when(k == 0)
            def _():
                acc_ref[...] = prod

            @pl.when(k > 0)
            def _():
                acc_ref[...] += prod

            @pl.when(k == nk - 1)
            def _():
                finish(acc_ref[...])

    outs = pl.pallas_call(
        body, name=name, grid=(M // tm, N // tn, nk),
        in_specs=[a_spec, b_spec] + extra_specs, out_specs=out_specs, out_shape=out_shape,
        scratch_shapes=[pltpu.VMEM((tm, tn), F32)] if nk > 1 and not in_place else [],
        compiler_params=_params(("parallel", "parallel", "arbitrary")),
    )(a.arr, b.arr, *[e[0] for e in extras])
    return outs[0] if n_out == 1 else outs


def _tile(n, cands):
    for t in cands:
        if n % t == 0:
            return t
    return n


def _grp(i, P):
    return 2 * (i // P) + jnp.minimum(i % P, 1)


def _mod_spec(D, P, part):
    return pl.BlockSpec((1, 1, D), lambda i: (_grp(i, P), 0, part))


def _res_norm(x, pending, modtab, shift_part, scale_part, gain, *, TR, P, name):
    T, D = x.shape
    row = pl.BlockSpec((TR, D), lambda i: (i, 0))
    has_branch = pending is not None
    ins, specs = [x], [row]
    if has_branch:
        branch, gate_tab, gate_part = pending
        ins += [branch, gate_tab]
        specs += [row, _mod_spec(D, P, gate_part)]
    ins += [modtab, modtab, gain]
    specs += [_mod_spec(D, P, shift_part), _mod_spec(D, P, scale_part), pl.BlockSpec((1, D), lambda i: (0, 0))]

    def body(*refs):
        if has_branch:
            x_ref, br_ref, g_ref, sh_ref, sc_ref, gn_ref, xo_ref, h_ref = refs
            xv = x_ref[...] + g_ref[0] * br_ref[...]
        else:
            x_ref, sh_ref, sc_ref, gn_ref, xo_ref, h_ref = refs
            xv = x_ref[...]
        xo_ref[...] = xv
        y = xv * lax.rsqrt(jnp.mean(xv * xv, axis=-1, keepdims=True) + EPS) * gn_ref[...]
        h_ref[...] = (y * (1.0 + sc_ref[0]) + sh_ref[0]).astype(BF16)

    return pl.pallas_call(
        body, name=name, grid=(T // TR,), in_specs=specs, out_specs=[row, row],
        out_shape=[_sds((T, D), F32), _sds((T, D), BF16)], compiler_params=_params(("parallel",)),
    )(*ins)


def _gate_bwd(dx, branch, modtab, gate_part, *, TR, P, name):
    T, D = dx.shape
    G = modtab.shape[0]
    row = pl.BlockSpec((TR, D), lambda i: (i, 0))
    acc = pl.BlockSpec((1, 1, D), lambda i: (_grp(i, P), 0, 0))

    def body(dx_ref, br_ref, g_ref, db_ref, dg_ref):
        r = pl.program_id(0) % P
        dxv = dx_ref[...]
        db_ref[...] = (dxv * g_ref[0]).astype(BF16)
        part = jnp.sum(dxv * br_ref[...], axis=0, keepdims=True)

        @pl.when(r <= 1)
        def _():
            dg_ref[0] = part

        @pl.when(r > 1)
        def _():
            dg_ref[0] += part

    return pl.pallas_call(
        body, name=name, grid=(T // TR,), in_specs=[row, row, _mod_spec(D, P, gate_part)], out_specs=[row, acc],
        out_shape=[_sds((T, D), BF16), _sds((G, 1, D), F32)], compiler_params=_params(("arbitrary",)),
    )(dx, branch, modtab)


def _norm_bwd(x, dh, dres, modtab, scale_part, gain, *, TR, P, name):
    T, D = x.shape
    G = modtab.shape[0]
    row = pl.BlockSpec((TR, D), lambda i: (i, 0))
    acc = pl.BlockSpec((1, 1, D), lambda i: (_grp(i, P), 0, 0))

    def body(x_ref, dh_ref, dres_ref, sc_ref, gn_ref, dx_ref, dsh_ref, dsc_ref, dgn_ref):
        r = pl.program_id(0) % P
        xv, dhv, gn = x_ref[...], dh_ref[...], gn_ref[...]
        rstd = lax.rsqrt(jnp.mean(xv * xv, axis=-1, keepdims=True) + EPS)
        xhat = xv * rstd
        dn = dhv * (1.0 + sc_ref[0])
        dxhat = dn * gn
        dx_ref[...] = dres_ref[...] + rstd * (dxhat - xhat * jnp.mean(dxhat * xhat, axis=-1, keepdims=True))
        p_sh = jnp.sum(dhv, axis=0, keepdims=True)
        p_sc = jnp.sum(dhv * (xhat * gn), axis=0, keepdims=True)
        p_gn = jnp.sum(dn * xhat, axis=0, keepdims=True)

        @pl.when(r <= 1)
        def _():
            dsh_ref[0] = p_sh
            dsc_ref[0] = p_sc
            dgn_ref[0] = p_gn

        @pl.when(r > 1)
        def _():
            dsh_ref[0] += p_sh
            dsc_ref[0] += p_sc
            dgn_ref[0] += p_gn

    return pl.pallas_call(
        body, name=name, grid=(T // TR,),
        in_specs=[row, row, row, _mod_spec(D, P, scale_part), pl.BlockSpec((1, D), lambda i: (0, 0))],
        out_specs=[row, acc, acc, acc],
        out_shape=[_sds((T, D), F32)] + [_sds((G, 1, D), F32)] * 3, compiler_params=_params(("arbitrary",)),
    )(x, dh, dres, modtab, gain)


def _loss_head(x, branch, modtab, gate_part, target, *, TR, P, name):
    T, D = x.shape
    row = pl.BlockSpec((TR, D), lambda i: (i, 0))
    tgt = pl.BlockSpec((TR, D), lambda i: ((i // P) * (P - 1) + jnp.maximum(i % P - 1, 0), 0))
    one = pl.BlockSpec((1, LANES), lambda i: (0, 0))

    def body(x_ref, br_ref, g_ref, t_ref, dy_ref, loss_ref):
        i = pl.program_id(0)
        r = i % P

        @pl.when(i == 0)
        def _():
            loss_ref[...] = jnp.zeros_like(loss_ref)

        @pl.when(r == 0)
        def _():
            dy_ref[...] = jnp.zeros_like(dy_ref)

        @pl.when(r > 0)
        def _():
            err = x_ref[...] + g_ref[0] * br_ref[...] - t_ref[...]
            dy_ref[...] = err / D
            per_tok = jnp.mean(err * err, axis=-1, keepdims=True)
            loss_ref[...] += 0.5 * jnp.sum(per_tok, axis=0, keepdims=True)

    return pl.pallas_call(
        body, name=name, grid=(T // TR,), in_specs=[row, row, _mod_spec(D, P, gate_part), tgt], out_specs=[row, one],
        out_shape=[_sds((T, D), F32), _sds((1, LANES), F32)], compiler_params=_params(("arbitrary",)),
    )(x, branch, modtab, target)


QKV_WIDTH = Q_WIDTH + 2 * KV_WIDTH
QK_NORMED = 4


def _seg_mean(v):
    lane = lax.broadcasted_iota(jnp.int32, v.shape, 1)
    lo = lane < HEAD_DIM
    s0 = jnp.sum(jnp.where(lo, v, 0.0), axis=-1, keepdims=True)
    s1 = jnp.sum(jnp.where(lo, 0.0, v), axis=-1, keepdims=True)
    return jnp.where(lo, s0, s1) * (1.0 / HEAD_DIM)


def _pair_swap(v):
    lane = lax.broadcasted_iota(jnp.int32, v.shape, 1)
    return jnp.where((lane & 1) == 0, pltpu.roll(v, LANES - 1, 1), pltpu.roll(v, 1, 1))


def _chunk(c):
    return slice(c * LANES, (c + 1) * LANES)


def _qk_prep(z, gains, cos, sin, *, TR, P, name):
    T = z.shape[0]

    def body(z_ref, g_ref, c_ref, s_ref, q_ref, k_ref, v_ref):
        cs, sn = c_ref[...], s_ref[...]
        for ch in range(QK_NORMED):
            xv = z_ref[:, _chunk(ch)]
            y = xv * lax.rsqrt(_seg_mean(xv * xv) + EPS) * g_ref[0, :, _chunk(ch)]
            out = (y * cs + _pair_swap(y) * sn).astype(BF16)
            if ch < QK_NORMED - 1:
                q_ref[:, _chunk(ch)] = out
            else:
                k_ref[...] = out
        v_ref[...] = z_ref[:, _chunk(QK_NORMED)].astype(BF16)

    def out(width):
        return pl.BlockSpec((None, TR, width), lambda i, j: (j, i, 0))
    return pl.pallas_call(
        body, name=name, grid=(T // TR, 2),
        in_specs=[pl.BlockSpec((TR, QKV_WIDTH), lambda i, j: (i, j)),
                  pl.BlockSpec((1, 1, QKV_WIDTH), lambda i, j: (j, 0, 0)),
                  pl.BlockSpec((TR, LANES), lambda i, j: (i % P, 0)),
                  pl.BlockSpec((TR, LANES), lambda i, j: (i % P, 0))],
        out_specs=[out(Q_WIDTH), out(KV_WIDTH), out(KV_WIDTH)],
        out_shape=[_sds((2, T, Q_WIDTH), BF16), _sds((2, T, KV_WIDTH), BF16), _sds((2, T, KV_WIDTH), BF16)],
        compiler_params=_params(("parallel", "parallel")),
    )(z, gains, cos, sin)


def _qk_prep_bwd(z, dq, dk, dv, gains, cos, sin, *, branch, TR, P, name):
    T = z.shape[0]
    nt = T // TR

    def body(z_ref, dq_ref, dk_ref, dv_ref, g_ref, c_ref, s_ref, dz_ref, dg_ref):
        i = pl.program_id(0)
        cs, sn = c_ref[...], s_ref[...]
        parts = []
        for ch in range(QK_NORMED):
            xv, g = z_ref[:, _chunk(ch)], g_ref[0, :, _chunk(ch)]
            dout = dq_ref[:, _chunk(ch)] if ch < QK_NORMED - 1 else dk_ref[...]
            dy = dout * cs + _pair_swap(dout * sn)
            rstd = lax.rsqrt(_seg_mean(xv * xv) + EPS)
            xhat = xv * rstd
            dxhat = dy * g
            dz_ref[:, _chunk(ch)] = (rstd * (dxhat - xhat * _seg_mean(dxhat * xhat))).astype(BF16)
            parts.append(jnp.sum(dy * xhat, axis=0, keepdims=True))
        dz_ref[:, _chunk(QK_NORMED)] = dv_ref[...].astype(BF16)
        parts.append(jnp.zeros((1, LANES), F32))
        part = jnp.concatenate(parts, axis=1)

        @pl.when(i == 0)
        def _():
            dg_ref[0] = part

        @pl.when(i > 0)
        def _():
            dg_ref[0] += part

    def rows(width, col=0):
        return pl.BlockSpec((TR, width), lambda i: (i, col))
    return pl.pallas_call(
        body, name=name, grid=(nt,),
        in_specs=[rows(QKV_WIDTH, branch), rows(Q_WIDTH), rows(KV_WIDTH), rows(KV_WIDTH),
                  pl.BlockSpec((1, 1, QKV_WIDTH), lambda i: (branch, 0, 0)),
                  pl.BlockSpec((TR, LANES), lambda i: (i % P, 0)),
                  pl.BlockSpec((TR, LANES), lambda i: (i % P, 0))],
        out_specs=[rows(QKV_WIDTH), pl.BlockSpec((1, 1, QKV_WIDTH), lambda i: (0, 0, 0))],
        out_shape=[_sds((T, QKV_WIDTH), BF16), _sds((1, 1, QKV_WIDTH), F32)],
        compiler_params=_params(("arbitrary",)),
    )(z, dq, dk, dv, gains, cos, sin)


NT_DIMS = (((1,), (1,)), ((), ()))
TN_DIMS = (((0,), (0,)), ((), ()))
QROWS = N_GROUP * Q_BLOCK
SCORE_SCALE = HEAD_DIM ** -0.5
BAND = Q_BLOCK + 2 * WINDOW
FWD_LATENT_CHUNK = 256
BWD_LATENT_CHUNK = 2048


def _move_head(block, half_from, half_to):
    lane = lax.broadcasted_iota(jnp.int32, block.shape, 1)
    src = block if half_from == half_to else pltpu.roll(block, HEAD_DIM, 1)
    keep = (lane < HEAD_DIM) if half_to == 0 else (lane >= HEAD_DIM)
    return jnp.where(keep, src, 0.0)


def _stack_heads(lane_block, j):
    pieces = []
    for h in range(N_GROUP * j, N_GROUP * (j + 1)):
        pieces.append(_move_head(lane_block(h // 2), h % 2, j))
    return jnp.concatenate(pieces, axis=0)


def _lane_blocks(ref):
    return lambda m: ref[:, m * LANES:(m + 1) * LANES].astype(F32)


def _unstack_heads(stacked, ref):
    heads = []
    for h in range(N_HEADS):
        j, r = h // N_GROUP, h % N_GROUP
        heads.append(_move_head(stacked[j][r * Q_BLOCK:(r + 1) * Q_BLOCK], j, h % 2))
    for m in range(N_HEADS // 2):
        ref[:, m * LANES:(m + 1) * LANES] = (heads[2 * m] + heads[2 * m + 1]).astype(ref.dtype)


def _key_chunks(i, latent, *, n_ctx, t_all, window, chunk, latent_chunk):
    ctx = [(s, chunk, False) for s in range(0, n_ctx, chunk)]
    if not latent:
        return ctx
    if not window:
        wide = latent_chunk if (t_all - n_ctx) % latent_chunk == 0 else chunk
        return ctx + [(s, wide, False) for s in range(n_ctx, t_all, wide)]
    start = pl.multiple_of(jnp.minimum((i - 1) * Q_BLOCK, t_all - BAND), Q_BLOCK)
    band_chunk = BAND if latent_chunk >= BAND else (chunk if BAND % chunk == 0 else Q_BLOCK)
    return ctx + [(start + s, band_chunk, True) for s in range(0, BAND, band_chunk)]


def _scores(q, k_ref, i, start, size, masked, *, n_ctx):
    s = lax.dot_general(q, k_ref[pl.ds(start, size), :], NT_DIMS, preferred_element_type=F32)
    if masked:
        qpos = (i * Q_BLOCK - n_ctx) + (lax.broadcasted_iota(jnp.int32, (QROWS, size), 0) & (Q_BLOCK - 1))
        kpos = (start - n_ctx) + lax.broadcasted_iota(jnp.int32, (QROWS, size), 1)
        valid = (kpos - qpos <= WINDOW) & (qpos - kpos <= WINDOW) & (kpos >= 0)
        s = jnp.where(valid, s, NEG)
    return s


def _sink_column(sink_ref, j):
    r = lax.broadcasted_iota(jnp.int32, (QROWS, 1), 0)
    s0, s1, s2 = sink_ref[j * N_GROUP], sink_ref[j * N_GROUP + 1], sink_ref[j * N_GROUP + 2]
    return jnp.where(r < Q_BLOCK, s0, jnp.where(r < 2 * Q_BLOCK, s1, s2))


def _attn_specs(Tp, branch):
    nq = Tp // Q_BLOCK
    q_in = pl.BlockSpec((None, Q_BLOCK, Q_WIDTH), lambda b, i: (branch, b * nq + i, 0))
    kv_in = pl.BlockSpec((None, Tp, KV_WIDTH), lambda b, i: (branch, b, 0))
    q_out = pl.BlockSpec((Q_BLOCK, Q_WIDTH), lambda b, i: (b * nq + i, 0))
    kv_out = pl.BlockSpec((Tp, KV_WIDTH), lambda b, i: (b, 0))
    return q_in, kv_in, q_out, kv_out


def _attn_chunk(Tp):
    return 256 if Tp % 256 == 0 else Q_BLOCK


def _attn_fwd(q, k, v, sink, *, branch, B, n_ctx, window, name):
    T = q.shape[1]
    Tp = T // B
    nq = Tp // Q_BLOCK
    has_sink = sink is not None
    q_in, kv_in, q_out, _ = _attn_specs(Tp, branch)
    lse_spec = pl.BlockSpec((None, N_KV * QROWS, 1), lambda b, i: (b * nq + i, 0, 0))

    def body(*refs):
        if has_sink:
            sink_ref, q_ref, k_ref, v_ref, o_ref, o32_ref, lse_ref = refs
        else:
            q_ref, k_ref, v_ref, o_ref, o32_ref, lse_ref = refs
        i = pl.program_id(1)

        def run(latent):
            outs = []
            for j in range(N_KV):
                qv = (_stack_heads(_lane_blocks(q_ref), j) * SCORE_SCALE).astype(BF16)
                if has_sink:
                    m, l = _sink_column(sink_ref, j), jnp.ones((QROWS, 1), F32)
                else:
                    m, l = jnp.full((QROWS, 1), NEG, F32), jnp.zeros((QROWS, 1), F32)
                acc = jnp.zeros((QROWS, LANES), F32)
                for start, size, masked in _key_chunks(i, latent, n_ctx=n_ctx, t_all=Tp, window=window,
                                                       chunk=_attn_chunk(Tp), latent_chunk=FWD_LATENT_CHUNK):
                    s = _scores(qv, k_ref, i, start, size, masked, n_ctx=n_ctx)
                    m_new = jnp.maximum(m, jnp.max(s, axis=-1, keepdims=True))
                    alpha = jnp.exp(m - m_new)
                    p = jnp.exp(s - m_new)
                    l = l * alpha + jnp.sum(p, axis=-1, keepdims=True)
                    acc = acc * alpha + jnp.dot(p.astype(BF16), v_ref[pl.ds(start, size), :], preferred_element_type=F32)
                    m = m_new
                outs.append(acc * (1.0 / l))
                lse_ref[j * QROWS:(j + 1) * QROWS, :] = m + jnp.log(l)
            _unstack_heads(outs, o_ref)
            _unstack_heads(outs, o32_ref)

        @pl.when(i < n_ctx // Q_BLOCK)
        def _():
            run(False)

        @pl.when(i >= n_ctx // Q_BLOCK)
        def _():
            run(True)

    ins, specs = [q, k, v], [q_in, kv_in, kv_in]
    if has_sink:
        ins, specs = [sink] + ins, [pl.BlockSpec(memory_space=pltpu.SMEM)] + specs
    return pl.pallas_call(
        body, name=name, grid=(B, nq), in_specs=specs, out_specs=[q_out, q_out, lse_spec],
        out_shape=[_sds((T, Q_WIDTH), BF16), _sds((T, Q_WIDTH), F32), _sds((T // Q_BLOCK, N_KV * QROWS, 1), F32)],
        compiler_params=_params(("parallel", "parallel")),
    )(*ins)


def _attn_bwd(q, k, v, do, o32, lse, sink, *, branch, B, n_ctx, window, name):
    T = q.shape[1]
    Tp = T // B
    nq = Tp // Q_BLOCK
    has_sink = sink is not None
    q_in, kv_in, q_out, kv_out = _attn_specs(Tp, branch)
    lse_spec = pl.BlockSpec((None, N_KV * QROWS, 1), lambda b, i: (b * nq + i, 0, 0))
    sink_spec = pl.BlockSpec((None, 8, LANES), lambda b, i: (b, 0, 0))

    def body(*refs):
        if has_sink:
            sink_ref, q_ref, k_ref, v_ref, do_ref, o_ref, lse_ref, dq_ref, dk_ref, dv_ref, ds_ref = refs
        else:
            q_ref, k_ref, v_ref, do_ref, o_ref, lse_ref, dq_ref, dk_ref, dv_ref = refs
        i = pl.program_id(1)

        @pl.when(i == 0)
        def _():
            dk_ref[...] = jnp.zeros_like(dk_ref)
            dv_ref[...] = jnp.zeros_like(dv_ref)
            if has_sink:
                ds_ref[...] = jnp.zeros_like(ds_ref)

        def run(latent):
            upd = jnp.zeros((8, LANES), F32)
            do_blocks, o_blocks = _lane_blocks(do_ref), _lane_blocks(o_ref)
            qvs = [(_stack_heads(_lane_blocks(q_ref), j) * SCORE_SCALE).astype(BF16) for j in range(N_KV)]
            dovs = [_stack_heads(do_blocks, j).astype(BF16) for j in range(N_KV)]
            deltas = [jnp.sum(_stack_heads(lambda m: do_blocks(m) * o_blocks(m), j), axis=-1, keepdims=True)
                      for j in range(N_KV)]
            lses = [lse_ref[j * QROWS:(j + 1) * QROWS, :] for j in range(N_KV)]
            q_all, do_all = jnp.concatenate(qvs, axis=0), jnp.concatenate(dovs, axis=0)
            dqs = [jnp.zeros((QROWS, LANES), F32) for _ in range(N_KV)]
            for start, size, masked in _key_chunks(i, latent, n_ctx=n_ctx, t_all=Tp, window=window,
                                                   chunk=_attn_chunk(Tp), latent_chunk=BWD_LATENT_CHUNK):
                rows = pl.ds(start, size)
                ds_all, p_all = [], []
                for j in range(N_KV):
                    p = jnp.exp(_scores(qvs[j], k_ref, i, start, size, masked, n_ctx=n_ctx) - lses[j])
                    dp = lax.dot_general(dovs[j], v_ref[rows, :], NT_DIMS, preferred_element_type=F32)
                    ds = (p * (dp - deltas[j])).astype(BF16)
                    dqs[j] = dqs[j] + jnp.dot(ds, k_ref[rows, :], preferred_element_type=F32)
                    ds_all.append(ds)
                    p_all.append(p.astype(BF16))
                dk_ref[rows, :] += lax.dot_general(jnp.concatenate(ds_all, axis=0), q_all, TN_DIMS, preferred_element_type=F32)
                dv_ref[rows, :] += lax.dot_general(jnp.concatenate(p_all, axis=0), do_all, TN_DIMS, preferred_element_type=F32)
            dqs = [dq * SCORE_SCALE for dq in dqs]
            for j in range(N_KV):
                if has_sink:
                    contrib = -(jnp.exp(_sink_column(sink_ref, j) - lses[j]) * deltas[j])
                    r = lax.broadcasted_iota(jnp.int32, (QROWS, 1), 0)
                    row8 = lax.broadcasted_iota(jnp.int32, (8, LANES), 0)
                    for h in range(N_GROUP):
                        in_head = (r >= h * Q_BLOCK) & (r < (h + 1) * Q_BLOCK)
                        tot = jnp.sum(jnp.where(in_head, contrib, 0.0), axis=0, keepdims=True)
                        upd = upd + jnp.where(row8 == j * N_GROUP + h, tot, 0.0)
            _unstack_heads(dqs, dq_ref)
            if has_sink:
                ds_ref[...] += upd

        @pl.when(i < n_ctx // Q_BLOCK)
        def _():
            run(False)

        @pl.when(i >= n_ctx // Q_BLOCK)
        def _():
            run(True)

    ins, specs = [q, k, v, do, o32, lse], [q_in, kv_in, kv_in, q_out, q_out, lse_spec]
    out_specs = [q_out, kv_out, kv_out]
    out_shape = [_sds((T, Q_WIDTH), F32), _sds((T, KV_WIDTH), F32), _sds((T, KV_WIDTH), F32)]
    if has_sink:
        ins, specs = [sink] + ins, [pl.BlockSpec(memory_space=pltpu.SMEM)] + specs
        out_specs.append(sink_spec)
        out_shape.append(_sds((B, 8, LANES), F32))
    return pl.pallas_call(
        body, name=name, grid=(B, Tp // Q_BLOCK), in_specs=specs, out_specs=out_specs, out_shape=out_shape,
        compiler_params=_params(("parallel", "arbitrary")),
    )(*ins)


def _window_sums(xp):
    n = xp.shape[0]

    def ahead(a, k):
        return pltpu.roll(a, n - k, 0)
    a2 = xp + ahead(xp, 1)
    a4 = a2 + ahead(a2, 2)
    a8 = a4 + ahead(a4, 4)
    a16 = a8 + ahead(a8, 8)
    return (a2, a4, a8, a16)


def _by_group(vals):
    lane = lax.broadcasted_iota(jnp.int32, vals[0].shape, 1)
    return jnp.where(lane < POOL_CH, vals[0], jnp.where(lane < 2 * POOL_CH, vals[1],
                     jnp.where(lane < 3 * POOL_CH, vals[2], vals[3])))


def _pool_counts(n):
    t = lax.broadcasted_iota(jnp.int32, (n, POOL_WIDTH), 0)
    cnts = [(jnp.minimum(t + w // 2, n) - jnp.maximum(t - w // 2, 0)).astype(F32) for w in POOL_WINDOWS]
    return _by_group(cnts)


def _pad_rows(x):
    zeros = jnp.zeros((POOL_PAD, x.shape[1]), x.dtype)
    return jnp.concatenate([zeros, x, zeros], axis=0)


def _pool_stream(u):
    n = u.shape[0]
    sums = _window_sums(_pad_rows(u))
    tots = [pltpu.roll(a, w // 2, 0)[POOL_PAD:POOL_PAD + n] for a, w in zip(sums, POOL_WINDOWS)]
    return _by_group(tots) / _pool_counts(n) - u


def _pool_stream_t(dp):
    n = dp.shape[0]
    sums = _window_sums(_pad_rows(dp / _pool_counts(n)))
    tots = [pltpu.roll(a, w // 2 - 1, 0)[POOL_PAD:POOL_PAD + n] if w > 2 else a[POOL_PAD:POOL_PAD + n]
            for a, w in zip(sums, POOL_WINDOWS)]
    return _by_group(tots) - dp


def _pool_fwd(z, w_bd, scale, *, B, Tp, n_ctx, name):
    T = z.shape[0]
    blk = pl.BlockSpec((Tp, POOL_WIDTH), lambda b: (b, U_COL // POOL_WIDTH))
    out = pl.BlockSpec((Tp, POOL_WIDTH), lambda b: (b, 0))

    def body(u_ref, w_ref, s_ref, p_ref, o_ref):
        for lo, hi in ((0, n_ctx), (n_ctx, Tp)):
            pooled = _pool_stream(u_ref[lo:hi, :]).astype(BF16)
            p_ref[lo:hi, :] = pooled
            mixed = jnp.dot(pooled, w_ref[...], preferred_element_type=F32)
            o_ref[lo:hi, :] = (mixed * s_ref[...]).astype(BF16)

    return pl.pallas_call(
        body, name=name, grid=(B,),
        in_specs=[blk, pl.BlockSpec((POOL_WIDTH, POOL_WIDTH), lambda b: (0, 0)), pl.BlockSpec((1, POOL_WIDTH), lambda b: (0, 0))],
        out_specs=[out, out], out_shape=[_sds((T, POOL_WIDTH), BF16)] * 2, compiler_params=_params(("parallel",)),
    )(z, w_bd, scale)


def _pool_bwd(d_ob, pooled, w_bd, scale, *, B, Tp, n_ctx, name):
    T = d_ob.shape[0]
    blk = pl.BlockSpec((Tp, POOL_WIDTH), lambda b: (b, 0))
    wsp = pl.BlockSpec((POOL_WIDTH, POOL_WIDTH), lambda b: (0, 0))
    ssp = pl.BlockSpec((1, POOL_WIDTH), lambda b: (0, 0))

    def body(d_ref, p_ref, w_ref, s_ref, du_ref, dw_ref, dsc_ref):
        @pl.when(pl.program_id(0) == 0)
        def _():
            dw_ref[...] = jnp.zeros_like(dw_ref)
            dsc_ref[...] = jnp.zeros_like(dsc_ref)

        dv, pv, wv = d_ref[...], p_ref[...], w_ref[...]
        mixed = jnp.dot(pv, wv, preferred_element_type=F32)
        dsc_ref[...] += jnp.sum(dv * mixed, axis=0, keepdims=True)
        dmixed = (dv * s_ref[...]).astype(BF16)
        dw_ref[...] += lax.dot_general(pv, dmixed, TN_DIMS, preferred_element_type=F32)
        dpooled = lax.dot_general(dmixed, wv, NT_DIMS, preferred_element_type=F32)
        for lo, hi in ((0, n_ctx), (n_ctx, Tp)):
            du_ref[lo:hi, :] = _pool_stream_t(dpooled[lo:hi, :]).astype(BF16)

    return pl.pallas_call(
        body, name=name, grid=(B,), in_specs=[blk, blk, wsp, ssp], out_specs=[blk, wsp, ssp],
        out_shape=[_sds((T, POOL_WIDTH), BF16), _sds((POOL_WIDTH, POOL_WIDTH), F32), _sds((1, POOL_WIDTH), F32)],
        compiler_params=_params(("arbitrary",)),
    )(d_ob, pooled, w_bd, scale)


def _merge_specs(z, D, TR, tc, wa, wb, wc):
    def act(width):
        return pl.BlockSpec((TR, width), lambda i, n: (i, 0))

    def gate(part):
        return pl.BlockSpec((TR, tc), lambda i, n: (i, (GATE_COL + part * D) // tc + n))
    w_specs = [w.spec(w.shape[0], tc, lambda i, n: (0, n)) for w in (wa, wb, wc)]
    return [act(Q_WIDTH), act(POOL_WIDTH), act(Q_WIDTH), gate(0), gate(1), gate(2)] + w_specs


def _merge_fwd(oa, ob, oc, z, wa, wb, wc, *, D, TR, name):
    T = oa.shape[0]
    tc = D // N_CHIPS

    def body(oa_ref, ob_ref, oc_ref, ga_ref, gb_ref, gc_ref, wa_ref, wb_ref, wc_ref, y_ref):
        acc = jax.nn.sigmoid(ga_ref[...]) * jnp.dot(oa_ref[...], wa_ref[...], preferred_element_type=F32)
        acc += jax.nn.sigmoid(gb_ref[...]) * jnp.dot(ob_ref[...], wb_ref[...], preferred_element_type=F32)
        acc += jax.nn.sigmoid(gc_ref[...]) * jnp.dot(oc_ref[...], wc_ref[...], preferred_element_type=F32)
        y_ref[...] = acc.astype(BF16)

    return pl.pallas_call(
        body, name=name, grid=(T // TR, D // tc), in_specs=_merge_specs(z, D, TR, tc, wa, wb, wc),
        out_specs=pl.BlockSpec((TR, tc), lambda i, n: (i, n)), out_shape=_sds((T, D), BF16),
        compiler_params=_params(("parallel", "parallel")),
    )(oa, ob, oc, z, z, z, wa.arr, wb.arr, wc.arr)


def _merge_bwd(dy, oa, ob, oc, z, wa, wb, wc, *, D, TR, name):
    T = oa.shape[0]
    tc = D // N_CHIPS
    out = pl.BlockSpec((TR, tc), lambda i, n: (i, n))

    def body(dy_ref, oa_ref, ob_ref, oc_ref, ga_ref, gb_ref, gc_ref, wa_ref, wb_ref, wc_ref,
             dpa_ref, dpb_ref, dpc_ref, dga_ref, dgb_ref, dgc_ref):
        dyv = dy_ref[...]
        for o_ref, g_ref, w_ref, dp_ref, dg_ref in ((oa_ref, ga_ref, wa_ref, dpa_ref, dga_ref),
                                                    (ob_ref, gb_ref, wb_ref, dpb_ref, dgb_ref),
                                                    (oc_ref, gc_ref, wc_ref, dpc_ref, dgc_ref)):
            s = jax.nn.sigmoid(g_ref[...])
            proj = jnp.dot(o_ref[...], w_ref[...], preferred_element_type=F32)
            dp_ref[...] = (dyv * s).astype(BF16)
            dg_ref[...] = (dyv * proj * (s * (1.0 - s))).astype(BF16)

    return pl.pallas_call(
        body, name=name, grid=(T // TR, D // tc), in_specs=[out] + _merge_specs(z, D, TR, tc, wa, wb, wc),
        out_specs=[out] * 6, out_shape=[_sds((T, D), BF16)] * 6, compiler_params=_params(("parallel", "parallel")),
    )(dy, oa, ob, oc, z, z, z, wa.arr, wb.arr, wc.arr)


def _silu_rows(cc, name):
    def body(c_ref, s_ref):
        v = c_ref[...]
        s_ref[...] = (v * jax.nn.sigmoid(v)).astype(BF16)
    return pl.pallas_call(body, name=name, out_shape=_sds(cc.shape, BF16))(cc)


def _ada_bwd_rows(dm, ds, cc, name):
    def body(dm_ref, ds_ref, c_ref, db_ref, dc_ref):
        db_ref[...] = jnp.sum(dm_ref[...], axis=0, keepdims=True)
        v = c_ref[...]
        s = jax.nn.sigmoid(v)
        dc_ref[...] = ds_ref[...] * (s * (1.0 + v * (1.0 - s)))
    return pl.pallas_call(body, name=name, out_shape=[_sds((1, dm.shape[1]), F32), _sds(cc.shape, F32)])(dm, ds, cc)


def _row_tile(rows, cols):
    for t in (512, 256, 128, 64, 32, 16, 8):
        if rows % t == 0 and t * cols * 4 <= (1 << 20):
            return t
    return rows


def _add_own_layer(layers, landed, core, name):
    R, C = landed.shape
    tr = _row_tile(R, C)
    n_layers = len(layers)

    def body(c_ref, *refs):
        b_ref, o_ref, o16_ref = refs[n_layers:]
        for l in range(n_layers):
            @pl.when(c_ref[0] == l)
            def _(a_ref=refs[l]):
                tot = a_ref[...] + b_ref[...].astype(F32)
                o_ref[...] = tot
                o16_ref[...] = tot.astype(BF16)

    row = pl.BlockSpec((tr, C), lambda i, c: (i, 0))
    own = [pl.BlockSpec((tr, C), functools.partial(lambda i, c, l: (jnp.where(c[0] == l, i, 0), 0), l=l))
           for l in range(n_layers)]
    grid_spec = pltpu.PrefetchScalarGridSpec(num_scalar_prefetch=1, grid=(R // tr,),
                                             in_specs=own + [row], out_specs=[row, row])
    return pl.pallas_call(body, name=name, grid_spec=grid_spec, out_shape=[_sds((R, C), F32), _sds((R, C), BF16)],
                          compiler_params=_params(("arbitrary",)))(core, *layers, landed)


def _sum_chips(own, landed, chip, name):
    _, R, C = own.shape
    tr = _row_tile(R, C)

    def body(k_ref, a_ref, b_ref, o_ref):
        o_ref[...] = ((a_ref[...] + b_ref[0].astype(F32)) + b_ref[1].astype(F32)) + b_ref[2].astype(F32)

    grid_spec = pltpu.PrefetchScalarGridSpec(
        num_scalar_prefetch=1, grid=(R // tr,),
        in_specs=[pl.BlockSpec((None, tr, C), lambda i, k: (k[0], i, 0)), pl.BlockSpec((3, tr, C), lambda i, k: (0, i, 0))],
        out_specs=pl.BlockSpec((tr, C), lambda i, k: (i, 0)))
    return pl.pallas_call(body, name=name, grid_spec=grid_spec, out_shape=_sds((R, C), F32),
                          compiler_params=_params(("parallel",)))(chip, own, landed)


def _adam_math(w, g, m, v):
    m = ADAM_B1 * m + (1.0 - ADAM_B1) * g
    v = ADAM_B2 * v + (1.0 - ADAM_B2) * (g * g)
    m_hat = m / (1.0 - ADAM_B1 ** ADAM_STEP)
    v_hat = v / (1.0 - ADAM_B2 ** ADAM_STEP)
    delta = -ADAM_LR * (m_hat / (jnp.sqrt(v_hat) + ADAM_EPS) + ADAM_WD * w)
    return delta, m, v


def _adamw(w, mine, other, m, v, core, name):
    L, R, C = w.shape
    tr = _row_tile(R, C)

    def body(c_ref, w_ref, a_ref, b_ref, m_ref, v_ref, g_ref, d_ref, mo_ref, vo_ref):
        def step(g):
            d, mn, vn = _adam_math(w_ref[...], g, m_ref[...], v_ref[...])
            g_ref[...] = g
            d_ref[...] = d
            mo_ref[...] = mn
            vo_ref[...] = vn

        @pl.when(pl.program_id(0) == c_ref[0])
        def _():
            step(a_ref[...])

        @pl.when(pl.program_id(0) != c_ref[0])
        def _():
            step(b_ref[...])

    lay = pl.BlockSpec((None, tr, C), lambda l, i, c: (l, i, 0))
    row = pl.BlockSpec((tr, C), lambda l, i, c: (i, 0))
    grid_spec = pltpu.PrefetchScalarGridSpec(num_scalar_prefetch=1, grid=(L, R // tr),
                                             in_specs=[lay, row, row, lay, lay], out_specs=[lay] * 4)
    return pl.pallas_call(body, name=name, grid_spec=grid_spec, out_shape=[_sds((L, R, C), F32)] * 4,
                          compiler_params=_params(("parallel", "parallel")))(core, w, mine, other, m, v)


def _adamw_small(w, parts, m, v, name):
    R, C = w.shape

    def body(w_ref, p_ref, m_ref, v_ref, g_ref, d_ref, mo_ref, vo_ref):
        g = p_ref[0]
        for dev in range(1, 8):
            g = g + p_ref[dev]
        d, mn, vn = _adam_math(w_ref[...], g, m_ref[...], v_ref[...])
        g_ref[...] = g
        d_ref[...] = d
        mo_ref[...] = mn
        vo_ref[...] = vn

    return pl.pallas_call(body, name=name, out_shape=[_sds((R, C), F32)] * 4)(w, parts, m, v)


def _place():
    return lax.axis_index("x"), lax.axis_index("y"), lax.axis_index("c")


def _other_chips(x, y):
    return [(1 - x, y), (x, 1 - y), (1 - x, 1 - y)]


def _rcopy(src, dst, ssem, rsem, dev):
    return pltpu.make_async_remote_copy(src_ref=src, dst_ref=dst, send_sem=ssem, recv_sem=rsem,
                                        device_id=dev, device_id_type=MESH)


def _gather_weights(shards, name):
    n = len(shards)
    own_sem = 6

    def body(*refs):
        src, out = refs[:n], refs[n:2 * n]
        send_sems, recv_sems = refs[2 * n:]
        x, y, c = _place()
        sibling = (x, y, 1 - c)
        chips = _other_chips(x, y)
        mine = 2 * x + y
        own = [_rcopy(src[w], out[w].at[mine], send_sems.at[w, own_sem], recv_sems.at[w, own_sem], sibling)
               for w in range(n)]
        for cp in own:
            cp.start()
        first = [_rcopy(src[w].at[c], out[w].at[mine, c], send_sems.at[w, j], recv_sems.at[w, j], (*chip, c))
                 for w in range(n) for j, chip in enumerate(chips)]
        for cp in first:
            cp.start()
        passed = []
        for w in range(n):
            for j, (px, py) in enumerate(chips):
                landed = out[w].at[2 * px + py, c]
                _rcopy(landed, landed, send_sems.at[w, j], recv_sems.at[w, j], (px, py, c)).wait_recv()
                cp = _rcopy(landed, landed, send_sems.at[w, 3 + j], recv_sems.at[w, 3 + j], sibling)
                cp.start()
                passed.append(cp)
        for w in range(n):
            for j, (px, py) in enumerate(chips):
                landed = out[w].at[2 * px + py, 1 - c]
                _rcopy(landed, landed, send_sems.at[w, 3 + j], recv_sems.at[w, 3 + j], sibling).wait_recv()
        for cp in own:
            cp.wait_recv()
        for cp in first + passed + own:
            cp.wait_send()

    return pl.pallas_call(
        body, name=name, in_specs=[ANY] * n, out_specs=[ANY] * n,
        out_shape=[_sds((N_CHIPS,) + s.shape, s.dtype) for s in shards],
        scratch_shapes=[pltpu.SemaphoreType.DMA((n, 7)), pltpu.SemaphoreType.DMA((n, 7))],
    )(*shards)


def _send_other_layer(layer0, layer1, name):
    n = len(layer0)

    def body(*refs):
        src0, src1, out = refs[:n], refs[n:2 * n], refs[2 * n:3 * n]
        send_sems, recv_sems = refs[3 * n:]
        x, y, c = _place()

        def copies(src):
            return [_rcopy(src[w], out[w], send_sems.at[w], recv_sems.at[w], (x, y, 1 - c)) for w in range(n)]

        @pl.when(c == 0)
        def _():
            for cp in copies(src1):
                cp.start()

        @pl.when(c == 1)
        def _():
            for cp in copies(src0):
                cp.start()

        for cp in copies(src0):
            cp.wait_recv()
        for cp in copies(src0):
            cp.wait_send()

    return pl.pallas_call(
        body, name=name, in_specs=[ANY] * (2 * n), out_specs=[ANY] * n,
        out_shape=[_sds(s.shape, s.dtype) for s in layer0],
        scratch_shapes=[pltpu.SemaphoreType.DMA((n,)), pltpu.SemaphoreType.DMA((n,))],
    )(*layer0, *layer1)


def _send_chip_blocks(blocked, name):
    n = len(blocked)

    def body(*refs):
        src, out = refs[:n], refs[n:2 * n]
        send_sems, recv_sems = refs[2 * n:]
        x, y, c = _place()
        cps = [_rcopy(src[w].at[2 * px + py], out[w].at[j], send_sems.at[w, j], recv_sems.at[w, j], (px, py, c))
               for w in range(n) for j, (px, py) in enumerate(_other_chips(x, y))]
        for cp in cps:
            cp.start()
        for cp in cps:
            cp.wait_recv()
        for cp in cps:
            cp.wait_send()

    return pl.pallas_call(
        body, name=name, in_specs=[ANY] * n, out_specs=[ANY] * n,
        out_shape=[_sds((3,) + s.shape[1:], s.dtype) for s in blocked],
        scratch_shapes=[pltpu.SemaphoreType.DMA((n, 3)), pltpu.SemaphoreType.DMA((n, 3))],
    )(*blocked)


def _share_layers(reduced, name):
    n = len(reduced)

    def body(*refs):
        src, out = refs[:n], refs[n:2 * n]
        send_sems, recv_sems = refs[2 * n:]
        x, y, c = _place()
        cps = [_rcopy(src[w], out[w], send_sems.at[w], recv_sems.at[w], (x, y, 1 - c)) for w in range(n)]
        for cp in cps:
            cp.start()
        for cp in cps:
            cp.wait_recv()
        for cp in cps:
            cp.wait_send()

    return pl.pallas_call(
        body, name=name, in_specs=[ANY] * n, out_specs=[ANY] * n,
        out_shape=[_sds(s.shape, s.dtype) for s in reduced],
        scratch_shapes=[pltpu.SemaphoreType.DMA((n,)), pltpu.SemaphoreType.DMA((n,))],
    )(*reduced)


def _gather_small(block, name):
    m_per, n = block.shape

    def body(x_ref, out_ref, send_sems, recv_sems, local_sem):
        x, y, c = _place()
        me, sibling = (x, y, c), (x, y, 1 - c)
        chips = _other_chips(x, y)

        def rows(px, py, pc):
            return out_ref.at[pl.ds((4 * px + 2 * py + pc) * m_per, m_per), :]

        def copy(k, blk, to, src=None):
            return _rcopy(rows(*blk) if src is None else src, rows(*blk), send_sems.at[k], recv_sems.at[k], to)

        mine = pltpu.make_async_copy(x_ref, rows(*me), local_sem)
        mine.start()
        first = [copy(0, me, sibling, src=x_ref)]
        first += [copy(1 + j, me, (*chip, c), src=x_ref) for j, chip in enumerate(chips)]
        for cp in first:
            cp.start()
        passed = [copy(4 + j, (*chip, c), sibling) for j, chip in enumerate(chips)]
        for j, chip in enumerate(chips):
            copy(1 + j, (*chip, c), me).wait_recv()
            passed[j].start()
        copy(0, sibling, me).wait_recv()
        for j, chip in enumerate(chips):
            copy(4 + j, (*chip, 1 - c), me).wait_recv()
        for cp in first + passed:
            cp.wait_send()
        mine.wait()

    return pl.pallas_call(
        body, name=name, out_shape=_sds((8 * m_per, n), block.dtype),
        in_specs=[pl.BlockSpec(memory_space=pltpu.VMEM)], out_specs=pl.BlockSpec(memory_space=pltpu.VMEM),
        scratch_shapes=[pltpu.SemaphoreType.DMA((7,)), pltpu.SemaphoreType.DMA((7,)), pltpu.SemaphoreType.DMA],
    )(block)


def _rope_tables(n_ctx, seq):
    rows = seq // GRID_W
    r = jnp.repeat(jnp.arange(rows, dtype=F32), GRID_W)
    col = jnp.tile(jnp.arange(GRID_W, dtype=F32), rows)
    inv = 1.0 / (ROPE_THETA ** (jnp.arange(0, AXIS_DIM, 2, dtype=F32) / AXIS_DIM))
    ang = jnp.concatenate([r[:, None] * inv, col[:, None] * inv], axis=-1)
    cos = jnp.repeat(jnp.cos(ang), 2, axis=-1)
    sin = jnp.repeat(jnp.sin(ang), 2, axis=-1) * jnp.tile(jnp.array([-1.0, 1.0], F32), HEAD_DIM // 2)
    cos = jnp.concatenate([jnp.ones((n_ctx, HEAD_DIM), F32), cos], axis=0)
    sin = jnp.concatenate([jnp.zeros((n_ctx, HEAD_DIM), F32), sin], axis=0)
    return jnp.tile(cos, (1, 2)), jnp.tile(sin, (1, 2))


def _block_diag(w_pool_l):
    out = jnp.zeros((POOL_WIDTH, POOL_WIDTH), w_pool_l.dtype)
    for g in range(w_pool_l.shape[0]):
        out = out.at[g * POOL_CH:(g + 1) * POOL_CH, g * POOL_CH:(g + 1) * POOL_CH].set(w_pool_l[g])
    return out


def _local_step(x, c, ctx, c_ctx, small, gw, target):
    B, S, D = x.shape
    N = ctx.shape[1]
    L = small["norm1"].shape[0]
    Tp = N + S
    T = B * Tp
    TR = N
    P = Tp // N
    rows16 = 16
    assert N % Q_BLOCK == 0 and S % N == 0 and B + 1 <= rows16
    TM = _tile(T, (1024, 768, 512, 384, 256, 128))
    TMG = _tile(T, (512, 384, 256, 128))

    X = jnp.concatenate([ctx, x], axis=1).reshape(T, D)
    cc = jnp.zeros((rows16, D), F32).at[:B].set(c).at[B].set(c_ctx)
    s_rows = _silu_rows(cc, "silu_rows")
    cos, sin = _rope_tables(N, S)

    def weights(l):
        return dict(
            ada=_Opnd(gw["w_ada"], "bcols", l), w_in=_Opnd(gw["w_in"], "bcols", l),
            a=_Opnd(gw["w_br_a"], "bcols", l), b=_Opnd(gw["w_br_b"], "bcols", l), c=_Opnd(gw["w_br_c"], "bcols", l),
            out=_Opnd(gw["w_out"], "brows", l), mlp1=_Opnd(gw["w_mlp1"], "bcols", l), mlp2=_Opnd(gw["w_mlp2"], "brows", l))

    IN = weights(0)["w_in"].shape[1]
    DFF = weights(0)["mlp1"].shape[1]
    tn_in = _tile(IN // N_CHIPS, (1152, 768, 512, 384, 256, 128))
    tn_ff = _tile(DFF // N_CHIPS, (1024, 512, 256, 128))
    tn_ada = _tile(6 * D // N_CHIPS, (1536, 768, 512, 256, 128))
    tn_d = D // N_CHIPS
    tk_d = _tile(D, (512,))
    tk_tok = _tile(T, (2304, 1536, 1024, 768, 512, 384, 256))

    saved = []
    xin, pending = X, None
    for l in range(L):
        W = weights(l)
        b_ada = small["b_ada"][l].reshape(1, 6 * D)
        mod = _matmul(s_rows, W["ada"], "nn", tm=rows16, tn=tn_ada, tk=D, name=f"ada_fwd{l}",
                      epilogue=lambda acc, b: (acc + b,), extras=[(b_ada, (1, tn_ada), lambda m, n: (0, n))])
        modtab = jnp.stack([jnp.broadcast_to(mod[B], (B, 6 * D)), mod[:B]], axis=1).reshape(2 * B, 1, 6 * D)
        ones = jnp.ones((LANES,), F32)
        gains = jnp.stack([
            jnp.concatenate([jnp.tile(small[q][l], 2)] * 3 + [jnp.tile(small[k][l], 2), ones])
            for q, k in (("q_norm_a", "k_norm_a"), ("q_norm_c", "k_norm_c"))]).reshape(2, 1, QKV_WIDTH)
        w_bd = _block_diag(small["w_pool"][l]).astype(BF16)
        p_scale = small["pool_scale"][l].reshape(1, POOL_WIDTH)
        sink = small["sink_c"][l]

        x0, h1 = _res_norm(xin, pending, modtab, 0, 1, small["norm1"][l][None], TR=TR, P=P, name=f"norm1_fwd{l}")
        z = _matmul(h1, W["w_in"], "nn", tm=TM, tn=tn_in, tk=D, name=f"in_proj{l}")
        q2, k2, v2 = _qk_prep(z, gains, cos, sin, TR=TR, P=P, name=f"qk_prep{l}")
        oa, oa32, lse_a = _attn_fwd(q2, k2, v2, None, branch=0, B=B, n_ctx=N, window=False, name=f"attn_a_fwd{l}")
        oc, oc32, lse_c = _attn_fwd(q2, k2, v2, sink, branch=1, B=B, n_ctx=N, window=True, name=f"attn_c_fwd{l}")
        pooled, ob = _pool_fwd(z, w_bd, p_scale, B=B, Tp=Tp, n_ctx=N, name=f"pool_fwd{l}")
        y = _merge_fwd(oa, ob, oc, z, W["a"], W["b"], W["c"], D=D, TR=TMG, name=f"merge_fwd{l}")
        ao = _matmul(y, W["out"], "nn", tm=TM, tn=D, tk=tn_d, name=f"out_proj{l}")
        x1, h2 = _res_norm(x0, (ao, modtab, 2), modtab, 3, 4, small["norm2"][l][None], TR=TR, P=P, name=f"norm2_fwd{l}")
        a_pre, r_act = _matmul(h2, W["mlp1"], "nn", tm=TM, tn=tn_ff, tk=D, name=f"mlp1_fwd{l}", out_dtypes=(F32, BF16),
                               epilogue=lambda acc: (acc, jnp.square(jnp.maximum(acc, 0.0))))
        mo = _matmul(r_act, W["mlp2"], "nn", tm=TM, tn=D, tk=tn_ff, name=f"mlp2_fwd{l}")
        saved.append(dict(modtab=modtab, gains=gains, w_bd=w_bd, p_scale=p_scale, sink=sink, x0=x0, h1=h1, z=z,
                          q2=q2, k2=k2, v2=v2, oa=oa, ob=ob, oc=oc, oa32=oa32, oc32=oc32, lse_a=lse_a, lse_c=lse_c,
                          pooled=pooled, y=y, ao=ao,
                          x1=x1, h2=h2, a_pre=a_pre, r_act=r_act, mo=mo))
        xin, pending = x1, (mo, modtab, 5)

    dxo, loss = _loss_head(xin, pending[0], pending[1], 5, target.reshape(B * S, D), TR=TR, P=P, name="loss_head")

    big = {k: [None] * L for k in gw}
    big16 = {k: [None] * L for k in gw}
    sm = {k: [None] * L for k in ("b_ada", "norm1", "norm2", "q_norm_a", "k_norm_a", "q_norm_c", "k_norm_c",
                                   "sink_c", "w_pool", "pool_scale")}

    def dw(key, l, a, b, *, tm, tn, name, tk=tk_tok, blocked=True):
        outs = _matmul(a, b, "tn", tm=tm, tn=tn, tk=tk, name=name, out_dtypes=(F32, BF16), out_blocked=blocked)
        if not blocked:
            outs = [o.reshape(N_CHIPS, o.shape[0] // N_CHIPS, o.shape[1]) for o in outs]
        big[key][l], big16[key][l] = outs
    d_cctx = jnp.zeros((D,), F32)
    for l in reversed(range(L)):
        W, sv = weights(l), saved[l]
        modtab = sv["modtab"]
        d_mo, dg2 = _gate_bwd(dxo, sv["mo"], modtab, 5, TR=TR, P=P, name=f"gate2_bwd{l}")
        d_a = _matmul(d_mo, W["mlp2"], "nt", tm=TM, tn=tn_ff, tk=D, name=f"mlp2_bwd{l}", out_dtypes=(BF16,),
                      epilogue=lambda acc, a: (acc * (2.0 * jnp.maximum(a, 0.0)),),
                      extras=[(sv["a_pre"], (TM, tn_ff), lambda m, n: (m, n))])
        dw("w_mlp2", l, sv["r_act"], d_mo, tm=tk_d, tn=D, name=f"mlp2_dw{l}", blocked=False)
        d_h2 = _matmul(d_a, W["mlp1"], "nt", tm=TM, tn=D, tk=tn_ff, name=f"mlp1_bwd{l}")
        dw("w_mlp1", l, sv["h2"], d_a, tm=tk_d, tn=tn_ff, name=f"mlp1_dw{l}")
        dx1, dsh2, dsc2, dn2 = _norm_bwd(sv["x1"], d_h2, dxo, modtab, 4, small["norm2"][l][None], TR=TR, P=P,
                                         name=f"norm2_bwd{l}")
        d_ao, dg1 = _gate_bwd(dx1, sv["ao"], modtab, 2, TR=TR, P=P, name=f"gate1_bwd{l}")
        d_y = _matmul(d_ao, W["out"], "nt", tm=TM, tn=tn_d, tk=D, name=f"out_bwd{l}")
        dw("w_out", l, sv["y"], d_ao, tm=tk_d, tn=D, name=f"out_dw{l}", blocked=False)
        d_pa, d_pb, d_pc, d_ga, d_gb, d_gc = _merge_bwd(d_y, sv["oa"], sv["ob"], sv["oc"], sv["z"], W["a"], W["b"], W["c"],
                                                        D=D, TR=TMG, name=f"merge_bwd{l}")
        d_oa = _matmul(d_pa, W["a"], "nt", tm=TM, tn=Q_WIDTH, tk=tn_d, name=f"br_a_bwd{l}", out_dtypes=(BF16,))
        d_ob = _matmul(d_pb, W["b"], "nt", tm=TM, tn=POOL_WIDTH, tk=tn_d, name=f"br_b_bwd{l}")
        d_oc = _matmul(d_pc, W["c"], "nt", tm=TM, tn=Q_WIDTH, tk=tn_d, name=f"br_c_bwd{l}", out_dtypes=(BF16,))
        dw("w_br_a", l, sv["oa"], d_pa, tm=Q_WIDTH, tn=tn_d, name=f"br_a_dw{l}")
        dw("w_br_b", l, sv["ob"], d_pb, tm=POOL_WIDTH, tn=tn_d, name=f"br_b_dw{l}")
        dw("w_br_c", l, sv["oc"], d_pc, tm=Q_WIDTH, tn=tn_d, name=f"br_c_dw{l}")
        d_u, d_wbd, d_ps = _pool_bwd(d_ob, sv["pooled"], sv["w_bd"], sv["p_scale"], B=B, Tp=Tp, n_ctx=N, name=f"pool_bwd{l}")
        dqa, dka, dva = _attn_bwd(sv["q2"], sv["k2"], sv["v2"], d_oa, sv["oa32"], sv["lse_a"], None, branch=0, B=B,
                                  n_ctx=N, window=False, name=f"attn_a_bwd{l}")
        dqc, dkc, dvc, dsink = _attn_bwd(sv["q2"], sv["k2"], sv["v2"], d_oc, sv["oc32"], sv["lse_c"], sv["sink"],
                                         branch=1, B=B, n_ctx=N, window=True, name=f"attn_c_bwd{l}")
        dz_a, dgains_a = _qk_prep_bwd(sv["z"], dqa, dka, dva, sv["gains"], cos, sin, branch=0, TR=TR, P=P,
                                      name=f"qk_prep_a_bwd{l}")
        dz_c, dgains_c = _qk_prep_bwd(sv["z"], dqc, dkc, dvc, sv["gains"], cos, sin, branch=1, TR=TR, P=P,
                                      name=f"qk_prep_c_bwd{l}")
        dz = jnp.concatenate([dz_a, dz_c, d_u, d_ga, d_gb, d_gc], axis=1)
        d_h1 = _matmul(dz, W["w_in"], "nt", tm=TM, tn=D, tk=tn_in, name=f"in_bwd{l}")
        dw("w_in", l, sv["h1"], dz, tm=tk_d, tn=tn_in, name=f"in_dw{l}")
        dx0, dsh1, dsc1, dn1 = _norm_bwd(sv["x0"], d_h1, dx1, modtab, 1, small["norm1"][l][None], TR=TR, P=P,
                                         name=f"norm1_bwd{l}")

        dm_groups = jnp.concatenate([dsh1, dsc1, dg1, dsh2, dsc2, dg2], axis=-1).reshape(B, 2, 6 * D)
        dm = jnp.zeros((rows16, 6 * D), F32).at[:B].set(dm_groups[:, 1]).at[B].set(jnp.sum(dm_groups[:, 0], axis=0))
        dm_bf = dm.astype(BF16)
        d_s = _matmul(dm_bf, W["ada"], "nt", tm=rows16, tn=D, tk=tn_ada, name=f"ada_bwd{l}")
        dw("w_ada", l, s_rows, dm_bf, tm=tk_d, tn=tn_ada, tk=rows16, name=f"ada_dw{l}")
        db_ada, dcc = _ada_bwd_rows(dm, d_s, cc, f"ada_rows_bwd{l}")
        d_cctx = d_cctx + dcc[B]

        sm["b_ada"][l] = db_ada[0]
        sm["norm1"][l] = jnp.sum(dn1, axis=(0, 1))
        sm["norm2"][l] = jnp.sum(dn2, axis=(0, 1))
        dgh = jnp.stack([dgains_a, dgains_c]).reshape(2, QKV_WIDTH // HEAD_DIM, HEAD_DIM)
        sm["q_norm_a"][l] = jnp.sum(dgh[0, :N_HEADS], axis=0)
        sm["k_norm_a"][l] = jnp.sum(dgh[0, N_HEADS:N_HEADS + N_KV], axis=0)
        sm["q_norm_c"][l] = jnp.sum(dgh[1, :N_HEADS], axis=0)
        sm["k_norm_c"][l] = jnp.sum(dgh[1, N_HEADS:N_HEADS + N_KV], axis=0)
        sm["sink_c"][l] = jnp.sum(dsink[:, :N_HEADS, 0], axis=0)
        sm["w_pool"][l] = jnp.stack([d_wbd[g * POOL_CH:(g + 1) * POOL_CH, g * POOL_CH:(g + 1) * POOL_CH]
                                     for g in range(POOL_WIDTH // POOL_CH)])
        sm["pool_scale"][l] = d_ps[0]
        dxo = dx0

    grad_x = dxo.reshape(B, Tp, D)[:, N:]
    small_grads = {k: jnp.stack(v) for k, v in sm.items()}
    small_grads["c_ctx"] = d_cctx
    return loss, grad_x, small_grads, big, big16


SMALL_NAMES = ("c_ctx", "b_ada", "norm1", "norm2", "q_norm_a", "k_norm_a", "q_norm_c", "k_norm_c", "sink_c",
               "w_pool", "pool_scale")
BIG_NAMES = ("w_ada", "w_in", "w_br_a", "w_br_b", "w_br_c", "w_out", "w_mlp1", "w_mlp2")
WEIGHT_NAMES = ("c_ctx", "w_ada", "b_ada", "norm1", "norm2", "w_in", "q_norm_a", "k_norm_a", "q_norm_c", "k_norm_c",
                "sink_c", "w_pool", "pool_scale", "w_br_a", "w_br_b", "w_br_c", "w_out", "w_mlp1", "w_mlp2")


def _pack(parts, rows):
    flat = jnp.concatenate([p.reshape(-1).astype(F32) for p in parts])
    return jnp.pad(flat, (0, rows * LANES - flat.shape[0])).reshape(rows, LANES)


def _unpack(packed, like):
    flat, out, at = packed.reshape(-1), [], 0
    for p in like:
        out.append(flat[at:at + p.size].reshape(p.shape))
        at += p.size
    return out


def _reduce_big(partials, partials16):
    names = list(partials)
    assert all(len(partials[k]) == 2 for k in names)
    x, y, c = _place()
    core = c.astype(jnp.int32).reshape(1)
    chip = (2 * x + y).astype(jnp.int32).reshape(1)
    shapes = [partials[k][0].shape for k in names]
    flat = [[g.reshape(-1, g.shape[-1]) for g in partials[k]] for k in names]
    flat16 = [[g.reshape(-1, g.shape[-1]) for g in partials16[k]] for k in names]
    landed = _send_other_layer([f[0] for f in flat16], [f[1] for f in flat16], "grads_to_sibling")
    in_chip = [_add_own_layer(f, r, core, f"grads_add_sibling_{k}") for k, f, r in zip(names, flat, landed)]
    blocked = [h.reshape(s) for s, (h, _) in zip(shapes, in_chip)]
    blocked16 = [h.reshape(s) for s, (_, h) in zip(shapes, in_chip)]
    from_chips = _send_chip_blocks(blocked16, "grads_to_chips")
    reduced = [_sum_chips(h, r, chip, f"grads_sum_chips_{k}") for k, h, r in zip(names, blocked, from_chips)]
    shared = _share_layers(reduced, "grads_share_layers")
    return core, dict(zip(names, zip(reduced, shared)))


def kernel(x, c, ctx, c_ctx, w_ada, b_ada, norm1, norm2, w_in, q_norm_a, k_norm_a, q_norm_c, k_norm_c, sink_c, w_pool, pool_scale, w_br_a, w_br_b, w_br_c, w_out, w_mlp1, w_mlp2, loss_target, m_c_ctx, m_w_ada, m_b_ada, m_norm1, m_norm2, m_w_in, m_q_norm_a, m_k_norm_a, m_q_norm_c, m_k_norm_c, m_sink_c, m_w_pool, m_pool_scale, m_w_br_a, m_w_br_b, m_w_br_c, m_w_out, m_w_mlp1, m_w_mlp2, v_c_ctx, v_w_ada, v_b_ada, v_norm1, v_norm2, v_w_in, v_q_norm_a, v_k_norm_a, v_q_norm_c, v_k_norm_c, v_sink_c, v_w_pool, v_pool_scale, v_w_br_a, v_w_br_b, v_w_br_c, v_w_out, v_w_mlp1, v_w_mlp2):
    given = dict(locals())
    w = {k: given[k] for k in WEIGHT_NAMES}
    m = {k: given["m_" + k] for k in WEIGHT_NAMES}
    v = {k: given["v_" + k] for k in WEIGHT_NAMES}

    gathered = _gather_weights([w[k].astype(BF16) for k in BIG_NAMES], "gather_weights")
    gw = dict(zip(BIG_NAMES, gathered))
    small = {k: w[k] for k in SMALL_NAMES}
    loss_part, grad_x, small_grads, big_grads, big_grads16 = _local_step(x, c, ctx, c_ctx, small, gw, loss_target)

    core, reduced = _reduce_big({k: big_grads[k] for k in BIG_NAMES}, {k: big_grads16[k] for k in BIG_NAMES})
    grads, deltas, new_m, new_v = {}, {}, {}, {}
    for k in BIG_NAMES:
        mine, other = reduced[k]
        grads[k], deltas[k], new_m[k], new_v[k] = _adamw(w[k], mine, other, m[k], v[k], core, f"adamw_{k}")

    sizes = sum(w[k].size for k in SMALL_NAMES) + LANES
    rows = -(-sizes // (8 * LANES)) * 8
    parts = _gather_small(_pack([small_grads[k] for k in SMALL_NAMES] + [loss_part[0]], rows), "gather_small")
    zero = jnp.zeros((LANES,), F32)
    packed = [_pack([t[k] for k in SMALL_NAMES] + [zero], rows) for t in (w, m, v)]
    outs = _adamw_small(packed[0], parts.reshape(8, rows, LANES), packed[1], packed[2], "adamw_small")
    like = [w[k] for k in SMALL_NAMES] + [zero]
    for store, packed_out in zip((grads, deltas, new_m, new_v), outs):
        pieces = _unpack(packed_out, like)
        for k, piece in zip(SMALL_NAMES, pieces):
            store[k] = piece
        if store is grads:
            loss = pieces[-1][0]

    return (loss, grad_x, *[grads[k] for k in WEIGHT_NAMES], *[deltas[k] for k in WEIGHT_NAMES],
            *[new_m[k] for k in WEIGHT_NAMES], *[new_v[k] for k in WEIGHT_NAMES])
```

```python
import functools

import jax
import jax.numpy as jnp
from jax import lax
from jax.experimental import pallas as pl
from jax.experimental.pallas import tpu as pltpu

F32 = jnp.float32
BF16 = jnp.bfloat16

HEAD_DIM = 64
GRID_W = 64
AXIS_DIM = HEAD_DIM // 2
ROPE_THETA = 10000.0
N_HEADS = 6
N_KV = 2
N_GROUP = N_HEADS // N_KV
POOL_CH = 64
POOL_WIDTH = 256
POOL_WINDOWS = (2, 4, 8, 16)
WINDOW = 128
Q_BLOCK = 128
Q_WIDTH = N_HEADS * HEAD_DIM
KV_WIDTH = N_KV * HEAD_DIM
GATE_COL = 2 * (Q_WIDTH + 2 * KV_WIDTH) + POOL_WIDTH
U_COL = 2 * (Q_WIDTH + 2 * KV_WIDTH)
EPS = 1e-6
NEG = -1e30
ADAM_LR = 0.001
ADAM_B1 = 0.9
ADAM_B2 = 0.999
ADAM_EPS = 1e-08
ADAM_WD = 0.01
ADAM_STEP = 10

N_CHIPS = 4
LANES = 128
POOL_PAD = 16
VMEM_LIMIT = 48 * 1024 * 1024
MESH = pl.DeviceIdType.MESH
ANY = pl.BlockSpec(memory_space=pl.ANY)


def _params(sem):
    return pltpu.CompilerParams(dimension_semantics=sem, vmem_limit_bytes=VMEM_LIMIT)


def _sds(shape, dtype):
    return jax.ShapeDtypeStruct(tuple(shape), dtype)


class _Opnd:
    def __init__(self, arr, kind="plain", layer=None):
        self.arr, self.kind, self.layer = arr, kind, layer

    @property
    def shape(self):
        a = self.arr
        if self.kind == "plain":
            return a.shape
        if self.kind == "bcols":
            return (a.shape[2], N_CHIPS * a.shape[3])
        return (N_CHIPS * a.shape[2], a.shape[3])

    def spec(self, tr, tc, fn):
        a, layer = self.arr, self.layer
        if self.kind == "plain":
            return pl.BlockSpec((tr, tc), lambda *g: fn(*g))
        if self.kind == "bcols":
            assert a.shape[3] % tc == 0, (a.shape, tc)
            per = a.shape[3] // tc

            def im(*g):
                ri, ci = fn(*g)
                return (ci // per, layer, ri, ci % per)
            return pl.BlockSpec((None, None, tr, tc), im)
        assert a.shape[2] % tr == 0, (a.shape, tr)
        per = a.shape[2] // tr

        def im(*g):
            ri, ci = fn(*g)
            return (ri // per, layer, ri % per, ci)
        return pl.BlockSpec((None, None, tr, tc), im)


def _matmul(a, b, mode, *, tm, tn, tk, name, out_dtypes=(F32,), epilogue=None, extras=(), out_blocked=False):
    if not isinstance(a, _Opnd):
        a = _Opnd(a)
    if not isinstance(b, _Opnd):
        b = _Opnd(b)
    if mode == "nn":
        (M, K), (K2, N) = a.shape, b.shape
        a_spec = a.spec(tm, tk, lambda m, n, k: (m, k))
        b_spec = b.spec(tk, tn, lambda m, n, k: (k, n))
        dims = (((1,), (0,)), ((), ()))
    elif mode == "nt":
        (M, K), (N, K2) = a.shape, b.shape
        a_spec = a.spec(tm, tk, lambda m, n, k: (m, k))
        b_spec = b.spec(tn, tk, lambda m, n, k: (n, k))
        dims = (((1,), (1,)), ((), ()))
    else:
        (K, M), (K2, N) = a.shape, b.shape
        a_spec = a.spec(tk, tm, lambda m, n, k: (k, m))
        b_spec = b.spec(tk, tn, lambda m, n, k: (k, n))
        dims = (((0,), (0,)), ((), ()))
    assert K == K2 and M % tm == 0 and N % tn == 0 and K % tk == 0, (name, M, N, K, K2, tm, tn, tk)
    nk = K // tk
    n_extra = len(extras)
    n_out = len(out_dtypes)
    extra_specs = [pl.BlockSpec(bs, functools.partial(lambda m, n, k, f: f(m, n), f=f)) for (_, bs, f) in extras]
    if out_blocked:
        assert (N // N_CHIPS) % tn == 0
        per = (N // N_CHIPS) // tn
        out_shape = [_sds((N_CHIPS, M, N // N_CHIPS), dt) for dt in out_dtypes]
        out_specs = [pl.BlockSpec((None, tm, tn), lambda m, n, k: (n // per, m, n % per)) for _ in out_dtypes]
    else:
        out_shape = [_sds((M, N), dt) for dt in out_dtypes]
        out_specs = [pl.BlockSpec((tm, tn), lambda m, n, k: (m, n)) for _ in out_dtypes]

    in_place = nk > 1 and epilogue is None and out_dtypes[0] == F32

    def body(*refs):
        a_ref, b_ref = refs[0], refs[1]
        extra_refs = refs[2:2 + n_extra]
        out_refs = refs[2 + n_extra:2 + n_extra + n_out]
        acc_ref = out_refs[0] if in_place else (refs[2 + n_extra + n_out] if nk > 1 else None)
        k = pl.program_id(2)
        prod = lax.dot_general(a_ref[...].astype(BF16), b_ref[...].astype(BF16), dims, preferred_element_type=F32)

        def finish(acc):
            outs = epilogue(acc, *[r[...] for r in extra_refs]) if epilogue is not None else (acc,) * n_out
            for o_ref, o in zip(out_refs, outs):
                o_ref[...] = o.astype(o_ref.dtype)

        if nk == 1:
            finish(prod)
        elif in_place:
            @pl.when(k == 0)
            def _():
                acc_ref[...] = prod

            @pl.when(k > 0)
            def _():
                acc_ref[...] += prod

            if n_out > 1:
                @pl.when(k == nk - 1)
                def _():
                    for o_ref in out_refs[1:]:
                        o_ref[...] = acc_ref[...].astype(o_ref.dtype)
        else:
            @pl.when(k == 0)
            def _():
                acc_ref[...] = prod

            @pl.when(k > 0)
            def _():
                acc_ref[...] += prod

            @pl.when(k == nk - 1)
            def _():
                finish(acc_ref[...])

    outs = pl.pallas_call(
        body, name=name, grid=(M // tm, N // tn, nk),
        in_specs=[a_spec, b_spec] + extra_specs, out_specs=out_specs, out_shape=out_shape,
        scratch_shapes=[pltpu.VMEM((tm, tn), F32)] if nk > 1 and not in_place else [],
        compiler_params=_params(("parallel", "parallel", "arbitrary")),
    )(a.arr, b.arr, *[e[0] for e in extras])
    return outs[0] if n_out == 1 else outs


def _tile(n, cands):
    for t in cands:
        if n % t == 0:
            return t
    return n


def _grp(i, P):
    return 2 * (i // P) + jnp.minimum(i % P, 1)


def _mod_spec(D, P, part):
    return pl.BlockSpec((1, 1, D), lambda i: (_grp(i, P), 0, part))


def _res_norm(x, pending, modtab, shift_part, scale_part, gain, *, TR, P, name):
    T, D = x.shape
    row = pl.BlockSpec((TR, D), lambda i: (i, 0))
    has_branch = pending is not None
    ins, specs = [x], [row]
    if has_branch:
        branch, gate_tab, gate_part = pending
        ins += [branch, gate_tab]
        specs += [row, _mod_spec(D, P, gate_part)]
    ins += [modtab, modtab, gain]
    specs += [_mod_spec(D, P, shift_part), _mod_spec(D, P, scale_part), pl.BlockSpec((1, D), lambda i: (0, 0))]

    def body(*refs):
        if has_branch:
            x_ref, br_ref, g_ref, sh_ref, sc_ref, gn_ref, xo_ref, h_ref = refs
            xv = x_ref[...] + g_ref[0] * br_ref[...]
        else:
            x_ref, sh_ref, sc_ref, gn_ref, xo_ref, h_ref = refs
            xv = x_ref[...]
        xo_ref[...] = xv
        y = xv * lax.rsqrt(jnp.mean(xv * xv, axis=-1, keepdims=True) + EPS) * gn_ref[...]
        h_ref[...] = (y * (1.0 + sc_ref[0]) + sh_ref[0]).astype(BF16)

    return pl.pallas_call(
        body, name=name, grid=(T // TR,), in_specs=specs, out_specs=[row, row],
        out_shape=[_sds((T, D), F32), _sds((T, D), BF16)], compiler_params=_params(("parallel",)),
    )(*ins)


def _gate_bwd(dx, branch, modtab, gate_part, *, TR, P, name):
    T, D = dx.shape
    G = modtab.shape[0]
    row = pl.BlockSpec((TR, D), lambda i: (i, 0))
    acc = pl.BlockSpec((1, 1, D), lambda i: (_grp(i, P), 0, 0))

    def body(dx_ref, br_ref, g_ref, db_ref, dg_ref):
        r = pl.program_id(0) % P
        dxv = dx_ref[...]
        db_ref[...] = (dxv * g_ref[0]).astype(BF16)
        part = jnp.sum(dxv * br_ref[...], axis=0, keepdims=True)

        @pl.when(r <= 1)
        def _():
            dg_ref[0] = part

        @pl.when(r > 1)
        def _():
            dg_ref[0] += part

    return pl.pallas_call(
        body, name=name, grid=(T // TR,), in_specs=[row, row, _mod_spec(D, P, gate_part)], out_specs=[row, acc],
        out_shape=[_sds((T, D), BF16), _sds((G, 1, D), F32)], compiler_params=_params(("arbitrary",)),
    )(dx, branch, modtab)


def _norm_bwd(x, dh, dres, modtab, scale_part, gain, *, TR, P, name):
    T, D = x.shape
    G = modtab.shape[0]
    row = pl.BlockSpec((TR, D), lambda i: (i, 0))
    acc = pl.BlockSpec((1, 1, D), lambda i: (_grp(i, P), 0, 0))

    def body(x_ref, dh_ref, dres_ref, sc_ref, gn_ref, dx_ref, dsh_ref, dsc_ref, dgn_ref):
        r = pl.program_id(0) % P
        xv, dhv, gn = x_ref[...], dh_ref[...], gn_ref[...]
        rstd = lax.rsqrt(jnp.mean(xv * xv, axis=-1, keepdims=True) + EPS)
        xhat = xv * rstd
        dn = dhv * (1.0 + sc_ref[0])
        dxhat = dn * gn
        dx_ref[...] = dres_ref[...] + rstd * (dxhat - xhat * jnp.mean(dxhat * xhat, axis=-1, keepdims=True))
        p_sh = jnp.sum(dhv, axis=0, keepdims=True)
        p_sc = jnp.sum(dhv * (xhat * gn), axis=0, keepdims=True)
        p_gn = jnp.sum(dn * xhat, axis=0, keepdims=True)

        @pl.when(r <= 1)
        def _():
            dsh_ref[0] = p_sh
            dsc_ref[0] = p_sc
            dgn_ref[0] = p_gn

        @pl.when(r > 1)
        def _():
            dsh_ref[0] += p_sh
            dsc_ref[0] += p_sc
            dgn_ref[0] += p_gn

    return pl.pallas_call(
        body, name=name, grid=(T // TR,),
        in_specs=[row, row, row, _mod_spec(D, P, scale_part), pl.BlockSpec((1, D), lambda i: (0, 0))],
        out_specs=[row, acc, acc, acc],
        out_shape=[_sds((T, D), F32)] + [_sds((G, 1, D), F32)] * 3, compiler_params=_params(("arbitrary",)),
    )(x, dh, dres, modtab, gain)


def _loss_head(x, branch, modtab, gate_part, target, *, TR, P, name):
    T, D = x.shape
    row = pl.BlockSpec((TR, D), lambda i: (i, 0))
    tgt = pl.BlockSpec((TR, D), lambda i: ((i // P) * (P - 1) + jnp.maximum(i % P - 1, 0), 0))
    one = pl.BlockSpec((1, LANES), lambda i: (0, 0))

    def body(x_ref, br_ref, g_ref, t_ref, dy_ref, loss_ref):
        i = pl.program_id(0)
        r = i % P

        @pl.when(i == 0)
        def _():
            loss_ref[...] = jnp.zeros_like(loss_ref)

        @pl.when(r == 0)
        def _():
            dy_ref[...] = jnp.zeros_like(dy_ref)

        @pl.when(r > 0)
        def _():
            err = x_ref[...] + g_ref[0] * br_ref[...] - t_ref[...]
            dy_ref[...] = err / D
            per_tok = jnp.mean(err * err, axis=-1, keepdims=True)
            loss_ref[...] += 0.5 * jnp.sum(per_tok, axis=0, keepdims=True)

    return pl.pallas_call(
        body, name=name, grid=(T // TR,), in_specs=[row, row, _mod_spec(D, P, gate_part), tgt], out_specs=[row, one],
        out_shape=[_sds((T, D), F32), _sds((1, LANES), F32)], compiler_params=_params(("arbitrary",)),
    )(x, branch, modtab, target)


QKV_WIDTH = Q_WIDTH + 2 * KV_WIDTH
QK_NORMED = 4


def _seg_mean(v):
    lane = lax.broadcasted_iota(jnp.int32, v.shape, 1)
    lo = lane < HEAD_DIM
    s0 = jnp.sum(jnp.where(lo, v, 0.0), axis=-1, keepdims=True)
    s1 = jnp.sum(jnp.where(lo, 0.0, v), axis=-1, keepdims=True)
    return jnp.where(lo, s0, s1) * (1.0 / HEAD_DIM)


def _pair_swap(v):
    lane = lax.broadcasted_iota(jnp.int32, v.shape, 1)
    return jnp.where((lane & 1) == 0, pltpu.roll(v, LANES - 1, 1), pltpu.roll(v, 1, 1))


def _chunk(c):
    return slice(c * LANES, (c + 1) * LANES)


def _qk_prep(z, gains, cos, sin, *, TR, P, name):
    T = z.shape[0]

    def body(z_ref, g_ref, c_ref, s_ref, q_ref, k_ref, v_ref):
        cs, sn = c_ref[...], s_ref[...]
        for ch in range(QK_NORMED):
            xv = z_ref[:, _chunk(ch)]
            y = xv * lax.rsqrt(_seg_mean(xv * xv) + EPS) * g_ref[0, :, _chunk(ch)]
            out = (y * cs + _pair_swap(y) * sn).astype(BF16)
            if ch < QK_NORMED - 1:
                q_ref[:, _chunk(ch)] = out
            else:
                k_ref[...] = out
        v_ref[...] = z_ref[:, _chunk(QK_NORMED)].astype(BF16)

    def out(width):
        return pl.BlockSpec((None, TR, width), lambda i, j: (j, i, 0))
    return pl.pallas_call(
        body, name=name, grid=(T // TR, 2),
        in_specs=[pl.BlockSpec((TR, QKV_WIDTH), lambda i, j: (i, j)),
                  pl.BlockSpec((1, 1, QKV_WIDTH), lambda i, j: (j, 0, 0)),
                  pl.BlockSpec((TR, LANES), lambda i, j: (i % P, 0)),
                  pl.BlockSpec((TR, LANES), lambda i, j: (i % P, 0))],
        out_specs=[out(Q_WIDTH), out(KV_WIDTH), out(KV_WIDTH)],
        out_shape=[_sds((2, T, Q_WIDTH), BF16), _sds((2, T, KV_WIDTH), BF16), _sds((2, T, KV_WIDTH), BF16)],
        compiler_params=_params(("parallel", "parallel")),
    )(z, gains, cos, sin)


def _qk_prep_bwd(z, dq, dk, dv, gains, cos, sin, *, branch, TR, P, name):
    T = z.shape[0]
    nt = T // TR

    def body(z_ref, dq_ref, dk_ref, dv_ref, g_ref, c_ref, s_ref, dz_ref, dg_ref):
        i = pl.program_id(0)
        cs, sn = c_ref[...], s_ref[...]
        parts = []
        for ch in range(QK_NORMED):
            xv, g = z_ref[:, _chunk(ch)], g_ref[0, :, _chunk(ch)]
            dout = dq_ref[:, _chunk(ch)] if ch < QK_NORMED - 1 else dk_ref[...]
            dy = dout * cs + _pair_swap(dout * sn)
            rstd = lax.rsqrt(_seg_mean(xv * xv) + EPS)
            xhat = xv * rstd
            dxhat = dy * g
            dz_ref[:, _chunk(ch)] = (rstd * (dxhat - xhat * _seg_mean(dxhat * xhat))).astype(BF16)
            parts.append(jnp.sum(dy * xhat, axis=0, keepdims=True))
        dz_ref[:, _chunk(QK_NORMED)] = dv_ref[...].astype(BF16)
        parts.append(jnp.zeros((1, LANES), F32))
        part = jnp.concatenate(parts, axis=1)

        @pl.when(i == 0)
        def _():
            dg_ref[0] = part

        @pl.when(i > 0)
        def _():
            dg_ref[0] += part

    def rows(width, col=0):
        return pl.BlockSpec((TR, width), lambda i: (i, col))
    return pl.pallas_call(
        body, name=name, grid=(nt,),
        in_specs=[rows(QKV_WIDTH, branch), rows(Q_WIDTH), rows(KV_WIDTH), rows(KV_WIDTH),
                  pl.BlockSpec((1, 1, QKV_WIDTH), lambda i: (branch, 0, 0)),
                  pl.BlockSpec((TR, LANES), lambda i: (i % P, 0)),
                  pl.BlockSpec((TR, LANES), lambda i: (i % P, 0))],
        out_specs=[rows(QKV_WIDTH), pl.BlockSpec((1, 1, QKV_WIDTH), lambda i: (0, 0, 0))],
        out_shape=[_sds((T, QKV_WIDTH), BF16), _sds((1, 1, QKV_WIDTH), F32)],
        compiler_params=_params(("arbitrary",)),
    )(z, dq, dk, dv, gains, cos, sin)


NT_DIMS = (((1,), (1,)), ((), ()))
TN_DIMS = (((0,), (0,)), ((), ()))
QROWS = N_GROUP * Q_BLOCK
SCORE_SCALE = HEAD_DIM ** -0.5
BAND = Q_BLOCK + 2 * WINDOW
FWD_LATENT_CHUNK = 256
BWD_LATENT_CHUNK = 1024


def _move_head(block, half_from, half_to):
    lane = lax.broadcasted_iota(jnp.int32, block.shape, 1)
    src = block if half_from == half_to else pltpu.roll(block, HEAD_DIM, 1)
    keep = (lane < HEAD_DIM) if half_to == 0 else (lane >= HEAD_DIM)
    return jnp.where(keep, src, 0.0)


def _stack_heads(lane_block, j):
    pieces = []
    for h in range(N_GROUP * j, N_GROUP * (j + 1)):
        pieces.append(_move_head(lane_block(h // 2), h % 2, j))
    return jnp.concatenate(pieces, axis=0)


def _lane_blocks(ref):
    return lambda m: ref[:, m * LANES:(m + 1) * LANES].astype(F32)


def _unstack_heads(stacked, ref):
    heads = []
    for h in range(N_HEADS):
        j, r = h // N_GROUP, h % N_GROUP
        heads.append(_move_head(stacked[j][r * Q_BLOCK:(r + 1) * Q_BLOCK], j, h % 2))
    for m in range(N_HEADS // 2):
        ref[:, m * LANES:(m + 1) * LANES] = (heads[2 * m] + heads[2 * m + 1]).astype(ref.dtype)


def _key_chunks(i, latent, *, n_ctx, t_all, window, chunk, latent_chunk):
    ctx = [(s, chunk, False) for s in range(0, n_ctx, chunk)]
    if not latent:
        return ctx
    if not window:
        wide = latent_chunk if (t_all - n_ctx) % latent_chunk == 0 else chunk
        return ctx + [(s, wide, False) for s in range(n_ctx, t_all, wide)]
    start = pl.multiple_of(jnp.minimum((i - 1) * Q_BLOCK, t_all - BAND), Q_BLOCK)
    band_chunk = BAND if latent_chunk >= BAND else (chunk if BAND % chunk == 0 else Q_BLOCK)
    return ctx + [(start + s, band_chunk, True) for s in range(0, BAND, band_chunk)]


def _scores(q, k_ref, i, start, size, masked, *, n_ctx):
    s = lax.dot_general(q, k_ref[pl.ds(start, size), :], NT_DIMS, preferred_element_type=F32)
    if masked:
        qpos = (i * Q_BLOCK - n_ctx) + (lax.broadcasted_iota(jnp.int32, (QROWS, size), 0) & (Q_BLOCK - 1))
        kpos = (start - n_ctx) + lax.broadcasted_iota(jnp.int32, (QROWS, size), 1)
        valid = (kpos - qpos <= WINDOW) & (qpos - kpos <= WINDOW) & (kpos >= 0)
        s = jnp.where(valid, s, NEG)
    return s


def _sink_column(sink_ref, j):
    r = lax.broadcasted_iota(jnp.int32, (QROWS, 1), 0)
    s0, s1, s2 = sink_ref[j * N_GROUP], sink_ref[j * N_GROUP + 1], sink_ref[j * N_GROUP + 2]
    return jnp.where(r < Q_BLOCK, s0, jnp.where(r < 2 * Q_BLOCK, s1, s2))


def _attn_specs(Tp, branch):
    nq = Tp // Q_BLOCK
    q_in = pl.BlockSpec((None, Q_BLOCK, Q_WIDTH), lambda b, i: (branch, b * nq + i, 0))
    kv_in = pl.BlockSpec((None, Tp, KV_WIDTH), lambda b, i: (branch, b, 0))
    q_out = pl.BlockSpec((Q_BLOCK, Q_WIDTH), lambda b, i: (b * nq + i, 0))
    kv_out = pl.BlockSpec((Tp, KV_WIDTH), lambda b, i: (b, 0))
    return q_in, kv_in, q_out, kv_out


def _attn_chunk(Tp):
    return 256 if Tp % 256 == 0 else Q_BLOCK


def _attn_fwd(q, k, v, sink, *, branch, B, n_ctx, window, name):
    T = q.shape[1]
    Tp = T // B
    nq = Tp // Q_BLOCK
    has_sink = sink is not None
    q_in, kv_in, q_out, _ = _attn_specs(Tp, branch)
    lse_spec = pl.BlockSpec((None, N_KV * QROWS, 1), lambda b, i: (b * nq + i, 0, 0))

    def body(*refs):
        if has_sink:
            sink_ref, q_ref, k_ref, v_ref, o_ref, o32_ref, lse_ref = refs
        else:
            q_ref, k_ref, v_ref, o_ref, o32_ref, lse_ref = refs
        i = pl.program_id(1)

        def run(latent):
            outs = []
            for j in range(N_KV):
                qv = (_stack_heads(_lane_blocks(q_ref), j) * SCORE_SCALE).astype(BF16)
                if has_sink:
                    m, l = _sink_column(sink_ref, j), jnp.ones((QROWS, 1), F32)
                else:
                    m, l = jnp.full((QROWS, 1), NEG, F32), jnp.zeros((QROWS, 1), F32)
                acc = jnp.zeros((QROWS, LANES), F32)
                for start, size, masked in _key_chunks(i, latent, n_ctx=n_ctx, t_all=Tp, window=window,
                                                       chunk=_attn_chunk(Tp), latent_chunk=FWD_LATENT_CHUNK):
                    s = _scores(qv, k_ref, i, start, size, masked, n_ctx=n_ctx)
                    m_new = jnp.maximum(m, jnp.max(s, axis=-1, keepdims=True))
                    alpha = jnp.exp(m - m_new)
                    p = jnp.exp(s - m_new)
                    l = l * alpha + jnp.sum(p, axis=-1, keepdims=True)
                    acc = acc * alpha + jnp.dot(p.astype(BF16), v_ref[pl.ds(start, size), :], preferred_element_type=F32)
                    m = m_new
                outs.append(acc * (1.0 / l))
                lse_ref[j * QROWS:(j + 1) * QROWS, :] = m + jnp.log(l)
            _unstack_heads(outs, o_ref)
            _unstack_heads(outs, o32_ref)

        @pl.when(i < n_ctx // Q_BLOCK)
        def _():
            run(False)

        @pl.when(i >= n_ctx // Q_BLOCK)
        def _():
            run(True)

    ins, specs = [q, k, v], [q_in, kv_in, kv_in]
    if has_sink:
        ins, specs = [sink] + ins, [pl.BlockSpec(memory_space=pltpu.SMEM)] + specs
    return pl.pallas_call(
        body, name=name, grid=(B, nq), in_specs=specs, out_specs=[q_out, q_out, lse_spec],
        out_shape=[_sds((T, Q_WIDTH), BF16), _sds((T, Q_WIDTH), F32), _sds((T // Q_BLOCK, N_KV * QROWS, 1), F32)],
        compiler_params=_params(("parallel", "parallel")),
    )(*ins)


def _attn_bwd(q, k, v, do, o32, lse, sink, *, branch, B, n_ctx, window, name):
    T = q.shape[1]
    Tp = T // B
    nq = Tp // Q_BLOCK
    has_sink = sink is not None
    q_in, kv_in, q_out, kv_out = _attn_specs(Tp, branch)
    lse_spec = pl.BlockSpec((None, N_KV * QROWS, 1), lambda b, i: (b * nq + i, 0, 0))
    sink_spec = pl.BlockSpec((None, 8, LANES), lambda b, i: (b, 0, 0))

    def body(*refs):
        if has_sink:
            sink_ref, q_ref, k_ref, v_ref, do_ref, o_ref, lse_ref, dq_ref, dk_ref, dv_ref, ds_ref, dkt_ref, dvt_ref = refs
        else:
            q_ref, k_ref, v_ref, do_ref, o_ref, lse_ref, dq_ref, dk_ref, dv_ref, dkt_ref, dvt_ref = refs
        i = pl.program_id(1)

        @pl.when(i == 0)
        def _():
            dk_ref[...] = jnp.zeros_like(dk_ref)
            dv_ref[...] = jnp.zeros_like(dv_ref)
            if not window:
                dkt_ref[...] = jnp.zeros_like(dkt_ref)
                dvt_ref[...] = jnp.zeros_like(dvt_ref)
            if has_sink:
                ds_ref[...] = jnp.zeros_like(ds_ref)

        def run(latent):
            upd = jnp.zeros((8, LANES), F32)
            do_blocks, o_blocks = _lane_blocks(do_ref), _lane_blocks(o_ref)
            qvs = [(_stack_heads(_lane_blocks(q_ref), j) * SCORE_SCALE).astype(BF16) for j in range(N_KV)]
            dovs = [_stack_heads(do_blocks, j).astype(BF16) for j in range(N_KV)]
            deltas = [jnp.sum(_stack_heads(lambda m: do_blocks(m) * o_blocks(m), j), axis=-1, keepdims=True)
                      for j in range(N_KV)]
            lses = [lse_ref[j * QROWS:(j + 1) * QROWS, :] for j in range(N_KV)]
            q_all, do_all = jnp.concatenate(qvs, axis=0), jnp.concatenate(dovs, axis=0)
            q_all_t, do_all_t = q_all.T, do_all.T
            dqs = [jnp.zeros((QROWS, LANES), F32) for _ in range(N_KV)]
            for start, size, masked in _key_chunks(i, latent, n_ctx=n_ctx, t_all=Tp, window=window,
                                                   chunk=_attn_chunk(Tp), latent_chunk=BWD_LATENT_CHUNK):
                rows = pl.ds(start, size)
                ds_all, p_all = [], []
                for j in range(N_KV):
                    p = jnp.exp(_scores(qvs[j], k_ref, i, start, size, masked, n_ctx=n_ctx) - lses[j])
                    dp = lax.dot_general(dovs[j], v_ref[rows, :], NT_DIMS, preferred_element_type=F32)
                    ds = (p * (dp - deltas[j])).astype(BF16)
                    dqs[j] = dqs[j] + jnp.dot(ds, k_ref[rows, :], preferred_element_type=F32)
                    ds_all.append(ds)
                    p_all.append(p.astype(BF16))
                ds_cat, p_cat = jnp.concatenate(ds_all, axis=0), jnp.concatenate(p_all, axis=0)
                if window:
                    dk_ref[rows, :] += lax.dot_general(ds_cat, q_all, TN_DIMS, preferred_element_type=F32)
                    dv_ref[rows, :] += lax.dot_general(p_cat, do_all, TN_DIMS, preferred_element_type=F32)
                else:
                    dkt_ref[:, start:start + size] += jnp.dot(q_all_t, ds_cat, preferred_element_type=F32)
                    dvt_ref[:, start:start + size] += jnp.dot(do_all_t, p_cat, preferred_element_type=F32)
            dqs = [dq * SCORE_SCALE for dq in dqs]
            for j in range(N_KV):
                if has_sink:
                    contrib = -(jnp.exp(_sink_column(sink_ref, j) - lses[j]) * deltas[j])
                    r = lax.broadcasted_iota(jnp.int32, (QROWS, 1), 0)
                    row8 = lax.broadcasted_iota(jnp.int32, (8, LANES), 0)
                    for h in range(N_GROUP):
                        in_head = (r >= h * Q_BLOCK) & (r < (h + 1) * Q_BLOCK)
                        tot = jnp.sum(jnp.where(in_head, contrib, 0.0), axis=0, keepdims=True)
                        upd = upd + jnp.where(row8 == j * N_GROUP + h, tot, 0.0)
            _unstack_heads(dqs, dq_ref)
            if has_sink:
                ds_ref[...] += upd

        @pl.when(i < n_ctx // Q_BLOCK)
        def _():
            run(False)

        @pl.when(i >= n_ctx // Q_BLOCK)
        def _():
            run(True)

        if not window:
            @pl.when(i == nq - 1)
            def _():
                dk_ref[...] += dkt_ref[...].T
                dv_ref[...] += dvt_ref[...].T

    ins, specs = [q, k, v, do, o32, lse], [q_in, kv_in, kv_in, q_out, q_out, lse_spec]
    out_specs = [q_out, kv_out, kv_out]
    out_shape = [_sds((T, Q_WIDTH), F32), _sds((T, KV_WIDTH), F32), _sds((T, KV_WIDTH), F32)]
    if has_sink:
        ins, specs = [sink] + ins, [pl.BlockSpec(memory_space=pltpu.SMEM)] + specs
        out_specs.append(sink_spec)
        out_shape.append(_sds((B, 8, LANES), F32))
    return pl.pallas_call(
        body, name=name, grid=(B, Tp // Q_BLOCK), in_specs=specs, out_specs=out_specs, out_shape=out_shape,
        scratch_shapes=[pltpu.VMEM((KV_WIDTH, LANES if window else Tp), F32)] * 2,
        compiler_params=_params(("parallel", "arbitrary")),
    )(*ins)


def _window_sums(xp):
    n = xp.shape[0]

    def ahead(a, k):
        return pltpu.roll(a, n - k, 0)
    a2 = xp + ahead(xp, 1)
    a4 = a2 + ahead(a2, 2)
    a8 = a4 + ahead(a4, 4)
    a16 = a8 + ahead(a8, 8)
    return (a2, a4, a8, a16)


def _by_group(vals):
    lane = lax.broadcasted_iota(jnp.int32, vals[0].shape, 1)
    return jnp.where(lane < POOL_CH, vals[0], jnp.where(lane < 2 * POOL_CH, vals[1],
                     jnp.where(lane < 3 * POOL_CH, vals[2], vals[3])))


def _pool_counts(n):
    t = lax.broadcasted_iota(jnp.int32, (n, POOL_WIDTH), 0)
    cnts = [(jnp.minimum(t + w // 2, n) - jnp.maximum(t - w // 2, 0)).astype(F32) for w in POOL_WINDOWS]
    return _by_group(cnts)


def _pad_rows(x):
    zeros = jnp.zeros((POOL_PAD, x.shape[1]), x.dtype)
    return jnp.concatenate([zeros, x, zeros], axis=0)


def _pool_stream(u):
    n = u.shape[0]
    sums = _window_sums(_pad_rows(u))
    tots = [pltpu.roll(a, w // 2, 0)[POOL_PAD:POOL_PAD + n] for a, w in zip(sums, POOL_WINDOWS)]
    return _by_group(tots) / _pool_counts(n) - u


def _pool_stream_t(dp):
    n = dp.shape[0]
    sums = _window_sums(_pad_rows(dp / _pool_counts(n)))
    tots = [pltpu.roll(a, w // 2 - 1, 0)[POOL_PAD:POOL_PAD + n] if w > 2 else a[POOL_PAD:POOL_PAD + n]
            for a, w in zip(sums, POOL_WINDOWS)]
    return _by_group(tots) - dp


def _pool_fwd(z, w_bd, scale, *, B, Tp, n_ctx, name):
    T = z.shape[0]
    blk = pl.BlockSpec((Tp, POOL_WIDTH), lambda b: (b, U_COL // POOL_WIDTH))
    out = pl.BlockSpec((Tp, POOL_WIDTH), lambda b: (b, 0))

    def body(u_ref, w_ref, s_ref, p_ref, o_ref):
        for lo, hi in ((0, n_ctx), (n_ctx, Tp)):
            pooled = _pool_stream(u_ref[lo:hi, :]).astype(BF16)
            p_ref[lo:hi, :] = pooled
            mixed = jnp.dot(pooled, w_ref[...], preferred_element_type=F32)
            o_ref[lo:hi, :] = (mixed * s_ref[...]).astype(BF16)

    return pl.pallas_call(
        body, name=name, grid=(B,),
        in_specs=[blk, pl.BlockSpec((POOL_WIDTH, POOL_WIDTH), lambda b: (0, 0)), pl.BlockSpec((1, POOL_WIDTH), lambda b: (0, 0))],
        out_specs=[out, out], out_shape=[_sds((T, POOL_WIDTH), BF16)] * 2, compiler_params=_params(("parallel",)),
    )(z, w_bd, scale)


def _pool_bwd(d_ob, pooled, w_bd, scale, *, B, Tp, n_ctx, name):
    T = d_ob.shape[0]
    blk = pl.BlockSpec((Tp, POOL_WIDTH), lambda b: (b, 0))
    wsp = pl.BlockSpec((POOL_WIDTH, POOL_WIDTH), lambda b: (0, 0))
    ssp = pl.BlockSpec((1, POOL_WIDTH), lambda b: (0, 0))

    def body(d_ref, p_ref, w_ref, s_ref, du_ref, dw_ref, dsc_ref):
        @pl.when(pl.program_id(0) == 0)
        def _():
            dw_ref[...] = jnp.zeros_like(dw_ref)
            dsc_ref[...] = jnp.zeros_like(dsc_ref)

        dv, pv, wv = d_ref[...], p_ref[...], w_ref[...]
        mixed = jnp.dot(pv, wv, preferred_element_type=F32)
        dsc_ref[...] += jnp.sum(dv * mixed, axis=0, keepdims=True)
        dmixed = (dv * s_ref[...]).astype(BF16)
        dw_ref[...] += lax.dot_general(pv, dmixed, TN_DIMS, preferred_element_type=F32)
        dpooled = lax.dot_general(dmixed, wv, NT_DIMS, preferred_element_type=F32)
        for lo, hi in ((0, n_ctx), (n_ctx, Tp)):
            du_ref[lo:hi, :] = _pool_stream_t(dpooled[lo:hi, :]).astype(BF16)

    return pl.pallas_call(
        body, name=name, grid=(B,), in_specs=[blk, blk, wsp, ssp], out_specs=[blk, wsp, ssp],
        out_shape=[_sds((T, POOL_WIDTH), BF16), _sds((POOL_WIDTH, POOL_WIDTH), F32), _sds((1, POOL_WIDTH), F32)],
        compiler_params=_params(("arbitrary",)),
    )(d_ob, pooled, w_bd, scale)


def _merge_specs(z, D, TR, tc, wa, wb, wc):
    def act(width):
        return pl.BlockSpec((TR, width), lambda i, n: (i, 0))

    def gate(part):
        return pl.BlockSpec((TR, tc), lambda i, n: (i, (GATE_COL + part * D) // tc + n))
    w_specs = [w.spec(w.shape[0], tc, lambda i, n: (0, n)) for w in (wa, wb, wc)]
    return [act(Q_WIDTH), act(POOL_WIDTH), act(Q_WIDTH), gate(0), gate(1), gate(2)] + w_specs


def _merge_fwd(oa, ob, oc, z, wa, wb, wc, *, D, TR, name):
    T = oa.shape[0]
    tc = D // N_CHIPS

    def body(oa_ref, ob_ref, oc_ref, ga_ref, gb_ref, gc_ref, wa_ref, wb_ref, wc_ref, y_ref):
        acc = jax.nn.sigmoid(ga_ref[...]) * jnp.dot(oa_ref[...], wa_ref[...], preferred_element_type=F32)
        acc += jax.nn.sigmoid(gb_ref[...]) * jnp.dot(ob_ref[...], wb_ref[...], preferred_element_type=F32)
        acc += jax.nn.sigmoid(gc_ref[...]) * jnp.dot(oc_ref[...], wc_ref[...], preferred_element_type=F32)
        y_ref[...] = acc.astype(BF16)

    return pl.pallas_call(
        body, name=name, grid=(T // TR, D // tc), in_specs=_merge_specs(z, D, TR, tc, wa, wb, wc),
        out_specs=pl.BlockSpec((TR, tc), lambda i, n: (i, n)), out_shape=_sds((T, D), BF16),
        compiler_params=_params(("parallel", "parallel")),
    )(oa, ob, oc, z, z, z, wa.arr, wb.arr, wc.arr)


def _merge_bwd(dy, oa, ob, oc, z, wa, wb, wc, *, D, TR, name):
    T = oa.shape[0]
    tc = D // N_CHIPS
    out = pl.BlockSpec((TR, tc), lambda i, n: (i, n))

    def body(dy_ref, oa_ref, ob_ref, oc_ref, ga_ref, gb_ref, gc_ref, wa_ref, wb_ref, wc_ref,
             dpa_ref, dpb_ref, dpc_ref, dga_ref, dgb_ref, dgc_ref):
        dyv = dy_ref[...]
        for o_ref, g_ref, w_ref, dp_ref, dg_ref in ((oa_ref, ga_ref, wa_ref, dpa_ref, dga_ref),
                                                    (ob_ref, gb_ref, wb_ref, dpb_ref, dgb_ref),
                                                    (oc_ref, gc_ref, wc_ref, dpc_ref, dgc_ref)):
            s = jax.nn.sigmoid(g_ref[...])
            proj = jnp.dot(o_ref[...], w_ref[...], preferred_element_type=F32)
            dp_ref[...] = (dyv * s).astype(BF16)
            dg_ref[...] = (dyv * proj * (s * (1.0 - s))).astype(BF16)

    return pl.pallas_call(
        body, name=name, grid=(T // TR, D // tc), in_specs=[out] + _merge_specs(z, D, TR, tc, wa, wb, wc),
        out_specs=[out] * 6, out_shape=[_sds((T, D), BF16)] * 6, compiler_params=_params(("parallel", "parallel")),
    )(dy, oa, ob, oc, z, z, z, wa.arr, wb.arr, wc.arr)


def _silu_rows(cc, name):
    def body(c_ref, s_ref):
        v = c_ref[...]
        s_ref[...] = (v * jax.nn.sigmoid(v)).astype(BF16)
    return pl.pallas_call(body, name=name, out_shape=_sds(cc.shape, BF16))(cc)


def _ada_bwd_rows(dm, ds, cc, name):
    def body(dm_ref, ds_ref, c_ref, db_ref, dc_ref):
        db_ref[...] = jnp.sum(dm_ref[...], axis=0, keepdims=True)
        v = c_ref[...]
        s = jax.nn.sigmoid(v)
        dc_ref[...] = ds_ref[...] * (s * (1.0 + v * (1.0 - s)))
    return pl.pallas_call(body, name=name, out_shape=[_sds((1, dm.shape[1]), F32), _sds(cc.shape, F32)])(dm, ds, cc)


def _row_tile(rows, cols):
    for t in (512, 256, 128, 64, 32, 16, 8):
        if rows % t == 0 and t * cols * 4 <= (1 << 20):
            return t
    return rows


def _add_own_layer(layers, landed, core, name):
    R, C = landed.shape
    tr = _row_tile(R, C)
    n_layers = len(layers)

    def body(c_ref, *refs):
        b_ref, o_ref, o16_ref = refs[n_layers:]
        for l in range(n_layers):
            @pl.when(c_ref[0] == l)
            def _(a_ref=refs[l]):
                tot = a_ref[...] + b_ref[...].astype(F32)
                o_ref[...] = tot
                o16_ref[...] = tot.astype(BF16)

    row = pl.BlockSpec((tr, C), lambda i, c: (i, 0))
    own = [pl.BlockSpec((tr, C), functools.partial(lambda i, c, l: (jnp.where(c[0] == l, i, 0), 0), l=l))
           for l in range(n_layers)]
    grid_spec = pltpu.PrefetchScalarGridSpec(num_scalar_prefetch=1, grid=(R // tr,),
                                             in_specs=own + [row], out_specs=[row, row])
    return pl.pallas_call(body, name=name, grid_spec=grid_spec, out_shape=[_sds((R, C), F32), _sds((R, C), BF16)],
                          compiler_params=_params(("arbitrary",)))(core, *layers, landed)


def _sum_chips(own, landed, chip, name):
    _, R, C = own.shape
    tr = _row_tile(R, C)

    def body(k_ref, a_ref, b_ref, o_ref):
        o_ref[...] = ((a_ref[...] + b_ref[0].astype(F32)) + b_ref[1].astype(F32)) + b_ref[2].astype(F32)

    grid_spec = pltpu.PrefetchScalarGridSpec(
        num_scalar_prefetch=1, grid=(R // tr,),
        in_specs=[pl.BlockSpec((None, tr, C), lambda i, k: (k[0], i, 0)), pl.BlockSpec((3, tr, C), lambda i, k: (0, i, 0))],
        out_specs=pl.BlockSpec((tr, C), lambda i, k: (i, 0)))
    return pl.pallas_call(body, name=name, grid_spec=grid_spec, out_shape=_sds((R, C), F32),
                          compiler_params=_params(("parallel",)))(chip, own, landed)


def _adam_math(w, g, m, v):
    m = ADAM_B1 * m + (1.0 - ADAM_B1) * g
    v = ADAM_B2 * v + (1.0 - ADAM_B2) * (g * g)
    m_hat = m / (1.0 - ADAM_B1 ** ADAM_STEP)
    v_hat = v / (1.0 - ADAM_B2 ** ADAM_STEP)
    delta = -ADAM_LR * (m_hat / (jnp.sqrt(v_hat) + ADAM_EPS) + ADAM_WD * w)
    return delta, m, v


def _adamw(w, mine, other, m, v, core, name):
    L, R, C = w.shape
    tr = _row_tile(R, C)

    def body(c_ref, w_ref, a_ref, b_ref, m_ref, v_ref, g_ref, d_ref, mo_ref, vo_ref):
        def step(g):
            d, mn, vn = _adam_math(w_ref[...], g, m_ref[...], v_ref[...])
            g_ref[...] = g
            d_ref[...] = d
            mo_ref[...] = mn
            vo_ref[...] = vn

        @pl.when(pl.program_id(0) == c_ref[0])
        def _():
            step(a_ref[...])

        @pl.when(pl.program_id(0) != c_ref[0])
        def _():
            step(b_ref[...])

    lay = pl.BlockSpec((None, tr, C), lambda l, i, c: (l, i, 0))
    row = pl.BlockSpec((tr, C), lambda l, i, c: (i, 0))
    grid_spec = pltpu.PrefetchScalarGridSpec(num_scalar_prefetch=1, grid=(L, R // tr),
                                             in_specs=[lay, row, row, lay, lay], out_specs=[lay] * 4)
    return pl.pallas_call(body, name=name, grid_spec=grid_spec, out_shape=[_sds((L, R, C), F32)] * 4,
                          compiler_params=_params(("parallel", "parallel")))(core, w, mine, other, m, v)


def _adamw_small(w, parts, m, v, name):
    R, C = w.shape

    def body(w_ref, p_ref, m_ref, v_ref, g_ref, d_ref, mo_ref, vo_ref):
        g = p_ref[0]
        for dev in range(1, 8):
            g = g + p_ref[dev]
        d, mn, vn = _adam_math(w_ref[...], g, m_ref[...], v_ref[...])
        g_ref[...] = g
        d_ref[...] = d
        mo_ref[...] = mn
        vo_ref[...] = vn

    return pl.pallas_call(body, name=name, out_shape=[_sds((R, C), F32)] * 4)(w, parts, m, v)


def _place():
    return lax.axis_index("x"), lax.axis_index("y"), lax.axis_index("c")


def _other_chips(x, y):
    return [(1 - x, y), (x, 1 - y), (1 - x, 1 - y)]


def _rcopy(src, dst, ssem, rsem, dev):
    return pltpu.make_async_remote_copy(src_ref=src, dst_ref=dst, send_sem=ssem, recv_sem=rsem,
                                        device_id=dev, device_id_type=MESH)


def _gather_weights(shards, name):
    n = len(shards)
    own_sem = 6

    def body(*refs):
        src, out = refs[:n], refs[n:2 * n]
        send_sems, recv_sems = refs[2 * n:]
        x, y, c = _place()
        sibling = (x, y, 1 - c)
        chips = _other_chips(x, y)
        mine = 2 * x + y
        own = [_rcopy(src[w], out[w].at[mine], send_sems.at[w, own_sem], recv_sems.at[w, own_sem], sibling)
               for w in range(n)]
        for cp in own:
            cp.start()
        first = [_rcopy(src[w].at[c], out[w].at[mine, c], send_sems.at[w, j], recv_sems.at[w, j], (*chip, c))
                 for w in range(n) for j, chip in enumerate(chips)]
        for cp in first:
            cp.start()
        passed = []
        for w in range(n):
            for j, (px, py) in enumerate(chips):
                landed = out[w].at[2 * px + py, c]
                _rcopy(landed, landed, send_sems.at[w, j], recv_sems.at[w, j], (px, py, c)).wait_recv()
                cp = _rcopy(landed, landed, send_sems.at[w, 3 + j], recv_sems.at[w, 3 + j], sibling)
                cp.start()
                passed.append(cp)
        for w in range(n):
            for j, (px, py) in enumerate(chips):
                landed = out[w].at[2 * px + py, 1 - c]
                _rcopy(landed, landed, send_sems.at[w, 3 + j], recv_sems.at[w, 3 + j], sibling).wait_recv()
        for cp in own:
            cp.wait_recv()
        for cp in first + passed + own:
            cp.wait_send()

    return pl.pallas_call(
        body, name=name, in_specs=[ANY] * n, out_specs=[ANY] * n,
        out_shape=[_sds((N_CHIPS,) + s.shape, s.dtype) for s in shards],
        scratch_shapes=[pltpu.SemaphoreType.DMA((n, 7)), pltpu.SemaphoreType.DMA((n, 7))],
    )(*shards)


def _send_other_layer(layer0, layer1, name):
    n = len(layer0)

    def body(*refs):
        src0, src1, out = refs[:n], refs[n:2 * n], refs[2 * n:3 * n]
        send_sems, recv_sems = refs[3 * n:]
        x, y, c = _place()

        def copies(src):
            return [_rcopy(src[w], out[w], send_sems.at[w], recv_sems.at[w], (x, y, 1 - c)) for w in range(n)]

        @pl.when(c == 0)
        def _():
            for cp in copies(src1):
                cp.start()

        @pl.when(c == 1)
        def _():
            for cp in copies(src0):
                cp.start()

        for cp in copies(src0):
            cp.wait_recv()
        for cp in copies(src0):
            cp.wait_send()

    return pl.pallas_call(
        body, name=name, in_specs=[ANY] * (2 * n), out_specs=[ANY] * n,
        out_shape=[_sds(s.shape, s.dtype) for s in layer0],
        scratch_shapes=[pltpu.SemaphoreType.DMA((n,)), pltpu.SemaphoreType.DMA((n,))],
    )(*layer0, *layer1)


def _send_chip_blocks(blocked, name):
    n = len(blocked)

    def body(*refs):
        src, out = refs[:n], refs[n:2 * n]
        send_sems, recv_sems = refs[2 * n:]
        x, y, c = _place()
        cps = [_rcopy(src[w].at[2 * px + py], out[w].at[j], send_sems.at[w, j], recv_sems.at[w, j], (px, py, c))
               for w in range(n) for j, (px, py) in enumerate(_other_chips(x, y))]
        for cp in cps:
            cp.start()
        for cp in cps:
            cp.wait_recv()
        for cp in cps:
            cp.wait_send()

    return pl.pallas_call(
        body, name=name, in_specs=[ANY] * n, out_specs=[ANY] * n,
        out_shape=[_sds((3,) + s.shape[1:], s.dtype) for s in blocked],
        scratch_shapes=[pltpu.SemaphoreType.DMA((n, 3)), pltpu.SemaphoreType.DMA((n, 3))],
    )(*blocked)


def _share_layers(reduced, name):
    n = len(reduced)

    def body(*refs):
        src, out = refs[:n], refs[n:2 * n]
        send_sems, recv_sems = refs[2 * n:]
        x, y, c = _place()
        cps = [_rcopy(src[w], out[w], send_sems.at[w], recv_sems.at[w], (x, y, 1 - c)) for w in range(n)]
        for cp in cps:
            cp.start()
        for cp in cps:
            cp.wait_recv()
        for cp in cps:
            cp.wait_send()

    return pl.pallas_call(
        body, name=name, in_specs=[ANY] * n, out_specs=[ANY] * n,
        out_shape=[_sds(s.shape, s.dtype) for s in reduced],
        scratch_shapes=[pltpu.SemaphoreType.DMA((n,)), pltpu.SemaphoreType.DMA((n,))],
    )(*reduced)


def _gather_small(block, name):
    m_per, n = block.shape

    def body(x_ref, out_ref, send_sems, recv_sems, local_sem):
        x, y, c = _place()
        me, sibling = (x, y, c), (x, y, 1 - c)
        chips = _other_chips(x, y)

        def rows(px, py, pc):
            return out_ref.at[pl.ds((4 * px + 2 * py + pc) * m_per, m_per), :]

        def copy(k, blk, to, src=None):
            return _rcopy(rows(*blk) if src is None else src, rows(*blk), send_sems.at[k], recv_sems.at[k], to)

        mine = pltpu.make_async_copy(x_ref, rows(*me), local_sem)
        mine.start()
        first = [copy(0, me, sibling, src=x_ref)]
        first += [copy(1 + j, me, (*chip, c), src=x_ref) for j, chip in enumerate(chips)]
        for cp in first:
            cp.start()
        passed = [copy(4 + j, (*chip, c), sibling) for j, chip in enumerate(chips)]
        for j, chip in enumerate(chips):
            copy(1 + j, (*chip, c), me).wait_recv()
            passed[j].start()
        copy(0, sibling, me).wait_recv()
        for j, chip in enumerate(chips):
            copy(4 + j, (*chip, 1 - c), me).wait_recv()
        for cp in first + passed:
            cp.wait_send()
        mine.wait()

    return pl.pallas_call(
        body, name=name, out_shape=_sds((8 * m_per, n), block.dtype),
        in_specs=[pl.BlockSpec(memory_space=pltpu.VMEM)], out_specs=pl.BlockSpec(memory_space=pltpu.VMEM),
        scratch_shapes=[pltpu.SemaphoreType.DMA((7,)), pltpu.SemaphoreType.DMA((7,)), pltpu.SemaphoreType.DMA],
    )(block)


def _rope_tables(n_ctx, seq):
    rows = seq // GRID_W
    r = jnp.repeat(jnp.arange(rows, dtype=F32), GRID_W)
    col = jnp.tile(jnp.arange(GRID_W, dtype=F32), rows)
    inv = 1.0 / (ROPE_THETA ** (jnp.arange(0, AXIS_DIM, 2, dtype=F32) / AXIS_DIM))
    ang = jnp.concatenate([r[:, None] * inv, col[:, None] * inv], axis=-1)
    cos = jnp.repeat(jnp.cos(ang), 2, axis=-1)
    sin = jnp.repeat(jnp.sin(ang), 2, axis=-1) * jnp.tile(jnp.array([-1.0, 1.0], F32), HEAD_DIM // 2)
    cos = jnp.concatenate([jnp.ones((n_ctx, HEAD_DIM), F32), cos], axis=0)
    sin = jnp.concatenate([jnp.zeros((n_ctx, HEAD_DIM), F32), sin], axis=0)
    return jnp.tile(cos, (1, 2)), jnp.tile(sin, (1, 2))


def _block_diag(w_pool):
    L, G = w_pool.shape[:2]
    eye = jnp.eye(G, dtype=w_pool.dtype)
    return (w_pool[:, :, :, None, :] * eye[None, :, None, :, None]).reshape(L, POOL_WIDTH, POOL_WIDTH)


def _qk_gains(small):
    qn = jnp.stack([small["q_norm_a"], small["q_norm_c"]], axis=1)[:, :, None, :]
    kn = jnp.stack([small["k_norm_a"], small["k_norm_c"]], axis=1)[:, :, None, :]
    L = qn.shape[0]
    rows = jnp.concatenate([jnp.broadcast_to(qn, (L, 2, N_HEADS, HEAD_DIM)), jnp.broadcast_to(kn, (L, 2, N_KV, HEAD_DIM)),
                            jnp.ones((L, 2, N_KV, HEAD_DIM), F32)], axis=2)
    return rows.reshape(L, 2, 1, QKV_WIDTH)


def _local_step(x, c, ctx, c_ctx, small, gw, target):
    B, S, D = x.shape
    N = ctx.shape[1]
    L = small["norm1"].shape[0]
    Tp = N + S
    T = B * Tp
    TR = N
    P = Tp // N
    rows16 = 16
    assert N % Q_BLOCK == 0 and S % N == 0 and B + 1 <= rows16
    TM = _tile(T, (1024, 768, 512, 384, 256, 128))
    TMG = _tile(T, (512, 384, 256, 128))

    X = jnp.concatenate([ctx, x], axis=1).reshape(T, D)
    cc = jnp.concatenate([c, c_ctx[None], jnp.zeros((rows16 - B - 1, D), F32)], axis=0)
    s_rows = _silu_rows(cc, "silu_rows")
    cos, sin = _rope_tables(N, S)
    all_gains = _qk_gains(small)
    all_w_bd = _block_diag(small["w_pool"]).astype(BF16)

    def weights(l):
        return dict(
            ada=_Opnd(gw["w_ada"], "bcols", l), w_in=_Opnd(gw["w_in"], "bcols", l),
            a=_Opnd(gw["w_br_a"], "bcols", l), b=_Opnd(gw["w_br_b"], "bcols", l), c=_Opnd(gw["w_br_c"], "bcols", l),
            out=_Opnd(gw["w_out"], "brows", l), mlp1=_Opnd(gw["w_mlp1"], "bcols", l), mlp2=_Opnd(gw["w_mlp2"], "brows", l))

    IN = weights(0)["w_in"].shape[1]
    DFF = weights(0)["mlp1"].shape[1]
    tn_in = _tile(IN // N_CHIPS, (1152, 768, 512, 384, 256, 128))
    tn_ff = _tile(DFF // N_CHIPS, (1024, 512, 256, 128))
    tn_ada = _tile(6 * D // N_CHIPS, (1536, 768, 512, 256, 128))
    tn_d = D // N_CHIPS
    tk_d = _tile(D, (512,))
    tk_tok = _tile(T, (2304, 1536, 1024, 768, 512, 384, 256))

    saved = []
    xin, pending = X, None
    for l in range(L):
        W = weights(l)
        b_ada = small["b_ada"][l].reshape(1, 6 * D)
        mod = _matmul(s_rows, W["ada"], "nn", tm=rows16, tn=tn_ada, tk=D, name=f"ada_fwd{l}",
                      epilogue=lambda acc, b: (acc + b,), extras=[(b_ada, (1, tn_ada), lambda m, n: (0, n))])
        modtab = jnp.stack([jnp.broadcast_to(mod[B], (B, 6 * D)), mod[:B]], axis=1).reshape(2 * B, 1, 6 * D)
        gains = all_gains[l]
        w_bd = all_w_bd[l]
        p_scale = small["pool_scale"][l].reshape(1, POOL_WIDTH)
        sink = small["sink_c"][l]

        x0, h1 = _res_norm(xin, pending, modtab, 0, 1, small["norm1"][l][None], TR=TR, P=P, name=f"norm1_fwd{l}")
        z = _matmul(h1, W["w_in"], "nn", tm=TM, tn=tn_in, tk=D, name=f"in_proj{l}")
        q2, k2, v2 = _qk_prep(z, gains, cos, sin, TR=TR, P=P, name=f"qk_prep{l}")
        oa, oa32, lse_a = _attn_fwd(q2, k2, v2, None, branch=0, B=B, n_ctx=N, window=False, name=f"attn_a_fwd{l}")
        oc, oc32, lse_c = _attn_fwd(q2, k2, v2, sink, branch=1, B=B, n_ctx=N, window=True, name=f"attn_c_fwd{l}")
        pooled, ob = _pool_fwd(z, w_bd, p_scale, B=B, Tp=Tp, n_ctx=N, name=f"pool_fwd{l}")
        y = _merge_fwd(oa, ob, oc, z, W["a"], W["b"], W["c"], D=D, TR=TMG, name=f"merge_fwd{l}")
        ao = _matmul(y, W["out"], "nn", tm=TM, tn=D, tk=tn_d, name=f"out_proj{l}")
        x1, h2 = _res_norm(x0, (ao, modtab, 2), modtab, 3, 4, small["norm2"][l][None], TR=TR, P=P, name=f"norm2_fwd{l}")
        a_pre, r_act = _matmul(h2, W["mlp1"], "nn", tm=TM, tn=tn_ff, tk=D, name=f"mlp1_fwd{l}", out_dtypes=(F32, BF16),
                               epilogue=lambda acc: (acc, jnp.square(jnp.maximum(acc, 0.0))))
        mo = _matmul(r_act, W["mlp2"], "nn", tm=TM, tn=D, tk=tn_ff, name=f"mlp2_fwd{l}")
        saved.append(dict(modtab=modtab, gains=gains, w_bd=w_bd, p_scale=p_scale, sink=sink, x0=x0, h1=h1, z=z,
                          q2=q2, k2=k2, v2=v2, oa=oa, ob=ob, oc=oc, oa32=oa32, oc32=oc32, lse_a=lse_a, lse_c=lse_c,
                          pooled=pooled, y=y, ao=ao,
                          x1=x1, h2=h2, a_pre=a_pre, r_act=r_act, mo=mo))
        xin, pending = x1, (mo, modtab, 5)

    dxo, loss = _loss_head(xin, pending[0], pending[1], 5, target.reshape(B * S, D), TR=TR, P=P, name="loss_head")

    big = {k: [None] * L for k in gw}
    big16 = {k: [None] * L for k in gw}
    sm = {k: [None] * L for k in ("b_ada", "norm1", "norm2", "q_norm_a", "k_norm_a", "q_norm_c", "k_norm_c",
                                   "sink_c", "w_pool", "pool_scale")}

    def dw(key, l, a, b, *, tm, tn, name, tk=tk_tok, blocked=True):
        outs = _matmul(a, b, "tn", tm=tm, tn=tn, tk=tk, name=name, out_dtypes=(F32, BF16), out_blocked=blocked)
        if not blocked:
            outs = [o.reshape(N_CHIPS, o.shape[0] // N_CHIPS, o.shape[1]) for o in outs]
        big[key][l], big16[key][l] = outs
    d_cctx = jnp.zeros((D,), F32)
    for l in reversed(range(L)):
        W, sv = weights(l), saved[l]
        modtab = sv["modtab"]
        d_mo, dg2 = _gate_bwd(dxo, sv["mo"], modtab, 5, TR=TR, P=P, name=f"gate2_bwd{l}")
        d_a = _matmul(d_mo, W["mlp2"], "nt", tm=TM, tn=tn_ff, tk=D, name=f"mlp2_bwd{l}", out_dtypes=(BF16,),
                      epilogue=lambda acc, a: (acc * (2.0 * jnp.maximum(a, 0.0)),),
                      extras=[(sv["a_pre"], (TM, tn_ff), lambda m, n: (m, n))])
        dw("w_mlp2", l, sv["r_act"], d_mo, tm=tk_d, tn=D, name=f"mlp2_dw{l}", blocked=False)
        d_h2 = _matmul(d_a, W["mlp1"], "nt", tm=TM, tn=D, tk=tn_ff, name=f"mlp1_bwd{l}")
        dw("w_mlp1", l, sv["h2"], d_a, tm=tk_d, tn=tn_ff, name=f"mlp1_dw{l}")
        dx1, dsh2, dsc2, dn2 = _norm_bwd(sv["x1"], d_h2, dxo, modtab, 4, small["norm2"][l][None], TR=TR, P=P,
                                         name=f"norm2_bwd{l}")
        d_ao, dg1 = _gate_bwd(dx1, sv["ao"], modtab, 2, TR=TR, P=P, name=f"gate1_bwd{l}")
        d_y = _matmul(d_ao, W["out"], "nt", tm=TM, tn=tn_d, tk=D, name=f"out_bwd{l}")
        dw("w_out", l, sv["y"], d_ao, tm=tk_d, tn=D, name=f"out_dw{l}", blocked=False)
        d_pa, d_pb, d_pc, d_ga, d_gb, d_gc = _merge_bwd(d_y, sv["oa"], sv["ob"], sv["oc"], sv["z"], W["a"], W["b"], W["c"],
                                                        D=D, TR=TMG, name=f"merge_bwd{l}")
        d_oa = _matmul(d_pa, W["a"], "nt", tm=TM, tn=Q_WIDTH, tk=tn_d, name=f"br_a_bwd{l}", out_dtypes=(BF16,))
        d_ob = _matmul(d_pb, W["b"], "nt", tm=TM, tn=POOL_WIDTH, tk=tn_d, name=f"br_b_bwd{l}")
        d_oc = _matmul(d_pc, W["c"], "nt", tm=TM, tn=Q_WIDTH, tk=tn_d, name=f"br_c_bwd{l}", out_dtypes=(BF16,))
        dw("w_br_a", l, sv["oa"], d_pa, tm=Q_WIDTH, tn=tn_d, name=f"br_a_dw{l}")
        dw("w_br_b", l, sv["ob"], d_pb, tm=POOL_WIDTH, tn=tn_d, name=f"br_b_dw{l}")
        dw("w_br_c", l, sv["oc"], d_pc, tm=Q_WIDTH, tn=tn_d, name=f"br_c_dw{l}")
        d_u, d_wbd, d_ps = _pool_bwd(d_ob, sv["pooled"], sv["w_bd"], sv["p_scale"], B=B, Tp=Tp, n_ctx=N, name=f"pool_bwd{l}")
        dqa, dka, dva = _attn_bwd(sv["q2"], sv["k2"], sv["v2"], d_oa, sv["oa32"], sv["lse_a"], None, branch=0, B=B,
                                  n_ctx=N, window=False, name=f"attn_a_bwd{l}")
        dqc, dkc, dvc, dsink = _attn_bwd(sv["q2"], sv["k2"], sv["v2"], d_oc, sv["oc32"], sv["lse_c"], sv["sink"],
                                         branch=1, B=B, n_ctx=N, window=True, name=f"attn_c_bwd{l}")
        dz_a, dgains_a = _qk_prep_bwd(sv["z"], dqa, dka, dva, sv["gains"], cos, sin, branch=0, TR=TR, P=P,
                                      name=f"qk_prep_a_bwd{l}")
        dz_c, dgains_c = _qk_prep_bwd(sv["z"], dqc, dkc, dvc, sv["gains"], cos, sin, branch=1, TR=TR, P=P,
                                      name=f"qk_prep_c_bwd{l}")
        dz = jnp.concatenate([dz_a, dz_c, d_u, d_ga, d_gb, d_gc], axis=1)
        d_h1 = _matmul(dz, W["w_in"], "nt", tm=TM, tn=D, tk=tn_in, name=f"in_bwd{l}")
        dw("w_in", l, sv["h1"], dz, tm=tk_d, tn=tn_in, name=f"in_dw{l}")
        dx0, dsh1, dsc1, dn1 = _norm_bwd(sv["x0"], d_h1, dx1, modtab, 1, small["norm1"][l][None], TR=TR, P=P,
                                         name=f"norm1_bwd{l}")

        dm_groups = jnp.concatenate([dsh1, dsc1, dg1, dsh2, dsc2, dg2], axis=-1).reshape(B, 2, 6 * D)
        dm = jnp.concatenate([dm_groups[:, 1], jnp.sum(dm_groups[:, 0], axis=0, keepdims=True),
                              jnp.zeros((rows16 - B - 1, 6 * D), F32)], axis=0)
        dm_bf = dm.astype(BF16)
        d_s = _matmul(dm_bf, W["ada"], "nt", tm=rows16, tn=D, tk=tn_ada, name=f"ada_bwd{l}")
        dw("w_ada", l, s_rows, dm_bf, tm=tk_d, tn=tn_ada, tk=rows16, name=f"ada_dw{l}")
        db_ada, dcc = _ada_bwd_rows(dm, d_s, cc, f"ada_rows_bwd{l}")
        d_cctx = d_cctx + dcc[B]

        sm["b_ada"][l] = db_ada[0]
        sm["norm1"][l] = jnp.sum(dn1, axis=(0, 1))
        sm["norm2"][l] = jnp.sum(dn2, axis=(0, 1))
        dgh = jnp.stack([dgains_a, dgains_c]).reshape(2, QKV_WIDTH // HEAD_DIM, HEAD_DIM)
        sm["q_norm_a"][l] = jnp.sum(dgh[0, :N_HEADS], axis=0)
        sm["k_norm_a"][l] = jnp.sum(dgh[0, N_HEADS:N_HEADS + N_KV], axis=0)
        sm["q_norm_c"][l] = jnp.sum(dgh[1, :N_HEADS], axis=0)
        sm["k_norm_c"][l] = jnp.sum(dgh[1, N_HEADS:N_HEADS + N_KV], axis=0)
        sm["sink_c"][l] = jnp.sum(dsink[:, :N_HEADS, 0], axis=0)
        sm["w_pool"][l] = jnp.stack([d_wbd[g * POOL_CH:(g + 1) * POOL_CH, g * POOL_CH:(g + 1) * POOL_CH]
                                     for g in range(POOL_WIDTH // POOL_CH)])
        sm["pool_scale"][l] = d_ps[0]
        dxo = dx0

    grad_x = dxo.reshape(B, Tp, D)[:, N:]
    small_grads = {k: jnp.stack(v) for k, v in sm.items()}
    small_grads["c_ctx"] = d_cctx
    return loss, grad_x, small_grads, big, big16


SMALL_NAMES = ("c_ctx", "b_ada", "norm1", "norm2", "q_norm_a", "k_norm_a", "q_norm_c", "k_norm_c", "sink_c",
               "w_pool", "pool_scale")
BIG_NAMES = ("w_ada", "w_in", "w_br_a", "w_br_b", "w_br_c", "w_out", "w_mlp1", "w_mlp2")
WEIGHT_NAMES = ("c_ctx", "w_ada", "b_ada", "norm1", "norm2", "w_in", "q_norm_a", "k_norm_a", "q_norm_c", "k_norm_c",
                "sink_c", "w_pool", "pool_scale", "w_br_a", "w_br_b", "w_br_c", "w_out", "w_mlp1", "w_mlp2")


def _pack(parts, rows):
    flat = jnp.concatenate([p.reshape(-1).astype(F32) for p in parts])
    return jnp.pad(flat, (0, rows * LANES - flat.shape[0])).reshape(rows, LANES)


def _unpack(packed, like):
    flat, out, at = packed.reshape(-1), [], 0
    for p in like:
        out.append(flat[at:at + p.size].reshape(p.shape))
        at += p.size
    return out


def _reduce_big(partials, partials16):
    names = list(partials)
    assert all(len(partials[k]) == 2 for k in names)
    x, y, c = _place()
    core = c.astype(jnp.int32).reshape(1)
    chip = (2 * x + y).astype(jnp.int32).reshape(1)
    shapes = [partials[k][0].shape for k in names]
    flat = [[g.reshape(-1, g.shape[-1]) for g in partials[k]] for k in names]
    flat16 = [[g.reshape(-1, g.shape[-1]) for g in partials16[k]] for k in names]
    landed = _send_other_layer([f[0] for f in flat16], [f[1] for f in flat16], "grads_to_sibling")
    in_chip = [_add_own_layer(f, r, core, f"grads_add_sibling_{k}") for k, f, r in zip(names, flat, landed)]
    blocked = [h.reshape(s) for s, (h, _) in zip(shapes, in_chip)]
    blocked16 = [h.reshape(s) for s, (_, h) in zip(shapes, in_chip)]
    from_chips = _send_chip_blocks(blocked16, "grads_to_chips")
    reduced = [_sum_chips(h, r, chip, f"grads_sum_chips_{k}") for k, h, r in zip(names, blocked, from_chips)]
    shared = _share_layers(reduced, "grads_share_layers")
    return core, dict(zip(names, zip(reduced, shared)))


def kernel(x, c, ctx, c_ctx, w_ada, b_ada, norm1, norm2, w_in, q_norm_a, k_norm_a, q_norm_c, k_norm_c, sink_c, w_pool, pool_scale, w_br_a, w_br_b, w_br_c, w_out, w_mlp1, w_mlp2, loss_target, m_c_ctx, m_w_ada, m_b_ada, m_norm1, m_norm2, m_w_in, m_q_norm_a, m_k_norm_a, m_q_norm_c, m_k_norm_c, m_sink_c, m_w_pool, m_pool_scale, m_w_br_a, m_w_br_b, m_w_br_c, m_w_out, m_w_mlp1, m_w_mlp2, v_c_ctx, v_w_ada, v_b_ada, v_norm1, v_norm2, v_w_in, v_q_norm_a, v_k_norm_a, v_q_norm_c, v_k_norm_c, v_sink_c, v_w_pool, v_pool_scale, v_w_br_a, v_w_br_b, v_w_br_c, v_w_out, v_w_mlp1, v_w_mlp2):
    given = dict(locals())
    w = {k: given[k] for k in WEIGHT_NAMES}
    m = {k: given["m_" + k] for k in WEIGHT_NAMES}
    v = {k: given["v_" + k] for k in WEIGHT_NAMES}

    gathered = _gather_weights([w[k].astype(BF16) for k in BIG_NAMES], "gather_weights")
    gw = dict(zip(BIG_NAMES, gathered))
    small = {k: w[k] for k in SMALL_NAMES}
    loss_part, grad_x, small_grads, big_grads, big_grads16 = _local_step(x, c, ctx, c_ctx, small, gw, loss_target)

    core, reduced = _reduce_big({k: big_grads[k] for k in BIG_NAMES}, {k: big_grads16[k] for k in BIG_NAMES})
    grads, deltas, new_m, new_v = {}, {}, {}, {}
    for k in BIG_NAMES:
        mine, other = reduced[k]
        grads[k], deltas[k], new_m[k], new_v[k] = _adamw(w[k], mine, other, m[k], v[k], core, f"adamw_{k}")

    sizes = sum(w[k].size for k in SMALL_NAMES) + LANES
    rows = -(-sizes // (8 * LANES)) * 8
    parts = _gather_small(_pack([small_grads[k] for k in SMALL_NAMES] + [loss_part[0]], rows), "gather_small")
    zero = jnp.zeros((LANES,), F32)
    packed = [_pack([t[k] for k in SMALL_NAMES] + [zero], rows) for t in (w, m, v)]
    outs = _adamw_small(packed[0], parts.reshape(8, rows, LANES), packed[1], packed[2], "adamw_small")
    like = [w[k] for k in SMALL_NAMES] + [zero]
    for store, packed_out in zip((grads, deltas, new_m, new_v), outs):
        pieces = _unpack(packed_out, like)
        for k, piece in zip(SMALL_NAMES, pieces):
            store[k] = piece
        if store is grads:
            loss = pieces[-1][0]

    return (loss, grad_x, *[grads[k] for k in WEIGHT_NAMES], *[deltas[k] for k in WEIGHT_NAMES],
            *[new_m[k] for k in WEIGHT_NAMES], *[new_v[k] for k in WEIGHT_NAMES])
```

```python
import functools

import jax
import jax.numpy as jnp
from jax import lax
from jax.experimental import pallas as pl
from jax.experimental.pallas import tpu as pltpu

F32 = jnp.float32
BF16 = jnp.bfloat16

HEAD_DIM = 64
GRID_W = 64
AXIS_DIM = HEAD_DIM // 2
ROPE_THETA = 10000.0
N_HEADS = 6
N_KV = 2
N_GROUP = N_HEADS // N_KV
POOL_CH = 64
POOL_WIDTH = 256
POOL_WINDOWS = (2, 4, 8, 16)
WINDOW = 128
Q_BLOCK = 128
Q_WIDTH = N_HEADS * HEAD_DIM
KV_WIDTH = N_KV * HEAD_DIM
GATE_COL = 2 * (Q_WIDTH + 2 * KV_WIDTH) + POOL_WIDTH
U_COL = 2 * (Q_WIDTH + 2 * KV_WIDTH)
EPS = 1e-6
NEG = -1e30
ADAM_LR = 0.001
ADAM_B1 = 0.9
ADAM_B2 = 0.999
ADAM_EPS = 1e-08
ADAM_WD = 0.01
ADAM_STEP = 10

N_CHIPS = 4
LANES = 128
POOL_PAD = 16
VMEM_LIMIT = 48 * 1024 * 1024
MESH = pl.DeviceIdType.MESH
ANY = pl.BlockSpec(memory_space=pl.ANY)


def _params(sem):
    return pltpu.CompilerParams(dimension_semantics=sem, vmem_limit_bytes=VMEM_LIMIT)


def _sds(shape, dtype):
    return jax.ShapeDtypeStruct(tuple(shape), dtype)


class _Opnd:
    def __init__(self, arr, kind="plain", layer=None):
        self.arr, self.kind, self.layer = arr, kind, layer

    @property
    def shape(self):
        a = self.arr
        if self.kind == "plain":
            return a.shape
        if self.kind == "bcols":
            return (a.shape[2], N_CHIPS * a.shape[3])
        return (N_CHIPS * a.shape[2], a.shape[3])

    def spec(self, tr, tc, fn):
        a, layer = self.arr, self.layer
        if self.kind == "plain":
            return pl.BlockSpec((tr, tc), lambda *g: fn(*g))
        if self.kind == "bcols":
            assert a.shape[3] % tc == 0, (a.shape, tc)
            per = a.shape[3] // tc

            def im(*g):
                ri, ci = fn(*g)
                return (ci // per, layer, ri, ci % per)
            return pl.BlockSpec((None, None, tr, tc), im)
        assert a.shape[2] % tr == 0, (a.shape, tr)
        per = a.shape[2] // tr

        def im(*g):
            ri, ci = fn(*g)
            return (ri // per, layer, ri % per, ci)
        return pl.BlockSpec((None, None, tr, tc), im)


def _matmul(a, b, mode, *, tm, tn, tk, name, out_dtypes=(F32,), epilogue=None, extras=(), out_blocked=False):
    if not isinstance(a, _Opnd):
        a = _Opnd(a)
    if not isinstance(b, _Opnd):
        b = _Opnd(b)
    if mode == "nn":
        (M, K), (K2, N) = a.shape, b.shape
        a_spec = a.spec(tm, tk, lambda m, n, k: (m, k))
        b_spec = b.spec(tk, tn, lambda m, n, k: (k, n))
        dims = (((1,), (0,)), ((), ()))
    elif mode == "nt":
        (M, K), (N, K2) = a.shape, b.shape
        a_spec = a.spec(tm, tk, lambda m, n, k: (m, k))
        b_spec = b.spec(tn, tk, lambda m, n, k: (n, k))
        dims = (((1,), (1,)), ((), ()))
    else:
        (K, M), (K2, N) = a.shape, b.shape
        a_spec = a.spec(tk, tm, lambda m, n, k: (k, m))
        b_spec = b.spec(tk, tn, lambda m, n, k: (k, n))
        dims = (((0,), (0,)), ((), ()))
    assert K == K2 and M % tm == 0 and N % tn == 0 and K % tk == 0, (name, M, N, K, K2, tm, tn, tk)
    nk = K // tk
    n_extra = len(extras)
    n_out = len(out_dtypes)
    extra_specs = [pl.BlockSpec(bs, functools.partial(lambda m, n, k, f: f(m, n), f=f)) for (_, bs, f) in extras]
    if out_blocked:
        assert (N // N_CHIPS) % tn == 0
        per = (N // N_CHIPS) // tn
        out_shape = [_sds((N_CHIPS, M, N // N_CHIPS), dt) for dt in out_dtypes]
        out_specs = [pl.BlockSpec((None, tm, tn), lambda m, n, k: (n // per, m, n % per)) for _ in out_dtypes]
    else:
        out_shape = [_sds((M, N), dt) for dt in out_dtypes]
        out_specs = [pl.BlockSpec((tm, tn), lambda m, n, k: (m, n)) for _ in out_dtypes]

    in_place = nk > 1 and epilogue is None and out_dtypes[0] == F32

    def body(*refs):
        a_ref, b_ref = refs[0], refs[1]
        extra_refs = refs[2:2 + n_extra]
        out_refs = refs[2 + n_extra:2 + n_extra + n_out]
        acc_ref = out_refs[0] if in_place else (refs[2 + n_extra + n_out] if nk > 1 else None)
        k = pl.program_id(2)
        prod = lax.dot_general(a_ref[...].astype(BF16), b_ref[...].astype(BF16), dims, preferred_element_type=F32)

        def finish(acc):
            outs = epilogue(acc, *[r[...] for r in extra_refs]) if epilogue is not None else (acc,) * n_out
            for o_ref, o in zip(out_refs, outs):
                o_ref[...] = o.astype(o_ref.dtype)

        if nk == 1:
            finish(prod)
        elif in_place:
            @pl.when(k == 0)
            def _():
                acc_ref[...] = prod

            @pl.when(k > 0)
            def _():
                acc_ref[...] += prod

            if n_out > 1:
                @pl.when(k == nk - 1)
                def _():
                    for o_ref in out_refs[1:]:
                        o_ref[...] = acc_ref[...].astype(o_ref.dtype)
        else:
            @pl.when(k == 0)
            def _():
                acc_ref[...] = prod

            @pl.when(k > 0)
            def _():
                acc_ref[...] += prod

            @pl.when(k == nk - 1)
            def _():
                finish(acc_ref[...])

    outs = pl.pallas_call(
        body, name=name, grid=(M // tm, N // tn, nk),
        in_specs=[a_spec, b_spec] + extra_specs, out_specs=out_specs, out_shape=out_shape,
        scratch_shapes=[pltpu.VMEM((tm, tn), F32)] if nk > 1 and not in_place else [],
        compiler_params=_params(("parallel", "parallel", "arbitrary")),
    )(a.arr, b.arr, *[e[0] for e in extras])
    return outs[0] if n_out == 1 else outs


def _tile(n, cands):
    for t in cands:
        if n % t == 0:
            return t
    return n


def _grp(i, P):
    return 2 * (i // P) + jnp.minimum(i % P, 1)


def _mod_spec(D, P, part):
    return pl.BlockSpec((1, 1, D), lambda i: (_grp(i, P), 0, part))


def _res_norm(x, pending, modtab, shift_part, scale_part, gain, *, TR, P, name):
    T, D = x.shape
    row = pl.BlockSpec((TR, D), lambda i: (i, 0))
    has_branch = pending is not None
    ins, specs = [x], [row]
    if has_branch:
        branch, gate_tab, gate_part = pending
        ins += [branch, gate_tab]
        specs += [row, _mod_spec(D, P, gate_part)]
    ins += [modtab, modtab, gain]
    specs += [_mod_spec(D, P, shift_part), _mod_spec(D, P, scale_part), pl.BlockSpec((1, D), lambda i: (0, 0))]

    def body(*refs):
        if has_branch:
            x_ref, br_ref, g_ref, sh_ref, sc_ref, gn_ref, xo_ref, h_ref = refs
            xv = x_ref[...] + g_ref[0] * br_ref[...]
        else:
            x_ref, sh_ref, sc_ref, gn_ref, xo_ref, h_ref = refs
            xv = x_ref[...]
        xo_ref[...] = xv
        y = xv * lax.rsqrt(jnp.mean(xv * xv, axis=-1, keepdims=True) + EPS) * gn_ref[...]
        h_ref[...] = (y * (1.0 + sc_ref[0]) + sh_ref[0]).astype(BF16)

    return pl.pallas_call(
        body, name=name, grid=(T // TR,), in_specs=specs, out_specs=[row, row],
        out_shape=[_sds((T, D), F32), _sds((T, D), BF16)], compiler_params=_params(("parallel",)),
    )(*ins)


def _norm_bwd(x, dh, dres, modtab, scale_part, gain, below, *, TR, P, name):
    T, D = x.shape
    G = modtab.shape[0]
    row = pl.BlockSpec((TR, D), lambda i: (i, 0))
    acc = pl.BlockSpec((1, 1, D), lambda i: (_grp(i, P), 0, 0))
    has_below = below is not None

    def body(*refs):
        if has_below:
            x_ref, dh_ref, dres_ref, sc_ref, gn_ref, br_ref, g_ref, dx_ref, dsh_ref, dsc_ref, dgn_ref, db_ref, dg_ref = refs
        else:
            x_ref, dh_ref, dres_ref, sc_ref, gn_ref, dx_ref, dsh_ref, dsc_ref, dgn_ref = refs
        r = pl.program_id(0) % P
        xv, dhv, gn = x_ref[...], dh_ref[...], gn_ref[...]
        rstd = lax.rsqrt(jnp.mean(xv * xv, axis=-1, keepdims=True) + EPS)
        xhat = xv * rstd
        dn = dhv * (1.0 + sc_ref[0])
        dxhat = dn * gn
        dxv = dres_ref[...] + rstd * (dxhat - xhat * jnp.mean(dxhat * xhat, axis=-1, keepdims=True))
        dx_ref[...] = dxv
        parts = [jnp.sum(dhv, axis=0, keepdims=True), jnp.sum(dhv * (xhat * gn), axis=0, keepdims=True),
                 jnp.sum(dn * xhat, axis=0, keepdims=True)]
        outs = [dsh_ref, dsc_ref, dgn_ref]
        if has_below:
            db_ref[...] = (dxv * g_ref[0]).astype(BF16)
            parts.append(jnp.sum(dxv * br_ref[...], axis=0, keepdims=True))
            outs.append(dg_ref)

        @pl.when(r <= 1)
        def _():
            for o_ref, part in zip(outs, parts):
                o_ref[0] = part

        @pl.when(r > 1)
        def _():
            for o_ref, part in zip(outs, parts):
                o_ref[0] += part

    ins = [x, dh, dres, modtab, gain]
    in_specs = [row, row, row, _mod_spec(D, P, scale_part), pl.BlockSpec((1, D), lambda i: (0, 0))]
    out_specs, out_shape = [row, acc, acc, acc], [_sds((T, D), F32)] + [_sds((G, 1, D), F32)] * 3
    if has_below:
        branch, gate_tab, gate_part = below
        ins += [branch, gate_tab]
        in_specs += [row, _mod_spec(D, P, gate_part)]
        out_specs += [row, acc]
        out_shape += [_sds((T, D), BF16), _sds((G, 1, D), F32)]
    return pl.pallas_call(body, name=name, grid=(T // TR,), in_specs=in_specs, out_specs=out_specs, out_shape=out_shape,
                          compiler_params=_params(("arbitrary",)))(*ins)


def _loss_head(x, branch, modtab, gate_part, target, *, TR, P, name):
    T, D = x.shape
    row = pl.BlockSpec((TR, D), lambda i: (i, 0))
    tgt = pl.BlockSpec((TR, D), lambda i: ((i // P) * (P - 1) + jnp.maximum(i % P - 1, 0), 0))
    one = pl.BlockSpec((1, LANES), lambda i: (0, 0))

    G = modtab.shape[0]
    acc = pl.BlockSpec((1, 1, D), lambda i: (_grp(i, P), 0, 0))

    def body(x_ref, br_ref, g_ref, t_ref, dy_ref, loss_ref, db_ref, dg_ref):
        i = pl.program_id(0)
        r = i % P

        @pl.when(i == 0)
        def _():
            loss_ref[...] = jnp.zeros_like(loss_ref)

        @pl.when(r == 0)
        def _():
            dy_ref[...] = jnp.zeros_like(dy_ref)
            db_ref[...] = jnp.zeros_like(db_ref)
            dg_ref[...] = jnp.zeros_like(dg_ref)

        @pl.when(r > 0)
        def _():
            brv, g = br_ref[...], g_ref[0]
            err = x_ref[...] + g * brv - t_ref[...]
            dy = err / D
            dy_ref[...] = dy
            db_ref[...] = (dy * g).astype(BF16)
            part = jnp.sum(dy * brv, axis=0, keepdims=True)
            per_tok = jnp.mean(err * err, axis=-1, keepdims=True)
            loss_ref[...] += 0.5 * jnp.sum(per_tok, axis=0, keepdims=True)

            @pl.when(r == 1)
            def _():
                dg_ref[0] = part

            @pl.when(r > 1)
            def _():
                dg_ref[0] += part

    return pl.pallas_call(
        body, name=name, grid=(T // TR,), in_specs=[row, row, _mod_spec(D, P, gate_part), tgt],
        out_specs=[row, one, row, acc],
        out_shape=[_sds((T, D), F32), _sds((1, LANES), F32), _sds((T, D), BF16), _sds((G, 1, D), F32)],
        compiler_params=_params(("arbitrary",)),
    )(x, branch, modtab, target)


QKV_WIDTH = Q_WIDTH + 2 * KV_WIDTH
QK_NORMED = 4


def _seg_mean(v):
    lane = lax.broadcasted_iota(jnp.int32, v.shape, 1)
    lo = lane < HEAD_DIM
    s0 = jnp.sum(jnp.where(lo, v, 0.0), axis=-1, keepdims=True)
    s1 = jnp.sum(jnp.where(lo, 0.0, v), axis=-1, keepdims=True)
    return jnp.where(lo, s0, s1) * (1.0 / HEAD_DIM)


def _pair_swap(v):
    lane = lax.broadcasted_iota(jnp.int32, v.shape, 1)
    return jnp.where((lane & 1) == 0, pltpu.roll(v, LANES - 1, 1), pltpu.roll(v, 1, 1))


def _chunk(c):
    return slice(c * LANES, (c + 1) * LANES)


def _qk_prep(z, gains, cos, sin, *, TR, P, name):
    T = z.shape[0]

    def body(z_ref, g_ref, c_ref, s_ref, q_ref, k_ref, v_ref):
        cs, sn = c_ref[...], s_ref[...]
        for ch in range(QK_NORMED):
            xv = z_ref[:, _chunk(ch)]
            y = xv * lax.rsqrt(_seg_mean(xv * xv) + EPS) * g_ref[0, :, _chunk(ch)]
            out = (y * cs + _pair_swap(y) * sn).astype(BF16)
            if ch < QK_NORMED - 1:
                q_ref[:, _chunk(ch)] = out
            else:
                k_ref[...] = out
        v_ref[...] = z_ref[:, _chunk(QK_NORMED)].astype(BF16)

    def out(width):
        return pl.BlockSpec((None, TR, width), lambda i, j: (j, i, 0))
    return pl.pallas_call(
        body, name=name, grid=(T // TR, 2),
        in_specs=[pl.BlockSpec((TR, QKV_WIDTH), lambda i, j: (i, j)),
                  pl.BlockSpec((1, 1, QKV_WIDTH), lambda i, j: (j, 0, 0)),
                  pl.BlockSpec((TR, LANES), lambda i, j: (i % P, 0)),
                  pl.BlockSpec((TR, LANES), lambda i, j: (i % P, 0))],
        out_specs=[out(Q_WIDTH), out(KV_WIDTH), out(KV_WIDTH)],
        out_shape=[_sds((2, T, Q_WIDTH), BF16), _sds((2, T, KV_WIDTH), BF16), _sds((2, T, KV_WIDTH), BF16)],
        compiler_params=_params(("parallel", "parallel")),
    )(z, gains, cos, sin)


def _qk_prep_bwd(z, dq, dk, dv, gains, cos, sin, *, branch, TR, P, name):
    T = z.shape[0]
    nt = T // TR

    def body(z_ref, dq_ref, dk_ref, dv_ref, g_ref, c_ref, s_ref, dz_ref, dg_ref):
        i = pl.program_id(0)
        cs, sn = c_ref[...], s_ref[...]
        parts = []
        for ch in range(QK_NORMED):
            xv, g = z_ref[:, _chunk(ch)], g_ref[0, :, _chunk(ch)]
            dout = dq_ref[:, _chunk(ch)] if ch < QK_NORMED - 1 else dk_ref[...]
            dy = dout * cs + _pair_swap(dout * sn)
            rstd = lax.rsqrt(_seg_mean(xv * xv) + EPS)
            xhat = xv * rstd
            dxhat = dy * g
            dz_ref[:, _chunk(ch)] = (rstd * (dxhat - xhat * _seg_mean(dxhat * xhat))).astype(BF16)
            parts.append(jnp.sum(dy * xhat, axis=0, keepdims=True))
        dz_ref[:, _chunk(QK_NORMED)] = dv_ref[...].astype(BF16)
        parts.append(jnp.zeros((1, LANES), F32))
        part = jnp.concatenate(parts, axis=1)

        @pl.when(i == 0)
        def _():
            dg_ref[0] = part

        @pl.when(i > 0)
        def _():
            dg_ref[0] += part

    def rows(width, col=0):
        return pl.BlockSpec((TR, width), lambda i: (i, col))
    return pl.pallas_call(
        body, name=name, grid=(nt,),
        in_specs=[rows(QKV_WIDTH, branch), rows(Q_WIDTH), rows(KV_WIDTH), rows(KV_WIDTH),
                  pl.BlockSpec((1, 1, QKV_WIDTH), lambda i: (branch, 0, 0)),
                  pl.BlockSpec((TR, LANES), lambda i: (i % P, 0)),
                  pl.BlockSpec((TR, LANES), lambda i: (i % P, 0))],
        out_specs=[rows(QKV_WIDTH), pl.BlockSpec((1, 1, QKV_WIDTH), lambda i: (0, 0, 0))],
        out_shape=[_sds((T, QKV_WIDTH), BF16), _sds((1, 1, QKV_WIDTH), F32)],
        compiler_params=_params(("arbitrary",)),
    )(z, dq, dk, dv, gains, cos, sin)


NT_DIMS = (((1,), (1,)), ((), ()))
TN_DIMS = (((0,), (0,)), ((), ()))
QROWS = N_GROUP * Q_BLOCK
SCORE_SCALE = HEAD_DIM ** -0.5
BAND = Q_BLOCK + 2 * WINDOW
FWD_LATENT_CHUNK = 256
BWD_LATENT_CHUNK = 1024


def _move_head(block, half_from, half_to):
    lane = lax.broadcasted_iota(jnp.int32, block.shape, 1)
    src = block if half_from == half_to else pltpu.roll(block, HEAD_DIM, 1)
    keep = (lane < HEAD_DIM) if half_to == 0 else (lane >= HEAD_DIM)
    return jnp.where(keep, src, 0.0)


def _stack_heads(lane_block, j):
    pieces = []
    for h in range(N_GROUP * j, N_GROUP * (j + 1)):
        pieces.append(_move_head(lane_block(h // 2), h % 2, j))
    return jnp.concatenate(pieces, axis=0)


def _lane_blocks(ref):
    return lambda m: ref[:, m * LANES:(m + 1) * LANES].astype(F32)


def _unstack_heads(stacked, ref):
    heads = []
    for h in range(N_HEADS):
        j, r = h // N_GROUP, h % N_GROUP
        heads.append(_move_head(stacked[j][r * Q_BLOCK:(r + 1) * Q_BLOCK], j, h % 2))
    for m in range(N_HEADS // 2):
        ref[:, m * LANES:(m + 1) * LANES] = (heads[2 * m] + heads[2 * m + 1]).astype(ref.dtype)


def _key_chunks(i, latent, *, n_ctx, t_all, window, chunk, latent_chunk):
    ctx = [(s, chunk, False) for s in range(0, n_ctx, chunk)]
    if not latent:
        return ctx
    if not window:
        wide = latent_chunk if (t_all - n_ctx) % latent_chunk == 0 else chunk
        return ctx + [(s, wide, False) for s in range(n_ctx, t_all, wide)]
    start = pl.multiple_of(jnp.minimum((i - 1) * Q_BLOCK, t_all - BAND), Q_BLOCK)
    band_chunk = BAND if latent_chunk >= BAND else (chunk if BAND % chunk == 0 else Q_BLOCK)
    return ctx + [(start + s, band_chunk, True) for s in range(0, BAND, band_chunk)]


def _scores(q, k_ref, i, start, size, masked, *, n_ctx):
    s = lax.dot_general(q, k_ref[pl.ds(start, size), :], NT_DIMS, preferred_element_type=F32)
    if masked:
        qpos = (i * Q_BLOCK - n_ctx) + (lax.broadcasted_iota(jnp.int32, (QROWS, size), 0) & (Q_BLOCK - 1))
        kpos = (start - n_ctx) + lax.broadcasted_iota(jnp.int32, (QROWS, size), 1)
        valid = (kpos - qpos <= WINDOW) & (qpos - kpos <= WINDOW) & (kpos >= 0)
        s = jnp.where(valid, s, NEG)
    return s


def _sink_column(sink_ref, j):
    r = lax.broadcasted_iota(jnp.int32, (QROWS, 1), 0)
    s0, s1, s2 = sink_ref[j * N_GROUP], sink_ref[j * N_GROUP + 1], sink_ref[j * N_GROUP + 2]
    return jnp.where(r < Q_BLOCK, s0, jnp.where(r < 2 * Q_BLOCK, s1, s2))


def _attn_specs(Tp, branch):
    nq = Tp // Q_BLOCK
    q_in = pl.BlockSpec((None, Q_BLOCK, Q_WIDTH), lambda b, i: (branch, b * nq + i, 0))
    kv_in = pl.BlockSpec((None, Tp, KV_WIDTH), lambda b, i: (branch, b, 0))
    q_out = pl.BlockSpec((Q_BLOCK, Q_WIDTH), lambda b, i: (b * nq + i, 0))
    kv_out = pl.BlockSpec((Tp, KV_WIDTH), lambda b, i: (b, 0))
    return q_in, kv_in, q_out, kv_out


def _attn_chunk(Tp):
    return 256 if Tp % 256 == 0 else Q_BLOCK


def _attn_fwd(q, k, v, sink, *, branch, B, n_ctx, window, name):
    T = q.shape[1]
    Tp = T // B
    nq = Tp // Q_BLOCK
    has_sink = sink is not None
    q_in, kv_in, q_out, _ = _attn_specs(Tp, branch)
    lse_spec = pl.BlockSpec((None, N_KV * QROWS, 1), lambda b, i: (b * nq + i, 0, 0))

    def body(*refs):
        if has_sink:
            sink_ref, q_ref, k_ref, v_ref, o_ref, o32_ref, lse_ref = refs
        else:
            q_ref, k_ref, v_ref, o_ref, o32_ref, lse_ref = refs
        i = pl.program_id(1)

        def run(latent):
            outs = []
            for j in range(N_KV):
                qv = (_stack_heads(_lane_blocks(q_ref), j) * SCORE_SCALE).astype(BF16)
                if has_sink:
                    m, l = _sink_column(sink_ref, j), jnp.ones((QROWS, 1), F32)
                else:
                    m, l = jnp.full((QROWS, 1), NEG, F32), jnp.zeros((QROWS, 1), F32)
                acc = jnp.zeros((QROWS, LANES), F32)
                for start, size, masked in _key_chunks(i, latent, n_ctx=n_ctx, t_all=Tp, window=window,
                                                       chunk=_attn_chunk(Tp), latent_chunk=FWD_LATENT_CHUNK):
                    s = _scores(qv, k_ref, i, start, size, masked, n_ctx=n_ctx)
                    m_new = jnp.maximum(m, jnp.max(s, axis=-1, keepdims=True))
                    alpha = jnp.exp(m - m_new)
                    p = jnp.exp(s - m_new)
                    l = l * alpha + jnp.sum(p, axis=-1, keepdims=True)
                    acc = acc * alpha + jnp.dot(p.astype(BF16), v_ref[pl.ds(start, size), :], preferred_element_type=F32)
                    m = m_new
                outs.append(acc * (1.0 / l))
                lse_ref[j * QROWS:(j + 1) * QROWS, :] = m + jnp.log(l)
            _unstack_heads(outs, o_ref)
            _unstack_heads(outs, o32_ref)

        @pl.when(i < n_ctx // Q_BLOCK)
        def _():
            run(False)

        @pl.when(i >= n_ctx // Q_BLOCK)
        def _():
            run(True)

    ins, specs = [q, k, v], [q_in, kv_in, kv_in]
    if has_sink:
        ins, specs = [sink] + ins, [pl.BlockSpec(memory_space=pltpu.SMEM)] + specs
    return pl.pallas_call(
        body, name=name, grid=(B, nq), in_specs=specs, out_specs=[q_out, q_out, lse_spec],
        out_shape=[_sds((T, Q_WIDTH), BF16), _sds((T, Q_WIDTH), F32), _sds((T // Q_BLOCK, N_KV * QROWS, 1), F32)],
        compiler_params=_params(("parallel", "parallel")),
    )(*ins)


def _attn_bwd(q, k, v, do, o32, lse, sink, *, branch, B, n_ctx, window, name):
    T = q.shape[1]
    Tp = T // B
    nq = Tp // Q_BLOCK
    has_sink = sink is not None
    q_in, kv_in, q_out, kv_out = _attn_specs(Tp, branch)
    lse_spec = pl.BlockSpec((None, N_KV * QROWS, 1), lambda b, i: (b * nq + i, 0, 0))
    sink_spec = pl.BlockSpec((None, 8, LANES), lambda b, i: (b, 0, 0))

    def body(*refs):
        if has_sink:
            sink_ref, q_ref, k_ref, v_ref, do_ref, o_ref, lse_ref, dq_ref, dk_ref, dv_ref, ds_ref, dkt_ref, dvt_ref = refs
        else:
            q_ref, k_ref, v_ref, do_ref, o_ref, lse_ref, dq_ref, dk_ref, dv_ref, dkt_ref, dvt_ref = refs
        i = pl.program_id(1)

        @pl.when(i == 0)
        def _():
            dk_ref[...] = jnp.zeros_like(dk_ref)
            dv_ref[...] = jnp.zeros_like(dv_ref)
            if not window:
                dkt_ref[...] = jnp.zeros_like(dkt_ref)
                dvt_ref[...] = jnp.zeros_like(dvt_ref)
            if has_sink:
                ds_ref[...] = jnp.zeros_like(ds_ref)

        def run(latent):
            upd = jnp.zeros((8, LANES), F32)
            do_blocks, o_blocks = _lane_blocks(do_ref), _lane_blocks(o_ref)
            qvs = [(_stack_heads(_lane_blocks(q_ref), j) * SCORE_SCALE).astype(BF16) for j in range(N_KV)]
            dovs = [_stack_heads(do_blocks, j).astype(BF16) for j in range(N_KV)]
            deltas = [jnp.sum(_stack_heads(lambda m: do_blocks(m) * o_blocks(m), j), axis=-1, keepdims=True)
                      for j in range(N_KV)]
            lses = [lse_ref[j * QROWS:(j + 1) * QROWS, :] for j in range(N_KV)]
            q_all, do_all = jnp.concatenate(qvs, axis=0), jnp.concatenate(dovs, axis=0)
            q_all_t, do_all_t = q_all.T, do_all.T
            dqs = [jnp.zeros((QROWS, LANES), F32) for _ in range(N_KV)]
            for start, size, masked in _key_chunks(i, latent, n_ctx=n_ctx, t_all=Tp, window=window,
                                                   chunk=_attn_chunk(Tp), latent_chunk=BWD_LATENT_CHUNK):
                rows = pl.ds(start, size)
                ds_all, p_all = [], []
                for j in range(N_KV):
                    p = jnp.exp(_scores(qvs[j], k_ref, i, start, size, masked, n_ctx=n_ctx) - lses[j])
                    dp = lax.dot_general(dovs[j], v_ref[rows, :], NT_DIMS, preferred_element_type=F32)
                    ds = (p * (dp - deltas[j])).astype(BF16)
                    dqs[j] = dqs[j] + jnp.dot(ds, k_ref[rows, :], preferred_element_type=F32)
                    ds_all.append(ds)
                    p_all.append(p.astype(BF16))
                ds_cat, p_cat = jnp.concatenate(ds_all, axis=0), jnp.concatenate(p_all, axis=0)
                if window:
                    dk_ref[rows, :] += lax.dot_general(ds_cat, q_all, TN_DIMS, preferred_element_type=F32)
                    dv_ref[rows, :] += lax.dot_general(p_cat, do_all, TN_DIMS, preferred_element_type=F32)
                else:
                    dkt_ref[:, start:start + size] += jnp.dot(q_all_t, ds_cat, preferred_element_type=F32)
                    dvt_ref[:, start:start + size] += jnp.dot(do_all_t, p_cat, preferred_element_type=F32)
            dqs = [dq * SCORE_SCALE for dq in dqs]
            for j in range(N_KV):
                if has_sink:
                    contrib = -(jnp.exp(_sink_column(sink_ref, j) - lses[j]) * deltas[j])
                    r = lax.broadcasted_iota(jnp.int32, (QROWS, 1), 0)
                    row8 = lax.broadcasted_iota(jnp.int32, (8, LANES), 0)
                    for h in range(N_GROUP):
                        in_head = (r >= h * Q_BLOCK) & (r < (h + 1) * Q_BLOCK)
                        tot = jnp.sum(jnp.where(in_head, contrib, 0.0), axis=0, keepdims=True)
                        upd = upd + jnp.where(row8 == j * N_GROUP + h, tot, 0.0)
            _unstack_heads(dqs, dq_ref)
            if has_sink:
                ds_ref[...] += upd

        @pl.when(i < n_ctx // Q_BLOCK)
        def _():
            run(False)

        @pl.when(i >= n_ctx // Q_BLOCK)
        def _():
            run(True)

        if not window:
            @pl.when(i == nq - 1)
            def _():
                dk_ref[...] += dkt_ref[...].T
                dv_ref[...] += dvt_ref[...].T

    ins, specs = [q, k, v, do, o32, lse], [q_in, kv_in, kv_in, q_out, q_out, lse_spec]
    out_specs = [q_out, kv_out, kv_out]
    out_shape = [_sds((T, Q_WIDTH), F32), _sds((T, KV_WIDTH), F32), _sds((T, KV_WIDTH), F32)]
    if has_sink:
        ins, specs = [sink] + ins, [pl.BlockSpec(memory_space=pltpu.SMEM)] + specs
        out_specs.append(sink_spec)
        out_shape.append(_sds((B, 8, LANES), F32))
    return pl.pallas_call(
        body, name=name, grid=(B, Tp // Q_BLOCK), in_specs=specs, out_specs=out_specs, out_shape=out_shape,
        scratch_shapes=[pltpu.VMEM((KV_WIDTH, LANES if window else Tp), F32)] * 2,
        compiler_params=_params(("parallel", "arbitrary")),
    )(*ins)


def _window_sums(xp):
    n = xp.shape[0]

    def ahead(a, k):
        return pltpu.roll(a, n - k, 0)
    a2 = xp + ahead(xp, 1)
    a4 = a2 + ahead(a2, 2)
    a8 = a4 + ahead(a4, 4)
    a16 = a8 + ahead(a8, 8)
    return (a2, a4, a8, a16)


def _by_group(vals):
    lane = lax.broadcasted_iota(jnp.int32, vals[0].shape, 1)
    return jnp.where(lane < POOL_CH, vals[0], jnp.where(lane < 2 * POOL_CH, vals[1],
                     jnp.where(lane < 3 * POOL_CH, vals[2], vals[3])))


def _pool_counts(n):
    t = lax.broadcasted_iota(jnp.int32, (n, POOL_WIDTH), 0)
    cnts = [(jnp.minimum(t + w // 2, n) - jnp.maximum(t - w // 2, 0)).astype(F32) for w in POOL_WINDOWS]
    return _by_group(cnts)


def _pad_rows(x):
    zeros = jnp.zeros((POOL_PAD, x.shape[1]), x.dtype)
    return jnp.concatenate([zeros, x, zeros], axis=0)


def _pool_stream(u):
    n = u.shape[0]
    sums = _window_sums(_pad_rows(u))
    tots = [pltpu.roll(a, w // 2, 0)[POOL_PAD:POOL_PAD + n] for a, w in zip(sums, POOL_WINDOWS)]
    return _by_group(tots) / _pool_counts(n) - u


def _pool_stream_t(dp):
    n = dp.shape[0]
    sums = _window_sums(_pad_rows(dp / _pool_counts(n)))
    tots = [pltpu.roll(a, w // 2 - 1, 0)[POOL_PAD:POOL_PAD + n] if w > 2 else a[POOL_PAD:POOL_PAD + n]
            for a, w in zip(sums, POOL_WINDOWS)]
    return _by_group(tots) - dp


def _pool_fwd(z, w_bd, scale, *, B, Tp, n_ctx, name):
    T = z.shape[0]
    blk = pl.BlockSpec((Tp, POOL_WIDTH), lambda b: (b, U_COL // POOL_WIDTH))
    out = pl.BlockSpec((Tp, POOL_WIDTH), lambda b: (b, 0))

    def body(u_ref, w_ref, s_ref, p_ref, o_ref):
        for lo, hi in ((0, n_ctx), (n_ctx, Tp)):
            pooled = _pool_stream(u_ref[lo:hi, :]).astype(BF16)
            p_ref[lo:hi, :] = pooled
            mixed = jnp.dot(pooled, w_ref[...], preferred_element_type=F32)
            o_ref[lo:hi, :] = (mixed * s_ref[...]).astype(BF16)

    return pl.pallas_call(
        body, name=name, grid=(B,),
        in_specs=[blk, pl.BlockSpec((POOL_WIDTH, POOL_WIDTH), lambda b: (0, 0)), pl.BlockSpec((1, POOL_WIDTH), lambda b: (0, 0))],
        out_specs=[out, out], out_shape=[_sds((T, POOL_WIDTH), BF16)] * 2, compiler_params=_params(("parallel",)),
    )(z, w_bd, scale)


def _pool_bwd(d_ob, pooled, w_bd, scale, *, B, Tp, n_ctx, name):
    T = d_ob.shape[0]
    blk = pl.BlockSpec((Tp, POOL_WIDTH), lambda b: (b, 0))
    wsp = pl.BlockSpec((POOL_WIDTH, POOL_WIDTH), lambda b: (0, 0))
    ssp = pl.BlockSpec((1, POOL_WIDTH), lambda b: (0, 0))

    def body(d_ref, p_ref, w_ref, s_ref, du_ref, dw_ref, dsc_ref):
        @pl.when(pl.program_id(0) == 0)
        def _():
            dw_ref[...] = jnp.zeros_like(dw_ref)
            dsc_ref[...] = jnp.zeros_like(dsc_ref)

        dv, pv, wv = d_ref[...], p_ref[...], w_ref[...]
        mixed = jnp.dot(pv, wv, preferred_element_type=F32)
        dsc_ref[...] += jnp.sum(dv * mixed, axis=0, keepdims=True)
        dmixed = (dv * s_ref[...]).astype(BF16)
        dw_ref[...] += lax.dot_general(pv, dmixed, TN_DIMS, preferred_element_type=F32)
        dpooled = lax.dot_general(dmixed, wv, NT_DIMS, preferred_element_type=F32)
        for lo, hi in ((0, n_ctx), (n_ctx, Tp)):
            du_ref[lo:hi, :] = _pool_stream_t(dpooled[lo:hi, :]).astype(BF16)

    return pl.pallas_call(
        body, name=name, grid=(B,), in_specs=[blk, blk, wsp, ssp], out_specs=[blk, wsp, ssp],
        out_shape=[_sds((T, POOL_WIDTH), BF16), _sds((POOL_WIDTH, POOL_WIDTH), F32), _sds((1, POOL_WIDTH), F32)],
        compiler_params=_params(("arbitrary",)),
    )(d_ob, pooled, w_bd, scale)


def _merge_specs(z, D, TR, tc, wa, wb, wc):
    def act(width):
        return pl.BlockSpec((TR, width), lambda i, n: (i, 0))

    def gate(part):
        return pl.BlockSpec((TR, tc), lambda i, n: (i, (GATE_COL + part * D) // tc + n))
    w_specs = [w.spec(w.shape[0], tc, lambda i, n: (0, n)) for w in (wa, wb, wc)]
    return [act(Q_WIDTH), act(POOL_WIDTH), act(Q_WIDTH), gate(0), gate(1), gate(2)] + w_specs


def _merge_fwd(oa, ob, oc, z, wa, wb, wc, *, D, TR, name):
    T = oa.shape[0]
    tc = D // N_CHIPS

    def body(oa_ref, ob_ref, oc_ref, ga_ref, gb_ref, gc_ref, wa_ref, wb_ref, wc_ref, y_ref):
        acc = jax.nn.sigmoid(ga_ref[...]) * jnp.dot(oa_ref[...], wa_ref[...], preferred_element_type=F32)
        acc += jax.nn.sigmoid(gb_ref[...]) * jnp.dot(ob_ref[...], wb_ref[...], preferred_element_type=F32)
        acc += jax.nn.sigmoid(gc_ref[...]) * jnp.dot(oc_ref[...], wc_ref[...], preferred_element_type=F32)
        y_ref[...] = acc.astype(BF16)

    return pl.pallas_call(
        body, name=name, grid=(T // TR, D // tc), in_specs=_merge_specs(z, D, TR, tc, wa, wb, wc),
        out_specs=pl.BlockSpec((TR, tc), lambda i, n: (i, n)), out_shape=_sds((T, D), BF16),
        compiler_params=_params(("parallel", "parallel")),
    )(oa, ob, oc, z, z, z, wa.arr, wb.arr, wc.arr)


def _merge_bwd(dy, oa, ob, oc, z, wa, wb, wc, *, D, TR, name):
    T = oa.shape[0]
    tc = D // N_CHIPS
    out = pl.BlockSpec((TR, tc), lambda i, n: (i, n))

    def body(dy_ref, oa_ref, ob_ref, oc_ref, ga_ref, gb_ref, gc_ref, wa_ref, wb_ref, wc_ref,
             dpa_ref, dpb_ref, dpc_ref, dga_ref, dgb_ref, dgc_ref):
        dyv = dy_ref[...]
        for o_ref, g_ref, w_ref, dp_ref, dg_ref in ((oa_ref, ga_ref, wa_ref, dpa_ref, dga_ref),
                                                    (ob_ref, gb_ref, wb_ref, dpb_ref, dgb_ref),
                                                    (oc_ref, gc_ref, wc_ref, dpc_ref, dgc_ref)):
            s = jax.nn.sigmoid(g_ref[...])
            proj = jnp.dot(o_ref[...], w_ref[...], preferred_element_type=F32)
            dp_ref[...] = (dyv * s).astype(BF16)
            dg_ref[...] = (dyv * proj * (s * (1.0 - s))).astype(BF16)

    return pl.pallas_call(
        body, name=name, grid=(T // TR, D // tc), in_specs=[out] + _merge_specs(z, D, TR, tc, wa, wb, wc),
        out_specs=[out] * 6, out_shape=[_sds((T, D), BF16)] * 6, compiler_params=_params(("parallel", "parallel")),
    )(dy, oa, ob, oc, z, z, z, wa.arr, wb.arr, wc.arr)


def _silu_rows(cc, name):
    def body(c_ref, s_ref):
        v = c_ref[...]
        s_ref[...] = (v * jax.nn.sigmoid(v)).astype(BF16)
    return pl.pallas_call(body, name=name, out_shape=_sds(cc.shape, BF16))(cc)


def _ada_bwd_rows(dm, ds, cc, name):
    def body(dm_ref, ds_ref, c_ref, db_ref, dc_ref):
        db_ref[...] = jnp.sum(dm_ref[...], axis=0, keepdims=True)
        v = c_ref[...]
        s = jax.nn.sigmoid(v)
        dc_ref[...] = ds_ref[...] * (s * (1.0 + v * (1.0 - s)))
    return pl.pallas_call(body, name=name, out_shape=[_sds((1, dm.shape[1]), F32), _sds(cc.shape, F32)])(dm, ds, cc)


def _row_tile(rows, cols):
    for t in (512, 256, 128, 64, 32, 16, 8):
        if rows % t == 0 and t * cols * 4 <= (1 << 20):
            return t
    return rows


def _add_own_layer(layers, landed, core, name):
    R, C = landed.shape
    tr = _row_tile(R, C)
    n_layers = len(layers)

    def body(c_ref, *refs):
        b_ref, o_ref, o16_ref = refs[n_layers:]
        for l in range(n_layers):
            @pl.when(c_ref[0] == l)
            def _(a_ref=refs[l]):
                tot = a_ref[...] + b_ref[...].astype(F32)
                o_ref[...] = tot
                o16_ref[...] = tot.astype(BF16)

    row = pl.BlockSpec((tr, C), lambda i, c: (i, 0))
    own = [pl.BlockSpec((tr, C), functools.partial(lambda i, c, l: (jnp.where(c[0] == l, i, 0), 0), l=l))
           for l in range(n_layers)]
    grid_spec = pltpu.PrefetchScalarGridSpec(num_scalar_prefetch=1, grid=(R // tr,),
                                             in_specs=own + [row], out_specs=[row, row])
    return pl.pallas_call(body, name=name, grid_spec=grid_spec, out_shape=[_sds((R, C), F32), _sds((R, C), BF16)],
                          compiler_params=_params(("arbitrary",)))(core, *layers, landed)


def _sum_chips(own, landed, chip, name):
    _, R, C = own.shape
    tr = _row_tile(R, C)

    def body(k_ref, a_ref, b_ref, o_ref):
        o_ref[...] = ((a_ref[...] + b_ref[0].astype(F32)) + b_ref[1].astype(F32)) + b_ref[2].astype(F32)

    grid_spec = pltpu.PrefetchScalarGridSpec(
        num_scalar_prefetch=1, grid=(R // tr,),
        in_specs=[pl.BlockSpec((None, tr, C), lambda i, k: (k[0], i, 0)), pl.BlockSpec((3, tr, C), lambda i, k: (0, i, 0))],
        out_specs=pl.BlockSpec((tr, C), lambda i, k: (i, 0)))
    return pl.pallas_call(body, name=name, grid_spec=grid_spec, out_shape=_sds((R, C), F32),
                          compiler_params=_params(("parallel",)))(chip, own, landed)


def _adam_math(w, g, m, v):
    m = ADAM_B1 * m + (1.0 - ADAM_B1) * g
    v = ADAM_B2 * v + (1.0 - ADAM_B2) * (g * g)
    m_hat = m / (1.0 - ADAM_B1 ** ADAM_STEP)
    v_hat = v / (1.0 - ADAM_B2 ** ADAM_STEP)
    delta = -ADAM_LR * (m_hat / (jnp.sqrt(v_hat) + ADAM_EPS) + ADAM_WD * w)
    return delta, m, v


def _adamw(w, mine, other, m, v, core, name):
    L, R, C = w.shape
    tr = _row_tile(R, C)

    def body(c_ref, w_ref, a_ref, b_ref, m_ref, v_ref, g_ref, d_ref, mo_ref, vo_ref):
        def step(g):
            d, mn, vn = _adam_math(w_ref[...], g, m_ref[...], v_ref[...])
            g_ref[...] = g
            d_ref[...] = d
            mo_ref[...] = mn
            vo_ref[...] = vn

        @pl.when(pl.program_id(0) == c_ref[0])
        def _():
            step(a_ref[...])

        @pl.when(pl.program_id(0) != c_ref[0])
        def _():
            step(b_ref[...])

    lay = pl.BlockSpec((None, tr, C), lambda l, i, c: (l, i, 0))
    row = pl.BlockSpec((tr, C), lambda l, i, c: (i, 0))
    grid_spec = pltpu.PrefetchScalarGridSpec(num_scalar_prefetch=1, grid=(L, R // tr),
                                             in_specs=[lay, row, row, lay, lay], out_specs=[lay] * 4)
    return pl.pallas_call(body, name=name, grid_spec=grid_spec, out_shape=[_sds((L, R, C), F32)] * 4,
                          compiler_params=_params(("parallel", "parallel")))(core, w, mine, other, m, v)


def _adamw_small(w, parts, m, v, name):
    R, C = w.shape

    def body(w_ref, p_ref, m_ref, v_ref, g_ref, d_ref, mo_ref, vo_ref):
        g = p_ref[0]
        for dev in range(1, 8):
            g = g + p_ref[dev]
        d, mn, vn = _adam_math(w_ref[...], g, m_ref[...], v_ref[...])
        g_ref[...] = g
        d_ref[...] = d
        mo_ref[...] = mn
        vo_ref[...] = vn

    return pl.pallas_call(body, name=name, out_shape=[_sds((R, C), F32)] * 4)(w, parts, m, v)


def _place():
    return lax.axis_index("x"), lax.axis_index("y"), lax.axis_index("c")


def _other_chips(x, y):
    return [(1 - x, y), (x, 1 - y), (1 - x, 1 - y)]


def _rcopy(src, dst, ssem, rsem, dev):
    return pltpu.make_async_remote_copy(src_ref=src, dst_ref=dst, send_sem=ssem, recv_sem=rsem,
                                        device_id=dev, device_id_type=MESH)


def _gather_weights(shards, name):
    n = len(shards)
    own_sem = 6

    def body(*refs):
        src, out = refs[:n], refs[n:2 * n]
        send_sems, recv_sems = refs[2 * n:]
        x, y, c = _place()
        sibling = (x, y, 1 - c)
        chips = _other_chips(x, y)
        mine = 2 * x + y
        own = [_rcopy(src[w], out[w].at[mine], send_sems.at[w, own_sem], recv_sems.at[w, own_sem], sibling)
               for w in range(n)]
        for cp in own:
            cp.start()
        first = [_rcopy(src[w].at[c], out[w].at[mine, c], send_sems.at[w, j], recv_sems.at[w, j], (*chip, c))
                 for w in range(n) for j, chip in enumerate(chips)]
        for cp in first:
            cp.start()
        passed = []
        for w in range(n):
            for j, (px, py) in enumerate(chips):
                landed = out[w].at[2 * px + py, c]
                _rcopy(landed, landed, send_sems.at[w, j], recv_sems.at[w, j], (px, py, c)).wait_recv()
                cp = _rcopy(landed, landed, send_sems.at[w, 3 + j], recv_sems.at[w, 3 + j], sibling)
                cp.start()
                passed.append(cp)
        for w in range(n):
            for j, (px, py) in enumerate(chips):
                landed = out[w].at[2 * px + py, 1 - c]
                _rcopy(landed, landed, send_sems.at[w, 3 + j], recv_sems.at[w, 3 + j], sibling).wait_recv()
        for cp in own:
            cp.wait_recv()
        for cp in first + passed + own:
            cp.wait_send()

    return pl.pallas_call(
        body, name=name, in_specs=[ANY] * n, out_specs=[ANY] * n,
        out_shape=[_sds((N_CHIPS,) + s.shape, s.dtype) for s in shards],
        scratch_shapes=[pltpu.SemaphoreType.DMA((n, 7)), pltpu.SemaphoreType.DMA((n, 7))],
    )(*shards)


def _send_other_layer(layer0, layer1, name):
    n = len(layer0)

    def body(*refs):
        src0, src1, out = refs[:n], refs[n:2 * n], refs[2 * n:3 * n]
        send_sems, recv_sems = refs[3 * n:]
        x, y, c = _place()

        def copies(src):
            return [_rcopy(src[w], out[w], send_sems.at[w], recv_sems.at[w], (x, y, 1 - c)) for w in range(n)]

        @pl.when(c == 0)
        def _():
            for cp in copies(src1):
                cp.start()

        @pl.when(c == 1)
        def _():
            for cp in copies(src0):
                cp.start()

        for cp in copies(src0):
            cp.wait_recv()
        for cp in copies(src0):
            cp.wait_send()

    return pl.pallas_call(
        body, name=name, in_specs=[ANY] * (2 * n), out_specs=[ANY] * n,
        out_shape=[_sds(s.shape, s.dtype) for s in layer0],
        scratch_shapes=[pltpu.SemaphoreType.DMA((n,)), pltpu.SemaphoreType.DMA((n,))],
    )(*layer0, *layer1)


def _send_chip_blocks(blocked, name):
    n = len(blocked)

    def body(*refs):
        src, out = refs[:n], refs[n:2 * n]
        send_sems, recv_sems = refs[2 * n:]
        x, y, c = _place()
        cps = [_rcopy(src[w].at[2 * px + py], out[w].at[j], send_sems.at[w, j], recv_sems.at[w, j], (px, py, c))
               for w in range(n) for j, (px, py) in enumerate(_other_chips(x, y))]
        for cp in cps:
            cp.start()
        for cp in cps:
            cp.wait_recv()
        for cp in cps:
            cp.wait_send()

    return pl.pallas_call(
        body, name=name, in_specs=[ANY] * n, out_specs=[ANY] * n,
        out_shape=[_sds((3,) + s.shape[1:], s.dtype) for s in blocked],
        scratch_shapes=[pltpu.SemaphoreType.DMA((n, 3)), pltpu.SemaphoreType.DMA((n, 3))],
    )(*blocked)


def _share_layers(reduced, name):
    n = len(reduced)

    def body(*refs):
        src, out = refs[:n], refs[n:2 * n]
        send_sems, recv_sems = refs[2 * n:]
        x, y, c = _place()
        cps = [_rcopy(src[w], out[w], send_sems.at[w], recv_sems.at[w], (x, y, 1 - c)) for w in range(n)]
        for cp in cps:
            cp.start()
        for cp in cps:
            cp.wait_recv()
        for cp in cps:
            cp.wait_send()

    return pl.pallas_call(
        body, name=name, in_specs=[ANY] * n, out_specs=[ANY] * n,
        out_shape=[_sds(s.shape, s.dtype) for s in reduced],
        scratch_shapes=[pltpu.SemaphoreType.DMA((n,)), pltpu.SemaphoreType.DMA((n,))],
    )(*reduced)


def _gather_small(block, name):
    m_per, n = block.shape

    def body(x_ref, out_ref, send_sems, recv_sems, local_sem):
        x, y, c = _place()
        me, sibling = (x, y, c), (x, y, 1 - c)
        chips = _other_chips(x, y)

        def rows(px, py, pc):
            return out_ref.at[pl.ds((4 * px + 2 * py + pc) * m_per, m_per), :]

        def copy(k, blk, to, src=None):
            return _rcopy(rows(*blk) if src is None else src, rows(*blk), send_sems.at[k], recv_sems.at[k], to)

        mine = pltpu.make_async_copy(x_ref, rows(*me), local_sem)
        mine.start()
        first = [copy(0, me, sibling, src=x_ref)]
        first += [copy(1 + j, me, (*chip, c), src=x_ref) for j, chip in enumerate(chips)]
        for cp in first:
            cp.start()
        passed = [copy(4 + j, (*chip, c), sibling) for j, chip in enumerate(chips)]
        for j, chip in enumerate(chips):
            copy(1 + j, (*chip, c), me).wait_recv()
            passed[j].start()
        copy(0, sibling, me).wait_recv()
        for j, chip in enumerate(chips):
            copy(4 + j, (*chip, 1 - c), me).wait_recv()
        for cp in first + passed:
            cp.wait_send()
        mine.wait()

    return pl.pallas_call(
        body, name=name, out_shape=_sds((8 * m_per, n), block.dtype),
        in_specs=[pl.BlockSpec(memory_space=pltpu.VMEM)], out_specs=pl.BlockSpec(memory_space=pltpu.VMEM),
        scratch_shapes=[pltpu.SemaphoreType.DMA((7,)), pltpu.SemaphoreType.DMA((7,)), pltpu.SemaphoreType.DMA],
    )(block)


def _rope_tables(n_ctx, seq):
    rows = seq // GRID_W
    r = jnp.repeat(jnp.arange(rows, dtype=F32), GRID_W)
    col = jnp.tile(jnp.arange(GRID_W, dtype=F32), rows)
    inv = 1.0 / (ROPE_THETA ** (jnp.arange(0, AXIS_DIM, 2, dtype=F32) / AXIS_DIM))
    ang = jnp.concatenate([r[:, None] * inv, col[:, None] * inv], axis=-1)
    cos = jnp.repeat(jnp.cos(ang), 2, axis=-1)
    sin = jnp.repeat(jnp.sin(ang), 2, axis=-1) * jnp.tile(jnp.array([-1.0, 1.0], F32), HEAD_DIM // 2)
    cos = jnp.concatenate([jnp.ones((n_ctx, HEAD_DIM), F32), cos], axis=0)
    sin = jnp.concatenate([jnp.zeros((n_ctx, HEAD_DIM), F32), sin], axis=0)
    return jnp.tile(cos, (1, 2)), jnp.tile(sin, (1, 2))


def _block_diag(w_pool):
    L, G = w_pool.shape[:2]
    eye = jnp.eye(G, dtype=w_pool.dtype)
    return (w_pool[:, :, :, None, :] * eye[None, :, None, :, None]).reshape(L, POOL_WIDTH, POOL_WIDTH)


def _qk_gains(small):
    qn = jnp.stack([small["q_norm_a"], small["q_norm_c"]], axis=1)[:, :, None, :]
    kn = jnp.stack([small["k_norm_a"], small["k_norm_c"]], axis=1)[:, :, None, :]
    L = qn.shape[0]
    rows = jnp.concatenate([jnp.broadcast_to(qn, (L, 2, N_HEADS, HEAD_DIM)), jnp.broadcast_to(kn, (L, 2, N_KV, HEAD_DIM)),
                            jnp.ones((L, 2, N_KV, HEAD_DIM), F32)], axis=2)
    return rows.reshape(L, 2, 1, QKV_WIDTH)


def _local_step(x, c, ctx, c_ctx, small, gw, target):
    B, S, D = x.shape
    N = ctx.shape[1]
    L = small["norm1"].shape[0]
    Tp = N + S
    T = B * Tp
    TR = N
    P = Tp // N
    rows16 = 16
    assert N % Q_BLOCK == 0 and S % N == 0 and B + 1 <= rows16
    TM = _tile(T, (1024, 768, 512, 384, 256, 128))
    TMG = _tile(T, (512, 384, 256, 128))

    X = jnp.concatenate([ctx, x], axis=1).reshape(T, D)
    cc = jnp.concatenate([c, c_ctx[None], jnp.zeros((rows16 - B - 1, D), F32)], axis=0)
    s_rows = _silu_rows(cc, "silu_rows")
    cos, sin = _rope_tables(N, S)
    all_gains = _qk_gains(small)
    all_w_bd = _block_diag(small["w_pool"]).astype(BF16)

    def weights(l):
        return dict(
            ada=_Opnd(gw["w_ada"], "bcols", l), w_in=_Opnd(gw["w_in"], "bcols", l),
            a=_Opnd(gw["w_br_a"], "bcols", l), b=_Opnd(gw["w_br_b"], "bcols", l), c=_Opnd(gw["w_br_c"], "bcols", l),
            out=_Opnd(gw["w_out"], "brows", l), mlp1=_Opnd(gw["w_mlp1"], "bcols", l), mlp2=_Opnd(gw["w_mlp2"], "brows", l))

    IN = weights(0)["w_in"].shape[1]
    DFF = weights(0)["mlp1"].shape[1]
    tn_in = _tile(IN // N_CHIPS, (1152, 768, 512, 384, 256, 128))
    tn_ff = _tile(DFF // N_CHIPS, (1024, 512, 256, 128))
    tn_ada = _tile(6 * D // N_CHIPS, (1536, 768, 512, 256, 128))
    tn_d = D // N_CHIPS
    tk_d = _tile(D, (512,))
    tk_tok = _tile(T, (2304, 1536, 1024, 768, 512, 384, 256))

    saved = []
    xin, pending = X, None
    for l in range(L):
        W = weights(l)
        b_ada = small["b_ada"][l].reshape(1, 6 * D)
        mod = _matmul(s_rows, W["ada"], "nn", tm=rows16, tn=tn_ada, tk=D, name=f"ada_fwd{l}",
                      epilogue=lambda acc, b: (acc + b,), extras=[(b_ada, (1, tn_ada), lambda m, n: (0, n))])
        modtab = jnp.stack([jnp.broadcast_to(mod[B], (B, 6 * D)), mod[:B]], axis=1).reshape(2 * B, 1, 6 * D)
        gains = all_gains[l]
        w_bd = all_w_bd[l]
        p_scale = small["pool_scale"][l].reshape(1, POOL_WIDTH)
        sink = small["sink_c"][l]

        x0, h1 = _res_norm(xin, pending, modtab, 0, 1, small["norm1"][l][None], TR=TR, P=P, name=f"norm1_fwd{l}")
        z = _matmul(h1, W["w_in"], "nn", tm=TM, tn=tn_in, tk=D, name=f"in_proj{l}")
        q2, k2, v2 = _qk_prep(z, gains, cos, sin, TR=TR, P=P, name=f"qk_prep{l}")
        oa, oa32, lse_a = _attn_fwd(q2, k2, v2, None, branch=0, B=B, n_ctx=N, window=False, name=f"attn_a_fwd{l}")
        oc, oc32, lse_c = _attn_fwd(q2, k2, v2, sink, branch=1, B=B, n_ctx=N, window=True, name=f"attn_c_fwd{l}")
        pooled, ob = _pool_fwd(z, w_bd, p_scale, B=B, Tp=Tp, n_ctx=N, name=f"pool_fwd{l}")
        y = _merge_fwd(oa, ob, oc, z, W["a"], W["b"], W["c"], D=D, TR=TMG, name=f"merge_fwd{l}")
        ao = _matmul(y, W["out"], "nn", tm=TM, tn=D, tk=tn_d, name=f"out_proj{l}")
        x1, h2 = _res_norm(x0, (ao, modtab, 2), modtab, 3, 4, small["norm2"][l][None], TR=TR, P=P, name=f"norm2_fwd{l}")
        a_pre, r_act = _matmul(h2, W["mlp1"], "nn", tm=TM, tn=tn_ff, tk=D, name=f"mlp1_fwd{l}", out_dtypes=(F32, BF16),
                               epilogue=lambda acc: (acc, jnp.square(jnp.maximum(acc, 0.0))))
        mo = _matmul(r_act, W["mlp2"], "nn", tm=TM, tn=D, tk=tn_ff, name=f"mlp2_fwd{l}")
        saved.append(dict(modtab=modtab, gains=gains, w_bd=w_bd, p_scale=p_scale, sink=sink, x0=x0, h1=h1, z=z,
                          q2=q2, k2=k2, v2=v2, oa=oa, ob=ob, oc=oc, oa32=oa32, oc32=oc32, lse_a=lse_a, lse_c=lse_c,
                          pooled=pooled, y=y, ao=ao,
                          x1=x1, h2=h2, a_pre=a_pre, r_act=r_act, mo=mo))
        xin, pending = x1, (mo, modtab, 5)

    dxo, loss, d_mo, dg2 = _loss_head(xin, pending[0], pending[1], 5, target.reshape(B * S, D), TR=TR, P=P, name="loss_head")

    big = {k: [None] * L for k in gw}
    big16 = {k: [None] * L for k in gw}
    sm = {k: [None] * L for k in ("b_ada", "norm1", "norm2", "q_norm_a", "k_norm_a", "q_norm_c", "k_norm_c",
                                   "sink_c", "w_pool", "pool_scale")}

    def dw(key, l, a, b, *, tm, tn, name, tk=tk_tok, blocked=True):
        outs = _matmul(a, b, "tn", tm=tm, tn=tn, tk=tk, name=name, out_dtypes=(F32, BF16), out_blocked=blocked)
        if not blocked:
            outs = [o.reshape(N_CHIPS, o.shape[0] // N_CHIPS, o.shape[1]) for o in outs]
        big[key][l], big16[key][l] = outs
    d_cctx = jnp.zeros((D,), F32)
    for l in reversed(range(L)):
        W, sv = weights(l), saved[l]
        modtab = sv["modtab"]
        d_a = _matmul(d_mo, W["mlp2"], "nt", tm=TM, tn=tn_ff, tk=D, name=f"mlp2_bwd{l}", out_dtypes=(BF16,),
                      epilogue=lambda acc, a: (acc * (2.0 * jnp.maximum(a, 0.0)),),
                      extras=[(sv["a_pre"], (TM, tn_ff), lambda m, n: (m, n))])
        dw("w_mlp2", l, sv["r_act"], d_mo, tm=tk_d, tn=D, name=f"mlp2_dw{l}", blocked=False)
        d_h2 = _matmul(d_a, W["mlp1"], "nt", tm=TM, tn=D, tk=tn_ff, name=f"mlp1_bwd{l}")
        dw("w_mlp1", l, sv["h2"], d_a, tm=tk_d, tn=tn_ff, name=f"mlp1_dw{l}")
        dx1, dsh2, dsc2, dn2, d_ao, dg1 = _norm_bwd(sv["x1"], d_h2, dxo, modtab, 4, small["norm2"][l][None],
                                                    (sv["ao"], modtab, 2), TR=TR, P=P, name=f"norm2_bwd{l}")
        d_y = _matmul(d_ao, W["out"], "nt", tm=TM, tn=tn_d, tk=D, name=f"out_bwd{l}")
        dw("w_out", l, sv["y"], d_ao, tm=tk_d, tn=D, name=f"out_dw{l}", blocked=False)
        d_pa, d_pb, d_pc, d_ga, d_gb, d_gc = _merge_bwd(d_y, sv["oa"], sv["ob"], sv["oc"], sv["z"], W["a"], W["b"], W["c"],
                                                        D=D, TR=TMG, name=f"merge_bwd{l}")
        d_oa = _matmul(d_pa, W["a"], "nt", tm=TM, tn=Q_WIDTH, tk=tn_d, name=f"br_a_bwd{l}", out_dtypes=(BF16,))
        d_ob = _matmul(d_pb, W["b"], "nt", tm=TM, tn=POOL_WIDTH, tk=tn_d, name=f"br_b_bwd{l}")
        d_oc = _matmul(d_pc, W["c"], "nt", tm=TM, tn=Q_WIDTH, tk=tn_d, name=f"br_c_bwd{l}", out_dtypes=(BF16,))
        dw("w_br_a", l, sv["oa"], d_pa, tm=Q_WIDTH, tn=tn_d, name=f"br_a_dw{l}")
        dw("w_br_b", l, sv["ob"], d_pb, tm=POOL_WIDTH, tn=tn_d, name=f"br_b_dw{l}")
        dw("w_br_c", l, sv["oc"], d_pc, tm=Q_WIDTH, tn=tn_d, name=f"br_c_dw{l}")
        d_u, d_wbd, d_ps = _pool_bwd(d_ob, sv["pooled"], sv["w_bd"], sv["p_scale"], B=B, Tp=Tp, n_ctx=N, name=f"pool_bwd{l}")
        dqa, dka, dva = _attn_bwd(sv["q2"], sv["k2"], sv["v2"], d_oa, sv["oa32"], sv["lse_a"], None, branch=0, B=B,
                                  n_ctx=N, window=False, name=f"attn_a_bwd{l}")
        dqc, dkc, dvc, dsink = _attn_bwd(sv["q2"], sv["k2"], sv["v2"], d_oc, sv["oc32"], sv["lse_c"], sv["sink"],
                                         branch=1, B=B, n_ctx=N, window=True, name=f"attn_c_bwd{l}")
        dz_a, dgains_a = _qk_prep_bwd(sv["z"], dqa, dka, dva, sv["gains"], cos, sin, branch=0, TR=TR, P=P,
                                      name=f"qk_prep_a_bwd{l}")
        dz_c, dgains_c = _qk_prep_bwd(sv["z"], dqc, dkc, dvc, sv["gains"], cos, sin, branch=1, TR=TR, P=P,
                                      name=f"qk_prep_c_bwd{l}")
        dz = jnp.concatenate([dz_a, dz_c, d_u, d_ga, d_gb, d_gc], axis=1)
        d_h1 = _matmul(dz, W["w_in"], "nt", tm=TM, tn=D, tk=tn_in, name=f"in_bwd{l}")
        dw("w_in", l, sv["h1"], dz, tm=tk_d, tn=tn_in, name=f"in_dw{l}")
        below = (saved[l - 1]["mo"], saved[l - 1]["modtab"], 5) if l > 0 else None
        dx0, dsh1, dsc1, dn1, *lower = _norm_bwd(sv["x0"], d_h1, dx1, modtab, 1, small["norm1"][l][None], below,
                                                 TR=TR, P=P, name=f"norm1_bwd{l}")
        this_dg2 = dg2
        if l > 0:
            d_mo, dg2 = lower

        dm_groups = jnp.concatenate([dsh1, dsc1, dg1, dsh2, dsc2, this_dg2], axis=-1).reshape(B, 2, 6 * D)
        dm = jnp.concatenate([dm_groups[:, 1], jnp.sum(dm_groups[:, 0], axis=0, keepdims=True),
                              jnp.zeros((rows16 - B - 1, 6 * D), F32)], axis=0)
        dm_bf = dm.astype(BF16)
        d_s = _matmul(dm_bf, W["ada"], "nt", tm=rows16, tn=D, tk=tn_ada, name=f"ada_bwd{l}")
        dw("w_ada", l, s_rows, dm_bf, tm=tk_d, tn=tn_ada, tk=rows16, name=f"ada_dw{l}")
        db_ada, dcc = _ada_bwd_rows(dm, d_s, cc, f"ada_rows_bwd{l}")
        d_cctx = d_cctx + dcc[B]

        sm["b_ada"][l] = db_ada[0]
        sm["norm1"][l] = jnp.sum(dn1, axis=(0, 1))
        sm["norm2"][l] = jnp.sum(dn2, axis=(0, 1))
        dgh = jnp.stack([dgains_a, dgains_c]).reshape(2, QKV_WIDTH // HEAD_DIM, HEAD_DIM)
        sm["q_norm_a"][l] = jnp.sum(dgh[0, :N_HEADS], axis=0)
        sm["k_norm_a"][l] = jnp.sum(dgh[0, N_HEADS:N_HEADS + N_KV], axis=0)
        sm["q_norm_c"][l] = jnp.sum(dgh[1, :N_HEADS], axis=0)
        sm["k_norm_c"][l] = jnp.sum(dgh[1, N_HEADS:N_HEADS + N_KV], axis=0)
        sm["sink_c"][l] = jnp.sum(dsink[:, :N_HEADS, 0], axis=0)
        sm["w_pool"][l] = jnp.stack([d_wbd[g * POOL_CH:(g + 1) * POOL_CH, g * POOL_CH:(g + 1) * POOL_CH]
                                     for g in range(POOL_WIDTH // POOL_CH)])
        sm["pool_scale"][l] = d_ps[0]
        dxo = dx0

    grad_x = dxo.reshape(B, Tp, D)[:, N:]
    small_grads = {k: jnp.stack(v) for k, v in sm.items()}
    small_grads["c_ctx"] = d_cctx
    return loss, grad_x, small_grads, big, big16


SMALL_NAMES = ("c_ctx", "b_ada", "norm1", "norm2", "q_norm_a", "k_norm_a", "q_norm_c", "k_norm_c", "sink_c",
               "w_pool", "pool_scale")
BIG_NAMES = ("w_ada", "w_in", "w_br_a", "w_br_b", "w_br_c", "w_out", "w_mlp1", "w_mlp2")
WEIGHT_NAMES = ("c_ctx", "w_ada", "b_ada", "norm1", "norm2", "w_in", "q_norm_a", "k_norm_a", "q_norm_c", "k_norm_c",
                "sink_c", "w_pool", "pool_scale", "w_br_a", "w_br_b", "w_br_c", "w_out", "w_mlp1", "w_mlp2")


def _pack(parts, rows):
    flat = jnp.concatenate([p.reshape(-1).astype(F32) for p in parts])
    return jnp.pad(flat, (0, rows * LANES - flat.shape[0])).reshape(rows, LANES)


def _unpack(packed, like):
    flat, out, at = packed.reshape(-1), [], 0
    for p in like:
        out.append(flat[at:at + p.size].reshape(p.shape))
        at += p.size
    return out


def _reduce_big(partials, partials16):
    names = list(partials)
    assert all(len(partials[k]) == 2 for k in names)
    x, y, c = _place()
    core = c.astype(jnp.int32).reshape(1)
    chip = (2 * x + y).astype(jnp.int32).reshape(1)
    shapes = [partials[k][0].shape for k in names]
    flat = [[g.reshape(-1, g.shape[-1]) for g in partials[k]] for k in names]
    flat16 = [[g.reshape(-1, g.shape[-1]) for g in partials16[k]] for k in names]
    landed = _send_other_layer([f[0] for f in flat16], [f[1] for f in flat16], "grads_to_sibling")
    in_chip = [_add_own_layer(f, r, core, f"grads_add_sibling_{k}") for k, f, r in zip(names, flat, landed)]
    blocked = [h.reshape(s) for s, (h, _) in zip(shapes, in_chip)]
    blocked16 = [h.reshape(s) for s, (_, h) in zip(shapes, in_chip)]
    from_chips = _send_chip_blocks(blocked16, "grads_to_chips")
    reduced = [_sum_chips(h, r, chip, f"grads_sum_chips_{k}") for k, h, r in zip(names, blocked, from_chips)]
    shared = _share_layers(reduced, "grads_share_layers")
    return core, dict(zip(names, zip(reduced, shared)))


def kernel(x, c, ctx, c_ctx, w_ada, b_ada, norm1, norm2, w_in, q_norm_a, k_norm_a, q_norm_c, k_norm_c, sink_c, w_pool, pool_scale, w_br_a, w_br_b, w_br_c, w_out, w_mlp1, w_mlp2, loss_target, m_c_ctx, m_w_ada, m_b_ada, m_norm1, m_norm2, m_w_in, m_q_norm_a, m_k_norm_a, m_q_norm_c, m_k_norm_c, m_sink_c, m_w_pool, m_pool_scale, m_w_br_a, m_w_br_b, m_w_br_c, m_w_out, m_w_mlp1, m_w_mlp2, v_c_ctx, v_w_ada, v_b_ada, v_norm1, v_norm2, v_w_in, v_q_norm_a, v_k_norm_a, v_q_norm_c, v_k_norm_c, v_sink_c, v_w_pool, v_pool_scale, v_w_br_a, v_w_br_b, v_w_br_c, v_w_out, v_w_mlp1, v_w_mlp2):
    given = dict(locals())
    w = {k: given[k] for k in WEIGHT_NAMES}
    m = {k: given["m_" + k] for k in WEIGHT_NAMES}
    v = {k: given["v_" + k] for k in WEIGHT_NAMES}

    gathered = _gather_weights([w[k].astype(BF16) for k in BIG_NAMES], "gather_weights")
    gw = dict(zip(BIG_NAMES, gathered))
    small = {k: w[k] for k in SMALL_NAMES}
    loss_part, grad_x, small_grads, big_grads, big_grads16 = _local_step(x, c, ctx, c_ctx, small, gw, loss_target)

    core, reduced = _reduce_big({k: big_grads[k] for k in BIG_NAMES}, {k: big_grads16[k] for k in BIG_NAMES})
    grads, deltas, new_m, new_v = {}, {}, {}, {}
    for k in BIG_NAMES:
        mine, other = reduced[k]
        grads[k], deltas[k], new_m[k], new_v[k] = _adamw(w[k], mine, other, m[k], v[k], core, f"adamw_{k}")

    sizes = sum(w[k].size for k in SMALL_NAMES) + LANES
    rows = -(-sizes // (8 * LANES)) * 8
    parts = _gather_small(_pack([small_grads[k] for k in SMALL_NAMES] + [loss_part[0]], rows), "gather_small")
    zero = jnp.zeros((LANES,), F32)
    packed = [_pack([t[k] for k in SMALL_NAMES] + [zero], rows) for t in (w, m, v)]
    outs = _adamw_small(packed[0], parts.reshape(8, rows, LANES), packed[1], packed[2], "adamw_small")
    like = [w[k] for k in SMALL_NAMES] + [zero]
    for store, packed_out in zip((grads, deltas, new_m, new_v), outs):
        pieces = _unpack(packed_out, like)
        for k, piece in zip(SMALL_NAMES, pieces):
            store[k] = piece
        if store is grads:
            loss = pieces[-1][0]

    return (loss, grad_x, *[grads[k] for k in WEIGHT_NAMES], *[deltas[k] for k in WEIGHT_NAMES],
            *[new_m[k] for k in WEIGHT_NAMES], *[new_v[k] for k in WEIGHT_NAMES])
```

```python
import functools

import jax
import jax.numpy as jnp
from jax import lax
from jax.experimental import pallas as pl
from jax.experimental.pallas import tpu as pltpu

F32 = jnp.float32
BF16 = jnp.bfloat16

HEAD_DIM = 64
GRID_W = 64
AXIS_DIM = HEAD_DIM // 2
ROPE_THETA = 10000.0
N_HEADS = 6
N_KV = 2
N_GROUP = N_HEADS // N_KV
POOL_CH = 64
POOL_WIDTH = 256
POOL_WINDOWS = (2, 4, 8, 16)
WINDOW = 128
Q_BLOCK = 128
Q_WIDTH = N_HEADS * HEAD_DIM
KV_WIDTH = N_KV * HEAD_DIM
GATE_COL = 2 * (Q_WIDTH + 2 * KV_WIDTH) + POOL_WIDTH
U_COL = 2 * (Q_WIDTH + 2 * KV_WIDTH)
EPS = 1e-6
NEG = -1e30
ADAM_LR = 0.001
ADAM_B1 = 0.9
ADAM_B2 = 0.999
ADAM_EPS = 1e-08
ADAM_WD = 0.01
ADAM_STEP = 10

N_CHIPS = 4
LANES = 128
POOL_PAD = 16
VMEM_LIMIT = 48 * 1024 * 1024
MESH = pl.DeviceIdType.MESH
ANY = pl.BlockSpec(memory_space=pl.ANY)


def _params(sem):
    return pltpu.CompilerParams(dimension_semantics=sem, vmem_limit_bytes=VMEM_LIMIT)


def _sds(shape, dtype):
    return jax.ShapeDtypeStruct(tuple(shape), dtype)


class _Opnd:
    def __init__(self, arr, kind="plain"):
        self.arr, self.kind = arr, kind

    @property
    def shape(self):
        a = self.arr
        if self.kind == "plain":
            return a.shape
        if self.kind == "bcols":
            return (a.shape[1], N_CHIPS * a.shape[2])
        return (N_CHIPS * a.shape[1], a.shape[2])

    def spec(self, tr, tc, fn):
        a = self.arr
        if self.kind == "plain":
            return pl.BlockSpec((tr, tc), lambda *g: fn(*g))
        if self.kind == "bcols":
            assert a.shape[2] % tc == 0, (a.shape, tc)
            per = a.shape[2] // tc

            def im(*g):
                ri, ci = fn(*g)
                return (ci // per, ri, ci % per)
            return pl.BlockSpec((None, tr, tc), im)
        assert a.shape[1] % tr == 0, (a.shape, tr)
        per = a.shape[1] // tr

        def im(*g):
            ri, ci = fn(*g)
            return (ri // per, ri % per, ci)
        return pl.BlockSpec((None, tr, tc), im)


def _matmul(a, b, mode, *, tm, tn, tk, name, out_dtypes=(F32,), epilogue=None, extras=(), out_blocked=False):
    if not isinstance(a, _Opnd):
        a = _Opnd(a)
    if not isinstance(b, _Opnd):
        b = _Opnd(b)
    if mode == "nn":
        (M, K), (K2, N) = a.shape, b.shape
        a_spec = a.spec(tm, tk, lambda m, n, k: (m, k))
        b_spec = b.spec(tk, tn, lambda m, n, k: (k, n))
        dims = (((1,), (0,)), ((), ()))
    elif mode == "nt":
        (M, K), (N, K2) = a.shape, b.shape
        a_spec = a.spec(tm, tk, lambda m, n, k: (m, k))
        b_spec = b.spec(tn, tk, lambda m, n, k: (n, k))
        dims = (((1,), (1,)), ((), ()))
    else:
        (K, M), (K2, N) = a.shape, b.shape
        a_spec = a.spec(tk, tm, lambda m, n, k: (k, m))
        b_spec = b.spec(tk, tn, lambda m, n, k: (k, n))
        dims = (((0,), (0,)), ((), ()))
    assert K == K2 and M % tm == 0 and N % tn == 0 and K % tk == 0, (name, M, N, K, K2, tm, tn, tk)
    nk = K // tk
    n_extra = len(extras)
    n_out = len(out_dtypes)
    extra_specs = [pl.BlockSpec(bs, functools.partial(lambda m, n, k, f: f(m, n), f=f)) for (_, bs, f) in extras]
    if out_blocked:
        assert (N // N_CHIPS) % tn == 0
        per = (N // N_CHIPS) // tn
        out_shape = [_sds((N_CHIPS, M, N // N_CHIPS), dt) for dt in out_dtypes]
        out_specs = [pl.BlockSpec((None, tm, tn), lambda m, n, k: (n // per, m, n % per)) for _ in out_dtypes]
    else:
        out_shape = [_sds((M, N), dt) for dt in out_dtypes]
        out_specs = [pl.BlockSpec((tm, tn), lambda m, n, k: (m, n)) for _ in out_dtypes]

    in_place = nk > 1 and epilogue is None and out_dtypes[0] == F32

    def body(*refs):
        a_ref, b_ref = refs[0], refs[1]
        extra_refs = refs[2:2 + n_extra]
        out_refs = refs[2 + n_extra:2 + n_extra + n_out]
        acc_ref = out_refs[0] if in_place else (refs[2 + n_extra + n_out] if nk > 1 else None)
        k = pl.program_id(2)
        prod = lax.dot_general(a_ref[...].astype(BF16), b_ref[...].astype(BF16), dims, preferred_element_type=F32)

        def finish(acc):
            outs = epilogue(acc, *[r[...] for r in extra_refs]) if epilogue is not None else (acc,) * n_out
            for o_ref, o in zip(out_refs, outs):
                o_ref[...] = o.astype(o_ref.dtype)

        if nk == 1:
            finish(prod)
        elif in_place:
            @pl.when(k == 0)
            def _():
                acc_ref[...] = prod

            @pl.when(k > 0)
            def _():
                acc_ref[...] += prod

            if n_out > 1:
                @pl.when(k == nk - 1)
                def _():
                    for o_ref in out_refs[1:]:
                        o_ref[...] = acc_ref[...].astype(o_ref.dtype)
        else:
            @pl.when(k == 0)
            def _():
                acc_ref[...] = prod

            @pl.when(k > 0)
            def _():
                acc_ref[...] += prod

            @pl.when(k == nk - 1)
            def _():
                finish(acc_ref[...])

    outs = pl.pallas_call(
        body, name=name, grid=(M // tm, N // tn, nk),
        in_specs=[a_spec, b_spec] + extra_specs, out_specs=out_specs, out_shape=out_shape,
        scratch_shapes=[pltpu.VMEM((tm, tn), F32)] if nk > 1 and not in_place else [],
        compiler_params=_params(("parallel", "parallel", "arbitrary")),
    )(a.arr, b.arr, *[e[0] for e in extras])
    return outs[0] if n_out == 1 else outs


def _tile(n, cands):
    for t in cands:
        if n % t == 0:
            return t
    return n


def _grp(i, P):
    return 2 * (i // P) + jnp.minimum(i % P, 1)


def _mod_spec(D, P, part):
    return pl.BlockSpec((1, 1, D), lambda i: (_grp(i, P), 0, part))


def _res_norm(x, pending, modtab, shift_part, scale_part, gain, *, TR, P, name):
    T, D = x.shape
    row = pl.BlockSpec((TR, D), lambda i: (i, 0))
    has_branch = pending is not None
    ins, specs = [x], [row]
    if has_branch:
        branch, gate_tab, gate_part = pending
        ins += [branch, gate_tab]
        specs += [row, _mod_spec(D, P, gate_part)]
    ins += [modtab, modtab, gain]
    specs += [_mod_spec(D, P, shift_part), _mod_spec(D, P, scale_part), pl.BlockSpec((1, D), lambda i: (0, 0))]

    def body(*refs):
        if has_branch:
            x_ref, br_ref, g_ref, sh_ref, sc_ref, gn_ref, xo_ref, h_ref = refs
            xv = x_ref[...] + g_ref[0] * br_ref[...]
        else:
            x_ref, sh_ref, sc_ref, gn_ref, xo_ref, h_ref = refs
            xv = x_ref[...]
        xo_ref[...] = xv
        y = xv * lax.rsqrt(jnp.mean(xv * xv, axis=-1, keepdims=True) + EPS) * gn_ref[...]
        h_ref[...] = (y * (1.0 + sc_ref[0]) + sh_ref[0]).astype(BF16)

    return pl.pallas_call(
        body, name=name, grid=(T // TR,), in_specs=specs, out_specs=[row, row],
        out_shape=[_sds((T, D), F32), _sds((T, D), BF16)], compiler_params=_params(("parallel",)),
    )(*ins)


def _norm_bwd(x, dh, dres, modtab, scale_part, gain, below, *, TR, P, name):
    T, D = x.shape
    G = modtab.shape[0]
    row = pl.BlockSpec((TR, D), lambda i: (i, 0))
    acc = pl.BlockSpec((1, 1, D), lambda i: (_grp(i, P), 0, 0))
    has_below = below is not None

    def body(*refs):
        if has_below:
            x_ref, dh_ref, dres_ref, sc_ref, gn_ref, br_ref, g_ref, dx_ref, dsh_ref, dsc_ref, dgn_ref, db_ref, dg_ref = refs
        else:
            x_ref, dh_ref, dres_ref, sc_ref, gn_ref, dx_ref, dsh_ref, dsc_ref, dgn_ref = refs
        r = pl.program_id(0) % P
        xv, dhv, gn = x_ref[...], dh_ref[...], gn_ref[...]
        rstd = lax.rsqrt(jnp.mean(xv * xv, axis=-1, keepdims=True) + EPS)
        xhat = xv * rstd
        dn = dhv * (1.0 + sc_ref[0])
        dxhat = dn * gn
        dxv = dres_ref[...] + rstd * (dxhat - xhat * jnp.mean(dxhat * xhat, axis=-1, keepdims=True))
        dx_ref[...] = dxv
        parts = [jnp.sum(dhv, axis=0, keepdims=True), jnp.sum(dhv * (xhat * gn), axis=0, keepdims=True),
                 jnp.sum(dn * xhat, axis=0, keepdims=True)]
        outs = [dsh_ref, dsc_ref, dgn_ref]
        if has_below:
            db_ref[...] = (dxv * g_ref[0]).astype(BF16)
            parts.append(jnp.sum(dxv * br_ref[...], axis=0, keepdims=True))
            outs.append(dg_ref)

        @pl.when(r <= 1)
        def _():
            for o_ref, part in zip(outs, parts):
                o_ref[0] = part

        @pl.when(r > 1)
        def _():
            for o_ref, part in zip(outs, parts):
                o_ref[0] += part

    ins = [x, dh, dres, modtab, gain]
    in_specs = [row, row, row, _mod_spec(D, P, scale_part), pl.BlockSpec((1, D), lambda i: (0, 0))]
    out_specs, out_shape = [row, acc, acc, acc], [_sds((T, D), F32)] + [_sds((G, 1, D), F32)] * 3
    if has_below:
        branch, gate_tab, gate_part = below
        ins += [branch, gate_tab]
        in_specs += [row, _mod_spec(D, P, gate_part)]
        out_specs += [row, acc]
        out_shape += [_sds((T, D), BF16), _sds((G, 1, D), F32)]
    return pl.pallas_call(body, name=name, grid=(T // TR,), in_specs=in_specs, out_specs=out_specs, out_shape=out_shape,
                          compiler_params=_params(("arbitrary",)))(*ins)


def _loss_head(x, branch, modtab, gate_part, target, *, TR, P, name):
    T, D = x.shape
    row = pl.BlockSpec((TR, D), lambda i: (i, 0))
    tgt = pl.BlockSpec((TR, D), lambda i: ((i // P) * (P - 1) + jnp.maximum(i % P - 1, 0), 0))
    one = pl.BlockSpec((1, LANES), lambda i: (0, 0))

    G = modtab.shape[0]
    acc = pl.BlockSpec((1, 1, D), lambda i: (_grp(i, P), 0, 0))

    def body(x_ref, br_ref, g_ref, t_ref, dy_ref, loss_ref, db_ref, dg_ref):
        i = pl.program_id(0)
        r = i % P

        @pl.when(i == 0)
        def _():
            loss_ref[...] = jnp.zeros_like(loss_ref)

        @pl.when(r == 0)
        def _():
            dy_ref[...] = jnp.zeros_like(dy_ref)
            db_ref[...] = jnp.zeros_like(db_ref)
            dg_ref[...] = jnp.zeros_like(dg_ref)

        @pl.when(r > 0)
        def _():
            brv, g = br_ref[...], g_ref[0]
            err = x_ref[...] + g * brv - t_ref[...]
            dy = err / D
            dy_ref[...] = dy
            db_ref[...] = (dy * g).astype(BF16)
            part = jnp.sum(dy * brv, axis=0, keepdims=True)
            per_tok = jnp.mean(err * err, axis=-1, keepdims=True)
            loss_ref[...] += 0.5 * jnp.sum(per_tok, axis=0, keepdims=True)

            @pl.when(r == 1)
            def _():
                dg_ref[0] = part

            @pl.when(r > 1)
            def _():
                dg_ref[0] += part

    return pl.pallas_call(
        body, name=name, grid=(T // TR,), in_specs=[row, row, _mod_spec(D, P, gate_part), tgt],
        out_specs=[row, one, row, acc],
        out_shape=[_sds((T, D), F32), _sds((1, LANES), F32), _sds((T, D), BF16), _sds((G, 1, D), F32)],
        compiler_params=_params(("arbitrary",)),
    )(x, branch, modtab, target)


QKV_WIDTH = Q_WIDTH + 2 * KV_WIDTH
QK_NORMED = 4


def _seg_mean(v):
    lane = lax.broadcasted_iota(jnp.int32, v.shape, 1)
    lo = lane < HEAD_DIM
    s0 = jnp.sum(jnp.where(lo, v, 0.0), axis=-1, keepdims=True)
    s1 = jnp.sum(jnp.where(lo, 0.0, v), axis=-1, keepdims=True)
    return jnp.where(lo, s0, s1) * (1.0 / HEAD_DIM)


def _pair_swap(v):
    lane = lax.broadcasted_iota(jnp.int32, v.shape, 1)
    return jnp.where((lane & 1) == 0, pltpu.roll(v, LANES - 1, 1), pltpu.roll(v, 1, 1))


def _chunk(c):
    return slice(c * LANES, (c + 1) * LANES)


def _qk_prep(z, gains, cos, sin, *, TR, P, name):
    T = z.shape[0]

    def body(z_ref, g_ref, c_ref, s_ref, q_ref, k_ref, v_ref):
        cs, sn = c_ref[...], s_ref[...]
        for ch in range(QK_NORMED):
            xv = z_ref[:, _chunk(ch)]
            y = xv * lax.rsqrt(_seg_mean(xv * xv) + EPS) * g_ref[0, :, _chunk(ch)]
            out = (y * cs + _pair_swap(y) * sn).astype(BF16)
            if ch < QK_NORMED - 1:
                q_ref[:, _chunk(ch)] = out
            else:
                k_ref[...] = out
        v_ref[...] = z_ref[:, _chunk(QK_NORMED)].astype(BF16)

    def out(width):
        return pl.BlockSpec((None, TR, width), lambda i, j: (j, i, 0))
    return pl.pallas_call(
        body, name=name, grid=(T // TR, 2),
        in_specs=[pl.BlockSpec((TR, QKV_WIDTH), lambda i, j: (i, j)),
                  pl.BlockSpec((1, 1, QKV_WIDTH), lambda i, j: (j, 0, 0)),
                  pl.BlockSpec((TR, LANES), lambda i, j: (i % P, 0)),
                  pl.BlockSpec((TR, LANES), lambda i, j: (i % P, 0))],
        out_specs=[out(Q_WIDTH), out(KV_WIDTH), out(KV_WIDTH)],
        out_shape=[_sds((2, T, Q_WIDTH), BF16), _sds((2, T, KV_WIDTH), BF16), _sds((2, T, KV_WIDTH), BF16)],
        compiler_params=_params(("parallel", "parallel")),
    )(z, gains, cos, sin)


def _qk_prep_bwd(z, dq, dk, dv, gains, cos, sin, *, branch, TR, P, name):
    T = z.shape[0]
    nt = T // TR

    def body(z_ref, dq_ref, dk_ref, dv_ref, g_ref, c_ref, s_ref, dz_ref, dg_ref):
        i = pl.program_id(0)
        cs, sn = c_ref[...], s_ref[...]
        parts = []
        for ch in range(QK_NORMED):
            xv, g = z_ref[:, _chunk(ch)], g_ref[0, :, _chunk(ch)]
            dout = dq_ref[:, _chunk(ch)] if ch < QK_NORMED - 1 else dk_ref[...]
            dy = dout * cs + _pair_swap(dout * sn)
            rstd = lax.rsqrt(_seg_mean(xv * xv) + EPS)
            xhat = xv * rstd
            dxhat = dy * g
            dz_ref[:, _chunk(ch)] = (rstd * (dxhat - xhat * _seg_mean(dxhat * xhat))).astype(BF16)
            parts.append(jnp.sum(dy * xhat, axis=0, keepdims=True))
        dz_ref[:, _chunk(QK_NORMED)] = dv_ref[...].astype(BF16)
        parts.append(jnp.zeros((1, LANES), F32))
        part = jnp.concatenate(parts, axis=1)

        @pl.when(i == 0)
        def _():
            dg_ref[0] = part

        @pl.when(i > 0)
        def _():
            dg_ref[0] += part

    def rows(width, col=0):
        return pl.BlockSpec((TR, width), lambda i: (i, col))
    return pl.pallas_call(
        body, name=name, grid=(nt,),
        in_specs=[rows(QKV_WIDTH, branch), rows(Q_WIDTH), rows(KV_WIDTH), rows(KV_WIDTH),
                  pl.BlockSpec((1, 1, QKV_WIDTH), lambda i: (branch, 0, 0)),
                  pl.BlockSpec((TR, LANES), lambda i: (i % P, 0)),
                  pl.BlockSpec((TR, LANES), lambda i: (i % P, 0))],
        out_specs=[rows(QKV_WIDTH), pl.BlockSpec((1, 1, QKV_WIDTH), lambda i: (0, 0, 0))],
        out_shape=[_sds((T, QKV_WIDTH), BF16), _sds((1, 1, QKV_WIDTH), F32)],
        compiler_params=_params(("arbitrary",)),
    )(z, dq, dk, dv, gains, cos, sin)


NT_DIMS = (((1,), (1,)), ((), ()))
TN_DIMS = (((0,), (0,)), ((), ()))
QROWS = N_GROUP * Q_BLOCK
SCORE_SCALE = HEAD_DIM ** -0.5
BAND = Q_BLOCK + 2 * WINDOW
FWD_LATENT_CHUNK = 256
BWD_LATENT_CHUNK = 1024


def _move_head(block, half_from, half_to):
    lane = lax.broadcasted_iota(jnp.int32, block.shape, 1)
    src = block if half_from == half_to else pltpu.roll(block, HEAD_DIM, 1)
    keep = (lane < HEAD_DIM) if half_to == 0 else (lane >= HEAD_DIM)
    return jnp.where(keep, src, 0.0)


def _stack_heads(lane_block, j):
    pieces = []
    for h in range(N_GROUP * j, N_GROUP * (j + 1)):
        pieces.append(_move_head(lane_block(h // 2), h % 2, j))
    return jnp.concatenate(pieces, axis=0)


def _lane_blocks(ref):
    return lambda m: ref[:, m * LANES:(m + 1) * LANES].astype(F32)


def _unstack_heads(stacked, ref):
    heads = []
    for h in range(N_HEADS):
        j, r = h // N_GROUP, h % N_GROUP
        heads.append(_move_head(stacked[j][r * Q_BLOCK:(r + 1) * Q_BLOCK], j, h % 2))
    for m in range(N_HEADS // 2):
        ref[:, m * LANES:(m + 1) * LANES] = (heads[2 * m] + heads[2 * m + 1]).astype(ref.dtype)


def _key_chunks(i, latent, *, n_ctx, t_all, window, chunk, latent_chunk):
    ctx = [(s, chunk, False) for s in range(0, n_ctx, chunk)]
    if not latent:
        return ctx
    if not window:
        wide = latent_chunk if (t_all - n_ctx) % latent_chunk == 0 else chunk
        return ctx + [(s, wide, False) for s in range(n_ctx, t_all, wide)]
    start = pl.multiple_of(jnp.minimum((i - 1) * Q_BLOCK, t_all - BAND), Q_BLOCK)
    band_chunk = BAND if latent_chunk >= BAND else (chunk if BAND % chunk == 0 else Q_BLOCK)
    return ctx + [(start + s, band_chunk, True) for s in range(0, BAND, band_chunk)]


def _scores(q, k_ref, i, start, size, masked, *, n_ctx):
    s = lax.dot_general(q, k_ref[pl.ds(start, size), :], NT_DIMS, preferred_element_type=F32)
    if masked:
        qpos = (i * Q_BLOCK - n_ctx) + (lax.broadcasted_iota(jnp.int32, (QROWS, size), 0) & (Q_BLOCK - 1))
        kpos = (start - n_ctx) + lax.broadcasted_iota(jnp.int32, (QROWS, size), 1)
        valid = (kpos - qpos <= WINDOW) & (qpos - kpos <= WINDOW) & (kpos >= 0)
        s = jnp.where(valid, s, NEG)
    return s


def _sink_column(sink_ref, j):
    r = lax.broadcasted_iota(jnp.int32, (QROWS, 1), 0)
    s0, s1, s2 = sink_ref[j * N_GROUP], sink_ref[j * N_GROUP + 1], sink_ref[j * N_GROUP + 2]
    return jnp.where(r < Q_BLOCK, s0, jnp.where(r < 2 * Q_BLOCK, s1, s2))


def _attn_specs(Tp, branch):
    nq = Tp // Q_BLOCK
    q_in = pl.BlockSpec((None, Q_BLOCK, Q_WIDTH), lambda b, i: (branch, b * nq + i, 0))
    kv_in = pl.BlockSpec((None, Tp, KV_WIDTH), lambda b, i: (branch, b, 0))
    q_out = pl.BlockSpec((Q_BLOCK, Q_WIDTH), lambda b, i: (b * nq + i, 0))
    kv_out = pl.BlockSpec((Tp, KV_WIDTH), lambda b, i: (b, 0))
    return q_in, kv_in, q_out, kv_out


def _attn_chunk(Tp):
    return 256 if Tp % 256 == 0 else Q_BLOCK


def _attn_fwd(q, k, v, sink, *, branch, B, n_ctx, window, name, rider=None):
    T = q.shape[1]
    Tp = T // B
    nq = Tp // Q_BLOCK
    has_sink = sink is not None
    n_ride = rider.n if rider is not None else 0
    q_in, kv_in, q_out, _ = _attn_specs(Tp, branch)
    lse_spec = pl.BlockSpec((None, N_KV * QROWS, 1), lambda b, i: (b * nq + i, 0, 0))

    def body(*refs):
        refs = list(refs)
        sink_ref = refs.pop(0) if has_sink else None
        q_ref, k_ref, v_ref = refs[:3]
        ride_src = refs[3:3 + n_ride]
        o_ref, o32_ref, lse_ref = refs[3 + n_ride:6 + n_ride]
        ride_out = refs[6 + n_ride:6 + 2 * n_ride]
        ride_sems = refs[6 + 2 * n_ride:]
        i = pl.program_id(1)
        if rider is not None:
            @pl.when((pl.program_id(0) == 0) & (i == 0))
            def _():
                rider.start(ride_src, ride_out, *ride_sems)

        def run(latent):
            outs = []
            for j in range(N_KV):
                qv = (_stack_heads(_lane_blocks(q_ref), j) * SCORE_SCALE).astype(BF16)
                if has_sink:
                    m, l = _sink_column(sink_ref, j), jnp.ones((QROWS, 1), F32)
                else:
                    m, l = jnp.full((QROWS, 1), NEG, F32), jnp.zeros((QROWS, 1), F32)
                acc = jnp.zeros((QROWS, LANES), F32)
                for start, size, masked in _key_chunks(i, latent, n_ctx=n_ctx, t_all=Tp, window=window,
                                                       chunk=_attn_chunk(Tp), latent_chunk=FWD_LATENT_CHUNK):
                    s = _scores(qv, k_ref, i, start, size, masked, n_ctx=n_ctx)
                    m_new = jnp.maximum(m, jnp.max(s, axis=-1, keepdims=True))
                    alpha = jnp.exp(m - m_new)
                    p = jnp.exp(s - m_new)
                    l = l * alpha + jnp.sum(p, axis=-1, keepdims=True)
                    acc = acc * alpha + jnp.dot(p.astype(BF16), v_ref[pl.ds(start, size), :], preferred_element_type=F32)
                    m = m_new
                outs.append(acc * (1.0 / l))
                lse_ref[j * QROWS:(j + 1) * QROWS, :] = m + jnp.log(l)
            _unstack_heads(outs, o_ref)
            _unstack_heads(outs, o32_ref)

        @pl.when(i < n_ctx // Q_BLOCK)
        def _():
            run(False)

        @pl.when(i >= n_ctx // Q_BLOCK)
        def _():
            run(True)

        if rider is not None:
            @pl.when((pl.program_id(0) == B - 1) & (i == nq - 1))
            def _():
                rider.finish(ride_src, ride_out, *ride_sems)

    ins, specs = [q, k, v], [q_in, kv_in, kv_in]
    if has_sink:
        ins, specs = [sink] + ins, [pl.BlockSpec(memory_space=pltpu.SMEM)] + specs
    out_specs = [q_out, q_out, lse_spec]
    out_shape = [_sds((T, Q_WIDTH), BF16), _sds((T, Q_WIDTH), F32), _sds((T // Q_BLOCK, N_KV * QROWS, 1), F32)]
    scratch = []
    if rider is not None:
        ins, specs = ins + list(rider.shards), specs + [ANY] * n_ride
        out_specs, out_shape, scratch = out_specs + [ANY] * n_ride, out_shape + rider.out_shape, rider.scratch
    return pl.pallas_call(
        body, name=name, grid=(B, nq), in_specs=specs, out_specs=out_specs, out_shape=out_shape, scratch_shapes=scratch,
        compiler_params=_params(("arbitrary", "arbitrary") if rider is not None else ("parallel", "parallel")),
    )(*ins)


def _attn_bwd(q, k, v, do, o32, lse, sink, *, branch, B, n_ctx, window, name):
    T = q.shape[1]
    Tp = T // B
    nq = Tp // Q_BLOCK
    has_sink = sink is not None
    q_in, kv_in, q_out, kv_out = _attn_specs(Tp, branch)
    lse_spec = pl.BlockSpec((None, N_KV * QROWS, 1), lambda b, i: (b * nq + i, 0, 0))
    sink_spec = pl.BlockSpec((None, 8, LANES), lambda b, i: (b, 0, 0))

    def body(*refs):
        if has_sink:
            sink_ref, q_ref, k_ref, v_ref, do_ref, o_ref, lse_ref, dq_ref, dk_ref, dv_ref, ds_ref, dkt_ref, dvt_ref = refs
        else:
            q_ref, k_ref, v_ref, do_ref, o_ref, lse_ref, dq_ref, dk_ref, dv_ref, dkt_ref, dvt_ref = refs
        i = pl.program_id(1)

        @pl.when(i == 0)
        def _():
            dk_ref[...] = jnp.zeros_like(dk_ref)
            dv_ref[...] = jnp.zeros_like(dv_ref)
            if not window:
                dkt_ref[...] = jnp.zeros_like(dkt_ref)
                dvt_ref[...] = jnp.zeros_like(dvt_ref)
            if has_sink:
                ds_ref[...] = jnp.zeros_like(ds_ref)

        def run(latent):
            upd = jnp.zeros((8, LANES), F32)
            do_blocks, o_blocks = _lane_blocks(do_ref), _lane_blocks(o_ref)
            qvs = [(_stack_heads(_lane_blocks(q_ref), j) * SCORE_SCALE).astype(BF16) for j in range(N_KV)]
            dovs = [_stack_heads(do_blocks, j).astype(BF16) for j in range(N_KV)]
            deltas = [jnp.sum(_stack_heads(lambda m: do_blocks(m) * o_blocks(m), j), axis=-1, keepdims=True)
                      for j in range(N_KV)]
            lses = [lse_ref[j * QROWS:(j + 1) * QROWS, :] for j in range(N_KV)]
            q_all, do_all = jnp.concatenate(qvs, axis=0), jnp.concatenate(dovs, axis=0)
            q_all_t, do_all_t = q_all.T, do_all.T
            dqs = [jnp.zeros((QROWS, LANES), F32) for _ in range(N_KV)]
            for start, size, masked in _key_chunks(i, latent, n_ctx=n_ctx, t_all=Tp, window=window,
                                                   chunk=_attn_chunk(Tp), latent_chunk=BWD_LATENT_CHUNK):
                rows = pl.ds(start, size)
                ds_all, p_all = [], []
                for j in range(N_KV):
                    p = jnp.exp(_scores(qvs[j], k_ref, i, start, size, masked, n_ctx=n_ctx) - lses[j])
                    dp = lax.dot_general(dovs[j], v_ref[rows, :], NT_DIMS, preferred_element_type=F32)
                    ds = (p * (dp - deltas[j])).astype(BF16)
                    dqs[j] = dqs[j] + jnp.dot(ds, k_ref[rows, :], preferred_element_type=F32)
                    ds_all.append(ds)
                    p_all.append(p.astype(BF16))
                ds_cat, p_cat = jnp.concatenate(ds_all, axis=0), jnp.concatenate(p_all, axis=0)
                if window:
                    dk_ref[rows, :] += lax.dot_general(ds_cat, q_all, TN_DIMS, preferred_element_type=F32)
                    dv_ref[rows, :] += lax.dot_general(p_cat, do_all, TN_DIMS, preferred_element_type=F32)
                else:
                    dkt_ref[:, start:start + size] += jnp.dot(q_all_t, ds_cat, preferred_element_type=F32)
                    dvt_ref[:, start:start + size] += jnp.dot(do_all_t, p_cat, preferred_element_type=F32)
            dqs = [dq * SCORE_SCALE for dq in dqs]
            for j in range(N_KV):
                if has_sink:
                    contrib = -(jnp.exp(_sink_column(sink_ref, j) - lses[j]) * deltas[j])
                    r = lax.broadcasted_iota(jnp.int32, (QROWS, 1), 0)
                    row8 = lax.broadcasted_iota(jnp.int32, (8, LANES), 0)
                    for h in range(N_GROUP):
                        in_head = (r >= h * Q_BLOCK) & (r < (h + 1) * Q_BLOCK)
                        tot = jnp.sum(jnp.where(in_head, contrib, 0.0), axis=0, keepdims=True)
                        upd = upd + jnp.where(row8 == j * N_GROUP + h, tot, 0.0)
            _unstack_heads(dqs, dq_ref)
            if has_sink:
                ds_ref[...] += upd

        @pl.when(i < n_ctx // Q_BLOCK)
        def _():
            run(False)

        @pl.when(i >= n_ctx // Q_BLOCK)
        def _():
            run(True)

        if not window:
            @pl.when(i == nq - 1)
            def _():
                dk_ref[...] += dkt_ref[...].T
                dv_ref[...] += dvt_ref[...].T

    ins, specs = [q, k, v, do, o32, lse], [q_in, kv_in, kv_in, q_out, q_out, lse_spec]
    out_specs = [q_out, kv_out, kv_out]
    out_shape = [_sds((T, Q_WIDTH), F32), _sds((T, KV_WIDTH), F32), _sds((T, KV_WIDTH), F32)]
    if has_sink:
        ins, specs = [sink] + ins, [pl.BlockSpec(memory_space=pltpu.SMEM)] + specs
        out_specs.append(sink_spec)
        out_shape.append(_sds((B, 8, LANES), F32))
    return pl.pallas_call(
        body, name=name, grid=(B, Tp // Q_BLOCK), in_specs=specs, out_specs=out_specs, out_shape=out_shape,
        scratch_shapes=[pltpu.VMEM((KV_WIDTH, LANES if window else Tp), F32)] * 2,
        compiler_params=_params(("parallel", "arbitrary")),
    )(*ins)


def _window_sums(xp):
    n = xp.shape[0]

    def ahead(a, k):
        return pltpu.roll(a, n - k, 0)
    a2 = xp + ahead(xp, 1)
    a4 = a2 + ahead(a2, 2)
    a8 = a4 + ahead(a4, 4)
    a16 = a8 + ahead(a8, 8)
    return (a2, a4, a8, a16)


def _by_group(vals):
    lane = lax.broadcasted_iota(jnp.int32, vals[0].shape, 1)
    return jnp.where(lane < POOL_CH, vals[0], jnp.where(lane < 2 * POOL_CH, vals[1],
                     jnp.where(lane < 3 * POOL_CH, vals[2], vals[3])))


def _pool_counts(n):
    t = lax.broadcasted_iota(jnp.int32, (n, POOL_WIDTH), 0)
    cnts = [(jnp.minimum(t + w // 2, n) - jnp.maximum(t - w // 2, 0)).astype(F32) for w in POOL_WINDOWS]
    return _by_group(cnts)


def _pad_rows(x):
    zeros = jnp.zeros((POOL_PAD, x.shape[1]), x.dtype)
    return jnp.concatenate([zeros, x, zeros], axis=0)


def _pool_stream(u):
    n = u.shape[0]
    sums = _window_sums(_pad_rows(u))
    tots = [pltpu.roll(a, w // 2, 0)[POOL_PAD:POOL_PAD + n] for a, w in zip(sums, POOL_WINDOWS)]
    return _by_group(tots) / _pool_counts(n) - u


def _pool_stream_t(dp):
    n = dp.shape[0]
    sums = _window_sums(_pad_rows(dp / _pool_counts(n)))
    tots = [pltpu.roll(a, w // 2 - 1, 0)[POOL_PAD:POOL_PAD + n] if w > 2 else a[POOL_PAD:POOL_PAD + n]
            for a, w in zip(sums, POOL_WINDOWS)]
    return _by_group(tots) - dp


def _pool_fwd(z, w_bd, scale, *, B, Tp, n_ctx, name):
    T = z.shape[0]
    blk = pl.BlockSpec((Tp, POOL_WIDTH), lambda b: (b, U_COL // POOL_WIDTH))
    out = pl.BlockSpec((Tp, POOL_WIDTH), lambda b: (b, 0))

    def body(u_ref, w_ref, s_ref, p_ref, o_ref):
        for lo, hi in ((0, n_ctx), (n_ctx, Tp)):
            pooled = _pool_stream(u_ref[lo:hi, :]).astype(BF16)
            p_ref[lo:hi, :] = pooled
            mixed = jnp.dot(pooled, w_ref[...], preferred_element_type=F32)
            o_ref[lo:hi, :] = (mixed * s_ref[...]).astype(BF16)

    return pl.pallas_call(
        body, name=name, grid=(B,),
        in_specs=[blk, pl.BlockSpec((POOL_WIDTH, POOL_WIDTH), lambda b: (0, 0)), pl.BlockSpec((1, POOL_WIDTH), lambda b: (0, 0))],
        out_specs=[out, out], out_shape=[_sds((T, POOL_WIDTH), BF16)] * 2, compiler_params=_params(("parallel",)),
    )(z, w_bd, scale)


def _pool_bwd(d_ob, pooled, w_bd, scale, *, B, Tp, n_ctx, name):
    T = d_ob.shape[0]
    blk = pl.BlockSpec((Tp, POOL_WIDTH), lambda b: (b, 0))
    wsp = pl.BlockSpec((POOL_WIDTH, POOL_WIDTH), lambda b: (0, 0))
    ssp = pl.BlockSpec((1, POOL_WIDTH), lambda b: (0, 0))

    def body(d_ref, p_ref, w_ref, s_ref, du_ref, dw_ref, dsc_ref):
        @pl.when(pl.program_id(0) == 0)
        def _():
            dw_ref[...] = jnp.zeros_like(dw_ref)
            dsc_ref[...] = jnp.zeros_like(dsc_ref)

        dv, pv, wv = d_ref[...], p_ref[...], w_ref[...]
        mixed = jnp.dot(pv, wv, preferred_element_type=F32)
        dsc_ref[...] += jnp.sum(dv * mixed, axis=0, keepdims=True)
        dmixed = (dv * s_ref[...]).astype(BF16)
        dw_ref[...] += lax.dot_general(pv, dmixed, TN_DIMS, preferred_element_type=F32)
        dpooled = lax.dot_general(dmixed, wv, NT_DIMS, preferred_element_type=F32)
        for lo, hi in ((0, n_ctx), (n_ctx, Tp)):
            du_ref[lo:hi, :] = _pool_stream_t(dpooled[lo:hi, :]).astype(BF16)

    return pl.pallas_call(
        body, name=name, grid=(B,), in_specs=[blk, blk, wsp, ssp], out_specs=[blk, wsp, ssp],
        out_shape=[_sds((T, POOL_WIDTH), BF16), _sds((POOL_WIDTH, POOL_WIDTH), F32), _sds((1, POOL_WIDTH), F32)],
        compiler_params=_params(("arbitrary",)),
    )(d_ob, pooled, w_bd, scale)


def _merge_specs(z, D, TR, tc, wa, wb, wc):
    def act(width):
        return pl.BlockSpec((TR, width), lambda i, n: (i, 0))

    def gate(part):
        return pl.BlockSpec((TR, tc), lambda i, n: (i, (GATE_COL + part * D) // tc + n))
    w_specs = [w.spec(w.shape[0], tc, lambda i, n: (0, n)) for w in (wa, wb, wc)]
    return [act(Q_WIDTH), act(POOL_WIDTH), act(Q_WIDTH), gate(0), gate(1), gate(2)] + w_specs


def _merge_fwd(oa, ob, oc, z, wa, wb, wc, *, D, TR, name):
    T = oa.shape[0]
    tc = D // N_CHIPS

    def body(oa_ref, ob_ref, oc_ref, ga_ref, gb_ref, gc_ref, wa_ref, wb_ref, wc_ref, y_ref):
        acc = jax.nn.sigmoid(ga_ref[...]) * jnp.dot(oa_ref[...], wa_ref[...], preferred_element_type=F32)
        acc += jax.nn.sigmoid(gb_ref[...]) * jnp.dot(ob_ref[...], wb_ref[...], preferred_element_type=F32)
        acc += jax.nn.sigmoid(gc_ref[...]) * jnp.dot(oc_ref[...], wc_ref[...], preferred_element_type=F32)
        y_ref[...] = acc.astype(BF16)

    return pl.pallas_call(
        body, name=name, grid=(T // TR, D // tc), in_specs=_merge_specs(z, D, TR, tc, wa, wb, wc),
        out_specs=pl.BlockSpec((TR, tc), lambda i, n: (i, n)), out_shape=_sds((T, D), BF16),
        compiler_params=_params(("parallel", "parallel")),
    )(oa, ob, oc, z, z, z, wa.arr, wb.arr, wc.arr)


def _merge_bwd(dy, oa, ob, oc, z, wa, wb, wc, *, D, TR, name):
    T = oa.shape[0]
    tc = D // N_CHIPS
    out = pl.BlockSpec((TR, tc), lambda i, n: (i, n))

    def body(dy_ref, oa_ref, ob_ref, oc_ref, ga_ref, gb_ref, gc_ref, wa_ref, wb_ref, wc_ref,
             dpa_ref, dpb_ref, dpc_ref, dga_ref, dgb_ref, dgc_ref):
        dyv = dy_ref[...]
        for o_ref, g_ref, w_ref, dp_ref, dg_ref in ((oa_ref, ga_ref, wa_ref, dpa_ref, dga_ref),
                                                    (ob_ref, gb_ref, wb_ref, dpb_ref, dgb_ref),
                                                    (oc_ref, gc_ref, wc_ref, dpc_ref, dgc_ref)):
            s = jax.nn.sigmoid(g_ref[...])
            proj = jnp.dot(o_ref[...], w_ref[...], preferred_element_type=F32)
            dp_ref[...] = (dyv * s).astype(BF16)
            dg_ref[...] = (dyv * proj * (s * (1.0 - s))).astype(BF16)

    return pl.pallas_call(
        body, name=name, grid=(T // TR, D // tc), in_specs=[out] + _merge_specs(z, D, TR, tc, wa, wb, wc),
        out_specs=[out] * 6, out_shape=[_sds((T, D), BF16)] * 6, compiler_params=_params(("parallel", "parallel")),
    )(dy, oa, ob, oc, z, z, z, wa.arr, wb.arr, wc.arr)


def _silu_rows(cc, name):
    def body(c_ref, s_ref):
        v = c_ref[...]
        s_ref[...] = (v * jax.nn.sigmoid(v)).astype(BF16)
    return pl.pallas_call(body, name=name, out_shape=_sds(cc.shape, BF16))(cc)


def _ada_bwd_rows(dm, ds, cc, name):
    def body(dm_ref, ds_ref, c_ref, db_ref, dc_ref):
        db_ref[...] = jnp.sum(dm_ref[...], axis=0, keepdims=True)
        v = c_ref[...]
        s = jax.nn.sigmoid(v)
        dc_ref[...] = ds_ref[...] * (s * (1.0 + v * (1.0 - s)))
    return pl.pallas_call(body, name=name, out_shape=[_sds((1, dm.shape[1]), F32), _sds(cc.shape, F32)])(dm, ds, cc)


def _row_tile(rows, cols):
    for t in (512, 256, 128, 64, 32, 16, 8):
        if rows % t == 0 and t * cols * 4 <= (1 << 20):
            return t
    return rows


def _add_own_layer(layers, landed, core, name):
    R, C = landed.shape
    tr = _row_tile(R, C)
    n_layers = len(layers)

    def body(c_ref, *refs):
        b_ref, o_ref, o16_ref = refs[n_layers:]
        for l in range(n_layers):
            @pl.when(c_ref[0] == l)
            def _(a_ref=refs[l]):
                tot = a_ref[...] + b_ref[...].astype(F32)
                o_ref[...] = tot
                o16_ref[...] = tot.astype(BF16)

    row = pl.BlockSpec((tr, C), lambda i, c: (i, 0))
    own = [pl.BlockSpec((tr, C), functools.partial(lambda i, c, l: (jnp.where(c[0] == l, i, 0), 0), l=l))
           for l in range(n_layers)]
    grid_spec = pltpu.PrefetchScalarGridSpec(num_scalar_prefetch=1, grid=(R // tr,),
                                             in_specs=own + [row], out_specs=[row, row])
    return pl.pallas_call(body, name=name, grid_spec=grid_spec, out_shape=[_sds((R, C), F32), _sds((R, C), BF16)],
                          compiler_params=_params(("arbitrary",)))(core, *layers, landed)


def _sum_chips(own, landed, chip, name):
    _, R, C = own.shape
    tr = _row_tile(R, C)

    def body(k_ref, a_ref, b_ref, o_ref):
        o_ref[...] = ((a_ref[...] + b_ref[0].astype(F32)) + b_ref[1].astype(F32)) + b_ref[2].astype(F32)

    grid_spec = pltpu.PrefetchScalarGridSpec(
        num_scalar_prefetch=1, grid=(R // tr,),
        in_specs=[pl.BlockSpec((None, tr, C), lambda i, k: (k[0], i, 0)), pl.BlockSpec((3, tr, C), lambda i, k: (0, i, 0))],
        out_specs=pl.BlockSpec((tr, C), lambda i, k: (i, 0)))
    return pl.pallas_call(body, name=name, grid_spec=grid_spec, out_shape=_sds((R, C), F32),
                          compiler_params=_params(("parallel",)))(chip, own, landed)


def _adam_math(w, g, m, v):
    m = ADAM_B1 * m + (1.0 - ADAM_B1) * g
    v = ADAM_B2 * v + (1.0 - ADAM_B2) * (g * g)
    m_hat = m / (1.0 - ADAM_B1 ** ADAM_STEP)
    v_hat = v / (1.0 - ADAM_B2 ** ADAM_STEP)
    delta = -ADAM_LR * (m_hat / (jnp.sqrt(v_hat) + ADAM_EPS) + ADAM_WD * w)
    return delta, m, v


def _adamw(w, mine, other, m, v, core, name):
    L, R, C = w.shape
    tr = _row_tile(R, C)

    def body(c_ref, w_ref, a_ref, b_ref, m_ref, v_ref, g_ref, d_ref, mo_ref, vo_ref):
        def step(g):
            d, mn, vn = _adam_math(w_ref[...], g, m_ref[...], v_ref[...])
            g_ref[...] = g
            d_ref[...] = d
            mo_ref[...] = mn
            vo_ref[...] = vn

        @pl.when(pl.program_id(0) == c_ref[0])
        def _():
            step(a_ref[...])

        @pl.when(pl.program_id(0) != c_ref[0])
        def _():
            step(b_ref[...])

    lay = pl.BlockSpec((None, tr, C), lambda l, i, c: (l, i, 0))
    row = pl.BlockSpec((tr, C), lambda l, i, c: (i, 0))
    grid_spec = pltpu.PrefetchScalarGridSpec(num_scalar_prefetch=1, grid=(L, R // tr),
                                             in_specs=[lay, row, row, lay, lay], out_specs=[lay] * 4)
    return pl.pallas_call(body, name=name, grid_spec=grid_spec, out_shape=[_sds((L, R, C), F32)] * 4,
                          compiler_params=_params(("parallel", "parallel")))(core, w, mine, other, m, v)


def _adamw_small(w, parts, m, v, name):
    R, C = w.shape

    def body(w_ref, p_ref, m_ref, v_ref, g_ref, d_ref, mo_ref, vo_ref):
        g = p_ref[0]
        for dev in range(1, 8):
            g = g + p_ref[dev]
        d, mn, vn = _adam_math(w_ref[...], g, m_ref[...], v_ref[...])
        g_ref[...] = g
        d_ref[...] = d
        mo_ref[...] = mn
        vo_ref[...] = vn

    return pl.pallas_call(body, name=name, out_shape=[_sds((R, C), F32)] * 4)(w, parts, m, v)


def _place():
    return lax.axis_index("x"), lax.axis_index("y"), lax.axis_index("c")


def _other_chips(x, y):
    return [(1 - x, y), (x, 1 - y), (1 - x, 1 - y)]


def _rcopy(src, dst, ssem, rsem, dev):
    return pltpu.make_async_remote_copy(src_ref=src, dst_ref=dst, send_sem=ssem, recv_sem=rsem,
                                        device_id=dev, device_id_type=MESH)


GATHER_SEMS = 7


class _LayerGather:
    def __init__(self, shards, layer):
        self.shards, self.layer, self.n = shards, layer, len(shards)
        load, self.groups = [0, 0], ([], [])
        for w in sorted(range(self.n), key=lambda w: -shards[w][0].size):
            g = 0 if load[0] <= load[1] else 1
            self.groups[g].append(w)
            load[g] += shards[w][0].size
        self.out_shape = [_sds((N_CHIPS,) + s.shape[1:], s.dtype) for s in shards]
        self.scratch = [pltpu.SemaphoreType.DMA((self.n, GATHER_SEMS)), pltpu.SemaphoreType.DMA((self.n, GATHER_SEMS))]

    def _own(self, src, out, send_sems, recv_sems):
        x, y, c = _place()
        return [_rcopy(src[w].at[self.layer], out[w].at[2 * x + y], send_sems.at[w, 6], recv_sems.at[w, 6], (x, y, 1 - c))
                for w in range(self.n)]

    def _to_chips(self, g, src, out, send_sems, recv_sems):
        x, y, c = _place()
        return [_rcopy(src[w].at[self.layer], out[w].at[2 * x + y], send_sems.at[w, j], recv_sems.at[w, j], (*chip, c))
                for w in self.groups[g] for j, chip in enumerate(_other_chips(x, y))]

    def start(self, src, out, send_sems, recv_sems):
        c = lax.axis_index("c")
        for cp in self._own(src, out, send_sems, recv_sems):
            cp.start()
        for g in (0, 1):
            @pl.when(c == g)
            def _(g=g):
                for cp in self._to_chips(g, src, out, send_sems, recv_sems):
                    cp.start()

    def finish(self, src, out, send_sems, recv_sems):
        x, y, c = _place()
        sibling = (x, y, 1 - c)
        chips = _other_chips(x, y)
        for g in (0, 1):
            @pl.when(c == g)
            def _(g=g):
                passed = []
                for w in self.groups[g]:
                    for j, (px, py) in enumerate(chips):
                        landed = out[w].at[2 * px + py]
                        _rcopy(landed, landed, send_sems.at[w, j], recv_sems.at[w, j], (px, py, c)).wait_recv()
                        cp = _rcopy(landed, landed, send_sems.at[w, 3 + j], recv_sems.at[w, 3 + j], sibling)
                        cp.start()
                        passed.append(cp)
                for w in self.groups[1 - g]:
                    for j, (px, py) in enumerate(chips):
                        landed = out[w].at[2 * px + py]
                        _rcopy(landed, landed, send_sems.at[w, 3 + j], recv_sems.at[w, 3 + j], sibling).wait_recv()
                for cp in self._to_chips(g, src, out, send_sems, recv_sems) + passed:
                    cp.wait_send()
        for cp in self._own(src, out, send_sems, recv_sems):
            cp.wait_recv()
            cp.wait_send()

    def alone(self, name):
        n = self.n

        def body(*refs):
            args = (refs[:n], refs[n:2 * n], refs[2 * n], refs[2 * n + 1])
            self.start(*args)
            self.finish(*args)

        return pl.pallas_call(body, name=name, in_specs=[ANY] * n, out_specs=[ANY] * n, out_shape=self.out_shape,
                              scratch_shapes=self.scratch)(*self.shards)


def _send_other_layer(layer0, layer1, name):
    n = len(layer0)

    def body(*refs):
        src0, src1, out = refs[:n], refs[n:2 * n], refs[2 * n:3 * n]
        send_sems, recv_sems = refs[3 * n:]
        x, y, c = _place()

        def copies(src):
            return [_rcopy(src[w], out[w], send_sems.at[w], recv_sems.at[w], (x, y, 1 - c)) for w in range(n)]

        @pl.when(c == 0)
        def _():
            for cp in copies(src1):
                cp.start()

        @pl.when(c == 1)
        def _():
            for cp in copies(src0):
                cp.start()

        for cp in copies(src0):
            cp.wait_recv()
        for cp in copies(src0):
            cp.wait_send()

    return pl.pallas_call(
        body, name=name, in_specs=[ANY] * (2 * n), out_specs=[ANY] * n,
        out_shape=[_sds(s.shape, s.dtype) for s in layer0],
        scratch_shapes=[pltpu.SemaphoreType.DMA((n,)), pltpu.SemaphoreType.DMA((n,))],
    )(*layer0, *layer1)


def _send_chip_blocks(blocked, name):
    n = len(blocked)

    def body(*refs):
        src, out = refs[:n], refs[n:2 * n]
        send_sems, recv_sems = refs[2 * n:]
        x, y, c = _place()
        cps = [_rcopy(src[w].at[2 * px + py], out[w].at[j], send_sems.at[w, j], recv_sems.at[w, j], (px, py, c))
               for w in range(n) for j, (px, py) in enumerate(_other_chips(x, y))]
        for cp in cps:
            cp.start()
        for cp in cps:
            cp.wait_recv()
        for cp in cps:
            cp.wait_send()

    return pl.pallas_call(
        body, name=name, in_specs=[ANY] * n, out_specs=[ANY] * n,
        out_shape=[_sds((3,) + s.shape[1:], s.dtype) for s in blocked],
        scratch_shapes=[pltpu.SemaphoreType.DMA((n, 3)), pltpu.SemaphoreType.DMA((n, 3))],
    )(*blocked)


def _share_layers(reduced, name):
    n = len(reduced)

    def body(*refs):
        src, out = refs[:n], refs[n:2 * n]
        send_sems, recv_sems = refs[2 * n:]
        x, y, c = _place()
        cps = [_rcopy(src[w], out[w], send_sems.at[w], recv_sems.at[w], (x, y, 1 - c)) for w in range(n)]
        for cp in cps:
            cp.start()
        for cp in cps:
            cp.wait_recv()
        for cp in cps:
            cp.wait_send()

    return pl.pallas_call(
        body, name=name, in_specs=[ANY] * n, out_specs=[ANY] * n,
        out_shape=[_sds(s.shape, s.dtype) for s in reduced],
        scratch_shapes=[pltpu.SemaphoreType.DMA((n,)), pltpu.SemaphoreType.DMA((n,))],
    )(*reduced)


def _gather_small(block, name):
    m_per, n = block.shape

    def body(x_ref, out_ref, send_sems, recv_sems, local_sem):
        x, y, c = _place()
        me, sibling = (x, y, c), (x, y, 1 - c)
        chips = _other_chips(x, y)

        def rows(px, py, pc):
            return out_ref.at[pl.ds((4 * px + 2 * py + pc) * m_per, m_per), :]

        def copy(k, blk, to, src=None):
            return _rcopy(rows(*blk) if src is None else src, rows(*blk), send_sems.at[k], recv_sems.at[k], to)

        mine = pltpu.make_async_copy(x_ref, rows(*me), local_sem)
        mine.start()
        first = [copy(0, me, sibling, src=x_ref)]
        first += [copy(1 + j, me, (*chip, c), src=x_ref) for j, chip in enumerate(chips)]
        for cp in first:
            cp.start()
        passed = [copy(4 + j, (*chip, c), sibling) for j, chip in enumerate(chips)]
        for j, chip in enumerate(chips):
            copy(1 + j, (*chip, c), me).wait_recv()
            passed[j].start()
        copy(0, sibling, me).wait_recv()
        for j, chip in enumerate(chips):
            copy(4 + j, (*chip, 1 - c), me).wait_recv()
        for cp in first + passed:
            cp.wait_send()
        mine.wait()

    return pl.pallas_call(
        body, name=name, out_shape=_sds((8 * m_per, n), block.dtype),
        in_specs=[pl.BlockSpec(memory_space=pltpu.VMEM)], out_specs=pl.BlockSpec(memory_space=pltpu.VMEM),
        scratch_shapes=[pltpu.SemaphoreType.DMA((7,)), pltpu.SemaphoreType.DMA((7,)), pltpu.SemaphoreType.DMA],
    )(block)


def _rope_tables(n_ctx, seq):
    rows = seq // GRID_W
    r = jnp.repeat(jnp.arange(rows, dtype=F32), GRID_W)
    col = jnp.tile(jnp.arange(GRID_W, dtype=F32), rows)
    inv = 1.0 / (ROPE_THETA ** (jnp.arange(0, AXIS_DIM, 2, dtype=F32) / AXIS_DIM))
    ang = jnp.concatenate([r[:, None] * inv, col[:, None] * inv], axis=-1)
    cos = jnp.repeat(jnp.cos(ang), 2, axis=-1)
    sin = jnp.repeat(jnp.sin(ang), 2, axis=-1) * jnp.tile(jnp.array([-1.0, 1.0], F32), HEAD_DIM // 2)
    cos = jnp.concatenate([jnp.ones((n_ctx, HEAD_DIM), F32), cos], axis=0)
    sin = jnp.concatenate([jnp.zeros((n_ctx, HEAD_DIM), F32), sin], axis=0)
    return jnp.tile(cos, (1, 2)), jnp.tile(sin, (1, 2))


def _block_diag(w_pool):
    L, G = w_pool.shape[:2]
    eye = jnp.eye(G, dtype=w_pool.dtype)
    return (w_pool[:, :, :, None, :] * eye[None, :, None, :, None]).reshape(L, POOL_WIDTH, POOL_WIDTH)


def _qk_gains(small):
    qn = jnp.stack([small["q_norm_a"], small["q_norm_c"]], axis=1)[:, :, None, :]
    kn = jnp.stack([small["k_norm_a"], small["k_norm_c"]], axis=1)[:, :, None, :]
    L = qn.shape[0]
    rows = jnp.concatenate([jnp.broadcast_to(qn, (L, 2, N_HEADS, HEAD_DIM)), jnp.broadcast_to(kn, (L, 2, N_KV, HEAD_DIM)),
                            jnp.ones((L, 2, N_KV, HEAD_DIM), F32)], axis=2)
    return rows.reshape(L, 2, 1, QKV_WIDTH)


def _local_step(x, c, ctx, c_ctx, small, gw, target, rider=None):
    gw = list(gw)
    B, S, D = x.shape
    N = ctx.shape[1]
    L = small["norm1"].shape[0]
    Tp = N + S
    T = B * Tp
    TR = N
    P = Tp // N
    rows16 = 16
    assert N % Q_BLOCK == 0 and S % N == 0 and B + 1 <= rows16
    TM = _tile(T, (1024, 768, 512, 384, 256, 128))
    TMG = _tile(T, (512, 384, 256, 128))

    X = jnp.concatenate([ctx, x], axis=1).reshape(T, D)
    cc = jnp.concatenate([c, c_ctx[None], jnp.zeros((rows16 - B - 1, D), F32)], axis=0)
    s_rows = _silu_rows(cc, "silu_rows")
    cos, sin = _rope_tables(N, S)
    all_gains = _qk_gains(small)
    all_w_bd = _block_diag(small["w_pool"]).astype(BF16)

    def weights(l):
        g = gw[l]
        return dict(
            ada=_Opnd(g["w_ada"], "bcols"), w_in=_Opnd(g["w_in"], "bcols"),
            a=_Opnd(g["w_br_a"], "bcols"), b=_Opnd(g["w_br_b"], "bcols"), c=_Opnd(g["w_br_c"], "bcols"),
            out=_Opnd(g["w_out"], "brows"), mlp1=_Opnd(g["w_mlp1"], "bcols"), mlp2=_Opnd(g["w_mlp2"], "brows"))

    IN = weights(0)["w_in"].shape[1]
    DFF = weights(0)["mlp1"].shape[1]
    tn_in = _tile(IN // N_CHIPS, (1152, 768, 512, 384, 256, 128))
    tn_ff = _tile(DFF // N_CHIPS, (1024, 512, 256, 128))
    tn_ada = _tile(6 * D // N_CHIPS, (1536, 768, 512, 256, 128))
    tn_d = D // N_CHIPS
    tk_d = _tile(D, (512,))
    tk_tok = _tile(T, (2304, 1536, 1024, 768, 512, 384, 256))

    saved = []
    xin, pending = X, None
    for l in range(L):
        W = weights(l)
        b_ada = small["b_ada"][l].reshape(1, 6 * D)
        mod = _matmul(s_rows, W["ada"], "nn", tm=rows16, tn=tn_ada, tk=D, name=f"ada_fwd{l}",
                      epilogue=lambda acc, b: (acc + b,), extras=[(b_ada, (1, tn_ada), lambda m, n: (0, n))])
        modtab = jnp.stack([jnp.broadcast_to(mod[B], (B, 6 * D)), mod[:B]], axis=1).reshape(2 * B, 1, 6 * D)
        gains = all_gains[l]
        w_bd = all_w_bd[l]
        p_scale = small["pool_scale"][l].reshape(1, POOL_WIDTH)
        sink = small["sink_c"][l]

        x0, h1 = _res_norm(xin, pending, modtab, 0, 1, small["norm1"][l][None], TR=TR, P=P, name=f"norm1_fwd{l}")
        z = _matmul(h1, W["w_in"], "nn", tm=TM, tn=tn_in, tk=D, name=f"in_proj{l}")
        q2, k2, v2 = _qk_prep(z, gains, cos, sin, TR=TR, P=P, name=f"qk_prep{l}")
        riding = rider if l == 0 else None
        oa, oa32, lse_a, *landed = _attn_fwd(q2, k2, v2, None, branch=0, B=B, n_ctx=N, window=False,
                                             name=f"attn_a_fwd{l}", rider=riding)
        if riding is not None:
            gw[riding.layer] = dict(zip(BIG_NAMES, landed))
        oc, oc32, lse_c = _attn_fwd(q2, k2, v2, sink, branch=1, B=B, n_ctx=N, window=True, name=f"attn_c_fwd{l}")
        pooled, ob = _pool_fwd(z, w_bd, p_scale, B=B, Tp=Tp, n_ctx=N, name=f"pool_fwd{l}")
        y = _merge_fwd(oa, ob, oc, z, W["a"], W["b"], W["c"], D=D, TR=TMG, name=f"merge_fwd{l}")
        ao = _matmul(y, W["out"], "nn", tm=TM, tn=D, tk=tn_d, name=f"out_proj{l}")
        x1, h2 = _res_norm(x0, (ao, modtab, 2), modtab, 3, 4, small["norm2"][l][None], TR=TR, P=P, name=f"norm2_fwd{l}")
        a_pre, r_act = _matmul(h2, W["mlp1"], "nn", tm=TM, tn=tn_ff, tk=D, name=f"mlp1_fwd{l}", out_dtypes=(F32, BF16),
                               epilogue=lambda acc: (acc, jnp.square(jnp.maximum(acc, 0.0))))
        mo = _matmul(r_act, W["mlp2"], "nn", tm=TM, tn=D, tk=tn_ff, name=f"mlp2_fwd{l}")
        saved.append(dict(modtab=modtab, gains=gains, w_bd=w_bd, p_scale=p_scale, sink=sink, x0=x0, h1=h1, z=z,
                          q2=q2, k2=k2, v2=v2, oa=oa, ob=ob, oc=oc, oa32=oa32, oc32=oc32, lse_a=lse_a, lse_c=lse_c,
                          pooled=pooled, y=y, ao=ao,
                          x1=x1, h2=h2, a_pre=a_pre, r_act=r_act, mo=mo))
        xin, pending = x1, (mo, modtab, 5)

    dxo, loss, d_mo, dg2 = _loss_head(xin, pending[0], pending[1], 5, target.reshape(B * S, D), TR=TR, P=P, name="loss_head")

    big = {k: [None] * L for k in BIG_NAMES}
    big16 = {k: [None] * L for k in BIG_NAMES}
    sm = {k: [None] * L for k in ("b_ada", "norm1", "norm2", "q_norm_a", "k_norm_a", "q_norm_c", "k_norm_c",
                                   "sink_c", "w_pool", "pool_scale")}

    def dw(key, l, a, b, *, tm, tn, name, tk=tk_tok, blocked=True):
        outs = _matmul(a, b, "tn", tm=tm, tn=tn, tk=tk, name=name, out_dtypes=(F32, BF16), out_blocked=blocked)
        if not blocked:
            outs = [o.reshape(N_CHIPS, o.shape[0] // N_CHIPS, o.shape[1]) for o in outs]
        big[key][l], big16[key][l] = outs
    d_cctx = jnp.zeros((D,), F32)
    for l in reversed(range(L)):
        W, sv = weights(l), saved[l]
        modtab = sv["modtab"]
        d_a = _matmul(d_mo, W["mlp2"], "nt", tm=TM, tn=tn_ff, tk=D, name=f"mlp2_bwd{l}", out_dtypes=(BF16,),
                      epilogue=lambda acc, a: (acc * (2.0 * jnp.maximum(a, 0.0)),),
                      extras=[(sv["a_pre"], (TM, tn_ff), lambda m, n: (m, n))])
        dw("w_mlp2", l, sv["r_act"], d_mo, tm=tk_d, tn=D, name=f"mlp2_dw{l}", blocked=False)
        d_h2 = _matmul(d_a, W["mlp1"], "nt", tm=TM, tn=D, tk=tn_ff, name=f"mlp1_bwd{l}")
        dw("w_mlp1", l, sv["h2"], d_a, tm=tk_d, tn=tn_ff, name=f"mlp1_dw{l}")
        dx1, dsh2, dsc2, dn2, d_ao, dg1 = _norm_bwd(sv["x1"], d_h2, dxo, modtab, 4, small["norm2"][l][None],
                                                    (sv["ao"], modtab, 2), TR=TR, P=P, name=f"norm2_bwd{l}")
        d_y = _matmul(d_ao, W["out"], "nt", tm=TM, tn=tn_d, tk=D, name=f"out_bwd{l}")
        dw("w_out", l, sv["y"], d_ao, tm=tk_d, tn=D, name=f"out_dw{l}", blocked=False)
        d_pa, d_pb, d_pc, d_ga, d_gb, d_gc = _merge_bwd(d_y, sv["oa"], sv["ob"], sv["oc"], sv["z"], W["a"], W["b"], W["c"],
                                                        D=D, TR=TMG, name=f"merge_bwd{l}")
        d_oa = _matmul(d_pa, W["a"], "nt", tm=TM, tn=Q_WIDTH, tk=tn_d, name=f"br_a_bwd{l}", out_dtypes=(BF16,))
        d_ob = _matmul(d_pb, W["b"], "nt", tm=TM, tn=POOL_WIDTH, tk=tn_d, name=f"br_b_bwd{l}")
        d_oc = _matmul(d_pc, W["c"], "nt", tm=TM, tn=Q_WIDTH, tk=tn_d, name=f"br_c_bwd{l}", out_dtypes=(BF16,))
        dw("w_br_a", l, sv["oa"], d_pa, tm=Q_WIDTH, tn=tn_d, name=f"br_a_dw{l}")
        dw("w_br_b", l, sv["ob"], d_pb, tm=POOL_WIDTH, tn=tn_d, name=f"br_b_dw{l}")
        dw("w_br_c", l, sv["oc"], d_pc, tm=Q_WIDTH, tn=tn_d, name=f"br_c_dw{l}")
        d_u, d_wbd, d_ps = _pool_bwd(d_ob, sv["pooled"], sv["w_bd"], sv["p_scale"], B=B, Tp=Tp, n_ctx=N, name=f"pool_bwd{l}")
        dqa, dka, dva = _attn_bwd(sv["q2"], sv["k2"], sv["v2"], d_oa, sv["oa32"], sv["lse_a"], None, branch=0, B=B,
                                  n_ctx=N, window=False, name=f"attn_a_bwd{l}")
        dqc, dkc, dvc, dsink = _attn_bwd(sv["q2"], sv["k2"], sv["v2"], d_oc, sv["oc32"], sv["lse_c"], sv["sink"],
                                         branch=1, B=B, n_ctx=N, window=True, name=f"attn_c_bwd{l}")
        dz_a, dgains_a = _qk_prep_bwd(sv["z"], dqa, dka, dva, sv["gains"], cos, sin, branch=0, TR=TR, P=P,
                                      name=f"qk_prep_a_bwd{l}")
        dz_c, dgains_c = _qk_prep_bwd(sv["z"], dqc, dkc, dvc, sv["gains"], cos, sin, branch=1, TR=TR, P=P,
                                      name=f"qk_prep_c_bwd{l}")
        dz = jnp.concatenate([dz_a, dz_c, d_u, d_ga, d_gb, d_gc], axis=1)
        d_h1 = _matmul(dz, W["w_in"], "nt", tm=TM, tn=D, tk=tn_in, name=f"in_bwd{l}")
        dw("w_in", l, sv["h1"], dz, tm=tk_d, tn=tn_in, name=f"in_dw{l}")
        below = (saved[l - 1]["mo"], saved[l - 1]["modtab"], 5) if l > 0 else None
        dx0, dsh1, dsc1, dn1, *lower = _norm_bwd(sv["x0"], d_h1, dx1, modtab, 1, small["norm1"][l][None], below,
                                                 TR=TR, P=P, name=f"norm1_bwd{l}")
        this_dg2 = dg2
        if l > 0:
            d_mo, dg2 = lower

        dm_groups = jnp.concatenate([dsh1, dsc1, dg1, dsh2, dsc2, this_dg2], axis=-1).reshape(B, 2, 6 * D)
        dm = jnp.concatenate([dm_groups[:, 1], jnp.sum(dm_groups[:, 0], axis=0, keepdims=True),
                              jnp.zeros((rows16 - B - 1, 6 * D), F32)], axis=0)
        dm_bf = dm.astype(BF16)
        d_s = _matmul(dm_bf, W["ada"], "nt", tm=rows16, tn=D, tk=tn_ada, name=f"ada_bwd{l}")
        dw("w_ada", l, s_rows, dm_bf, tm=tk_d, tn=tn_ada, tk=rows16, name=f"ada_dw{l}")
        db_ada, dcc = _ada_bwd_rows(dm, d_s, cc, f"ada_rows_bwd{l}")
        d_cctx = d_cctx + dcc[B]

        sm["b_ada"][l] = db_ada[0]
        sm["norm1"][l] = jnp.sum(dn1, axis=(0, 1))
        sm["norm2"][l] = jnp.sum(dn2, axis=(0, 1))
        dgh = jnp.stack([dgains_a, dgains_c]).reshape(2, QKV_WIDTH // HEAD_DIM, HEAD_DIM)
        sm["q_norm_a"][l] = jnp.sum(dgh[0, :N_HEADS], axis=0)
        sm["k_norm_a"][l] = jnp.sum(dgh[0, N_HEADS:N_HEADS + N_KV], axis=0)
        sm["q_norm_c"][l] = jnp.sum(dgh[1, :N_HEADS], axis=0)
        sm["k_norm_c"][l] = jnp.sum(dgh[1, N_HEADS:N_HEADS + N_KV], axis=0)
        sm["sink_c"][l] = jnp.sum(dsink[:, :N_HEADS, 0], axis=0)
        sm["w_pool"][l] = jnp.stack([d_wbd[g * POOL_CH:(g + 1) * POOL_CH, g * POOL_CH:(g + 1) * POOL_CH]
                                     for g in range(POOL_WIDTH // POOL_CH)])
        sm["pool_scale"][l] = d_ps[0]
        dxo = dx0

    grad_x = dxo.reshape(B, Tp, D)[:, N:]
    small_grads = {k: jnp.stack(v) for k, v in sm.items()}
    small_grads["c_ctx"] = d_cctx
    return loss, grad_x, small_grads, big, big16


SMALL_NAMES = ("c_ctx", "b_ada", "norm1", "norm2", "q_norm_a", "k_norm_a", "q_norm_c", "k_norm_c", "sink_c",
               "w_pool", "pool_scale")
BIG_NAMES = ("w_ada", "w_in", "w_br_a", "w_br_b", "w_br_c", "w_out", "w_mlp1", "w_mlp2")
WEIGHT_NAMES = ("c_ctx", "w_ada", "b_ada", "norm1", "norm2", "w_in", "q_norm_a", "k_norm_a", "q_norm_c", "k_norm_c",
                "sink_c", "w_pool", "pool_scale", "w_br_a", "w_br_b", "w_br_c", "w_out", "w_mlp1", "w_mlp2")


def _pack(parts, rows):
    flat = jnp.concatenate([p.reshape(-1).astype(F32) for p in parts])
    return jnp.pad(flat, (0, rows * LANES - flat.shape[0])).reshape(rows, LANES)


def _unpack(packed, like):
    flat, out, at = packed.reshape(-1), [], 0
    for p in like:
        out.append(flat[at:at + p.size].reshape(p.shape))
        at += p.size
    return out


def _reduce_big(partials, partials16):
    names = list(partials)
    assert all(len(partials[k]) == 2 for k in names)
    x, y, c = _place()
    core = c.astype(jnp.int32).reshape(1)
    chip = (2 * x + y).astype(jnp.int32).reshape(1)
    shapes = [partials[k][0].shape for k in names]
    flat = [[g.reshape(-1, g.shape[-1]) for g in partials[k]] for k in names]
    flat16 = [[g.reshape(-1, g.shape[-1]) for g in partials16[k]] for k in names]
    landed = _send_other_layer([f[0] for f in flat16], [f[1] for f in flat16], "grads_to_sibling")
    in_chip = [_add_own_layer(f, r, core, f"grads_add_sibling_{k}") for k, f, r in zip(names, flat, landed)]
    blocked = [h.reshape(s) for s, (h, _) in zip(shapes, in_chip)]
    blocked16 = [h.reshape(s) for s, (_, h) in zip(shapes, in_chip)]
    from_chips = _send_chip_blocks(blocked16, "grads_to_chips")
    reduced = [_sum_chips(h, r, chip, f"grads_sum_chips_{k}") for k, h, r in zip(names, blocked, from_chips)]
    shared = _share_layers(reduced, "grads_share_layers")
    return core, dict(zip(names, zip(reduced, shared)))


def kernel(x, c, ctx, c_ctx, w_ada, b_ada, norm1, norm2, w_in, q_norm_a, k_norm_a, q_norm_c, k_norm_c, sink_c, w_pool, pool_scale, w_br_a, w_br_b, w_br_c, w_out, w_mlp1, w_mlp2, loss_target, m_c_ctx, m_w_ada, m_b_ada, m_norm1, m_norm2, m_w_in, m_q_norm_a, m_k_norm_a, m_q_norm_c, m_k_norm_c, m_sink_c, m_w_pool, m_pool_scale, m_w_br_a, m_w_br_b, m_w_br_c, m_w_out, m_w_mlp1, m_w_mlp2, v_c_ctx, v_w_ada, v_b_ada, v_norm1, v_norm2, v_w_in, v_q_norm_a, v_k_norm_a, v_q_norm_c, v_k_norm_c, v_sink_c, v_w_pool, v_pool_scale, v_w_br_a, v_w_br_b, v_w_br_c, v_w_out, v_w_mlp1, v_w_mlp2):
    given = dict(locals())
    w = {k: given[k] for k in WEIGHT_NAMES}
    m = {k: given["m_" + k] for k in WEIGHT_NAMES}
    v = {k: given["v_" + k] for k in WEIGHT_NAMES}

    shards = [w[k].astype(BF16) for k in BIG_NAMES]
    assert all(s.shape[0] == 2 for s in shards)
    first_layer = dict(zip(BIG_NAMES, _LayerGather(shards, 0).alone("gather_weights0")))
    small = {k: w[k] for k in SMALL_NAMES}
    loss_part, grad_x, small_grads, big_grads, big_grads16 = _local_step(
        x, c, ctx, c_ctx, small, [first_layer, None], loss_target, rider=_LayerGather(shards, 1))

    core, reduced = _reduce_big({k: big_grads[k] for k in BIG_NAMES}, {k: big_grads16[k] for k in BIG_NAMES})
    grads, deltas, new_m, new_v = {}, {}, {}, {}
    for k in BIG_NAMES:
        mine, other = reduced[k]
        grads[k], deltas[k], new_m[k], new_v[k] = _adamw(w[k], mine, other, m[k], v[k], core, f"adamw_{k}")

    sizes = sum(w[k].size for k in SMALL_NAMES) + LANES
    rows = -(-sizes // (8 * LANES)) * 8
    parts = _gather_small(_pack([small_grads[k] for k in SMALL_NAMES] + [loss_part[0]], rows), "gather_small")
    zero = jnp.zeros((LANES,), F32)
    packed = [_pack([t[k] for k in SMALL_NAMES] + [zero], rows) for t in (w, m, v)]
    outs = _adamw_small(packed[0], parts.reshape(8, rows, LANES), packed[1], packed[2], "adamw_small")
    like = [w[k] for k in SMALL_NAMES] + [zero]
    for store, packed_out in zip((grads, deltas, new_m, new_v), outs):
        pieces = _unpack(packed_out, like)
        for k, piece in zip(SMALL_NAMES, pieces):
            store[k] = piece
        if store is grads:
            loss = pieces[-1][0]

    return (loss, grad_x, *[grads[k] for k in WEIGHT_NAMES], *[deltas[k] for k in WEIGHT_NAMES],
            *[new_m[k] for k in WEIGHT_NAMES], *[new_v[k] for k in WEIGHT_NAMES])
```

```python
import functools

import jax
import jax.numpy as jnp
from jax import lax
from jax.experimental import pallas as pl
from jax.experimental.pallas import tpu as pltpu

F32 = jnp.float32
BF16 = jnp.bfloat16

HEAD_DIM = 64
GRID_W = 64
AXIS_DIM = HEAD_DIM // 2
ROPE_THETA = 10000.0
N_HEADS = 6
N_KV = 2
N_GROUP = N_HEADS // N_KV
POOL_CH = 64
POOL_WIDTH = 256
POOL_WINDOWS = (2, 4, 8, 16)
WINDOW = 128
Q_BLOCK = 128
Q_WIDTH = N_HEADS * HEAD_DIM
KV_WIDTH = N_KV * HEAD_DIM
GATE_COL = 2 * (Q_WIDTH + 2 * KV_WIDTH) + POOL_WIDTH
U_COL = 2 * (Q_WIDTH + 2 * KV_WIDTH)
EPS = 1e-6
NEG = -1e30
ADAM_LR = 0.001
ADAM_B1 = 0.9
ADAM_B2 = 0.999
ADAM_EPS = 1e-08
ADAM_WD = 0.01
ADAM_STEP = 10

N_CHIPS = 4
LANES = 128
POOL_PAD = 16
VMEM_LIMIT = 48 * 1024 * 1024
MESH = pl.DeviceIdType.MESH
ANY = pl.BlockSpec(memory_space=pl.ANY)


def _params(sem):
    return pltpu.CompilerParams(dimension_semantics=sem, vmem_limit_bytes=VMEM_LIMIT)


def _sds(shape, dtype):
    return jax.ShapeDtypeStruct(tuple(shape), dtype)


class _Opnd:
    def __init__(self, arr, kind="plain"):
        self.arr, self.kind = arr, kind

    @property
    def shape(self):
        a = self.arr
        if self.kind == "plain":
            return a.shape
        if self.kind == "bcols":
            return (a.shape[1], N_CHIPS * a.shape[2])
        return (N_CHIPS * a.shape[1], a.shape[2])

    def spec(self, tr, tc, fn):
        a = self.arr
        if self.kind == "plain":
            return pl.BlockSpec((tr, tc), lambda *g: fn(*g))
        if self.kind == "bcols":
            assert a.shape[2] % tc == 0, (a.shape, tc)
            per = a.shape[2] // tc

            def im(*g):
                ri, ci = fn(*g)
                return (ci // per, ri, ci % per)
            return pl.BlockSpec((None, tr, tc), im)
        assert a.shape[1] % tr == 0, (a.shape, tr)
        per = a.shape[1] // tr

        def im(*g):
            ri, ci = fn(*g)
            return (ri // per, ri % per, ci)
        return pl.BlockSpec((None, tr, tc), im)


def _matmul(a, b, mode, *, tm, tn, tk, name, out_dtypes=(F32,), epilogue=None, extras=(), out_blocked=False, rider=None):
    if not isinstance(a, _Opnd):
        a = _Opnd(a)
    if not isinstance(b, _Opnd):
        b = _Opnd(b)
    if mode == "nn":
        (M, K), (K2, N) = a.shape, b.shape
        a_spec = a.spec(tm, tk, lambda m, n, k: (m, k))
        b_spec = b.spec(tk, tn, lambda m, n, k: (k, n))
        dims = (((1,), (0,)), ((), ()))
    elif mode == "nt":
        (M, K), (N, K2) = a.shape, b.shape
        a_spec = a.spec(tm, tk, lambda m, n, k: (m, k))
        b_spec = b.spec(tn, tk, lambda m, n, k: (n, k))
        dims = (((1,), (1,)), ((), ()))
    else:
        (K, M), (K2, N) = a.shape, b.shape
        a_spec = a.spec(tk, tm, lambda m, n, k: (k, m))
        b_spec = b.spec(tk, tn, lambda m, n, k: (k, n))
        dims = (((0,), (0,)), ((), ()))
    assert K == K2 and M % tm == 0 and N % tn == 0 and K % tk == 0, (name, M, N, K, K2, tm, tn, tk)
    nk = K // tk
    n_extra = len(extras)
    n_out = len(out_dtypes)
    extra_specs = [pl.BlockSpec(bs, functools.partial(lambda m, n, k, f: f(m, n), f=f)) for (_, bs, f) in extras]
    if out_blocked:
        assert (N // N_CHIPS) % tn == 0
        per = (N // N_CHIPS) // tn
        out_shape = [_sds((N_CHIPS, M, N // N_CHIPS), dt) for dt in out_dtypes]
        out_specs = [pl.BlockSpec((None, tm, tn), lambda m, n, k: (n // per, m, n % per)) for _ in out_dtypes]
    else:
        out_shape = [_sds((M, N), dt) for dt in out_dtypes]
        out_specs = [pl.BlockSpec((tm, tn), lambda m, n, k: (m, n)) for _ in out_dtypes]

    in_place = nk > 1 and epilogue is None and out_dtypes[0] == F32

    grid = (M // tm, N // tn, nk)
    own_scratch = [pltpu.VMEM((tm, tn), F32)] if nk > 1 and not in_place else []

    def body(*refs):
        refs, finish_ride = _ride(rider, refs, 2 + n_extra, n_out, len(own_scratch), grid)
        a_ref, b_ref = refs[0], refs[1]
        extra_refs = refs[2:2 + n_extra]
        out_refs = refs[2 + n_extra:2 + n_extra + n_out]
        acc_ref = out_refs[0] if in_place else (refs[2 + n_extra + n_out] if nk > 1 else None)
        k = pl.program_id(2)
        prod = lax.dot_general(a_ref[...].astype(BF16), b_ref[...].astype(BF16), dims, preferred_element_type=F32)

        def finish(acc):
            outs = epilogue(acc, *[r[...] for r in extra_refs]) if epilogue is not None else (acc,) * n_out
            for o_ref, o in zip(out_refs, outs):
                o_ref[...] = o.astype(o_ref.dtype)

        if nk == 1:
            finish(prod)
        elif in_place:
            @pl.when(k == 0)
            def _():
                acc_ref[...] = prod

            @pl.when(k > 0)
            def _():
                acc_ref[...] += prod

            if n_out > 1:
                @pl.when(k == nk - 1)
                def _():
                    for o_ref in out_refs[1:]:
                        o_ref[...] = acc_ref[...].astype(o_ref.dtype)
        else:
            @pl.when(k == 0)
            def _():
                acc_ref[...] = prod

            @pl.when(k > 0)
            def _():
                acc_ref[...] += prod

            @pl.when(k == nk - 1)
            def _():
                finish(acc_ref[...])

        finish_ride()

    ins, in_specs, out_specs, out_shape, scratch = _hitch(
        rider, [a.arr, b.arr] + [e[0] for e in extras], [a_spec, b_spec] + extra_specs, out_specs, out_shape, own_scratch)
    outs = pl.pallas_call(
        body, name=name, grid=grid, in_specs=in_specs, out_specs=out_specs, out_shape=out_shape, scratch_shapes=scratch,
        compiler_params=_params(("arbitrary",) * 3 if rider is not None else ("parallel", "parallel", "arbitrary")),
    )(*ins)
    if rider is not None:
        return (outs[0] if n_out == 1 else outs[:n_out]), outs[n_out:]
    return outs[0] if n_out == 1 else outs


def _tile(n, cands):
    for t in cands:
        if n % t == 0:
            return t
    return n


def _grp(i, P):
    return 2 * (i // P) + jnp.minimum(i % P, 1)


def _mod_spec(D, P, part):
    return pl.BlockSpec((1, 1, D), lambda i: (_grp(i, P), 0, part))


def _res_norm(x, pending, modtab, shift_part, scale_part, gain, *, TR, P, name):
    T, D = x.shape
    row = pl.BlockSpec((TR, D), lambda i: (i, 0))
    has_branch = pending is not None
    ins, specs = [x], [row]
    if has_branch:
        branch, gate_tab, gate_part = pending
        ins += [branch, gate_tab]
        specs += [row, _mod_spec(D, P, gate_part)]
    ins += [modtab, modtab, gain]
    specs += [_mod_spec(D, P, shift_part), _mod_spec(D, P, scale_part), pl.BlockSpec((1, D), lambda i: (0, 0))]

    def body(*refs):
        if has_branch:
            x_ref, br_ref, g_ref, sh_ref, sc_ref, gn_ref, xo_ref, h_ref = refs
            xv = x_ref[...] + g_ref[0] * br_ref[...]
        else:
            x_ref, sh_ref, sc_ref, gn_ref, xo_ref, h_ref = refs
            xv = x_ref[...]
        xo_ref[...] = xv
        y = xv * lax.rsqrt(jnp.mean(xv * xv, axis=-1, keepdims=True) + EPS) * gn_ref[...]
        h_ref[...] = (y * (1.0 + sc_ref[0]) + sh_ref[0]).astype(BF16)

    return pl.pallas_call(
        body, name=name, grid=(T // TR,), in_specs=specs, out_specs=[row, row],
        out_shape=[_sds((T, D), F32), _sds((T, D), BF16)], compiler_params=_params(("parallel",)),
    )(*ins)


def _norm_bwd(x, dh, dres, modtab, scale_part, gain, below, *, TR, P, name):
    T, D = x.shape
    G = modtab.shape[0]
    row = pl.BlockSpec((TR, D), lambda i: (i, 0))
    acc = pl.BlockSpec((1, 1, D), lambda i: (_grp(i, P), 0, 0))
    has_below = below is not None

    def body(*refs):
        if has_below:
            x_ref, dh_ref, dres_ref, sc_ref, gn_ref, br_ref, g_ref, dx_ref, dsh_ref, dsc_ref, dgn_ref, db_ref, dg_ref = refs
        else:
            x_ref, dh_ref, dres_ref, sc_ref, gn_ref, dx_ref, dsh_ref, dsc_ref, dgn_ref = refs
        r = pl.program_id(0) % P
        xv, dhv, gn = x_ref[...], dh_ref[...], gn_ref[...]
        rstd = lax.rsqrt(jnp.mean(xv * xv, axis=-1, keepdims=True) + EPS)
        xhat = xv * rstd
        dn = dhv * (1.0 + sc_ref[0])
        dxhat = dn * gn
        dxv = dres_ref[...] + rstd * (dxhat - xhat * jnp.mean(dxhat * xhat, axis=-1, keepdims=True))
        dx_ref[...] = dxv
        parts = [jnp.sum(dhv, axis=0, keepdims=True), jnp.sum(dhv * (xhat * gn), axis=0, keepdims=True),
                 jnp.sum(dn * xhat, axis=0, keepdims=True)]
        outs = [dsh_ref, dsc_ref, dgn_ref]
        if has_below:
            db_ref[...] = (dxv * g_ref[0]).astype(BF16)
            parts.append(jnp.sum(dxv * br_ref[...], axis=0, keepdims=True))
            outs.append(dg_ref)

        @pl.when(r <= 1)
        def _():
            for o_ref, part in zip(outs, parts):
                o_ref[0] = part

        @pl.when(r > 1)
        def _():
            for o_ref, part in zip(outs, parts):
                o_ref[0] += part

    ins = [x, dh, dres, modtab, gain]
    in_specs = [row, row, row, _mod_spec(D, P, scale_part), pl.BlockSpec((1, D), lambda i: (0, 0))]
    out_specs, out_shape = [row, acc, acc, acc], [_sds((T, D), F32)] + [_sds((G, 1, D), F32)] * 3
    if has_below:
        branch, gate_tab, gate_part = below
        ins += [branch, gate_tab]
        in_specs += [row, _mod_spec(D, P, gate_part)]
        out_specs += [row, acc]
        out_shape += [_sds((T, D), BF16), _sds((G, 1, D), F32)]
    return pl.pallas_call(body, name=name, grid=(T // TR,), in_specs=in_specs, out_specs=out_specs, out_shape=out_shape,
                          compiler_params=_params(("arbitrary",)))(*ins)


def _loss_head(x, branch, modtab, gate_part, target, *, TR, P, name):
    T, D = x.shape
    row = pl.BlockSpec((TR, D), lambda i: (i, 0))
    tgt = pl.BlockSpec((TR, D), lambda i: ((i // P) * (P - 1) + jnp.maximum(i % P - 1, 0), 0))
    one = pl.BlockSpec((1, LANES), lambda i: (0, 0))

    G = modtab.shape[0]
    acc = pl.BlockSpec((1, 1, D), lambda i: (_grp(i, P), 0, 0))

    def body(x_ref, br_ref, g_ref, t_ref, dy_ref, loss_ref, db_ref, dg_ref):
        i = pl.program_id(0)
        r = i % P

        @pl.when(i == 0)
        def _():
            loss_ref[...] = jnp.zeros_like(loss_ref)

        @pl.when(r == 0)
        def _():
            dy_ref[...] = jnp.zeros_like(dy_ref)
            db_ref[...] = jnp.zeros_like(db_ref)
            dg_ref[...] = jnp.zeros_like(dg_ref)

        @pl.when(r > 0)
        def _():
            brv, g = br_ref[...], g_ref[0]
            err = x_ref[...] + g * brv - t_ref[...]
            dy = err / D
            dy_ref[...] = dy
            db_ref[...] = (dy * g).astype(BF16)
            part = jnp.sum(dy * brv, axis=0, keepdims=True)
            per_tok = jnp.mean(err * err, axis=-1, keepdims=True)
            loss_ref[...] += 0.5 * jnp.sum(per_tok, axis=0, keepdims=True)

            @pl.when(r == 1)
            def _():
                dg_ref[0] = part

            @pl.when(r > 1)
            def _():
                dg_ref[0] += part

    return pl.pallas_call(
        body, name=name, grid=(T // TR,), in_specs=[row, row, _mod_spec(D, P, gate_part), tgt],
        out_specs=[row, one, row, acc],
        out_shape=[_sds((T, D), F32), _sds((1, LANES), F32), _sds((T, D), BF16), _sds((G, 1, D), F32)],
        compiler_params=_params(("arbitrary",)),
    )(x, branch, modtab, target)


QKV_WIDTH = Q_WIDTH + 2 * KV_WIDTH
QK_NORMED = 4


def _seg_mean(v):
    lane = lax.broadcasted_iota(jnp.int32, v.shape, 1)
    lo = lane < HEAD_DIM
    s0 = jnp.sum(jnp.where(lo, v, 0.0), axis=-1, keepdims=True)
    s1 = jnp.sum(jnp.where(lo, 0.0, v), axis=-1, keepdims=True)
    return jnp.where(lo, s0, s1) * (1.0 / HEAD_DIM)


def _pair_swap(v):
    lane = lax.broadcasted_iota(jnp.int32, v.shape, 1)
    return jnp.where((lane & 1) == 0, pltpu.roll(v, LANES - 1, 1), pltpu.roll(v, 1, 1))


def _chunk(c):
    return slice(c * LANES, (c + 1) * LANES)


def _qk_prep(z, gains, cos, sin, *, TR, P, name):
    T = z.shape[0]

    def body(z_ref, g_ref, c_ref, s_ref, q_ref, k_ref, v_ref):
        cs, sn = c_ref[...], s_ref[...]
        for ch in range(QK_NORMED):
            xv = z_ref[:, _chunk(ch)]
            y = xv * lax.rsqrt(_seg_mean(xv * xv) + EPS) * g_ref[0, :, _chunk(ch)]
            out = (y * cs + _pair_swap(y) * sn).astype(BF16)
            if ch < QK_NORMED - 1:
                q_ref[:, _chunk(ch)] = out
            else:
                k_ref[...] = out
        v_ref[...] = z_ref[:, _chunk(QK_NORMED)].astype(BF16)

    def out(width):
        return pl.BlockSpec((None, TR, width), lambda i, j: (j, i, 0))
    return pl.pallas_call(
        body, name=name, grid=(T // TR, 2),
        in_specs=[pl.BlockSpec((TR, QKV_WIDTH), lambda i, j: (i, j)),
                  pl.BlockSpec((1, 1, QKV_WIDTH), lambda i, j: (j, 0, 0)),
                  pl.BlockSpec((TR, LANES), lambda i, j: (i % P, 0)),
                  pl.BlockSpec((TR, LANES), lambda i, j: (i % P, 0))],
        out_specs=[out(Q_WIDTH), out(KV_WIDTH), out(KV_WIDTH)],
        out_shape=[_sds((2, T, Q_WIDTH), BF16), _sds((2, T, KV_WIDTH), BF16), _sds((2, T, KV_WIDTH), BF16)],
        compiler_params=_params(("parallel", "parallel")),
    )(z, gains, cos, sin)


def _qk_prep_bwd(z, dq, dk, dv, gains, cos, sin, *, branch, TR, P, name):
    T = z.shape[0]
    nt = T // TR

    def body(z_ref, dq_ref, dk_ref, dv_ref, g_ref, c_ref, s_ref, dz_ref, dg_ref):
        i = pl.program_id(0)
        cs, sn = c_ref[...], s_ref[...]
        parts = []
        for ch in range(QK_NORMED):
            xv, g = z_ref[:, _chunk(ch)], g_ref[0, :, _chunk(ch)]
            dout = dq_ref[:, _chunk(ch)] if ch < QK_NORMED - 1 else dk_ref[...]
            dy = dout * cs + _pair_swap(dout * sn)
            rstd = lax.rsqrt(_seg_mean(xv * xv) + EPS)
            xhat = xv * rstd
            dxhat = dy * g
            dz_ref[:, _chunk(ch)] = (rstd * (dxhat - xhat * _seg_mean(dxhat * xhat))).astype(BF16)
            parts.append(jnp.sum(dy * xhat, axis=0, keepdims=True))
        dz_ref[:, _chunk(QK_NORMED)] = dv_ref[...].astype(BF16)
        parts.append(jnp.zeros((1, LANES), F32))
        part = jnp.concatenate(parts, axis=1)

        @pl.when(i == 0)
        def _():
            dg_ref[0] = part

        @pl.when(i > 0)
        def _():
            dg_ref[0] += part

    def rows(width, col=0):
        return pl.BlockSpec((TR, width), lambda i: (i, col))
    return pl.pallas_call(
        body, name=name, grid=(nt,),
        in_specs=[rows(QKV_WIDTH, branch), rows(Q_WIDTH), rows(KV_WIDTH), rows(KV_WIDTH),
                  pl.BlockSpec((1, 1, QKV_WIDTH), lambda i: (branch, 0, 0)),
                  pl.BlockSpec((TR, LANES), lambda i: (i % P, 0)),
                  pl.BlockSpec((TR, LANES), lambda i: (i % P, 0))],
        out_specs=[rows(QKV_WIDTH), pl.BlockSpec((1, 1, QKV_WIDTH), lambda i: (0, 0, 0))],
        out_shape=[_sds((T, QKV_WIDTH), BF16), _sds((1, 1, QKV_WIDTH), F32)],
        compiler_params=_params(("arbitrary",)),
    )(z, dq, dk, dv, gains, cos, sin)


NT_DIMS = (((1,), (1,)), ((), ()))
TN_DIMS = (((0,), (0,)), ((), ()))
QROWS = N_GROUP * Q_BLOCK
SCORE_SCALE = HEAD_DIM ** -0.5
BAND = Q_BLOCK + 2 * WINDOW
FWD_LATENT_CHUNK = 256
BWD_LATENT_CHUNK = 1024


def _move_head(block, half_from, half_to):
    lane = lax.broadcasted_iota(jnp.int32, block.shape, 1)
    src = block if half_from == half_to else pltpu.roll(block, HEAD_DIM, 1)
    keep = (lane < HEAD_DIM) if half_to == 0 else (lane >= HEAD_DIM)
    return jnp.where(keep, src, 0.0)


def _stack_heads(lane_block, j):
    pieces = []
    for h in range(N_GROUP * j, N_GROUP * (j + 1)):
        pieces.append(_move_head(lane_block(h // 2), h % 2, j))
    return jnp.concatenate(pieces, axis=0)


def _lane_blocks(ref):
    return lambda m: ref[:, m * LANES:(m + 1) * LANES].astype(F32)


def _unstack_heads(stacked, ref):
    heads = []
    for h in range(N_HEADS):
        j, r = h // N_GROUP, h % N_GROUP
        heads.append(_move_head(stacked[j][r * Q_BLOCK:(r + 1) * Q_BLOCK], j, h % 2))
    for m in range(N_HEADS // 2):
        ref[:, m * LANES:(m + 1) * LANES] = (heads[2 * m] + heads[2 * m + 1]).astype(ref.dtype)


def _key_chunks(i, latent, *, n_ctx, t_all, window, chunk, latent_chunk):
    ctx = [(s, chunk, False) for s in range(0, n_ctx, chunk)]
    if not latent:
        return ctx
    if not window:
        wide = latent_chunk if (t_all - n_ctx) % latent_chunk == 0 else chunk
        return ctx + [(s, wide, False) for s in range(n_ctx, t_all, wide)]
    start = pl.multiple_of(jnp.minimum((i - 1) * Q_BLOCK, t_all - BAND), Q_BLOCK)
    band_chunk = BAND if latent_chunk >= BAND else (chunk if BAND % chunk == 0 else Q_BLOCK)
    return ctx + [(start + s, band_chunk, True) for s in range(0, BAND, band_chunk)]


def _scores(q, k_ref, i, start, size, masked, *, n_ctx):
    s = lax.dot_general(q, k_ref[pl.ds(start, size), :], NT_DIMS, preferred_element_type=F32)
    if masked:
        qpos = (i * Q_BLOCK - n_ctx) + (lax.broadcasted_iota(jnp.int32, (QROWS, size), 0) & (Q_BLOCK - 1))
        kpos = (start - n_ctx) + lax.broadcasted_iota(jnp.int32, (QROWS, size), 1)
        valid = (kpos - qpos <= WINDOW) & (qpos - kpos <= WINDOW) & (kpos >= 0)
        s = jnp.where(valid, s, NEG)
    return s


def _sink_column(sink_ref, j):
    r = lax.broadcasted_iota(jnp.int32, (QROWS, 1), 0)
    s0, s1, s2 = sink_ref[j * N_GROUP], sink_ref[j * N_GROUP + 1], sink_ref[j * N_GROUP + 2]
    return jnp.where(r < Q_BLOCK, s0, jnp.where(r < 2 * Q_BLOCK, s1, s2))


def _attn_specs(Tp, branch):
    nq = Tp // Q_BLOCK
    q_in = pl.BlockSpec((None, Q_BLOCK, Q_WIDTH), lambda b, i: (branch, b * nq + i, 0))
    kv_in = pl.BlockSpec((None, Tp, KV_WIDTH), lambda b, i: (branch, b, 0))
    q_out = pl.BlockSpec((Q_BLOCK, Q_WIDTH), lambda b, i: (b * nq + i, 0))
    kv_out = pl.BlockSpec((Tp, KV_WIDTH), lambda b, i: (b, 0))
    return q_in, kv_in, q_out, kv_out


def _attn_chunk(Tp):
    return 256 if Tp % 256 == 0 else Q_BLOCK


def _attn_fwd(q, k, v, sink, *, branch, B, n_ctx, window, name, rider=None):
    T = q.shape[1]
    Tp = T // B
    nq = Tp // Q_BLOCK
    has_sink = sink is not None
    n_in = 4 if has_sink else 3
    q_in, kv_in, q_out, _ = _attn_specs(Tp, branch)
    lse_spec = pl.BlockSpec((None, N_KV * QROWS, 1), lambda b, i: (b * nq + i, 0, 0))

    def body(*refs):
        refs, finish_ride = _ride(rider, refs, n_in, 3, 0, (B, nq))
        sink_ref = refs.pop(0) if has_sink else None
        q_ref, k_ref, v_ref, o_ref, o32_ref, lse_ref = refs
        i = pl.program_id(1)

        def run(latent):
            outs = []
            for j in range(N_KV):
                qv = (_stack_heads(_lane_blocks(q_ref), j) * SCORE_SCALE).astype(BF16)
                if has_sink:
                    m, l = _sink_column(sink_ref, j), jnp.ones((QROWS, 1), F32)
                else:
                    m, l = jnp.full((QROWS, 1), NEG, F32), jnp.zeros((QROWS, 1), F32)
                acc = jnp.zeros((QROWS, LANES), F32)
                for start, size, masked in _key_chunks(i, latent, n_ctx=n_ctx, t_all=Tp, window=window,
                                                       chunk=_attn_chunk(Tp), latent_chunk=FWD_LATENT_CHUNK):
                    s = _scores(qv, k_ref, i, start, size, masked, n_ctx=n_ctx)
                    m_new = jnp.maximum(m, jnp.max(s, axis=-1, keepdims=True))
                    alpha = jnp.exp(m - m_new)
                    p = jnp.exp(s - m_new)
                    l = l * alpha + jnp.sum(p, axis=-1, keepdims=True)
                    acc = acc * alpha + jnp.dot(p.astype(BF16), v_ref[pl.ds(start, size), :], preferred_element_type=F32)
                    m = m_new
                outs.append(acc * (1.0 / l))
                lse_ref[j * QROWS:(j + 1) * QROWS, :] = m + jnp.log(l)
            _unstack_heads(outs, o_ref)
            _unstack_heads(outs, o32_ref)

        @pl.when(i < n_ctx // Q_BLOCK)
        def _():
            run(False)

        @pl.when(i >= n_ctx // Q_BLOCK)
        def _():
            run(True)

        finish_ride()

    ins, specs = [q, k, v], [q_in, kv_in, kv_in]
    if has_sink:
        ins, specs = [sink] + ins, [pl.BlockSpec(memory_space=pltpu.SMEM)] + specs
    out_specs = [q_out, q_out, lse_spec]
    out_shape = [_sds((T, Q_WIDTH), BF16), _sds((T, Q_WIDTH), F32), _sds((T // Q_BLOCK, N_KV * QROWS, 1), F32)]
    ins, specs, out_specs, out_shape, scratch = _hitch(rider, ins, specs, out_specs, out_shape, [])
    return pl.pallas_call(
        body, name=name, grid=(B, nq), in_specs=specs, out_specs=out_specs, out_shape=out_shape, scratch_shapes=scratch,
        compiler_params=_params(("arbitrary", "arbitrary") if rider is not None else ("parallel", "parallel")),
    )(*ins)


def _attn_bwd(q, k, v, do, o32, lse, sink, *, branch, B, n_ctx, window, name, rider=None):
    T = q.shape[1]
    Tp = T // B
    nq = Tp // Q_BLOCK
    has_sink = sink is not None
    q_in, kv_in, q_out, kv_out = _attn_specs(Tp, branch)
    lse_spec = pl.BlockSpec((None, N_KV * QROWS, 1), lambda b, i: (b * nq + i, 0, 0))
    sink_spec = pl.BlockSpec((None, 8, LANES), lambda b, i: (b, 0, 0))

    def body(*refs):
        refs, finish_ride = _ride(rider, refs, 7 if has_sink else 6, 4 if has_sink else 3, 2, (B, nq))
        if has_sink:
            sink_ref, q_ref, k_ref, v_ref, do_ref, o_ref, lse_ref, dq_ref, dk_ref, dv_ref, ds_ref, dkt_ref, dvt_ref = refs
        else:
            q_ref, k_ref, v_ref, do_ref, o_ref, lse_ref, dq_ref, dk_ref, dv_ref, dkt_ref, dvt_ref = refs
        i = pl.program_id(1)

        @pl.when(i == 0)
        def _():
            dk_ref[...] = jnp.zeros_like(dk_ref)
            dv_ref[...] = jnp.zeros_like(dv_ref)
            if not window:
                dkt_ref[...] = jnp.zeros_like(dkt_ref)
                dvt_ref[...] = jnp.zeros_like(dvt_ref)
            if has_sink:
                ds_ref[...] = jnp.zeros_like(ds_ref)

        def run(latent):
            upd = jnp.zeros((8, LANES), F32)
            do_blocks, o_blocks = _lane_blocks(do_ref), _lane_blocks(o_ref)
            qvs = [(_stack_heads(_lane_blocks(q_ref), j) * SCORE_SCALE).astype(BF16) for j in range(N_KV)]
            dovs = [_stack_heads(do_blocks, j).astype(BF16) for j in range(N_KV)]
            deltas = [jnp.sum(_stack_heads(lambda m: do_blocks(m) * o_blocks(m), j), axis=-1, keepdims=True)
                      for j in range(N_KV)]
            lses = [lse_ref[j * QROWS:(j + 1) * QROWS, :] for j in range(N_KV)]
            q_all, do_all = jnp.concatenate(qvs, axis=0), jnp.concatenate(dovs, axis=0)
            q_all_t, do_all_t = q_all.T, do_all.T
            dqs = [jnp.zeros((QROWS, LANES), F32) for _ in range(N_KV)]
            for start, size, masked in _key_chunks(i, latent, n_ctx=n_ctx, t_all=Tp, window=window,
                                                   chunk=_attn_chunk(Tp), latent_chunk=BWD_LATENT_CHUNK):
                rows = pl.ds(start, size)
                ds_all, p_all = [], []
                for j in range(N_KV):
                    p = jnp.exp(_scores(qvs[j], k_ref, i, start, size, masked, n_ctx=n_ctx) - lses[j])
                    dp = lax.dot_general(dovs[j], v_ref[rows, :], NT_DIMS, preferred_element_type=F32)
                    ds = (p * (dp - deltas[j])).astype(BF16)
                    dqs[j] = dqs[j] + jnp.dot(ds, k_ref[rows, :], preferred_element_type=F32)
                    ds_all.append(ds)
                    p_all.append(p.astype(BF16))
                ds_cat, p_cat = jnp.concatenate(ds_all, axis=0), jnp.concatenate(p_all, axis=0)
                if window:
                    dk_ref[rows, :] += lax.dot_general(ds_cat, q_all, TN_DIMS, preferred_element_type=F32)
                    dv_ref[rows, :] += lax.dot_general(p_cat, do_all, TN_DIMS, preferred_element_type=F32)
                else:
                    dkt_ref[:, start:start + size] += jnp.dot(q_all_t, ds_cat, preferred_element_type=F32)
                    dvt_ref[:, start:start + size] += jnp.dot(do_all_t, p_cat, preferred_element_type=F32)
            dqs = [dq * SCORE_SCALE for dq in dqs]
            for j in range(N_KV):
                if has_sink:
                    contrib = -(jnp.exp(_sink_column(sink_ref, j) - lses[j]) * deltas[j])
                    r = lax.broadcasted_iota(jnp.int32, (QROWS, 1), 0)
                    row8 = lax.broadcasted_iota(jnp.int32, (8, LANES), 0)
                    for h in range(N_GROUP):
                        in_head = (r >= h * Q_BLOCK) & (r < (h + 1) * Q_BLOCK)
                        tot = jnp.sum(jnp.where(in_head, contrib, 0.0), axis=0, keepdims=True)
                        upd = upd + jnp.where(row8 == j * N_GROUP + h, tot, 0.0)
            _unstack_heads(dqs, dq_ref)
            if has_sink:
                ds_ref[...] += upd

        @pl.when(i < n_ctx // Q_BLOCK)
        def _():
            run(False)

        @pl.when(i >= n_ctx // Q_BLOCK)
        def _():
            run(True)

        if not window:
            @pl.when(i == nq - 1)
            def _():
                dk_ref[...] += dkt_ref[...].T
                dv_ref[...] += dvt_ref[...].T

        finish_ride()

    ins, specs = [q, k, v, do, o32, lse], [q_in, kv_in, kv_in, q_out, q_out, lse_spec]
    out_specs = [q_out, kv_out, kv_out]
    out_shape = [_sds((T, Q_WIDTH), F32), _sds((T, KV_WIDTH), F32), _sds((T, KV_WIDTH), F32)]
    if has_sink:
        ins, specs = [sink] + ins, [pl.BlockSpec(memory_space=pltpu.SMEM)] + specs
        out_specs.append(sink_spec)
        out_shape.append(_sds((B, 8, LANES), F32))
    scratch = [pltpu.VMEM((KV_WIDTH, LANES if window else Tp), F32)] * 2
    ins, specs, out_specs, out_shape, scratch = _hitch(rider, ins, specs, out_specs, out_shape, scratch)
    return pl.pallas_call(
        body, name=name, grid=(B, nq), in_specs=specs, out_specs=out_specs, out_shape=out_shape, scratch_shapes=scratch,
        compiler_params=_params(("arbitrary", "arbitrary") if rider is not None else ("parallel", "arbitrary")),
    )(*ins)


def _window_sums(xp):
    n = xp.shape[0]

    def ahead(a, k):
        return pltpu.roll(a, n - k, 0)
    a2 = xp + ahead(xp, 1)
    a4 = a2 + ahead(a2, 2)
    a8 = a4 + ahead(a4, 4)
    a16 = a8 + ahead(a8, 8)
    return (a2, a4, a8, a16)


def _by_group(vals):
    lane = lax.broadcasted_iota(jnp.int32, vals[0].shape, 1)
    return jnp.where(lane < POOL_CH, vals[0], jnp.where(lane < 2 * POOL_CH, vals[1],
                     jnp.where(lane < 3 * POOL_CH, vals[2], vals[3])))


def _pool_counts(n):
    t = lax.broadcasted_iota(jnp.int32, (n, POOL_WIDTH), 0)
    cnts = [(jnp.minimum(t + w // 2, n) - jnp.maximum(t - w // 2, 0)).astype(F32) for w in POOL_WINDOWS]
    return _by_group(cnts)


def _pad_rows(x):
    zeros = jnp.zeros((POOL_PAD, x.shape[1]), x.dtype)
    return jnp.concatenate([zeros, x, zeros], axis=0)


def _pool_stream(u):
    n = u.shape[0]
    sums = _window_sums(_pad_rows(u))
    tots = [pltpu.roll(a, w // 2, 0)[POOL_PAD:POOL_PAD + n] for a, w in zip(sums, POOL_WINDOWS)]
    return _by_group(tots) / _pool_counts(n) - u


def _pool_stream_t(dp):
    n = dp.shape[0]
    sums = _window_sums(_pad_rows(dp / _pool_counts(n)))
    tots = [pltpu.roll(a, w // 2 - 1, 0)[POOL_PAD:POOL_PAD + n] if w > 2 else a[POOL_PAD:POOL_PAD + n]
            for a, w in zip(sums, POOL_WINDOWS)]
    return _by_group(tots) - dp


def _pool_fwd(z, w_bd, scale, *, B, Tp, n_ctx, name):
    T = z.shape[0]
    blk = pl.BlockSpec((Tp, POOL_WIDTH), lambda b: (b, U_COL // POOL_WIDTH))
    out = pl.BlockSpec((Tp, POOL_WIDTH), lambda b: (b, 0))

    def body(u_ref, w_ref, s_ref, p_ref, o_ref):
        for lo, hi in ((0, n_ctx), (n_ctx, Tp)):
            pooled = _pool_stream(u_ref[lo:hi, :]).astype(BF16)
            p_ref[lo:hi, :] = pooled
            mixed = jnp.dot(pooled, w_ref[...], preferred_element_type=F32)
            o_ref[lo:hi, :] = (mixed * s_ref[...]).astype(BF16)

    return pl.pallas_call(
        body, name=name, grid=(B,),
        in_specs=[blk, pl.BlockSpec((POOL_WIDTH, POOL_WIDTH), lambda b: (0, 0)), pl.BlockSpec((1, POOL_WIDTH), lambda b: (0, 0))],
        out_specs=[out, out], out_shape=[_sds((T, POOL_WIDTH), BF16)] * 2, compiler_params=_params(("parallel",)),
    )(z, w_bd, scale)


def _pool_bwd(d_ob, pooled, w_bd, scale, *, B, Tp, n_ctx, name):
    T = d_ob.shape[0]
    blk = pl.BlockSpec((Tp, POOL_WIDTH), lambda b: (b, 0))
    wsp = pl.BlockSpec((POOL_WIDTH, POOL_WIDTH), lambda b: (0, 0))
    ssp = pl.BlockSpec((1, POOL_WIDTH), lambda b: (0, 0))

    def body(d_ref, p_ref, w_ref, s_ref, du_ref, dw_ref, dsc_ref):
        @pl.when(pl.program_id(0) == 0)
        def _():
            dw_ref[...] = jnp.zeros_like(dw_ref)
            dsc_ref[...] = jnp.zeros_like(dsc_ref)

        dv, pv, wv = d_ref[...], p_ref[...], w_ref[...]
        mixed = jnp.dot(pv, wv, preferred_element_type=F32)
        dsc_ref[...] += jnp.sum(dv * mixed, axis=0, keepdims=True)
        dmixed = (dv * s_ref[...]).astype(BF16)
        dw_ref[...] += lax.dot_general(pv, dmixed, TN_DIMS, preferred_element_type=F32)
        dpooled = lax.dot_general(dmixed, wv, NT_DIMS, preferred_element_type=F32)
        for lo, hi in ((0, n_ctx), (n_ctx, Tp)):
            du_ref[lo:hi, :] = _pool_stream_t(dpooled[lo:hi, :]).astype(BF16)

    return pl.pallas_call(
        body, name=name, grid=(B,), in_specs=[blk, blk, wsp, ssp], out_specs=[blk, wsp, ssp],
        out_shape=[_sds((T, POOL_WIDTH), BF16), _sds((POOL_WIDTH, POOL_WIDTH), F32), _sds((1, POOL_WIDTH), F32)],
        compiler_params=_params(("arbitrary",)),
    )(d_ob, pooled, w_bd, scale)


def _merge_specs(z, D, TR, tc, wa, wb, wc):
    def act(width):
        return pl.BlockSpec((TR, width), lambda i, n: (i, 0))

    def gate(part):
        return pl.BlockSpec((TR, tc), lambda i, n: (i, (GATE_COL + part * D) // tc + n))
    w_specs = [w.spec(w.shape[0], tc, lambda i, n: (0, n)) for w in (wa, wb, wc)]
    return [act(Q_WIDTH), act(POOL_WIDTH), act(Q_WIDTH), gate(0), gate(1), gate(2)] + w_specs


def _merge_fwd(oa, ob, oc, z, wa, wb, wc, *, D, TR, name):
    T = oa.shape[0]
    tc = D // N_CHIPS

    def body(oa_ref, ob_ref, oc_ref, ga_ref, gb_ref, gc_ref, wa_ref, wb_ref, wc_ref, y_ref):
        acc = jax.nn.sigmoid(ga_ref[...]) * jnp.dot(oa_ref[...], wa_ref[...], preferred_element_type=F32)
        acc += jax.nn.sigmoid(gb_ref[...]) * jnp.dot(ob_ref[...], wb_ref[...], preferred_element_type=F32)
        acc += jax.nn.sigmoid(gc_ref[...]) * jnp.dot(oc_ref[...], wc_ref[...], preferred_element_type=F32)
        y_ref[...] = acc.astype(BF16)

    return pl.pallas_call(
        body, name=name, grid=(T // TR, D // tc), in_specs=_merge_specs(z, D, TR, tc, wa, wb, wc),
        out_specs=pl.BlockSpec((TR, tc), lambda i, n: (i, n)), out_shape=_sds((T, D), BF16),
        compiler_params=_params(("parallel", "parallel")),
    )(oa, ob, oc, z, z, z, wa.arr, wb.arr, wc.arr)


def _merge_bwd(dy, oa, ob, oc, z, wa, wb, wc, *, D, TR, name):
    T = oa.shape[0]
    tc = D // N_CHIPS
    out = pl.BlockSpec((TR, tc), lambda i, n: (i, n))

    def body(dy_ref, oa_ref, ob_ref, oc_ref, ga_ref, gb_ref, gc_ref, wa_ref, wb_ref, wc_ref,
             dpa_ref, dpb_ref, dpc_ref, dga_ref, dgb_ref, dgc_ref):
        dyv = dy_ref[...]
        for o_ref, g_ref, w_ref, dp_ref, dg_ref in ((oa_ref, ga_ref, wa_ref, dpa_ref, dga_ref),
                                                    (ob_ref, gb_ref, wb_ref, dpb_ref, dgb_ref),
                                                    (oc_ref, gc_ref, wc_ref, dpc_ref, dgc_ref)):
            s = jax.nn.sigmoid(g_ref[...])
            proj = jnp.dot(o_ref[...], w_ref[...], preferred_element_type=F32)
            dp_ref[...] = (dyv * s).astype(BF16)
            dg_ref[...] = (dyv * proj * (s * (1.0 - s))).astype(BF16)

    return pl.pallas_call(
        body, name=name, grid=(T // TR, D // tc), in_specs=[out] + _merge_specs(z, D, TR, tc, wa, wb, wc),
        out_specs=[out] * 6, out_shape=[_sds((T, D), BF16)] * 6, compiler_params=_params(("parallel", "parallel")),
    )(dy, oa, ob, oc, z, z, z, wa.arr, wb.arr, wc.arr)


def _silu_rows(cc, name):
    def body(c_ref, s_ref):
        v = c_ref[...]
        s_ref[...] = (v * jax.nn.sigmoid(v)).astype(BF16)
    return pl.pallas_call(body, name=name, out_shape=_sds(cc.shape, BF16))(cc)


def _ada_bwd_rows(dm, ds, cc, name):
    def body(dm_ref, ds_ref, c_ref, db_ref, dc_ref):
        db_ref[...] = jnp.sum(dm_ref[...], axis=0, keepdims=True)
        v = c_ref[...]
        s = jax.nn.sigmoid(v)
        dc_ref[...] = ds_ref[...] * (s * (1.0 + v * (1.0 - s)))
    return pl.pallas_call(body, name=name, out_shape=[_sds((1, dm.shape[1]), F32), _sds(cc.shape, F32)])(dm, ds, cc)


def _row_tile(rows, cols):
    for t in (512, 256, 128, 64, 32, 16, 8):
        if rows % t == 0 and t * cols * 4 <= (1 << 20):
            return t
    return rows


def _add_landed(own, landed, name):
    R, C = own.shape
    tr = _row_tile(R, C)
    row = pl.BlockSpec((tr, C), lambda i: (i, 0))

    def body(a_ref, b_ref, o_ref, o16_ref):
        tot = a_ref[...] + b_ref[...].astype(F32)
        o_ref[...] = tot
        o16_ref[...] = tot.astype(BF16)

    return pl.pallas_call(body, name=name, grid=(R // tr,), in_specs=[row, row], out_specs=[row, row],
                          out_shape=[_sds((R, C), F32), _sds((R, C), BF16)], compiler_params=_params(("parallel",)))(own, landed)


def _sum_chips(own, landed, chip, name):
    _, R, C = own.shape
    tr = _row_tile(R, C)

    def body(k_ref, a_ref, b_ref, o_ref):
        o_ref[...] = ((a_ref[...] + b_ref[0].astype(F32)) + b_ref[1].astype(F32)) + b_ref[2].astype(F32)

    grid_spec = pltpu.PrefetchScalarGridSpec(
        num_scalar_prefetch=1, grid=(R // tr,),
        in_specs=[pl.BlockSpec((None, tr, C), lambda i, k: (k[0], i, 0)), pl.BlockSpec((3, tr, C), lambda i, k: (0, i, 0))],
        out_specs=pl.BlockSpec((tr, C), lambda i, k: (i, 0)))
    return pl.pallas_call(body, name=name, grid_spec=grid_spec, out_shape=_sds((R, C), F32),
                          compiler_params=_params(("parallel",)))(chip, own, landed)


def _adam_math(w, g, m, v):
    m = ADAM_B1 * m + (1.0 - ADAM_B1) * g
    v = ADAM_B2 * v + (1.0 - ADAM_B2) * (g * g)
    m_hat = m / (1.0 - ADAM_B1 ** ADAM_STEP)
    v_hat = v / (1.0 - ADAM_B2 ** ADAM_STEP)
    delta = -ADAM_LR * (m_hat / (jnp.sqrt(v_hat) + ADAM_EPS) + ADAM_WD * w)
    return delta, m, v


def _adamw(w, reduced, other, m, v, core, name):
    L, R, C = w.shape
    tr = _row_tile(R, C)

    def body(c_ref, w_ref, r0_ref, r1_ref, b_ref, m_ref, v_ref, g_ref, d_ref, mo_ref, vo_ref):
        def step(g):
            d, mn, vn = _adam_math(w_ref[...], g, m_ref[...], v_ref[...])
            g_ref[...] = g
            d_ref[...] = d
            mo_ref[...] = mn
            vo_ref[...] = vn

        layer, core_id = pl.program_id(0), c_ref[0]
        for l, r_ref in enumerate((r0_ref, r1_ref)):
            @pl.when((layer == l) & (core_id == l))
            def _(r_ref=r_ref):
                step(r_ref[...])

        @pl.when(layer != core_id)
        def _():
            step(b_ref[...])

    lay = pl.BlockSpec((None, tr, C), lambda l, i, c: (l, i, 0))
    row = pl.BlockSpec((tr, C), lambda l, i, c: (i, 0))
    grid_spec = pltpu.PrefetchScalarGridSpec(num_scalar_prefetch=1, grid=(L, R // tr),
                                             in_specs=[lay, row, row, row, lay, lay], out_specs=[lay] * 4)
    return pl.pallas_call(body, name=name, grid_spec=grid_spec, out_shape=[_sds((L, R, C), F32)] * 4,
                          compiler_params=_params(("parallel", "parallel")))(core, w, reduced[0], reduced[1], other, m, v)


def _adamw_small(w, parts, m, v, name):
    R, C = w.shape

    def body(w_ref, p_ref, m_ref, v_ref, g_ref, d_ref, mo_ref, vo_ref):
        g = p_ref[0]
        for dev in range(1, 8):
            g = g + p_ref[dev]
        d, mn, vn = _adam_math(w_ref[...], g, m_ref[...], v_ref[...])
        g_ref[...] = g
        d_ref[...] = d
        mo_ref[...] = mn
        vo_ref[...] = vn

    return pl.pallas_call(body, name=name, out_shape=[_sds((R, C), F32)] * 4)(w, parts, m, v)


def _place():
    return lax.axis_index("x"), lax.axis_index("y"), lax.axis_index("c")


def _other_chips(x, y):
    return [(1 - x, y), (x, 1 - y), (1 - x, 1 - y)]


def _rcopy(src, dst, ssem, rsem, dev):
    return pltpu.make_async_remote_copy(src_ref=src, dst_ref=dst, send_sem=ssem, recv_sem=rsem,
                                        device_id=dev, device_id_type=MESH)


GATHER_SEMS = 7


class _LayerGather:
    def __init__(self, shards, layer):
        self.inputs, self.layer, self.n = list(shards), layer, len(shards)
        load, self.groups = [0, 0], ([], [])
        for w in sorted(range(self.n), key=lambda w: -shards[w][0].size):
            g = 0 if load[0] <= load[1] else 1
            self.groups[g].append(w)
            load[g] += shards[w][0].size
        self.out_shape = [_sds((N_CHIPS,) + s.shape[1:], s.dtype) for s in shards]
        self.scratch = [pltpu.SemaphoreType.DMA((self.n, GATHER_SEMS)), pltpu.SemaphoreType.DMA((self.n, GATHER_SEMS))]

    def _own(self, src, out, send_sems, recv_sems):
        x, y, c = _place()
        return [_rcopy(src[w].at[self.layer], out[w].at[2 * x + y], send_sems.at[w, 6], recv_sems.at[w, 6], (x, y, 1 - c))
                for w in range(self.n)]

    def _to_chips(self, g, src, out, send_sems, recv_sems):
        x, y, c = _place()
        return [_rcopy(src[w].at[self.layer], out[w].at[2 * x + y], send_sems.at[w, j], recv_sems.at[w, j], (*chip, c))
                for w in self.groups[g] for j, chip in enumerate(_other_chips(x, y))]

    def start(self, src, out, send_sems, recv_sems):
        c = lax.axis_index("c")
        for cp in self._own(src, out, send_sems, recv_sems):
            cp.start()
        for g in (0, 1):
            @pl.when(c == g)
            def _(g=g):
                for cp in self._to_chips(g, src, out, send_sems, recv_sems):
                    cp.start()

    def finish(self, src, out, send_sems, recv_sems):
        x, y, c = _place()
        sibling = (x, y, 1 - c)
        chips = _other_chips(x, y)
        for g in (0, 1):
            @pl.when(c == g)
            def _(g=g):
                passed = []
                for w in self.groups[g]:
                    for j, (px, py) in enumerate(chips):
                        landed = out[w].at[2 * px + py]
                        _rcopy(landed, landed, send_sems.at[w, j], recv_sems.at[w, j], (px, py, c)).wait_recv()
                        cp = _rcopy(landed, landed, send_sems.at[w, 3 + j], recv_sems.at[w, 3 + j], sibling)
                        cp.start()
                        passed.append(cp)
                for w in self.groups[1 - g]:
                    for j, (px, py) in enumerate(chips):
                        landed = out[w].at[2 * px + py]
                        _rcopy(landed, landed, send_sems.at[w, 3 + j], recv_sems.at[w, 3 + j], sibling).wait_recv()
                for cp in self._to_chips(g, src, out, send_sems, recv_sems) + passed:
                    cp.wait_send()
        for cp in self._own(src, out, send_sems, recv_sems):
            cp.wait_recv()
            cp.wait_send()


class _SiblingSwap:
    def __init__(self, arrays):
        self.inputs = list(arrays)
        n = len(self.inputs)
        self.out_shape = [_sds(a.shape, a.dtype) for a in self.inputs]
        self.scratch = [pltpu.SemaphoreType.DMA((n,)), pltpu.SemaphoreType.DMA((n,))]

    def _copies(self, src, out, send_sems, recv_sems):
        x, y, c = _place()
        return [_rcopy(src[w], out[w], send_sems.at[w], recv_sems.at[w], (x, y, 1 - c)) for w in range(len(src))]

    def start(self, src, out, send_sems, recv_sems):
        for cp in self._copies(src, out, send_sems, recv_sems):
            cp.start()

    def finish(self, src, out, send_sems, recv_sems):
        cps = self._copies(src, out, send_sems, recv_sems)
        for cp in cps:
            cp.wait_recv()
        for cp in cps:
            cp.wait_send()


class _ChipSend:
    def __init__(self, blocked, sender):
        self.inputs, self.sender = list(blocked), sender
        n = len(self.inputs)
        self.out_shape = [_sds((3,) + a.shape[1:], a.dtype) for a in self.inputs]
        self.scratch = [pltpu.SemaphoreType.DMA((n, 3)), pltpu.SemaphoreType.DMA((n, 3))]

    def _copies(self, src, out, send_sems, recv_sems):
        x, y, c = _place()
        return [_rcopy(src[w].at[2 * px + py], out[w].at[j], send_sems.at[w, j], recv_sems.at[w, j], (px, py, c))
                for w in range(len(src)) for j, (px, py) in enumerate(_other_chips(x, y))]

    def start(self, src, out, send_sems, recv_sems):
        @pl.when(lax.axis_index("c") == self.sender)
        def _():
            for cp in self._copies(src, out, send_sems, recv_sems):
                cp.start()

    def finish(self, src, out, send_sems, recv_sems):
        @pl.when(lax.axis_index("c") == self.sender)
        def _():
            cps = self._copies(src, out, send_sems, recv_sems)
            for cp in cps:
                cp.wait_recv()
            for cp in cps:
                cp.wait_send()


def _ride_alone(rider, name):
    n_in, n_out = len(rider.inputs), len(rider.out_shape)

    def body(*refs):
        args = (refs[:n_in], refs[n_in:n_in + n_out]) + tuple(refs[n_in + n_out:])
        rider.start(*args)
        rider.finish(*args)

    return pl.pallas_call(body, name=name, in_specs=[ANY] * n_in, out_specs=[ANY] * n_out, out_shape=rider.out_shape,
                          scratch_shapes=rider.scratch)(*rider.inputs)


def _hitch(rider, ins, in_specs, out_specs, out_shape, scratch):
    if rider is None:
        return ins, in_specs, out_specs, out_shape, scratch
    return (list(ins) + rider.inputs, list(in_specs) + [ANY] * len(rider.inputs),
            list(out_specs) + [ANY] * len(rider.out_shape), list(out_shape) + rider.out_shape, list(scratch) + rider.scratch)


def _ride(rider, refs, n_in, n_out, n_scratch, grid):
    if rider is None:
        return list(refs), lambda: None
    r_in, r_out = len(rider.inputs), len(rider.out_shape)
    refs = list(refs)
    own_in, ride_in = refs[:n_in], refs[n_in:n_in + r_in]
    rest = refs[n_in + r_in:]
    own_out, ride_out = rest[:n_out], rest[n_out:n_out + r_out]
    rest = rest[n_out + r_out:]
    own_scratch, sems = rest[:n_scratch], rest[n_scratch:]
    ids = [pl.program_id(a) for a in range(len(grid))]
    first = functools.reduce(jnp.logical_and, [i == 0 for i in ids])
    last = functools.reduce(jnp.logical_and, [i == g - 1 for i, g in zip(ids, grid)])

    @pl.when(first)
    def _():
        rider.start(ride_in, ride_out, *sems)

    def finish():
        @pl.when(last)
        def _():
            rider.finish(ride_in, ride_out, *sems)

    return own_in + own_out + own_scratch, finish


def _send_other_layer(layer0, layer1, name):
    n = len(layer0)

    def body(*refs):
        src0, src1, out = refs[:n], refs[n:2 * n], refs[2 * n:3 * n]
        send_sems, recv_sems = refs[3 * n:]
        x, y, c = _place()

        def copies(src):
            return [_rcopy(src[w], out[w], send_sems.at[w], recv_sems.at[w], (x, y, 1 - c)) for w in range(n)]

        @pl.when(c == 0)
        def _():
            for cp in copies(src1):
                cp.start()

        @pl.when(c == 1)
        def _():
            for cp in copies(src0):
                cp.start()

        for cp in copies(src0):
            cp.wait_recv()
        for cp in copies(src0):
            cp.wait_send()

    return pl.pallas_call(
        body, name=name, in_specs=[ANY] * (2 * n), out_specs=[ANY] * n,
        out_shape=[_sds(s.shape, s.dtype) for s in layer0],
        scratch_shapes=[pltpu.SemaphoreType.DMA((n,)), pltpu.SemaphoreType.DMA((n,))],
    )(*layer0, *layer1)


def _gather_small(block, name):
    m_per, n = block.shape

    def body(x_ref, out_ref, send_sems, recv_sems, local_sem):
        x, y, c = _place()
        me, sibling = (x, y, c), (x, y, 1 - c)
        chips = _other_chips(x, y)

        def rows(px, py, pc):
            return out_ref.at[pl.ds((4 * px + 2 * py + pc) * m_per, m_per), :]

        def copy(k, blk, to, src=None):
            return _rcopy(rows(*blk) if src is None else src, rows(*blk), send_sems.at[k], recv_sems.at[k], to)

        mine = pltpu.make_async_copy(x_ref, rows(*me), local_sem)
        mine.start()
        first = [copy(0, me, sibling, src=x_ref)]
        first += [copy(1 + j, me, (*chip, c), src=x_ref) for j, chip in enumerate(chips)]
        for cp in first:
            cp.start()
        passed = [copy(4 + j, (*chip, c), sibling) for j, chip in enumerate(chips)]
        for j, chip in enumerate(chips):
            copy(1 + j, (*chip, c), me).wait_recv()
            passed[j].start()
        copy(0, sibling, me).wait_recv()
        for j, chip in enumerate(chips):
            copy(4 + j, (*chip, 1 - c), me).wait_recv()
        for cp in first + passed:
            cp.wait_send()
        mine.wait()

    return pl.pallas_call(
        body, name=name, out_shape=_sds((8 * m_per, n), block.dtype),
        in_specs=[pl.BlockSpec(memory_space=pltpu.VMEM)], out_specs=pl.BlockSpec(memory_space=pltpu.VMEM),
        scratch_shapes=[pltpu.SemaphoreType.DMA((7,)), pltpu.SemaphoreType.DMA((7,)), pltpu.SemaphoreType.DMA],
    )(block)


def _rope_tables(n_ctx, seq):
    rows = seq // GRID_W
    r = jnp.repeat(jnp.arange(rows, dtype=F32), GRID_W)
    col = jnp.tile(jnp.arange(GRID_W, dtype=F32), rows)
    inv = 1.0 / (ROPE_THETA ** (jnp.arange(0, AXIS_DIM, 2, dtype=F32) / AXIS_DIM))
    ang = jnp.concatenate([r[:, None] * inv, col[:, None] * inv], axis=-1)
    cos = jnp.repeat(jnp.cos(ang), 2, axis=-1)
    sin = jnp.repeat(jnp.sin(ang), 2, axis=-1) * jnp.tile(jnp.array([-1.0, 1.0], F32), HEAD_DIM // 2)
    cos = jnp.concatenate([jnp.ones((n_ctx, HEAD_DIM), F32), cos], axis=0)
    sin = jnp.concatenate([jnp.zeros((n_ctx, HEAD_DIM), F32), sin], axis=0)
    return jnp.tile(cos, (1, 2)), jnp.tile(sin, (1, 2))


def _block_diag(w_pool):
    L, G = w_pool.shape[:2]
    eye = jnp.eye(G, dtype=w_pool.dtype)
    return (w_pool[:, :, :, None, :] * eye[None, :, None, :, None]).reshape(L, POOL_WIDTH, POOL_WIDTH)


def _qk_gains(small):
    qn = jnp.stack([small["q_norm_a"], small["q_norm_c"]], axis=1)[:, :, None, :]
    kn = jnp.stack([small["k_norm_a"], small["k_norm_c"]], axis=1)[:, :, None, :]
    L = qn.shape[0]
    rows = jnp.concatenate([jnp.broadcast_to(qn, (L, 2, N_HEADS, HEAD_DIM)), jnp.broadcast_to(kn, (L, 2, N_KV, HEAD_DIM)),
                            jnp.ones((L, 2, N_KV, HEAD_DIM), F32)], axis=2)
    return rows.reshape(L, 2, 1, QKV_WIDTH)


def _local_step(x, c, ctx, c_ctx, small, gw, target, rider=None, overlap=False):
    gw = list(gw)
    B, S, D = x.shape
    N = ctx.shape[1]
    L = small["norm1"].shape[0]
    Tp = N + S
    T = B * Tp
    TR = N
    P = Tp // N
    rows16 = 16
    assert N % Q_BLOCK == 0 and S % N == 0 and B + 1 <= rows16
    TM = _tile(T, (1024, 768, 512, 384, 256, 128))
    TMG = _tile(T, (512, 384, 256, 128))

    X = jnp.concatenate([ctx, x], axis=1).reshape(T, D)
    cc = jnp.concatenate([c, c_ctx[None], jnp.zeros((rows16 - B - 1, D), F32)], axis=0)
    s_rows = _silu_rows(cc, "silu_rows")
    cos, sin = _rope_tables(N, S)
    all_gains = _qk_gains(small)
    all_w_bd = _block_diag(small["w_pool"]).astype(BF16)

    def weights(l):
        g = gw[l]
        return dict(
            ada=_Opnd(g["w_ada"], "bcols"), w_in=_Opnd(g["w_in"], "bcols"),
            a=_Opnd(g["w_br_a"], "bcols"), b=_Opnd(g["w_br_b"], "bcols"), c=_Opnd(g["w_br_c"], "bcols"),
            out=_Opnd(g["w_out"], "brows"), mlp1=_Opnd(g["w_mlp1"], "bcols"), mlp2=_Opnd(g["w_mlp2"], "brows"))

    IN = weights(0)["w_in"].shape[1]
    DFF = weights(0)["mlp1"].shape[1]
    tn_in = _tile(IN // N_CHIPS, (1152, 768, 512, 384, 256, 128))
    tn_ff = _tile(DFF // N_CHIPS, (1024, 512, 256, 128))
    tn_ada = _tile(6 * D // N_CHIPS, (1536, 768, 512, 256, 128))
    tn_d = D // N_CHIPS
    tk_d = _tile(D, (512,))
    tk_tok = _tile(T, (2304, 1536, 1024, 768, 512, 384, 256))

    saved = []
    xin, pending = X, None
    for l in range(L):
        W = weights(l)
        b_ada = small["b_ada"][l].reshape(1, 6 * D)
        mod = _matmul(s_rows, W["ada"], "nn", tm=rows16, tn=tn_ada, tk=D, name=f"ada_fwd{l}",
                      epilogue=lambda acc, b: (acc + b,), extras=[(b_ada, (1, tn_ada), lambda m, n: (0, n))])
        modtab = jnp.stack([jnp.broadcast_to(mod[B], (B, 6 * D)), mod[:B]], axis=1).reshape(2 * B, 1, 6 * D)
        gains = all_gains[l]
        w_bd = all_w_bd[l]
        p_scale = small["pool_scale"][l].reshape(1, POOL_WIDTH)
        sink = small["sink_c"][l]

        x0, h1 = _res_norm(xin, pending, modtab, 0, 1, small["norm1"][l][None], TR=TR, P=P, name=f"norm1_fwd{l}")
        z = _matmul(h1, W["w_in"], "nn", tm=TM, tn=tn_in, tk=D, name=f"in_proj{l}")
        q2, k2, v2 = _qk_prep(z, gains, cos, sin, TR=TR, P=P, name=f"qk_prep{l}")
        riding = rider if l == 0 else None
        oa, oa32, lse_a, *landed = _attn_fwd(q2, k2, v2, None, branch=0, B=B, n_ctx=N, window=False,
                                             name=f"attn_a_fwd{l}", rider=riding)
        if riding is not None:
            gw[riding.layer] = dict(zip(BIG_NAMES, landed))
        oc, oc32, lse_c = _attn_fwd(q2, k2, v2, sink, branch=1, B=B, n_ctx=N, window=True, name=f"attn_c_fwd{l}")
        pooled, ob = _pool_fwd(z, w_bd, p_scale, B=B, Tp=Tp, n_ctx=N, name=f"pool_fwd{l}")
        y = _merge_fwd(oa, ob, oc, z, W["a"], W["b"], W["c"], D=D, TR=TMG, name=f"merge_fwd{l}")
        ao = _matmul(y, W["out"], "nn", tm=TM, tn=D, tk=tn_d, name=f"out_proj{l}")
        x1, h2 = _res_norm(x0, (ao, modtab, 2), modtab, 3, 4, small["norm2"][l][None], TR=TR, P=P, name=f"norm2_fwd{l}")
        a_pre, r_act = _matmul(h2, W["mlp1"], "nn", tm=TM, tn=tn_ff, tk=D, name=f"mlp1_fwd{l}", out_dtypes=(F32, BF16),
                               epilogue=lambda acc: (acc, jnp.square(jnp.maximum(acc, 0.0))))
        mo = _matmul(r_act, W["mlp2"], "nn", tm=TM, tn=D, tk=tn_ff, name=f"mlp2_fwd{l}")
        saved.append(dict(modtab=modtab, gains=gains, w_bd=w_bd, p_scale=p_scale, sink=sink, x0=x0, h1=h1, z=z,
                          q2=q2, k2=k2, v2=v2, oa=oa, ob=ob, oc=oc, oa32=oa32, oc32=oc32, lse_a=lse_a, lse_c=lse_c,
                          pooled=pooled, y=y, ao=ao,
                          x1=x1, h2=h2, a_pre=a_pre, r_act=r_act, mo=mo))
        xin, pending = x1, (mo, modtab, 5)

    dxo, loss, d_mo, dg2 = _loss_head(xin, pending[0], pending[1], 5, target.reshape(B * S, D), TR=TR, P=P, name="loss_head")

    big = {k: [None] * L for k in BIG_NAMES}
    big16 = {k: [None] * L for k in BIG_NAMES}
    sm = {k: [None] * L for k in ("b_ada", "norm1", "norm2", "q_norm_a", "k_norm_a", "q_norm_c", "k_norm_c",
                                   "sink_c", "w_pool", "pool_scale")}

    def dw(key, l, a, b, *, tm, tn, name, tk=tk_tok, blocked=True):
        outs = _matmul(a, b, "tn", tm=tm, tn=tn, tk=tk, name=name, out_dtypes=(F32, BF16), out_blocked=blocked)
        if not blocked:
            outs = [o.reshape(N_CHIPS, o.shape[0] // N_CHIPS, o.shape[1]) for o in outs]
        big[key][l], big16[key][l] = outs
    d_cctx = jnp.zeros((D,), F32)
    for l in reversed(range(L)):
        W, sv = weights(l), saved[l]
        modtab = sv["modtab"]
        ride_now = overlap and l == L - 2
        if ride_now:
            early = _LayerReduce(l + 1, [big[k][l + 1] for k in BIG_NAMES], [big16[k][l + 1] for k in BIG_NAMES])
        d_a = _matmul(d_mo, W["mlp2"], "nt", tm=TM, tn=tn_ff, tk=D, name=f"mlp2_bwd{l}", out_dtypes=(BF16,),
                      epilogue=lambda acc, a: (acc * (2.0 * jnp.maximum(a, 0.0)),),
                      extras=[(sv["a_pre"], (TM, tn_ff), lambda m, n: (m, n))], rider=early.swap if ride_now else None)
        if ride_now:
            d_a, landed = d_a
            early_send = early.add(landed)
        dw("w_mlp2", l, sv["r_act"], d_mo, tm=tk_d, tn=D, name=f"mlp2_dw{l}", blocked=False)
        d_h2 = _matmul(d_a, W["mlp1"], "nt", tm=TM, tn=D, tk=tn_ff, name=f"mlp1_bwd{l}")
        dw("w_mlp1", l, sv["h2"], d_a, tm=tk_d, tn=tn_ff, name=f"mlp1_dw{l}")
        dx1, dsh2, dsc2, dn2, d_ao, dg1 = _norm_bwd(sv["x1"], d_h2, dxo, modtab, 4, small["norm2"][l][None],
                                                    (sv["ao"], modtab, 2), TR=TR, P=P, name=f"norm2_bwd{l}")
        d_y = _matmul(d_ao, W["out"], "nt", tm=TM, tn=tn_d, tk=D, name=f"out_bwd{l}")
        dw("w_out", l, sv["y"], d_ao, tm=tk_d, tn=D, name=f"out_dw{l}", blocked=False)
        d_pa, d_pb, d_pc, d_ga, d_gb, d_gc = _merge_bwd(d_y, sv["oa"], sv["ob"], sv["oc"], sv["z"], W["a"], W["b"], W["c"],
                                                        D=D, TR=TMG, name=f"merge_bwd{l}")
        d_oa = _matmul(d_pa, W["a"], "nt", tm=TM, tn=Q_WIDTH, tk=tn_d, name=f"br_a_bwd{l}", out_dtypes=(BF16,))
        d_ob = _matmul(d_pb, W["b"], "nt", tm=TM, tn=POOL_WIDTH, tk=tn_d, name=f"br_b_bwd{l}")
        d_oc = _matmul(d_pc, W["c"], "nt", tm=TM, tn=Q_WIDTH, tk=tn_d, name=f"br_c_bwd{l}", out_dtypes=(BF16,))
        dw("w_br_a", l, sv["oa"], d_pa, tm=Q_WIDTH, tn=tn_d, name=f"br_a_dw{l}")
        dw("w_br_b", l, sv["ob"], d_pb, tm=POOL_WIDTH, tn=tn_d, name=f"br_b_dw{l}")
        dw("w_br_c", l, sv["oc"], d_pc, tm=Q_WIDTH, tn=tn_d, name=f"br_c_dw{l}")
        d_u, d_wbd, d_ps = _pool_bwd(d_ob, sv["pooled"], sv["w_bd"], sv["p_scale"], B=B, Tp=Tp, n_ctx=N, name=f"pool_bwd{l}")
        dqa, dka, dva, *arrived = _attn_bwd(sv["q2"], sv["k2"], sv["v2"], d_oa, sv["oa32"], sv["lse_a"], None, branch=0,
                                            B=B, n_ctx=N, window=False, name=f"attn_a_bwd{l}",
                                            rider=early_send if ride_now else None)
        if ride_now:
            early.from_chips = arrived
        dqc, dkc, dvc, dsink = _attn_bwd(sv["q2"], sv["k2"], sv["v2"], d_oc, sv["oc32"], sv["lse_c"], sv["sink"],
                                         branch=1, B=B, n_ctx=N, window=True, name=f"attn_c_bwd{l}")
        dz_a, dgains_a = _qk_prep_bwd(sv["z"], dqa, dka, dva, sv["gains"], cos, sin, branch=0, TR=TR, P=P,
                                      name=f"qk_prep_a_bwd{l}")
        dz_c, dgains_c = _qk_prep_bwd(sv["z"], dqc, dkc, dvc, sv["gains"], cos, sin, branch=1, TR=TR, P=P,
                                      name=f"qk_prep_c_bwd{l}")
        dz = jnp.concatenate([dz_a, dz_c, d_u, d_ga, d_gb, d_gc], axis=1)
        d_h1 = _matmul(dz, W["w_in"], "nt", tm=TM, tn=D, tk=tn_in, name=f"in_bwd{l}")
        dw("w_in", l, sv["h1"], dz, tm=tk_d, tn=tn_in, name=f"in_dw{l}")
        below = (saved[l - 1]["mo"], saved[l - 1]["modtab"], 5) if l > 0 else None
        dx0, dsh1, dsc1, dn1, *lower = _norm_bwd(sv["x0"], d_h1, dx1, modtab, 1, small["norm1"][l][None], below,
                                                 TR=TR, P=P, name=f"norm1_bwd{l}")
        this_dg2 = dg2
        if l > 0:
            d_mo, dg2 = lower

        dm_groups = jnp.concatenate([dsh1, dsc1, dg1, dsh2, dsc2, this_dg2], axis=-1).reshape(B, 2, 6 * D)
        dm = jnp.concatenate([dm_groups[:, 1], jnp.sum(dm_groups[:, 0], axis=0, keepdims=True),
                              jnp.zeros((rows16 - B - 1, 6 * D), F32)], axis=0)
        dm_bf = dm.astype(BF16)
        d_s = _matmul(dm_bf, W["ada"], "nt", tm=rows16, tn=D, tk=tn_ada, name=f"ada_bwd{l}")
        dw("w_ada", l, s_rows, dm_bf, tm=tk_d, tn=tn_ada, tk=rows16, name=f"ada_dw{l}")
        db_ada, dcc = _ada_bwd_rows(dm, d_s, cc, f"ada_rows_bwd{l}")
        d_cctx = d_cctx + dcc[B]

        sm["b_ada"][l] = db_ada[0]
        sm["norm1"][l] = jnp.sum(dn1, axis=(0, 1))
        sm["norm2"][l] = jnp.sum(dn2, axis=(0, 1))
        dgh = jnp.stack([dgains_a, dgains_c]).reshape(2, QKV_WIDTH // HEAD_DIM, HEAD_DIM)
        sm["q_norm_a"][l] = jnp.sum(dgh[0, :N_HEADS], axis=0)
        sm["k_norm_a"][l] = jnp.sum(dgh[0, N_HEADS:N_HEADS + N_KV], axis=0)
        sm["q_norm_c"][l] = jnp.sum(dgh[1, :N_HEADS], axis=0)
        sm["k_norm_c"][l] = jnp.sum(dgh[1, N_HEADS:N_HEADS + N_KV], axis=0)
        sm["sink_c"][l] = jnp.sum(dsink[:, :N_HEADS, 0], axis=0)
        sm["w_pool"][l] = jnp.stack([d_wbd[g * POOL_CH:(g + 1) * POOL_CH, g * POOL_CH:(g + 1) * POOL_CH]
                                     for g in range(POOL_WIDTH // POOL_CH)])
        sm["pool_scale"][l] = d_ps[0]
        dxo = dx0

    grad_x = dxo.reshape(B, Tp, D)[:, N:]
    small_grads = {k: jnp.stack(v) for k, v in sm.items()}
    small_grads["c_ctx"] = d_cctx
    return loss, grad_x, small_grads, big, big16, (early if overlap else None)


SMALL_NAMES = ("c_ctx", "b_ada", "norm1", "norm2", "q_norm_a", "k_norm_a", "q_norm_c", "k_norm_c", "sink_c",
               "w_pool", "pool_scale")
BIG_NAMES = ("w_ada", "w_in", "w_br_a", "w_br_b", "w_br_c", "w_out", "w_mlp1", "w_mlp2")
WEIGHT_NAMES = ("c_ctx", "w_ada", "b_ada", "norm1", "norm2", "w_in", "q_norm_a", "k_norm_a", "q_norm_c", "k_norm_c",
                "sink_c", "w_pool", "pool_scale", "w_br_a", "w_br_b", "w_br_c", "w_out", "w_mlp1", "w_mlp2")


def _pack(parts, rows):
    flat = jnp.concatenate([p.reshape(-1).astype(F32) for p in parts])
    return jnp.pad(flat, (0, rows * LANES - flat.shape[0])).reshape(rows, LANES)


def _unpack(packed, like):
    flat, out, at = packed.reshape(-1), [], 0
    for p in like:
        out.append(flat[at:at + p.size].reshape(p.shape))
        at += p.size
    return out


class _LayerReduce:
    def __init__(self, layer, partials, partials16):
        self.layer, self.partials = layer, list(partials)
        self.swap = _SiblingSwap([g.reshape(-1, g.shape[-1]) for g in partials16])

    def add(self, landed):
        sums = [_add_landed(g.reshape(-1, g.shape[-1]), r, f"grads{self.layer}_add_sibling_{k}")
                for k, g, r in zip(BIG_NAMES, self.partials, landed)]
        self.in_chip = [h.reshape(g.shape) for g, (h, _) in zip(self.partials, sums)]
        return _ChipSend([h.reshape(g.shape) for g, (_, h) in zip(self.partials, sums)], self.layer)

    def sum(self, from_chips, chip):
        return [_sum_chips(h, r, chip, f"grads{self.layer}_sum_chips_{k}")
                for k, h, r in zip(BIG_NAMES, self.in_chip, from_chips)]


def kernel(x, c, ctx, c_ctx, w_ada, b_ada, norm1, norm2, w_in, q_norm_a, k_norm_a, q_norm_c, k_norm_c, sink_c, w_pool, pool_scale, w_br_a, w_br_b, w_br_c, w_out, w_mlp1, w_mlp2, loss_target, m_c_ctx, m_w_ada, m_b_ada, m_norm1, m_norm2, m_w_in, m_q_norm_a, m_k_norm_a, m_q_norm_c, m_k_norm_c, m_sink_c, m_w_pool, m_pool_scale, m_w_br_a, m_w_br_b, m_w_br_c, m_w_out, m_w_mlp1, m_w_mlp2, v_c_ctx, v_w_ada, v_b_ada, v_norm1, v_norm2, v_w_in, v_q_norm_a, v_k_norm_a, v_q_norm_c, v_k_norm_c, v_sink_c, v_w_pool, v_pool_scale, v_w_br_a, v_w_br_b, v_w_br_c, v_w_out, v_w_mlp1, v_w_mlp2):
    given = dict(locals())
    w = {k: given[k] for k in WEIGHT_NAMES}
    m = {k: given["m_" + k] for k in WEIGHT_NAMES}
    v = {k: given["v_" + k] for k in WEIGHT_NAMES}

    shards = [w[k].astype(BF16) for k in BIG_NAMES]
    assert all(s.shape[0] == 2 for s in shards)
    first_layer = dict(zip(BIG_NAMES, _ride_alone(_LayerGather(shards, 0), "gather_weights0")))
    small = {k: w[k] for k in SMALL_NAMES}
    loss_part, grad_x, small_grads, big_grads, big_grads16, early = _local_step(
        x, c, ctx, c_ctx, small, [first_layer, None], loss_target, rider=_LayerGather(shards, 1), overlap=True)

    px, py, pc = _place()
    core = pc.astype(jnp.int32).reshape(1)
    chip = (2 * px + py).astype(jnp.int32).reshape(1)
    late = _LayerReduce(0, [big_grads[k][0] for k in BIG_NAMES], [big_grads16[k][0] for k in BIG_NAMES])
    send = late.add(_ride_alone(late.swap, "grads0_to_sibling"))
    reduced = [late.sum(_ride_alone(send, "grads0_to_chips"), chip), early.sum(early.from_chips, chip)]
    others = _send_other_layer(reduced[1], reduced[0], "grads_share_layers")
    grads, deltas, new_m, new_v = {}, {}, {}, {}
    for i, k in enumerate(BIG_NAMES):
        grads[k], deltas[k], new_m[k], new_v[k] = _adamw(w[k], (reduced[0][i], reduced[1][i]), others[i], m[k], v[k],
                                                         core, f"adamw_{k}")

    sizes = sum(w[k].size for k in SMALL_NAMES) + LANES
    rows = -(-sizes // (8 * LANES)) * 8
    parts = _gather_small(_pack([small_grads[k] for k in SMALL_NAMES] + [loss_part[0]], rows), "gather_small")
    zero = jnp.zeros((LANES,), F32)
    packed = [_pack([t[k] for k in SMALL_NAMES] + [zero], rows) for t in (w, m, v)]
    outs = _adamw_small(packed[0], parts.reshape(8, rows, LANES), packed[1], packed[2], "adamw_small")
    like = [w[k] for k in SMALL_NAMES] + [zero]
    for store, packed_out in zip((grads, deltas, new_m, new_v), outs):
        pieces = _unpack(packed_out, like)
        for k, piece in zip(SMALL_NAMES, pieces):
            store[k] = piece
        if store is grads:
            loss = pieces[-1][0]

    return (loss, grad_x, *[grads[k] for k in WEIGHT_NAMES], *[deltas[k] for k in WEIGHT_NAMES],
            *[new_m[k] for k in WEIGHT_NAMES], *[new_v[k] for k in WEIGHT_NAMES])
```

```python
import functools

import jax
import jax.numpy as jnp
from jax import lax
from jax.experimental import pallas as pl
from jax.experimental.pallas import tpu as pltpu

F32 = jnp.float32
BF16 = jnp.bfloat16

HEAD_DIM = 64
GRID_W = 64
AXIS_DIM = HEAD_DIM // 2
ROPE_THETA = 10000.0
N_HEADS = 6
N_KV = 2
N_GROUP = N_HEADS // N_KV
POOL_CH = 64
POOL_WIDTH = 256
POOL_WINDOWS = (2, 4, 8, 16)
WINDOW = 128
Q_BLOCK = 128
Q_WIDTH = N_HEADS * HEAD_DIM
KV_WIDTH = N_KV * HEAD_DIM
GATE_COL = 2 * (Q_WIDTH + 2 * KV_WIDTH) + POOL_WIDTH
U_COL = 2 * (Q_WIDTH + 2 * KV_WIDTH)
EPS = 1e-6
NEG = -1e30
ADAM_LR = 0.001
ADAM_B1 = 0.9
ADAM_B2 = 0.999
ADAM_EPS = 1e-08
ADAM_WD = 0.01
ADAM_STEP = 10

N_CHIPS = 4
LANES = 128
POOL_PAD = 16
VMEM_LIMIT = 48 * 1024 * 1024
MESH = pl.DeviceIdType.MESH
ANY = pl.BlockSpec(memory_space=pl.ANY)


def _params(sem):
    return pltpu.CompilerParams(dimension_semantics=sem, vmem_limit_bytes=VMEM_LIMIT)


def _sds(shape, dtype):
    return jax.ShapeDtypeStruct(tuple(shape), dtype)


class _Opnd:
    def __init__(self, arr, kind="plain"):
        self.arr, self.kind = arr, kind

    @property
    def shape(self):
        a = self.arr
        if self.kind == "plain":
            return a.shape
        if self.kind == "bcols":
            return (a.shape[1], N_CHIPS * a.shape[2])
        return (N_CHIPS * a.shape[1], a.shape[2])

    def spec(self, tr, tc, fn):
        a = self.arr
        if self.kind == "plain":
            return pl.BlockSpec((tr, tc), lambda *g: fn(*g))
        if self.kind == "bcols":
            assert a.shape[2] % tc == 0, (a.shape, tc)
            per = a.shape[2] // tc

            def im(*g):
                ri, ci = fn(*g)
                return (ci // per, ri, ci % per)
            return pl.BlockSpec((None, tr, tc), im)
        assert a.shape[1] % tr == 0, (a.shape, tr)
        per = a.shape[1] // tr

        def im(*g):
            ri, ci = fn(*g)
            return (ri // per, ri % per, ci)
        return pl.BlockSpec((None, tr, tc), im)


def _matmul(a, b, mode, *, tm, tn, tk, name, out_dtypes=(F32,), epilogue=None, extras=(), out_blocked=False, rider=None):
    if not isinstance(a, _Opnd):
        a = _Opnd(a)
    if not isinstance(b, _Opnd):
        b = _Opnd(b)
    if mode == "nn":
        (M, K), (K2, N) = a.shape, b.shape
        a_spec = a.spec(tm, tk, lambda m, n, k: (m, k))
        b_spec = b.spec(tk, tn, lambda m, n, k: (k, n))
        dims = (((1,), (0,)), ((), ()))
    elif mode == "nt":
        (M, K), (N, K2) = a.shape, b.shape
        a_spec = a.spec(tm, tk, lambda m, n, k: (m, k))
        b_spec = b.spec(tn, tk, lambda m, n, k: (n, k))
        dims = (((1,), (1,)), ((), ()))
    else:
        (K, M), (K2, N) = a.shape, b.shape
        a_spec = a.spec(tk, tm, lambda m, n, k: (k, m))
        b_spec = b.spec(tk, tn, lambda m, n, k: (k, n))
        dims = (((0,), (0,)), ((), ()))
    assert K == K2 and M % tm == 0 and N % tn == 0 and K % tk == 0, (name, M, N, K, K2, tm, tn, tk)
    nk = K // tk
    n_extra = len(extras)
    n_out = len(out_dtypes)
    extra_specs = [pl.BlockSpec(bs, functools.partial(lambda m, n, k, f: f(m, n), f=f)) for (_, bs, f) in extras]
    if out_blocked:
        assert (N // N_CHIPS) % tn == 0
        per = (N // N_CHIPS) // tn
        out_shape = [_sds((N_CHIPS, M, N // N_CHIPS), dt) for dt in out_dtypes]
        out_specs = [pl.BlockSpec((None, tm, tn), lambda m, n, k: (n // per, m, n % per)) for _ in out_dtypes]
    else:
        out_shape = [_sds((M, N), dt) for dt in out_dtypes]
        out_specs = [pl.BlockSpec((tm, tn), lambda m, n, k: (m, n)) for _ in out_dtypes]

    in_place = nk > 1 and epilogue is None and out_dtypes[0] == F32

    grid = (M // tm, N // tn, nk)
    own_scratch = [pltpu.VMEM((tm, tn), F32)] if nk > 1 and not in_place else []

    def body(*refs):
        refs, finish_ride = _ride(rider, refs, 2 + n_extra, n_out, len(own_scratch), grid)
        a_ref, b_ref = refs[0], refs[1]
        extra_refs = refs[2:2 + n_extra]
        out_refs = refs[2 + n_extra:2 + n_extra + n_out]
        acc_ref = out_refs[0] if in_place else (refs[2 + n_extra + n_out] if nk > 1 else None)
        k = pl.program_id(2)
        prod = lax.dot_general(a_ref[...].astype(BF16), b_ref[...].astype(BF16), dims, preferred_element_type=F32)

        def finish(acc):
            outs = epilogue(acc, *[r[...] for r in extra_refs]) if epilogue is not None else (acc,) * n_out
            for o_ref, o in zip(out_refs, outs):
                o_ref[...] = o.astype(o_ref.dtype)

        if nk == 1:
            finish(prod)
        elif in_place:
            @pl.when(k == 0)
            def _():
                acc_ref[...] = prod

            @pl.when(k > 0)
            def _():
                acc_ref[...] += prod

            if n_out > 1:
                @pl.when(k == nk - 1)
                def _():
                    for o_ref in out_refs[1:]:
                        o_ref[...] = acc_ref[...].astype(o_ref.dtype)
        else:
            @pl.when(k == 0)
            def _():
                acc_ref[...] = prod

            @pl.when(k > 0)
            def _():
                acc_ref[...] += prod

            @pl.when(k == nk - 1)
            def _():
                finish(acc_ref[...])

        finish_ride()

    ins, in_specs, out_specs, out_shape, scratch = _hitch(
        rider, [a.arr, b.arr] + [e[0] for e in extras], [a_spec, b_spec] + extra_specs, out_specs, out_shape, own_scratch)
    outs = pl.pallas_call(
        body, name=name, grid=grid, in_specs=in_specs, out_specs=out_specs, out_shape=out_shape, scratch_shapes=scratch,
        compiler_params=_params(("arbitrary",) * 3 if rider is not None else ("parallel", "parallel", "arbitrary")),
    )(*ins)
    if rider is not None:
        return (outs[0] if n_out == 1 else outs[:n_out]), outs[n_out:]
    return outs[0] if n_out == 1 else outs


def _tile(n, cands):
    for t in cands:
        if n % t == 0:
            return t
    return n


def _grp(i, P):
    return 2 * (i // P) + jnp.minimum(i % P, 1)


def _mod_spec(D, P, part):
    return pl.BlockSpec((1, 1, D), lambda i: (_grp(i, P), 0, part))


def _res_norm(x, pending, modtab, shift_part, scale_part, gain, *, TR, P, name):
    T, D = x.shape
    row = pl.BlockSpec((TR, D), lambda i: (i, 0))
    has_branch = pending is not None
    ins, specs = [x], [row]
    if has_branch:
        branch, gate_tab, gate_part = pending
        ins += [branch, gate_tab]
        specs += [row, _mod_spec(D, P, gate_part)]
    ins += [modtab, modtab, gain]
    specs += [_mod_spec(D, P, shift_part), _mod_spec(D, P, scale_part), pl.BlockSpec((1, D), lambda i: (0, 0))]

    def body(*refs):
        if has_branch:
            x_ref, br_ref, g_ref, sh_ref, sc_ref, gn_ref, xo_ref, h_ref = refs
            xv = x_ref[...] + g_ref[0] * br_ref[...]
        else:
            x_ref, sh_ref, sc_ref, gn_ref, xo_ref, h_ref = refs
            xv = x_ref[...]
        xo_ref[...] = xv
        y = xv * lax.rsqrt(jnp.mean(xv * xv, axis=-1, keepdims=True) + EPS) * gn_ref[...]
        h_ref[...] = (y * (1.0 + sc_ref[0]) + sh_ref[0]).astype(BF16)

    return pl.pallas_call(
        body, name=name, grid=(T // TR,), in_specs=specs, out_specs=[row, row],
        out_shape=[_sds((T, D), F32), _sds((T, D), BF16)], compiler_params=_params(("parallel",)),
    )(*ins)


def _norm_bwd(x, dh, dres, modtab, scale_part, gain, below, *, TR, P, name):
    T, D = x.shape
    G = modtab.shape[0]
    row = pl.BlockSpec((TR, D), lambda i: (i, 0))
    acc = pl.BlockSpec((1, 1, D), lambda i: (_grp(i, P), 0, 0))
    has_below = below is not None

    def body(*refs):
        if has_below:
            x_ref, dh_ref, dres_ref, sc_ref, gn_ref, br_ref, g_ref, dx_ref, dsh_ref, dsc_ref, dgn_ref, db_ref, dg_ref = refs
        else:
            x_ref, dh_ref, dres_ref, sc_ref, gn_ref, dx_ref, dsh_ref, dsc_ref, dgn_ref = refs
        r = pl.program_id(0) % P
        xv, dhv, gn = x_ref[...], dh_ref[...], gn_ref[...]
        rstd = lax.rsqrt(jnp.mean(xv * xv, axis=-1, keepdims=True) + EPS)
        xhat = xv * rstd
        dn = dhv * (1.0 + sc_ref[0])
        dxhat = dn * gn
        dxv = dres_ref[...] + rstd * (dxhat - xhat * jnp.mean(dxhat * xhat, axis=-1, keepdims=True))
        dx_ref[...] = dxv
        parts = [jnp.sum(dhv, axis=0, keepdims=True), jnp.sum(dhv * (xhat * gn), axis=0, keepdims=True),
                 jnp.sum(dn * xhat, axis=0, keepdims=True)]
        outs = [dsh_ref, dsc_ref, dgn_ref]
        if has_below:
            db_ref[...] = (dxv * g_ref[0]).astype(BF16)
            parts.append(jnp.sum(dxv * br_ref[...], axis=0, keepdims=True))
            outs.append(dg_ref)

        @pl.when(r <= 1)
        def _():
            for o_ref, part in zip(outs, parts):
                o_ref[0] = part

        @pl.when(r > 1)
        def _():
            for o_ref, part in zip(outs, parts):
                o_ref[0] += part

    ins = [x, dh, dres, modtab, gain]
    in_specs = [row, row, row, _mod_spec(D, P, scale_part), pl.BlockSpec((1, D), lambda i: (0, 0))]
    out_specs, out_shape = [row, acc, acc, acc], [_sds((T, D), F32)] + [_sds((G, 1, D), F32)] * 3
    if has_below:
        branch, gate_tab, gate_part = below
        ins += [branch, gate_tab]
        in_specs += [row, _mod_spec(D, P, gate_part)]
        out_specs += [row, acc]
        out_shape += [_sds((T, D), BF16), _sds((G, 1, D), F32)]
    return pl.pallas_call(body, name=name, grid=(T // TR,), in_specs=in_specs, out_specs=out_specs, out_shape=out_shape,
                          compiler_params=_params(("arbitrary",)))(*ins)


def _loss_head(x, branch, modtab, gate_part, target, *, TR, P, name):
    T, D = x.shape
    row = pl.BlockSpec((TR, D), lambda i: (i, 0))
    tgt = pl.BlockSpec((TR, D), lambda i: ((i // P) * (P - 1) + jnp.maximum(i % P - 1, 0), 0))
    one = pl.BlockSpec((1, LANES), lambda i: (0, 0))

    G = modtab.shape[0]
    acc = pl.BlockSpec((1, 1, D), lambda i: (_grp(i, P), 0, 0))

    def body(x_ref, br_ref, g_ref, t_ref, dy_ref, loss_ref, db_ref, dg_ref):
        i = pl.program_id(0)
        r = i % P

        @pl.when(i == 0)
        def _():
            loss_ref[...] = jnp.zeros_like(loss_ref)

        @pl.when(r == 0)
        def _():
            dy_ref[...] = jnp.zeros_like(dy_ref)
            db_ref[...] = jnp.zeros_like(db_ref)
            dg_ref[...] = jnp.zeros_like(dg_ref)

        @pl.when(r > 0)
        def _():
            brv, g = br_ref[...], g_ref[0]
            err = x_ref[...] + g * brv - t_ref[...]
            dy = err / D
            dy_ref[...] = dy
            db_ref[...] = (dy * g).astype(BF16)
            part = jnp.sum(dy * brv, axis=0, keepdims=True)
            per_tok = jnp.mean(err * err, axis=-1, keepdims=True)
            loss_ref[...] += 0.5 * jnp.sum(per_tok, axis=0, keepdims=True)

            @pl.when(r == 1)
            def _():
                dg_ref[0] = part

            @pl.when(r > 1)
            def _():
                dg_ref[0] += part

    return pl.pallas_call(
        body, name=name, grid=(T // TR,), in_specs=[row, row, _mod_spec(D, P, gate_part), tgt],
        out_specs=[row, one, row, acc],
        out_shape=[_sds((T, D), F32), _sds((1, LANES), F32), _sds((T, D), BF16), _sds((G, 1, D), F32)],
        compiler_params=_params(("arbitrary",)),
    )(x, branch, modtab, target)


QKV_WIDTH = Q_WIDTH + 2 * KV_WIDTH
QK_NORMED = 4


def _seg_mean(v):
    lane = lax.broadcasted_iota(jnp.int32, v.shape, 1)
    lo = lane < HEAD_DIM
    s0 = jnp.sum(jnp.where(lo, v, 0.0), axis=-1, keepdims=True)
    s1 = jnp.sum(jnp.where(lo, 0.0, v), axis=-1, keepdims=True)
    return jnp.where(lo, s0, s1) * (1.0 / HEAD_DIM)


def _pair_swap(v):
    lane = lax.broadcasted_iota(jnp.int32, v.shape, 1)
    return jnp.where((lane & 1) == 0, pltpu.roll(v, LANES - 1, 1), pltpu.roll(v, 1, 1))


def _chunk(c):
    return slice(c * LANES, (c + 1) * LANES)


def _qk_prep(z, gains, cos, sin, *, TR, P, name):
    T = z.shape[0]

    def body(z_ref, g_ref, c_ref, s_ref, q_ref, k_ref, v_ref):
        cs, sn = c_ref[...], s_ref[...]
        for ch in range(QK_NORMED):
            xv = z_ref[:, _chunk(ch)]
            y = xv * lax.rsqrt(_seg_mean(xv * xv) + EPS) * g_ref[0, :, _chunk(ch)]
            out = (y * cs + _pair_swap(y) * sn).astype(BF16)
            if ch < QK_NORMED - 1:
                q_ref[:, _chunk(ch)] = out
            else:
                k_ref[...] = out
        v_ref[...] = z_ref[:, _chunk(QK_NORMED)].astype(BF16)

    def out(width):
        return pl.BlockSpec((None, TR, width), lambda i, j: (j, i, 0))
    return pl.pallas_call(
        body, name=name, grid=(T // TR, 2),
        in_specs=[pl.BlockSpec((TR, QKV_WIDTH), lambda i, j: (i, j)),
                  pl.BlockSpec((1, 1, QKV_WIDTH), lambda i, j: (j, 0, 0)),
                  pl.BlockSpec((TR, LANES), lambda i, j: (i % P, 0)),
                  pl.BlockSpec((TR, LANES), lambda i, j: (i % P, 0))],
        out_specs=[out(Q_WIDTH), out(KV_WIDTH), out(KV_WIDTH)],
        out_shape=[_sds((2, T, Q_WIDTH), BF16), _sds((2, T, KV_WIDTH), BF16), _sds((2, T, KV_WIDTH), BF16)],
        compiler_params=_params(("parallel", "parallel")),
    )(z, gains, cos, sin)


def _qk_prep_bwd(z, dq, dk, dv, gains, cos, sin, *, branch, TR, P, name):
    T = z.shape[0]
    nt = T // TR

    def body(z_ref, dq_ref, dk_ref, dv_ref, g_ref, c_ref, s_ref, dz_ref, dg_ref):
        i = pl.program_id(0)
        cs, sn = c_ref[...], s_ref[...]
        parts = []
        for ch in range(QK_NORMED):
            xv, g = z_ref[:, _chunk(ch)], g_ref[0, :, _chunk(ch)]
            dout = dq_ref[:, _chunk(ch)] if ch < QK_NORMED - 1 else dk_ref[...]
            dy = dout * cs + _pair_swap(dout * sn)
            rstd = lax.rsqrt(_seg_mean(xv * xv) + EPS)
            xhat = xv * rstd
            dxhat = dy * g
            dz_ref[:, _chunk(ch)] = (rstd * (dxhat - xhat * _seg_mean(dxhat * xhat))).astype(BF16)
            parts.append(jnp.sum(dy * xhat, axis=0, keepdims=True))
        dz_ref[:, _chunk(QK_NORMED)] = dv_ref[...].astype(BF16)
        parts.append(jnp.zeros((1, LANES), F32))
        part = jnp.concatenate(parts, axis=1)

        @pl.when(i == 0)
        def _():
            dg_ref[0] = part

        @pl.when(i > 0)
        def _():
            dg_ref[0] += part

    def rows(width, col=0):
        return pl.BlockSpec((TR, width), lambda i: (i, col))
    return pl.pallas_call(
        body, name=name, grid=(nt,),
        in_specs=[rows(QKV_WIDTH, branch), rows(Q_WIDTH), rows(KV_WIDTH), rows(KV_WIDTH),
                  pl.BlockSpec((1, 1, QKV_WIDTH), lambda i: (branch, 0, 0)),
                  pl.BlockSpec((TR, LANES), lambda i: (i % P, 0)),
                  pl.BlockSpec((TR, LANES), lambda i: (i % P, 0))],
        out_specs=[rows(QKV_WIDTH), pl.BlockSpec((1, 1, QKV_WIDTH), lambda i: (0, 0, 0))],
        out_shape=[_sds((T, QKV_WIDTH), BF16), _sds((1, 1, QKV_WIDTH), F32)],
        compiler_params=_params(("arbitrary",)),
    )(z, dq, dk, dv, gains, cos, sin)


NT_DIMS = (((1,), (1,)), ((), ()))
TN_DIMS = (((0,), (0,)), ((), ()))
QROWS = N_GROUP * Q_BLOCK
SCORE_SCALE = HEAD_DIM ** -0.5
BAND = Q_BLOCK + 2 * WINDOW
FWD_LATENT_CHUNK = 256
BWD_LATENT_CHUNK = 1024


def _move_head(block, half_from, half_to):
    lane = lax.broadcasted_iota(jnp.int32, block.shape, 1)
    src = block if half_from == half_to else pltpu.roll(block, HEAD_DIM, 1)
    keep = (lane < HEAD_DIM) if half_to == 0 else (lane >= HEAD_DIM)
    return jnp.where(keep, src, 0.0)


def _stack_heads(lane_block, j):
    pieces = []
    for h in range(N_GROUP * j, N_GROUP * (j + 1)):
        pieces.append(_move_head(lane_block(h // 2), h % 2, j))
    return jnp.concatenate(pieces, axis=0)


def _lane_blocks(ref):
    return lambda m: ref[:, m * LANES:(m + 1) * LANES].astype(F32)


def _unstack_heads(stacked, ref):
    heads = []
    for h in range(N_HEADS):
        j, r = h // N_GROUP, h % N_GROUP
        heads.append(_move_head(stacked[j][r * Q_BLOCK:(r + 1) * Q_BLOCK], j, h % 2))
    for m in range(N_HEADS // 2):
        ref[:, m * LANES:(m + 1) * LANES] = (heads[2 * m] + heads[2 * m + 1]).astype(ref.dtype)


def _key_chunks(i, latent, *, n_ctx, t_all, window, chunk, latent_chunk):
    ctx = [(s, chunk, False) for s in range(0, n_ctx, chunk)]
    if not latent:
        return ctx
    if not window:
        wide = latent_chunk if (t_all - n_ctx) % latent_chunk == 0 else chunk
        return ctx + [(s, wide, False) for s in range(n_ctx, t_all, wide)]
    start = pl.multiple_of(jnp.minimum((i - 1) * Q_BLOCK, t_all - BAND), Q_BLOCK)
    band_chunk = BAND if latent_chunk >= BAND else (chunk if BAND % chunk == 0 else Q_BLOCK)
    return ctx + [(start + s, band_chunk, True) for s in range(0, BAND, band_chunk)]


def _scores(q, k_ref, i, start, size, masked, *, n_ctx):
    s = lax.dot_general(q, k_ref[pl.ds(start, size), :], NT_DIMS, preferred_element_type=F32)
    if masked:
        qpos = (i * Q_BLOCK - n_ctx) + (lax.broadcasted_iota(jnp.int32, (QROWS, size), 0) & (Q_BLOCK - 1))
        kpos = (start - n_ctx) + lax.broadcasted_iota(jnp.int32, (QROWS, size), 1)
        valid = (kpos - qpos <= WINDOW) & (qpos - kpos <= WINDOW) & (kpos >= 0)
        s = jnp.where(valid, s, NEG)
    return s


def _sink_column(sink_ref, j):
    r = lax.broadcasted_iota(jnp.int32, (QROWS, 1), 0)
    s0, s1, s2 = sink_ref[j * N_GROUP], sink_ref[j * N_GROUP + 1], sink_ref[j * N_GROUP + 2]
    return jnp.where(r < Q_BLOCK, s0, jnp.where(r < 2 * Q_BLOCK, s1, s2))


def _attn_specs(Tp, branch):
    nq = Tp // Q_BLOCK
    q_in = pl.BlockSpec((None, Q_BLOCK, Q_WIDTH), lambda b, i: (branch, b * nq + i, 0))
    kv_in = pl.BlockSpec((None, Tp, KV_WIDTH), lambda b, i: (branch, b, 0))
    q_out = pl.BlockSpec((Q_BLOCK, Q_WIDTH), lambda b, i: (b * nq + i, 0))
    kv_out = pl.BlockSpec((Tp, KV_WIDTH), lambda b, i: (b, 0))
    return q_in, kv_in, q_out, kv_out


def _attn_chunk(Tp):
    return 256 if Tp % 256 == 0 else Q_BLOCK


def _attn_fwd(q, k, v, sink, *, branch, B, n_ctx, window, name, rider=None):
    T = q.shape[1]
    Tp = T // B
    nq = Tp // Q_BLOCK
    has_sink = sink is not None
    n_in = 4 if has_sink else 3
    q_in, kv_in, q_out, _ = _attn_specs(Tp, branch)
    lse_spec = pl.BlockSpec((None, N_KV * QROWS, 1), lambda b, i: (b * nq + i, 0, 0))

    def body(*refs):
        refs, finish_ride = _ride(rider, refs, n_in, 3, 0, (B, nq))
        sink_ref = refs.pop(0) if has_sink else None
        q_ref, k_ref, v_ref, o_ref, o32_ref, lse_ref = refs
        i = pl.program_id(1)

        def run(latent):
            outs = []
            for j in range(N_KV):
                qv = (_stack_heads(_lane_blocks(q_ref), j) * SCORE_SCALE).astype(BF16)
                if has_sink:
                    m, l = _sink_column(sink_ref, j), jnp.ones((QROWS, 1), F32)
                else:
                    m, l = jnp.full((QROWS, 1), NEG, F32), jnp.zeros((QROWS, 1), F32)
                acc = jnp.zeros((QROWS, LANES), F32)
                for start, size, masked in _key_chunks(i, latent, n_ctx=n_ctx, t_all=Tp, window=window,
                                                       chunk=_attn_chunk(Tp), latent_chunk=FWD_LATENT_CHUNK):
                    s = _scores(qv, k_ref, i, start, size, masked, n_ctx=n_ctx)
                    m_new = jnp.maximum(m, jnp.max(s, axis=-1, keepdims=True))
                    alpha = jnp.exp(m - m_new)
                    p = jnp.exp(s - m_new)
                    l = l * alpha + jnp.sum(p, axis=-1, keepdims=True)
                    acc = acc * alpha + jnp.dot(p.astype(BF16), v_ref[pl.ds(start, size), :], preferred_element_type=F32)
                    m = m_new
                outs.append(acc * (1.0 / l))
                lse_ref[j * QROWS:(j + 1) * QROWS, :] = m + jnp.log(l)
            _unstack_heads(outs, o_ref)
            _unstack_heads(outs, o32_ref)

        @pl.when(i < n_ctx // Q_BLOCK)
        def _():
            run(False)

        @pl.when(i >= n_ctx // Q_BLOCK)
        def _():
            run(True)

        finish_ride()

    ins, specs = [q, k, v], [q_in, kv_in, kv_in]
    if has_sink:
        ins, specs = [sink] + ins, [pl.BlockSpec(memory_space=pltpu.SMEM)] + specs
    out_specs = [q_out, q_out, lse_spec]
    out_shape = [_sds((T, Q_WIDTH), BF16), _sds((T, Q_WIDTH), F32), _sds((T // Q_BLOCK, N_KV * QROWS, 1), F32)]
    ins, specs, out_specs, out_shape, scratch = _hitch(rider, ins, specs, out_specs, out_shape, [])
    return pl.pallas_call(
        body, name=name, grid=(B, nq), in_specs=specs, out_specs=out_specs, out_shape=out_shape, scratch_shapes=scratch,
        compiler_params=_params(("arbitrary", "arbitrary") if rider is not None else ("parallel", "parallel")),
    )(*ins)


def _attn_bwd(q, k, v, do, o32, lse, sink, *, branch, B, n_ctx, window, name, rider=None):
    T = q.shape[1]
    Tp = T // B
    nq = Tp // Q_BLOCK
    has_sink = sink is not None
    q_in, kv_in, q_out, kv_out = _attn_specs(Tp, branch)
    lse_spec = pl.BlockSpec((None, N_KV * QROWS, 1), lambda b, i: (b * nq + i, 0, 0))
    sink_spec = pl.BlockSpec((None, 8, LANES), lambda b, i: (b, 0, 0))

    def body(*refs):
        refs, finish_ride = _ride(rider, refs, 7 if has_sink else 6, 4 if has_sink else 3, 2, (B, nq))
        if has_sink:
            sink_ref, q_ref, k_ref, v_ref, do_ref, o_ref, lse_ref, dq_ref, dk_ref, dv_ref, ds_ref, dkt_ref, dvt_ref = refs
        else:
            q_ref, k_ref, v_ref, do_ref, o_ref, lse_ref, dq_ref, dk_ref, dv_ref, dkt_ref, dvt_ref = refs
        i = pl.program_id(1)

        @pl.when(i == 0)
        def _():
            dk_ref[...] = jnp.zeros_like(dk_ref)
            dv_ref[...] = jnp.zeros_like(dv_ref)
            if not window:
                dkt_ref[...] = jnp.zeros_like(dkt_ref)
                dvt_ref[...] = jnp.zeros_like(dvt_ref)
            if has_sink:
                ds_ref[...] = jnp.zeros_like(ds_ref)

        def run(latent):
            upd = jnp.zeros((8, LANES), F32)
            do_blocks, o_blocks = _lane_blocks(do_ref), _lane_blocks(o_ref)
            qvs = [(_stack_heads(_lane_blocks(q_ref), j) * SCORE_SCALE).astype(BF16) for j in range(N_KV)]
            dovs = [_stack_heads(do_blocks, j).astype(BF16) for j in range(N_KV)]
            deltas = [jnp.sum(_stack_heads(lambda m: do_blocks(m) * o_blocks(m), j), axis=-1, keepdims=True)
                      for j in range(N_KV)]
            lses = [lse_ref[j * QROWS:(j + 1) * QROWS, :] for j in range(N_KV)]
            q_all, do_all = jnp.concatenate(qvs, axis=0), jnp.concatenate(dovs, axis=0)
            q_all_t, do_all_t = q_all.T, do_all.T
            dqs = [jnp.zeros((QROWS, LANES), F32) for _ in range(N_KV)]
            for start, size, masked in _key_chunks(i, latent, n_ctx=n_ctx, t_all=Tp, window=window,
                                                   chunk=_attn_chunk(Tp), latent_chunk=BWD_LATENT_CHUNK):
                rows = pl.ds(start, size)
                ds_all, p_all = [], []
                for j in range(N_KV):
                    p = jnp.exp(_scores(qvs[j], k_ref, i, start, size, masked, n_ctx=n_ctx) - lses[j])
                    dp = lax.dot_general(dovs[j], v_ref[rows, :], NT_DIMS, preferred_element_type=F32)
                    ds = (p * (dp - deltas[j])).astype(BF16)
                    dqs[j] = dqs[j] + jnp.dot(ds, k_ref[rows, :], preferred_element_type=F32)
                    ds_all.append(ds)
                    p_all.append(p.astype(BF16))
                ds_cat, p_cat = jnp.concatenate(ds_all, axis=0), jnp.concatenate(p_all, axis=0)
                if window:
                    dk_ref[rows, :] += lax.dot_general(ds_cat, q_all, TN_DIMS, preferred_element_type=F32)
                    dv_ref[rows, :] += lax.dot_general(p_cat, do_all, TN_DIMS, preferred_element_type=F32)
                else:
                    dkt_ref[:, start:start + size] += jnp.dot(q_all_t, ds_cat, preferred_element_type=F32)
                    dvt_ref[:, start:start + size] += jnp.dot(do_all_t, p_cat, preferred_element_type=F32)
            dqs = [dq * SCORE_SCALE for dq in dqs]
            for j in range(N_KV):
                if has_sink:
                    contrib = -(jnp.exp(_sink_column(sink_ref, j) - lses[j]) * deltas[j])
                    r = lax.broadcasted_iota(jnp.int32, (QROWS, 1), 0)
                    row8 = lax.broadcasted_iota(jnp.int32, (8, LANES), 0)
                    for h in range(N_GROUP):
                        in_head = (r >= h * Q_BLOCK) & (r < (h + 1) * Q_BLOCK)
                        tot = jnp.sum(jnp.where(in_head, contrib, 0.0), axis=0, keepdims=True)
                        upd = upd + jnp.where(row8 == j * N_GROUP + h, tot, 0.0)
            _unstack_heads(dqs, dq_ref)
            if has_sink:
                ds_ref[...] += upd

        @pl.when(i < n_ctx // Q_BLOCK)
        def _():
            run(False)

        @pl.when(i >= n_ctx // Q_BLOCK)
        def _():
            run(True)

        if not window:
            @pl.when(i == nq - 1)
            def _():
                dk_ref[...] += dkt_ref[...].T
                dv_ref[...] += dvt_ref[...].T

        finish_ride()

    ins, specs = [q, k, v, do, o32, lse], [q_in, kv_in, kv_in, q_out, q_out, lse_spec]
    out_specs = [q_out, kv_out, kv_out]
    out_shape = [_sds((T, Q_WIDTH), F32), _sds((T, KV_WIDTH), F32), _sds((T, KV_WIDTH), F32)]
    if has_sink:
        ins, specs = [sink] + ins, [pl.BlockSpec(memory_space=pltpu.SMEM)] + specs
        out_specs.append(sink_spec)
        out_shape.append(_sds((B, 8, LANES), F32))
    scratch = [pltpu.VMEM((KV_WIDTH, LANES if window else Tp), F32)] * 2
    ins, specs, out_specs, out_shape, scratch = _hitch(rider, ins, specs, out_specs, out_shape, scratch)
    return pl.pallas_call(
        body, name=name, grid=(B, nq), in_specs=specs, out_specs=out_specs, out_shape=out_shape, scratch_shapes=scratch,
        compiler_params=_params(("arbitrary", "arbitrary") if rider is not None else ("parallel", "arbitrary")),
    )(*ins)


def _window_sums(xp):
    n = xp.shape[0]

    def ahead(a, k):
        return pltpu.roll(a, n - k, 0)
    a2 = xp + ahead(xp, 1)
    a4 = a2 + ahead(a2, 2)
    a8 = a4 + ahead(a4, 4)
    a16 = a8 + ahead(a8, 8)
    return (a2, a4, a8, a16)


def _by_group(vals):
    lane = lax.broadcasted_iota(jnp.int32, vals[0].shape, 1)
    return jnp.where(lane < POOL_CH, vals[0], jnp.where(lane < 2 * POOL_CH, vals[1],
                     jnp.where(lane < 3 * POOL_CH, vals[2], vals[3])))


def _pool_counts(n):
    t = lax.broadcasted_iota(jnp.int32, (n, POOL_WIDTH), 0)
    cnts = [(jnp.minimum(t + w // 2, n) - jnp.maximum(t - w // 2, 0)).astype(F32) for w in POOL_WINDOWS]
    return _by_group(cnts)


def _pad_rows(x):
    zeros = jnp.zeros((POOL_PAD, x.shape[1]), x.dtype)
    return jnp.concatenate([zeros, x, zeros], axis=0)


def _pool_stream(u):
    n = u.shape[0]
    sums = _window_sums(_pad_rows(u))
    tots = [pltpu.roll(a, w // 2, 0)[POOL_PAD:POOL_PAD + n] for a, w in zip(sums, POOL_WINDOWS)]
    return _by_group(tots) / _pool_counts(n) - u


def _pool_stream_t(dp):
    n = dp.shape[0]
    sums = _window_sums(_pad_rows(dp / _pool_counts(n)))
    tots = [pltpu.roll(a, w // 2 - 1, 0)[POOL_PAD:POOL_PAD + n] if w > 2 else a[POOL_PAD:POOL_PAD + n]
            for a, w in zip(sums, POOL_WINDOWS)]
    return _by_group(tots) - dp


def _pool_fwd(z, w_bd, scale, *, B, Tp, n_ctx, name):
    T = z.shape[0]
    blk = pl.BlockSpec((Tp, POOL_WIDTH), lambda b: (b, U_COL // POOL_WIDTH))
    out = pl.BlockSpec((Tp, POOL_WIDTH), lambda b: (b, 0))

    def body(u_ref, w_ref, s_ref, p_ref, o_ref):
        for lo, hi in ((0, n_ctx), (n_ctx, Tp)):
            pooled = _pool_stream(u_ref[lo:hi, :]).astype(BF16)
            p_ref[lo:hi, :] = pooled
            mixed = jnp.dot(pooled, w_ref[...], preferred_element_type=F32)
            o_ref[lo:hi, :] = (mixed * s_ref[...]).astype(BF16)

    return pl.pallas_call(
        body, name=name, grid=(B,),
        in_specs=[blk, pl.BlockSpec((POOL_WIDTH, POOL_WIDTH), lambda b: (0, 0)), pl.BlockSpec((1, POOL_WIDTH), lambda b: (0, 0))],
        out_specs=[out, out], out_shape=[_sds((T, POOL_WIDTH), BF16)] * 2, compiler_params=_params(("parallel",)),
    )(z, w_bd, scale)


def _pool_bwd(d_ob, pooled, w_bd, scale, *, B, Tp, n_ctx, name):
    T = d_ob.shape[0]
    blk = pl.BlockSpec((Tp, POOL_WIDTH), lambda b: (b, 0))
    wsp = pl.BlockSpec((POOL_WIDTH, POOL_WIDTH), lambda b: (0, 0))
    ssp = pl.BlockSpec((1, POOL_WIDTH), lambda b: (0, 0))

    def body(d_ref, p_ref, w_ref, s_ref, du_ref, dw_ref, dsc_ref):
        @pl.when(pl.program_id(0) == 0)
        def _():
            dw_ref[...] = jnp.zeros_like(dw_ref)
            dsc_ref[...] = jnp.zeros_like(dsc_ref)

        dv, pv, wv = d_ref[...], p_ref[...], w_ref[...]
        mixed = jnp.dot(pv, wv, preferred_element_type=F32)
        dsc_ref[...] += jnp.sum(dv * mixed, axis=0, keepdims=True)
        dmixed = (dv * s_ref[...]).astype(BF16)
        dw_ref[...] += lax.dot_general(pv, dmixed, TN_DIMS, preferred_element_type=F32)
        dpooled = lax.dot_general(dmixed, wv, NT_DIMS, preferred_element_type=F32)
        for lo, hi in ((0, n_ctx), (n_ctx, Tp)):
            du_ref[lo:hi, :] = _pool_stream_t(dpooled[lo:hi, :]).astype(BF16)

    return pl.pallas_call(
        body, name=name, grid=(B,), in_specs=[blk, blk, wsp, ssp], out_specs=[blk, wsp, ssp],
        out_shape=[_sds((T, POOL_WIDTH), BF16), _sds((POOL_WIDTH, POOL_WIDTH), F32), _sds((1, POOL_WIDTH), F32)],
        compiler_params=_params(("arbitrary",)),
    )(d_ob, pooled, w_bd, scale)


def _merge_specs(z, D, TR, tc, wa, wb, wc):
    def act(width):
        return pl.BlockSpec((TR, width), lambda i, n: (i, 0))

    def gate(part):
        return pl.BlockSpec((TR, tc), lambda i, n: (i, (GATE_COL + part * D) // tc + n))
    w_specs = [w.spec(w.shape[0], tc, lambda i, n: (0, n)) for w in (wa, wb, wc)]
    return [act(Q_WIDTH), act(POOL_WIDTH), act(Q_WIDTH), gate(0), gate(1), gate(2)] + w_specs


def _merge_fwd(oa, ob, oc, z, wa, wb, wc, *, D, TR, name):
    T = oa.shape[0]
    tc = D // N_CHIPS

    def body(oa_ref, ob_ref, oc_ref, ga_ref, gb_ref, gc_ref, wa_ref, wb_ref, wc_ref, y_ref):
        acc = jax.nn.sigmoid(ga_ref[...]) * jnp.dot(oa_ref[...], wa_ref[...], preferred_element_type=F32)
        acc += jax.nn.sigmoid(gb_ref[...]) * jnp.dot(ob_ref[...], wb_ref[...], preferred_element_type=F32)
        acc += jax.nn.sigmoid(gc_ref[...]) * jnp.dot(oc_ref[...], wc_ref[...], preferred_element_type=F32)
        y_ref[...] = acc.astype(BF16)

    return pl.pallas_call(
        body, name=name, grid=(T // TR, D // tc), in_specs=_merge_specs(z, D, TR, tc, wa, wb, wc),
        out_specs=pl.BlockSpec((TR, tc), lambda i, n: (i, n)), out_shape=_sds((T, D), BF16),
        compiler_params=_params(("parallel", "parallel")),
    )(oa, ob, oc, z, z, z, wa.arr, wb.arr, wc.arr)


def _merge_bwd(dy, oa, ob, oc, z, wa, wb, wc, *, D, TR, name):
    T = oa.shape[0]
    tc = D // N_CHIPS
    out = pl.BlockSpec((TR, tc), lambda i, n: (i, n))

    def body(dy_ref, oa_ref, ob_ref, oc_ref, ga_ref, gb_ref, gc_ref, wa_ref, wb_ref, wc_ref,
             dpa_ref, dpb_ref, dpc_ref, dga_ref, dgb_ref, dgc_ref):
        dyv = dy_ref[...]
        for o_ref, g_ref, w_ref, dp_ref, dg_ref in ((oa_ref, ga_ref, wa_ref, dpa_ref, dga_ref),
                                                    (ob_ref, gb_ref, wb_ref, dpb_ref, dgb_ref),
                                                    (oc_ref, gc_ref, wc_ref, dpc_ref, dgc_ref)):
            s = jax.nn.sigmoid(g_ref[...])
            proj = jnp.dot(o_ref[...], w_ref[...], preferred_element_type=F32)
            dp_ref[...] = (dyv * s).astype(BF16)
            dg_ref[...] = (dyv * proj * (s * (1.0 - s))).astype(BF16)

    return pl.pallas_call(
        body, name=name, grid=(T // TR, D // tc), in_specs=[out] + _merge_specs(z, D, TR, tc, wa, wb, wc),
        out_specs=[out] * 6, out_shape=[_sds((T, D), BF16)] * 6, compiler_params=_params(("parallel", "parallel")),
    )(dy, oa, ob, oc, z, z, z, wa.arr, wb.arr, wc.arr)


def _silu_rows(cc, name):
    def body(c_ref, s_ref):
        v = c_ref[...]
        s_ref[...] = (v * jax.nn.sigmoid(v)).astype(BF16)
    return pl.pallas_call(body, name=name, out_shape=_sds(cc.shape, BF16))(cc)


def _ada_bwd_rows(dm, ds, cc, name):
    def body(dm_ref, ds_ref, c_ref, db_ref, dc_ref):
        db_ref[...] = jnp.sum(dm_ref[...], axis=0, keepdims=True)
        v = c_ref[...]
        s = jax.nn.sigmoid(v)
        dc_ref[...] = ds_ref[...] * (s * (1.0 + v * (1.0 - s)))
    return pl.pallas_call(body, name=name, out_shape=[_sds((1, dm.shape[1]), F32), _sds(cc.shape, F32)])(dm, ds, cc)


def _row_tile(rows, cols):
    for t in (512, 256, 128, 64, 32, 16, 8):
        if rows % t == 0 and t * cols * 4 <= (1 << 20):
            return t
    return rows


def _working_rows(tr, C, worker):
    return pl.BlockSpec((tr, C), lambda i, c: (jnp.where(c[0] == worker, i, 0), 0))


def _add_landed(own, landed, core, worker, name):
    R, C = own.shape
    tr = _row_tile(R, C)
    row = _working_rows(tr, C, worker)

    def body(c_ref, a_ref, b_ref, o_ref, o16_ref):
        @pl.when(c_ref[0] == worker)
        def _():
            tot = a_ref[...] + b_ref[...].astype(F32)
            o_ref[...] = tot
            o16_ref[...] = tot.astype(BF16)

    grid_spec = pltpu.PrefetchScalarGridSpec(num_scalar_prefetch=1, grid=(R // tr,), in_specs=[row, row], out_specs=[row, row])
    return pl.pallas_call(body, name=name, grid_spec=grid_spec, out_shape=[_sds((R, C), F32), _sds((R, C), BF16)],
                          compiler_params=_params(("arbitrary",)))(core, own, landed)


def _sum_chips(own, landed, chip, core, worker, name):
    _, R, C = own.shape
    tr = _row_tile(R, C)

    def row(i, c):
        return jnp.where(c[0] == worker, i, 0)

    def body(k_ref, c_ref, a_ref, b_ref, o_ref):
        @pl.when(c_ref[0] == worker)
        def _():
            o_ref[...] = ((a_ref[...] + b_ref[0].astype(F32)) + b_ref[1].astype(F32)) + b_ref[2].astype(F32)

    grid_spec = pltpu.PrefetchScalarGridSpec(
        num_scalar_prefetch=2, grid=(R // tr,),
        in_specs=[pl.BlockSpec((None, tr, C), lambda i, k, c: (k[0], row(i, c), 0)),
                  pl.BlockSpec((3, tr, C), lambda i, k, c: (0, row(i, c), 0))],
        out_specs=pl.BlockSpec((tr, C), lambda i, k, c: (row(i, c), 0)))
    return pl.pallas_call(body, name=name, grid_spec=grid_spec, out_shape=_sds((R, C), F32),
                          compiler_params=_params(("arbitrary",)))(chip, core, own, landed)


def _adam_math(w, g, m, v):
    m = ADAM_B1 * m + (1.0 - ADAM_B1) * g
    v = ADAM_B2 * v + (1.0 - ADAM_B2) * (g * g)
    m_hat = m / (1.0 - ADAM_B1 ** ADAM_STEP)
    v_hat = v / (1.0 - ADAM_B2 ** ADAM_STEP)
    delta = -ADAM_LR * (m_hat / (jnp.sqrt(v_hat) + ADAM_EPS) + ADAM_WD * w)
    return delta, m, v


def _adamw(w, reduced, other, m, v, core, name):
    L, R, C = w.shape
    tr = _row_tile(R, C)

    def body(c_ref, w_ref, r0_ref, r1_ref, b_ref, m_ref, v_ref, g_ref, d_ref, mo_ref, vo_ref):
        def step(g):
            d, mn, vn = _adam_math(w_ref[...], g, m_ref[...], v_ref[...])
            g_ref[...] = g
            d_ref[...] = d
            mo_ref[...] = mn
            vo_ref[...] = vn

        layer, core_id = pl.program_id(0), c_ref[0]
        for l, r_ref in enumerate((r0_ref, r1_ref)):
            @pl.when((layer == l) & (core_id == l))
            def _(r_ref=r_ref):
                step(r_ref[...])

        @pl.when(layer != core_id)
        def _():
            step(b_ref[...])

    lay = pl.BlockSpec((None, tr, C), lambda l, i, c: (l, i, 0))
    row = pl.BlockSpec((tr, C), lambda l, i, c: (i, 0))
    grid_spec = pltpu.PrefetchScalarGridSpec(num_scalar_prefetch=1, grid=(L, R // tr),
                                             in_specs=[lay, row, row, row, lay, lay], out_specs=[lay] * 4)
    return pl.pallas_call(body, name=name, grid_spec=grid_spec, out_shape=[_sds((L, R, C), F32)] * 4,
                          compiler_params=_params(("parallel", "parallel")))(core, w, reduced[0], reduced[1], other, m, v)


def _adamw_small(w, parts, m, v, name):
    R, C = w.shape

    def body(w_ref, p_ref, m_ref, v_ref, g_ref, d_ref, mo_ref, vo_ref):
        g = p_ref[0]
        for dev in range(1, 8):
            g = g + p_ref[dev]
        d, mn, vn = _adam_math(w_ref[...], g, m_ref[...], v_ref[...])
        g_ref[...] = g
        d_ref[...] = d
        mo_ref[...] = mn
        vo_ref[...] = vn

    return pl.pallas_call(body, name=name, out_shape=[_sds((R, C), F32)] * 4)(w, parts, m, v)


def _place():
    return lax.axis_index("x"), lax.axis_index("y"), lax.axis_index("c")


def _other_chips(x, y):
    return [(1 - x, y), (x, 1 - y), (1 - x, 1 - y)]


def _rcopy(src, dst, ssem, rsem, dev):
    return pltpu.make_async_remote_copy(src_ref=src, dst_ref=dst, send_sem=ssem, recv_sem=rsem,
                                        device_id=dev, device_id_type=MESH)


GATHER_SEMS = 7


class _LayerGather:
    def __init__(self, shards, layer):
        self.inputs, self.layer, self.n = list(shards), layer, len(shards)
        load, self.groups = [0, 0], ([], [])
        for w in sorted(range(self.n), key=lambda w: -shards[w][0].size):
            g = 0 if load[0] <= load[1] else 1
            self.groups[g].append(w)
            load[g] += shards[w][0].size
        self.out_shape = [_sds((N_CHIPS,) + s.shape[1:], s.dtype) for s in shards]
        self.scratch = [pltpu.SemaphoreType.DMA((self.n, GATHER_SEMS)), pltpu.SemaphoreType.DMA((self.n, GATHER_SEMS))]

    def _own(self, src, out, send_sems, recv_sems):
        x, y, c = _place()
        return [_rcopy(src[w].at[self.layer], out[w].at[2 * x + y], send_sems.at[w, 6], recv_sems.at[w, 6], (x, y, 1 - c))
                for w in range(self.n)]

    def _to_chips(self, g, src, out, send_sems, recv_sems):
        x, y, c = _place()
        return [_rcopy(src[w].at[self.layer], out[w].at[2 * x + y], send_sems.at[w, j], recv_sems.at[w, j], (*chip, c))
                for w in self.groups[g] for j, chip in enumerate(_other_chips(x, y))]

    def start(self, src, out, send_sems, recv_sems):
        c = lax.axis_index("c")
        for cp in self._own(src, out, send_sems, recv_sems):
            cp.start()
        for g in (0, 1):
            @pl.when(c == g)
            def _(g=g):
                for cp in self._to_chips(g, src, out, send_sems, recv_sems):
                    cp.start()

    def finish(self, src, out, send_sems, recv_sems):
        x, y, c = _place()
        sibling = (x, y, 1 - c)
        chips = _other_chips(x, y)
        for g in (0, 1):
            @pl.when(c == g)
            def _(g=g):
                passed = []
                for w in self.groups[g]:
                    for j, (px, py) in enumerate(chips):
                        landed = out[w].at[2 * px + py]
                        _rcopy(landed, landed, send_sems.at[w, j], recv_sems.at[w, j], (px, py, c)).wait_recv()
                        cp = _rcopy(landed, landed, send_sems.at[w, 3 + j], recv_sems.at[w, 3 + j], sibling)
                        cp.start()
                        passed.append(cp)
                for w in self.groups[1 - g]:
                    for j, (px, py) in enumerate(chips):
                        landed = out[w].at[2 * px + py]
                        _rcopy(landed, landed, send_sems.at[w, 3 + j], recv_sems.at[w, 3 + j], sibling).wait_recv()
                for cp in self._to_chips(g, src, out, send_sems, recv_sems) + passed:
                    cp.wait_send()
        for cp in self._own(src, out, send_sems, recv_sems):
            cp.wait_recv()
            cp.wait_send()


class _SiblingSwap:
    def __init__(self, arrays):
        self.inputs = list(arrays)
        n = len(self.inputs)
        self.out_shape = [_sds(a.shape, a.dtype) for a in self.inputs]
        self.scratch = [pltpu.SemaphoreType.DMA((n,)), pltpu.SemaphoreType.DMA((n,))]

    def _copies(self, src, out, send_sems, recv_sems):
        x, y, c = _place()
        return [_rcopy(src[w], out[w], send_sems.at[w], recv_sems.at[w], (x, y, 1 - c)) for w in range(len(src))]

    def start(self, src, out, send_sems, recv_sems):
        for cp in self._copies(src, out, send_sems, recv_sems):
            cp.start()

    def finish(self, src, out, send_sems, recv_sems):
        cps = self._copies(src, out, send_sems, recv_sems)
        for cp in cps:
            cp.wait_recv()
        for cp in cps:
            cp.wait_send()


class _ChipSend:
    def __init__(self, blocked, sender):
        self.inputs, self.sender = list(blocked), sender
        n = len(self.inputs)
        self.out_shape = [_sds((3,) + a.shape[1:], a.dtype) for a in self.inputs]
        self.scratch = [pltpu.SemaphoreType.DMA((n, 3)), pltpu.SemaphoreType.DMA((n, 3))]

    def _copies(self, src, out, send_sems, recv_sems):
        x, y, c = _place()
        return [_rcopy(src[w].at[2 * px + py], out[w].at[j], send_sems.at[w, j], recv_sems.at[w, j], (px, py, c))
                for w in range(len(src)) for j, (px, py) in enumerate(_other_chips(x, y))]

    def start(self, src, out, send_sems, recv_sems):
        @pl.when(lax.axis_index("c") == self.sender)
        def _():
            for cp in self._copies(src, out, send_sems, recv_sems):
                cp.start()

    def finish(self, src, out, send_sems, recv_sems):
        @pl.when(lax.axis_index("c") == self.sender)
        def _():
            cps = self._copies(src, out, send_sems, recv_sems)
            for cp in cps:
                cp.wait_recv()
            for cp in cps:
                cp.wait_send()


def _ride_alone(rider, name):
    n_in, n_out = len(rider.inputs), len(rider.out_shape)

    def body(*refs):
        args = (refs[:n_in], refs[n_in:n_in + n_out]) + tuple(refs[n_in + n_out:])
        rider.start(*args)
        rider.finish(*args)

    return pl.pallas_call(body, name=name, in_specs=[ANY] * n_in, out_specs=[ANY] * n_out, out_shape=rider.out_shape,
                          scratch_shapes=rider.scratch)(*rider.inputs)


def _hitch(rider, ins, in_specs, out_specs, out_shape, scratch):
    if rider is None:
        return ins, in_specs, out_specs, out_shape, scratch
    return (list(ins) + rider.inputs, list(in_specs) + [ANY] * len(rider.inputs),
            list(out_specs) + [ANY] * len(rider.out_shape), list(out_shape) + rider.out_shape, list(scratch) + rider.scratch)


def _ride(rider, refs, n_in, n_out, n_scratch, grid):
    if rider is None:
        return list(refs), lambda: None
    r_in, r_out = len(rider.inputs), len(rider.out_shape)
    refs = list(refs)
    own_in, ride_in = refs[:n_in], refs[n_in:n_in + r_in]
    rest = refs[n_in + r_in:]
    own_out, ride_out = rest[:n_out], rest[n_out:n_out + r_out]
    rest = rest[n_out + r_out:]
    own_scratch, sems = rest[:n_scratch], rest[n_scratch:]
    ids = [pl.program_id(a) for a in range(len(grid))]
    first = functools.reduce(jnp.logical_and, [i == 0 for i in ids])
    last = functools.reduce(jnp.logical_and, [i == g - 1 for i, g in zip(ids, grid)])

    @pl.when(first)
    def _():
        rider.start(ride_in, ride_out, *sems)

    def finish():
        @pl.when(last)
        def _():
            rider.finish(ride_in, ride_out, *sems)

    return own_in + own_out + own_scratch, finish


def _send_other_layer(layer0, layer1, name):
    n = len(layer0)

    def body(*refs):
        src0, src1, out = refs[:n], refs[n:2 * n], refs[2 * n:3 * n]
        send_sems, recv_sems = refs[3 * n:]
        x, y, c = _place()

        def copies(src):
            return [_rcopy(src[w], out[w], send_sems.at[w], recv_sems.at[w], (x, y, 1 - c)) for w in range(n)]

        @pl.when(c == 0)
        def _():
            for cp in copies(src1):
                cp.start()

        @pl.when(c == 1)
        def _():
            for cp in copies(src0):
                cp.start()

        for cp in copies(src0):
            cp.wait_recv()
        for cp in copies(src0):
            cp.wait_send()

    return pl.pallas_call(
        body, name=name, in_specs=[ANY] * (2 * n), out_specs=[ANY] * n,
        out_shape=[_sds(s.shape, s.dtype) for s in layer0],
        scratch_shapes=[pltpu.SemaphoreType.DMA((n,)), pltpu.SemaphoreType.DMA((n,))],
    )(*layer0, *layer1)


def _gather_small(block, name):
    m_per, n = block.shape

    def body(x_ref, out_ref, send_sems, recv_sems, local_sem):
        x, y, c = _place()
        me, sibling = (x, y, c), (x, y, 1 - c)
        chips = _other_chips(x, y)

        def rows(px, py, pc):
            return out_ref.at[pl.ds((4 * px + 2 * py + pc) * m_per, m_per), :]

        def copy(k, blk, to, src=None):
            return _rcopy(rows(*blk) if src is None else src, rows(*blk), send_sems.at[k], recv_sems.at[k], to)

        mine = pltpu.make_async_copy(x_ref, rows(*me), local_sem)
        mine.start()
        first = [copy(0, me, sibling, src=x_ref)]
        first += [copy(1 + j, me, (*chip, c), src=x_ref) for j, chip in enumerate(chips)]
        for cp in first:
            cp.start()
        passed = [copy(4 + j, (*chip, c), sibling) for j, chip in enumerate(chips)]
        for j, chip in enumerate(chips):
            copy(1 + j, (*chip, c), me).wait_recv()
            passed[j].start()
        copy(0, sibling, me).wait_recv()
        for j, chip in enumerate(chips):
            copy(4 + j, (*chip, 1 - c), me).wait_recv()
        for cp in first + passed:
            cp.wait_send()
        mine.wait()

    return pl.pallas_call(
        body, name=name, out_shape=_sds((8 * m_per, n), block.dtype),
        in_specs=[pl.BlockSpec(memory_space=pltpu.VMEM)], out_specs=pl.BlockSpec(memory_space=pltpu.VMEM),
        scratch_shapes=[pltpu.SemaphoreType.DMA((7,)), pltpu.SemaphoreType.DMA((7,)), pltpu.SemaphoreType.DMA],
    )(block)


def _rope_tables(n_ctx, seq):
    rows = seq // GRID_W
    r = jnp.repeat(jnp.arange(rows, dtype=F32), GRID_W)
    col = jnp.tile(jnp.arange(GRID_W, dtype=F32), rows)
    inv = 1.0 / (ROPE_THETA ** (jnp.arange(0, AXIS_DIM, 2, dtype=F32) / AXIS_DIM))
    ang = jnp.concatenate([r[:, None] * inv, col[:, None] * inv], axis=-1)
    cos = jnp.repeat(jnp.cos(ang), 2, axis=-1)
    sin = jnp.repeat(jnp.sin(ang), 2, axis=-1) * jnp.tile(jnp.array([-1.0, 1.0], F32), HEAD_DIM // 2)
    cos = jnp.concatenate([jnp.ones((n_ctx, HEAD_DIM), F32), cos], axis=0)
    sin = jnp.concatenate([jnp.zeros((n_ctx, HEAD_DIM), F32), sin], axis=0)
    return jnp.tile(cos, (1, 2)), jnp.tile(sin, (1, 2))


def _block_diag(w_pool):
    L, G = w_pool.shape[:2]
    eye = jnp.eye(G, dtype=w_pool.dtype)
    return (w_pool[:, :, :, None, :] * eye[None, :, None, :, None]).reshape(L, POOL_WIDTH, POOL_WIDTH)


def _qk_gains(small):
    qn = jnp.stack([small["q_norm_a"], small["q_norm_c"]], axis=1)[:, :, None, :]
    kn = jnp.stack([small["k_norm_a"], small["k_norm_c"]], axis=1)[:, :, None, :]
    L = qn.shape[0]
    rows = jnp.concatenate([jnp.broadcast_to(qn, (L, 2, N_HEADS, HEAD_DIM)), jnp.broadcast_to(kn, (L, 2, N_KV, HEAD_DIM)),
                            jnp.ones((L, 2, N_KV, HEAD_DIM), F32)], axis=2)
    return rows.reshape(L, 2, 1, QKV_WIDTH)


def _local_step(x, c, ctx, c_ctx, small, gw, target, rider=None, overlap=False):
    gw = list(gw)
    B, S, D = x.shape
    N = ctx.shape[1]
    L = small["norm1"].shape[0]
    Tp = N + S
    T = B * Tp
    TR = N
    P = Tp // N
    rows16 = 16
    assert N % Q_BLOCK == 0 and S % N == 0 and B + 1 <= rows16
    TM = _tile(T, (1024, 768, 512, 384, 256, 128))
    TMG = _tile(T, (512, 384, 256, 128))

    X = jnp.concatenate([ctx, x], axis=1).reshape(T, D)
    cc = jnp.concatenate([c, c_ctx[None], jnp.zeros((rows16 - B - 1, D), F32)], axis=0)
    s_rows = _silu_rows(cc, "silu_rows")
    cos, sin = _rope_tables(N, S)
    all_gains = _qk_gains(small)
    all_w_bd = _block_diag(small["w_pool"]).astype(BF16)

    def weights(l):
        g = gw[l]
        return dict(
            ada=_Opnd(g["w_ada"], "bcols"), w_in=_Opnd(g["w_in"], "bcols"),
            a=_Opnd(g["w_br_a"], "bcols"), b=_Opnd(g["w_br_b"], "bcols"), c=_Opnd(g["w_br_c"], "bcols"),
            out=_Opnd(g["w_out"], "brows"), mlp1=_Opnd(g["w_mlp1"], "bcols"), mlp2=_Opnd(g["w_mlp2"], "brows"))

    IN = weights(0)["w_in"].shape[1]
    DFF = weights(0)["mlp1"].shape[1]
    tn_in = _tile(IN // N_CHIPS, (1152, 768, 512, 384, 256, 128))
    tn_ff = _tile(DFF // N_CHIPS, (1024, 512, 256, 128))
    tn_ada = _tile(6 * D // N_CHIPS, (1536, 768, 512, 256, 128))
    tn_d = D // N_CHIPS
    tk_d = _tile(D, (512,))
    tk_tok = _tile(T, (2304, 1536, 1024, 768, 512, 384, 256))

    saved = []
    xin, pending = X, None
    for l in range(L):
        W = weights(l)
        b_ada = small["b_ada"][l].reshape(1, 6 * D)
        mod = _matmul(s_rows, W["ada"], "nn", tm=rows16, tn=tn_ada, tk=D, name=f"ada_fwd{l}",
                      epilogue=lambda acc, b: (acc + b,), extras=[(b_ada, (1, tn_ada), lambda m, n: (0, n))])
        modtab = jnp.stack([jnp.broadcast_to(mod[B], (B, 6 * D)), mod[:B]], axis=1).reshape(2 * B, 1, 6 * D)
        gains = all_gains[l]
        w_bd = all_w_bd[l]
        p_scale = small["pool_scale"][l].reshape(1, POOL_WIDTH)
        sink = small["sink_c"][l]

        x0, h1 = _res_norm(xin, pending, modtab, 0, 1, small["norm1"][l][None], TR=TR, P=P, name=f"norm1_fwd{l}")
        z = _matmul(h1, W["w_in"], "nn", tm=TM, tn=tn_in, tk=D, name=f"in_proj{l}")
        q2, k2, v2 = _qk_prep(z, gains, cos, sin, TR=TR, P=P, name=f"qk_prep{l}")
        riding = rider if l == 0 else None
        oa, oa32, lse_a, *landed = _attn_fwd(q2, k2, v2, None, branch=0, B=B, n_ctx=N, window=False,
                                             name=f"attn_a_fwd{l}", rider=riding)
        if riding is not None:
            gw[riding.layer] = dict(zip(BIG_NAMES, landed))
        oc, oc32, lse_c = _attn_fwd(q2, k2, v2, sink, branch=1, B=B, n_ctx=N, window=True, name=f"attn_c_fwd{l}")
        pooled, ob = _pool_fwd(z, w_bd, p_scale, B=B, Tp=Tp, n_ctx=N, name=f"pool_fwd{l}")
        y = _merge_fwd(oa, ob, oc, z, W["a"], W["b"], W["c"], D=D, TR=TMG, name=f"merge_fwd{l}")
        ao = _matmul(y, W["out"], "nn", tm=TM, tn=D, tk=tn_d, name=f"out_proj{l}")
        x1, h2 = _res_norm(x0, (ao, modtab, 2), modtab, 3, 4, small["norm2"][l][None], TR=TR, P=P, name=f"norm2_fwd{l}")
        a_pre, r_act = _matmul(h2, W["mlp1"], "nn", tm=TM, tn=tn_ff, tk=D, name=f"mlp1_fwd{l}", out_dtypes=(F32, BF16),
                               epilogue=lambda acc: (acc, jnp.square(jnp.maximum(acc, 0.0))))
        mo = _matmul(r_act, W["mlp2"], "nn", tm=TM, tn=D, tk=tn_ff, name=f"mlp2_fwd{l}")
        saved.append(dict(modtab=modtab, gains=gains, w_bd=w_bd, p_scale=p_scale, sink=sink, x0=x0, h1=h1, z=z,
                          q2=q2, k2=k2, v2=v2, oa=oa, ob=ob, oc=oc, oa32=oa32, oc32=oc32, lse_a=lse_a, lse_c=lse_c,
                          pooled=pooled, y=y, ao=ao,
                          x1=x1, h2=h2, a_pre=a_pre, r_act=r_act, mo=mo))
        xin, pending = x1, (mo, modtab, 5)

    dxo, loss, d_mo, dg2 = _loss_head(xin, pending[0], pending[1], 5, target.reshape(B * S, D), TR=TR, P=P, name="loss_head")

    big = {k: [None] * L for k in BIG_NAMES}
    big16 = {k: [None] * L for k in BIG_NAMES}
    sm = {k: [None] * L for k in ("b_ada", "norm1", "norm2", "q_norm_a", "k_norm_a", "q_norm_c", "k_norm_c",
                                   "sink_c", "w_pool", "pool_scale")}

    def dw(key, l, a, b, *, tm, tn, name, tk=tk_tok, blocked=True):
        outs = _matmul(a, b, "tn", tm=tm, tn=tn, tk=tk, name=name, out_dtypes=(F32, BF16), out_blocked=blocked)
        if not blocked:
            outs = [o.reshape(N_CHIPS, o.shape[0] // N_CHIPS, o.shape[1]) for o in outs]
        big[key][l], big16[key][l] = outs
    d_cctx = jnp.zeros((D,), F32)
    for l in reversed(range(L)):
        W, sv = weights(l), saved[l]
        modtab = sv["modtab"]
        ride_now = overlap and l == L - 2
        if ride_now:
            early = _LayerReduce(l + 1, [big[k][l + 1] for k in BIG_NAMES], [big16[k][l + 1] for k in BIG_NAMES])
        d_a = _matmul(d_mo, W["mlp2"], "nt", tm=TM, tn=tn_ff, tk=D, name=f"mlp2_bwd{l}", out_dtypes=(BF16,),
                      epilogue=lambda acc, a: (acc * (2.0 * jnp.maximum(a, 0.0)),),
                      extras=[(sv["a_pre"], (TM, tn_ff), lambda m, n: (m, n))], rider=early.swap if ride_now else None)
        if ride_now:
            d_a, landed = d_a
            early_send = early.add(landed)
        dw("w_mlp2", l, sv["r_act"], d_mo, tm=tk_d, tn=D, name=f"mlp2_dw{l}", blocked=False)
        d_h2 = _matmul(d_a, W["mlp1"], "nt", tm=TM, tn=D, tk=tn_ff, name=f"mlp1_bwd{l}")
        dw("w_mlp1", l, sv["h2"], d_a, tm=tk_d, tn=tn_ff, name=f"mlp1_dw{l}")
        dx1, dsh2, dsc2, dn2, d_ao, dg1 = _norm_bwd(sv["x1"], d_h2, dxo, modtab, 4, small["norm2"][l][None],
                                                    (sv["ao"], modtab, 2), TR=TR, P=P, name=f"norm2_bwd{l}")
        d_y = _matmul(d_ao, W["out"], "nt", tm=TM, tn=tn_d, tk=D, name=f"out_bwd{l}")
        dw("w_out", l, sv["y"], d_ao, tm=tk_d, tn=D, name=f"out_dw{l}", blocked=False)
        d_pa, d_pb, d_pc, d_ga, d_gb, d_gc = _merge_bwd(d_y, sv["oa"], sv["ob"], sv["oc"], sv["z"], W["a"], W["b"], W["c"],
                                                        D=D, TR=TMG, name=f"merge_bwd{l}")
        d_oa = _matmul(d_pa, W["a"], "nt", tm=TM, tn=Q_WIDTH, tk=tn_d, name=f"br_a_bwd{l}", out_dtypes=(BF16,))
        d_ob = _matmul(d_pb, W["b"], "nt", tm=TM, tn=POOL_WIDTH, tk=tn_d, name=f"br_b_bwd{l}")
        d_oc = _matmul(d_pc, W["c"], "nt", tm=TM, tn=Q_WIDTH, tk=tn_d, name=f"br_c_bwd{l}", out_dtypes=(BF16,))
        dw("w_br_a", l, sv["oa"], d_pa, tm=Q_WIDTH, tn=tn_d, name=f"br_a_dw{l}")
        dw("w_br_b", l, sv["ob"], d_pb, tm=POOL_WIDTH, tn=tn_d, name=f"br_b_dw{l}")
        dw("w_br_c", l, sv["oc"], d_pc, tm=Q_WIDTH, tn=tn_d, name=f"br_c_dw{l}")
        d_u, d_wbd, d_ps = _pool_bwd(d_ob, sv["pooled"], sv["w_bd"], sv["p_scale"], B=B, Tp=Tp, n_ctx=N, name=f"pool_bwd{l}")
        dqa, dka, dva, *arrived = _attn_bwd(sv["q2"], sv["k2"], sv["v2"], d_oa, sv["oa32"], sv["lse_a"], None, branch=0,
                                            B=B, n_ctx=N, window=False, name=f"attn_a_bwd{l}",
                                            rider=early_send if ride_now else None)
        if ride_now:
            early.from_chips = arrived
        dqc, dkc, dvc, dsink = _attn_bwd(sv["q2"], sv["k2"], sv["v2"], d_oc, sv["oc32"], sv["lse_c"], sv["sink"],
                                         branch=1, B=B, n_ctx=N, window=True, name=f"attn_c_bwd{l}")
        dz_a, dgains_a = _qk_prep_bwd(sv["z"], dqa, dka, dva, sv["gains"], cos, sin, branch=0, TR=TR, P=P,
                                      name=f"qk_prep_a_bwd{l}")
        dz_c, dgains_c = _qk_prep_bwd(sv["z"], dqc, dkc, dvc, sv["gains"], cos, sin, branch=1, TR=TR, P=P,
                                      name=f"qk_prep_c_bwd{l}")
        dz = jnp.concatenate([dz_a, dz_c, d_u, d_ga, d_gb, d_gc], axis=1)
        d_h1 = _matmul(dz, W["w_in"], "nt", tm=TM, tn=D, tk=tn_in, name=f"in_bwd{l}")
        dw("w_in", l, sv["h1"], dz, tm=tk_d, tn=tn_in, name=f"in_dw{l}")
        below = (saved[l - 1]["mo"], saved[l - 1]["modtab"], 5) if l > 0 else None
        dx0, dsh1, dsc1, dn1, *lower = _norm_bwd(sv["x0"], d_h1, dx1, modtab, 1, small["norm1"][l][None], below,
                                                 TR=TR, P=P, name=f"norm1_bwd{l}")
        this_dg2 = dg2
        if l > 0:
            d_mo, dg2 = lower

        dm_groups = jnp.concatenate([dsh1, dsc1, dg1, dsh2, dsc2, this_dg2], axis=-1).reshape(B, 2, 6 * D)
        dm = jnp.concatenate([dm_groups[:, 1], jnp.sum(dm_groups[:, 0], axis=0, keepdims=True),
                              jnp.zeros((rows16 - B - 1, 6 * D), F32)], axis=0)
        dm_bf = dm.astype(BF16)
        d_s = _matmul(dm_bf, W["ada"], "nt", tm=rows16, tn=D, tk=tn_ada, name=f"ada_bwd{l}")
        dw("w_ada", l, s_rows, dm_bf, tm=tk_d, tn=tn_ada, tk=rows16, name=f"ada_dw{l}")
        db_ada, dcc = _ada_bwd_rows(dm, d_s, cc, f"ada_rows_bwd{l}")
        d_cctx = d_cctx + dcc[B]

        sm["b_ada"][l] = db_ada[0]
        sm["norm1"][l] = jnp.sum(dn1, axis=(0, 1))
        sm["norm2"][l] = jnp.sum(dn2, axis=(0, 1))
        dgh = jnp.stack([dgains_a, dgains_c]).reshape(2, QKV_WIDTH // HEAD_DIM, HEAD_DIM)
        sm["q_norm_a"][l] = jnp.sum(dgh[0, :N_HEADS], axis=0)
        sm["k_norm_a"][l] = jnp.sum(dgh[0, N_HEADS:N_HEADS + N_KV], axis=0)
        sm["q_norm_c"][l] = jnp.sum(dgh[1, :N_HEADS], axis=0)
        sm["k_norm_c"][l] = jnp.sum(dgh[1, N_HEADS:N_HEADS + N_KV], axis=0)
        sm["sink_c"][l] = jnp.sum(dsink[:, :N_HEADS, 0], axis=0)
        sm["w_pool"][l] = jnp.stack([d_wbd[g * POOL_CH:(g + 1) * POOL_CH, g * POOL_CH:(g + 1) * POOL_CH]
                                     for g in range(POOL_WIDTH // POOL_CH)])
        sm["pool_scale"][l] = d_ps[0]
        dxo = dx0

    grad_x = dxo.reshape(B, Tp, D)[:, N:]
    small_grads = {k: jnp.stack(v) for k, v in sm.items()}
    small_grads["c_ctx"] = d_cctx
    return loss, grad_x, small_grads, big, big16, (early if overlap else None)


SMALL_NAMES = ("c_ctx", "b_ada", "norm1", "norm2", "q_norm_a", "k_norm_a", "q_norm_c", "k_norm_c", "sink_c",
               "w_pool", "pool_scale")
BIG_NAMES = ("w_ada", "w_in", "w_br_a", "w_br_b", "w_br_c", "w_out", "w_mlp1", "w_mlp2")
WEIGHT_NAMES = ("c_ctx", "w_ada", "b_ada", "norm1", "norm2", "w_in", "q_norm_a", "k_norm_a", "q_norm_c", "k_norm_c",
                "sink_c", "w_pool", "pool_scale", "w_br_a", "w_br_b", "w_br_c", "w_out", "w_mlp1", "w_mlp2")


def _pack(parts, rows):
    flat = jnp.concatenate([p.reshape(-1).astype(F32) for p in parts])
    return jnp.pad(flat, (0, rows * LANES - flat.shape[0])).reshape(rows, LANES)


def _unpack(packed, like):
    flat, out, at = packed.reshape(-1), [], 0
    for p in like:
        out.append(flat[at:at + p.size].reshape(p.shape))
        at += p.size
    return out


class _LayerReduce:
    def __init__(self, layer, partials, partials16):
        self.layer, self.partials = layer, list(partials)
        self.swap = _SiblingSwap([g.reshape(-1, g.shape[-1]) for g in partials16])
        x, y, c = _place()
        self.core = c.astype(jnp.int32).reshape(1)
        self.chip = (2 * x + y).astype(jnp.int32).reshape(1)

    def add(self, landed):
        sums = [_add_landed(g.reshape(-1, g.shape[-1]), r, self.core, self.layer, f"grads{self.layer}_add_sibling_{k}")
                for k, g, r in zip(BIG_NAMES, self.partials, landed)]
        self.in_chip = [h.reshape(g.shape) for g, (h, _) in zip(self.partials, sums)]
        return _ChipSend([h.reshape(g.shape) for g, (_, h) in zip(self.partials, sums)], self.layer)

    def sum(self, from_chips):
        return [_sum_chips(h, r, self.chip, self.core, self.layer, f"grads{self.layer}_sum_chips_{k}")
                for k, h, r in zip(BIG_NAMES, self.in_chip, from_chips)]


def kernel(x, c, ctx, c_ctx, w_ada, b_ada, norm1, norm2, w_in, q_norm_a, k_norm_a, q_norm_c, k_norm_c, sink_c, w_pool, pool_scale, w_br_a, w_br_b, w_br_c, w_out, w_mlp1, w_mlp2, loss_target, m_c_ctx, m_w_ada, m_b_ada, m_norm1, m_norm2, m_w_in, m_q_norm_a, m_k_norm_a, m_q_norm_c, m_k_norm_c, m_sink_c, m_w_pool, m_pool_scale, m_w_br_a, m_w_br_b, m_w_br_c, m_w_out, m_w_mlp1, m_w_mlp2, v_c_ctx, v_w_ada, v_b_ada, v_norm1, v_norm2, v_w_in, v_q_norm_a, v_k_norm_a, v_q_norm_c, v_k_norm_c, v_sink_c, v_w_pool, v_pool_scale, v_w_br_a, v_w_br_b, v_w_br_c, v_w_out, v_w_mlp1, v_w_mlp2):
    given = dict(locals())
    w = {k: given[k] for k in WEIGHT_NAMES}
    m = {k: given["m_" + k] for k in WEIGHT_NAMES}
    v = {k: given["v_" + k] for k in WEIGHT_NAMES}

    shards = [w[k].astype(BF16) for k in BIG_NAMES]
    assert all(s.shape[0] == 2 for s in shards)
    first_layer = dict(zip(BIG_NAMES, _ride_alone(_LayerGather(shards, 0), "gather_weights0")))
    small = {k: w[k] for k in SMALL_NAMES}
    loss_part, grad_x, small_grads, big_grads, big_grads16, early = _local_step(
        x, c, ctx, c_ctx, small, [first_layer, None], loss_target, rider=_LayerGather(shards, 1), overlap=True)

    late = _LayerReduce(0, [big_grads[k][0] for k in BIG_NAMES], [big_grads16[k][0] for k in BIG_NAMES])
    core = late.core
    send = late.add(_ride_alone(late.swap, "grads0_to_sibling"))
    reduced = [late.sum(_ride_alone(send, "grads0_to_chips")), early.sum(early.from_chips)]
    others = _send_other_layer(reduced[1], reduced[0], "grads_share_layers")
    grads, deltas, new_m, new_v = {}, {}, {}, {}
    for i, k in enumerate(BIG_NAMES):
        grads[k], deltas[k], new_m[k], new_v[k] = _adamw(w[k], (reduced[0][i], reduced[1][i]), others[i], m[k], v[k],
                                                         core, f"adamw_{k}")

    sizes = sum(w[k].size for k in SMALL_NAMES) + LANES
    rows = -(-sizes // (8 * LANES)) * 8
    parts = _gather_small(_pack([small_grads[k] for k in SMALL_NAMES] + [loss_part[0]], rows), "gather_small")
    zero = jnp.zeros((LANES,), F32)
    packed = [_pack([t[k] for k in SMALL_NAMES] + [zero], rows) for t in (w, m, v)]
    outs = _adamw_small(packed[0], parts.reshape(8, rows, LANES), packed[1], packed[2], "adamw_small")
    like = [w[k] for k in SMALL_NAMES] + [zero]
    for store, packed_out in zip((grads, deltas, new_m, new_v), outs):
        pieces = _unpack(packed_out, like)
        for k, piece in zip(SMALL_NAMES, pieces):
            store[k] = piece
        if store is grads:
            loss = pieces[-1][0]

    return (loss, grad_x, *[grads[k] for k in WEIGHT_NAMES], *[deltas[k] for k in WEIGHT_NAMES],
            *[new_m[k] for k in WEIGHT_NAMES], *[new_v[k] for k in WEIGHT_NAMES])
```

```python
import functools

import jax
import jax.numpy as jnp
from jax import lax
from jax.experimental import pallas as pl
from jax.experimental.pallas import tpu as pltpu

F32 = jnp.float32
BF16 = jnp.bfloat16

HEAD_DIM = 64
GRID_W = 64
AXIS_DIM = HEAD_DIM // 2
ROPE_THETA = 10000.0
N_HEADS = 6
N_KV = 2
N_GROUP = N_HEADS // N_KV
POOL_CH = 64
POOL_WIDTH = 256
POOL_WINDOWS = (2, 4, 8, 16)
WINDOW = 128
Q_BLOCK = 128
Q_WIDTH = N_HEADS * HEAD_DIM
KV_WIDTH = N_KV * HEAD_DIM
GATE_COL = 2 * (Q_WIDTH + 2 * KV_WIDTH) + POOL_WIDTH
U_COL = 2 * (Q_WIDTH + 2 * KV_WIDTH)
EPS = 1e-6
NEG = -1e30
ADAM_LR = 0.001
ADAM_B1 = 0.9
ADAM_B2 = 0.999
ADAM_EPS = 1e-08
ADAM_WD = 0.01
ADAM_STEP = 10

N_CHIPS = 4
LANES = 128
POOL_PAD = 16
VMEM_LIMIT = 48 * 1024 * 1024
MESH = pl.DeviceIdType.MESH
ANY = pl.BlockSpec(memory_space=pl.ANY)


def _params(sem):
    return pltpu.CompilerParams(dimension_semantics=sem, vmem_limit_bytes=VMEM_LIMIT)


def _sds(shape, dtype):
    return jax.ShapeDtypeStruct(tuple(shape), dtype)


class _Opnd:
    def __init__(self, arr, kind="plain"):
        self.arr, self.kind = arr, kind

    @property
    def shape(self):
        a = self.arr
        if self.kind == "plain":
            return a.shape
        if self.kind == "bcols":
            return (a.shape[1], N_CHIPS * a.shape[2])
        return (N_CHIPS * a.shape[1], a.shape[2])

    def spec(self, tr, tc, fn):
        a = self.arr
        if self.kind == "plain":
            return pl.BlockSpec((tr, tc), lambda *g: fn(*g))
        if self.kind == "bcols":
            assert a.shape[2] % tc == 0, (a.shape, tc)
            per = a.shape[2] // tc

            def im(*g):
                ri, ci = fn(*g)
                return (ci // per, ri, ci % per)
            return pl.BlockSpec((None, tr, tc), im)
        assert a.shape[1] % tr == 0, (a.shape, tr)
        per = a.shape[1] // tr

        def im(*g):
            ri, ci = fn(*g)
            return (ri // per, ri % per, ci)
        return pl.BlockSpec((None, tr, tc), im)


def _matmul(a, b, mode, *, tm, tn, tk, name, out_dtypes=(F32,), epilogue=None, extras=(), out_blocked=False, rider=None):
    if not isinstance(a, _Opnd):
        a = _Opnd(a)
    if not isinstance(b, _Opnd):
        b = _Opnd(b)
    if mode == "nn":
        (M, K), (K2, N) = a.shape, b.shape
        a_spec = a.spec(tm, tk, lambda m, n, k: (m, k))
        b_spec = b.spec(tk, tn, lambda m, n, k: (k, n))
        dims = (((1,), (0,)), ((), ()))
    elif mode == "nt":
        (M, K), (N, K2) = a.shape, b.shape
        a_spec = a.spec(tm, tk, lambda m, n, k: (m, k))
        b_spec = b.spec(tn, tk, lambda m, n, k: (n, k))
        dims = (((1,), (1,)), ((), ()))
    else:
        (K, M), (K2, N) = a.shape, b.shape
        a_spec = a.spec(tk, tm, lambda m, n, k: (k, m))
        b_spec = b.spec(tk, tn, lambda m, n, k: (k, n))
        dims = (((0,), (0,)), ((), ()))
    assert K == K2 and M % tm == 0 and N % tn == 0 and K % tk == 0, (name, M, N, K, K2, tm, tn, tk)
    nk = K // tk
    n_extra = len(extras)
    n_out = len(out_dtypes)
    extra_specs = [pl.BlockSpec(bs, functools.partial(lambda m, n, k, f: f(m, n), f=f)) for (_, bs, f) in extras]
    if out_blocked:
        assert (N // N_CHIPS) % tn == 0
        per = (N // N_CHIPS) // tn
        out_shape = [_sds((N_CHIPS, M, N // N_CHIPS), dt) for dt in out_dtypes]
        out_specs = [pl.BlockSpec((None, tm, tn), lambda m, n, k: (n // per, m, n % per)) for _ in out_dtypes]
    else:
        out_shape = [_sds((M, N), dt) for dt in out_dtypes]
        out_specs = [pl.BlockSpec((tm, tn), lambda m, n, k: (m, n)) for _ in out_dtypes]

    in_place = nk > 1 and epilogue is None and out_dtypes[0] == F32

    grid = (M // tm, N // tn, nk)
    own_scratch = [pltpu.VMEM((tm, tn), F32)] if nk > 1 and not in_place else []

    def body(*refs):
        refs, finish_ride = _ride(rider, refs, 2 + n_extra, n_out, len(own_scratch), grid)
        a_ref, b_ref = refs[0], refs[1]
        extra_refs = refs[2:2 + n_extra]
        out_refs = refs[2 + n_extra:2 + n_extra + n_out]
        acc_ref = out_refs[0] if in_place else (refs[2 + n_extra + n_out] if nk > 1 else None)
        k = pl.program_id(2)
        prod = lax.dot_general(a_ref[...].astype(BF16), b_ref[...].astype(BF16), dims, preferred_element_type=F32)

        def finish(acc):
            outs = epilogue(acc, *[r[...] for r in extra_refs]) if epilogue is not None else (acc,) * n_out
            for o_ref, o in zip(out_refs, outs):
                o_ref[...] = o.astype(o_ref.dtype)

        if nk == 1:
            finish(prod)
        elif in_place:
            @pl.when(k == 0)
            def _():
                acc_ref[...] = prod

            @pl.when(k > 0)
            def _():
                acc_ref[...] += prod

            if n_out > 1:
                @pl.when(k == nk - 1)
                def _():
                    for o_ref in out_refs[1:]:
                        o_ref[...] = acc_ref[...].astype(o_ref.dtype)
        else:
            @pl.when(k == 0)
            def _():
                acc_ref[...] = prod

            @pl.when(k > 0)
            def _():
                acc_ref[...] += prod

            @pl.when(k == nk - 1)
            def _():
                finish(acc_ref[...])

        finish_ride()

    ins, in_specs, out_specs, out_shape, scratch = _hitch(
        rider, [a.arr, b.arr] + [e[0] for e in extras], [a_spec, b_spec] + extra_specs, out_specs, out_shape, own_scratch)
    outs = pl.pallas_call(
        body, name=name, grid=grid, in_specs=in_specs, out_specs=out_specs, out_shape=out_shape, scratch_shapes=scratch,
        compiler_params=_params(("arbitrary",) * 3 if rider is not None else ("parallel", "parallel", "arbitrary")),
    )(*ins)
    if rider is not None:
        return (outs[0] if n_out == 1 else outs[:n_out]), outs[n_out:]
    return outs[0] if n_out == 1 else outs


def _tile(n, cands):
    for t in cands:
        if n % t == 0:
            return t
    return n


def _grp(i, P):
    return 2 * (i // P) + jnp.minimum(i % P, 1)


def _mod_spec(D, P, part):
    return pl.BlockSpec((1, 1, D), lambda i: (_grp(i, P), 0, part))


def _res_norm(x, pending, modtab, shift_part, scale_part, gain, *, TR, P, name):
    T, D = x.shape
    row = pl.BlockSpec((TR, D), lambda i: (i, 0))
    has_branch = pending is not None
    ins, specs = [x], [row]
    if has_branch:
        branch, gate_tab, gate_part = pending
        ins += [branch, gate_tab]
        specs += [row, _mod_spec(D, P, gate_part)]
    ins += [modtab, modtab, gain]
    specs += [_mod_spec(D, P, shift_part), _mod_spec(D, P, scale_part), pl.BlockSpec((1, D), lambda i: (0, 0))]

    def body(*refs):
        if has_branch:
            x_ref, br_ref, g_ref, sh_ref, sc_ref, gn_ref, xo_ref, h_ref = refs
            xv = x_ref[...] + g_ref[0] * br_ref[...]
        else:
            x_ref, sh_ref, sc_ref, gn_ref, xo_ref, h_ref = refs
            xv = x_ref[...]
        xo_ref[...] = xv
        y = xv * lax.rsqrt(jnp.mean(xv * xv, axis=-1, keepdims=True) + EPS) * gn_ref[...]
        h_ref[...] = (y * (1.0 + sc_ref[0]) + sh_ref[0]).astype(BF16)

    return pl.pallas_call(
        body, name=name, grid=(T // TR,), in_specs=specs, out_specs=[row, row],
        out_shape=[_sds((T, D), F32), _sds((T, D), BF16)], compiler_params=_params(("parallel",)),
    )(*ins)


def _norm_bwd(x, dh, dres, modtab, scale_part, gain, below, *, TR, P, name):
    T, D = x.shape
    G = modtab.shape[0]
    row = pl.BlockSpec((TR, D), lambda i: (i, 0))
    acc = pl.BlockSpec((1, 1, D), lambda i: (_grp(i, P), 0, 0))
    has_below = below is not None

    def body(*refs):
        if has_below:
            x_ref, dh_ref, dres_ref, sc_ref, gn_ref, br_ref, g_ref, dx_ref, dsh_ref, dsc_ref, dgn_ref, db_ref, dg_ref = refs
        else:
            x_ref, dh_ref, dres_ref, sc_ref, gn_ref, dx_ref, dsh_ref, dsc_ref, dgn_ref = refs
        r = pl.program_id(0) % P
        xv, dhv, gn = x_ref[...], dh_ref[...], gn_ref[...]
        rstd = lax.rsqrt(jnp.mean(xv * xv, axis=-1, keepdims=True) + EPS)
        xhat = xv * rstd
        dn = dhv * (1.0 + sc_ref[0])
        dxhat = dn * gn
        dxv = dres_ref[...] + rstd * (dxhat - xhat * jnp.mean(dxhat * xhat, axis=-1, keepdims=True))
        dx_ref[...] = dxv
        parts = [jnp.sum(dhv, axis=0, keepdims=True), jnp.sum(dhv * (xhat * gn), axis=0, keepdims=True),
                 jnp.sum(dn * xhat, axis=0, keepdims=True)]
        outs = [dsh_ref, dsc_ref, dgn_ref]
        if has_below:
            db_ref[...] = (dxv * g_ref[0]).astype(BF16)
            parts.append(jnp.sum(dxv * br_ref[...], axis=0, keepdims=True))
            outs.append(dg_ref)

        @pl.when(r <= 1)
        def _():
            for o_ref, part in zip(outs, parts):
                o_ref[0] = part

        @pl.when(r > 1)
        def _():
            for o_ref, part in zip(outs, parts):
                o_ref[0] += part

    ins = [x, dh, dres, modtab, gain]
    in_specs = [row, row, row, _mod_spec(D, P, scale_part), pl.BlockSpec((1, D), lambda i: (0, 0))]
    out_specs, out_shape = [row, acc, acc, acc], [_sds((T, D), F32)] + [_sds((G, 1, D), F32)] * 3
    if has_below:
        branch, gate_tab, gate_part = below
        ins += [branch, gate_tab]
        in_specs += [row, _mod_spec(D, P, gate_part)]
        out_specs += [row, acc]
        out_shape += [_sds((T, D), BF16), _sds((G, 1, D), F32)]
    return pl.pallas_call(body, name=name, grid=(T // TR,), in_specs=in_specs, out_specs=out_specs, out_shape=out_shape,
                          compiler_params=_params(("arbitrary",)))(*ins)


def _loss_head(x, branch, modtab, gate_part, target, *, TR, P, name):
    T, D = x.shape
    row = pl.BlockSpec((TR, D), lambda i: (i, 0))
    tgt = pl.BlockSpec((TR, D), lambda i: ((i // P) * (P - 1) + jnp.maximum(i % P - 1, 0), 0))
    one = pl.BlockSpec((1, LANES), lambda i: (0, 0))

    G = modtab.shape[0]
    acc = pl.BlockSpec((1, 1, D), lambda i: (_grp(i, P), 0, 0))

    def body(x_ref, br_ref, g_ref, t_ref, dy_ref, loss_ref, db_ref, dg_ref):
        i = pl.program_id(0)
        r = i % P

        @pl.when(i == 0)
        def _():
            loss_ref[...] = jnp.zeros_like(loss_ref)

        @pl.when(r == 0)
        def _():
            dy_ref[...] = jnp.zeros_like(dy_ref)
            db_ref[...] = jnp.zeros_like(db_ref)
            dg_ref[...] = jnp.zeros_like(dg_ref)

        @pl.when(r > 0)
        def _():
            brv, g = br_ref[...], g_ref[0]
            err = x_ref[...] + g * brv - t_ref[...]
            dy = err / D
            dy_ref[...] = dy
            db_ref[...] = (dy * g).astype(BF16)
            part = jnp.sum(dy * brv, axis=0, keepdims=True)
            per_tok = jnp.mean(err * err, axis=-1, keepdims=True)
            loss_ref[...] += 0.5 * jnp.sum(per_tok, axis=0, keepdims=True)

            @pl.when(r == 1)
            def _():
                dg_ref[0] = part

            @pl.when(r > 1)
            def _():
                dg_ref[0] += part

    return pl.pallas_call(
        body, name=name, grid=(T // TR,), in_specs=[row, row, _mod_spec(D, P, gate_part), tgt],
        out_specs=[row, one, row, acc],
        out_shape=[_sds((T, D), F32), _sds((1, LANES), F32), _sds((T, D), BF16), _sds((G, 1, D), F32)],
        compiler_params=_params(("arbitrary",)),
    )(x, branch, modtab, target)


QKV_WIDTH = Q_WIDTH + 2 * KV_WIDTH
QK_NORMED = 4


def _seg_mean(v):
    lane = lax.broadcasted_iota(jnp.int32, v.shape, 1)
    lo = lane < HEAD_DIM
    s0 = jnp.sum(jnp.where(lo, v, 0.0), axis=-1, keepdims=True)
    s1 = jnp.sum(jnp.where(lo, 0.0, v), axis=-1, keepdims=True)
    return jnp.where(lo, s0, s1) * (1.0 / HEAD_DIM)


def _pair_swap(v):
    lane = lax.broadcasted_iota(jnp.int32, v.shape, 1)
    return jnp.where((lane & 1) == 0, pltpu.roll(v, LANES - 1, 1), pltpu.roll(v, 1, 1))


def _chunk(c):
    return slice(c * LANES, (c + 1) * LANES)


def _qk_prep(z, gains, cos, sin, *, TR, P, name):
    T = z.shape[0]

    def body(z_ref, g_ref, c_ref, s_ref, q_ref, k_ref, v_ref):
        cs, sn = c_ref[...], s_ref[...]
        for ch in range(QK_NORMED):
            xv = z_ref[:, _chunk(ch)]
            y = xv * lax.rsqrt(_seg_mean(xv * xv) + EPS) * g_ref[0, :, _chunk(ch)]
            out = (y * cs + _pair_swap(y) * sn).astype(BF16)
            if ch < QK_NORMED - 1:
                q_ref[:, _chunk(ch)] = out
            else:
                k_ref[...] = out
        v_ref[...] = z_ref[:, _chunk(QK_NORMED)].astype(BF16)

    def out(width):
        return pl.BlockSpec((None, TR, width), lambda i, j: (j, i, 0))
    return pl.pallas_call(
        body, name=name, grid=(T // TR, 2),
        in_specs=[pl.BlockSpec((TR, QKV_WIDTH), lambda i, j: (i, j)),
                  pl.BlockSpec((1, 1, QKV_WIDTH), lambda i, j: (j, 0, 0)),
                  pl.BlockSpec((TR, LANES), lambda i, j: (i % P, 0)),
                  pl.BlockSpec((TR, LANES), lambda i, j: (i % P, 0))],
        out_specs=[out(Q_WIDTH), out(KV_WIDTH), out(KV_WIDTH)],
        out_shape=[_sds((2, T, Q_WIDTH), BF16), _sds((2, T, KV_WIDTH), BF16), _sds((2, T, KV_WIDTH), BF16)],
        compiler_params=_params(("parallel", "parallel")),
    )(z, gains, cos, sin)


def _qk_prep_bwd(z, dq, dk, dv, gains, cos, sin, *, branch, TR, P, name):
    T = z.shape[0]
    nt = T // TR

    def body(z_ref, dq_ref, dk_ref, dv_ref, g_ref, c_ref, s_ref, dz_ref, dg_ref):
        i = pl.program_id(0)
        cs, sn = c_ref[...], s_ref[...]
        parts = []
        for ch in range(QK_NORMED):
            xv, g = z_ref[:, _chunk(ch)], g_ref[0, :, _chunk(ch)]
            dout = dq_ref[:, _chunk(ch)] if ch < QK_NORMED - 1 else dk_ref[...]
            dy = dout * cs + _pair_swap(dout * sn)
            rstd = lax.rsqrt(_seg_mean(xv * xv) + EPS)
            xhat = xv * rstd
            dxhat = dy * g
            dz_ref[:, _chunk(ch)] = (rstd * (dxhat - xhat * _seg_mean(dxhat * xhat))).astype(BF16)
            parts.append(jnp.sum(dy * xhat, axis=0, keepdims=True))
        dz_ref[:, _chunk(QK_NORMED)] = dv_ref[...].astype(BF16)
        parts.append(jnp.zeros((1, LANES), F32))
        part = jnp.concatenate(parts, axis=1)

        @pl.when(i == 0)
        def _():
            dg_ref[0] = part

        @pl.when(i > 0)
        def _():
            dg_ref[0] += part

    def rows(width, col=0):
        return pl.BlockSpec((TR, width), lambda i: (i, col))
    return pl.pallas_call(
        body, name=name, grid=(nt,),
        in_specs=[rows(QKV_WIDTH, branch), rows(Q_WIDTH), rows(KV_WIDTH), rows(KV_WIDTH),
                  pl.BlockSpec((1, 1, QKV_WIDTH), lambda i: (branch, 0, 0)),
                  pl.BlockSpec((TR, LANES), lambda i: (i % P, 0)),
                  pl.BlockSpec((TR, LANES), lambda i: (i % P, 0))],
        out_specs=[rows(QKV_WIDTH), pl.BlockSpec((1, 1, QKV_WIDTH), lambda i: (0, 0, 0))],
        out_shape=[_sds((T, QKV_WIDTH), BF16), _sds((1, 1, QKV_WIDTH), F32)],
        compiler_params=_params(("arbitrary",)),
    )(z, dq, dk, dv, gains, cos, sin)


NT_DIMS = (((1,), (1,)), ((), ()))
TN_DIMS = (((0,), (0,)), ((), ()))
QROWS = N_GROUP * Q_BLOCK
SCORE_SCALE = HEAD_DIM ** -0.5
BAND = Q_BLOCK + 2 * WINDOW
FWD_LATENT_CHUNK = 256
BWD_LATENT_CHUNK = 1024


def _move_head(block, half_from, half_to):
    lane = lax.broadcasted_iota(jnp.int32, block.shape, 1)
    src = block if half_from == half_to else pltpu.roll(block, HEAD_DIM, 1)
    keep = (lane < HEAD_DIM) if half_to == 0 else (lane >= HEAD_DIM)
    return jnp.where(keep, src, 0.0)


def _stack_heads(lane_block, j):
    pieces = []
    for h in range(N_GROUP * j, N_GROUP * (j + 1)):
        pieces.append(_move_head(lane_block(h // 2), h % 2, j))
    return jnp.concatenate(pieces, axis=0)


def _lane_blocks(ref):
    return lambda m: ref[:, m * LANES:(m + 1) * LANES].astype(F32)


def _unstack_heads(stacked, ref):
    heads = []
    for h in range(N_HEADS):
        j, r = h // N_GROUP, h % N_GROUP
        heads.append(_move_head(stacked[j][r * Q_BLOCK:(r + 1) * Q_BLOCK], j, h % 2))
    for m in range(N_HEADS // 2):
        ref[:, m * LANES:(m + 1) * LANES] = (heads[2 * m] + heads[2 * m + 1]).astype(ref.dtype)


def _key_chunks(i, latent, *, n_ctx, t_all, window, chunk, latent_chunk):
    ctx = [(s, chunk, False) for s in range(0, n_ctx, chunk)]
    if not latent:
        return ctx
    if not window:
        wide = latent_chunk if (t_all - n_ctx) % latent_chunk == 0 else chunk
        return ctx + [(s, wide, False) for s in range(n_ctx, t_all, wide)]
    start = pl.multiple_of(jnp.minimum((i - 1) * Q_BLOCK, t_all - BAND), Q_BLOCK)
    band_chunk = BAND if latent_chunk >= BAND else (chunk if BAND % chunk == 0 else Q_BLOCK)
    return ctx + [(start + s, band_chunk, True) for s in range(0, BAND, band_chunk)]


def _scores(q, k_ref, i, start, size, masked, *, n_ctx):
    s = lax.dot_general(q, k_ref[pl.ds(start, size), :], NT_DIMS, preferred_element_type=F32)
    if masked:
        qpos = (i * Q_BLOCK - n_ctx) + (lax.broadcasted_iota(jnp.int32, (QROWS, size), 0) & (Q_BLOCK - 1))
        kpos = (start - n_ctx) + lax.broadcasted_iota(jnp.int32, (QROWS, size), 1)
        valid = (kpos - qpos <= WINDOW) & (qpos - kpos <= WINDOW) & (kpos >= 0)
        s = jnp.where(valid, s, NEG)
    return s


def _sink_column(sink_ref, j):
    r = lax.broadcasted_iota(jnp.int32, (QROWS, 1), 0)
    s0, s1, s2 = sink_ref[j * N_GROUP], sink_ref[j * N_GROUP + 1], sink_ref[j * N_GROUP + 2]
    return jnp.where(r < Q_BLOCK, s0, jnp.where(r < 2 * Q_BLOCK, s1, s2))


def _attn_specs(Tp, branch):
    nq = Tp // Q_BLOCK
    q_in = pl.BlockSpec((None, Q_BLOCK, Q_WIDTH), lambda b, i: (branch, b * nq + i, 0))
    kv_in = pl.BlockSpec((None, Tp, KV_WIDTH), lambda b, i: (branch, b, 0))
    q_out = pl.BlockSpec((Q_BLOCK, Q_WIDTH), lambda b, i: (b * nq + i, 0))
    kv_out = pl.BlockSpec((Tp, KV_WIDTH), lambda b, i: (b, 0))
    return q_in, kv_in, q_out, kv_out


def _attn_chunk(Tp):
    return 256 if Tp % 256 == 0 else Q_BLOCK


def _attn_fwd(q, k, v, sink, *, branch, B, n_ctx, window, name, rider=None):
    T = q.shape[1]
    Tp = T // B
    nq = Tp // Q_BLOCK
    has_sink = sink is not None
    n_in = 4 if has_sink else 3
    q_in, kv_in, q_out, _ = _attn_specs(Tp, branch)
    lse_spec = pl.BlockSpec((None, N_KV * QROWS, 1), lambda b, i: (b * nq + i, 0, 0))

    def body(*refs):
        refs, finish_ride = _ride(rider, refs, n_in, 3, 0, (B, nq))
        sink_ref = refs.pop(0) if has_sink else None
        q_ref, k_ref, v_ref, o_ref, o32_ref, lse_ref = refs
        i = pl.program_id(1)

        def run(latent):
            outs = []
            for j in range(N_KV):
                qv = (_stack_heads(_lane_blocks(q_ref), j) * SCORE_SCALE).astype(BF16)
                if has_sink:
                    m, l = _sink_column(sink_ref, j), jnp.ones((QROWS, 1), F32)
                else:
                    m, l = jnp.full((QROWS, 1), NEG, F32), jnp.zeros((QROWS, 1), F32)
                acc = jnp.zeros((QROWS, LANES), F32)
                for start, size, masked in _key_chunks(i, latent, n_ctx=n_ctx, t_all=Tp, window=window,
                                                       chunk=_attn_chunk(Tp), latent_chunk=FWD_LATENT_CHUNK):
                    s = _scores(qv, k_ref, i, start, size, masked, n_ctx=n_ctx)
                    m_new = jnp.maximum(m, jnp.max(s, axis=-1, keepdims=True))
                    alpha = jnp.exp(m - m_new)
                    p = jnp.exp(s - m_new)
                    l = l * alpha + jnp.sum(p, axis=-1, keepdims=True)
                    acc = acc * alpha + jnp.dot(p.astype(BF16), v_ref[pl.ds(start, size), :], preferred_element_type=F32)
                    m = m_new
                outs.append(acc * (1.0 / l))
                lse_ref[j * QROWS:(j + 1) * QROWS, :] = m + jnp.log(l)
            _unstack_heads(outs, o_ref)
            _unstack_heads(outs, o32_ref)

        @pl.when(i < n_ctx // Q_BLOCK)
        def _():
            run(False)

        @pl.when(i >= n_ctx // Q_BLOCK)
        def _():
            run(True)

        finish_ride()

    ins, specs = [q, k, v], [q_in, kv_in, kv_in]
    if has_sink:
        ins, specs = [sink] + ins, [pl.BlockSpec(memory_space=pltpu.SMEM)] + specs
    out_specs = [q_out, q_out, lse_spec]
    out_shape = [_sds((T, Q_WIDTH), BF16), _sds((T, Q_WIDTH), F32), _sds((T // Q_BLOCK, N_KV * QROWS, 1), F32)]
    ins, specs, out_specs, out_shape, scratch = _hitch(rider, ins, specs, out_specs, out_shape, [])
    return pl.pallas_call(
        body, name=name, grid=(B, nq), in_specs=specs, out_specs=out_specs, out_shape=out_shape, scratch_shapes=scratch,
        compiler_params=_params(("arbitrary", "arbitrary") if rider is not None else ("parallel", "parallel")),
    )(*ins)


def _attn_bwd(q, k, v, do, o32, lse, sink, *, branch, B, n_ctx, window, name, rider=None):
    T = q.shape[1]
    Tp = T // B
    nq = Tp // Q_BLOCK
    has_sink = sink is not None
    q_in, kv_in, q_out, kv_out = _attn_specs(Tp, branch)
    lse_spec = pl.BlockSpec((None, N_KV * QROWS, 1), lambda b, i: (b * nq + i, 0, 0))
    sink_spec = pl.BlockSpec((None, 8, LANES), lambda b, i: (b, 0, 0))

    def body(*refs):
        refs, finish_ride = _ride(rider, refs, 7 if has_sink else 6, 4 if has_sink else 3, 2, (B, nq))
        if has_sink:
            sink_ref, q_ref, k_ref, v_ref, do_ref, o_ref, lse_ref, dq_ref, dk_ref, dv_ref, ds_ref, dkt_ref, dvt_ref = refs
        else:
            q_ref, k_ref, v_ref, do_ref, o_ref, lse_ref, dq_ref, dk_ref, dv_ref, dkt_ref, dvt_ref = refs
        i = pl.program_id(1)

        @pl.when(i == 0)
        def _():
            dk_ref[...] = jnp.zeros_like(dk_ref)
            dv_ref[...] = jnp.zeros_like(dv_ref)
            if not window:
                dkt_ref[...] = jnp.zeros_like(dkt_ref)
                dvt_ref[...] = jnp.zeros_like(dvt_ref)
            if has_sink:
                ds_ref[...] = jnp.zeros_like(ds_ref)

        def run(latent):
            upd = jnp.zeros((8, LANES), F32)
            do_blocks, o_blocks = _lane_blocks(do_ref), _lane_blocks(o_ref)
            qvs = [(_stack_heads(_lane_blocks(q_ref), j) * SCORE_SCALE).astype(BF16) for j in range(N_KV)]
            dovs = [_stack_heads(do_blocks, j).astype(BF16) for j in range(N_KV)]
            deltas = [jnp.sum(_stack_heads(lambda m: do_blocks(m) * o_blocks(m), j), axis=-1, keepdims=True)
                      for j in range(N_KV)]
            lses = [lse_ref[j * QROWS:(j + 1) * QROWS, :] for j in range(N_KV)]
            q_all, do_all = jnp.concatenate(qvs, axis=0), jnp.concatenate(dovs, axis=0)
            q_all_t, do_all_t = q_all.T, do_all.T
            dqs = [jnp.zeros((QROWS, LANES), F32) for _ in range(N_KV)]
            for start, size, masked in _key_chunks(i, latent, n_ctx=n_ctx, t_all=Tp, window=window,
                                                   chunk=_attn_chunk(Tp), latent_chunk=BWD_LATENT_CHUNK):
                rows = pl.ds(start, size)
                ds_all, p_all = [], []
                for j in range(N_KV):
                    p = jnp.exp(_scores(qvs[j], k_ref, i, start, size, masked, n_ctx=n_ctx) - lses[j])
                    dp = lax.dot_general(dovs[j], v_ref[rows, :], NT_DIMS, preferred_element_type=F32)
                    ds = (p * (dp - deltas[j])).astype(BF16)
                    dqs[j] = dqs[j] + jnp.dot(ds, k_ref[rows, :], preferred_element_type=F32)
                    ds_all.append(ds)
                    p_all.append(p.astype(BF16))
                ds_cat, p_cat = jnp.concatenate(ds_all, axis=0), jnp.concatenate(p_all, axis=0)
                if window:
                    dk_ref[rows, :] += lax.dot_general(ds_cat, q_all, TN_DIMS, preferred_element_type=F32)
                    dv_ref[rows, :] += lax.dot_general(p_cat, do_all, TN_DIMS, preferred_element_type=F32)
                else:
                    dkt_ref[:, start:start + size] += jnp.dot(q_all_t, ds_cat, preferred_element_type=F32)
                    dvt_ref[:, start:start + size] += jnp.dot(do_all_t, p_cat, preferred_element_type=F32)
            dqs = [dq * SCORE_SCALE for dq in dqs]
            for j in range(N_KV):
                if has_sink:
                    contrib = -(jnp.exp(_sink_column(sink_ref, j) - lses[j]) * deltas[j])
                    r = lax.broadcasted_iota(jnp.int32, (QROWS, 1), 0)
                    row8 = lax.broadcasted_iota(jnp.int32, (8, LANES), 0)
                    for h in range(N_GROUP):
                        in_head = (r >= h * Q_BLOCK) & (r < (h + 1) * Q_BLOCK)
                        tot = jnp.sum(jnp.where(in_head, contrib, 0.0), axis=0, keepdims=True)
                        upd = upd + jnp.where(row8 == j * N_GROUP + h, tot, 0.0)
            _unstack_heads(dqs, dq_ref)
            if has_sink:
                ds_ref[...] += upd

        @pl.when(i < n_ctx // Q_BLOCK)
        def _():
            run(False)

        @pl.when(i >= n_ctx // Q_BLOCK)
        def _():
            run(True)

        if not window:
            @pl.when(i == nq - 1)
            def _():
                dk_ref[...] += dkt_ref[...].T
                dv_ref[...] += dvt_ref[...].T

        finish_ride()

    ins, specs = [q, k, v, do, o32, lse], [q_in, kv_in, kv_in, q_out, q_out, lse_spec]
    out_specs = [q_out, kv_out, kv_out]
    out_shape = [_sds((T, Q_WIDTH), F32), _sds((T, KV_WIDTH), F32), _sds((T, KV_WIDTH), F32)]
    if has_sink:
        ins, specs = [sink] + ins, [pl.BlockSpec(memory_space=pltpu.SMEM)] + specs
        out_specs.append(sink_spec)
        out_shape.append(_sds((B, 8, LANES), F32))
    scratch = [pltpu.VMEM((KV_WIDTH, LANES if window else Tp), F32)] * 2
    ins, specs, out_specs, out_shape, scratch = _hitch(rider, ins, specs, out_specs, out_shape, scratch)
    return pl.pallas_call(
        body, name=name, grid=(B, nq), in_specs=specs, out_specs=out_specs, out_shape=out_shape, scratch_shapes=scratch,
        compiler_params=_params(("arbitrary", "arbitrary") if rider is not None else ("parallel", "arbitrary")),
    )(*ins)


def _window_sums(xp):
    n = xp.shape[0]

    def ahead(a, k):
        return pltpu.roll(a, n - k, 0)
    a2 = xp + ahead(xp, 1)
    a4 = a2 + ahead(a2, 2)
    a8 = a4 + ahead(a4, 4)
    a16 = a8 + ahead(a8, 8)
    return (a2, a4, a8, a16)


def _by_group(vals):
    lane = lax.broadcasted_iota(jnp.int32, vals[0].shape, 1)
    return jnp.where(lane < POOL_CH, vals[0], jnp.where(lane < 2 * POOL_CH, vals[1],
                     jnp.where(lane < 3 * POOL_CH, vals[2], vals[3])))


def _pool_counts(n):
    t = lax.broadcasted_iota(jnp.int32, (n, POOL_WIDTH), 0)
    cnts = [(jnp.minimum(t + w // 2, n) - jnp.maximum(t - w // 2, 0)).astype(F32) for w in POOL_WINDOWS]
    return _by_group(cnts)


def _pad_rows(x):
    zeros = jnp.zeros((POOL_PAD, x.shape[1]), x.dtype)
    return jnp.concatenate([zeros, x, zeros], axis=0)


def _pool_stream(u):
    n = u.shape[0]
    sums = _window_sums(_pad_rows(u))
    tots = [pltpu.roll(a, w // 2, 0)[POOL_PAD:POOL_PAD + n] for a, w in zip(sums, POOL_WINDOWS)]
    return _by_group(tots) / _pool_counts(n) - u


def _pool_stream_t(dp):
    n = dp.shape[0]
    sums = _window_sums(_pad_rows(dp / _pool_counts(n)))
    tots = [pltpu.roll(a, w // 2 - 1, 0)[POOL_PAD:POOL_PAD + n] if w > 2 else a[POOL_PAD:POOL_PAD + n]
            for a, w in zip(sums, POOL_WINDOWS)]
    return _by_group(tots) - dp


def _pool_fwd(z, w_bd, scale, *, B, Tp, n_ctx, name):
    T = z.shape[0]
    blk = pl.BlockSpec((Tp, POOL_WIDTH), lambda b: (b, U_COL // POOL_WIDTH))
    out = pl.BlockSpec((Tp, POOL_WIDTH), lambda b: (b, 0))

    def body(u_ref, w_ref, s_ref, p_ref, o_ref):
        for lo, hi in ((0, n_ctx), (n_ctx, Tp)):
            pooled = _pool_stream(u_ref[lo:hi, :]).astype(BF16)
            p_ref[lo:hi, :] = pooled
            mixed = jnp.dot(pooled, w_ref[...], preferred_element_type=F32)
            o_ref[lo:hi, :] = (mixed * s_ref[...]).astype(BF16)

    return pl.pallas_call(
        body, name=name, grid=(B,),
        in_specs=[blk, pl.BlockSpec((POOL_WIDTH, POOL_WIDTH), lambda b: (0, 0)), pl.BlockSpec((1, POOL_WIDTH), lambda b: (0, 0))],
        out_specs=[out, out], out_shape=[_sds((T, POOL_WIDTH), BF16)] * 2, compiler_params=_params(("parallel",)),
    )(z, w_bd, scale)


def _pool_bwd(d_ob, pooled, w_bd, scale, *, B, Tp, n_ctx, name):
    T = d_ob.shape[0]
    blk = pl.BlockSpec((Tp, POOL_WIDTH), lambda b: (b, 0))
    wsp = pl.BlockSpec((POOL_WIDTH, POOL_WIDTH), lambda b: (0, 0))
    ssp = pl.BlockSpec((1, POOL_WIDTH), lambda b: (0, 0))

    def body(d_ref, p_ref, w_ref, s_ref, du_ref, dw_ref, dsc_ref):
        @pl.when(pl.program_id(0) == 0)
        def _():
            dw_ref[...] = jnp.zeros_like(dw_ref)
            dsc_ref[...] = jnp.zeros_like(dsc_ref)

        dv, pv, wv = d_ref[...], p_ref[...], w_ref[...]
        mixed = jnp.dot(pv, wv, preferred_element_type=F32)
        dsc_ref[...] += jnp.sum(dv * mixed, axis=0, keepdims=True)
        dmixed = (dv * s_ref[...]).astype(BF16)
        dw_ref[...] += lax.dot_general(pv, dmixed, TN_DIMS, preferred_element_type=F32)
        dpooled = lax.dot_general(dmixed, wv, NT_DIMS, preferred_element_type=F32)
        for lo, hi in ((0, n_ctx), (n_ctx, Tp)):
            du_ref[lo:hi, :] = _pool_stream_t(dpooled[lo:hi, :]).astype(BF16)

    return pl.pallas_call(
        body, name=name, grid=(B,), in_specs=[blk, blk, wsp, ssp], out_specs=[blk, wsp, ssp],
        out_shape=[_sds((T, POOL_WIDTH), BF16), _sds((POOL_WIDTH, POOL_WIDTH), F32), _sds((1, POOL_WIDTH), F32)],
        compiler_params=_params(("arbitrary",)),
    )(d_ob, pooled, w_bd, scale)


def _merge_specs(z, D, TR, tc, wa, wb, wc):
    def act(width):
        return pl.BlockSpec((TR, width), lambda i, n: (i, 0))

    def gate(part):
        return pl.BlockSpec((TR, tc), lambda i, n: (i, (GATE_COL + part * D) // tc + n))
    w_specs = [w.spec(w.shape[0], tc, lambda i, n: (0, n)) for w in (wa, wb, wc)]
    return [act(Q_WIDTH), act(POOL_WIDTH), act(Q_WIDTH), gate(0), gate(1), gate(2)] + w_specs


def _merge_fwd(oa, ob, oc, z, wa, wb, wc, *, D, TR, name):
    T = oa.shape[0]
    tc = D // N_CHIPS

    def body(oa_ref, ob_ref, oc_ref, ga_ref, gb_ref, gc_ref, wa_ref, wb_ref, wc_ref, y_ref):
        acc = jax.nn.sigmoid(ga_ref[...]) * jnp.dot(oa_ref[...], wa_ref[...], preferred_element_type=F32)
        acc += jax.nn.sigmoid(gb_ref[...]) * jnp.dot(ob_ref[...], wb_ref[...], preferred_element_type=F32)
        acc += jax.nn.sigmoid(gc_ref[...]) * jnp.dot(oc_ref[...], wc_ref[...], preferred_element_type=F32)
        y_ref[...] = acc.astype(BF16)

    return pl.pallas_call(
        body, name=name, grid=(T // TR, D // tc), in_specs=_merge_specs(z, D, TR, tc, wa, wb, wc),
        out_specs=pl.BlockSpec((TR, tc), lambda i, n: (i, n)), out_shape=_sds((T, D), BF16),
        compiler_params=_params(("parallel", "parallel")),
    )(oa, ob, oc, z, z, z, wa.arr, wb.arr, wc.arr)


def _merge_bwd(dy, oa, ob, oc, z, wa, wb, wc, *, D, TR, name):
    T = oa.shape[0]
    tc = D // N_CHIPS
    out = pl.BlockSpec((TR, tc), lambda i, n: (i, n))

    def body(dy_ref, oa_ref, ob_ref, oc_ref, ga_ref, gb_ref, gc_ref, wa_ref, wb_ref, wc_ref,
             dpa_ref, dpb_ref, dpc_ref, dga_ref, dgb_ref, dgc_ref):
        dyv = dy_ref[...]
        for o_ref, g_ref, w_ref, dp_ref, dg_ref in ((oa_ref, ga_ref, wa_ref, dpa_ref, dga_ref),
                                                    (ob_ref, gb_ref, wb_ref, dpb_ref, dgb_ref),
                                                    (oc_ref, gc_ref, wc_ref, dpc_ref, dgc_ref)):
            s = jax.nn.sigmoid(g_ref[...])
            proj = jnp.dot(o_ref[...], w_ref[...], preferred_element_type=F32)
            dp_ref[...] = (dyv * s).astype(BF16)
            dg_ref[...] = (dyv * proj * (s * (1.0 - s))).astype(BF16)

    return pl.pallas_call(
        body, name=name, grid=(T // TR, D // tc), in_specs=[out] + _merge_specs(z, D, TR, tc, wa, wb, wc),
        out_specs=[out] * 6, out_shape=[_sds((T, D), BF16)] * 6, compiler_params=_params(("parallel", "parallel")),
    )(dy, oa, ob, oc, z, z, z, wa.arr, wb.arr, wc.arr)


def _silu_rows(cc, name):
    def body(c_ref, s_ref):
        v = c_ref[...]
        s_ref[...] = (v * jax.nn.sigmoid(v)).astype(BF16)
    return pl.pallas_call(body, name=name, out_shape=_sds(cc.shape, BF16))(cc)


def _ada_bwd_rows(dm, ds, cc, name):
    def body(dm_ref, ds_ref, c_ref, db_ref, dc_ref):
        db_ref[...] = jnp.sum(dm_ref[...], axis=0, keepdims=True)
        v = c_ref[...]
        s = jax.nn.sigmoid(v)
        dc_ref[...] = ds_ref[...] * (s * (1.0 + v * (1.0 - s)))
    return pl.pallas_call(body, name=name, out_shape=[_sds((1, dm.shape[1]), F32), _sds(cc.shape, F32)])(dm, ds, cc)


def _row_tile(rows, cols):
    for t in (512, 256, 128, 64, 32, 16, 8):
        if rows % t == 0 and t * cols * 4 <= (1 << 20):
            return t
    return rows


def _working_rows(tr, C, worker):
    return pl.BlockSpec((tr, C), lambda i, c: (jnp.where(c[0] == worker, i, 0), 0))


def _add_landed(own, landed, core, worker, name):
    R, C = own.shape
    tr = _row_tile(R, C)
    row = _working_rows(tr, C, worker)

    def body(c_ref, a_ref, b_ref, o_ref, o16_ref):
        @pl.when(c_ref[0] == worker)
        def _():
            tot = a_ref[...] + b_ref[...].astype(F32)
            o_ref[...] = tot
            o16_ref[...] = tot.astype(BF16)

    grid_spec = pltpu.PrefetchScalarGridSpec(num_scalar_prefetch=1, grid=(R // tr,), in_specs=[row, row], out_specs=[row, row])
    return pl.pallas_call(body, name=name, grid_spec=grid_spec, out_shape=[_sds((R, C), F32), _sds((R, C), BF16)],
                          compiler_params=_params(("arbitrary",)))(core, own, landed)


def _sum_chips(own, landed, chip, core, worker, name):
    _, R, C = own.shape
    tr = _row_tile(R, C)

    def row(i, c):
        return jnp.where(c[0] == worker, i, 0)

    def body(k_ref, c_ref, a_ref, b_ref, o_ref):
        @pl.when(c_ref[0] == worker)
        def _():
            o_ref[...] = ((a_ref[...] + b_ref[0].astype(F32)) + b_ref[1].astype(F32)) + b_ref[2].astype(F32)

    grid_spec = pltpu.PrefetchScalarGridSpec(
        num_scalar_prefetch=2, grid=(R // tr,),
        in_specs=[pl.BlockSpec((None, tr, C), lambda i, k, c: (k[0], row(i, c), 0)),
                  pl.BlockSpec((3, tr, C), lambda i, k, c: (0, row(i, c), 0))],
        out_specs=pl.BlockSpec((tr, C), lambda i, k, c: (row(i, c), 0)))
    return pl.pallas_call(body, name=name, grid_spec=grid_spec, out_shape=_sds((R, C), F32),
                          compiler_params=_params(("arbitrary",)))(chip, core, own, landed)


def _adam_math(w, g, m, v):
    m = ADAM_B1 * m + (1.0 - ADAM_B1) * g
    v = ADAM_B2 * v + (1.0 - ADAM_B2) * (g * g)
    m_hat = m / (1.0 - ADAM_B1 ** ADAM_STEP)
    v_hat = v / (1.0 - ADAM_B2 ** ADAM_STEP)
    delta = -ADAM_LR * (m_hat / (jnp.sqrt(v_hat) + ADAM_EPS) + ADAM_WD * w)
    return delta, m, v


def _adamw(w, reduced, shared, m, v, core, worker, name):
    L, R, C = w.shape
    tr = _row_tile(R, C)

    def body(c_ref, w_ref, r0_ref, r1_ref, s0_ref, s1_ref, m_ref, v_ref, g_ref, d_ref, mo_ref, vo_ref):
        def step(g):
            d, mn, vn = _adam_math(w_ref[...], g, m_ref[...], v_ref[...])
            g_ref[...] = g
            d_ref[...] = d
            mo_ref[...] = mn
            vo_ref[...] = vn

        layer, here = pl.program_id(0), c_ref[0] == worker
        for l, (r_ref, s_ref) in enumerate(((r0_ref, s0_ref), (r1_ref, s1_ref))):
            @pl.when((layer == l) & here)
            def _(r_ref=r_ref):
                step(r_ref[...])

            @pl.when((layer == l) & jnp.logical_not(here))
            def _(s_ref=s_ref):
                step(s_ref[...])

    lay = pl.BlockSpec((None, tr, C), lambda l, i, c: (l, i, 0))
    row = pl.BlockSpec((tr, C), lambda l, i, c: (i, 0))
    grid_spec = pltpu.PrefetchScalarGridSpec(num_scalar_prefetch=1, grid=(L, R // tr),
                                             in_specs=[lay, row, row, row, row, lay, lay], out_specs=[lay] * 4)
    return pl.pallas_call(body, name=name, grid_spec=grid_spec, out_shape=[_sds((L, R, C), F32)] * 4,
                          compiler_params=_params(("parallel", "parallel")))(core, w, *reduced, *shared, m, v)


def _adamw_small(w, parts, m, v, name):
    R, C = w.shape

    def body(w_ref, p_ref, m_ref, v_ref, g_ref, d_ref, mo_ref, vo_ref):
        g = p_ref[0]
        for dev in range(1, 8):
            g = g + p_ref[dev]
        d, mn, vn = _adam_math(w_ref[...], g, m_ref[...], v_ref[...])
        g_ref[...] = g
        d_ref[...] = d
        mo_ref[...] = mn
        vo_ref[...] = vn

    return pl.pallas_call(body, name=name, out_shape=[_sds((R, C), F32)] * 4)(w, parts, m, v)


def _place():
    return lax.axis_index("x"), lax.axis_index("y"), lax.axis_index("c")


def _other_chips(x, y):
    return [(1 - x, y), (x, 1 - y), (1 - x, 1 - y)]


def _rcopy(src, dst, ssem, rsem, dev):
    return pltpu.make_async_remote_copy(src_ref=src, dst_ref=dst, send_sem=ssem, recv_sem=rsem,
                                        device_id=dev, device_id_type=MESH)


GATHER_SEMS = 7


class _LayerGather:
    def __init__(self, shards, layer):
        self.inputs, self.layer, self.n = list(shards), layer, len(shards)
        load, self.groups = [0, 0], ([], [])
        for w in sorted(range(self.n), key=lambda w: -shards[w][0].size):
            g = 0 if load[0] <= load[1] else 1
            self.groups[g].append(w)
            load[g] += shards[w][0].size
        self.out_shape = [_sds((N_CHIPS,) + s.shape[1:], s.dtype) for s in shards]
        self.scratch = [pltpu.SemaphoreType.DMA((self.n, GATHER_SEMS)), pltpu.SemaphoreType.DMA((self.n, GATHER_SEMS))]

    def _own(self, src, out, send_sems, recv_sems):
        x, y, c = _place()
        return [_rcopy(src[w].at[self.layer], out[w].at[2 * x + y], send_sems.at[w, 6], recv_sems.at[w, 6], (x, y, 1 - c))
                for w in range(self.n)]

    def _to_chips(self, g, src, out, send_sems, recv_sems):
        x, y, c = _place()
        return [_rcopy(src[w].at[self.layer], out[w].at[2 * x + y], send_sems.at[w, j], recv_sems.at[w, j], (*chip, c))
                for w in self.groups[g] for j, chip in enumerate(_other_chips(x, y))]

    def start(self, src, out, send_sems, recv_sems):
        c = lax.axis_index("c")
        for cp in self._own(src, out, send_sems, recv_sems):
            cp.start()
        for g in (0, 1):
            @pl.when(c == g)
            def _(g=g):
                for cp in self._to_chips(g, src, out, send_sems, recv_sems):
                    cp.start()

    def finish(self, src, out, send_sems, recv_sems):
        x, y, c = _place()
        sibling = (x, y, 1 - c)
        chips = _other_chips(x, y)
        for g in (0, 1):
            @pl.when(c == g)
            def _(g=g):
                passed = []
                for w in self.groups[g]:
                    for j, (px, py) in enumerate(chips):
                        landed = out[w].at[2 * px + py]
                        _rcopy(landed, landed, send_sems.at[w, j], recv_sems.at[w, j], (px, py, c)).wait_recv()
                        cp = _rcopy(landed, landed, send_sems.at[w, 3 + j], recv_sems.at[w, 3 + j], sibling)
                        cp.start()
                        passed.append(cp)
                for w in self.groups[1 - g]:
                    for j, (px, py) in enumerate(chips):
                        landed = out[w].at[2 * px + py]
                        _rcopy(landed, landed, send_sems.at[w, 3 + j], recv_sems.at[w, 3 + j], sibling).wait_recv()
                for cp in self._to_chips(g, src, out, send_sems, recv_sems) + passed:
                    cp.wait_send()
        for cp in self._own(src, out, send_sems, recv_sems):
            cp.wait_recv()
            cp.wait_send()


def _on_core(fn):
    for g in (0, 1):
        @pl.when(lax.axis_index("c") == g)
        def _(g=g):
            fn(g)


class _ToSibling:
    def __init__(self, arrays, senders):
        self.inputs, self.senders = list(arrays), list(senders)
        n = len(self.inputs)
        self.out_shape = [_sds(a.shape, a.dtype) for a in self.inputs]
        self.scratch = [pltpu.SemaphoreType.DMA((n,)), pltpu.SemaphoreType.DMA((n,))]

    def _copies(self, sender, src, out, send_sems, recv_sems):
        x, y, c = _place()
        return [_rcopy(src[w], out[w], send_sems.at[w], recv_sems.at[w], (x, y, 1 - c))
                for w in range(len(src)) if self.senders[w] == sender]

    def start(self, *refs):
        def go(g):
            for cp in self._copies(g, *refs):
                cp.start()
        _on_core(go)

    def finish(self, *refs):
        def go(g):
            for cp in self._copies(1 - g, *refs):
                cp.wait_recv()
            for cp in self._copies(g, *refs):
                cp.wait_send()
        _on_core(go)


class _ChipSend:
    def __init__(self, blocked, senders):
        self.inputs, self.senders = list(blocked), list(senders)
        n = len(self.inputs)
        self.out_shape = [_sds((3,) + a.shape[1:], a.dtype) for a in self.inputs]
        self.scratch = [pltpu.SemaphoreType.DMA((n, 3)), pltpu.SemaphoreType.DMA((n, 3))]

    def _copies(self, sender, src, out, send_sems, recv_sems):
        x, y, c = _place()
        return [_rcopy(src[w].at[2 * px + py], out[w].at[j], send_sems.at[w, j], recv_sems.at[w, j], (px, py, c))
                for w in range(len(src)) if self.senders[w] == sender for j, (px, py) in enumerate(_other_chips(x, y))]

    def start(self, *refs):
        def go(g):
            for cp in self._copies(g, *refs):
                cp.start()
        _on_core(go)

    def finish(self, *refs):
        def go(g):
            cps = self._copies(g, *refs)
            for cp in cps:
                cp.wait_recv()
            for cp in cps:
                cp.wait_send()
        _on_core(go)


def _ride_alone(rider, name):
    n_in, n_out = len(rider.inputs), len(rider.out_shape)

    def body(*refs):
        args = (refs[:n_in], refs[n_in:n_in + n_out]) + tuple(refs[n_in + n_out:])
        rider.start(*args)
        rider.finish(*args)

    return pl.pallas_call(body, name=name, in_specs=[ANY] * n_in, out_specs=[ANY] * n_out, out_shape=rider.out_shape,
                          scratch_shapes=rider.scratch)(*rider.inputs)


def _hitch(rider, ins, in_specs, out_specs, out_shape, scratch):
    if rider is None:
        return ins, in_specs, out_specs, out_shape, scratch
    return (list(ins) + rider.inputs, list(in_specs) + [ANY] * len(rider.inputs),
            list(out_specs) + [ANY] * len(rider.out_shape), list(out_shape) + rider.out_shape, list(scratch) + rider.scratch)


def _ride(rider, refs, n_in, n_out, n_scratch, grid):
    if rider is None:
        return list(refs), lambda: None
    r_in, r_out = len(rider.inputs), len(rider.out_shape)
    refs = list(refs)
    own_in, ride_in = refs[:n_in], refs[n_in:n_in + r_in]
    rest = refs[n_in + r_in:]
    own_out, ride_out = rest[:n_out], rest[n_out:n_out + r_out]
    rest = rest[n_out + r_out:]
    own_scratch, sems = rest[:n_scratch], rest[n_scratch:]
    ids = [pl.program_id(a) for a in range(len(grid))]
    first = functools.reduce(jnp.logical_and, [i == 0 for i in ids])
    last = functools.reduce(jnp.logical_and, [i == g - 1 for i, g in zip(ids, grid)])

    @pl.when(first)
    def _():
        rider.start(ride_in, ride_out, *sems)

    def finish():
        @pl.when(last)
        def _():
            rider.finish(ride_in, ride_out, *sems)

    return own_in + own_out + own_scratch, finish


def _gather_small(block, name):
    m_per, n = block.shape

    def body(x_ref, out_ref, send_sems, recv_sems, local_sem):
        x, y, c = _place()
        me, sibling = (x, y, c), (x, y, 1 - c)
        chips = _other_chips(x, y)

        def rows(px, py, pc):
            return out_ref.at[pl.ds((4 * px + 2 * py + pc) * m_per, m_per), :]

        def copy(k, blk, to, src=None):
            return _rcopy(rows(*blk) if src is None else src, rows(*blk), send_sems.at[k], recv_sems.at[k], to)

        mine = pltpu.make_async_copy(x_ref, rows(*me), local_sem)
        mine.start()
        first = [copy(0, me, sibling, src=x_ref)]
        first += [copy(1 + j, me, (*chip, c), src=x_ref) for j, chip in enumerate(chips)]
        for cp in first:
            cp.start()
        passed = [copy(4 + j, (*chip, c), sibling) for j, chip in enumerate(chips)]
        for j, chip in enumerate(chips):
            copy(1 + j, (*chip, c), me).wait_recv()
            passed[j].start()
        copy(0, sibling, me).wait_recv()
        for j, chip in enumerate(chips):
            copy(4 + j, (*chip, 1 - c), me).wait_recv()
        for cp in first + passed:
            cp.wait_send()
        mine.wait()

    return pl.pallas_call(
        body, name=name, out_shape=_sds((8 * m_per, n), block.dtype),
        in_specs=[pl.BlockSpec(memory_space=pltpu.VMEM)], out_specs=pl.BlockSpec(memory_space=pltpu.VMEM),
        scratch_shapes=[pltpu.SemaphoreType.DMA((7,)), pltpu.SemaphoreType.DMA((7,)), pltpu.SemaphoreType.DMA],
    )(block)


def _rope_tables(n_ctx, seq):
    rows = seq // GRID_W
    r = jnp.repeat(jnp.arange(rows, dtype=F32), GRID_W)
    col = jnp.tile(jnp.arange(GRID_W, dtype=F32), rows)
    inv = 1.0 / (ROPE_THETA ** (jnp.arange(0, AXIS_DIM, 2, dtype=F32) / AXIS_DIM))
    ang = jnp.concatenate([r[:, None] * inv, col[:, None] * inv], axis=-1)
    cos = jnp.repeat(jnp.cos(ang), 2, axis=-1)
    sin = jnp.repeat(jnp.sin(ang), 2, axis=-1) * jnp.tile(jnp.array([-1.0, 1.0], F32), HEAD_DIM // 2)
    cos = jnp.concatenate([jnp.ones((n_ctx, HEAD_DIM), F32), cos], axis=0)
    sin = jnp.concatenate([jnp.zeros((n_ctx, HEAD_DIM), F32), sin], axis=0)
    return jnp.tile(cos, (1, 2)), jnp.tile(sin, (1, 2))


def _block_diag(w_pool):
    L, G = w_pool.shape[:2]
    eye = jnp.eye(G, dtype=w_pool.dtype)
    return (w_pool[:, :, :, None, :] * eye[None, :, None, :, None]).reshape(L, POOL_WIDTH, POOL_WIDTH)


def _qk_gains(small):
    qn = jnp.stack([small["q_norm_a"], small["q_norm_c"]], axis=1)[:, :, None, :]
    kn = jnp.stack([small["k_norm_a"], small["k_norm_c"]], axis=1)[:, :, None, :]
    L = qn.shape[0]
    rows = jnp.concatenate([jnp.broadcast_to(qn, (L, 2, N_HEADS, HEAD_DIM)), jnp.broadcast_to(kn, (L, 2, N_KV, HEAD_DIM)),
                            jnp.ones((L, 2, N_KV, HEAD_DIM), F32)], axis=2)
    return rows.reshape(L, 2, 1, QKV_WIDTH)


def _local_step(x, c, ctx, c_ctx, small, gw, target, rider=None, overlap=False):
    gw = list(gw)
    B, S, D = x.shape
    N = ctx.shape[1]
    L = small["norm1"].shape[0]
    Tp = N + S
    T = B * Tp
    TR = N
    P = Tp // N
    rows16 = 16
    assert N % Q_BLOCK == 0 and S % N == 0 and B + 1 <= rows16
    TM = _tile(T, (1024, 768, 512, 384, 256, 128))
    TMG = _tile(T, (512, 384, 256, 128))

    X = jnp.concatenate([ctx, x], axis=1).reshape(T, D)
    cc = jnp.concatenate([c, c_ctx[None], jnp.zeros((rows16 - B - 1, D), F32)], axis=0)
    s_rows = _silu_rows(cc, "silu_rows")
    cos, sin = _rope_tables(N, S)
    all_gains = _qk_gains(small)
    all_w_bd = _block_diag(small["w_pool"]).astype(BF16)

    def weights(l):
        g = gw[l]
        return dict(
            ada=_Opnd(g["w_ada"], "bcols"), w_in=_Opnd(g["w_in"], "bcols"),
            a=_Opnd(g["w_br_a"], "bcols"), b=_Opnd(g["w_br_b"], "bcols"), c=_Opnd(g["w_br_c"], "bcols"),
            out=_Opnd(g["w_out"], "brows"), mlp1=_Opnd(g["w_mlp1"], "bcols"), mlp2=_Opnd(g["w_mlp2"], "brows"))

    IN = weights(0)["w_in"].shape[1]
    DFF = weights(0)["mlp1"].shape[1]
    tn_in = _tile(IN // N_CHIPS, (1152, 768, 512, 384, 256, 128))
    tn_ff = _tile(DFF // N_CHIPS, (1024, 512, 256, 128))
    tn_ada = _tile(6 * D // N_CHIPS, (1536, 768, 512, 256, 128))
    tn_d = D // N_CHIPS
    tk_d = _tile(D, (512,))
    tk_tok = _tile(T, (2304, 1536, 1024, 768, 512, 384, 256))

    saved = []
    xin, pending = X, None
    for l in range(L):
        W = weights(l)
        b_ada = small["b_ada"][l].reshape(1, 6 * D)
        mod = _matmul(s_rows, W["ada"], "nn", tm=rows16, tn=tn_ada, tk=D, name=f"ada_fwd{l}",
                      epilogue=lambda acc, b: (acc + b,), extras=[(b_ada, (1, tn_ada), lambda m, n: (0, n))])
        modtab = jnp.stack([jnp.broadcast_to(mod[B], (B, 6 * D)), mod[:B]], axis=1).reshape(2 * B, 1, 6 * D)
        gains = all_gains[l]
        w_bd = all_w_bd[l]
        p_scale = small["pool_scale"][l].reshape(1, POOL_WIDTH)
        sink = small["sink_c"][l]

        x0, h1 = _res_norm(xin, pending, modtab, 0, 1, small["norm1"][l][None], TR=TR, P=P, name=f"norm1_fwd{l}")
        z = _matmul(h1, W["w_in"], "nn", tm=TM, tn=tn_in, tk=D, name=f"in_proj{l}")
        q2, k2, v2 = _qk_prep(z, gains, cos, sin, TR=TR, P=P, name=f"qk_prep{l}")
        riding = rider if l == 0 else None
        oa, oa32, lse_a, *landed = _attn_fwd(q2, k2, v2, None, branch=0, B=B, n_ctx=N, window=False,
                                             name=f"attn_a_fwd{l}", rider=riding)
        if riding is not None:
            gw[riding.layer] = dict(zip(BIG_NAMES, landed))
        oc, oc32, lse_c = _attn_fwd(q2, k2, v2, sink, branch=1, B=B, n_ctx=N, window=True, name=f"attn_c_fwd{l}")
        pooled, ob = _pool_fwd(z, w_bd, p_scale, B=B, Tp=Tp, n_ctx=N, name=f"pool_fwd{l}")
        y = _merge_fwd(oa, ob, oc, z, W["a"], W["b"], W["c"], D=D, TR=TMG, name=f"merge_fwd{l}")
        ao = _matmul(y, W["out"], "nn", tm=TM, tn=D, tk=tn_d, name=f"out_proj{l}")
        x1, h2 = _res_norm(x0, (ao, modtab, 2), modtab, 3, 4, small["norm2"][l][None], TR=TR, P=P, name=f"norm2_fwd{l}")
        a_pre, r_act = _matmul(h2, W["mlp1"], "nn", tm=TM, tn=tn_ff, tk=D, name=f"mlp1_fwd{l}", out_dtypes=(F32, BF16),
                               epilogue=lambda acc: (acc, jnp.square(jnp.maximum(acc, 0.0))))
        mo = _matmul(r_act, W["mlp2"], "nn", tm=TM, tn=D, tk=tn_ff, name=f"mlp2_fwd{l}")
        saved.append(dict(modtab=modtab, gains=gains, w_bd=w_bd, p_scale=p_scale, sink=sink, x0=x0, h1=h1, z=z,
                          q2=q2, k2=k2, v2=v2, oa=oa, ob=ob, oc=oc, oa32=oa32, oc32=oc32, lse_a=lse_a, lse_c=lse_c,
                          pooled=pooled, y=y, ao=ao,
                          x1=x1, h2=h2, a_pre=a_pre, r_act=r_act, mo=mo))
        xin, pending = x1, (mo, modtab, 5)

    dxo, loss, d_mo, dg2 = _loss_head(xin, pending[0], pending[1], 5, target.reshape(B * S, D), TR=TR, P=P, name="loss_head")

    big = {k: [None] * L for k in BIG_NAMES}
    big16 = {k: [None] * L for k in BIG_NAMES}
    sm = {k: [None] * L for k in ("b_ada", "norm1", "norm2", "q_norm_a", "k_norm_a", "q_norm_c", "k_norm_c",
                                   "sink_c", "w_pool", "pool_scale")}

    def dw(key, l, a, b, *, tm, tn, name, tk=tk_tok, blocked=True):
        outs = _matmul(a, b, "tn", tm=tm, tn=tn, tk=tk, name=name, out_dtypes=(F32, BF16), out_blocked=blocked)
        if not blocked:
            outs = [o.reshape(N_CHIPS, o.shape[0] // N_CHIPS, o.shape[1]) for o in outs]
        big[key][l], big16[key][l] = outs
    d_cctx = jnp.zeros((D,), F32)
    for l in reversed(range(L)):
        W, sv = weights(l), saved[l]
        modtab = sv["modtab"]
        ride_now = overlap and l == L - 2
        if ride_now:
            early = _LayerReduce(l + 1, [big[k][l + 1] for k in BIG_NAMES], [big16[k][l + 1] for k in BIG_NAMES])
        d_a = _matmul(d_mo, W["mlp2"], "nt", tm=TM, tn=tn_ff, tk=D, name=f"mlp2_bwd{l}", out_dtypes=(BF16,),
                      epilogue=lambda acc, a: (acc * (2.0 * jnp.maximum(a, 0.0)),),
                      extras=[(sv["a_pre"], (TM, tn_ff), lambda m, n: (m, n))], rider=early.to_worker if ride_now else None)
        if ride_now:
            d_a, landed = d_a
            early_send = early.add(landed)
        dw("w_mlp2", l, sv["r_act"], d_mo, tm=tk_d, tn=D, name=f"mlp2_dw{l}", blocked=False)
        d_h2 = _matmul(d_a, W["mlp1"], "nt", tm=TM, tn=D, tk=tn_ff, name=f"mlp1_bwd{l}")
        dw("w_mlp1", l, sv["h2"], d_a, tm=tk_d, tn=tn_ff, name=f"mlp1_dw{l}")
        dx1, dsh2, dsc2, dn2, d_ao, dg1 = _norm_bwd(sv["x1"], d_h2, dxo, modtab, 4, small["norm2"][l][None],
                                                    (sv["ao"], modtab, 2), TR=TR, P=P, name=f"norm2_bwd{l}")
        d_y = _matmul(d_ao, W["out"], "nt", tm=TM, tn=tn_d, tk=D, name=f"out_bwd{l}")
        dw("w_out", l, sv["y"], d_ao, tm=tk_d, tn=D, name=f"out_dw{l}", blocked=False)
        d_pa, d_pb, d_pc, d_ga, d_gb, d_gc = _merge_bwd(d_y, sv["oa"], sv["ob"], sv["oc"], sv["z"], W["a"], W["b"], W["c"],
                                                        D=D, TR=TMG, name=f"merge_bwd{l}")
        d_oa = _matmul(d_pa, W["a"], "nt", tm=TM, tn=Q_WIDTH, tk=tn_d, name=f"br_a_bwd{l}", out_dtypes=(BF16,))
        d_ob = _matmul(d_pb, W["b"], "nt", tm=TM, tn=POOL_WIDTH, tk=tn_d, name=f"br_b_bwd{l}")
        d_oc = _matmul(d_pc, W["c"], "nt", tm=TM, tn=Q_WIDTH, tk=tn_d, name=f"br_c_bwd{l}", out_dtypes=(BF16,))
        dw("w_br_a", l, sv["oa"], d_pa, tm=Q_WIDTH, tn=tn_d, name=f"br_a_dw{l}")
        dw("w_br_b", l, sv["ob"], d_pb, tm=POOL_WIDTH, tn=tn_d, name=f"br_b_dw{l}")
        dw("w_br_c", l, sv["oc"], d_pc, tm=Q_WIDTH, tn=tn_d, name=f"br_c_dw{l}")
        d_u, d_wbd, d_ps = _pool_bwd(d_ob, sv["pooled"], sv["w_bd"], sv["p_scale"], B=B, Tp=Tp, n_ctx=N, name=f"pool_bwd{l}")
        dqa, dka, dva, *arrived = _attn_bwd(sv["q2"], sv["k2"], sv["v2"], d_oa, sv["oa32"], sv["lse_a"], None, branch=0,
                                            B=B, n_ctx=N, window=False, name=f"attn_a_bwd{l}",
                                            rider=early_send if ride_now else None)
        if ride_now:
            early.from_chips = arrived
        dqc, dkc, dvc, dsink = _attn_bwd(sv["q2"], sv["k2"], sv["v2"], d_oc, sv["oc32"], sv["lse_c"], sv["sink"],
                                         branch=1, B=B, n_ctx=N, window=True, name=f"attn_c_bwd{l}")
        dz_a, dgains_a = _qk_prep_bwd(sv["z"], dqa, dka, dva, sv["gains"], cos, sin, branch=0, TR=TR, P=P,
                                      name=f"qk_prep_a_bwd{l}")
        dz_c, dgains_c = _qk_prep_bwd(sv["z"], dqc, dkc, dvc, sv["gains"], cos, sin, branch=1, TR=TR, P=P,
                                      name=f"qk_prep_c_bwd{l}")
        dz = jnp.concatenate([dz_a, dz_c, d_u, d_ga, d_gb, d_gc], axis=1)
        d_h1 = _matmul(dz, W["w_in"], "nt", tm=TM, tn=D, tk=tn_in, name=f"in_bwd{l}")
        dw("w_in", l, sv["h1"], dz, tm=tk_d, tn=tn_in, name=f"in_dw{l}")
        below = (saved[l - 1]["mo"], saved[l - 1]["modtab"], 5) if l > 0 else None
        dx0, dsh1, dsc1, dn1, *lower = _norm_bwd(sv["x0"], d_h1, dx1, modtab, 1, small["norm1"][l][None], below,
                                                 TR=TR, P=P, name=f"norm1_bwd{l}")
        this_dg2 = dg2
        if l > 0:
            d_mo, dg2 = lower

        dm_groups = jnp.concatenate([dsh1, dsc1, dg1, dsh2, dsc2, this_dg2], axis=-1).reshape(B, 2, 6 * D)
        dm = jnp.concatenate([dm_groups[:, 1], jnp.sum(dm_groups[:, 0], axis=0, keepdims=True),
                              jnp.zeros((rows16 - B - 1, 6 * D), F32)], axis=0)
        dm_bf = dm.astype(BF16)
        d_s = _matmul(dm_bf, W["ada"], "nt", tm=rows16, tn=D, tk=tn_ada, name=f"ada_bwd{l}")
        dw("w_ada", l, s_rows, dm_bf, tm=tk_d, tn=tn_ada, tk=rows16, name=f"ada_dw{l}")
        db_ada, dcc = _ada_bwd_rows(dm, d_s, cc, f"ada_rows_bwd{l}")
        d_cctx = d_cctx + dcc[B]

        sm["b_ada"][l] = db_ada[0]
        sm["norm1"][l] = jnp.sum(dn1, axis=(0, 1))
        sm["norm2"][l] = jnp.sum(dn2, axis=(0, 1))
        dgh = jnp.stack([dgains_a, dgains_c]).reshape(2, QKV_WIDTH // HEAD_DIM, HEAD_DIM)
        sm["q_norm_a"][l] = jnp.sum(dgh[0, :N_HEADS], axis=0)
        sm["k_norm_a"][l] = jnp.sum(dgh[0, N_HEADS:N_HEADS + N_KV], axis=0)
        sm["q_norm_c"][l] = jnp.sum(dgh[1, :N_HEADS], axis=0)
        sm["k_norm_c"][l] = jnp.sum(dgh[1, N_HEADS:N_HEADS + N_KV], axis=0)
        sm["sink_c"][l] = jnp.sum(dsink[:, :N_HEADS, 0], axis=0)
        sm["w_pool"][l] = jnp.stack([d_wbd[g * POOL_CH:(g + 1) * POOL_CH, g * POOL_CH:(g + 1) * POOL_CH]
                                     for g in range(POOL_WIDTH // POOL_CH)])
        sm["pool_scale"][l] = d_ps[0]
        dxo = dx0

    grad_x = dxo.reshape(B, Tp, D)[:, N:]
    small_grads = {k: jnp.stack(v) for k, v in sm.items()}
    small_grads["c_ctx"] = d_cctx
    return loss, grad_x, small_grads, big, big16, (early if overlap else None)


SMALL_NAMES = ("c_ctx", "b_ada", "norm1", "norm2", "q_norm_a", "k_norm_a", "q_norm_c", "k_norm_c", "sink_c",
               "w_pool", "pool_scale")
BIG_NAMES = ("w_ada", "w_in", "w_br_a", "w_br_b", "w_br_c", "w_out", "w_mlp1", "w_mlp2")
WEIGHT_NAMES = ("c_ctx", "w_ada", "b_ada", "norm1", "norm2", "w_in", "q_norm_a", "k_norm_a", "q_norm_c", "k_norm_c",
                "sink_c", "w_pool", "pool_scale", "w_br_a", "w_br_b", "w_br_c", "w_out", "w_mlp1", "w_mlp2")


def _pack(parts, rows):
    flat = jnp.concatenate([p.reshape(-1).astype(F32) for p in parts])
    return jnp.pad(flat, (0, rows * LANES - flat.shape[0])).reshape(rows, LANES)


def _unpack(packed, like):
    flat, out, at = packed.reshape(-1), [], 0
    for p in like:
        out.append(flat[at:at + p.size].reshape(p.shape))
        at += p.size
    return out


def _split_by_bytes(arrays):
    load, owner = [0, 0], [0] * len(arrays)
    for w in sorted(range(len(arrays)), key=lambda w: -arrays[w].size):
        owner[w] = 0 if load[0] <= load[1] else 1
        load[owner[w]] += arrays[w].size
    return owner


class _LayerReduce:
    def __init__(self, layer, partials, partials16):
        self.layer, self.partials = layer, list(partials)
        self.workers = _split_by_bytes(self.partials)
        self.to_worker = _ToSibling([g.reshape(-1, g.shape[-1]) for g in partials16], [1 - wk for wk in self.workers])
        x, y, c = _place()
        self.core = c.astype(jnp.int32).reshape(1)
        self.chip = (2 * x + y).astype(jnp.int32).reshape(1)

    def add(self, landed):
        sums = [_add_landed(g.reshape(-1, g.shape[-1]), r, self.core, wk, f"grads{self.layer}_add_sibling_{k}")
                for k, g, r, wk in zip(BIG_NAMES, self.partials, landed, self.workers)]
        self.in_chip = [h.reshape(g.shape) for g, (h, _) in zip(self.partials, sums)]
        return _ChipSend([h.reshape(g.shape) for g, (_, h) in zip(self.partials, sums)], self.workers)

    def sum(self, from_chips):
        return [_sum_chips(h, r, self.chip, self.core, wk, f"grads{self.layer}_sum_chips_{k}")
                for k, h, r, wk in zip(BIG_NAMES, self.in_chip, from_chips, self.workers)]


def kernel(x, c, ctx, c_ctx, w_ada, b_ada, norm1, norm2, w_in, q_norm_a, k_norm_a, q_norm_c, k_norm_c, sink_c, w_pool, pool_scale, w_br_a, w_br_b, w_br_c, w_out, w_mlp1, w_mlp2, loss_target, m_c_ctx, m_w_ada, m_b_ada, m_norm1, m_norm2, m_w_in, m_q_norm_a, m_k_norm_a, m_q_norm_c, m_k_norm_c, m_sink_c, m_w_pool, m_pool_scale, m_w_br_a, m_w_br_b, m_w_br_c, m_w_out, m_w_mlp1, m_w_mlp2, v_c_ctx, v_w_ada, v_b_ada, v_norm1, v_norm2, v_w_in, v_q_norm_a, v_k_norm_a, v_q_norm_c, v_k_norm_c, v_sink_c, v_w_pool, v_pool_scale, v_w_br_a, v_w_br_b, v_w_br_c, v_w_out, v_w_mlp1, v_w_mlp2):
    given = dict(locals())
    w = {k: given[k] for k in WEIGHT_NAMES}
    m = {k: given["m_" + k] for k in WEIGHT_NAMES}
    v = {k: given["v_" + k] for k in WEIGHT_NAMES}

    shards = [w[k].astype(BF16) for k in BIG_NAMES]
    assert all(s.shape[0] == 2 for s in shards)
    first_layer = dict(zip(BIG_NAMES, _ride_alone(_LayerGather(shards, 0), "gather_weights0")))
    small = {k: w[k] for k in SMALL_NAMES}
    loss_part, grad_x, small_grads, big_grads, big_grads16, early = _local_step(
        x, c, ctx, c_ctx, small, [first_layer, None], loss_target, rider=_LayerGather(shards, 1), overlap=True)

    late = _LayerReduce(0, [big_grads[k][0] for k in BIG_NAMES], [big_grads16[k][0] for k in BIG_NAMES])
    send = late.add(_ride_alone(late.to_worker, "grads0_to_sibling"))
    reduced = [late.sum(_ride_alone(send, "grads0_to_chips")), early.sum(early.from_chips)]
    n_big = len(BIG_NAMES)
    shared = _ride_alone(_ToSibling(reduced[0] + reduced[1], late.workers + early.workers), "grads_share")
    grads, deltas, new_m, new_v = {}, {}, {}, {}
    for i, k in enumerate(BIG_NAMES):
        assert late.workers[i] == early.workers[i]
        grads[k], deltas[k], new_m[k], new_v[k] = _adamw(
            w[k], (reduced[0][i], reduced[1][i]), (shared[i], shared[n_big + i]), m[k], v[k], late.core, late.workers[i],
            f"adamw_{k}")

    sizes = sum(w[k].size for k in SMALL_NAMES) + LANES
    rows = -(-sizes // (8 * LANES)) * 8
    parts = _gather_small(_pack([small_grads[k] for k in SMALL_NAMES] + [loss_part[0]], rows), "gather_small")
    zero = jnp.zeros((LANES,), F32)
    packed = [_pack([t[k] for k in SMALL_NAMES] + [zero], rows) for t in (w, m, v)]
    outs = _adamw_small(packed[0], parts.reshape(8, rows, LANES), packed[1], packed[2], "adamw_small")
    like = [w[k] for k in SMALL_NAMES] + [zero]
    for store, packed_out in zip((grads, deltas, new_m, new_v), outs):
        pieces = _unpack(packed_out, like)
        for k, piece in zip(SMALL_NAMES, pieces):
            store[k] = piece
        if store is grads:
            loss = pieces[-1][0]

    return (loss, grad_x, *[grads[k] for k in WEIGHT_NAMES], *[deltas[k] for k in WEIGHT_NAMES],
            *[new_m[k] for k in WEIGHT_NAMES], *[new_v[k] for k in WEIGHT_NAMES])
```

```python
import functools

import jax
import jax.numpy as jnp
from jax import lax
from jax.experimental import pallas as pl
from jax.experimental.pallas import tpu as pltpu

F32 = jnp.float32
BF16 = jnp.bfloat16

HEAD_DIM = 64
GRID_W = 64
AXIS_DIM = HEAD_DIM // 2
ROPE_THETA = 10000.0
N_HEADS = 6
N_KV = 2
N_GROUP = N_HEADS // N_KV
POOL_CH = 64
POOL_WIDTH = 256
POOL_WINDOWS = (2, 4, 8, 16)
WINDOW = 128
Q_BLOCK = 128
Q_WIDTH = N_HEADS * HEAD_DIM
KV_WIDTH = N_KV * HEAD_DIM
GATE_COL = 2 * (Q_WIDTH + 2 * KV_WIDTH) + POOL_WIDTH
U_COL = 2 * (Q_WIDTH + 2 * KV_WIDTH)
EPS = 1e-6
NEG = -1e30
ADAM_LR = 0.001
ADAM_B1 = 0.9
ADAM_B2 = 0.999
ADAM_EPS = 1e-08
ADAM_WD = 0.01
ADAM_STEP = 10

N_CHIPS = 4
LANES = 128
POOL_PAD = 16
VMEM_LIMIT = 48 * 1024 * 1024
MESH = pl.DeviceIdType.MESH
ANY = pl.BlockSpec(memory_space=pl.ANY)


def _params(sem):
    return pltpu.CompilerParams(dimension_semantics=sem, vmem_limit_bytes=VMEM_LIMIT)


def _sds(shape, dtype):
    return jax.ShapeDtypeStruct(tuple(shape), dtype)


class _Opnd:
    def __init__(self, arr, kind="plain"):
        self.arr, self.kind = arr, kind

    @property
    def shape(self):
        a = self.arr
        if self.kind == "plain":
            return a.shape
        if self.kind == "bcols":
            return (a.shape[1], N_CHIPS * a.shape[2])
        return (N_CHIPS * a.shape[1], a.shape[2])

    def spec(self, tr, tc, fn):
        a = self.arr
        if self.kind == "plain":
            return pl.BlockSpec((tr, tc), lambda *g: fn(*g))
        if self.kind == "bcols":
            assert a.shape[2] % tc == 0, (a.shape, tc)
            per = a.shape[2] // tc

            def im(*g):
                ri, ci = fn(*g)
                return (ci // per, ri, ci % per)
            return pl.BlockSpec((None, tr, tc), im)
        assert a.shape[1] % tr == 0, (a.shape, tr)
        per = a.shape[1] // tr

        def im(*g):
            ri, ci = fn(*g)
            return (ri // per, ri % per, ci)
        return pl.BlockSpec((None, tr, tc), im)


def _matmul(a, b, mode, *, tm, tn, tk, name, out_dtypes=(F32,), epilogue=None, extras=(), out_blocked=False, rider=None):
    if not isinstance(a, _Opnd):
        a = _Opnd(a)
    if not isinstance(b, _Opnd):
        b = _Opnd(b)
    if mode == "nn":
        (M, K), (K2, N) = a.shape, b.shape
        a_spec = a.spec(tm, tk, lambda m, n, k: (m, k))
        b_spec = b.spec(tk, tn, lambda m, n, k: (k, n))
        dims = (((1,), (0,)), ((), ()))
    elif mode == "nt":
        (M, K), (N, K2) = a.shape, b.shape
        a_spec = a.spec(tm, tk, lambda m, n, k: (m, k))
        b_spec = b.spec(tn, tk, lambda m, n, k: (n, k))
        dims = (((1,), (1,)), ((), ()))
    else:
        (K, M), (K2, N) = a.shape, b.shape
        a_spec = a.spec(tk, tm, lambda m, n, k: (k, m))
        b_spec = b.spec(tk, tn, lambda m, n, k: (k, n))
        dims = (((0,), (0,)), ((), ()))
    assert K == K2 and M % tm == 0 and N % tn == 0 and K % tk == 0, (name, M, N, K, K2, tm, tn, tk)
    nk = K // tk
    n_extra = len(extras)
    n_out = len(out_dtypes)
    extra_specs = [pl.BlockSpec(bs, functools.partial(lambda m, n, k, f: f(m, n), f=f)) for (_, bs, f) in extras]
    if out_blocked:
        assert (N // N_CHIPS) % tn == 0
        per = (N // N_CHIPS) // tn
        out_shape = [_sds((N_CHIPS, M, N // N_CHIPS), dt) for dt in out_dtypes]
        out_specs = [pl.BlockSpec((None, tm, tn), lambda m, n, k: (n // per, m, n % per)) for _ in out_dtypes]
    else:
        out_shape = [_sds((M, N), dt) for dt in out_dtypes]
        out_specs = [pl.BlockSpec((tm, tn), lambda m, n, k: (m, n)) for _ in out_dtypes]

    in_place = nk > 1 and epilogue is None and out_dtypes[0] == F32

    grid = (M // tm, N // tn, nk)
    own_scratch = [pltpu.VMEM((tm, tn), F32)] if nk > 1 and not in_place else []

    def body(*refs):
        refs, finish_ride = _ride(rider, refs, 2 + n_extra, n_out, len(own_scratch), grid)
        a_ref, b_ref = refs[0], refs[1]
        extra_refs = refs[2:2 + n_extra]
        out_refs = refs[2 + n_extra:2 + n_extra + n_out]
        acc_ref = out_refs[0] if in_place else (refs[2 + n_extra + n_out] if nk > 1 else None)
        k = pl.program_id(2)
        prod = lax.dot_general(a_ref[...].astype(BF16), b_ref[...].astype(BF16), dims, preferred_element_type=F32)

        def finish(acc):
            outs = epilogue(acc, *[r[...] for r in extra_refs]) if epilogue is not None else (acc,) * n_out
            for o_ref, o in zip(out_refs, outs):
                o_ref[...] = o.astype(o_ref.dtype)

        if nk == 1:
            finish(prod)
        elif in_place:
            @pl.when(k == 0)
            def _():
                acc_ref[...] = prod

            @pl.when(k > 0)
            def _():
                acc_ref[...] += prod

            if n_out > 1:
                @pl.when(k == nk - 1)
                def _():
                    for o_ref in out_refs[1:]:
                        o_ref[...] = acc_ref[...].astype(o_ref.dtype)
        else:
            @pl.when(k == 0)
            def _():
                acc_ref[...] = prod

            @pl.when(k > 0)
            def _():
                acc_ref[...] += prod

            @pl.when(k == nk - 1)
            def _():
                finish(acc_ref[...])

        finish_ride()

    ins, in_specs, out_specs, out_shape, scratch = _hitch(
        rider, [a.arr, b.arr] + [e[0] for e in extras], [a_spec, b_spec] + extra_specs, out_specs, out_shape, own_scratch)
    outs = pl.pallas_call(
        body, name=name, grid=grid, in_specs=in_specs, out_specs=out_specs, out_shape=out_shape, scratch_shapes=scratch,
        compiler_params=_params(("arbitrary",) * 3 if rider is not None else ("parallel", "parallel", "arbitrary")),
    )(*ins)
    if rider is not None:
        return (outs[0] if n_out == 1 else outs[:n_out]), outs[n_out:]
    return outs[0] if n_out == 1 else outs


def _tile(n, cands):
    for t in cands:
        if n % t == 0:
            return t
    return n


def _grp(i, P):
    return 2 * (i // P) + jnp.minimum(i % P, 1)


def _mod_spec(D, P, part, B):
    return pl.BlockSpec((1, 1, D), lambda i: (jnp.where(i % P == 0, B, i // P), 0, part))


def _res_norm(x, pending, modtab, shift_part, scale_part, gain, *, TR, P, name):
    T, D = x.shape
    row = pl.BlockSpec((TR, D), lambda i: (i, 0))
    has_branch = pending is not None
    ins, specs = [x], [row]
    if has_branch:
        branch, gate_tab, gate_part = pending
        ins += [branch, gate_tab]
        specs += [row, _mod_spec(D, P, gate_part, T // (TR * P))]
    ins += [modtab, modtab, gain]
    specs += [_mod_spec(D, P, shift_part, T // (TR * P)), _mod_spec(D, P, scale_part, T // (TR * P)), pl.BlockSpec((1, D), lambda i: (0, 0))]

    def body(*refs):
        if has_branch:
            x_ref, br_ref, g_ref, sh_ref, sc_ref, gn_ref, xo_ref, h_ref = refs
            xv = x_ref[...] + g_ref[0] * br_ref[...]
        else:
            x_ref, sh_ref, sc_ref, gn_ref, xo_ref, h_ref = refs
            xv = x_ref[...]
        xo_ref[...] = xv
        y = xv * lax.rsqrt(jnp.mean(xv * xv, axis=-1, keepdims=True) + EPS) * gn_ref[...]
        h_ref[...] = (y * (1.0 + sc_ref[0]) + sh_ref[0]).astype(BF16)

    return pl.pallas_call(
        body, name=name, grid=(T // TR,), in_specs=specs, out_specs=[row, row],
        out_shape=[_sds((T, D), F32), _sds((T, D), BF16)], compiler_params=_params(("parallel",)),
    )(*ins)


def _norm_bwd(x, dh, dres, modtab, scale_part, gain, below, *, TR, P, name):
    T, D = x.shape
    G = 2 * (T // (TR * P))
    row = pl.BlockSpec((TR, D), lambda i: (i, 0))
    acc = pl.BlockSpec((1, 1, D), lambda i: (_grp(i, P), 0, 0))
    has_below = below is not None

    def body(*refs):
        if has_below:
            x_ref, dh_ref, dres_ref, sc_ref, gn_ref, br_ref, g_ref, dx_ref, dsh_ref, dsc_ref, dgn_ref, db_ref, dg_ref = refs
        else:
            x_ref, dh_ref, dres_ref, sc_ref, gn_ref, dx_ref, dsh_ref, dsc_ref, dgn_ref = refs
        r = pl.program_id(0) % P
        xv, dhv, gn = x_ref[...], dh_ref[...], gn_ref[...]
        rstd = lax.rsqrt(jnp.mean(xv * xv, axis=-1, keepdims=True) + EPS)
        xhat = xv * rstd
        dn = dhv * (1.0 + sc_ref[0])
        dxhat = dn * gn
        dxv = dres_ref[...] + rstd * (dxhat - xhat * jnp.mean(dxhat * xhat, axis=-1, keepdims=True))
        dx_ref[...] = dxv
        parts = [jnp.sum(dhv, axis=0, keepdims=True), jnp.sum(dhv * (xhat * gn), axis=0, keepdims=True),
                 jnp.sum(dn * xhat, axis=0, keepdims=True)]
        outs = [dsh_ref, dsc_ref, dgn_ref]
        if has_below:
            db_ref[...] = (dxv * g_ref[0]).astype(BF16)
            parts.append(jnp.sum(dxv * br_ref[...], axis=0, keepdims=True))
            outs.append(dg_ref)

        @pl.when(r <= 1)
        def _():
            for o_ref, part in zip(outs, parts):
                o_ref[0] = part

        @pl.when(r > 1)
        def _():
            for o_ref, part in zip(outs, parts):
                o_ref[0] += part

    ins = [x, dh, dres, modtab, gain]
    in_specs = [row, row, row, _mod_spec(D, P, scale_part, T // (TR * P)), pl.BlockSpec((1, D), lambda i: (0, 0))]
    out_specs, out_shape = [row, acc, acc, acc], [_sds((T, D), F32)] + [_sds((G, 1, D), F32)] * 3
    if has_below:
        branch, gate_tab, gate_part = below
        ins += [branch, gate_tab]
        in_specs += [row, _mod_spec(D, P, gate_part, T // (TR * P))]
        out_specs += [row, acc]
        out_shape += [_sds((T, D), BF16), _sds((G, 1, D), F32)]
    return pl.pallas_call(body, name=name, grid=(T // TR,), in_specs=in_specs, out_specs=out_specs, out_shape=out_shape,
                          compiler_params=_params(("arbitrary",)))(*ins)


def _loss_head(x, branch, modtab, gate_part, target, *, TR, P, name):
    T, D = x.shape
    row = pl.BlockSpec((TR, D), lambda i: (i, 0))
    tgt = pl.BlockSpec((TR, D), lambda i: ((i // P) * (P - 1) + jnp.maximum(i % P - 1, 0), 0))
    one = pl.BlockSpec((1, LANES), lambda i: (0, 0))

    G = 2 * (T // (TR * P))
    acc = pl.BlockSpec((1, 1, D), lambda i: (_grp(i, P), 0, 0))

    def body(x_ref, br_ref, g_ref, t_ref, dy_ref, loss_ref, db_ref, dg_ref):
        i = pl.program_id(0)
        r = i % P

        @pl.when(i == 0)
        def _():
            loss_ref[...] = jnp.zeros_like(loss_ref)

        @pl.when(r == 0)
        def _():
            dy_ref[...] = jnp.zeros_like(dy_ref)
            db_ref[...] = jnp.zeros_like(db_ref)
            dg_ref[...] = jnp.zeros_like(dg_ref)

        @pl.when(r > 0)
        def _():
            brv, g = br_ref[...], g_ref[0]
            err = x_ref[...] + g * brv - t_ref[...]
            dy = err / D
            dy_ref[...] = dy
            db_ref[...] = (dy * g).astype(BF16)
            part = jnp.sum(dy * brv, axis=0, keepdims=True)
            per_tok = jnp.mean(err * err, axis=-1, keepdims=True)
            loss_ref[...] += 0.5 * jnp.sum(per_tok, axis=0, keepdims=True)

            @pl.when(r == 1)
            def _():
                dg_ref[0] = part

            @pl.when(r > 1)
            def _():
                dg_ref[0] += part

    return pl.pallas_call(
        body, name=name, grid=(T // TR,), in_specs=[row, row, _mod_spec(D, P, gate_part, T // (TR * P)), tgt],
        out_specs=[row, one, row, acc],
        out_shape=[_sds((T, D), F32), _sds((1, LANES), F32), _sds((T, D), BF16), _sds((G, 1, D), F32)],
        compiler_params=_params(("arbitrary",)),
    )(x, branch, modtab, target)


QKV_WIDTH = Q_WIDTH + 2 * KV_WIDTH
QK_NORMED = 4


def _seg_mean(v):
    lane = lax.broadcasted_iota(jnp.int32, v.shape, 1)
    lo = lane < HEAD_DIM
    s0 = jnp.sum(jnp.where(lo, v, 0.0), axis=-1, keepdims=True)
    s1 = jnp.sum(jnp.where(lo, 0.0, v), axis=-1, keepdims=True)
    return jnp.where(lo, s0, s1) * (1.0 / HEAD_DIM)


def _pair_swap(v):
    lane = lax.broadcasted_iota(jnp.int32, v.shape, 1)
    return jnp.where((lane & 1) == 0, pltpu.roll(v, LANES - 1, 1), pltpu.roll(v, 1, 1))


def _chunk(c):
    return slice(c * LANES, (c + 1) * LANES)


def _qk_prep(z, gains, cos, sin, *, TR, P, name):
    T = z.shape[0]

    def body(z_ref, g_ref, c_ref, s_ref, q_ref, k_ref, v_ref):
        cs, sn = c_ref[...], s_ref[...]
        for ch in range(QK_NORMED):
            xv = z_ref[:, _chunk(ch)]
            y = xv * lax.rsqrt(_seg_mean(xv * xv) + EPS) * g_ref[0, :, _chunk(ch)]
            out = (y * cs + _pair_swap(y) * sn).astype(BF16)
            if ch < QK_NORMED - 1:
                q_ref[:, _chunk(ch)] = out
            else:
                k_ref[...] = out
        v_ref[...] = z_ref[:, _chunk(QK_NORMED)].astype(BF16)

    def out(width):
        return pl.BlockSpec((None, TR, width), lambda i, j: (j, i, 0))
    return pl.pallas_call(
        body, name=name, grid=(T // TR, 2),
        in_specs=[pl.BlockSpec((TR, QKV_WIDTH), lambda i, j: (i, j)),
                  pl.BlockSpec((1, 1, QKV_WIDTH), lambda i, j: (j, 0, 0)),
                  pl.BlockSpec((TR, LANES), lambda i, j: (i % P, 0)),
                  pl.BlockSpec((TR, LANES), lambda i, j: (i % P, 0))],
        out_specs=[out(Q_WIDTH), out(KV_WIDTH), out(KV_WIDTH)],
        out_shape=[_sds((2, T, Q_WIDTH), BF16), _sds((2, T, KV_WIDTH), BF16), _sds((2, T, KV_WIDTH), BF16)],
        compiler_params=_params(("parallel", "parallel")),
    )(z, gains, cos, sin)


def _qk_prep_bwd(z, dq, dk, dv, gains, cos, sin, *, branch, TR, P, name):
    T = z.shape[0]
    nt = T // TR

    def body(z_ref, dq_ref, dk_ref, dv_ref, g_ref, c_ref, s_ref, dz_ref, dg_ref):
        i = pl.program_id(0)
        cs, sn = c_ref[...], s_ref[...]
        parts = []
        for ch in range(QK_NORMED):
            xv, g = z_ref[:, _chunk(ch)], g_ref[0, :, _chunk(ch)]
            dout = dq_ref[:, _chunk(ch)] if ch < QK_NORMED - 1 else dk_ref[...]
            dy = dout * cs + _pair_swap(dout * sn)
            rstd = lax.rsqrt(_seg_mean(xv * xv) + EPS)
            xhat = xv * rstd
            dxhat = dy * g
            dz_ref[:, _chunk(ch)] = (rstd * (dxhat - xhat * _seg_mean(dxhat * xhat))).astype(BF16)
            parts.append(jnp.sum(dy * xhat, axis=0, keepdims=True))
        dz_ref[:, _chunk(QK_NORMED)] = dv_ref[...].astype(BF16)
        parts.append(jnp.zeros((1, LANES), F32))
        part = jnp.concatenate(parts, axis=1)

        @pl.when(i == 0)
        def _():
            dg_ref[0] = part

        @pl.when(i > 0)
        def _():
            dg_ref[0] += part

    def rows(width, col=0):
        return pl.BlockSpec((TR, width), lambda i: (i, col))
    return pl.pallas_call(
        body, name=name, grid=(nt,),
        in_specs=[rows(QKV_WIDTH, branch), rows(Q_WIDTH), rows(KV_WIDTH), rows(KV_WIDTH),
                  pl.BlockSpec((1, 1, QKV_WIDTH), lambda i: (branch, 0, 0)),
                  pl.BlockSpec((TR, LANES), lambda i: (i % P, 0)),
                  pl.BlockSpec((TR, LANES), lambda i: (i % P, 0))],
        out_specs=[rows(QKV_WIDTH), pl.BlockSpec((1, 1, QKV_WIDTH), lambda i: (0, 0, 0))],
        out_shape=[_sds((T, QKV_WIDTH), BF16), _sds((1, 1, QKV_WIDTH), F32)],
        compiler_params=_params(("arbitrary",)),
    )(z, dq, dk, dv, gains, cos, sin)


NT_DIMS = (((1,), (1,)), ((), ()))
TN_DIMS = (((0,), (0,)), ((), ()))
QROWS = N_GROUP * Q_BLOCK
SCORE_SCALE = HEAD_DIM ** -0.5
BAND = Q_BLOCK + 2 * WINDOW
FWD_LATENT_CHUNK = 256
BWD_LATENT_CHUNK = 1024


def _move_head(block, half_from, half_to):
    lane = lax.broadcasted_iota(jnp.int32, block.shape, 1)
    src = block if half_from == half_to else pltpu.roll(block, HEAD_DIM, 1)
    keep = (lane < HEAD_DIM) if half_to == 0 else (lane >= HEAD_DIM)
    return jnp.where(keep, src, 0.0)


def _stack_heads(lane_block, j):
    pieces = []
    for h in range(N_GROUP * j, N_GROUP * (j + 1)):
        pieces.append(_move_head(lane_block(h // 2), h % 2, j))
    return jnp.concatenate(pieces, axis=0)


def _lane_blocks(ref):
    return lambda m: ref[:, m * LANES:(m + 1) * LANES].astype(F32)


def _unstack_heads(stacked, ref):
    heads = []
    for h in range(N_HEADS):
        j, r = h // N_GROUP, h % N_GROUP
        heads.append(_move_head(stacked[j][r * Q_BLOCK:(r + 1) * Q_BLOCK], j, h % 2))
    for m in range(N_HEADS // 2):
        ref[:, m * LANES:(m + 1) * LANES] = (heads[2 * m] + heads[2 * m + 1]).astype(ref.dtype)


def _key_chunks(i, latent, *, n_ctx, t_all, window, chunk, latent_chunk):
    ctx = [(s, chunk, False) for s in range(0, n_ctx, chunk)]
    if not latent:
        return ctx
    if not window:
        wide = latent_chunk if (t_all - n_ctx) % latent_chunk == 0 else chunk
        return ctx + [(s, wide, False) for s in range(n_ctx, t_all, wide)]
    start = pl.multiple_of(jnp.minimum((i - 1) * Q_BLOCK, t_all - BAND), Q_BLOCK)
    band_chunk = BAND if latent_chunk >= BAND else (chunk if BAND % chunk == 0 else Q_BLOCK)
    return ctx + [(start + s, band_chunk, True) for s in range(0, BAND, band_chunk)]


def _scores(q, k_ref, i, start, size, masked, *, n_ctx):
    s = lax.dot_general(q, k_ref[pl.ds(start, size), :], NT_DIMS, preferred_element_type=F32)
    if masked:
        qpos = (i * Q_BLOCK - n_ctx) + (lax.broadcasted_iota(jnp.int32, (QROWS, size), 0) & (Q_BLOCK - 1))
        kpos = (start - n_ctx) + lax.broadcasted_iota(jnp.int32, (QROWS, size), 1)
        valid = (kpos - qpos <= WINDOW) & (qpos - kpos <= WINDOW) & (kpos >= 0)
        s = jnp.where(valid, s, NEG)
    return s


def _sink_column(sink_ref, j):
    r = lax.broadcasted_iota(jnp.int32, (QROWS, 1), 0)
    s0, s1, s2 = sink_ref[j * N_GROUP], sink_ref[j * N_GROUP + 1], sink_ref[j * N_GROUP + 2]
    return jnp.where(r < Q_BLOCK, s0, jnp.where(r < 2 * Q_BLOCK, s1, s2))


def _attn_specs(Tp, branch):
    nq = Tp // Q_BLOCK
    q_in = pl.BlockSpec((None, Q_BLOCK, Q_WIDTH), lambda b, i: (branch, b * nq + i, 0))
    kv_in = pl.BlockSpec((None, Tp, KV_WIDTH), lambda b, i: (branch, b, 0))
    q_out = pl.BlockSpec((Q_BLOCK, Q_WIDTH), lambda b, i: (b * nq + i, 0))
    kv_out = pl.BlockSpec((Tp, KV_WIDTH), lambda b, i: (b, 0))
    return q_in, kv_in, q_out, kv_out


def _attn_chunk(Tp):
    return 256 if Tp % 256 == 0 else Q_BLOCK


def _attn_fwd(q, k, v, sink, *, branch, B, n_ctx, window, name, rider=None):
    T = q.shape[1]
    Tp = T // B
    nq = Tp // Q_BLOCK
    has_sink = sink is not None
    n_in = 4 if has_sink else 3
    q_in, kv_in, q_out, _ = _attn_specs(Tp, branch)
    lse_spec = pl.BlockSpec((None, N_KV * QROWS, 1), lambda b, i: (b * nq + i, 0, 0))

    def body(*refs):
        refs, finish_ride = _ride(rider, refs, n_in, 3, 0, (B, nq))
        sink_ref = refs.pop(0) if has_sink else None
        q_ref, k_ref, v_ref, o_ref, o32_ref, lse_ref = refs
        i = pl.program_id(1)

        def run(latent):
            outs = []
            for j in range(N_KV):
                qv = (_stack_heads(_lane_blocks(q_ref), j) * SCORE_SCALE).astype(BF16)
                if has_sink:
                    m, l = _sink_column(sink_ref, j), jnp.ones((QROWS, 1), F32)
                else:
                    m, l = jnp.full((QROWS, 1), NEG, F32), jnp.zeros((QROWS, 1), F32)
                acc = jnp.zeros((QROWS, LANES), F32)
                for start, size, masked in _key_chunks(i, latent, n_ctx=n_ctx, t_all=Tp, window=window,
                                                       chunk=_attn_chunk(Tp), latent_chunk=FWD_LATENT_CHUNK):
                    s = _scores(qv, k_ref, i, start, size, masked, n_ctx=n_ctx)
                    m_new = jnp.maximum(m, jnp.max(s, axis=-1, keepdims=True))
                    alpha = jnp.exp(m - m_new)
                    p = jnp.exp(s - m_new)
                    l = l * alpha + jnp.sum(p, axis=-1, keepdims=True)
                    acc = acc * alpha + jnp.dot(p.astype(BF16), v_ref[pl.ds(start, size), :], preferred_element_type=F32)
                    m = m_new
                outs.append(acc * (1.0 / l))
                lse_ref[j * QROWS:(j + 1) * QROWS, :] = m + jnp.log(l)
            _unstack_heads(outs, o_ref)
            _unstack_heads(outs, o32_ref)

        @pl.when(i < n_ctx // Q_BLOCK)
        def _():
            run(False)

        @pl.when(i >= n_ctx // Q_BLOCK)
        def _():
            run(True)

        finish_ride()

    ins, specs = [q, k, v], [q_in, kv_in, kv_in]
    if has_sink:
        ins, specs = [sink] + ins, [pl.BlockSpec(memory_space=pltpu.SMEM)] + specs
    out_specs = [q_out, q_out, lse_spec]
    out_shape = [_sds((T, Q_WIDTH), BF16), _sds((T, Q_WIDTH), F32), _sds((T // Q_BLOCK, N_KV * QROWS, 1), F32)]
    ins, specs, out_specs, out_shape, scratch = _hitch(rider, ins, specs, out_specs, out_shape, [])
    return pl.pallas_call(
        body, name=name, grid=(B, nq), in_specs=specs, out_specs=out_specs, out_shape=out_shape, scratch_shapes=scratch,
        compiler_params=_params(("arbitrary", "arbitrary") if rider is not None else ("parallel", "parallel")),
    )(*ins)


def _attn_bwd(q, k, v, do, o32, lse, sink, *, branch, B, n_ctx, window, name, rider=None):
    T = q.shape[1]
    Tp = T // B
    nq = Tp // Q_BLOCK
    has_sink = sink is not None
    q_in, kv_in, q_out, kv_out = _attn_specs(Tp, branch)
    lse_spec = pl.BlockSpec((None, N_KV * QROWS, 1), lambda b, i: (b * nq + i, 0, 0))
    sink_spec = pl.BlockSpec((None, 8, LANES), lambda b, i: (b, 0, 0))

    def body(*refs):
        refs, finish_ride = _ride(rider, refs, 7 if has_sink else 6, 4 if has_sink else 3, 2, (B, nq))
        if has_sink:
            sink_ref, q_ref, k_ref, v_ref, do_ref, o_ref, lse_ref, dq_ref, dk_ref, dv_ref, ds_ref, dkt_ref, dvt_ref = refs
        else:
            q_ref, k_ref, v_ref, do_ref, o_ref, lse_ref, dq_ref, dk_ref, dv_ref, dkt_ref, dvt_ref = refs
        i = pl.program_id(1)

        @pl.when(i == 0)
        def _():
            dk_ref[...] = jnp.zeros_like(dk_ref)
            dv_ref[...] = jnp.zeros_like(dv_ref)
            if not window:
                dkt_ref[...] = jnp.zeros_like(dkt_ref)
                dvt_ref[...] = jnp.zeros_like(dvt_ref)
            if has_sink:
                ds_ref[...] = jnp.zeros_like(ds_ref)

        def run(latent):
            upd = jnp.zeros((8, LANES), F32)
            do_blocks, o_blocks = _lane_blocks(do_ref), _lane_blocks(o_ref)
            qvs = [(_stack_heads(_lane_blocks(q_ref), j) * SCORE_SCALE).astype(BF16) for j in range(N_KV)]
            dovs = [_stack_heads(do_blocks, j).astype(BF16) for j in range(N_KV)]
            deltas = [jnp.sum(_stack_heads(lambda m: do_blocks(m) * o_blocks(m), j), axis=-1, keepdims=True)
                      for j in range(N_KV)]
            lses = [lse_ref[j * QROWS:(j + 1) * QROWS, :] for j in range(N_KV)]
            q_all, do_all = jnp.concatenate(qvs, axis=0), jnp.concatenate(dovs, axis=0)
            q_all_t, do_all_t = q_all.T, do_all.T
            dqs = [jnp.zeros((QROWS, LANES), F32) for _ in range(N_KV)]
            for start, size, masked in _key_chunks(i, latent, n_ctx=n_ctx, t_all=Tp, window=window,
                                                   chunk=_attn_chunk(Tp), latent_chunk=BWD_LATENT_CHUNK):
                rows = pl.ds(start, size)
                ds_all, p_all = [], []
                for j in range(N_KV):
                    p = jnp.exp(_scores(qvs[j], k_ref, i, start, size, masked, n_ctx=n_ctx) - lses[j])
                    dp = lax.dot_general(dovs[j], v_ref[rows, :], NT_DIMS, preferred_element_type=F32)
                    ds = (p * (dp - deltas[j])).astype(BF16)
                    dqs[j] = dqs[j] + jnp.dot(ds, k_ref[rows, :], preferred_element_type=F32)
                    ds_all.append(ds)
                    p_all.append(p.astype(BF16))
                ds_cat, p_cat = jnp.concatenate(ds_all, axis=0), jnp.concatenate(p_all, axis=0)
                if window:
                    dk_ref[rows, :] += lax.dot_general(ds_cat, q_all, TN_DIMS, preferred_element_type=F32)
                    dv_ref[rows, :] += lax.dot_general(p_cat, do_all, TN_DIMS, preferred_element_type=F32)
                else:
                    dkt_ref[:, start:start + size] += jnp.dot(q_all_t, ds_cat, preferred_element_type=F32)
                    dvt_ref[:, start:start + size] += jnp.dot(do_all_t, p_cat, preferred_element_type=F32)
            dqs = [dq * SCORE_SCALE for dq in dqs]
            for j in range(N_KV):
                if has_sink:
                    contrib = -(jnp.exp(_sink_column(sink_ref, j) - lses[j]) * deltas[j])
                    r = lax.broadcasted_iota(jnp.int32, (QROWS, 1), 0)
                    row8 = lax.broadcasted_iota(jnp.int32, (8, LANES), 0)
                    for h in range(N_GROUP):
                        in_head = (r >= h * Q_BLOCK) & (r < (h + 1) * Q_BLOCK)
                        tot = jnp.sum(jnp.where(in_head, contrib, 0.0), axis=0, keepdims=True)
                        upd = upd + jnp.where(row8 == j * N_GROUP + h, tot, 0.0)
            _unstack_heads(dqs, dq_ref)
            if has_sink:
                ds_ref[...] += upd

        @pl.when(i < n_ctx // Q_BLOCK)
        def _():
            run(False)

        @pl.when(i >= n_ctx // Q_BLOCK)
        def _():
            run(True)

        if not window:
            @pl.when(i == nq - 1)
            def _():
                dk_ref[...] += dkt_ref[...].T
                dv_ref[...] += dvt_ref[...].T

        finish_ride()

    ins, specs = [q, k, v, do, o32, lse], [q_in, kv_in, kv_in, q_out, q_out, lse_spec]
    out_specs = [q_out, kv_out, kv_out]
    out_shape = [_sds((T, Q_WIDTH), F32), _sds((T, KV_WIDTH), F32), _sds((T, KV_WIDTH), F32)]
    if has_sink:
        ins, specs = [sink] + ins, [pl.BlockSpec(memory_space=pltpu.SMEM)] + specs
        out_specs.append(sink_spec)
        out_shape.append(_sds((B, 8, LANES), F32))
    scratch = [pltpu.VMEM((KV_WIDTH, LANES if window else Tp), F32)] * 2
    ins, specs, out_specs, out_shape, scratch = _hitch(rider, ins, specs, out_specs, out_shape, scratch)
    return pl.pallas_call(
        body, name=name, grid=(B, nq), in_specs=specs, out_specs=out_specs, out_shape=out_shape, scratch_shapes=scratch,
        compiler_params=_params(("arbitrary", "arbitrary") if rider is not None else ("parallel", "arbitrary")),
    )(*ins)


def _window_sums(xp):
    n = xp.shape[0]

    def ahead(a, k):
        return pltpu.roll(a, n - k, 0)
    a2 = xp + ahead(xp, 1)
    a4 = a2 + ahead(a2, 2)
    a8 = a4 + ahead(a4, 4)
    a16 = a8 + ahead(a8, 8)
    return (a2, a4, a8, a16)


def _by_group(vals):
    lane = lax.broadcasted_iota(jnp.int32, vals[0].shape, 1)
    return jnp.where(lane < POOL_CH, vals[0], jnp.where(lane < 2 * POOL_CH, vals[1],
                     jnp.where(lane < 3 * POOL_CH, vals[2], vals[3])))


def _pool_counts(n):
    t = lax.broadcasted_iota(jnp.int32, (n, POOL_WIDTH), 0)
    cnts = [(jnp.minimum(t + w // 2, n) - jnp.maximum(t - w // 2, 0)).astype(F32) for w in POOL_WINDOWS]
    return _by_group(cnts)


def _pad_rows(x):
    zeros = jnp.zeros((POOL_PAD, x.shape[1]), x.dtype)
    return jnp.concatenate([zeros, x, zeros], axis=0)


def _pool_stream(u):
    n = u.shape[0]
    sums = _window_sums(_pad_rows(u))
    tots = [pltpu.roll(a, w // 2, 0)[POOL_PAD:POOL_PAD + n] for a, w in zip(sums, POOL_WINDOWS)]
    return _by_group(tots) / _pool_counts(n) - u


def _pool_stream_t(dp):
    n = dp.shape[0]
    sums = _window_sums(_pad_rows(dp / _pool_counts(n)))
    tots = [pltpu.roll(a, w // 2 - 1, 0)[POOL_PAD:POOL_PAD + n] if w > 2 else a[POOL_PAD:POOL_PAD + n]
            for a, w in zip(sums, POOL_WINDOWS)]
    return _by_group(tots) - dp


def _pool_fwd(z, w_bd, scale, *, B, Tp, n_ctx, name):
    T = z.shape[0]
    blk = pl.BlockSpec((Tp, POOL_WIDTH), lambda b: (b, U_COL // POOL_WIDTH))
    out = pl.BlockSpec((Tp, POOL_WIDTH), lambda b: (b, 0))

    def body(u_ref, w_ref, s_ref, p_ref, o_ref):
        for lo, hi in ((0, n_ctx), (n_ctx, Tp)):
            pooled = _pool_stream(u_ref[lo:hi, :]).astype(BF16)
            p_ref[lo:hi, :] = pooled
            mixed = jnp.dot(pooled, w_ref[...], preferred_element_type=F32)
            o_ref[lo:hi, :] = (mixed * s_ref[...]).astype(BF16)

    return pl.pallas_call(
        body, name=name, grid=(B,),
        in_specs=[blk, pl.BlockSpec((POOL_WIDTH, POOL_WIDTH), lambda b: (0, 0)), pl.BlockSpec((1, POOL_WIDTH), lambda b: (0, 0))],
        out_specs=[out, out], out_shape=[_sds((T, POOL_WIDTH), BF16)] * 2, compiler_params=_params(("parallel",)),
    )(z, w_bd, scale)


def _pool_bwd(d_ob, pooled, w_bd, scale, *, B, Tp, n_ctx, name):
    T = d_ob.shape[0]
    blk = pl.BlockSpec((Tp, POOL_WIDTH), lambda b: (b, 0))
    wsp = pl.BlockSpec((POOL_WIDTH, POOL_WIDTH), lambda b: (0, 0))
    ssp = pl.BlockSpec((1, POOL_WIDTH), lambda b: (0, 0))

    def body(d_ref, p_ref, w_ref, s_ref, du_ref, dw_ref, dsc_ref):
        @pl.when(pl.program_id(0) == 0)
        def _():
            dw_ref[...] = jnp.zeros_like(dw_ref)
            dsc_ref[...] = jnp.zeros_like(dsc_ref)

        dv, pv, wv = d_ref[...], p_ref[...], w_ref[...]
        mixed = jnp.dot(pv, wv, preferred_element_type=F32)
        dsc_ref[...] += jnp.sum(dv * mixed, axis=0, keepdims=True)
        dmixed = (dv * s_ref[...]).astype(BF16)
        dw_ref[...] += lax.dot_general(pv, dmixed, TN_DIMS, preferred_element_type=F32)
        dpooled = lax.dot_general(dmixed, wv, NT_DIMS, preferred_element_type=F32)
        for lo, hi in ((0, n_ctx), (n_ctx, Tp)):
            du_ref[lo:hi, :] = _pool_stream_t(dpooled[lo:hi, :]).astype(BF16)

    return pl.pallas_call(
        body, name=name, grid=(B,), in_specs=[blk, blk, wsp, ssp], out_specs=[blk, wsp, ssp],
        out_shape=[_sds((T, POOL_WIDTH), BF16), _sds((POOL_WIDTH, POOL_WIDTH), F32), _sds((1, POOL_WIDTH), F32)],
        compiler_params=_params(("arbitrary",)),
    )(d_ob, pooled, w_bd, scale)


def _merge_specs(z, D, TR, tc, wa, wb, wc):
    def act(width):
        return pl.BlockSpec((TR, width), lambda i, n: (i, 0))

    def gate(part):
        return pl.BlockSpec((TR, tc), lambda i, n: (i, (GATE_COL + part * D) // tc + n))
    w_specs = [w.spec(w.shape[0], tc, lambda i, n: (0, n)) for w in (wa, wb, wc)]
    return [act(Q_WIDTH), act(POOL_WIDTH), act(Q_WIDTH), gate(0), gate(1), gate(2)] + w_specs


def _merge_fwd(oa, ob, oc, z, wa, wb, wc, *, D, TR, name):
    T = oa.shape[0]
    tc = D // N_CHIPS

    def body(oa_ref, ob_ref, oc_ref, ga_ref, gb_ref, gc_ref, wa_ref, wb_ref, wc_ref, y_ref):
        acc = jax.nn.sigmoid(ga_ref[...]) * jnp.dot(oa_ref[...], wa_ref[...], preferred_element_type=F32)
        acc += jax.nn.sigmoid(gb_ref[...]) * jnp.dot(ob_ref[...], wb_ref[...], preferred_element_type=F32)
        acc += jax.nn.sigmoid(gc_ref[...]) * jnp.dot(oc_ref[...], wc_ref[...], preferred_element_type=F32)
        y_ref[...] = acc.astype(BF16)

    return pl.pallas_call(
        body, name=name, grid=(T // TR, D // tc), in_specs=_merge_specs(z, D, TR, tc, wa, wb, wc),
        out_specs=pl.BlockSpec((TR, tc), lambda i, n: (i, n)), out_shape=_sds((T, D), BF16),
        compiler_params=_params(("parallel", "parallel")),
    )(oa, ob, oc, z, z, z, wa.arr, wb.arr, wc.arr)


def _merge_bwd(dy, oa, ob, oc, z, wa, wb, wc, *, D, TR, name):
    T = oa.shape[0]
    tc = D // N_CHIPS
    out = pl.BlockSpec((TR, tc), lambda i, n: (i, n))

    def body(dy_ref, oa_ref, ob_ref, oc_ref, ga_ref, gb_ref, gc_ref, wa_ref, wb_ref, wc_ref,
             dpa_ref, dpb_ref, dpc_ref, dga_ref, dgb_ref, dgc_ref):
        dyv = dy_ref[...]
        for o_ref, g_ref, w_ref, dp_ref, dg_ref in ((oa_ref, ga_ref, wa_ref, dpa_ref, dga_ref),
                                                    (ob_ref, gb_ref, wb_ref, dpb_ref, dgb_ref),
                                                    (oc_ref, gc_ref, wc_ref, dpc_ref, dgc_ref)):
            s = jax.nn.sigmoid(g_ref[...])
            proj = jnp.dot(o_ref[...], w_ref[...], preferred_element_type=F32)
            dp_ref[...] = (dyv * s).astype(BF16)
            dg_ref[...] = (dyv * proj * (s * (1.0 - s))).astype(BF16)

    return pl.pallas_call(
        body, name=name, grid=(T // TR, D // tc), in_specs=[out] + _merge_specs(z, D, TR, tc, wa, wb, wc),
        out_specs=[out] * 6, out_shape=[_sds((T, D), BF16)] * 6, compiler_params=_params(("parallel", "parallel")),
    )(dy, oa, ob, oc, z, z, z, wa.arr, wb.arr, wc.arr)


def _silu_rows(cc, name):
    def body(c_ref, s_ref):
        v = c_ref[...]
        s_ref[...] = (v * jax.nn.sigmoid(v)).astype(BF16)
    return pl.pallas_call(body, name=name, out_shape=_sds(cc.shape, BF16))(cc)


def _ada_bwd_rows(dm, ds, cc, name):
    def body(dm_ref, ds_ref, c_ref, db_ref, dc_ref):
        db_ref[...] = jnp.sum(dm_ref[...], axis=0, keepdims=True)
        v = c_ref[...]
        s = jax.nn.sigmoid(v)
        dc_ref[...] = ds_ref[...] * (s * (1.0 + v * (1.0 - s)))
    return pl.pallas_call(body, name=name, out_shape=[_sds((1, dm.shape[1]), F32), _sds(cc.shape, F32)])(dm, ds, cc)


def _row_tile(rows, cols):
    for t in (512, 256, 128, 64, 32, 16, 8):
        if rows % t == 0 and t * cols * 4 <= (1 << 20):
            return t
    return rows


def _working_rows(tr, C, worker):
    return pl.BlockSpec((tr, C), lambda i, c: (jnp.where(c[0] == worker, i, 0), 0))


def _add_landed(own, landed, core, worker, name):
    R, C = own.shape
    tr = _row_tile(R, C)
    row = _working_rows(tr, C, worker)

    def body(c_ref, a_ref, b_ref, o_ref, o16_ref):
        @pl.when(c_ref[0] == worker)
        def _():
            tot = a_ref[...] + b_ref[...].astype(F32)
            o_ref[...] = tot
            o16_ref[...] = tot.astype(BF16)

    grid_spec = pltpu.PrefetchScalarGridSpec(num_scalar_prefetch=1, grid=(R // tr,), in_specs=[row, row], out_specs=[row, row])
    return pl.pallas_call(body, name=name, grid_spec=grid_spec, out_shape=[_sds((R, C), F32), _sds((R, C), BF16)],
                          compiler_params=_params(("arbitrary",)))(core, own, landed)


def _sum_chips(own, landed, chip, core, worker, name):
    _, R, C = own.shape
    tr = _row_tile(R, C)

    def row(i, c):
        return jnp.where(c[0] == worker, i, 0)

    def body(k_ref, c_ref, a_ref, b_ref, o_ref):
        @pl.when(c_ref[0] == worker)
        def _():
            o_ref[...] = ((a_ref[...] + b_ref[0].astype(F32)) + b_ref[1].astype(F32)) + b_ref[2].astype(F32)

    grid_spec = pltpu.PrefetchScalarGridSpec(
        num_scalar_prefetch=2, grid=(R // tr,),
        in_specs=[pl.BlockSpec((None, tr, C), lambda i, k, c: (k[0], row(i, c), 0)),
                  pl.BlockSpec((3, tr, C), lambda i, k, c: (0, row(i, c), 0))],
        out_specs=pl.BlockSpec((tr, C), lambda i, k, c: (row(i, c), 0)))
    return pl.pallas_call(body, name=name, grid_spec=grid_spec, out_shape=_sds((R, C), F32),
                          compiler_params=_params(("arbitrary",)))(chip, core, own, landed)


def _adam_math(w, g, m, v):
    m = ADAM_B1 * m + (1.0 - ADAM_B1) * g
    v = ADAM_B2 * v + (1.0 - ADAM_B2) * (g * g)
    m_hat = m / (1.0 - ADAM_B1 ** ADAM_STEP)
    v_hat = v / (1.0 - ADAM_B2 ** ADAM_STEP)
    delta = -ADAM_LR * (m_hat / (jnp.sqrt(v_hat) + ADAM_EPS) + ADAM_WD * w)
    return delta, m, v


def _adamw(w, reduced, shared, m, v, core, worker, name):
    L, R, C = w.shape
    tr = _row_tile(R, C)

    def body(c_ref, w_ref, r0_ref, r1_ref, s0_ref, s1_ref, m_ref, v_ref, g_ref, d_ref, mo_ref, vo_ref):
        def step(g):
            d, mn, vn = _adam_math(w_ref[...], g, m_ref[...], v_ref[...])
            g_ref[...] = g
            d_ref[...] = d
            mo_ref[...] = mn
            vo_ref[...] = vn

        layer, here = pl.program_id(0), c_ref[0] == worker
        for l, (r_ref, s_ref) in enumerate(((r0_ref, s0_ref), (r1_ref, s1_ref))):
            @pl.when((layer == l) & here)
            def _(r_ref=r_ref):
                step(r_ref[...])

            @pl.when((layer == l) & jnp.logical_not(here))
            def _(s_ref=s_ref):
                step(s_ref[...])

    lay = pl.BlockSpec((None, tr, C), lambda l, i, c: (l, i, 0))
    row = pl.BlockSpec((tr, C), lambda l, i, c: (i, 0))
    grid_spec = pltpu.PrefetchScalarGridSpec(num_scalar_prefetch=1, grid=(L, R // tr),
                                             in_specs=[lay, row, row, row, row, lay, lay], out_specs=[lay] * 4)
    return pl.pallas_call(body, name=name, grid_spec=grid_spec, out_shape=[_sds((L, R, C), F32)] * 4,
                          compiler_params=_params(("parallel", "parallel")))(core, w, *reduced, *shared, m, v)


def _adamw_small(w, parts, m, v, name):
    R, C = w.shape

    def body(w_ref, p_ref, m_ref, v_ref, g_ref, d_ref, mo_ref, vo_ref):
        g = p_ref[0]
        for dev in range(1, 8):
            g = g + p_ref[dev]
        d, mn, vn = _adam_math(w_ref[...], g, m_ref[...], v_ref[...])
        g_ref[...] = g
        d_ref[...] = d
        mo_ref[...] = mn
        vo_ref[...] = vn

    return pl.pallas_call(body, name=name, out_shape=[_sds((R, C), F32)] * 4)(w, parts, m, v)


def _place():
    return lax.axis_index("x"), lax.axis_index("y"), lax.axis_index("c")


def _other_chips(x, y):
    return [(1 - x, y), (x, 1 - y), (1 - x, 1 - y)]


def _rcopy(src, dst, ssem, rsem, dev):
    return pltpu.make_async_remote_copy(src_ref=src, dst_ref=dst, send_sem=ssem, recv_sem=rsem,
                                        device_id=dev, device_id_type=MESH)


GATHER_SEMS = 7


class _LayerGather:
    def __init__(self, shards, layer):
        self.inputs, self.layer, self.n = list(shards), layer, len(shards)
        load, self.groups = [0, 0], ([], [])
        for w in sorted(range(self.n), key=lambda w: -shards[w][0].size):
            g = 0 if load[0] <= load[1] else 1
            self.groups[g].append(w)
            load[g] += shards[w][0].size
        self.out_shape = [_sds((N_CHIPS,) + s.shape[1:], s.dtype) for s in shards]
        self.scratch = [pltpu.SemaphoreType.DMA((self.n, GATHER_SEMS)), pltpu.SemaphoreType.DMA((self.n, GATHER_SEMS))]

    def _own(self, src, out, send_sems, recv_sems):
        x, y, c = _place()
        return [_rcopy(src[w].at[self.layer], out[w].at[2 * x + y], send_sems.at[w, 6], recv_sems.at[w, 6], (x, y, 1 - c))
                for w in range(self.n)]

    def _to_chips(self, g, src, out, send_sems, recv_sems):
        x, y, c = _place()
        return [_rcopy(src[w].at[self.layer], out[w].at[2 * x + y], send_sems.at[w, j], recv_sems.at[w, j], (*chip, c))
                for w in self.groups[g] for j, chip in enumerate(_other_chips(x, y))]

    def start(self, src, out, send_sems, recv_sems):
        c = lax.axis_index("c")
        for cp in self._own(src, out, send_sems, recv_sems):
            cp.start()
        for g in (0, 1):
            @pl.when(c == g)
            def _(g=g):
                for cp in self._to_chips(g, src, out, send_sems, recv_sems):
                    cp.start()

    def finish(self, src, out, send_sems, recv_sems):
        x, y, c = _place()
        sibling = (x, y, 1 - c)
        chips = _other_chips(x, y)
        for g in (0, 1):
            @pl.when(c == g)
            def _(g=g):
                passed = []
                for w in self.groups[g]:
                    for j, (px, py) in enumerate(chips):
                        landed = out[w].at[2 * px + py]
                        _rcopy(landed, landed, send_sems.at[w, j], recv_sems.at[w, j], (px, py, c)).wait_recv()
                        cp = _rcopy(landed, landed, send_sems.at[w, 3 + j], recv_sems.at[w, 3 + j], sibling)
                        cp.start()
                        passed.append(cp)
                for w in self.groups[1 - g]:
                    for j, (px, py) in enumerate(chips):
                        landed = out[w].at[2 * px + py]
                        _rcopy(landed, landed, send_sems.at[w, 3 + j], recv_sems.at[w, 3 + j], sibling).wait_recv()
                for cp in self._to_chips(g, src, out, send_sems, recv_sems) + passed:
                    cp.wait_send()
        for cp in self._own(src, out, send_sems, recv_sems):
            cp.wait_recv()
            cp.wait_send()


def _on_core(fn):
    for g in (0, 1):
        @pl.when(lax.axis_index("c") == g)
        def _(g=g):
            fn(g)


class _ToSibling:
    def __init__(self, arrays, senders):
        self.inputs, self.senders = list(arrays), list(senders)
        n = len(self.inputs)
        self.out_shape = [_sds(a.shape, a.dtype) for a in self.inputs]
        self.scratch = [pltpu.SemaphoreType.DMA((n,)), pltpu.SemaphoreType.DMA((n,))]

    def _copies(self, sender, src, out, send_sems, recv_sems):
        x, y, c = _place()
        return [_rcopy(src[w], out[w], send_sems.at[w], recv_sems.at[w], (x, y, 1 - c))
                for w in range(len(src)) if self.senders[w] == sender]

    def start(self, *refs):
        def go(g):
            for cp in self._copies(g, *refs):
                cp.start()
        _on_core(go)

    def finish(self, *refs):
        def go(g):
            for cp in self._copies(1 - g, *refs):
                cp.wait_recv()
            for cp in self._copies(g, *refs):
                cp.wait_send()
        _on_core(go)


class _ChipSend:
    def __init__(self, blocked, senders):
        self.inputs, self.senders = list(blocked), list(senders)
        n = len(self.inputs)
        self.out_shape = [_sds((3,) + a.shape[1:], a.dtype) for a in self.inputs]
        self.scratch = [pltpu.SemaphoreType.DMA((n, 3)), pltpu.SemaphoreType.DMA((n, 3))]

    def _copies(self, sender, src, out, send_sems, recv_sems):
        x, y, c = _place()
        return [_rcopy(src[w].at[2 * px + py], out[w].at[j], send_sems.at[w, j], recv_sems.at[w, j], (px, py, c))
                for w in range(len(src)) if self.senders[w] == sender for j, (px, py) in enumerate(_other_chips(x, y))]

    def start(self, *refs):
        def go(g):
            for cp in self._copies(g, *refs):
                cp.start()
        _on_core(go)

    def finish(self, *refs):
        def go(g):
            cps = self._copies(g, *refs)
            for cp in cps:
                cp.wait_recv()
            for cp in cps:
                cp.wait_send()
        _on_core(go)


def _ride_alone(rider, name):
    n_in, n_out = len(rider.inputs), len(rider.out_shape)

    def body(*refs):
        args = (refs[:n_in], refs[n_in:n_in + n_out]) + tuple(refs[n_in + n_out:])
        rider.start(*args)
        rider.finish(*args)

    return pl.pallas_call(body, name=name, in_specs=[ANY] * n_in, out_specs=[ANY] * n_out, out_shape=rider.out_shape,
                          scratch_shapes=rider.scratch)(*rider.inputs)


def _hitch(rider, ins, in_specs, out_specs, out_shape, scratch):
    if rider is None:
        return ins, in_specs, out_specs, out_shape, scratch
    return (list(ins) + rider.inputs, list(in_specs) + [ANY] * len(rider.inputs),
            list(out_specs) + [ANY] * len(rider.out_shape), list(out_shape) + rider.out_shape, list(scratch) + rider.scratch)


def _ride(rider, refs, n_in, n_out, n_scratch, grid):
    if rider is None:
        return list(refs), lambda: None
    r_in, r_out = len(rider.inputs), len(rider.out_shape)
    refs = list(refs)
    own_in, ride_in = refs[:n_in], refs[n_in:n_in + r_in]
    rest = refs[n_in + r_in:]
    own_out, ride_out = rest[:n_out], rest[n_out:n_out + r_out]
    rest = rest[n_out + r_out:]
    own_scratch, sems = rest[:n_scratch], rest[n_scratch:]
    ids = [pl.program_id(a) for a in range(len(grid))]
    first = functools.reduce(jnp.logical_and, [i == 0 for i in ids])
    last = functools.reduce(jnp.logical_and, [i == g - 1 for i, g in zip(ids, grid)])

    @pl.when(first)
    def _():
        rider.start(ride_in, ride_out, *sems)

    def finish():
        @pl.when(last)
        def _():
            rider.finish(ride_in, ride_out, *sems)

    return own_in + own_out + own_scratch, finish


def _gather_small(block, name):
    m_per, n = block.shape

    def body(x_ref, out_ref, send_sems, recv_sems, local_sem):
        x, y, c = _place()
        me, sibling = (x, y, c), (x, y, 1 - c)
        chips = _other_chips(x, y)

        def rows(px, py, pc):
            return out_ref.at[pl.ds((4 * px + 2 * py + pc) * m_per, m_per), :]

        def copy(k, blk, to, src=None):
            return _rcopy(rows(*blk) if src is None else src, rows(*blk), send_sems.at[k], recv_sems.at[k], to)

        mine = pltpu.make_async_copy(x_ref, rows(*me), local_sem)
        mine.start()
        first = [copy(0, me, sibling, src=x_ref)]
        first += [copy(1 + j, me, (*chip, c), src=x_ref) for j, chip in enumerate(chips)]
        for cp in first:
            cp.start()
        passed = [copy(4 + j, (*chip, c), sibling) for j, chip in enumerate(chips)]
        for j, chip in enumerate(chips):
            copy(1 + j, (*chip, c), me).wait_recv()
            passed[j].start()
        copy(0, sibling, me).wait_recv()
        for j, chip in enumerate(chips):
            copy(4 + j, (*chip, 1 - c), me).wait_recv()
        for cp in first + passed:
            cp.wait_send()
        mine.wait()

    return pl.pallas_call(
        body, name=name, out_shape=_sds((8 * m_per, n), block.dtype),
        in_specs=[pl.BlockSpec(memory_space=pltpu.VMEM)], out_specs=pl.BlockSpec(memory_space=pltpu.VMEM),
        scratch_shapes=[pltpu.SemaphoreType.DMA((7,)), pltpu.SemaphoreType.DMA((7,)), pltpu.SemaphoreType.DMA],
    )(block)


def _rope_tables(n_ctx, seq):
    rows = seq // GRID_W
    r = jnp.repeat(jnp.arange(rows, dtype=F32), GRID_W)
    col = jnp.tile(jnp.arange(GRID_W, dtype=F32), rows)
    inv = 1.0 / (ROPE_THETA ** (jnp.arange(0, AXIS_DIM, 2, dtype=F32) / AXIS_DIM))
    ang = jnp.concatenate([r[:, None] * inv, col[:, None] * inv], axis=-1)
    cos = jnp.repeat(jnp.cos(ang), 2, axis=-1)
    sin = jnp.repeat(jnp.sin(ang), 2, axis=-1) * jnp.tile(jnp.array([-1.0, 1.0], F32), HEAD_DIM // 2)
    cos = jnp.concatenate([jnp.ones((n_ctx, HEAD_DIM), F32), cos], axis=0)
    sin = jnp.concatenate([jnp.zeros((n_ctx, HEAD_DIM), F32), sin], axis=0)
    return jnp.tile(cos, (1, 2)), jnp.tile(sin, (1, 2))


def _block_diag(w_pool):
    L, G = w_pool.shape[:2]
    eye = jnp.eye(G, dtype=w_pool.dtype)
    return (w_pool[:, :, :, None, :] * eye[None, :, None, :, None]).reshape(L, POOL_WIDTH, POOL_WIDTH)


def _qk_gains(small):
    qn = jnp.stack([small["q_norm_a"], small["q_norm_c"]], axis=1)[:, :, None, :]
    kn = jnp.stack([small["k_norm_a"], small["k_norm_c"]], axis=1)[:, :, None, :]
    L = qn.shape[0]
    rows = jnp.concatenate([jnp.broadcast_to(qn, (L, 2, N_HEADS, HEAD_DIM)), jnp.broadcast_to(kn, (L, 2, N_KV, HEAD_DIM)),
                            jnp.ones((L, 2, N_KV, HEAD_DIM), F32)], axis=2)
    return rows.reshape(L, 2, 1, QKV_WIDTH)


def _local_step(x, c, ctx, c_ctx, small, gw, target, rider=None, overlap=False):
    gw = list(gw)
    B, S, D = x.shape
    N = ctx.shape[1]
    L = small["norm1"].shape[0]
    Tp = N + S
    T = B * Tp
    TR = N
    P = Tp // N
    rows16 = 16
    assert N % Q_BLOCK == 0 and S % N == 0 and B + 1 <= rows16
    TM = _tile(T, (1024, 768, 512, 384, 256, 128))
    TMG = _tile(T, (512, 384, 256, 128))

    X = jnp.concatenate([ctx, x], axis=1).reshape(T, D)
    cc = jnp.concatenate([c, c_ctx[None], jnp.zeros((rows16 - B - 1, D), F32)], axis=0)
    s_rows = _silu_rows(cc, "silu_rows")
    cos, sin = _rope_tables(N, S)
    all_gains = _qk_gains(small)
    all_w_bd = _block_diag(small["w_pool"]).astype(BF16)

    def weights(l):
        g = gw[l]
        return dict(
            ada=_Opnd(g["w_ada"], "bcols"), w_in=_Opnd(g["w_in"], "bcols"),
            a=_Opnd(g["w_br_a"], "bcols"), b=_Opnd(g["w_br_b"], "bcols"), c=_Opnd(g["w_br_c"], "bcols"),
            out=_Opnd(g["w_out"], "brows"), mlp1=_Opnd(g["w_mlp1"], "bcols"), mlp2=_Opnd(g["w_mlp2"], "brows"))

    IN = weights(0)["w_in"].shape[1]
    DFF = weights(0)["mlp1"].shape[1]
    tn_in = _tile(IN // N_CHIPS, (1152, 768, 512, 384, 256, 128))
    tn_ff = _tile(DFF // N_CHIPS, (1024, 512, 256, 128))
    tn_ada = _tile(6 * D // N_CHIPS, (1536, 768, 512, 256, 128))
    tn_d = D // N_CHIPS
    tk_d = _tile(D, (512,))
    tk_tok = _tile(T, (2304, 1536, 1024, 768, 512, 384, 256))

    saved = []
    xin, pending = X, None
    for l in range(L):
        W = weights(l)
        b_ada = small["b_ada"][l].reshape(1, 6 * D)
        mod = _matmul(s_rows, W["ada"], "nn", tm=rows16, tn=tn_ada, tk=D, name=f"ada_fwd{l}",
                      epilogue=lambda acc, b: (acc + b,), extras=[(b_ada, (1, tn_ada), lambda m, n: (0, n))])
        modtab = mod.reshape(rows16, 1, 6 * D)
        gains = all_gains[l]
        w_bd = all_w_bd[l]
        p_scale = small["pool_scale"][l].reshape(1, POOL_WIDTH)
        sink = small["sink_c"][l]

        x0, h1 = _res_norm(xin, pending, modtab, 0, 1, small["norm1"][l][None], TR=TR, P=P, name=f"norm1_fwd{l}")
        z = _matmul(h1, W["w_in"], "nn", tm=TM, tn=tn_in, tk=D, name=f"in_proj{l}")
        q2, k2, v2 = _qk_prep(z, gains, cos, sin, TR=TR, P=P, name=f"qk_prep{l}")
        riding = rider if l == 0 else None
        oa, oa32, lse_a, *landed = _attn_fwd(q2, k2, v2, None, branch=0, B=B, n_ctx=N, window=False,
                                             name=f"attn_a_fwd{l}", rider=riding)
        if riding is not None:
            gw[riding.layer] = dict(zip(BIG_NAMES, landed))
        oc, oc32, lse_c = _attn_fwd(q2, k2, v2, sink, branch=1, B=B, n_ctx=N, window=True, name=f"attn_c_fwd{l}")
        pooled, ob = _pool_fwd(z, w_bd, p_scale, B=B, Tp=Tp, n_ctx=N, name=f"pool_fwd{l}")
        y = _merge_fwd(oa, ob, oc, z, W["a"], W["b"], W["c"], D=D, TR=TMG, name=f"merge_fwd{l}")
        ao = _matmul(y, W["out"], "nn", tm=TM, tn=D, tk=tn_d, name=f"out_proj{l}")
        x1, h2 = _res_norm(x0, (ao, modtab, 2), modtab, 3, 4, small["norm2"][l][None], TR=TR, P=P, name=f"norm2_fwd{l}")
        a_pre, r_act = _matmul(h2, W["mlp1"], "nn", tm=TM, tn=tn_ff, tk=D, name=f"mlp1_fwd{l}", out_dtypes=(F32, BF16),
                               epilogue=lambda acc: (acc, jnp.square(jnp.maximum(acc, 0.0))))
        mo = _matmul(r_act, W["mlp2"], "nn", tm=TM, tn=D, tk=tn_ff, name=f"mlp2_fwd{l}")
        saved.append(dict(modtab=modtab, gains=gains, w_bd=w_bd, p_scale=p_scale, sink=sink, x0=x0, h1=h1, z=z,
                          q2=q2, k2=k2, v2=v2, oa=oa, ob=ob, oc=oc, oa32=oa32, oc32=oc32, lse_a=lse_a, lse_c=lse_c,
                          pooled=pooled, y=y, ao=ao,
                          x1=x1, h2=h2, a_pre=a_pre, r_act=r_act, mo=mo))
        xin, pending = x1, (mo, modtab, 5)

    dxo, loss, d_mo, dg2 = _loss_head(xin, pending[0], pending[1], 5, target.reshape(B * S, D), TR=TR, P=P, name="loss_head")

    big = {k: [None] * L for k in BIG_NAMES}
    big16 = {k: [None] * L for k in BIG_NAMES}
    sm = {k: [None] * L for k in ("b_ada", "norm1", "norm2", "q_norm_a", "k_norm_a", "q_norm_c", "k_norm_c",
                                   "sink_c", "w_pool", "pool_scale")}

    def dw(key, l, a, b, *, tm, tn, name, tk=tk_tok, blocked=True):
        outs = _matmul(a, b, "tn", tm=tm, tn=tn, tk=tk, name=name, out_dtypes=(F32, BF16), out_blocked=blocked)
        if not blocked:
            outs = [o.reshape(N_CHIPS, o.shape[0] // N_CHIPS, o.shape[1]) for o in outs]
        big[key][l], big16[key][l] = outs
    d_cctx = jnp.zeros((D,), F32)
    for l in reversed(range(L)):
        W, sv = weights(l), saved[l]
        modtab = sv["modtab"]
        ride_now = overlap and l == L - 2
        if ride_now:
            early = _LayerReduce(l + 1, [big[k][l + 1] for k in BIG_NAMES], [big16[k][l + 1] for k in BIG_NAMES])
        d_a = _matmul(d_mo, W["mlp2"], "nt", tm=TM, tn=tn_ff, tk=D, name=f"mlp2_bwd{l}", out_dtypes=(BF16,),
                      epilogue=lambda acc, a: (acc * (2.0 * jnp.maximum(a, 0.0)),),
                      extras=[(sv["a_pre"], (TM, tn_ff), lambda m, n: (m, n))], rider=early.to_worker if ride_now else None)
        if ride_now:
            d_a, landed = d_a
            early_send = early.add(landed)
        dw("w_mlp2", l, sv["r_act"], d_mo, tm=tk_d, tn=D, name=f"mlp2_dw{l}", blocked=False)
        d_h2 = _matmul(d_a, W["mlp1"], "nt", tm=TM, tn=D, tk=tn_ff, name=f"mlp1_bwd{l}")
        dw("w_mlp1", l, sv["h2"], d_a, tm=tk_d, tn=tn_ff, name=f"mlp1_dw{l}")
        dx1, dsh2, dsc2, dn2, d_ao, dg1 = _norm_bwd(sv["x1"], d_h2, dxo, modtab, 4, small["norm2"][l][None],
                                                    (sv["ao"], modtab, 2), TR=TR, P=P, name=f"norm2_bwd{l}")
        d_y = _matmul(d_ao, W["out"], "nt", tm=TM, tn=tn_d, tk=D, name=f"out_bwd{l}")
        dw("w_out", l, sv["y"], d_ao, tm=tk_d, tn=D, name=f"out_dw{l}", blocked=False)
        d_pa, d_pb, d_pc, d_ga, d_gb, d_gc = _merge_bwd(d_y, sv["oa"], sv["ob"], sv["oc"], sv["z"], W["a"], W["b"], W["c"],
                                                        D=D, TR=TMG, name=f"merge_bwd{l}")
        d_oa = _matmul(d_pa, W["a"], "nt", tm=TM, tn=Q_WIDTH, tk=tn_d, name=f"br_a_bwd{l}", out_dtypes=(BF16,))
        d_ob = _matmul(d_pb, W["b"], "nt", tm=TM, tn=POOL_WIDTH, tk=tn_d, name=f"br_b_bwd{l}")
        d_oc = _matmul(d_pc, W["c"], "nt", tm=TM, tn=Q_WIDTH, tk=tn_d, name=f"br_c_bwd{l}", out_dtypes=(BF16,))
        dw("w_br_a", l, sv["oa"], d_pa, tm=Q_WIDTH, tn=tn_d, name=f"br_a_dw{l}")
        dw("w_br_b", l, sv["ob"], d_pb, tm=POOL_WIDTH, tn=tn_d, name=f"br_b_dw{l}")
        dw("w_br_c", l, sv["oc"], d_pc, tm=Q_WIDTH, tn=tn_d, name=f"br_c_dw{l}")
        d_u, d_wbd, d_ps = _pool_bwd(d_ob, sv["pooled"], sv["w_bd"], sv["p_scale"], B=B, Tp=Tp, n_ctx=N, name=f"pool_bwd{l}")
        dqa, dka, dva, *arrived = _attn_bwd(sv["q2"], sv["k2"], sv["v2"], d_oa, sv["oa32"], sv["lse_a"], None, branch=0,
                                            B=B, n_ctx=N, window=False, name=f"attn_a_bwd{l}",
                                            rider=early_send if ride_now else None)
        if ride_now:
            early.from_chips = arrived
        dqc, dkc, dvc, dsink = _attn_bwd(sv["q2"], sv["k2"], sv["v2"], d_oc, sv["oc32"], sv["lse_c"], sv["sink"],
                                         branch=1, B=B, n_ctx=N, window=True, name=f"attn_c_bwd{l}")
        dz_a, dgains_a = _qk_prep_bwd(sv["z"], dqa, dka, dva, sv["gains"], cos, sin, branch=0, TR=TR, P=P,
                                      name=f"qk_prep_a_bwd{l}")
        dz_c, dgains_c = _qk_prep_bwd(sv["z"], dqc, dkc, dvc, sv["gains"], cos, sin, branch=1, TR=TR, P=P,
                                      name=f"qk_prep_c_bwd{l}")
        dz = jnp.concatenate([dz_a, dz_c, d_u, d_ga, d_gb, d_gc], axis=1)
        d_h1 = _matmul(dz, W["w_in"], "nt", tm=TM, tn=D, tk=tn_in, name=f"in_bwd{l}")
        dw("w_in", l, sv["h1"], dz, tm=tk_d, tn=tn_in, name=f"in_dw{l}")
        below = (saved[l - 1]["mo"], saved[l - 1]["modtab"], 5) if l > 0 else None
        dx0, dsh1, dsc1, dn1, *lower = _norm_bwd(sv["x0"], d_h1, dx1, modtab, 1, small["norm1"][l][None], below,
                                                 TR=TR, P=P, name=f"norm1_bwd{l}")
        this_dg2 = dg2
        if l > 0:
            d_mo, dg2 = lower

        dm_groups = jnp.concatenate([dsh1, dsc1, dg1, dsh2, dsc2, this_dg2], axis=-1).reshape(B, 2, 6 * D)
        dm = jnp.concatenate([dm_groups[:, 1], jnp.sum(dm_groups[:, 0], axis=0, keepdims=True),
                              jnp.zeros((rows16 - B - 1, 6 * D), F32)], axis=0)
        dm_bf = dm.astype(BF16)
        d_s = _matmul(dm_bf, W["ada"], "nt", tm=rows16, tn=D, tk=tn_ada, name=f"ada_bwd{l}")
        dw("w_ada", l, s_rows, dm_bf, tm=tk_d, tn=tn_ada, tk=rows16, name=f"ada_dw{l}")
        db_ada, dcc = _ada_bwd_rows(dm, d_s, cc, f"ada_rows_bwd{l}")
        d_cctx = d_cctx + dcc[B]

        sm["b_ada"][l] = db_ada[0]
        sm["norm1"][l] = jnp.sum(dn1, axis=(0, 1))
        sm["norm2"][l] = jnp.sum(dn2, axis=(0, 1))
        dgh = jnp.stack([dgains_a, dgains_c]).reshape(2, QKV_WIDTH // HEAD_DIM, HEAD_DIM)
        sm["q_norm_a"][l] = jnp.sum(dgh[0, :N_HEADS], axis=0)
        sm["k_norm_a"][l] = jnp.sum(dgh[0, N_HEADS:N_HEADS + N_KV], axis=0)
        sm["q_norm_c"][l] = jnp.sum(dgh[1, :N_HEADS], axis=0)
        sm["k_norm_c"][l] = jnp.sum(dgh[1, N_HEADS:N_HEADS + N_KV], axis=0)
        sm["sink_c"][l] = jnp.sum(dsink[:, :N_HEADS, 0], axis=0)
        sm["w_pool"][l] = jnp.stack([d_wbd[g * POOL_CH:(g + 1) * POOL_CH, g * POOL_CH:(g + 1) * POOL_CH]
                                     for g in range(POOL_WIDTH // POOL_CH)])
        sm["pool_scale"][l] = d_ps[0]
        dxo = dx0

    grad_x = dxo.reshape(B, Tp, D)[:, N:]
    small_grads = {k: jnp.stack(v) for k, v in sm.items()}
    small_grads["c_ctx"] = d_cctx
    return loss, grad_x, small_grads, big, big16, (early if overlap else None)


SMALL_NAMES = ("c_ctx", "b_ada", "norm1", "norm2", "q_norm_a", "k_norm_a", "q_norm_c", "k_norm_c", "sink_c",
               "w_pool", "pool_scale")
BIG_NAMES = ("w_ada", "w_in", "w_br_a", "w_br_b", "w_br_c", "w_out", "w_mlp1", "w_mlp2")
WEIGHT_NAMES = ("c_ctx", "w_ada", "b_ada", "norm1", "norm2", "w_in", "q_norm_a", "k_norm_a", "q_norm_c", "k_norm_c",
                "sink_c", "w_pool", "pool_scale", "w_br_a", "w_br_b", "w_br_c", "w_out", "w_mlp1", "w_mlp2")


def _pack(parts, rows):
    flat = jnp.concatenate([p.reshape(-1).astype(F32) for p in parts])
    return jnp.pad(flat, (0, rows * LANES - flat.shape[0])).reshape(rows, LANES)


def _unpack(packed, like):
    flat, out, at = packed.reshape(-1), [], 0
    for p in like:
        out.append(flat[at:at + p.size].reshape(p.shape))
        at += p.size
    return out


def _split_by_bytes(arrays):
    load, owner = [0, 0], [0] * len(arrays)
    for w in sorted(range(len(arrays)), key=lambda w: -arrays[w].size):
        owner[w] = 0 if load[0] <= load[1] else 1
        load[owner[w]] += arrays[w].size
    return owner


class _LayerReduce:
    def __init__(self, layer, partials, partials16):
        self.layer, self.partials = layer, list(partials)
        self.workers = _split_by_bytes(self.partials)
        self.to_worker = _ToSibling([g.reshape(-1, g.shape[-1]) for g in partials16], [1 - wk for wk in self.workers])
        x, y, c = _place()
        self.core = c.astype(jnp.int32).reshape(1)
        self.chip = (2 * x + y).astype(jnp.int32).reshape(1)

    def add(self, landed):
        sums = [_add_landed(g.reshape(-1, g.shape[-1]), r, self.core, wk, f"grads{self.layer}_add_sibling_{k}")
                for k, g, r, wk in zip(BIG_NAMES, self.partials, landed, self.workers)]
        self.in_chip = [h.reshape(g.shape) for g, (h, _) in zip(self.partials, sums)]
        return _ChipSend([h.reshape(g.shape) for g, (_, h) in zip(self.partials, sums)], self.workers)

    def sum(self, from_chips):
        return [_sum_chips(h, r, self.chip, self.core, wk, f"grads{self.layer}_sum_chips_{k}")
                for k, h, r, wk in zip(BIG_NAMES, self.in_chip, from_chips, self.workers)]


def kernel(x, c, ctx, c_ctx, w_ada, b_ada, norm1, norm2, w_in, q_norm_a, k_norm_a, q_norm_c, k_norm_c, sink_c, w_pool, pool_scale, w_br_a, w_br_b, w_br_c, w_out, w_mlp1, w_mlp2, loss_target, m_c_ctx, m_w_ada, m_b_ada, m_norm1, m_norm2, m_w_in, m_q_norm_a, m_k_norm_a, m_q_norm_c, m_k_norm_c, m_sink_c, m_w_pool, m_pool_scale, m_w_br_a, m_w_br_b, m_w_br_c, m_w_out, m_w_mlp1, m_w_mlp2, v_c_ctx, v_w_ada, v_b_ada, v_norm1, v_norm2, v_w_in, v_q_norm_a, v_k_norm_a, v_q_norm_c, v_k_norm_c, v_sink_c, v_w_pool, v_pool_scale, v_w_br_a, v_w_br_b, v_w_br_c, v_w_out, v_w_mlp1, v_w_mlp2):
    given = dict(locals())
    w = {k: given[k] for k in WEIGHT_NAMES}
    m = {k: given["m_" + k] for k in WEIGHT_NAMES}
    v = {k: given["v_" + k] for k in WEIGHT_NAMES}

    shards = [w[k].astype(BF16) for k in BIG_NAMES]
    assert all(s.shape[0] == 2 for s in shards)
    first_layer = dict(zip(BIG_NAMES, _ride_alone(_LayerGather(shards, 0), "gather_weights0")))
    small = {k: w[k] for k in SMALL_NAMES}
    loss_part, grad_x, small_grads, big_grads, big_grads16, early = _local_step(
        x, c, ctx, c_ctx, small, [first_layer, None], loss_target, rider=_LayerGather(shards, 1), overlap=True)

    late = _LayerReduce(0, [big_grads[k][0] for k in BIG_NAMES], [big_grads16[k][0] for k in BIG_NAMES])
    send = late.add(_ride_alone(late.to_worker, "grads0_to_sibling"))
    reduced = [late.sum(_ride_alone(send, "grads0_to_chips")), early.sum(early.from_chips)]
    n_big = len(BIG_NAMES)
    shared = _ride_alone(_ToSibling(reduced[0] + reduced[1], late.workers + early.workers), "grads_share")
    grads, deltas, new_m, new_v = {}, {}, {}, {}
    for i, k in enumerate(BIG_NAMES):
        assert late.workers[i] == early.workers[i]
        grads[k], deltas[k], new_m[k], new_v[k] = _adamw(
            w[k], (reduced[0][i], reduced[1][i]), (shared[i], shared[n_big + i]), m[k], v[k], late.core, late.workers[i],
            f"adamw_{k}")

    sizes = sum(w[k].size for k in SMALL_NAMES) + LANES
    rows = -(-sizes // (8 * LANES)) * 8
    parts = _gather_small(_pack([small_grads[k] for k in SMALL_NAMES] + [loss_part[0]], rows), "gather_small")
    zero = jnp.zeros((LANES,), F32)
    packed = [_pack([t[k] for k in SMALL_NAMES] + [zero], rows) for t in (w, m, v)]
    outs = _adamw_small(packed[0], parts.reshape(8, rows, LANES), packed[1], packed[2], "adamw_small")
    like = [w[k] for k in SMALL_NAMES] + [zero]
    for store, packed_out in zip((grads, deltas, new_m, new_v), outs):
        pieces = _unpack(packed_out, like)
        for k, piece in zip(SMALL_NAMES, pieces):
            store[k] = piece
        if store is grads:
            loss = pieces[-1][0]

    return (loss, grad_x, *[grads[k] for k in WEIGHT_NAMES], *[deltas[k] for k in WEIGHT_NAMES],
            *[new_m[k] for k in WEIGHT_NAMES], *[new_v[k] for k in WEIGHT_NAMES])
```

```python
import functools

import jax
import jax.numpy as jnp
from jax import lax
from jax.experimental import pallas as pl
from jax.experimental.pallas import tpu as pltpu

F32 = jnp.float32
BF16 = jnp.bfloat16

HEAD_DIM = 64
GRID_W = 64
AXIS_DIM = HEAD_DIM // 2
ROPE_THETA = 10000.0
N_HEADS = 6
N_KV = 2
N_GROUP = N_HEADS // N_KV
POOL_CH = 64
POOL_WIDTH = 256
POOL_WINDOWS = (2, 4, 8, 16)
WINDOW = 128
Q_BLOCK = 128
Q_WIDTH = N_HEADS * HEAD_DIM
KV_WIDTH = N_KV * HEAD_DIM
GATE_COL = 2 * (Q_WIDTH + 2 * KV_WIDTH) + POOL_WIDTH
U_COL = 2 * (Q_WIDTH + 2 * KV_WIDTH)
EPS = 1e-6
NEG = -1e30
ADAM_LR = 0.001
ADAM_B1 = 0.9
ADAM_B2 = 0.999
ADAM_EPS = 1e-08
ADAM_WD = 0.01
ADAM_STEP = 10

N_CHIPS = 4
LANES = 128
POOL_PAD = 16
VMEM_LIMIT = 48 * 1024 * 1024
MESH = pl.DeviceIdType.MESH
ANY = pl.BlockSpec(memory_space=pl.ANY)


def _params(sem):
    return pltpu.CompilerParams(dimension_semantics=sem, vmem_limit_bytes=VMEM_LIMIT)


def _sds(shape, dtype):
    return jax.ShapeDtypeStruct(tuple(shape), dtype)


class _Opnd:
    def __init__(self, arr, kind="plain"):
        self.arr, self.kind = arr, kind

    @property
    def shape(self):
        a = self.arr
        if self.kind == "plain":
            return a.shape
        if self.kind == "bcols":
            return (a.shape[1], N_CHIPS * a.shape[2])
        return (N_CHIPS * a.shape[1], a.shape[2])

    def spec(self, tr, tc, fn):
        a = self.arr
        if self.kind == "plain":
            return pl.BlockSpec((tr, tc), lambda *g: fn(*g))
        if self.kind == "bcols":
            assert a.shape[2] % tc == 0, (a.shape, tc)
            per = a.shape[2] // tc

            def im(*g):
                ri, ci = fn(*g)
                return (ci // per, ri, ci % per)
            return pl.BlockSpec((None, tr, tc), im)
        assert a.shape[1] % tr == 0, (a.shape, tr)
        per = a.shape[1] // tr

        def im(*g):
            ri, ci = fn(*g)
            return (ri // per, ri % per, ci)
        return pl.BlockSpec((None, tr, tc), im)


def _matmul(a, b, mode, *, tm, tn, tk, name, out_dtypes=(F32,), epilogue=None, extras=(), out_blocked=False, rider=None):
    if not isinstance(a, _Opnd):
        a = _Opnd(a)
    if not isinstance(b, _Opnd):
        b = _Opnd(b)
    if mode == "nn":
        (M, K), (K2, N) = a.shape, b.shape
        a_spec = a.spec(tm, tk, lambda m, n, k: (m, k))
        b_spec = b.spec(tk, tn, lambda m, n, k: (k, n))
        dims = (((1,), (0,)), ((), ()))
    elif mode == "nt":
        (M, K), (N, K2) = a.shape, b.shape
        a_spec = a.spec(tm, tk, lambda m, n, k: (m, k))
        b_spec = b.spec(tn, tk, lambda m, n, k: (n, k))
        dims = (((1,), (1,)), ((), ()))
    else:
        (K, M), (K2, N) = a.shape, b.shape
        a_spec = a.spec(tk, tm, lambda m, n, k: (k, m))
        b_spec = b.spec(tk, tn, lambda m, n, k: (k, n))
        dims = (((0,), (0,)), ((), ()))
    assert K == K2 and M % tm == 0 and N % tn == 0 and K % tk == 0, (name, M, N, K, K2, tm, tn, tk)
    nk = K // tk
    n_extra = len(extras)
    n_out = len(out_dtypes)
    extra_specs = [pl.BlockSpec(bs, functools.partial(lambda m, n, k, f: f(m, n), f=f)) for (_, bs, f) in extras]
    if out_blocked:
        assert (N // N_CHIPS) % tn == 0
        per = (N // N_CHIPS) // tn
        out_shape = [_sds((N_CHIPS, M, N // N_CHIPS), dt) for dt in out_dtypes]
        out_specs = [pl.BlockSpec((None, tm, tn), lambda m, n, k: (n // per, m, n % per)) for _ in out_dtypes]
    else:
        out_shape = [_sds((M, N), dt) for dt in out_dtypes]
        out_specs = [pl.BlockSpec((tm, tn), lambda m, n, k: (m, n)) for _ in out_dtypes]

    in_place = nk > 1 and epilogue is None and out_dtypes[0] == F32

    grid = (M // tm, N // tn, nk)
    own_scratch = [pltpu.VMEM((tm, tn), F32)] if nk > 1 and not in_place else []

    def body(*refs):
        refs, finish_ride = _ride(rider, refs, 2 + n_extra, n_out, len(own_scratch), grid)
        a_ref, b_ref = refs[0], refs[1]
        extra_refs = refs[2:2 + n_extra]
        out_refs = refs[2 + n_extra:2 + n_extra + n_out]
        acc_ref = out_refs[0] if in_place else (refs[2 + n_extra + n_out] if nk > 1 else None)
        k = pl.program_id(2)
        prod = lax.dot_general(a_ref[...].astype(BF16), b_ref[...].astype(BF16), dims, preferred_element_type=F32)

        def finish(acc):
            outs = epilogue(acc, *[r[...] for r in extra_refs]) if epilogue is not None else (acc,) * n_out
            for o_ref, o in zip(out_refs, outs):
                o_ref[...] = o.astype(o_ref.dtype)

        if nk == 1:
            finish(prod)
        elif in_place:
            @pl.when(k == 0)
            def _():
                acc_ref[...] = prod

            @pl.when(k > 0)
            def _():
                acc_ref[...] += prod

            if n_out > 1:
                @pl.when(k == nk - 1)
                def _():
                    for o_ref in out_refs[1:]:
                        o_ref[...] = acc_ref[...].astype(o_ref.dtype)
        else:
            @pl.when(k == 0)
            def _():
                acc_ref[...] = prod

            @pl.when(k > 0)
            def _():
                acc_ref[...] += prod

            @pl.when(k == nk - 1)
            def _():
                finish(acc_ref[...])

        finish_ride()

    ins, in_specs, out_specs, out_shape, scratch = _hitch(
        rider, [a.arr, b.arr] + [e[0] for e in extras], [a_spec, b_spec] + extra_specs, out_specs, out_shape, own_scratch)
    outs = pl.pallas_call(
        body, name=name, grid=grid, in_specs=in_specs, out_specs=out_specs, out_shape=out_shape, scratch_shapes=scratch,
        compiler_params=_params(("arbitrary",) * 3 if rider is not None else ("parallel", "parallel", "arbitrary")),
    )(*ins)
    if rider is not None:
        return (outs[0] if n_out == 1 else outs[:n_out]), outs[n_out:]
    return outs[0] if n_out == 1 else outs


def _tile(n, cands):
    for t in cands:
        if n % t == 0:
            return t
    return n


def _grp(i, P):
    return 2 * (i // P) + jnp.minimum(i % P, 1)


def _mod_spec(D, P, part, B):
    return pl.BlockSpec((1, 1, D), lambda i: (jnp.where(i % P == 0, B, i // P), 0, part))


def _res_norm(x, pending, modtab, shift_part, scale_part, gain, *, TR, P, name):
    T, D = x.shape
    row = pl.BlockSpec((TR, D), lambda i: (i, 0))
    has_branch = pending is not None
    ins, specs = [x], [row]
    if has_branch:
        branch, gate_tab, gate_part = pending
        ins += [branch, gate_tab]
        specs += [row, _mod_spec(D, P, gate_part, T // (TR * P))]
    ins += [modtab, modtab, gain]
    specs += [_mod_spec(D, P, shift_part, T // (TR * P)), _mod_spec(D, P, scale_part, T // (TR * P)), pl.BlockSpec((1, D), lambda i: (0, 0))]

    def body(*refs):
        if has_branch:
            x_ref, br_ref, g_ref, sh_ref, sc_ref, gn_ref, xo_ref, h_ref = refs
            xv = x_ref[...] + g_ref[0] * br_ref[...]
        else:
            x_ref, sh_ref, sc_ref, gn_ref, xo_ref, h_ref = refs
            xv = x_ref[...]
        xo_ref[...] = xv
        y = xv * lax.rsqrt(jnp.mean(xv * xv, axis=-1, keepdims=True) + EPS) * gn_ref[...]
        h_ref[...] = (y * (1.0 + sc_ref[0]) + sh_ref[0]).astype(BF16)

    return pl.pallas_call(
        body, name=name, grid=(T // TR,), in_specs=specs, out_specs=[row, row],
        out_shape=[_sds((T, D), F32), _sds((T, D), BF16)], compiler_params=_params(("parallel",)),
    )(*ins)


def _norm_bwd(x, dh, dres, modtab, scale_part, gain, below, *, TR, P, name):
    T, D = x.shape
    G = 2 * (T // (TR * P))
    row = pl.BlockSpec((TR, D), lambda i: (i, 0))
    acc = pl.BlockSpec((1, 1, D), lambda i: (_grp(i, P), 0, 0))
    has_below = below is not None

    def body(*refs):
        if has_below:
            x_ref, dh_ref, dres_ref, sc_ref, gn_ref, br_ref, g_ref, dx_ref, dsh_ref, dsc_ref, dgn_ref, db_ref, dg_ref = refs
        else:
            x_ref, dh_ref, dres_ref, sc_ref, gn_ref, dx_ref, dsh_ref, dsc_ref, dgn_ref = refs
        r = pl.program_id(0) % P
        xv, dhv, gn = x_ref[...], dh_ref[...], gn_ref[...]
        rstd = lax.rsqrt(jnp.mean(xv * xv, axis=-1, keepdims=True) + EPS)
        xhat = xv * rstd
        dn = dhv * (1.0 + sc_ref[0])
        dxhat = dn * gn
        dxv = dres_ref[...] + rstd * (dxhat - xhat * jnp.mean(dxhat * xhat, axis=-1, keepdims=True))
        dx_ref[...] = dxv
        parts = [jnp.sum(dhv, axis=0, keepdims=True), jnp.sum(dhv * (xhat * gn), axis=0, keepdims=True),
                 jnp.sum(dn * xhat, axis=0, keepdims=True)]
        outs = [dsh_ref, dsc_ref, dgn_ref]
        if has_below:
            db_ref[...] = (dxv * g_ref[0]).astype(BF16)
            parts.append(jnp.sum(dxv * br_ref[...], axis=0, keepdims=True))
            outs.append(dg_ref)

        @pl.when(r <= 1)
        def _():
            for o_ref, part in zip(outs, parts):
                o_ref[0] = part

        @pl.when(r > 1)
        def _():
            for o_ref, part in zip(outs, parts):
                o_ref[0] += part

    ins = [x, dh, dres, modtab, gain]
    in_specs = [row, row, row, _mod_spec(D, P, scale_part, T // (TR * P)), pl.BlockSpec((1, D), lambda i: (0, 0))]
    out_specs, out_shape = [row, acc, acc, acc], [_sds((T, D), F32)] + [_sds((G, 1, D), F32)] * 3
    if has_below:
        branch, gate_tab, gate_part = below
        ins += [branch, gate_tab]
        in_specs += [row, _mod_spec(D, P, gate_part, T // (TR * P))]
        out_specs += [row, acc]
        out_shape += [_sds((T, D), BF16), _sds((G, 1, D), F32)]
    return pl.pallas_call(body, name=name, grid=(T // TR,), in_specs=in_specs, out_specs=out_specs, out_shape=out_shape,
                          compiler_params=_params(("arbitrary",)))(*ins)


def _loss_head(x, branch, modtab, gate_part, target, *, TR, P, name):
    T, D = x.shape
    row = pl.BlockSpec((TR, D), lambda i: (i, 0))
    tgt = pl.BlockSpec((TR, D), lambda i: ((i // P) * (P - 1) + jnp.maximum(i % P - 1, 0), 0))
    one = pl.BlockSpec((1, LANES), lambda i: (0, 0))

    G = 2 * (T // (TR * P))
    acc = pl.BlockSpec((1, 1, D), lambda i: (_grp(i, P), 0, 0))

    def body(x_ref, br_ref, g_ref, t_ref, dy_ref, loss_ref, db_ref, dg_ref):
        i = pl.program_id(0)
        r = i % P

        @pl.when(i == 0)
        def _():
            loss_ref[...] = jnp.zeros_like(loss_ref)

        @pl.when(r == 0)
        def _():
            dy_ref[...] = jnp.zeros_like(dy_ref)
            db_ref[...] = jnp.zeros_like(db_ref)
            dg_ref[...] = jnp.zeros_like(dg_ref)

        @pl.when(r > 0)
        def _():
            brv, g = br_ref[...], g_ref[0]
            err = x_ref[...] + g * brv - t_ref[...]
            dy = err / D
            dy_ref[...] = dy
            db_ref[...] = (dy * g).astype(BF16)
            part = jnp.sum(dy * brv, axis=0, keepdims=True)
            per_tok = jnp.mean(err * err, axis=-1, keepdims=True)
            loss_ref[...] += 0.5 * jnp.sum(per_tok, axis=0, keepdims=True)

            @pl.when(r == 1)
            def _():
                dg_ref[0] = part

            @pl.when(r > 1)
            def _():
                dg_ref[0] += part

    return pl.pallas_call(
        body, name=name, grid=(T // TR,), in_specs=[row, row, _mod_spec(D, P, gate_part, T // (TR * P)), tgt],
        out_specs=[row, one, row, acc],
        out_shape=[_sds((T, D), F32), _sds((1, LANES), F32), _sds((T, D), BF16), _sds((G, 1, D), F32)],
        compiler_params=_params(("arbitrary",)),
    )(x, branch, modtab, target)


QKV_WIDTH = Q_WIDTH + 2 * KV_WIDTH
QK_NORMED = 4


def _seg_mean(v):
    lane = lax.broadcasted_iota(jnp.int32, v.shape, 1)
    lo = lane < HEAD_DIM
    s0 = jnp.sum(jnp.where(lo, v, 0.0), axis=-1, keepdims=True)
    s1 = jnp.sum(jnp.where(lo, 0.0, v), axis=-1, keepdims=True)
    return jnp.where(lo, s0, s1) * (1.0 / HEAD_DIM)


def _pair_swap(v):
    lane = lax.broadcasted_iota(jnp.int32, v.shape, 1)
    return jnp.where((lane & 1) == 0, pltpu.roll(v, LANES - 1, 1), pltpu.roll(v, 1, 1))


def _chunk(c):
    return slice(c * LANES, (c + 1) * LANES)


def _qk_prep(z, gains, cos, sin, *, TR, P, name):
    T = z.shape[0]

    def body(z_ref, g_ref, c_ref, s_ref, q_ref, k_ref, v_ref):
        cs, sn = c_ref[...], s_ref[...]
        for ch in range(QK_NORMED):
            xv = z_ref[:, _chunk(ch)]
            y = xv * lax.rsqrt(_seg_mean(xv * xv) + EPS) * g_ref[0, :, _chunk(ch)]
            out = (y * cs + _pair_swap(y) * sn).astype(BF16)
            if ch < QK_NORMED - 1:
                q_ref[:, _chunk(ch)] = out
            else:
                k_ref[...] = out
        v_ref[...] = z_ref[:, _chunk(QK_NORMED)].astype(BF16)

    def out(width):
        return pl.BlockSpec((None, TR, width), lambda i, j: (j, i, 0))
    return pl.pallas_call(
        body, name=name, grid=(T // TR, 2),
        in_specs=[pl.BlockSpec((TR, QKV_WIDTH), lambda i, j: (i, j)),
                  pl.BlockSpec((1, 1, QKV_WIDTH), lambda i, j: (j, 0, 0)),
                  pl.BlockSpec((TR, LANES), lambda i, j: (i % P, 0)),
                  pl.BlockSpec((TR, LANES), lambda i, j: (i % P, 0))],
        out_specs=[out(Q_WIDTH), out(KV_WIDTH), out(KV_WIDTH)],
        out_shape=[_sds((2, T, Q_WIDTH), BF16), _sds((2, T, KV_WIDTH), BF16), _sds((2, T, KV_WIDTH), BF16)],
        compiler_params=_params(("parallel", "parallel")),
    )(z, gains, cos, sin)


def _qk_prep_bwd(z, dq, dk, dv, gains, cos, sin, *, branch, TR, P, name):
    T = z.shape[0]
    nt = T // TR

    def body(z_ref, dq_ref, dk_ref, dv_ref, g_ref, c_ref, s_ref, dz_ref, dg_ref):
        i = pl.program_id(0)
        cs, sn = c_ref[...], s_ref[...]
        parts = []
        for ch in range(QK_NORMED):
            xv, g = z_ref[:, _chunk(ch)], g_ref[0, :, _chunk(ch)]
            dout = dq_ref[:, _chunk(ch)] if ch < QK_NORMED - 1 else dk_ref[...]
            dy = dout * cs + _pair_swap(dout * sn)
            rstd = lax.rsqrt(_seg_mean(xv * xv) + EPS)
            xhat = xv * rstd
            dxhat = dy * g
            dz_ref[:, _chunk(ch)] = (rstd * (dxhat - xhat * _seg_mean(dxhat * xhat))).astype(BF16)
            parts.append(jnp.sum(dy * xhat, axis=0, keepdims=True))
        dz_ref[:, _chunk(QK_NORMED)] = dv_ref[...].astype(BF16)
        parts.append(jnp.zeros((1, LANES), F32))
        part = jnp.concatenate(parts, axis=1)

        @pl.when(i == 0)
        def _():
            dg_ref[0] = part

        @pl.when(i > 0)
        def _():
            dg_ref[0] += part

    def rows(width, col=0):
        return pl.BlockSpec((TR, width), lambda i: (i, col))
    return pl.pallas_call(
        body, name=name, grid=(nt,),
        in_specs=[rows(QKV_WIDTH, branch), rows(Q_WIDTH), rows(KV_WIDTH), rows(KV_WIDTH),
                  pl.BlockSpec((1, 1, QKV_WIDTH), lambda i: (branch, 0, 0)),
                  pl.BlockSpec((TR, LANES), lambda i: (i % P, 0)),
                  pl.BlockSpec((TR, LANES), lambda i: (i % P, 0))],
        out_specs=[rows(QKV_WIDTH), pl.BlockSpec((1, 1, QKV_WIDTH), lambda i: (0, 0, 0))],
        out_shape=[_sds((T, QKV_WIDTH), BF16), _sds((1, 1, QKV_WIDTH), F32)],
        compiler_params=_params(("arbitrary",)),
    )(z, dq, dk, dv, gains, cos, sin)


NT_DIMS = (((1,), (1,)), ((), ()))
TN_DIMS = (((0,), (0,)), ((), ()))
QROWS = N_GROUP * Q_BLOCK
SCORE_SCALE = HEAD_DIM ** -0.5
BAND = Q_BLOCK + 2 * WINDOW
FWD_LATENT_CHUNK = 256
BWD_LATENT_CHUNK = 1024


def _move_head(block, half_from, half_to):
    lane = lax.broadcasted_iota(jnp.int32, block.shape, 1)
    src = block if half_from == half_to else pltpu.roll(block, HEAD_DIM, 1)
    keep = (lane < HEAD_DIM) if half_to == 0 else (lane >= HEAD_DIM)
    return jnp.where(keep, src, 0.0)


def _stack_heads(lane_block, j):
    pieces = []
    for h in range(N_GROUP * j, N_GROUP * (j + 1)):
        pieces.append(_move_head(lane_block(h // 2), h % 2, j))
    return jnp.concatenate(pieces, axis=0)


def _lane_blocks(ref):
    return lambda m: ref[:, m * LANES:(m + 1) * LANES].astype(F32)


def _unstack_heads(stacked, ref):
    heads = []
    for h in range(N_HEADS):
        j, r = h // N_GROUP, h % N_GROUP
        heads.append(_move_head(stacked[j][r * Q_BLOCK:(r + 1) * Q_BLOCK], j, h % 2))
    for m in range(N_HEADS // 2):
        ref[:, m * LANES:(m + 1) * LANES] = (heads[2 * m] + heads[2 * m + 1]).astype(ref.dtype)


def _key_chunks(i, latent, *, n_ctx, t_all, window, chunk, latent_chunk):
    ctx = [(s, chunk, False) for s in range(0, n_ctx, chunk)]
    if not latent:
        return ctx
    if not window:
        wide = latent_chunk if (t_all - n_ctx) % latent_chunk == 0 else chunk
        return ctx + [(s, wide, False) for s in range(n_ctx, t_all, wide)]
    start = pl.multiple_of(jnp.minimum((i - 1) * Q_BLOCK, t_all - BAND), Q_BLOCK)
    band_chunk = BAND if latent_chunk >= BAND else (chunk if BAND % chunk == 0 else Q_BLOCK)
    return ctx + [(start + s, band_chunk, True) for s in range(0, BAND, band_chunk)]


def _scores(q, k_ref, i, start, size, masked, *, n_ctx):
    s = lax.dot_general(q, k_ref[pl.ds(start, size), :], NT_DIMS, preferred_element_type=F32)
    if masked:
        qpos = (i * Q_BLOCK - n_ctx) + (lax.broadcasted_iota(jnp.int32, (QROWS, size), 0) & (Q_BLOCK - 1))
        kpos = (start - n_ctx) + lax.broadcasted_iota(jnp.int32, (QROWS, size), 1)
        valid = (kpos - qpos <= WINDOW) & (qpos - kpos <= WINDOW) & (kpos >= 0)
        s = jnp.where(valid, s, NEG)
    return s


def _sink_column(sink_ref, j):
    r = lax.broadcasted_iota(jnp.int32, (QROWS, 1), 0)
    s0, s1, s2 = sink_ref[j * N_GROUP], sink_ref[j * N_GROUP + 1], sink_ref[j * N_GROUP + 2]
    return jnp.where(r < Q_BLOCK, s0, jnp.where(r < 2 * Q_BLOCK, s1, s2))


def _attn_specs(Tp, branch):
    nq = Tp // Q_BLOCK
    q_in = pl.BlockSpec((None, Q_BLOCK, Q_WIDTH), lambda b, i: (branch, b * nq + i, 0))
    kv_in = pl.BlockSpec((None, Tp, KV_WIDTH), lambda b, i: (branch, b, 0))
    q_out = pl.BlockSpec((Q_BLOCK, Q_WIDTH), lambda b, i: (b * nq + i, 0))
    kv_out = pl.BlockSpec((Tp, KV_WIDTH), lambda b, i: (b, 0))
    return q_in, kv_in, q_out, kv_out


def _attn_chunk(Tp):
    return 256 if Tp % 256 == 0 else Q_BLOCK


def _attn_fwd(q, k, v, sink, *, branch, B, n_ctx, window, name, rider=None):
    T = q.shape[1]
    Tp = T // B
    nq = Tp // Q_BLOCK
    has_sink = sink is not None
    n_in = 4 if has_sink else 3
    q_in, kv_in, q_out, _ = _attn_specs(Tp, branch)
    lse_spec = pl.BlockSpec((None, N_KV * QROWS, 1), lambda b, i: (b * nq + i, 0, 0))

    def body(*refs):
        refs, finish_ride = _ride(rider, refs, n_in, 3, 0, (B, nq))
        sink_ref = refs.pop(0) if has_sink else None
        q_ref, k_ref, v_ref, o_ref, o32_ref, lse_ref = refs
        i = pl.program_id(1)

        def run(latent):
            outs = []
            for j in range(N_KV):
                qv = (_stack_heads(_lane_blocks(q_ref), j) * SCORE_SCALE).astype(BF16)
                if has_sink:
                    m, l = _sink_column(sink_ref, j), jnp.ones((QROWS, 1), F32)
                else:
                    m, l = jnp.full((QROWS, 1), NEG, F32), jnp.zeros((QROWS, 1), F32)
                acc = jnp.zeros((QROWS, LANES), F32)
                for start, size, masked in _key_chunks(i, latent, n_ctx=n_ctx, t_all=Tp, window=window,
                                                       chunk=_attn_chunk(Tp), latent_chunk=FWD_LATENT_CHUNK):
                    s = _scores(qv, k_ref, i, start, size, masked, n_ctx=n_ctx)
                    m_new = jnp.maximum(m, jnp.max(s, axis=-1, keepdims=True))
                    alpha = jnp.exp(m - m_new)
                    p = jnp.exp(s - m_new)
                    l = l * alpha + jnp.sum(p, axis=-1, keepdims=True)
                    acc = acc * alpha + jnp.dot(p.astype(BF16), v_ref[pl.ds(start, size), :], preferred_element_type=F32)
                    m = m_new
                outs.append(acc * (1.0 / l))
                lse_ref[j * QROWS:(j + 1) * QROWS, :] = m + jnp.log(l)
            _unstack_heads(outs, o_ref)
            _unstack_heads(outs, o32_ref)

        @pl.when(i < n_ctx // Q_BLOCK)
        def _():
            run(False)

        @pl.when(i >= n_ctx // Q_BLOCK)
        def _():
            run(True)

        finish_ride()

    ins, specs = [q, k, v], [q_in, kv_in, kv_in]
    if has_sink:
        ins, specs = [sink] + ins, [pl.BlockSpec(memory_space=pltpu.SMEM)] + specs
    out_specs = [q_out, q_out, lse_spec]
    out_shape = [_sds((T, Q_WIDTH), BF16), _sds((T, Q_WIDTH), F32), _sds((T // Q_BLOCK, N_KV * QROWS, 1), F32)]
    ins, specs, out_specs, out_shape, scratch = _hitch(rider, ins, specs, out_specs, out_shape, [])
    return pl.pallas_call(
        body, name=name, grid=(B, nq), in_specs=specs, out_specs=out_specs, out_shape=out_shape, scratch_shapes=scratch,
        compiler_params=_params(("arbitrary", "arbitrary") if rider is not None else ("parallel", "parallel")),
    )(*ins)


def _attn_bwd(q, k, v, do, o32, lse, sink, *, branch, B, n_ctx, window, name, rider=None):
    T = q.shape[1]
    Tp = T // B
    nq = Tp // Q_BLOCK
    has_sink = sink is not None
    q_in, kv_in, q_out, kv_out = _attn_specs(Tp, branch)
    lse_spec = pl.BlockSpec((None, N_KV * QROWS, 1), lambda b, i: (b * nq + i, 0, 0))
    sink_spec = pl.BlockSpec((None, 8, LANES), lambda b, i: (b, 0, 0))

    def body(*refs):
        refs, finish_ride = _ride(rider, refs, 7 if has_sink else 6, 4 if has_sink else 3, 2, (B, nq))
        if has_sink:
            sink_ref, q_ref, k_ref, v_ref, do_ref, o_ref, lse_ref, dq_ref, dk_ref, dv_ref, ds_ref, dkt_ref, dvt_ref = refs
        else:
            q_ref, k_ref, v_ref, do_ref, o_ref, lse_ref, dq_ref, dk_ref, dv_ref, dkt_ref, dvt_ref = refs
        i = pl.program_id(1)

        @pl.when(i == 0)
        def _():
            dk_ref[...] = jnp.zeros_like(dk_ref)
            dv_ref[...] = jnp.zeros_like(dv_ref)
            if not window:
                dkt_ref[...] = jnp.zeros_like(dkt_ref)
                dvt_ref[...] = jnp.zeros_like(dvt_ref)
            if has_sink:
                ds_ref[...] = jnp.zeros_like(ds_ref)

        def run(latent):
            upd = jnp.zeros((8, LANES), F32)
            do_blocks, o_blocks = _lane_blocks(do_ref), _lane_blocks(o_ref)
            qvs = [(_stack_heads(_lane_blocks(q_ref), j) * SCORE_SCALE).astype(BF16) for j in range(N_KV)]
            dovs = [_stack_heads(do_blocks, j).astype(BF16) for j in range(N_KV)]
            deltas = [jnp.sum(_stack_heads(lambda m: do_blocks(m) * o_blocks(m), j), axis=-1, keepdims=True)
                      for j in range(N_KV)]
            lses = [lse_ref[j * QROWS:(j + 1) * QROWS, :] for j in range(N_KV)]
            q_all, do_all = jnp.concatenate(qvs, axis=0), jnp.concatenate(dovs, axis=0)
            q_all_t, do_all_t = q_all.T, do_all.T
            dqs = [jnp.zeros((QROWS, LANES), F32) for _ in range(N_KV)]
            for start, size, masked in _key_chunks(i, latent, n_ctx=n_ctx, t_all=Tp, window=window,
                                                   chunk=_attn_chunk(Tp), latent_chunk=BWD_LATENT_CHUNK):
                rows = pl.ds(start, size)
                ds_all, p_all = [], []
                for j in range(N_KV):
                    p = jnp.exp(_scores(qvs[j], k_ref, i, start, size, masked, n_ctx=n_ctx) - lses[j])
                    dp = lax.dot_general(dovs[j], v_ref[rows, :], NT_DIMS, preferred_element_type=F32)
                    ds = (p * (dp - deltas[j])).astype(BF16)
                    dqs[j] = dqs[j] + jnp.dot(ds, k_ref[rows, :], preferred_element_type=F32)
                    ds_all.append(ds)
                    p_all.append(p.astype(BF16))
                ds_cat, p_cat = jnp.concatenate(ds_all, axis=0), jnp.concatenate(p_all, axis=0)
                if window:
                    dk_ref[rows, :] += lax.dot_general(ds_cat, q_all, TN_DIMS, preferred_element_type=F32)
                    dv_ref[rows, :] += lax.dot_general(p_cat, do_all, TN_DIMS, preferred_element_type=F32)
                else:
                    dkt_ref[:, start:start + size] += jnp.dot(q_all_t, ds_cat, preferred_element_type=F32)
                    dvt_ref[:, start:start + size] += jnp.dot(do_all_t, p_cat, preferred_element_type=F32)
            dqs = [dq * SCORE_SCALE for dq in dqs]
            for j in range(N_KV):
                if has_sink:
                    contrib = -(jnp.exp(_sink_column(sink_ref, j) - lses[j]) * deltas[j])
                    r = lax.broadcasted_iota(jnp.int32, (QROWS, 1), 0)
                    row8 = lax.broadcasted_iota(jnp.int32, (8, LANES), 0)
                    for h in range(N_GROUP):
                        in_head = (r >= h * Q_BLOCK) & (r < (h + 1) * Q_BLOCK)
                        tot = jnp.sum(jnp.where(in_head, contrib, 0.0), axis=0, keepdims=True)
                        upd = upd + jnp.where(row8 == j * N_GROUP + h, tot, 0.0)
            _unstack_heads(dqs, dq_ref)
            if has_sink:
                ds_ref[...] += upd

        @pl.when(i < n_ctx // Q_BLOCK)
        def _():
            run(False)

        @pl.when(i >= n_ctx // Q_BLOCK)
        def _():
            run(True)

        if not window:
            @pl.when(i == nq - 1)
            def _():
                dk_ref[...] += dkt_ref[...].T
                dv_ref[...] += dvt_ref[...].T

        finish_ride()

    ins, specs = [q, k, v, do, o32, lse], [q_in, kv_in, kv_in, q_out, q_out, lse_spec]
    out_specs = [q_out, kv_out, kv_out]
    out_shape = [_sds((T, Q_WIDTH), F32), _sds((T, KV_WIDTH), F32), _sds((T, KV_WIDTH), F32)]
    if has_sink:
        ins, specs = [sink] + ins, [pl.BlockSpec(memory_space=pltpu.SMEM)] + specs
        out_specs.append(sink_spec)
        out_shape.append(_sds((B, 8, LANES), F32))
    scratch = [pltpu.VMEM((KV_WIDTH, LANES if window else Tp), F32)] * 2
    ins, specs, out_specs, out_shape, scratch = _hitch(rider, ins, specs, out_specs, out_shape, scratch)
    return pl.pallas_call(
        body, name=name, grid=(B, nq), in_specs=specs, out_specs=out_specs, out_shape=out_shape, scratch_shapes=scratch,
        compiler_params=_params(("arbitrary", "arbitrary") if rider is not None else ("parallel", "arbitrary")),
    )(*ins)


def _window_sums(xp):
    n = xp.shape[0]

    def ahead(a, k):
        return pltpu.roll(a, n - k, 0)
    a2 = xp + ahead(xp, 1)
    a4 = a2 + ahead(a2, 2)
    a8 = a4 + ahead(a4, 4)
    a16 = a8 + ahead(a8, 8)
    return (a2, a4, a8, a16)


def _by_group(vals):
    lane = lax.broadcasted_iota(jnp.int32, vals[0].shape, 1)
    return jnp.where(lane < POOL_CH, vals[0], jnp.where(lane < 2 * POOL_CH, vals[1],
                     jnp.where(lane < 3 * POOL_CH, vals[2], vals[3])))


def _pool_counts(n):
    t = lax.broadcasted_iota(jnp.int32, (n, POOL_WIDTH), 0)
    cnts = [(jnp.minimum(t + w // 2, n) - jnp.maximum(t - w // 2, 0)).astype(F32) for w in POOL_WINDOWS]
    return _by_group(cnts)


def _pad_rows(x):
    zeros = jnp.zeros((POOL_PAD, x.shape[1]), x.dtype)
    return jnp.concatenate([zeros, x, zeros], axis=0)


def _pool_stream(u):
    n = u.shape[0]
    sums = _window_sums(_pad_rows(u))
    tots = [pltpu.roll(a, w // 2, 0)[POOL_PAD:POOL_PAD + n] for a, w in zip(sums, POOL_WINDOWS)]
    return _by_group(tots) / _pool_counts(n) - u


def _pool_stream_t(dp):
    n = dp.shape[0]
    sums = _window_sums(_pad_rows(dp / _pool_counts(n)))
    tots = [pltpu.roll(a, w // 2 - 1, 0)[POOL_PAD:POOL_PAD + n] if w > 2 else a[POOL_PAD:POOL_PAD + n]
            for a, w in zip(sums, POOL_WINDOWS)]
    return _by_group(tots) - dp


def _pool_fwd(z, w_bd, scale, *, B, Tp, n_ctx, name):
    T = z.shape[0]
    blk = pl.BlockSpec((Tp, POOL_WIDTH), lambda b: (b, U_COL // POOL_WIDTH))
    out = pl.BlockSpec((Tp, POOL_WIDTH), lambda b: (b, 0))

    def body(u_ref, w_ref, s_ref, p_ref, o_ref):
        for lo, hi in ((0, n_ctx), (n_ctx, Tp)):
            pooled = _pool_stream(u_ref[lo:hi, :]).astype(BF16)
            p_ref[lo:hi, :] = pooled
            mixed = jnp.dot(pooled, w_ref[...], preferred_element_type=F32)
            o_ref[lo:hi, :] = (mixed * s_ref[...]).astype(BF16)

    return pl.pallas_call(
        body, name=name, grid=(B,),
        in_specs=[blk, pl.BlockSpec((POOL_WIDTH, POOL_WIDTH), lambda b: (0, 0)), pl.BlockSpec((1, POOL_WIDTH), lambda b: (0, 0))],
        out_specs=[out, out], out_shape=[_sds((T, POOL_WIDTH), BF16)] * 2, compiler_params=_params(("parallel",)),
    )(z, w_bd, scale)


def _pool_bwd(d_ob, pooled, w_bd, scale, *, B, Tp, n_ctx, name):
    T = d_ob.shape[0]
    blk = pl.BlockSpec((Tp, POOL_WIDTH), lambda b: (b, 0))
    wsp = pl.BlockSpec((POOL_WIDTH, POOL_WIDTH), lambda b: (0, 0))
    ssp = pl.BlockSpec((1, POOL_WIDTH), lambda b: (0, 0))

    def body(d_ref, p_ref, w_ref, s_ref, du_ref, dw_ref, dsc_ref):
        @pl.when(pl.program_id(0) == 0)
        def _():
            dw_ref[...] = jnp.zeros_like(dw_ref)
            dsc_ref[...] = jnp.zeros_like(dsc_ref)

        dv, pv, wv = d_ref[...], p_ref[...], w_ref[...]
        mixed = jnp.dot(pv, wv, preferred_element_type=F32)
        dsc_ref[...] += jnp.sum(dv * mixed, axis=0, keepdims=True)
        dmixed = (dv * s_ref[...]).astype(BF16)
        dw_ref[...] += lax.dot_general(pv, dmixed, TN_DIMS, preferred_element_type=F32)
        dpooled = lax.dot_general(dmixed, wv, NT_DIMS, preferred_element_type=F32)
        for lo, hi in ((0, n_ctx), (n_ctx, Tp)):
            du_ref[lo:hi, :] = _pool_stream_t(dpooled[lo:hi, :]).astype(BF16)

    return pl.pallas_call(
        body, name=name, grid=(B,), in_specs=[blk, blk, wsp, ssp], out_specs=[blk, wsp, ssp],
        out_shape=[_sds((T, POOL_WIDTH), BF16), _sds((POOL_WIDTH, POOL_WIDTH), F32), _sds((1, POOL_WIDTH), F32)],
        compiler_params=_params(("arbitrary",)),
    )(d_ob, pooled, w_bd, scale)


def _merge_specs(z, D, TR, tc, wa, wb, wc):
    def act(width):
        return pl.BlockSpec((TR, width), lambda i, n: (i, 0))

    def gate(part):
        return pl.BlockSpec((TR, tc), lambda i, n: (i, (GATE_COL + part * D) // tc + n))
    w_specs = [w.spec(w.shape[0], tc, lambda i, n: (0, n)) for w in (wa, wb, wc)]
    return [act(Q_WIDTH), act(POOL_WIDTH), act(Q_WIDTH), gate(0), gate(1), gate(2)] + w_specs


def _merge_fwd(oa, ob, oc, z, wa, wb, wc, *, D, TR, name):
    T = oa.shape[0]
    tc = D // N_CHIPS

    def body(oa_ref, ob_ref, oc_ref, ga_ref, gb_ref, gc_ref, wa_ref, wb_ref, wc_ref, y_ref):
        acc = jax.nn.sigmoid(ga_ref[...]) * jnp.dot(oa_ref[...], wa_ref[...], preferred_element_type=F32)
        acc += jax.nn.sigmoid(gb_ref[...]) * jnp.dot(ob_ref[...], wb_ref[...], preferred_element_type=F32)
        acc += jax.nn.sigmoid(gc_ref[...]) * jnp.dot(oc_ref[...], wc_ref[...], preferred_element_type=F32)
        y_ref[...] = acc.astype(BF16)

    return pl.pallas_call(
        body, name=name, grid=(T // TR, D // tc), in_specs=_merge_specs(z, D, TR, tc, wa, wb, wc),
        out_specs=pl.BlockSpec((TR, tc), lambda i, n: (i, n)), out_shape=_sds((T, D), BF16),
        compiler_params=_params(("parallel", "parallel")),
    )(oa, ob, oc, z, z, z, wa.arr, wb.arr, wc.arr)


def _merge_bwd(dy, oa, ob, oc, z, wa, wb, wc, *, D, TR, name):
    T = oa.shape[0]
    tc = D // N_CHIPS
    out = pl.BlockSpec((TR, tc), lambda i, n: (i, n))

    def body(dy_ref, oa_ref, ob_ref, oc_ref, ga_ref, gb_ref, gc_ref, wa_ref, wb_ref, wc_ref,
             dpa_ref, dpb_ref, dpc_ref, dga_ref, dgb_ref, dgc_ref):
        dyv = dy_ref[...]
        for o_ref, g_ref, w_ref, dp_ref, dg_ref in ((oa_ref, ga_ref, wa_ref, dpa_ref, dga_ref),
                                                    (ob_ref, gb_ref, wb_ref, dpb_ref, dgb_ref),
                                                    (oc_ref, gc_ref, wc_ref, dpc_ref, dgc_ref)):
            s = jax.nn.sigmoid(g_ref[...])
            proj = jnp.dot(o_ref[...], w_ref[...], preferred_element_type=F32)
            dp_ref[...] = (dyv * s).astype(BF16)
            dg_ref[...] = (dyv * proj * (s * (1.0 - s))).astype(BF16)

    return pl.pallas_call(
        body, name=name, grid=(T // TR, D // tc), in_specs=[out] + _merge_specs(z, D, TR, tc, wa, wb, wc),
        out_specs=[out] * 6, out_shape=[_sds((T, D), BF16)] * 6, compiler_params=_params(("parallel", "parallel")),
    )(dy, oa, ob, oc, z, z, z, wa.arr, wb.arr, wc.arr)


def _silu_rows(cc, name):
    def body(c_ref, s_ref):
        v = c_ref[...]
        s_ref[...] = (v * jax.nn.sigmoid(v)).astype(BF16)
    return pl.pallas_call(body, name=name, out_shape=_sds(cc.shape, BF16))(cc)


def _ada_bwd_rows(dm, ds, cc, name):
    def body(dm_ref, ds_ref, c_ref, db_ref, dc_ref):
        db_ref[...] = jnp.sum(dm_ref[...], axis=0, keepdims=True)
        v = c_ref[...]
        s = jax.nn.sigmoid(v)
        dc_ref[...] = ds_ref[...] * (s * (1.0 + v * (1.0 - s)))
    return pl.pallas_call(body, name=name, out_shape=[_sds((1, dm.shape[1]), F32), _sds(cc.shape, F32)])(dm, ds, cc)


def _row_tile(rows, cols):
    for t in (512, 256, 128, 64, 32, 16, 8):
        if rows % t == 0 and t * cols * 4 <= (1 << 20):
            return t
    return rows


def _working_rows(tr, C, worker):
    return pl.BlockSpec((tr, C), lambda i, c: (jnp.where(c[0] == worker, i, 0), 0))


def _add_landed(own, landed, core, worker, name):
    R, C = own.shape
    tr = _row_tile(R, C)
    row = _working_rows(tr, C, worker)

    def body(c_ref, a_ref, b_ref, o_ref, o16_ref):
        @pl.when(c_ref[0] == worker)
        def _():
            tot = a_ref[...] + b_ref[...].astype(F32)
            o_ref[...] = tot
            o16_ref[...] = tot.astype(BF16)

    grid_spec = pltpu.PrefetchScalarGridSpec(num_scalar_prefetch=1, grid=(R // tr,), in_specs=[row, row], out_specs=[row, row])
    return pl.pallas_call(body, name=name, grid_spec=grid_spec, out_shape=[_sds((R, C), F32), _sds((R, C), BF16)],
                          compiler_params=_params(("arbitrary",)))(core, own, landed)


def _sum_chips(own, landed, chip, core, worker, name):
    _, R, C = own.shape
    tr = _row_tile(R, C)

    def row(i, c):
        return jnp.where(c[0] == worker, i, 0)

    def body(k_ref, c_ref, a_ref, b_ref, o_ref):
        @pl.when(c_ref[0] == worker)
        def _():
            o_ref[...] = ((a_ref[...] + b_ref[0].astype(F32)) + b_ref[1].astype(F32)) + b_ref[2].astype(F32)

    grid_spec = pltpu.PrefetchScalarGridSpec(
        num_scalar_prefetch=2, grid=(R // tr,),
        in_specs=[pl.BlockSpec((None, tr, C), lambda i, k, c: (k[0], row(i, c), 0)),
                  pl.BlockSpec((3, tr, C), lambda i, k, c: (0, row(i, c), 0))],
        out_specs=pl.BlockSpec((tr, C), lambda i, k, c: (row(i, c), 0)))
    return pl.pallas_call(body, name=name, grid_spec=grid_spec, out_shape=_sds((R, C), F32),
                          compiler_params=_params(("arbitrary",)))(chip, core, own, landed)


def _adam_math(w, g, m, v):
    m = ADAM_B1 * m + (1.0 - ADAM_B1) * g
    v = ADAM_B2 * v + (1.0 - ADAM_B2) * (g * g)
    m_hat = m / (1.0 - ADAM_B1 ** ADAM_STEP)
    v_hat = v / (1.0 - ADAM_B2 ** ADAM_STEP)
    delta = -ADAM_LR * (m_hat / (jnp.sqrt(v_hat) + ADAM_EPS) + ADAM_WD * w)
    return delta, m, v


def _adamw(w, reduced, shared, m, v, core, worker, name):
    L, R, C = w.shape
    tr = _row_tile(R, C)

    def body(c_ref, w_ref, r0_ref, r1_ref, s0_ref, s1_ref, m_ref, v_ref, g_ref, d_ref, mo_ref, vo_ref):
        def step(g):
            d, mn, vn = _adam_math(w_ref[...], g, m_ref[...], v_ref[...])
            g_ref[...] = g
            d_ref[...] = d
            mo_ref[...] = mn
            vo_ref[...] = vn

        layer, here = pl.program_id(0), c_ref[0] == worker
        for l, (r_ref, s_ref) in enumerate(((r0_ref, s0_ref), (r1_ref, s1_ref))):
            @pl.when((layer == l) & here)
            def _(r_ref=r_ref):
                step(r_ref[...])

            @pl.when((layer == l) & jnp.logical_not(here))
            def _(s_ref=s_ref):
                step(s_ref[...])

    lay = pl.BlockSpec((None, tr, C), lambda l, i, c: (l, i, 0))
    row = pl.BlockSpec((tr, C), lambda l, i, c: (i, 0))
    grid_spec = pltpu.PrefetchScalarGridSpec(num_scalar_prefetch=1, grid=(L, R // tr),
                                             in_specs=[lay, row, row, row, row, lay, lay], out_specs=[lay] * 4)
    return pl.pallas_call(body, name=name, grid_spec=grid_spec, out_shape=[_sds((L, R, C), F32)] * 4,
                          compiler_params=_params(("parallel", "parallel")))(core, w, *reduced, *shared, m, v)


def _adamw_small(w, parts, m, v, name):
    R, C = w.shape

    def body(w_ref, p_ref, m_ref, v_ref, g_ref, d_ref, mo_ref, vo_ref):
        g = p_ref[0]
        for dev in range(1, 8):
            g = g + p_ref[dev]
        d, mn, vn = _adam_math(w_ref[...], g, m_ref[...], v_ref[...])
        g_ref[...] = g
        d_ref[...] = d
        mo_ref[...] = mn
        vo_ref[...] = vn

    return pl.pallas_call(body, name=name, out_shape=[_sds((R, C), F32)] * 4)(w, parts, m, v)


def _place():
    return lax.axis_index("x"), lax.axis_index("y"), lax.axis_index("c")


def _other_chips(x, y):
    return [(1 - x, y), (x, 1 - y), (1 - x, 1 - y)]


def _rcopy(src, dst, ssem, rsem, dev):
    return pltpu.make_async_remote_copy(src_ref=src, dst_ref=dst, send_sem=ssem, recv_sem=rsem,
                                        device_id=dev, device_id_type=MESH)


GATHER_SEMS = 7


class _LayerGather:
    def __init__(self, shards, layer):
        self.inputs, self.layer, self.n = list(shards), layer, len(shards)
        load, self.groups = [0, 0], ([], [])
        for w in sorted(range(self.n), key=lambda w: -shards[w][0].size):
            g = 0 if load[0] <= load[1] else 1
            self.groups[g].append(w)
            load[g] += shards[w][0].size
        self.out_shape = [_sds((N_CHIPS,) + s.shape[1:], s.dtype) for s in shards]
        self.scratch = [pltpu.SemaphoreType.DMA((self.n, GATHER_SEMS)), pltpu.SemaphoreType.DMA((self.n, GATHER_SEMS))]

    def _own(self, src, out, send_sems, recv_sems):
        x, y, c = _place()
        return [_rcopy(src[w].at[self.layer], out[w].at[2 * x + y], send_sems.at[w, 6], recv_sems.at[w, 6], (x, y, 1 - c))
                for w in range(self.n)]

    def _to_chips(self, g, src, out, send_sems, recv_sems):
        x, y, c = _place()
        return [_rcopy(src[w].at[self.layer], out[w].at[2 * x + y], send_sems.at[w, j], recv_sems.at[w, j], (*chip, c))
                for w in self.groups[g] for j, chip in enumerate(_other_chips(x, y))]

    def start(self, src, out, send_sems, recv_sems):
        c = lax.axis_index("c")
        for cp in self._own(src, out, send_sems, recv_sems):
            cp.start()
        for g in (0, 1):
            @pl.when(c == g)
            def _(g=g):
                for cp in self._to_chips(g, src, out, send_sems, recv_sems):
                    cp.start()

    def finish(self, src, out, send_sems, recv_sems):
        x, y, c = _place()
        sibling = (x, y, 1 - c)
        chips = _other_chips(x, y)
        for g in (0, 1):
            @pl.when(c == g)
            def _(g=g):
                passed = []
                for w in self.groups[g]:
                    for j, (px, py) in enumerate(chips):
                        landed = out[w].at[2 * px + py]
                        _rcopy(landed, landed, send_sems.at[w, j], recv_sems.at[w, j], (px, py, c)).wait_recv()
                        cp = _rcopy(landed, landed, send_sems.at[w, 3 + j], recv_sems.at[w, 3 + j], sibling)
                        cp.start()
                        passed.append(cp)
                for w in self.groups[1 - g]:
                    for j, (px, py) in enumerate(chips):
                        landed = out[w].at[2 * px + py]
                        _rcopy(landed, landed, send_sems.at[w, 3 + j], recv_sems.at[w, 3 + j], sibling).wait_recv()
                for cp in self._to_chips(g, src, out, send_sems, recv_sems) + passed:
                    cp.wait_send()
        for cp in self._own(src, out, send_sems, recv_sems):
            cp.wait_recv()
            cp.wait_send()


def _on_core(fn):
    for g in (0, 1):
        @pl.when(lax.axis_index("c") == g)
        def _(g=g):
            fn(g)


class _ToSibling:
    def __init__(self, arrays, senders):
        self.inputs, self.senders = list(arrays), list(senders)
        n = len(self.inputs)
        self.out_shape = [_sds(a.shape, a.dtype) for a in self.inputs]
        self.scratch = [pltpu.SemaphoreType.DMA((n,)), pltpu.SemaphoreType.DMA((n,))]

    def _copies(self, sender, src, out, send_sems, recv_sems):
        x, y, c = _place()
        return [_rcopy(src[w], out[w], send_sems.at[w], recv_sems.at[w], (x, y, 1 - c))
                for w in range(len(src)) if self.senders[w] == sender]

    def start(self, *refs):
        def go(g):
            for cp in self._copies(g, *refs):
                cp.start()
        _on_core(go)

    def finish(self, *refs):
        def go(g):
            for cp in self._copies(1 - g, *refs):
                cp.wait_recv()
            for cp in self._copies(g, *refs):
                cp.wait_send()
        _on_core(go)


class _ChipSend:
    def __init__(self, blocked, senders):
        self.inputs, self.senders = list(blocked), list(senders)
        n = len(self.inputs)
        self.out_shape = [_sds((3,) + a.shape[1:], a.dtype) for a in self.inputs]
        self.scratch = [pltpu.SemaphoreType.DMA((n, 3)), pltpu.SemaphoreType.DMA((n, 3))]

    def _copies(self, sender, src, out, send_sems, recv_sems):
        x, y, c = _place()
        return [_rcopy(src[w].at[2 * px + py], out[w].at[j], send_sems.at[w, j], recv_sems.at[w, j], (px, py, c))
                for w in range(len(src)) if self.senders[w] == sender for j, (px, py) in enumerate(_other_chips(x, y))]

    def start(self, *refs):
        def go(g):
            for cp in self._copies(g, *refs):
                cp.start()
        _on_core(go)

    def finish(self, *refs):
        def go(g):
            cps = self._copies(g, *refs)
            for cp in cps:
                cp.wait_recv()
            for cp in cps:
                cp.wait_send()
        _on_core(go)


def _ride_alone(rider, name):
    n_in, n_out = len(rider.inputs), len(rider.out_shape)

    def body(*refs):
        args = (refs[:n_in], refs[n_in:n_in + n_out]) + tuple(refs[n_in + n_out:])
        rider.start(*args)
        rider.finish(*args)

    return pl.pallas_call(body, name=name, in_specs=[ANY] * n_in, out_specs=[ANY] * n_out, out_shape=rider.out_shape,
                          scratch_shapes=rider.scratch)(*rider.inputs)


def _hitch(rider, ins, in_specs, out_specs, out_shape, scratch):
    if rider is None:
        return ins, in_specs, out_specs, out_shape, scratch
    return (list(ins) + rider.inputs, list(in_specs) + [ANY] * len(rider.inputs),
            list(out_specs) + [ANY] * len(rider.out_shape), list(out_shape) + rider.out_shape, list(scratch) + rider.scratch)


def _ride(rider, refs, n_in, n_out, n_scratch, grid):
    if rider is None:
        return list(refs), lambda: None
    r_in, r_out = len(rider.inputs), len(rider.out_shape)
    refs = list(refs)
    own_in, ride_in = refs[:n_in], refs[n_in:n_in + r_in]
    rest = refs[n_in + r_in:]
    own_out, ride_out = rest[:n_out], rest[n_out:n_out + r_out]
    rest = rest[n_out + r_out:]
    own_scratch, sems = rest[:n_scratch], rest[n_scratch:]
    ids = [pl.program_id(a) for a in range(len(grid))]
    first = functools.reduce(jnp.logical_and, [i == 0 for i in ids])
    last = functools.reduce(jnp.logical_and, [i == g - 1 for i, g in zip(ids, grid)])

    @pl.when(first)
    def _():
        rider.start(ride_in, ride_out, *sems)

    def finish():
        @pl.when(last)
        def _():
            rider.finish(ride_in, ride_out, *sems)

    return own_in + own_out + own_scratch, finish


def _gather_small(block, name):
    m_per, n = block.shape

    def body(x_ref, out_ref, send_sems, recv_sems, local_sem):
        x, y, c = _place()
        me, sibling = (x, y, c), (x, y, 1 - c)
        chips = _other_chips(x, y)

        def rows(px, py, pc):
            return out_ref.at[pl.ds((4 * px + 2 * py + pc) * m_per, m_per), :]

        def copy(k, blk, to, src=None):
            return _rcopy(rows(*blk) if src is None else src, rows(*blk), send_sems.at[k], recv_sems.at[k], to)

        mine = pltpu.make_async_copy(x_ref, rows(*me), local_sem)
        mine.start()
        first = [copy(0, me, sibling, src=x_ref)]
        first += [copy(1 + j, me, (*chip, c), src=x_ref) for j, chip in enumerate(chips)]
        for cp in first:
            cp.start()
        passed = [copy(4 + j, (*chip, c), sibling) for j, chip in enumerate(chips)]
        for j, chip in enumerate(chips):
            copy(1 + j, (*chip, c), me).wait_recv()
            passed[j].start()
        copy(0, sibling, me).wait_recv()
        for j, chip in enumerate(chips):
            copy(4 + j, (*chip, 1 - c), me).wait_recv()
        for cp in first + passed:
            cp.wait_send()
        mine.wait()

    return pl.pallas_call(
        body, name=name, out_shape=_sds((8 * m_per, n), block.dtype),
        in_specs=[pl.BlockSpec(memory_space=pltpu.VMEM)], out_specs=pl.BlockSpec(memory_space=pltpu.VMEM),
        scratch_shapes=[pltpu.SemaphoreType.DMA((7,)), pltpu.SemaphoreType.DMA((7,)), pltpu.SemaphoreType.DMA],
    )(block)


def _rope_tables(n_ctx, seq):
    rows = seq // GRID_W
    r = jnp.repeat(jnp.arange(rows, dtype=F32), GRID_W)
    col = jnp.tile(jnp.arange(GRID_W, dtype=F32), rows)
    inv = 1.0 / (ROPE_THETA ** (jnp.arange(0, AXIS_DIM, 2, dtype=F32) / AXIS_DIM))
    ang = jnp.concatenate([r[:, None] * inv, col[:, None] * inv], axis=-1)
    cos = jnp.repeat(jnp.cos(ang), 2, axis=-1)
    sin = jnp.repeat(jnp.sin(ang), 2, axis=-1) * jnp.tile(jnp.array([-1.0, 1.0], F32), HEAD_DIM // 2)
    cos = jnp.concatenate([jnp.ones((n_ctx, HEAD_DIM), F32), cos], axis=0)
    sin = jnp.concatenate([jnp.zeros((n_ctx, HEAD_DIM), F32), sin], axis=0)
    return jnp.tile(cos, (1, 2)), jnp.tile(sin, (1, 2))


def _block_diag(w_pool):
    L, G = w_pool.shape[:2]
    eye = jnp.eye(G, dtype=w_pool.dtype)
    return (w_pool[:, :, :, None, :] * eye[None, :, None, :, None]).reshape(L, POOL_WIDTH, POOL_WIDTH)


def _qk_gains(small):
    qn = jnp.stack([small["q_norm_a"], small["q_norm_c"]], axis=1)[:, :, None, :]
    kn = jnp.stack([small["k_norm_a"], small["k_norm_c"]], axis=1)[:, :, None, :]
    L = qn.shape[0]
    rows = jnp.concatenate([jnp.broadcast_to(qn, (L, 2, N_HEADS, HEAD_DIM)), jnp.broadcast_to(kn, (L, 2, N_KV, HEAD_DIM)),
                            jnp.ones((L, 2, N_KV, HEAD_DIM), F32)], axis=2)
    return rows.reshape(L, 2, 1, QKV_WIDTH)


def _local_step(x, c, ctx, c_ctx, small, gw, target, rider=None, overlap=False):
    gw = list(gw)
    B, S, D = x.shape
    N = ctx.shape[1]
    L = small["norm1"].shape[0]
    Tp = N + S
    T = B * Tp
    TR = N
    P = Tp // N
    rows16 = 16
    assert N % Q_BLOCK == 0 and S % N == 0 and B + 1 <= rows16
    TM = _tile(T, (1024, 768, 512, 384, 256, 128))
    TMG = _tile(T, (1024, 768, 512, 384, 256, 128))

    X = jnp.concatenate([ctx, x], axis=1).reshape(T, D)
    cc = jnp.concatenate([c, c_ctx[None], jnp.zeros((rows16 - B - 1, D), F32)], axis=0)
    s_rows = _silu_rows(cc, "silu_rows")
    cos, sin = _rope_tables(N, S)
    all_gains = _qk_gains(small)
    all_w_bd = _block_diag(small["w_pool"]).astype(BF16)

    def weights(l):
        g = gw[l]
        return dict(
            ada=_Opnd(g["w_ada"], "bcols"), w_in=_Opnd(g["w_in"], "bcols"),
            a=_Opnd(g["w_br_a"], "bcols"), b=_Opnd(g["w_br_b"], "bcols"), c=_Opnd(g["w_br_c"], "bcols"),
            out=_Opnd(g["w_out"], "brows"), mlp1=_Opnd(g["w_mlp1"], "bcols"), mlp2=_Opnd(g["w_mlp2"], "brows"))

    IN = weights(0)["w_in"].shape[1]
    DFF = weights(0)["mlp1"].shape[1]
    tn_in = _tile(IN // N_CHIPS, (1152, 768, 512, 384, 256, 128))
    tn_ff = _tile(DFF // N_CHIPS, (1024, 512, 256, 128))
    tn_ada = _tile(6 * D // N_CHIPS, (1536, 768, 512, 256, 128))
    tn_d = D // N_CHIPS
    tk_d = _tile(D, (512,))
    tk_tok = _tile(T, (2304, 1536, 1024, 768, 512, 384, 256))

    saved = []
    xin, pending = X, None
    for l in range(L):
        W = weights(l)
        b_ada = small["b_ada"][l].reshape(1, 6 * D)
        mod = _matmul(s_rows, W["ada"], "nn", tm=rows16, tn=tn_ada, tk=D, name=f"ada_fwd{l}",
                      epilogue=lambda acc, b: (acc + b,), extras=[(b_ada, (1, tn_ada), lambda m, n: (0, n))])
        modtab = mod.reshape(rows16, 1, 6 * D)
        gains = all_gains[l]
        w_bd = all_w_bd[l]
        p_scale = small["pool_scale"][l].reshape(1, POOL_WIDTH)
        sink = small["sink_c"][l]

        x0, h1 = _res_norm(xin, pending, modtab, 0, 1, small["norm1"][l][None], TR=TR, P=P, name=f"norm1_fwd{l}")
        z = _matmul(h1, W["w_in"], "nn", tm=TM, tn=tn_in, tk=D, name=f"in_proj{l}")
        q2, k2, v2 = _qk_prep(z, gains, cos, sin, TR=TR, P=P, name=f"qk_prep{l}")
        riding = rider if l == 0 else None
        oa, oa32, lse_a, *landed = _attn_fwd(q2, k2, v2, None, branch=0, B=B, n_ctx=N, window=False,
                                             name=f"attn_a_fwd{l}", rider=riding)
        if riding is not None:
            gw[riding.layer] = dict(zip(BIG_NAMES, landed))
        oc, oc32, lse_c = _attn_fwd(q2, k2, v2, sink, branch=1, B=B, n_ctx=N, window=True, name=f"attn_c_fwd{l}")
        pooled, ob = _pool_fwd(z, w_bd, p_scale, B=B, Tp=Tp, n_ctx=N, name=f"pool_fwd{l}")
        y = _merge_fwd(oa, ob, oc, z, W["a"], W["b"], W["c"], D=D, TR=TMG, name=f"merge_fwd{l}")
        ao = _matmul(y, W["out"], "nn", tm=TM, tn=D, tk=tn_d, name=f"out_proj{l}")
        x1, h2 = _res_norm(x0, (ao, modtab, 2), modtab, 3, 4, small["norm2"][l][None], TR=TR, P=P, name=f"norm2_fwd{l}")
        a_pre, r_act = _matmul(h2, W["mlp1"], "nn", tm=TM, tn=tn_ff, tk=D, name=f"mlp1_fwd{l}", out_dtypes=(F32, BF16),
                               epilogue=lambda acc: (acc, jnp.square(jnp.maximum(acc, 0.0))))
        mo = _matmul(r_act, W["mlp2"], "nn", tm=TM, tn=D, tk=tn_ff, name=f"mlp2_fwd{l}")
        saved.append(dict(modtab=modtab, gains=gains, w_bd=w_bd, p_scale=p_scale, sink=sink, x0=x0, h1=h1, z=z,
                          q2=q2, k2=k2, v2=v2, oa=oa, ob=ob, oc=oc, oa32=oa32, oc32=oc32, lse_a=lse_a, lse_c=lse_c,
                          pooled=pooled, y=y, ao=ao,
                          x1=x1, h2=h2, a_pre=a_pre, r_act=r_act, mo=mo))
        xin, pending = x1, (mo, modtab, 5)

    dxo, loss, d_mo, dg2 = _loss_head(xin, pending[0], pending[1], 5, target.reshape(B * S, D), TR=TR, P=P, name="loss_head")

    big = {k: [None] * L for k in BIG_NAMES}
    big16 = {k: [None] * L for k in BIG_NAMES}
    sm = {k: [None] * L for k in ("b_ada", "norm1", "norm2", "q_norm_a", "k_norm_a", "q_norm_c", "k_norm_c",
                                   "sink_c", "w_pool", "pool_scale")}

    def dw(key, l, a, b, *, tm, tn, name, tk=tk_tok, blocked=True):
        outs = _matmul(a, b, "tn", tm=tm, tn=tn, tk=tk, name=name, out_dtypes=(F32, BF16), out_blocked=blocked)
        if not blocked:
            outs = [o.reshape(N_CHIPS, o.shape[0] // N_CHIPS, o.shape[1]) for o in outs]
        big[key][l], big16[key][l] = outs
    d_cctx = jnp.zeros((D,), F32)
    for l in reversed(range(L)):
        W, sv = weights(l), saved[l]
        modtab = sv["modtab"]
        ride_now = overlap and l == L - 2
        if ride_now:
            early = _LayerReduce(l + 1, [big[k][l + 1] for k in BIG_NAMES], [big16[k][l + 1] for k in BIG_NAMES])
        d_a = _matmul(d_mo, W["mlp2"], "nt", tm=TM, tn=tn_ff, tk=D, name=f"mlp2_bwd{l}", out_dtypes=(BF16,),
                      epilogue=lambda acc, a: (acc * (2.0 * jnp.maximum(a, 0.0)),),
                      extras=[(sv["a_pre"], (TM, tn_ff), lambda m, n: (m, n))], rider=early.to_worker if ride_now else None)
        if ride_now:
            d_a, landed = d_a
            early_send = early.add(landed)
        dw("w_mlp2", l, sv["r_act"], d_mo, tm=tk_d, tn=D, name=f"mlp2_dw{l}", blocked=False)
        d_h2 = _matmul(d_a, W["mlp1"], "nt", tm=TM, tn=D, tk=tn_ff, name=f"mlp1_bwd{l}")
        dw("w_mlp1", l, sv["h2"], d_a, tm=tk_d, tn=tn_ff, name=f"mlp1_dw{l}")
        dx1, dsh2, dsc2, dn2, d_ao, dg1 = _norm_bwd(sv["x1"], d_h2, dxo, modtab, 4, small["norm2"][l][None],
                                                    (sv["ao"], modtab, 2), TR=TR, P=P, name=f"norm2_bwd{l}")
        d_y = _matmul(d_ao, W["out"], "nt", tm=TM, tn=tn_d, tk=D, name=f"out_bwd{l}")
        dw("w_out", l, sv["y"], d_ao, tm=tk_d, tn=D, name=f"out_dw{l}", blocked=False)
        d_pa, d_pb, d_pc, d_ga, d_gb, d_gc = _merge_bwd(d_y, sv["oa"], sv["ob"], sv["oc"], sv["z"], W["a"], W["b"], W["c"],
                                                        D=D, TR=TMG, name=f"merge_bwd{l}")
        d_oa = _matmul(d_pa, W["a"], "nt", tm=TM, tn=Q_WIDTH, tk=tn_d, name=f"br_a_bwd{l}", out_dtypes=(BF16,))
        d_ob = _matmul(d_pb, W["b"], "nt", tm=TM, tn=POOL_WIDTH, tk=tn_d, name=f"br_b_bwd{l}")
        d_oc = _matmul(d_pc, W["c"], "nt", tm=TM, tn=Q_WIDTH, tk=tn_d, name=f"br_c_bwd{l}", out_dtypes=(BF16,))
        dw("w_br_a", l, sv["oa"], d_pa, tm=Q_WIDTH, tn=tn_d, name=f"br_a_dw{l}")
        dw("w_br_b", l, sv["ob"], d_pb, tm=POOL_WIDTH, tn=tn_d, name=f"br_b_dw{l}")
        dw("w_br_c", l, sv["oc"], d_pc, tm=Q_WIDTH, tn=tn_d, name=f"br_c_dw{l}")
        d_u, d_wbd, d_ps = _pool_bwd(d_ob, sv["pooled"], sv["w_bd"], sv["p_scale"], B=B, Tp=Tp, n_ctx=N, name=f"pool_bwd{l}")
        dqa, dka, dva, *arrived = _attn_bwd(sv["q2"], sv["k2"], sv["v2"], d_oa, sv["oa32"], sv["lse_a"], None, branch=0,
                                            B=B, n_ctx=N, window=False, name=f"attn_a_bwd{l}",
                                            rider=early_send if ride_now else None)
        if ride_now:
            early.from_chips = arrived
        dqc, dkc, dvc, dsink = _attn_bwd(sv["q2"], sv["k2"], sv["v2"], d_oc, sv["oc32"], sv["lse_c"], sv["sink"],
                                         branch=1, B=B, n_ctx=N, window=True, name=f"attn_c_bwd{l}")
        dz_a, dgains_a = _qk_prep_bwd(sv["z"], dqa, dka, dva, sv["gains"], cos, sin, branch=0, TR=TR, P=P,
                                      name=f"qk_prep_a_bwd{l}")
        dz_c, dgains_c = _qk_prep_bwd(sv["z"], dqc, dkc, dvc, sv["gains"], cos, sin, branch=1, TR=TR, P=P,
                                      name=f"qk_prep_c_bwd{l}")
        dz = jnp.concatenate([dz_a, dz_c, d_u, d_ga, d_gb, d_gc], axis=1)
        d_h1 = _matmul(dz, W["w_in"], "nt", tm=TM, tn=D, tk=tn_in, name=f"in_bwd{l}")
        dw("w_in", l, sv["h1"], dz, tm=tk_d, tn=tn_in, name=f"in_dw{l}")
        below = (saved[l - 1]["mo"], saved[l - 1]["modtab"], 5) if l > 0 else None
        dx0, dsh1, dsc1, dn1, *lower = _norm_bwd(sv["x0"], d_h1, dx1, modtab, 1, small["norm1"][l][None], below,
                                                 TR=TR, P=P, name=f"norm1_bwd{l}")
        this_dg2 = dg2
        if l > 0:
            d_mo, dg2 = lower

        dm_groups = jnp.concatenate([dsh1, dsc1, dg1, dsh2, dsc2, this_dg2], axis=-1).reshape(B, 2, 6 * D)
        dm = jnp.concatenate([dm_groups[:, 1], jnp.sum(dm_groups[:, 0], axis=0, keepdims=True),
                              jnp.zeros((rows16 - B - 1, 6 * D), F32)], axis=0)
        dm_bf = dm.astype(BF16)
        d_s = _matmul(dm_bf, W["ada"], "nt", tm=rows16, tn=D, tk=tn_ada, name=f"ada_bwd{l}")
        dw("w_ada", l, s_rows, dm_bf, tm=tk_d, tn=tn_ada, tk=rows16, name=f"ada_dw{l}")
        db_ada, dcc = _ada_bwd_rows(dm, d_s, cc, f"ada_rows_bwd{l}")
        d_cctx = d_cctx + dcc[B]

        sm["b_ada"][l] = db_ada[0]
        sm["norm1"][l] = jnp.sum(dn1, axis=(0, 1))
        sm["norm2"][l] = jnp.sum(dn2, axis=(0, 1))
        dgh = jnp.stack([dgains_a, dgains_c]).reshape(2, QKV_WIDTH // HEAD_DIM, HEAD_DIM)
        sm["q_norm_a"][l] = jnp.sum(dgh[0, :N_HEADS], axis=0)
        sm["k_norm_a"][l] = jnp.sum(dgh[0, N_HEADS:N_HEADS + N_KV], axis=0)
        sm["q_norm_c"][l] = jnp.sum(dgh[1, :N_HEADS], axis=0)
        sm["k_norm_c"][l] = jnp.sum(dgh[1, N_HEADS:N_HEADS + N_KV], axis=0)
        sm["sink_c"][l] = jnp.sum(dsink[:, :N_HEADS, 0], axis=0)
        sm["w_pool"][l] = jnp.stack([d_wbd[g * POOL_CH:(g + 1) * POOL_CH, g * POOL_CH:(g + 1) * POOL_CH]
                                     for g in range(POOL_WIDTH // POOL_CH)])
        sm["pool_scale"][l] = d_ps[0]
        dxo = dx0

    grad_x = dxo.reshape(B, Tp, D)[:, N:]
    small_grads = {k: jnp.stack(v) for k, v in sm.items()}
    small_grads["c_ctx"] = d_cctx
    return loss, grad_x, small_grads, big, big16, (early if overlap else None)


SMALL_NAMES = ("c_ctx", "b_ada", "norm1", "norm2", "q_norm_a", "k_norm_a", "q_norm_c", "k_norm_c", "sink_c",
               "w_pool", "pool_scale")
BIG_NAMES = ("w_ada", "w_in", "w_br_a", "w_br_b", "w_br_c", "w_out", "w_mlp1", "w_mlp2")
WEIGHT_NAMES = ("c_ctx", "w_ada", "b_ada", "norm1", "norm2", "w_in", "q_norm_a", "k_norm_a", "q_norm_c", "k_norm_c",
                "sink_c", "w_pool", "pool_scale", "w_br_a", "w_br_b", "w_br_c", "w_out", "w_mlp1", "w_mlp2")


def _pack(parts, rows):
    flat = jnp.concatenate([p.reshape(-1).astype(F32) for p in parts])
    return jnp.pad(flat, (0, rows * LANES - flat.shape[0])).reshape(rows, LANES)


def _unpack(packed, like):
    flat, out, at = packed.reshape(-1), [], 0
    for p in like:
        out.append(flat[at:at + p.size].reshape(p.shape))
        at += p.size
    return out


def _split_by_bytes(arrays):
    load, owner = [0, 0], [0] * len(arrays)
    for w in sorted(range(len(arrays)), key=lambda w: -arrays[w].size):
        owner[w] = 0 if load[0] <= load[1] else 1
        load[owner[w]] += arrays[w].size
    return owner


class _LayerReduce:
    def __init__(self, layer, partials, partials16):
        self.layer, self.partials = layer, list(partials)
        self.workers = _split_by_bytes(self.partials)
        self.to_worker = _ToSibling([g.reshape(-1, g.shape[-1]) for g in partials16], [1 - wk for wk in self.workers])
        x, y, c = _place()
        self.core = c.astype(jnp.int32).reshape(1)
        self.chip = (2 * x + y).astype(jnp.int32).reshape(1)

    def add(self, landed):
        sums = [_add_landed(g.reshape(-1, g.shape[-1]), r, self.core, wk, f"grads{self.layer}_add_sibling_{k}")
                for k, g, r, wk in zip(BIG_NAMES, self.partials, landed, self.workers)]
        self.in_chip = [h.reshape(g.shape) for g, (h, _) in zip(self.partials, sums)]
        return _ChipSend([h.reshape(g.shape) for g, (_, h) in zip(self.partials, sums)], self.workers)

    def sum(self, from_chips):
        return [_sum_chips(h, r, self.chip, self.core, wk, f"grads{self.layer}_sum_chips_{k}")
                for k, h, r, wk in zip(BIG_NAMES, self.in_chip, from_chips, self.workers)]


def kernel(x, c, ctx, c_ctx, w_ada, b_ada, norm1, norm2, w_in, q_norm_a, k_norm_a, q_norm_c, k_norm_c, sink_c, w_pool, pool_scale, w_br_a, w_br_b, w_br_c, w_out, w_mlp1, w_mlp2, loss_target, m_c_ctx, m_w_ada, m_b_ada, m_norm1, m_norm2, m_w_in, m_q_norm_a, m_k_norm_a, m_q_norm_c, m_k_norm_c, m_sink_c, m_w_pool, m_pool_scale, m_w_br_a, m_w_br_b, m_w_br_c, m_w_out, m_w_mlp1, m_w_mlp2, v_c_ctx, v_w_ada, v_b_ada, v_norm1, v_norm2, v_w_in, v_q_norm_a, v_k_norm_a, v_q_norm_c, v_k_norm_c, v_sink_c, v_w_pool, v_pool_scale, v_w_br_a, v_w_br_b, v_w_br_c, v_w_out, v_w_mlp1, v_w_mlp2):
    given = dict(locals())
    w = {k: given[k] for k in WEIGHT_NAMES}
    m = {k: given["m_" + k] for k in WEIGHT_NAMES}
    v = {k: given["v_" + k] for k in WEIGHT_NAMES}

    shards = [w[k].astype(BF16) for k in BIG_NAMES]
    assert all(s.shape[0] == 2 for s in shards)
    first_layer = dict(zip(BIG_NAMES, _ride_alone(_LayerGather(shards, 0), "gather_weights0")))
    small = {k: w[k] for k in SMALL_NAMES}
    loss_part, grad_x, small_grads, big_grads, big_grads16, early = _local_step(
        x, c, ctx, c_ctx, small, [first_layer, None], loss_target, rider=_LayerGather(shards, 1), overlap=True)

    late = _LayerReduce(0, [big_grads[k][0] for k in BIG_NAMES], [big_grads16[k][0] for k in BIG_NAMES])
    send = late.add(_ride_alone(late.to_worker, "grads0_to_sibling"))
    reduced = [late.sum(_ride_alone(send, "grads0_to_chips")), early.sum(early.from_chips)]
    n_big = len(BIG_NAMES)
    shared = _ride_alone(_ToSibling(reduced[0] + reduced[1], late.workers + early.workers), "grads_share")
    grads, deltas, new_m, new_v = {}, {}, {}, {}
    for i, k in enumerate(BIG_NAMES):
        assert late.workers[i] == early.workers[i]
        grads[k], deltas[k], new_m[k], new_v[k] = _adamw(
            w[k], (reduced[0][i], reduced[1][i]), (shared[i], shared[n_big + i]), m[k], v[k], late.core, late.workers[i],
            f"adamw_{k}")

    sizes = sum(w[k].size for k in SMALL_NAMES) + LANES
    rows = -(-sizes // (8 * LANES)) * 8
    parts = _gather_small(_pack([small_grads[k] for k in SMALL_NAMES] + [loss_part[0]], rows), "gather_small")
    zero = jnp.zeros((LANES,), F32)
    packed = [_pack([t[k] for k in SMALL_NAMES] + [zero], rows) for t in (w, m, v)]
    outs = _adamw_small(packed[0], parts.reshape(8, rows, LANES), packed[1], packed[2], "adamw_small")
    like = [w[k] for k in SMALL_NAMES] + [zero]
    for store, packed_out in zip((grads, deltas, new_m, new_v), outs):
        pieces = _unpack(packed_out, like)
        for k, piece in zip(SMALL_NAMES, pieces):
            store[k] = piece
        if store is grads:
            loss = pieces[-1][0]

    return (loss, grad_x, *[grads[k] for k in WEIGHT_NAMES], *[deltas[k] for k in WEIGHT_NAMES],
            *[new_m[k] for k in WEIGHT_NAMES], *[new_v[k] for k in WEIGHT_NAMES])
```

```python
import functools

import jax
import jax.numpy as jnp
from jax import lax
from jax.experimental import pallas as pl
from jax.experimental.pallas import tpu as pltpu

F32 = jnp.float32
BF16 = jnp.bfloat16

HEAD_DIM = 64
GRID_W = 64
AXIS_DIM = HEAD_DIM // 2
ROPE_THETA = 10000.0
N_HEADS = 6
N_KV = 2
N_GROUP = N_HEADS // N_KV
POOL_CH = 64
POOL_WIDTH = 256
POOL_WINDOWS = (2, 4, 8, 16)
WINDOW = 128
Q_BLOCK = 128
Q_WIDTH = N_HEADS * HEAD_DIM
KV_WIDTH = N_KV * HEAD_DIM
GATE_COL = 2 * (Q_WIDTH + 2 * KV_WIDTH) + POOL_WIDTH
U_COL = 2 * (Q_WIDTH + 2 * KV_WIDTH)
EPS = 1e-6
NEG = -1e30
ADAM_LR = 0.001
ADAM_B1 = 0.9
ADAM_B2 = 0.999
ADAM_EPS = 1e-08
ADAM_WD = 0.01
ADAM_STEP = 10

N_CHIPS = 4
LANES = 128
POOL_PAD = 16
VMEM_LIMIT = 48 * 1024 * 1024
MESH = pl.DeviceIdType.MESH
ANY = pl.BlockSpec(memory_space=pl.ANY)


def _params(sem):
    return pltpu.CompilerParams(dimension_semantics=sem, vmem_limit_bytes=VMEM_LIMIT)


def _sds(shape, dtype):
    return jax.ShapeDtypeStruct(tuple(shape), dtype)


class _Opnd:
    def __init__(self, arr, kind="plain"):
        self.arr, self.kind = arr, kind

    @property
    def shape(self):
        a = self.arr
        if self.kind == "plain":
            return a.shape
        if self.kind == "bcols":
            return (a.shape[1], N_CHIPS * a.shape[2])
        return (N_CHIPS * a.shape[1], a.shape[2])

    def spec(self, tr, tc, fn):
        a = self.arr
        if self.kind == "plain":
            return pl.BlockSpec((tr, tc), lambda *g: fn(*g))
        if self.kind == "bcols":
            assert a.shape[2] % tc == 0, (a.shape, tc)
            per = a.shape[2] // tc

            def im(*g):
                ri, ci = fn(*g)
                return (ci // per, ri, ci % per)
            return pl.BlockSpec((None, tr, tc), im)
        assert a.shape[1] % tr == 0, (a.shape, tr)
        per = a.shape[1] // tr

        def im(*g):
            ri, ci = fn(*g)
            return (ri // per, ri % per, ci)
        return pl.BlockSpec((None, tr, tc), im)


def _matmul(a, b, mode, *, tm, tn, tk, name, out_dtypes=(F32,), epilogue=None, extras=(), out_blocked=False, rider=None):
    if not isinstance(a, _Opnd):
        a = _Opnd(a)
    if not isinstance(b, _Opnd):
        b = _Opnd(b)
    if mode == "nn":
        (M, K), (K2, N) = a.shape, b.shape
        a_spec = a.spec(tm, tk, lambda m, n, k: (m, k))
        b_spec = b.spec(tk, tn, lambda m, n, k: (k, n))
        dims = (((1,), (0,)), ((), ()))
    elif mode == "nt":
        (M, K), (N, K2) = a.shape, b.shape
        a_spec = a.spec(tm, tk, lambda m, n, k: (m, k))
        b_spec = b.spec(tn, tk, lambda m, n, k: (n, k))
        dims = (((1,), (1,)), ((), ()))
    else:
        (K, M), (K2, N) = a.shape, b.shape
        a_spec = a.spec(tk, tm, lambda m, n, k: (k, m))
        b_spec = b.spec(tk, tn, lambda m, n, k: (k, n))
        dims = (((0,), (0,)), ((), ()))
    assert K == K2 and M % tm == 0 and N % tn == 0 and K % tk == 0, (name, M, N, K, K2, tm, tn, tk)
    nk = K // tk
    n_extra = len(extras)
    n_out = len(out_dtypes)
    extra_specs = [pl.BlockSpec(bs, functools.partial(lambda m, n, k, f: f(m, n), f=f)) for (_, bs, f) in extras]
    if out_blocked:
        assert (N // N_CHIPS) % tn == 0
        per = (N // N_CHIPS) // tn
        out_shape = [_sds((N_CHIPS, M, N // N_CHIPS), dt) for dt in out_dtypes]
        out_specs = [pl.BlockSpec((None, tm, tn), lambda m, n, k: (n // per, m, n % per)) for _ in out_dtypes]
    else:
        out_shape = [_sds((M, N), dt) for dt in out_dtypes]
        out_specs = [pl.BlockSpec((tm, tn), lambda m, n, k: (m, n)) for _ in out_dtypes]

    in_place = nk > 1 and epilogue is None and out_dtypes[0] == F32

    grid = (M // tm, N // tn, nk)
    own_scratch = [pltpu.VMEM((tm, tn), F32)] if nk > 1 and not in_place else []

    def body(*refs):
        refs, finish_ride = _ride(rider, refs, 2 + n_extra, n_out, len(own_scratch), grid)
        a_ref, b_ref = refs[0], refs[1]
        extra_refs = refs[2:2 + n_extra]
        out_refs = refs[2 + n_extra:2 + n_extra + n_out]
        acc_ref = out_refs[0] if in_place else (refs[2 + n_extra + n_out] if nk > 1 else None)
        k = pl.program_id(2)
        prod = lax.dot_general(a_ref[...].astype(BF16), b_ref[...].astype(BF16), dims, preferred_element_type=F32)

        def finish(acc):
            outs = epilogue(acc, *[r[...] for r in extra_refs]) if epilogue is not None else (acc,) * n_out
            for o_ref, o in zip(out_refs, outs):
                o_ref[...] = o.astype(o_ref.dtype)

        if nk == 1:
            finish(prod)
        elif in_place:
            @pl.when(k == 0)
            def _():
                acc_ref[...] = prod

            @pl.when(k > 0)
            def _():
                acc_ref[...] += prod

            if n_out > 1:
                @pl.when(k == nk - 1)
                def _():
                    for o_ref in out_refs[1:]:
                        o_ref[...] = acc_ref[...].astype(o_ref.dtype)
        else:
            @pl.when(k == 0)
            def _():
                acc_ref[...] = prod

            @pl.when(k > 0)
            def _():
                acc_ref[...] += prod

            @pl.when(k == nk - 1)
            def _():
                finish(acc_ref[...])

        finish_ride()

    ins, in_specs, out_specs, out_shape, scratch = _hitch(
        rider, [a.arr, b.arr] + [e[0] for e in extras], [a_spec, b_spec] + extra_specs, out_specs, out_shape, own_scratch)
    outs = pl.pallas_call(
        body, name=name, grid=grid, in_specs=in_specs, out_specs=out_specs, out_shape=out_shape, scratch_shapes=scratch,
        compiler_params=_params(("arbitrary",) * 3 if rider is not None else ("parallel", "parallel", "arbitrary")),
    )(*ins)
    if rider is not None:
        return (outs[0] if n_out == 1 else outs[:n_out]), outs[n_out:]
    return outs[0] if n_out == 1 else outs


def _tile(n, cands):
    for t in cands:
        if n % t == 0:
            return t
    return n


def _grp(i, P):
    return 2 * (i // P) + jnp.minimum(i % P, 1)


def _mod_spec(D, P, part, B):
    return pl.BlockSpec((1, 1, D), lambda i: (jnp.where(i % P == 0, B, i // P), 0, part))


def _res_norm(x, pending, modtab, shift_part, scale_part, gain, *, TR, P, name):
    T, D = x.shape
    row = pl.BlockSpec((TR, D), lambda i: (i, 0))
    has_branch = pending is not None
    ins, specs = [x], [row]
    if has_branch:
        branch, gate_tab, gate_part = pending
        ins += [branch, gate_tab]
        specs += [row, _mod_spec(D, P, gate_part, T // (TR * P))]
    ins += [modtab, modtab, gain]
    specs += [_mod_spec(D, P, shift_part, T // (TR * P)), _mod_spec(D, P, scale_part, T // (TR * P)), pl.BlockSpec((1, D), lambda i: (0, 0))]

    def body(*refs):
        if has_branch:
            x_ref, br_ref, g_ref, sh_ref, sc_ref, gn_ref, xo_ref, h_ref = refs
            xv = x_ref[...] + g_ref[0] * br_ref[...]
        else:
            x_ref, sh_ref, sc_ref, gn_ref, xo_ref, h_ref = refs
            xv = x_ref[...]
        xo_ref[...] = xv
        y = xv * lax.rsqrt(jnp.mean(xv * xv, axis=-1, keepdims=True) + EPS) * gn_ref[...]
        h_ref[...] = (y * (1.0 + sc_ref[0]) + sh_ref[0]).astype(BF16)

    return pl.pallas_call(
        body, name=name, grid=(T // TR,), in_specs=specs, out_specs=[row, row],
        out_shape=[_sds((T, D), F32), _sds((T, D), BF16)], compiler_params=_params(("parallel",)),
    )(*ins)


def _norm_bwd(x, dh, dres, modtab, scale_part, gain, below, *, TR, P, name):
    T, D = x.shape
    G = 2 * (T // (TR * P))
    row = pl.BlockSpec((TR, D), lambda i: (i, 0))
    acc = pl.BlockSpec((1, 1, D), lambda i: (_grp(i, P), 0, 0))
    has_below = below is not None

    def body(*refs):
        if has_below:
            x_ref, dh_ref, dres_ref, sc_ref, gn_ref, br_ref, g_ref, dx_ref, dsh_ref, dsc_ref, dgn_ref, db_ref, dg_ref = refs
        else:
            x_ref, dh_ref, dres_ref, sc_ref, gn_ref, dx_ref, dsh_ref, dsc_ref, dgn_ref = refs
        r = pl.program_id(0) % P
        xv, dhv, gn = x_ref[...], dh_ref[...], gn_ref[...]
        rstd = lax.rsqrt(jnp.mean(xv * xv, axis=-1, keepdims=True) + EPS)
        xhat = xv * rstd
        dn = dhv * (1.0 + sc_ref[0])
        dxhat = dn * gn
        dxv = dres_ref[...] + rstd * (dxhat - xhat * jnp.mean(dxhat * xhat, axis=-1, keepdims=True))
        dx_ref[...] = dxv
        parts = [jnp.sum(dhv, axis=0, keepdims=True), jnp.sum(dhv * (xhat * gn), axis=0, keepdims=True),
                 jnp.sum(dn * xhat, axis=0, keepdims=True)]
        outs = [dsh_ref, dsc_ref, dgn_ref]
        if has_below:
            db_ref[...] = (dxv * g_ref[0]).astype(BF16)
            parts.append(jnp.sum(dxv * br_ref[...], axis=0, keepdims=True))
            outs.append(dg_ref)

        @pl.when(r <= 1)
        def _():
            for o_ref, part in zip(outs, parts):
                o_ref[0] = part

        @pl.when(r > 1)
        def _():
            for o_ref, part in zip(outs, parts):
                o_ref[0] += part

    ins = [x, dh, dres, modtab, gain]
    in_specs = [row, row, row, _mod_spec(D, P, scale_part, T // (TR * P)), pl.BlockSpec((1, D), lambda i: (0, 0))]
    out_specs, out_shape = [row, acc, acc, acc], [_sds((T, D), F32)] + [_sds((G, 1, D), F32)] * 3
    if has_below:
        branch, gate_tab, gate_part = below
        ins += [branch, gate_tab]
        in_specs += [row, _mod_spec(D, P, gate_part, T // (TR * P))]
        out_specs += [row, acc]
        out_shape += [_sds((T, D), BF16), _sds((G, 1, D), F32)]
    return pl.pallas_call(body, name=name, grid=(T // TR,), in_specs=in_specs, out_specs=out_specs, out_shape=out_shape,
                          compiler_params=_params(("arbitrary",)))(*ins)


def _loss_head(x, branch, modtab, gate_part, target, *, TR, P, name):
    T, D = x.shape
    row = pl.BlockSpec((TR, D), lambda i: (i, 0))
    tgt = pl.BlockSpec((TR, D), lambda i: ((i // P) * (P - 1) + jnp.maximum(i % P - 1, 0), 0))
    one = pl.BlockSpec((1, LANES), lambda i: (0, 0))

    G = 2 * (T // (TR * P))
    acc = pl.BlockSpec((1, 1, D), lambda i: (_grp(i, P), 0, 0))

    def body(x_ref, br_ref, g_ref, t_ref, dy_ref, loss_ref, db_ref, dg_ref):
        i = pl.program_id(0)
        r = i % P

        @pl.when(i == 0)
        def _():
            loss_ref[...] = jnp.zeros_like(loss_ref)

        @pl.when(r == 0)
        def _():
            dy_ref[...] = jnp.zeros_like(dy_ref)
            db_ref[...] = jnp.zeros_like(db_ref)
            dg_ref[...] = jnp.zeros_like(dg_ref)

        @pl.when(r > 0)
        def _():
            brv, g = br_ref[...], g_ref[0]
            err = x_ref[...] + g * brv - t_ref[...]
            dy = err / D
            dy_ref[...] = dy
            db_ref[...] = (dy * g).astype(BF16)
            part = jnp.sum(dy * brv, axis=0, keepdims=True)
            per_tok = jnp.mean(err * err, axis=-1, keepdims=True)
            loss_ref[...] += 0.5 * jnp.sum(per_tok, axis=0, keepdims=True)

            @pl.when(r == 1)
            def _():
                dg_ref[0] = part

            @pl.when(r > 1)
            def _():
                dg_ref[0] += part

    return pl.pallas_call(
        body, name=name, grid=(T // TR,), in_specs=[row, row, _mod_spec(D, P, gate_part, T // (TR * P)), tgt],
        out_specs=[row, one, row, acc],
        out_shape=[_sds((T, D), F32), _sds((1, LANES), F32), _sds((T, D), BF16), _sds((G, 1, D), F32)],
        compiler_params=_params(("arbitrary",)),
    )(x, branch, modtab, target)


QKV_WIDTH = Q_WIDTH + 2 * KV_WIDTH
QK_NORMED = 4


def _seg_mean(v):
    lane = lax.broadcasted_iota(jnp.int32, v.shape, 1)
    lo = lane < HEAD_DIM
    s0 = jnp.sum(jnp.where(lo, v, 0.0), axis=-1, keepdims=True)
    s1 = jnp.sum(jnp.where(lo, 0.0, v), axis=-1, keepdims=True)
    return jnp.where(lo, s0, s1) * (1.0 / HEAD_DIM)


def _pair_swap(v):
    lane = lax.broadcasted_iota(jnp.int32, v.shape, 1)
    return jnp.where((lane & 1) == 0, pltpu.roll(v, LANES - 1, 1), pltpu.roll(v, 1, 1))


def _chunk(c):
    return slice(c * LANES, (c + 1) * LANES)


def _qk_prep(z, gains, cos, sin, *, TR, P, name):
    T = z.shape[0]

    def body(z_ref, g_ref, c_ref, s_ref, q_ref, k_ref, v_ref):
        cs, sn = c_ref[...], s_ref[...]
        for ch in range(QK_NORMED):
            xv = z_ref[:, _chunk(ch)]
            y = xv * lax.rsqrt(_seg_mean(xv * xv) + EPS) * g_ref[0, :, _chunk(ch)]
            out = (y * cs + _pair_swap(y) * sn).astype(BF16)
            if ch < QK_NORMED - 1:
                q_ref[:, _chunk(ch)] = out
            else:
                k_ref[...] = out
        v_ref[...] = z_ref[:, _chunk(QK_NORMED)].astype(BF16)

    def out(width):
        return pl.BlockSpec((None, TR, width), lambda i, j: (j, i, 0))
    return pl.pallas_call(
        body, name=name, grid=(T // TR, 2),
        in_specs=[pl.BlockSpec((TR, QKV_WIDTH), lambda i, j: (i, j)),
                  pl.BlockSpec((1, 1, QKV_WIDTH), lambda i, j: (j, 0, 0)),
                  pl.BlockSpec((TR, LANES), lambda i, j: (i % P, 0)),
                  pl.BlockSpec((TR, LANES), lambda i, j: (i % P, 0))],
        out_specs=[out(Q_WIDTH), out(KV_WIDTH), out(KV_WIDTH)],
        out_shape=[_sds((2, T, Q_WIDTH), BF16), _sds((2, T, KV_WIDTH), BF16), _sds((2, T, KV_WIDTH), BF16)],
        compiler_params=_params(("parallel", "parallel")),
    )(z, gains, cos, sin)


def _qk_prep_bwd(z, dq, dk, dv, gains, cos, sin, *, branch, TR, P, name):
    T = z.shape[0]
    nt = T // TR

    def body(z_ref, dq_ref, dk_ref, dv_ref, g_ref, c_ref, s_ref, dz_ref, dg_ref):
        i = pl.program_id(0)
        cs, sn = c_ref[...], s_ref[...]
        parts = []
        for ch in range(QK_NORMED):
            xv, g = z_ref[:, _chunk(ch)], g_ref[0, :, _chunk(ch)]
            dout = dq_ref[:, _chunk(ch)] if ch < QK_NORMED - 1 else dk_ref[...]
            dy = dout * cs + _pair_swap(dout * sn)
            rstd = lax.rsqrt(_seg_mean(xv * xv) + EPS)
            xhat = xv * rstd
            dxhat = dy * g
            dz_ref[:, _chunk(ch)] = (rstd * (dxhat - xhat * _seg_mean(dxhat * xhat))).astype(BF16)
            parts.append(jnp.sum(dy * xhat, axis=0, keepdims=True))
        dz_ref[:, _chunk(QK_NORMED)] = dv_ref[...].astype(BF16)
        parts.append(jnp.zeros((1, LANES), F32))
        part = jnp.concatenate(parts, axis=1)

        @pl.when(i == 0)
        def _():
            dg_ref[0] = part

        @pl.when(i > 0)
        def _():
            dg_ref[0] += part

    def rows(width, col=0):
        return pl.BlockSpec((TR, width), lambda i: (i, col))
    return pl.pallas_call(
        body, name=name, grid=(nt,),
        in_specs=[rows(QKV_WIDTH, branch), rows(Q_WIDTH), rows(KV_WIDTH), rows(KV_WIDTH),
                  pl.BlockSpec((1, 1, QKV_WIDTH), lambda i: (branch, 0, 0)),
                  pl.BlockSpec((TR, LANES), lambda i: (i % P, 0)),
                  pl.BlockSpec((TR, LANES), lambda i: (i % P, 0))],
        out_specs=[rows(QKV_WIDTH), pl.BlockSpec((1, 1, QKV_WIDTH), lambda i: (0, 0, 0))],
        out_shape=[_sds((T, QKV_WIDTH), BF16), _sds((1, 1, QKV_WIDTH), F32)],
        compiler_params=_params(("arbitrary",)),
    )(z, dq, dk, dv, gains, cos, sin)


NT_DIMS = (((1,), (1,)), ((), ()))
TN_DIMS = (((0,), (0,)), ((), ()))
QROWS = N_GROUP * Q_BLOCK
SCORE_SCALE = HEAD_DIM ** -0.5
BAND = Q_BLOCK + 2 * WINDOW
FWD_LATENT_CHUNK = 256
BWD_LATENT_CHUNK = 1024


def _move_head(block, half_from, half_to):
    lane = lax.broadcasted_iota(jnp.int32, block.shape, 1)
    src = block if half_from == half_to else pltpu.roll(block, HEAD_DIM, 1)
    keep = (lane < HEAD_DIM) if half_to == 0 else (lane >= HEAD_DIM)
    return jnp.where(keep, src, 0.0)


def _stack_heads(lane_block, j):
    pieces = []
    for h in range(N_GROUP * j, N_GROUP * (j + 1)):
        pieces.append(_move_head(lane_block(h // 2), h % 2, j))
    return jnp.concatenate(pieces, axis=0)


def _lane_blocks(ref):
    return lambda m: ref[:, m * LANES:(m + 1) * LANES].astype(F32)


def _unstack_heads(stacked, ref):
    heads = []
    for h in range(N_HEADS):
        j, r = h // N_GROUP, h % N_GROUP
        heads.append(_move_head(stacked[j][r * Q_BLOCK:(r + 1) * Q_BLOCK], j, h % 2))
    for m in range(N_HEADS // 2):
        ref[:, m * LANES:(m + 1) * LANES] = (heads[2 * m] + heads[2 * m + 1]).astype(ref.dtype)


def _key_chunks(i, latent, *, n_ctx, t_all, window, chunk, latent_chunk):
    ctx = [(s, chunk, False) for s in range(0, n_ctx, chunk)]
    if not latent:
        return ctx
    if not window:
        wide = latent_chunk if (t_all - n_ctx) % latent_chunk == 0 else chunk
        return ctx + [(s, wide, False) for s in range(n_ctx, t_all, wide)]
    start = pl.multiple_of(jnp.minimum((i - 1) * Q_BLOCK, t_all - BAND), Q_BLOCK)
    band_chunk = BAND if latent_chunk >= BAND else (chunk if BAND % chunk == 0 else Q_BLOCK)
    return ctx + [(start + s, band_chunk, True) for s in range(0, BAND, band_chunk)]


def _scores(q, k_ref, i, start, size, masked, *, n_ctx):
    s = lax.dot_general(q, k_ref[pl.ds(start, size), :], NT_DIMS, preferred_element_type=F32)
    if masked:
        qpos = (i * Q_BLOCK - n_ctx) + (lax.broadcasted_iota(jnp.int32, (QROWS, size), 0) & (Q_BLOCK - 1))
        kpos = (start - n_ctx) + lax.broadcasted_iota(jnp.int32, (QROWS, size), 1)
        valid = (kpos - qpos <= WINDOW) & (qpos - kpos <= WINDOW) & (kpos >= 0)
        s = jnp.where(valid, s, NEG)
    return s


def _sink_column(sink_ref, j):
    r = lax.broadcasted_iota(jnp.int32, (QROWS, 1), 0)
    s0, s1, s2 = sink_ref[j * N_GROUP], sink_ref[j * N_GROUP + 1], sink_ref[j * N_GROUP + 2]
    return jnp.where(r < Q_BLOCK, s0, jnp.where(r < 2 * Q_BLOCK, s1, s2))


def _attn_specs(Tp, branch):
    nq = Tp // Q_BLOCK
    q_in = pl.BlockSpec((None, Q_BLOCK, Q_WIDTH), lambda b, i: (branch, b * nq + i, 0))
    kv_in = pl.BlockSpec((None, Tp, KV_WIDTH), lambda b, i: (branch, b, 0))
    q_out = pl.BlockSpec((Q_BLOCK, Q_WIDTH), lambda b, i: (b * nq + i, 0))
    kv_out = pl.BlockSpec((Tp, KV_WIDTH), lambda b, i: (b, 0))
    return q_in, kv_in, q_out, kv_out


def _attn_chunk(Tp):
    return 256 if Tp % 256 == 0 else Q_BLOCK


def _attn_fwd(q, k, v, sink, *, branch, B, n_ctx, window, name, rider=None):
    T = q.shape[1]
    Tp = T // B
    nq = Tp // Q_BLOCK
    has_sink = sink is not None
    n_in = 4 if has_sink else 3
    q_in, kv_in, q_out, _ = _attn_specs(Tp, branch)
    lse_spec = pl.BlockSpec((None, N_KV * QROWS, 1), lambda b, i: (b * nq + i, 0, 0))

    def body(*refs):
        refs, finish_ride = _ride(rider, refs, n_in, 3, 0, (B, nq))
        sink_ref = refs.pop(0) if has_sink else None
        q_ref, k_ref, v_ref, o_ref, o32_ref, lse_ref = refs
        i = pl.program_id(1)

        def run(latent):
            outs = []
            for j in range(N_KV):
                qv = (_stack_heads(_lane_blocks(q_ref), j) * SCORE_SCALE).astype(BF16)
                if has_sink:
                    m, l = _sink_column(sink_ref, j), jnp.ones((QROWS, 1), F32)
                else:
                    m, l = jnp.full((QROWS, 1), NEG, F32), jnp.zeros((QROWS, 1), F32)
                acc = jnp.zeros((QROWS, LANES), F32)
                for start, size, masked in _key_chunks(i, latent, n_ctx=n_ctx, t_all=Tp, window=window,
                                                       chunk=_attn_chunk(Tp), latent_chunk=FWD_LATENT_CHUNK):
                    s = _scores(qv, k_ref, i, start, size, masked, n_ctx=n_ctx)
                    m_new = jnp.maximum(m, jnp.max(s, axis=-1, keepdims=True))
                    alpha = jnp.exp(m - m_new)
                    p = jnp.exp(s - m_new)
                    l = l * alpha + jnp.sum(p, axis=-1, keepdims=True)
                    acc = acc * alpha + jnp.dot(p.astype(BF16), v_ref[pl.ds(start, size), :], preferred_element_type=F32)
                    m = m_new
                outs.append(acc * (1.0 / l))
                lse_ref[j * QROWS:(j + 1) * QROWS, :] = m + jnp.log(l)
            _unstack_heads(outs, o_ref)
            _unstack_heads(outs, o32_ref)

        @pl.when(i < n_ctx // Q_BLOCK)
        def _():
            run(False)

        @pl.when(i >= n_ctx // Q_BLOCK)
        def _():
            run(True)

        finish_ride()

    ins, specs = [q, k, v], [q_in, kv_in, kv_in]
    if has_sink:
        ins, specs = [sink] + ins, [pl.BlockSpec(memory_space=pltpu.SMEM)] + specs
    out_specs = [q_out, q_out, lse_spec]
    out_shape = [_sds((T, Q_WIDTH), BF16), _sds((T, Q_WIDTH), F32), _sds((T // Q_BLOCK, N_KV * QROWS, 1), F32)]
    ins, specs, out_specs, out_shape, scratch = _hitch(rider, ins, specs, out_specs, out_shape, [])
    return pl.pallas_call(
        body, name=name, grid=(B, nq), in_specs=specs, out_specs=out_specs, out_shape=out_shape, scratch_shapes=scratch,
        compiler_params=_params(("arbitrary", "arbitrary") if rider is not None else ("parallel", "parallel")),
    )(*ins)


def _attn_bwd(q, k, v, do, o32, lse, sink, *, branch, B, n_ctx, window, name, rider=None):
    T = q.shape[1]
    Tp = T // B
    nq = Tp // Q_BLOCK
    has_sink = sink is not None
    q_in, kv_in, q_out, kv_out = _attn_specs(Tp, branch)
    lse_spec = pl.BlockSpec((None, N_KV * QROWS, 1), lambda b, i: (b * nq + i, 0, 0))
    sink_spec = pl.BlockSpec((None, 8, LANES), lambda b, i: (b, 0, 0))

    def body(*refs):
        refs, finish_ride = _ride(rider, refs, 7 if has_sink else 6, 4 if has_sink else 3, 2, (B, nq))
        if has_sink:
            sink_ref, q_ref, k_ref, v_ref, do_ref, o_ref, lse_ref, dq_ref, dk_ref, dv_ref, ds_ref, dkt_ref, dvt_ref = refs
        else:
            q_ref, k_ref, v_ref, do_ref, o_ref, lse_ref, dq_ref, dk_ref, dv_ref, dkt_ref, dvt_ref = refs
        i = pl.program_id(1)

        @pl.when(i == 0)
        def _():
            dk_ref[...] = jnp.zeros_like(dk_ref)
            dv_ref[...] = jnp.zeros_like(dv_ref)
            if not window:
                dkt_ref[...] = jnp.zeros_like(dkt_ref)
                dvt_ref[...] = jnp.zeros_like(dvt_ref)
            if has_sink:
                ds_ref[...] = jnp.zeros_like(ds_ref)

        def run(latent):
            upd = jnp.zeros((8, LANES), F32)
            do_blocks, o_blocks = _lane_blocks(do_ref), _lane_blocks(o_ref)
            qvs = [(_stack_heads(_lane_blocks(q_ref), j) * SCORE_SCALE).astype(BF16) for j in range(N_KV)]
            dovs = [_stack_heads(do_blocks, j).astype(BF16) for j in range(N_KV)]
            deltas = [jnp.sum(_stack_heads(lambda m: do_blocks(m) * o_blocks(m), j), axis=-1, keepdims=True)
                      for j in range(N_KV)]
            lses = [lse_ref[j * QROWS:(j + 1) * QROWS, :] for j in range(N_KV)]
            q_all, do_all = jnp.concatenate(qvs, axis=0), jnp.concatenate(dovs, axis=0)
            q_all_t, do_all_t = q_all.T, do_all.T
            dqs = [jnp.zeros((QROWS, LANES), F32) for _ in range(N_KV)]
            for start, size, masked in _key_chunks(i, latent, n_ctx=n_ctx, t_all=Tp, window=window,
                                                   chunk=_attn_chunk(Tp), latent_chunk=BWD_LATENT_CHUNK):
                rows = pl.ds(start, size)
                ds_all, p_all = [], []
                for j in range(N_KV):
                    p = jnp.exp(_scores(qvs[j], k_ref, i, start, size, masked, n_ctx=n_ctx) - lses[j])
                    dp = lax.dot_general(dovs[j], v_ref[rows, :], NT_DIMS, preferred_element_type=F32)
                    ds = (p * (dp - deltas[j])).astype(BF16)
                    dqs[j] = dqs[j] + jnp.dot(ds, k_ref[rows, :], preferred_element_type=F32)
                    ds_all.append(ds)
                    p_all.append(p.astype(BF16))
                ds_cat, p_cat = jnp.concatenate(ds_all, axis=0), jnp.concatenate(p_all, axis=0)
                if window:
                    dk_ref[rows, :] += lax.dot_general(ds_cat, q_all, TN_DIMS, preferred_element_type=F32)
                    dv_ref[rows, :] += lax.dot_general(p_cat, do_all, TN_DIMS, preferred_element_type=F32)
                else:
                    dkt_ref[:, start:start + size] += jnp.dot(q_all_t, ds_cat, preferred_element_type=F32)
                    dvt_ref[:, start:start + size] += jnp.dot(do_all_t, p_cat, preferred_element_type=F32)
            dqs = [dq * SCORE_SCALE for dq in dqs]
            for j in range(N_KV):
                if has_sink:
                    contrib = -(jnp.exp(_sink_column(sink_ref, j) - lses[j]) * deltas[j])
                    r = lax.broadcasted_iota(jnp.int32, (QROWS, 1), 0)
                    row8 = lax.broadcasted_iota(jnp.int32, (8, LANES), 0)
                    for h in range(N_GROUP):
                        in_head = (r >= h * Q_BLOCK) & (r < (h + 1) * Q_BLOCK)
                        tot = jnp.sum(jnp.where(in_head, contrib, 0.0), axis=0, keepdims=True)
                        upd = upd + jnp.where(row8 == j * N_GROUP + h, tot, 0.0)
            _unstack_heads(dqs, dq_ref)
            if has_sink:
                ds_ref[...] += upd

        @pl.when(i < n_ctx // Q_BLOCK)
        def _():
            run(False)

        @pl.when(i >= n_ctx // Q_BLOCK)
        def _():
            run(True)

        if not window:
            @pl.when(i == nq - 1)
            def _():
                dk_ref[...] += dkt_ref[...].T
                dv_ref[...] += dvt_ref[...].T

        finish_ride()

    ins, specs = [q, k, v, do, o32, lse], [q_in, kv_in, kv_in, q_out, q_out, lse_spec]
    out_specs = [q_out, kv_out, kv_out]
    out_shape = [_sds((T, Q_WIDTH), F32), _sds((T, KV_WIDTH), F32), _sds((T, KV_WIDTH), F32)]
    if has_sink:
        ins, specs = [sink] + ins, [pl.BlockSpec(memory_space=pltpu.SMEM)] + specs
        out_specs.append(sink_spec)
        out_shape.append(_sds((B, 8, LANES), F32))
    scratch = [pltpu.VMEM((KV_WIDTH, LANES if window else Tp), F32)] * 2
    ins, specs, out_specs, out_shape, scratch = _hitch(rider, ins, specs, out_specs, out_shape, scratch)
    return pl.pallas_call(
        body, name=name, grid=(B, nq), in_specs=specs, out_specs=out_specs, out_shape=out_shape, scratch_shapes=scratch,
        compiler_params=_params(("arbitrary", "arbitrary") if rider is not None else ("parallel", "arbitrary")),
    )(*ins)


def _window_sums(xp):
    n = xp.shape[0]

    def ahead(a, k):
        return pltpu.roll(a, n - k, 0)
    a2 = xp + ahead(xp, 1)
    a4 = a2 + ahead(a2, 2)
    a8 = a4 + ahead(a4, 4)
    a16 = a8 + ahead(a8, 8)
    return (a2, a4, a8, a16)


def _by_group(vals):
    lane = lax.broadcasted_iota(jnp.int32, vals[0].shape, 1)
    return jnp.where(lane < POOL_CH, vals[0], jnp.where(lane < 2 * POOL_CH, vals[1],
                     jnp.where(lane < 3 * POOL_CH, vals[2], vals[3])))


def _pool_counts(n):
    t = lax.broadcasted_iota(jnp.int32, (n, POOL_WIDTH), 0)
    cnts = [(jnp.minimum(t + w // 2, n) - jnp.maximum(t - w // 2, 0)).astype(F32) for w in POOL_WINDOWS]
    return _by_group(cnts)


def _pad_rows(x):
    zeros = jnp.zeros((POOL_PAD, x.shape[1]), x.dtype)
    return jnp.concatenate([zeros, x, zeros], axis=0)


def _pool_stream(u):
    n = u.shape[0]
    sums = _window_sums(_pad_rows(u))
    tots = [pltpu.roll(a, w // 2, 0)[POOL_PAD:POOL_PAD + n] for a, w in zip(sums, POOL_WINDOWS)]
    return _by_group(tots) / _pool_counts(n) - u


def _pool_stream_t(dp):
    n = dp.shape[0]
    sums = _window_sums(_pad_rows(dp / _pool_counts(n)))
    tots = [pltpu.roll(a, w // 2 - 1, 0)[POOL_PAD:POOL_PAD + n] if w > 2 else a[POOL_PAD:POOL_PAD + n]
            for a, w in zip(sums, POOL_WINDOWS)]
    return _by_group(tots) - dp


def _pool_fwd(z, w_bd, scale, *, B, Tp, n_ctx, name):
    T = z.shape[0]
    blk = pl.BlockSpec((Tp, POOL_WIDTH), lambda b: (b, U_COL // POOL_WIDTH))
    out = pl.BlockSpec((Tp, POOL_WIDTH), lambda b: (b, 0))

    def body(u_ref, w_ref, s_ref, p_ref, o_ref):
        for lo, hi in ((0, n_ctx), (n_ctx, Tp)):
            pooled = _pool_stream(u_ref[lo:hi, :]).astype(BF16)
            p_ref[lo:hi, :] = pooled
            mixed = jnp.dot(pooled, w_ref[...], preferred_element_type=F32)
            o_ref[lo:hi, :] = (mixed * s_ref[...]).astype(BF16)

    return pl.pallas_call(
        body, name=name, grid=(B,),
        in_specs=[blk, pl.BlockSpec((POOL_WIDTH, POOL_WIDTH), lambda b: (0, 0)), pl.BlockSpec((1, POOL_WIDTH), lambda b: (0, 0))],
        out_specs=[out, out], out_shape=[_sds((T, POOL_WIDTH), BF16)] * 2, compiler_params=_params(("parallel",)),
    )(z, w_bd, scale)


def _pool_bwd(d_ob, pooled, w_bd, scale, *, B, Tp, n_ctx, name):
    T = d_ob.shape[0]
    blk = pl.BlockSpec((Tp, POOL_WIDTH), lambda b: (b, 0))
    wsp = pl.BlockSpec((POOL_WIDTH, POOL_WIDTH), lambda b: (0, 0))
    ssp = pl.BlockSpec((1, POOL_WIDTH), lambda b: (0, 0))

    def body(d_ref, p_ref, w_ref, s_ref, du_ref, dw_ref, dsc_ref):
        @pl.when(pl.program_id(0) == 0)
        def _():
            dw_ref[...] = jnp.zeros_like(dw_ref)
            dsc_ref[...] = jnp.zeros_like(dsc_ref)

        dv, pv, wv = d_ref[...], p_ref[...], w_ref[...]
        mixed = jnp.dot(pv, wv, preferred_element_type=F32)
        dsc_ref[...] += jnp.sum(dv * mixed, axis=0, keepdims=True)
        dmixed = (dv * s_ref[...]).astype(BF16)
        dw_ref[...] += lax.dot_general(pv, dmixed, TN_DIMS, preferred_element_type=F32)
        dpooled = lax.dot_general(dmixed, wv, NT_DIMS, preferred_element_type=F32)
        for lo, hi in ((0, n_ctx), (n_ctx, Tp)):
            du_ref[lo:hi, :] = _pool_stream_t(dpooled[lo:hi, :]).astype(BF16)

    return pl.pallas_call(
        body, name=name, grid=(B,), in_specs=[blk, blk, wsp, ssp], out_specs=[blk, wsp, ssp],
        out_shape=[_sds((T, POOL_WIDTH), BF16), _sds((POOL_WIDTH, POOL_WIDTH), F32), _sds((1, POOL_WIDTH), F32)],
        compiler_params=_params(("arbitrary",)),
    )(d_ob, pooled, w_bd, scale)


def _merge_specs(z, D, TR, tc, wa, wb, wc):
    def act(width):
        return pl.BlockSpec((TR, width), lambda i, n: (i, 0))

    def gate(part):
        return pl.BlockSpec((TR, tc), lambda i, n: (i, (GATE_COL + part * D) // tc + n))
    w_specs = [w.spec(w.shape[0], tc, lambda i, n: (0, n)) for w in (wa, wb, wc)]
    return [act(Q_WIDTH), act(POOL_WIDTH), act(Q_WIDTH), gate(0), gate(1), gate(2)] + w_specs


def _merge_fwd(oa, ob, oc, z, wa, wb, wc, *, D, TR, name):
    T = oa.shape[0]
    tc = D // N_CHIPS

    def body(oa_ref, ob_ref, oc_ref, ga_ref, gb_ref, gc_ref, wa_ref, wb_ref, wc_ref, y_ref):
        acc = jax.nn.sigmoid(ga_ref[...]) * jnp.dot(oa_ref[...], wa_ref[...], preferred_element_type=F32)
        acc += jax.nn.sigmoid(gb_ref[...]) * jnp.dot(ob_ref[...], wb_ref[...], preferred_element_type=F32)
        acc += jax.nn.sigmoid(gc_ref[...]) * jnp.dot(oc_ref[...], wc_ref[...], preferred_element_type=F32)
        y_ref[...] = acc.astype(BF16)

    return pl.pallas_call(
        body, name=name, grid=(T // TR, D // tc), in_specs=_merge_specs(z, D, TR, tc, wa, wb, wc),
        out_specs=pl.BlockSpec((TR, tc), lambda i, n: (i, n)), out_shape=_sds((T, D), BF16),
        compiler_params=_params(("parallel", "parallel")),
    )(oa, ob, oc, z, z, z, wa.arr, wb.arr, wc.arr)


def _merge_bwd(dy, oa, ob, oc, z, wa, wb, wc, *, D, TR, name):
    T = oa.shape[0]
    tc = D // N_CHIPS
    out = pl.BlockSpec((TR, tc), lambda i, n: (i, n))

    def body(dy_ref, oa_ref, ob_ref, oc_ref, ga_ref, gb_ref, gc_ref, wa_ref, wb_ref, wc_ref,
             dpa_ref, dpb_ref, dpc_ref, dga_ref, dgb_ref, dgc_ref):
        dyv = dy_ref[...]
        for o_ref, g_ref, w_ref, dp_ref, dg_ref in ((oa_ref, ga_ref, wa_ref, dpa_ref, dga_ref),
                                                    (ob_ref, gb_ref, wb_ref, dpb_ref, dgb_ref),
                                                    (oc_ref, gc_ref, wc_ref, dpc_ref, dgc_ref)):
            s = jax.nn.sigmoid(g_ref[...])
            proj = jnp.dot(o_ref[...], w_ref[...], preferred_element_type=F32)
            dp_ref[...] = (dyv * s).astype(BF16)
            dg_ref[...] = (dyv * proj * (s * (1.0 - s))).astype(BF16)

    return pl.pallas_call(
        body, name=name, grid=(T // TR, D // tc), in_specs=[out] + _merge_specs(z, D, TR, tc, wa, wb, wc),
        out_specs=[out] * 6, out_shape=[_sds((T, D), BF16)] * 6, compiler_params=_params(("parallel", "parallel")),
    )(dy, oa, ob, oc, z, z, z, wa.arr, wb.arr, wc.arr)


def _silu_rows(cc, name):
    def body(c_ref, s_ref):
        v = c_ref[...]
        s_ref[...] = (v * jax.nn.sigmoid(v)).astype(BF16)
    return pl.pallas_call(body, name=name, out_shape=_sds(cc.shape, BF16))(cc)


def _ada_bwd_rows(dm, ds, cc, name):
    def body(dm_ref, ds_ref, c_ref, db_ref, dc_ref):
        db_ref[...] = jnp.sum(dm_ref[...], axis=0, keepdims=True)
        v = c_ref[...]
        s = jax.nn.sigmoid(v)
        dc_ref[...] = ds_ref[...] * (s * (1.0 + v * (1.0 - s)))
    return pl.pallas_call(body, name=name, out_shape=[_sds((1, dm.shape[1]), F32), _sds(cc.shape, F32)])(dm, ds, cc)


def _row_tile(rows, cols):
    for t in (512, 256, 128, 64, 32, 16, 8):
        if rows % t == 0 and t * cols * 4 <= (1 << 20):
            return t
    return rows


def _working_rows(tr, C, worker):
    return pl.BlockSpec((tr, C), lambda i, c: (jnp.where(c[0] == worker, i, 0), 0))


def _add_landed(own, landed, core, worker, name):
    R, C = own.shape
    tr = _row_tile(R, C)
    row = _working_rows(tr, C, worker)

    def body(c_ref, a_ref, b_ref, o_ref, o16_ref):
        @pl.when(c_ref[0] == worker)
        def _():
            tot = a_ref[...] + b_ref[...].astype(F32)
            o_ref[...] = tot
            o16_ref[...] = tot.astype(BF16)

    grid_spec = pltpu.PrefetchScalarGridSpec(num_scalar_prefetch=1, grid=(R // tr,), in_specs=[row, row], out_specs=[row, row])
    return pl.pallas_call(body, name=name, grid_spec=grid_spec, out_shape=[_sds((R, C), F32), _sds((R, C), BF16)],
                          compiler_params=_params(("arbitrary",)))(core, own, landed)


def _sum_chips(own, landed, chip, core, worker, name):
    _, R, C = own.shape
    tr = _row_tile(R, C)

    def row(i, c):
        return jnp.where(c[0] == worker, i, 0)

    def body(k_ref, c_ref, a_ref, b_ref, o_ref):
        @pl.when(c_ref[0] == worker)
        def _():
            o_ref[...] = ((a_ref[...] + b_ref[0].astype(F32)) + b_ref[1].astype(F32)) + b_ref[2].astype(F32)

    grid_spec = pltpu.PrefetchScalarGridSpec(
        num_scalar_prefetch=2, grid=(R // tr,),
        in_specs=[pl.BlockSpec((None, tr, C), lambda i, k, c: (k[0], row(i, c), 0)),
                  pl.BlockSpec((3, tr, C), lambda i, k, c: (0, row(i, c), 0))],
        out_specs=pl.BlockSpec((tr, C), lambda i, k, c: (row(i, c), 0)))
    return pl.pallas_call(body, name=name, grid_spec=grid_spec, out_shape=_sds((R, C), F32),
                          compiler_params=_params(("arbitrary",)))(chip, core, own, landed)


def _adam_math(w, g, m, v):
    m = ADAM_B1 * m + (1.0 - ADAM_B1) * g
    v = ADAM_B2 * v + (1.0 - ADAM_B2) * (g * g)
    m_hat = m / (1.0 - ADAM_B1 ** ADAM_STEP)
    v_hat = v / (1.0 - ADAM_B2 ** ADAM_STEP)
    delta = -ADAM_LR * (m_hat / (jnp.sqrt(v_hat) + ADAM_EPS) + ADAM_WD * w)
    return delta, m, v


def _adamw(w, reduced, shared, m, v, core, worker, name):
    L, R, C = w.shape
    tr = _row_tile(R, C)

    def body(c_ref, w_ref, r0_ref, r1_ref, s0_ref, s1_ref, m_ref, v_ref, g_ref, d_ref, mo_ref, vo_ref):
        def step(g):
            d, mn, vn = _adam_math(w_ref[...], g, m_ref[...], v_ref[...])
            g_ref[...] = g
            d_ref[...] = d
            mo_ref[...] = mn
            vo_ref[...] = vn

        layer, here = pl.program_id(0), c_ref[0] == worker
        for l, (r_ref, s_ref) in enumerate(((r0_ref, s0_ref), (r1_ref, s1_ref))):
            @pl.when((layer == l) & here)
            def _(r_ref=r_ref):
                step(r_ref[...])

            @pl.when((layer == l) & jnp.logical_not(here))
            def _(s_ref=s_ref):
                step(s_ref[...])

    lay = pl.BlockSpec((None, tr, C), lambda l, i, c: (l, i, 0))
    row = pl.BlockSpec((tr, C), lambda l, i, c: (i, 0))
    grid_spec = pltpu.PrefetchScalarGridSpec(num_scalar_prefetch=1, grid=(L, R // tr),
                                             in_specs=[lay, row, row, row, row, lay, lay], out_specs=[lay] * 4)
    return pl.pallas_call(body, name=name, grid_spec=grid_spec, out_shape=[_sds((L, R, C), F32)] * 4,
                          compiler_params=_params(("parallel", "parallel")))(core, w, *reduced, *shared, m, v)


def _adamw_small(w, parts, m, v, name):
    R, C = w.shape

    def body(w_ref, p_ref, m_ref, v_ref, g_ref, d_ref, mo_ref, vo_ref):
        g = p_ref[0]
        for dev in range(1, 8):
            g = g + p_ref[dev]
        d, mn, vn = _adam_math(w_ref[...], g, m_ref[...], v_ref[...])
        g_ref[...] = g
        d_ref[...] = d
        mo_ref[...] = mn
        vo_ref[...] = vn

    return pl.pallas_call(body, name=name, out_shape=[_sds((R, C), F32)] * 4)(w, parts, m, v)


def _place():
    return lax.axis_index("x"), lax.axis_index("y"), lax.axis_index("c")


def _other_chips(x, y):
    return [(1 - x, y), (x, 1 - y), (1 - x, 1 - y)]


def _rcopy(src, dst, ssem, rsem, dev):
    return pltpu.make_async_remote_copy(src_ref=src, dst_ref=dst, send_sem=ssem, recv_sem=rsem,
                                        device_id=dev, device_id_type=MESH)


GATHER_SEMS = 7


class _LayerGather:
    def __init__(self, shards, layer):
        self.inputs, self.layer, self.n = list(shards), layer, len(shards)
        load, self.groups = [0, 0], ([], [])
        for w in sorted(range(self.n), key=lambda w: -shards[w][0].size):
            g = 0 if load[0] <= load[1] else 1
            self.groups[g].append(w)
            load[g] += shards[w][0].size
        self.out_shape = [_sds((N_CHIPS,) + s.shape[1:], s.dtype) for s in shards]
        self.scratch = [pltpu.SemaphoreType.DMA((self.n, GATHER_SEMS)), pltpu.SemaphoreType.DMA((self.n, GATHER_SEMS))]

    def _own(self, src, out, send_sems, recv_sems):
        x, y, c = _place()
        return [_rcopy(src[w].at[self.layer], out[w].at[2 * x + y], send_sems.at[w, 6], recv_sems.at[w, 6], (x, y, 1 - c))
                for w in range(self.n)]

    def _to_chips(self, g, src, out, send_sems, recv_sems):
        x, y, c = _place()
        return [_rcopy(src[w].at[self.layer], out[w].at[2 * x + y], send_sems.at[w, j], recv_sems.at[w, j], (*chip, c))
                for w in self.groups[g] for j, chip in enumerate(_other_chips(x, y))]

    def start(self, src, out, send_sems, recv_sems):
        c = lax.axis_index("c")
        for cp in self._own(src, out, send_sems, recv_sems):
            cp.start()
        for g in (0, 1):
            @pl.when(c == g)
            def _(g=g):
                for cp in self._to_chips(g, src, out, send_sems, recv_sems):
                    cp.start()

    def finish(self, src, out, send_sems, recv_sems):
        x, y, c = _place()
        sibling = (x, y, 1 - c)
        chips = _other_chips(x, y)
        for g in (0, 1):
            @pl.when(c == g)
            def _(g=g):
                passed = []
                for w in self.groups[g]:
                    for j, (px, py) in enumerate(chips):
                        landed = out[w].at[2 * px + py]
                        _rcopy(landed, landed, send_sems.at[w, j], recv_sems.at[w, j], (px, py, c)).wait_recv()
                        cp = _rcopy(landed, landed, send_sems.at[w, 3 + j], recv_sems.at[w, 3 + j], sibling)
                        cp.start()
                        passed.append(cp)
                for w in self.groups[1 - g]:
                    for j, (px, py) in enumerate(chips):
                        landed = out[w].at[2 * px + py]
                        _rcopy(landed, landed, send_sems.at[w, 3 + j], recv_sems.at[w, 3 + j], sibling).wait_recv()
                for cp in self._to_chips(g, src, out, send_sems, recv_sems) + passed:
                    cp.wait_send()
        for cp in self._own(src, out, send_sems, recv_sems):
            cp.wait_recv()
            cp.wait_send()


def _on_core(fn):
    for g in (0, 1):
        @pl.when(lax.axis_index("c") == g)
        def _(g=g):
            fn(g)


class _ToSibling:
    def __init__(self, arrays, senders):
        self.inputs, self.senders = list(arrays), list(senders)
        n = len(self.inputs)
        self.out_shape = [_sds(a.shape, a.dtype) for a in self.inputs]
        self.scratch = [pltpu.SemaphoreType.DMA((n,)), pltpu.SemaphoreType.DMA((n,))]

    def _copies(self, sender, src, out, send_sems, recv_sems):
        x, y, c = _place()
        return [_rcopy(src[w], out[w], send_sems.at[w], recv_sems.at[w], (x, y, 1 - c))
                for w in range(len(src)) if self.senders[w] == sender]

    def start(self, *refs):
        def go(g):
            for cp in self._copies(g, *refs):
                cp.start()
        _on_core(go)

    def finish(self, *refs):
        def go(g):
            for cp in self._copies(1 - g, *refs):
                cp.wait_recv()
            for cp in self._copies(g, *refs):
                cp.wait_send()
        _on_core(go)


class _ChipSend:
    def __init__(self, blocked, senders):
        self.inputs, self.senders = list(blocked), list(senders)
        n = len(self.inputs)
        self.out_shape = [_sds((3,) + a.shape[1:], a.dtype) for a in self.inputs]
        self.scratch = [pltpu.SemaphoreType.DMA((n, 3)), pltpu.SemaphoreType.DMA((n, 3))]

    def _copies(self, sender, src, out, send_sems, recv_sems):
        x, y, c = _place()
        return [_rcopy(src[w].at[2 * px + py], out[w].at[j], send_sems.at[w, j], recv_sems.at[w, j], (px, py, c))
                for w in range(len(src)) if self.senders[w] == sender for j, (px, py) in enumerate(_other_chips(x, y))]

    def start(self, *refs):
        def go(g):
            for cp in self._copies(g, *refs):
                cp.start()
        _on_core(go)

    def finish(self, *refs):
        def go(g):
            cps = self._copies(g, *refs)
            for cp in cps:
                cp.wait_recv()
            for cp in cps:
                cp.wait_send()
        _on_core(go)


def _ride_alone(rider, name):
    n_in, n_out = len(rider.inputs), len(rider.out_shape)

    def body(*refs):
        args = (refs[:n_in], refs[n_in:n_in + n_out]) + tuple(refs[n_in + n_out:])
        rider.start(*args)
        rider.finish(*args)

    return pl.pallas_call(body, name=name, in_specs=[ANY] * n_in, out_specs=[ANY] * n_out, out_shape=rider.out_shape,
                          scratch_shapes=rider.scratch)(*rider.inputs)


def _hitch(rider, ins, in_specs, out_specs, out_shape, scratch):
    if rider is None:
        return ins, in_specs, out_specs, out_shape, scratch
    return (list(ins) + rider.inputs, list(in_specs) + [ANY] * len(rider.inputs),
            list(out_specs) + [ANY] * len(rider.out_shape), list(out_shape) + rider.out_shape, list(scratch) + rider.scratch)


def _ride(rider, refs, n_in, n_out, n_scratch, grid):
    if rider is None:
        return list(refs), lambda: None
    r_in, r_out = len(rider.inputs), len(rider.out_shape)
    refs = list(refs)
    own_in, ride_in = refs[:n_in], refs[n_in:n_in + r_in]
    rest = refs[n_in + r_in:]
    own_out, ride_out = rest[:n_out], rest[n_out:n_out + r_out]
    rest = rest[n_out + r_out:]
    own_scratch, sems = rest[:n_scratch], rest[n_scratch:]
    ids = [pl.program_id(a) for a in range(len(grid))]
    first = functools.reduce(jnp.logical_and, [i == 0 for i in ids])
    last = functools.reduce(jnp.logical_and, [i == g - 1 for i, g in zip(ids, grid)])

    @pl.when(first)
    def _():
        rider.start(ride_in, ride_out, *sems)

    def finish():
        @pl.when(last)
        def _():
            rider.finish(ride_in, ride_out, *sems)

    return own_in + own_out + own_scratch, finish


def _gather_small(block, name):
    m_per, n = block.shape

    def body(x_ref, out_ref, send_sems, recv_sems, local_sem):
        x, y, c = _place()
        me, sibling = (x, y, c), (x, y, 1 - c)
        chips = _other_chips(x, y)

        def rows(px, py, pc):
            return out_ref.at[pl.ds((4 * px + 2 * py + pc) * m_per, m_per), :]

        def copy(k, blk, to, src=None):
            return _rcopy(rows(*blk) if src is None else src, rows(*blk), send_sems.at[k], recv_sems.at[k], to)

        mine = pltpu.make_async_copy(x_ref, rows(*me), local_sem)
        mine.start()
        first = [copy(0, me, sibling, src=x_ref)]
        first += [copy(1 + j, me, (*chip, c), src=x_ref) for j, chip in enumerate(chips)]
        for cp in first:
            cp.start()
        passed = [copy(4 + j, (*chip, c), sibling) for j, chip in enumerate(chips)]
        for j, chip in enumerate(chips):
            copy(1 + j, (*chip, c), me).wait_recv()
            passed[j].start()
        copy(0, sibling, me).wait_recv()
        for j, chip in enumerate(chips):
            copy(4 + j, (*chip, 1 - c), me).wait_recv()
        for cp in first + passed:
            cp.wait_send()
        mine.wait()

    return pl.pallas_call(
        body, name=name, out_shape=_sds((8 * m_per, n), block.dtype),
        in_specs=[pl.BlockSpec(memory_space=pltpu.VMEM)], out_specs=pl.BlockSpec(memory_space=pltpu.VMEM),
        scratch_shapes=[pltpu.SemaphoreType.DMA((7,)), pltpu.SemaphoreType.DMA((7,)), pltpu.SemaphoreType.DMA],
    )(block)


def _rope_tables(n_ctx, seq):
    rows = seq // GRID_W
    r = jnp.repeat(jnp.arange(rows, dtype=F32), GRID_W)
    col = jnp.tile(jnp.arange(GRID_W, dtype=F32), rows)
    inv = 1.0 / (ROPE_THETA ** (jnp.arange(0, AXIS_DIM, 2, dtype=F32) / AXIS_DIM))
    ang = jnp.concatenate([r[:, None] * inv, col[:, None] * inv], axis=-1)
    cos = jnp.repeat(jnp.cos(ang), 2, axis=-1)
    sin = jnp.repeat(jnp.sin(ang), 2, axis=-1) * jnp.tile(jnp.array([-1.0, 1.0], F32), HEAD_DIM // 2)
    cos = jnp.concatenate([jnp.ones((n_ctx, HEAD_DIM), F32), cos], axis=0)
    sin = jnp.concatenate([jnp.zeros((n_ctx, HEAD_DIM), F32), sin], axis=0)
    return jnp.tile(cos, (1, 2)), jnp.tile(sin, (1, 2))


def _block_diag(w_pool):
    L, G = w_pool.shape[:2]
    eye = jnp.eye(G, dtype=w_pool.dtype)
    return (w_pool[:, :, :, None, :] * eye[None, :, None, :, None]).reshape(L, POOL_WIDTH, POOL_WIDTH)


def _qk_gains(small):
    qn = jnp.stack([small["q_norm_a"], small["q_norm_c"]], axis=1)[:, :, None, :]
    kn = jnp.stack([small["k_norm_a"], small["k_norm_c"]], axis=1)[:, :, None, :]
    L = qn.shape[0]
    rows = jnp.concatenate([jnp.broadcast_to(qn, (L, 2, N_HEADS, HEAD_DIM)), jnp.broadcast_to(kn, (L, 2, N_KV, HEAD_DIM)),
                            jnp.ones((L, 2, N_KV, HEAD_DIM), F32)], axis=2)
    return rows.reshape(L, 2, 1, QKV_WIDTH)


def _local_step(x, c, ctx, c_ctx, small, gw, target, rider=None, overlap=False):
    gw = list(gw)
    B, S, D = x.shape
    N = ctx.shape[1]
    L = small["norm1"].shape[0]
    Tp = N + S
    T = B * Tp
    TR = N
    P = Tp // N
    rows16 = 16
    assert N % Q_BLOCK == 0 and S % N == 0 and B + 1 <= rows16
    TM = _tile(T, (1536, 1024, 768, 512, 384, 256, 128))
    TMG = _tile(T, (1024, 768, 512, 384, 256, 128))

    X = jnp.concatenate([ctx, x], axis=1).reshape(T, D)
    cc = jnp.concatenate([c, c_ctx[None], jnp.zeros((rows16 - B - 1, D), F32)], axis=0)
    s_rows = _silu_rows(cc, "silu_rows")
    cos, sin = _rope_tables(N, S)
    all_gains = _qk_gains(small)
    all_w_bd = _block_diag(small["w_pool"]).astype(BF16)

    def weights(l):
        g = gw[l]
        return dict(
            ada=_Opnd(g["w_ada"], "bcols"), w_in=_Opnd(g["w_in"], "bcols"),
            a=_Opnd(g["w_br_a"], "bcols"), b=_Opnd(g["w_br_b"], "bcols"), c=_Opnd(g["w_br_c"], "bcols"),
            out=_Opnd(g["w_out"], "brows"), mlp1=_Opnd(g["w_mlp1"], "bcols"), mlp2=_Opnd(g["w_mlp2"], "brows"))

    IN = weights(0)["w_in"].shape[1]
    DFF = weights(0)["mlp1"].shape[1]
    tn_in = _tile(IN // N_CHIPS, (1152, 768, 512, 384, 256, 128))
    tn_ff = _tile(DFF // N_CHIPS, (1024, 512, 256, 128))
    tn_ada = _tile(6 * D // N_CHIPS, (1536, 768, 512, 256, 128))
    tn_d = D // N_CHIPS
    tk_d = _tile(D, (512,))
    tk_tok = _tile(T, (2304, 1536, 1024, 768, 512, 384, 256))

    saved = []
    xin, pending = X, None
    for l in range(L):
        W = weights(l)
        b_ada = small["b_ada"][l].reshape(1, 6 * D)
        mod = _matmul(s_rows, W["ada"], "nn", tm=rows16, tn=tn_ada, tk=D, name=f"ada_fwd{l}",
                      epilogue=lambda acc, b: (acc + b,), extras=[(b_ada, (1, tn_ada), lambda m, n: (0, n))])
        modtab = mod.reshape(rows16, 1, 6 * D)
        gains = all_gains[l]
        w_bd = all_w_bd[l]
        p_scale = small["pool_scale"][l].reshape(1, POOL_WIDTH)
        sink = small["sink_c"][l]

        x0, h1 = _res_norm(xin, pending, modtab, 0, 1, small["norm1"][l][None], TR=TR, P=P, name=f"norm1_fwd{l}")
        z = _matmul(h1, W["w_in"], "nn", tm=TM, tn=tn_in, tk=D, name=f"in_proj{l}")
        q2, k2, v2 = _qk_prep(z, gains, cos, sin, TR=TR, P=P, name=f"qk_prep{l}")
        riding = rider if l == 0 else None
        oa, oa32, lse_a, *landed = _attn_fwd(q2, k2, v2, None, branch=0, B=B, n_ctx=N, window=False,
                                             name=f"attn_a_fwd{l}", rider=riding)
        if riding is not None:
            gw[riding.layer] = dict(zip(BIG_NAMES, landed))
        oc, oc32, lse_c = _attn_fwd(q2, k2, v2, sink, branch=1, B=B, n_ctx=N, window=True, name=f"attn_c_fwd{l}")
        pooled, ob = _pool_fwd(z, w_bd, p_scale, B=B, Tp=Tp, n_ctx=N, name=f"pool_fwd{l}")
        y = _merge_fwd(oa, ob, oc, z, W["a"], W["b"], W["c"], D=D, TR=TMG, name=f"merge_fwd{l}")
        ao = _matmul(y, W["out"], "nn", tm=TM, tn=D, tk=tn_d, name=f"out_proj{l}")
        x1, h2 = _res_norm(x0, (ao, modtab, 2), modtab, 3, 4, small["norm2"][l][None], TR=TR, P=P, name=f"norm2_fwd{l}")
        a_pre, r_act = _matmul(h2, W["mlp1"], "nn", tm=TM, tn=tn_ff, tk=D, name=f"mlp1_fwd{l}", out_dtypes=(F32, BF16),
                               epilogue=lambda acc: (acc, jnp.square(jnp.maximum(acc, 0.0))))
        mo = _matmul(r_act, W["mlp2"], "nn", tm=TM, tn=D, tk=tn_ff, name=f"mlp2_fwd{l}")
        saved.append(dict(modtab=modtab, gains=gains, w_bd=w_bd, p_scale=p_scale, sink=sink, x0=x0, h1=h1, z=z,
                          q2=q2, k2=k2, v2=v2, oa=oa, ob=ob, oc=oc, oa32=oa32, oc32=oc32, lse_a=lse_a, lse_c=lse_c,
                          pooled=pooled, y=y, ao=ao,
                          x1=x1, h2=h2, a_pre=a_pre, r_act=r_act, mo=mo))
        xin, pending = x1, (mo, modtab, 5)

    dxo, loss, d_mo, dg2 = _loss_head(xin, pending[0], pending[1], 5, target.reshape(B * S, D), TR=TR, P=P, name="loss_head")

    big = {k: [None] * L for k in BIG_NAMES}
    big16 = {k: [None] * L for k in BIG_NAMES}
    sm = {k: [None] * L for k in ("b_ada", "norm1", "norm2", "q_norm_a", "k_norm_a", "q_norm_c", "k_norm_c",
                                   "sink_c", "w_pool", "pool_scale")}

    def dw(key, l, a, b, *, tm, tn, name, tk=tk_tok, blocked=True):
        outs = _matmul(a, b, "tn", tm=tm, tn=tn, tk=tk, name=name, out_dtypes=(F32, BF16), out_blocked=blocked)
        if not blocked:
            outs = [o.reshape(N_CHIPS, o.shape[0] // N_CHIPS, o.shape[1]) for o in outs]
        big[key][l], big16[key][l] = outs
    d_cctx = jnp.zeros((D,), F32)
    for l in reversed(range(L)):
        W, sv = weights(l), saved[l]
        modtab = sv["modtab"]
        ride_now = overlap and l == L - 2
        if ride_now:
            early = _LayerReduce(l + 1, [big[k][l + 1] for k in BIG_NAMES], [big16[k][l + 1] for k in BIG_NAMES])
        d_a = _matmul(d_mo, W["mlp2"], "nt", tm=TM, tn=tn_ff, tk=D, name=f"mlp2_bwd{l}", out_dtypes=(BF16,),
                      epilogue=lambda acc, a: (acc * (2.0 * jnp.maximum(a, 0.0)),),
                      extras=[(sv["a_pre"], (TM, tn_ff), lambda m, n: (m, n))], rider=early.to_worker if ride_now else None)
        if ride_now:
            d_a, landed = d_a
            early_send = early.add(landed)
        dw("w_mlp2", l, sv["r_act"], d_mo, tm=tk_d, tn=D, name=f"mlp2_dw{l}", blocked=False)
        d_h2 = _matmul(d_a, W["mlp1"], "nt", tm=TM, tn=D, tk=tn_ff, name=f"mlp1_bwd{l}")
        dw("w_mlp1", l, sv["h2"], d_a, tm=tk_d, tn=tn_ff, name=f"mlp1_dw{l}")
        dx1, dsh2, dsc2, dn2, d_ao, dg1 = _norm_bwd(sv["x1"], d_h2, dxo, modtab, 4, small["norm2"][l][None],
                                                    (sv["ao"], modtab, 2), TR=TR, P=P, name=f"norm2_bwd{l}")
        d_y = _matmul(d_ao, W["out"], "nt", tm=TM, tn=tn_d, tk=D, name=f"out_bwd{l}")
        dw("w_out", l, sv["y"], d_ao, tm=tk_d, tn=D, name=f"out_dw{l}", blocked=False)
        d_pa, d_pb, d_pc, d_ga, d_gb, d_gc = _merge_bwd(d_y, sv["oa"], sv["ob"], sv["oc"], sv["z"], W["a"], W["b"], W["c"],
                                                        D=D, TR=TMG, name=f"merge_bwd{l}")
        d_oa = _matmul(d_pa, W["a"], "nt", tm=TM, tn=Q_WIDTH, tk=tn_d, name=f"br_a_bwd{l}", out_dtypes=(BF16,))
        d_ob = _matmul(d_pb, W["b"], "nt", tm=TM, tn=POOL_WIDTH, tk=tn_d, name=f"br_b_bwd{l}")
        d_oc = _matmul(d_pc, W["c"], "nt", tm=TM, tn=Q_WIDTH, tk=tn_d, name=f"br_c_bwd{l}", out_dtypes=(BF16,))
        dw("w_br_a", l, sv["oa"], d_pa, tm=Q_WIDTH, tn=tn_d, name=f"br_a_dw{l}")
        dw("w_br_b", l, sv["ob"], d_pb, tm=POOL_WIDTH, tn=tn_d, name=f"br_b_dw{l}")
        dw("w_br_c", l, sv["oc"], d_pc, tm=Q_WIDTH, tn=tn_d, name=f"br_c_dw{l}")
        d_u, d_wbd, d_ps = _pool_bwd(d_ob, sv["pooled"], sv["w_bd"], sv["p_scale"], B=B, Tp=Tp, n_ctx=N, name=f"pool_bwd{l}")
        dqa, dka, dva, *arrived = _attn_bwd(sv["q2"], sv["k2"], sv["v2"], d_oa, sv["oa32"], sv["lse_a"], None, branch=0,
                                            B=B, n_ctx=N, window=False, name=f"attn_a_bwd{l}",
                                            rider=early_send if ride_now else None)
        if ride_now:
            early.from_chips = arrived
        dqc, dkc, dvc, dsink = _attn_bwd(sv["q2"], sv["k2"], sv["v2"], d_oc, sv["oc32"], sv["lse_c"], sv["sink"],
                                         branch=1, B=B, n_ctx=N, window=True, name=f"attn_c_bwd{l}")
        dz_a, dgains_a = _qk_prep_bwd(sv["z"], dqa, dka, dva, sv["gains"], cos, sin, branch=0, TR=TR, P=P,
                                      name=f"qk_prep_a_bwd{l}")
        dz_c, dgains_c = _qk_prep_bwd(sv["z"], dqc, dkc, dvc, sv["gains"], cos, sin, branch=1, TR=TR, P=P,
                                      name=f"qk_prep_c_bwd{l}")
        dz = jnp.concatenate([dz_a, dz_c, d_u, d_ga, d_gb, d_gc], axis=1)
        d_h1 = _matmul(dz, W["w_in"], "nt", tm=TM, tn=D, tk=tn_in, name=f"in_bwd{l}")
        dw("w_in", l, sv["h1"], dz, tm=tk_d, tn=tn_in, name=f"in_dw{l}")
        below = (saved[l - 1]["mo"], saved[l - 1]["modtab"], 5) if l > 0 else None
        dx0, dsh1, dsc1, dn1, *lower = _norm_bwd(sv["x0"], d_h1, dx1, modtab, 1, small["norm1"][l][None], below,
                                                 TR=TR, P=P, name=f"norm1_bwd{l}")
        this_dg2 = dg2
        if l > 0:
            d_mo, dg2 = lower

        dm_groups = jnp.concatenate([dsh1, dsc1, dg1, dsh2, dsc2, this_dg2], axis=-1).reshape(B, 2, 6 * D)
        dm = jnp.concatenate([dm_groups[:, 1], jnp.sum(dm_groups[:, 0], axis=0, keepdims=True),
                              jnp.zeros((rows16 - B - 1, 6 * D), F32)], axis=0)
        dm_bf = dm.astype(BF16)
        d_s = _matmul(dm_bf, W["ada"], "nt", tm=rows16, tn=D, tk=tn_ada, name=f"ada_bwd{l}")
        dw("w_ada", l, s_rows, dm_bf, tm=tk_d, tn=tn_ada, tk=rows16, name=f"ada_dw{l}")
        db_ada, dcc = _ada_bwd_rows(dm, d_s, cc, f"ada_rows_bwd{l}")
        d_cctx = d_cctx + dcc[B]

        sm["b_ada"][l] = db_ada[0]
        sm["norm1"][l] = jnp.sum(dn1, axis=(0, 1))
        sm["norm2"][l] = jnp.sum(dn2, axis=(0, 1))
        dgh = jnp.stack([dgains_a, dgains_c]).reshape(2, QKV_WIDTH // HEAD_DIM, HEAD_DIM)
        sm["q_norm_a"][l] = jnp.sum(dgh[0, :N_HEADS], axis=0)
        sm["k_norm_a"][l] = jnp.sum(dgh[0, N_HEADS:N_HEADS + N_KV], axis=0)
        sm["q_norm_c"][l] = jnp.sum(dgh[1, :N_HEADS], axis=0)
        sm["k_norm_c"][l] = jnp.sum(dgh[1, N_HEADS:N_HEADS + N_KV], axis=0)
        sm["sink_c"][l] = jnp.sum(dsink[:, :N_HEADS, 0], axis=0)
        sm["w_pool"][l] = jnp.stack([d_wbd[g * POOL_CH:(g + 1) * POOL_CH, g * POOL_CH:(g + 1) * POOL_CH]
                                     for g in range(POOL_WIDTH // POOL_CH)])
        sm["pool_scale"][l] = d_ps[0]
        dxo = dx0

    grad_x = dxo.reshape(B, Tp, D)[:, N:]
    small_grads = {k: jnp.stack(v) for k, v in sm.items()}
    small_grads["c_ctx"] = d_cctx
    return loss, grad_x, small_grads, big, big16, (early if overlap else None)


SMALL_NAMES = ("c_ctx", "b_ada", "norm1", "norm2", "q_norm_a", "k_norm_a", "q_norm_c", "k_norm_c", "sink_c",
               "w_pool", "pool_scale")
BIG_NAMES = ("w_ada", "w_in", "w_br_a", "w_br_b", "w_br_c", "w_out", "w_mlp1", "w_mlp2")
WEIGHT_NAMES = ("c_ctx", "w_ada", "b_ada", "norm1", "norm2", "w_in", "q_norm_a", "k_norm_a", "q_norm_c", "k_norm_c",
                "sink_c", "w_pool", "pool_scale", "w_br_a", "w_br_b", "w_br_c", "w_out", "w_mlp1", "w_mlp2")


def _pack(parts, rows):
    flat = jnp.concatenate([p.reshape(-1).astype(F32) for p in parts])
    return jnp.pad(flat, (0, rows * LANES - flat.shape[0])).reshape(rows, LANES)


def _unpack(packed, like):
    flat, out, at = packed.reshape(-1), [], 0
    for p in like:
        out.append(flat[at:at + p.size].reshape(p.shape))
        at += p.size
    return out


def _split_by_bytes(arrays):
    load, owner = [0, 0], [0] * len(arrays)
    for w in sorted(range(len(arrays)), key=lambda w: -arrays[w].size):
        owner[w] = 0 if load[0] <= load[1] else 1
        load[owner[w]] += arrays[w].size
    return owner


class _LayerReduce:
    def __init__(self, layer, partials, partials16):
        self.layer, self.partials = layer, list(partials)
        self.workers = _split_by_bytes(self.partials)
        self.to_worker = _ToSibling([g.reshape(-1, g.shape[-1]) for g in partials16], [1 - wk for wk in self.workers])
        x, y, c = _place()
        self.core = c.astype(jnp.int32).reshape(1)
        self.chip = (2 * x + y).astype(jnp.int32).reshape(1)

    def add(self, landed):
        sums = [_add_landed(g.reshape(-1, g.shape[-1]), r, self.core, wk, f"grads{self.layer}_add_sibling_{k}")
                for k, g, r, wk in zip(BIG_NAMES, self.partials, landed, self.workers)]
        self.in_chip = [h.reshape(g.shape) for g, (h, _) in zip(self.partials, sums)]
        return _ChipSend([h.reshape(g.shape) for g, (_, h) in zip(self.partials, sums)], self.workers)

    def sum(self, from_chips):
        return [_sum_chips(h, r, self.chip, self.core, wk, f"grads{self.layer}_sum_chips_{k}")
                for k, h, r, wk in zip(BIG_NAMES, self.in_chip, from_chips, self.workers)]


def kernel(x, c, ctx, c_ctx, w_ada, b_ada, norm1, norm2, w_in, q_norm_a, k_norm_a, q_norm_c, k_norm_c, sink_c, w_pool, pool_scale, w_br_a, w_br_b, w_br_c, w_out, w_mlp1, w_mlp2, loss_target, m_c_ctx, m_w_ada, m_b_ada, m_norm1, m_norm2, m_w_in, m_q_norm_a, m_k_norm_a, m_q_norm_c, m_k_norm_c, m_sink_c, m_w_pool, m_pool_scale, m_w_br_a, m_w_br_b, m_w_br_c, m_w_out, m_w_mlp1, m_w_mlp2, v_c_ctx, v_w_ada, v_b_ada, v_norm1, v_norm2, v_w_in, v_q_norm_a, v_k_norm_a, v_q_norm_c, v_k_norm_c, v_sink_c, v_w_pool, v_pool_scale, v_w_br_a, v_w_br_b, v_w_br_c, v_w_out, v_w_mlp1, v_w_mlp2):
    given = dict(locals())
    w = {k: given[k] for k in WEIGHT_NAMES}
    m = {k: given["m_" + k] for k in WEIGHT_NAMES}
    v = {k: given["v_" + k] for k in WEIGHT_NAMES}

    shards = [w[k].astype(BF16) for k in BIG_NAMES]
    assert all(s.shape[0] == 2 for s in shards)
    first_layer = dict(zip(BIG_NAMES, _ride_alone(_LayerGather(shards, 0), "gather_weights0")))
    small = {k: w[k] for k in SMALL_NAMES}
    loss_part, grad_x, small_grads, big_grads, big_grads16, early = _local_step(
        x, c, ctx, c_ctx, small, [first_layer, None], loss_target, rider=_LayerGather(shards, 1), overlap=True)

    late = _LayerReduce(0, [big_grads[k][0] for k in BIG_NAMES], [big_grads16[k][0] for k in BIG_NAMES])
    send = late.add(_ride_alone(late.to_worker, "grads0_to_sibling"))
    reduced = [late.sum(_ride_alone(send, "grads0_to_chips")), early.sum(early.from_chips)]
    n_big = len(BIG_NAMES)
    shared = _ride_alone(_ToSibling(reduced[0] + reduced[1], late.workers + early.workers), "grads_share")
    grads, deltas, new_m, new_v = {}, {}, {}, {}
    for i, k in enumerate(BIG_NAMES):
        assert late.workers[i] == early.workers[i]
        grads[k], deltas[k], new_m[k], new_v[k] = _adamw(
            w[k], (reduced[0][i], reduced[1][i]), (shared[i], shared[n_big + i]), m[k], v[k], late.core, late.workers[i],
            f"adamw_{k}")

    sizes = sum(w[k].size for k in SMALL_NAMES) + LANES
    rows = -(-sizes // (8 * LANES)) * 8
    parts = _gather_small(_pack([small_grads[k] for k in SMALL_NAMES] + [loss_part[0]], rows), "gather_small")
    zero = jnp.zeros((LANES,), F32)
    packed = [_pack([t[k] for k in SMALL_NAMES] + [zero], rows) for t in (w, m, v)]
    outs = _adamw_small(packed[0], parts.reshape(8, rows, LANES), packed[1], packed[2], "adamw_small")
    like = [w[k] for k in SMALL_NAMES] + [zero]
    for store, packed_out in zip((grads, deltas, new_m, new_v), outs):
        pieces = _unpack(packed_out, like)
        for k, piece in zip(SMALL_NAMES, pieces):
            store[k] = piece
        if store is grads:
            loss = pieces[-1][0]

    return (loss, grad_x, *[grads[k] for k in WEIGHT_NAMES], *[deltas[k] for k in WEIGHT_NAMES],
            *[new_m[k] for k in WEIGHT_NAMES], *[new_v[k] for k in WEIGHT_NAMES])
```

```python
import functools

import jax
import jax.numpy as jnp
from jax import lax
from jax.experimental import pallas as pl
from jax.experimental.pallas import tpu as pltpu

F32 = jnp.float32
BF16 = jnp.bfloat16

HEAD_DIM = 64
GRID_W = 64
AXIS_DIM = HEAD_DIM // 2
ROPE_THETA = 10000.0
N_HEADS = 6
N_KV = 2
N_GROUP = N_HEADS // N_KV
POOL_CH = 64
POOL_WIDTH = 256
POOL_WINDOWS = (2, 4, 8, 16)
WINDOW = 128
Q_BLOCK = 128
Q_WIDTH = N_HEADS * HEAD_DIM
KV_WIDTH = N_KV * HEAD_DIM
GATE_COL = 2 * (Q_WIDTH + 2 * KV_WIDTH) + POOL_WIDTH
U_COL = 2 * (Q_WIDTH + 2 * KV_WIDTH)
EPS = 1e-6
NEG = -1e30
ADAM_LR = 0.001
ADAM_B1 = 0.9
ADAM_B2 = 0.999
ADAM_EPS = 1e-08
ADAM_WD = 0.01
ADAM_STEP = 10

N_CHIPS = 4
LANES = 128
POOL_PAD = 16
VMEM_LIMIT = 48 * 1024 * 1024
MESH = pl.DeviceIdType.MESH
ANY = pl.BlockSpec(memory_space=pl.ANY)


def _params(sem):
    return pltpu.CompilerParams(dimension_semantics=sem, vmem_limit_bytes=VMEM_LIMIT)


def _sds(shape, dtype):
    return jax.ShapeDtypeStruct(tuple(shape), dtype)


class _Opnd:
    def __init__(self, arr, kind="plain"):
        self.arr, self.kind = arr, kind

    @property
    def shape(self):
        a = self.arr
        if self.kind == "plain":
            return a.shape
        if self.kind == "bcols":
            return (a.shape[1], N_CHIPS * a.shape[2])
        return (N_CHIPS * a.shape[1], a.shape[2])

    def spec(self, tr, tc, fn):
        a = self.arr
        if self.kind == "plain":
            return pl.BlockSpec((tr, tc), lambda *g: fn(*g))
        if self.kind == "bcols":
            assert a.shape[2] % tc == 0, (a.shape, tc)
            per = a.shape[2] // tc

            def im(*g):
                ri, ci = fn(*g)
                return (ci // per, ri, ci % per)
            return pl.BlockSpec((None, tr, tc), im)
        assert a.shape[1] % tr == 0, (a.shape, tr)
        per = a.shape[1] // tr

        def im(*g):
            ri, ci = fn(*g)
            return (ri // per, ri % per, ci)
        return pl.BlockSpec((None, tr, tc), im)


def _matmul(a, b, mode, *, tm, tn, tk, name, out_dtypes=(F32,), epilogue=None, extras=(), out_blocked=False, rider=None):
    if not isinstance(a, _Opnd):
        a = _Opnd(a)
    if not isinstance(b, _Opnd):
        b = _Opnd(b)
    if mode == "nn":
        (M, K), (K2, N) = a.shape, b.shape
        a_spec = a.spec(tm, tk, lambda m, n, k: (m, k))
        b_spec = b.spec(tk, tn, lambda m, n, k: (k, n))
        dims = (((1,), (0,)), ((), ()))
    elif mode == "nt":
        (M, K), (N, K2) = a.shape, b.shape
        a_spec = a.spec(tm, tk, lambda m, n, k: (m, k))
        b_spec = b.spec(tn, tk, lambda m, n, k: (n, k))
        dims = (((1,), (1,)), ((), ()))
    else:
        (K, M), (K2, N) = a.shape, b.shape
        a_spec = a.spec(tk, tm, lambda m, n, k: (k, m))
        b_spec = b.spec(tk, tn, lambda m, n, k: (k, n))
        dims = (((0,), (0,)), ((), ()))
    assert K == K2 and M % tm == 0 and N % tn == 0 and K % tk == 0, (name, M, N, K, K2, tm, tn, tk)
    nk = K // tk
    n_extra = len(extras)
    n_out = len(out_dtypes)
    extra_specs = [pl.BlockSpec(bs, functools.partial(lambda m, n, k, f: f(m, n), f=f)) for (_, bs, f) in extras]
    if out_blocked:
        assert (N // N_CHIPS) % tn == 0
        per = (N // N_CHIPS) // tn
        out_shape = [_sds((N_CHIPS, M, N // N_CHIPS), dt) for dt in out_dtypes]
        out_specs = [pl.BlockSpec((None, tm, tn), lambda m, n, k: (n // per, m, n % per)) for _ in out_dtypes]
    else:
        out_shape = [_sds((M, N), dt) for dt in out_dtypes]
        out_specs = [pl.BlockSpec((tm, tn), lambda m, n, k: (m, n)) for _ in out_dtypes]

    in_place = nk > 1 and epilogue is None and out_dtypes[0] == F32

    grid = (M // tm, N // tn, nk)
    own_scratch = [pltpu.VMEM((tm, tn), F32)] if nk > 1 and not in_place else []

    def body(*refs):
        refs, finish_ride = _ride(rider, refs, 2 + n_extra, n_out, len(own_scratch), grid)
        a_ref, b_ref = refs[0], refs[1]
        extra_refs = refs[2:2 + n_extra]
        out_refs = refs[2 + n_extra:2 + n_extra + n_out]
        acc_ref = out_refs[0] if in_place else (refs[2 + n_extra + n_out] if nk > 1 else None)
        k = pl.program_id(2)
        prod = lax.dot_general(a_ref[...].astype(BF16), b_ref[...].astype(BF16), dims, preferred_element_type=F32)

        def finish(acc):
            outs = epilogue(acc, *[r[...] for r in extra_refs]) if epilogue is not None else (acc,) * n_out
            for o_ref, o in zip(out_refs, outs):
                o_ref[...] = o.astype(o_ref.dtype)

        if nk == 1:
            finish(prod)
        elif in_place:
            @pl.when(k == 0)
            def _():
                acc_ref[...] = prod

            @pl.when(k > 0)
            def _():
                acc_ref[...] += prod

            if n_out > 1:
                @pl.when(k == nk - 1)
                def _():
                    for o_ref in out_refs[1:]:
                        o_ref[...] = acc_ref[...].astype(o_ref.dtype)
        else:
            @pl.when(k == 0)
            def _():
                acc_ref[...] = prod

            @pl.when(k > 0)
            def _():
                acc_ref[...] += prod

            @pl.when(k == nk - 1)
            def _():
                finish(acc_ref[...])

        finish_ride()

    ins, in_specs, out_specs, out_shape, scratch = _hitch(
        rider, [a.arr, b.arr] + [e[0] for e in extras], [a_spec, b_spec] + extra_specs, out_specs, out_shape, own_scratch)
    outs = pl.pallas_call(
        body, name=name, grid=grid, in_specs=in_specs, out_specs=out_specs, out_shape=out_shape, scratch_shapes=scratch,
        compiler_params=_params(("arbitrary",) * 3 if rider is not None else ("parallel", "parallel", "arbitrary")),
    )(*ins)
    if rider is not None:
        return (outs[0] if n_out == 1 else outs[:n_out]), outs[n_out:]
    return outs[0] if n_out == 1 else outs


def _tile(n, cands):
    for t in cands:
        if n % t == 0:
            return t
    return n


def _grp(i, P):
    return 2 * (i // P) + jnp.minimum(i % P, 1)


def _mod_spec(D, P, part, B):
    return pl.BlockSpec((1, 1, D), lambda i: (jnp.where(i % P == 0, B, i // P), 0, part))


def _res_norm(x, pending, modtab, shift_part, scale_part, gain, *, TR, P, name):
    T, D = x.shape
    row = pl.BlockSpec((TR, D), lambda i: (i, 0))
    has_branch = pending is not None
    ins, specs = [x], [row]
    if has_branch:
        branch, gate_tab, gate_part = pending
        ins += [branch, gate_tab]
        specs += [row, _mod_spec(D, P, gate_part, T // (TR * P))]
    ins += [modtab, modtab, gain]
    specs += [_mod_spec(D, P, shift_part, T // (TR * P)), _mod_spec(D, P, scale_part, T // (TR * P)), pl.BlockSpec((1, D), lambda i: (0, 0))]

    def body(*refs):
        if has_branch:
            x_ref, br_ref, g_ref, sh_ref, sc_ref, gn_ref, xo_ref, h_ref = refs
            xv = x_ref[...] + g_ref[0] * br_ref[...]
        else:
            x_ref, sh_ref, sc_ref, gn_ref, xo_ref, h_ref = refs
            xv = x_ref[...]
        xo_ref[...] = xv
        y = xv * lax.rsqrt(jnp.mean(xv * xv, axis=-1, keepdims=True) + EPS) * gn_ref[...]
        h_ref[...] = (y * (1.0 + sc_ref[0]) + sh_ref[0]).astype(BF16)

    return pl.pallas_call(
        body, name=name, grid=(T // TR,), in_specs=specs, out_specs=[row, row],
        out_shape=[_sds((T, D), F32), _sds((T, D), BF16)], compiler_params=_params(("parallel",)),
    )(*ins)


def _norm_bwd(x, dh, dres, modtab, scale_part, gain, below, *, TR, P, name):
    T, D = x.shape
    G = 2 * (T // (TR * P))
    row = pl.BlockSpec((TR, D), lambda i: (i, 0))
    acc = pl.BlockSpec((1, 1, D), lambda i: (_grp(i, P), 0, 0))
    has_below = below is not None

    def body(*refs):
        if has_below:
            x_ref, dh_ref, dres_ref, sc_ref, gn_ref, br_ref, g_ref, dx_ref, dsh_ref, dsc_ref, dgn_ref, db_ref, dg_ref = refs
        else:
            x_ref, dh_ref, dres_ref, sc_ref, gn_ref, dx_ref, dsh_ref, dsc_ref, dgn_ref = refs
        r = pl.program_id(0) % P
        xv, dhv, gn = x_ref[...], dh_ref[...], gn_ref[...]
        rstd = lax.rsqrt(jnp.mean(xv * xv, axis=-1, keepdims=True) + EPS)
        xhat = xv * rstd
        dn = dhv * (1.0 + sc_ref[0])
        dxhat = dn * gn
        dxv = dres_ref[...] + rstd * (dxhat - xhat * jnp.mean(dxhat * xhat, axis=-1, keepdims=True))
        dx_ref[...] = dxv
        parts = [jnp.sum(dhv, axis=0, keepdims=True), jnp.sum(dhv * (xhat * gn), axis=0, keepdims=True),
                 jnp.sum(dn * xhat, axis=0, keepdims=True)]
        outs = [dsh_ref, dsc_ref, dgn_ref]
        if has_below:
            db_ref[...] = (dxv * g_ref[0]).astype(BF16)
            parts.append(jnp.sum(dxv * br_ref[...], axis=0, keepdims=True))
            outs.append(dg_ref)

        @pl.when(r <= 1)
        def _():
            for o_ref, part in zip(outs, parts):
                o_ref[0] = part

        @pl.when(r > 1)
        def _():
            for o_ref, part in zip(outs, parts):
                o_ref[0] += part

    ins = [x, dh, dres, modtab, gain]
    in_specs = [row, row, row, _mod_spec(D, P, scale_part, T // (TR * P)), pl.BlockSpec((1, D), lambda i: (0, 0))]
    out_specs, out_shape = [row, acc, acc, acc], [_sds((T, D), F32)] + [_sds((G, 1, D), F32)] * 3
    if has_below:
        branch, gate_tab, gate_part = below
        ins += [branch, gate_tab]
        in_specs += [row, _mod_spec(D, P, gate_part, T // (TR * P))]
        out_specs += [row, acc]
        out_shape += [_sds((T, D), BF16), _sds((G, 1, D), F32)]
    return pl.pallas_call(body, name=name, grid=(T // TR,), in_specs=in_specs, out_specs=out_specs, out_shape=out_shape,
                          compiler_params=_params(("arbitrary",)))(*ins)


def _loss_head(x, branch, modtab, gate_part, target, *, TR, P, name):
    T, D = x.shape
    row = pl.BlockSpec((TR, D), lambda i: (i, 0))
    tgt = pl.BlockSpec((TR, D), lambda i: ((i // P) * (P - 1) + jnp.maximum(i % P - 1, 0), 0))
    one = pl.BlockSpec((1, LANES), lambda i: (0, 0))

    G = 2 * (T // (TR * P))
    acc = pl.BlockSpec((1, 1, D), lambda i: (_grp(i, P), 0, 0))

    def body(x_ref, br_ref, g_ref, t_ref, dy_ref, loss_ref, db_ref, dg_ref):
        i = pl.program_id(0)
        r = i % P

        @pl.when(i == 0)
        def _():
            loss_ref[...] = jnp.zeros_like(loss_ref)

        @pl.when(r == 0)
        def _():
            dy_ref[...] = jnp.zeros_like(dy_ref)
            db_ref[...] = jnp.zeros_like(db_ref)
            dg_ref[...] = jnp.zeros_like(dg_ref)

        @pl.when(r > 0)
        def _():
            brv, g = br_ref[...], g_ref[0]
            err = x_ref[...] + g * brv - t_ref[...]
            dy = err / D
            dy_ref[...] = dy
            db_ref[...] = (dy * g).astype(BF16)
            part = jnp.sum(dy * brv, axis=0, keepdims=True)
            per_tok = jnp.mean(err * err, axis=-1, keepdims=True)
            loss_ref[...] += 0.5 * jnp.sum(per_tok, axis=0, keepdims=True)

            @pl.when(r == 1)
            def _():
                dg_ref[0] = part

            @pl.when(r > 1)
            def _():
                dg_ref[0] += part

    return pl.pallas_call(
        body, name=name, grid=(T // TR,), in_specs=[row, row, _mod_spec(D, P, gate_part, T // (TR * P)), tgt],
        out_specs=[row, one, row, acc],
        out_shape=[_sds((T, D), F32), _sds((1, LANES), F32), _sds((T, D), BF16), _sds((G, 1, D), F32)],
        compiler_params=_params(("arbitrary",)),
    )(x, branch, modtab, target)


QKV_WIDTH = Q_WIDTH + 2 * KV_WIDTH
QK_NORMED = 4


def _seg_mean(v):
    lane = lax.broadcasted_iota(jnp.int32, v.shape, 1)
    lo = lane < HEAD_DIM
    s0 = jnp.sum(jnp.where(lo, v, 0.0), axis=-1, keepdims=True)
    s1 = jnp.sum(jnp.where(lo, 0.0, v), axis=-1, keepdims=True)
    return jnp.where(lo, s0, s1) * (1.0 / HEAD_DIM)


def _pair_swap(v):
    lane = lax.broadcasted_iota(jnp.int32, v.shape, 1)
    return jnp.where((lane & 1) == 0, pltpu.roll(v, LANES - 1, 1), pltpu.roll(v, 1, 1))


def _chunk(c):
    return slice(c * LANES, (c + 1) * LANES)


def _qk_prep(z, gains, cos, sin, *, TR, P, name):
    T = z.shape[0]

    def body(z_ref, g_ref, c_ref, s_ref, q_ref, k_ref, v_ref):
        cs, sn = c_ref[...], s_ref[...]
        for ch in range(QK_NORMED):
            xv = z_ref[:, _chunk(ch)]
            y = xv * lax.rsqrt(_seg_mean(xv * xv) + EPS) * g_ref[0, :, _chunk(ch)]
            out = (y * cs + _pair_swap(y) * sn).astype(BF16)
            if ch < QK_NORMED - 1:
                q_ref[:, _chunk(ch)] = out
            else:
                k_ref[...] = out
        v_ref[...] = z_ref[:, _chunk(QK_NORMED)].astype(BF16)

    def out(width):
        return pl.BlockSpec((None, TR, width), lambda i, j: (j, i, 0))
    return pl.pallas_call(
        body, name=name, grid=(T // TR, 2),
        in_specs=[pl.BlockSpec((TR, QKV_WIDTH), lambda i, j: (i, j)),
                  pl.BlockSpec((1, 1, QKV_WIDTH), lambda i, j: (j, 0, 0)),
                  pl.BlockSpec((TR, LANES), lambda i, j: (i % P, 0)),
                  pl.BlockSpec((TR, LANES), lambda i, j: (i % P, 0))],
        out_specs=[out(Q_WIDTH), out(KV_WIDTH), out(KV_WIDTH)],
        out_shape=[_sds((2, T, Q_WIDTH), BF16), _sds((2, T, KV_WIDTH), BF16), _sds((2, T, KV_WIDTH), BF16)],
        compiler_params=_params(("parallel", "parallel")),
    )(z, gains, cos, sin)


def _qk_prep_bwd(z, dq, dk, dv, gains, cos, sin, *, branch, TR, P, name):
    T = z.shape[0]
    nt = T // TR

    def body(z_ref, dq_ref, dk_ref, dv_ref, g_ref, c_ref, s_ref, dz_ref, dg_ref):
        i = pl.program_id(0)
        cs, sn = c_ref[...], s_ref[...]
        parts = []
        for ch in range(QK_NORMED):
            xv, g = z_ref[:, _chunk(ch)], g_ref[0, :, _chunk(ch)]
            dout = dq_ref[:, _chunk(ch)] if ch < QK_NORMED - 1 else dk_ref[...]
            dy = dout * cs + _pair_swap(dout * sn)
            rstd = lax.rsqrt(_seg_mean(xv * xv) + EPS)
            xhat = xv * rstd
            dxhat = dy * g
            dz_ref[:, _chunk(ch)] = (rstd * (dxhat - xhat * _seg_mean(dxhat * xhat))).astype(BF16)
            parts.append(jnp.sum(dy * xhat, axis=0, keepdims=True))
        dz_ref[:, _chunk(QK_NORMED)] = dv_ref[...].astype(BF16)
        parts.append(jnp.zeros((1, LANES), F32))
        part = jnp.concatenate(parts, axis=1)

        @pl.when(i == 0)
        def _():
            dg_ref[0] = part

        @pl.when(i > 0)
        def _():
            dg_ref[0] += part

    def rows(width, col=0):
        return pl.BlockSpec((TR, width), lambda i: (i, col))
    return pl.pallas_call(
        body, name=name, grid=(nt,),
        in_specs=[rows(QKV_WIDTH, branch), rows(Q_WIDTH), rows(KV_WIDTH), rows(KV_WIDTH),
                  pl.BlockSpec((1, 1, QKV_WIDTH), lambda i: (branch, 0, 0)),
                  pl.BlockSpec((TR, LANES), lambda i: (i % P, 0)),
                  pl.BlockSpec((TR, LANES), lambda i: (i % P, 0))],
        out_specs=[rows(QKV_WIDTH), pl.BlockSpec((1, 1, QKV_WIDTH), lambda i: (0, 0, 0))],
        out_shape=[_sds((T, QKV_WIDTH), BF16), _sds((1, 1, QKV_WIDTH), F32)],
        compiler_params=_params(("arbitrary",)),
    )(z, dq, dk, dv, gains, cos, sin)


NT_DIMS = (((1,), (1,)), ((), ()))
TN_DIMS = (((0,), (0,)), ((), ()))
QROWS = N_GROUP * Q_BLOCK
SCORE_SCALE = HEAD_DIM ** -0.5
BAND = Q_BLOCK + 2 * WINDOW
FWD_LATENT_CHUNK = 256
BWD_LATENT_CHUNK = 1024


def _move_head(block, half_from, half_to):
    lane = lax.broadcasted_iota(jnp.int32, block.shape, 1)
    src = block if half_from == half_to else pltpu.roll(block, HEAD_DIM, 1)
    keep = (lane < HEAD_DIM) if half_to == 0 else (lane >= HEAD_DIM)
    return jnp.where(keep, src, 0.0)


def _stack_heads(lane_block, j):
    pieces = []
    for h in range(N_GROUP * j, N_GROUP * (j + 1)):
        pieces.append(_move_head(lane_block(h // 2), h % 2, j))
    return jnp.concatenate(pieces, axis=0)


def _lane_blocks(ref):
    return lambda m: ref[:, m * LANES:(m + 1) * LANES].astype(F32)


def _unstack_heads(stacked, ref):
    heads = []
    for h in range(N_HEADS):
        j, r = h // N_GROUP, h % N_GROUP
        heads.append(_move_head(stacked[j][r * Q_BLOCK:(r + 1) * Q_BLOCK], j, h % 2))
    for m in range(N_HEADS // 2):
        ref[:, m * LANES:(m + 1) * LANES] = (heads[2 * m] + heads[2 * m + 1]).astype(ref.dtype)


def _key_chunks(i, latent, *, n_ctx, t_all, window, chunk, latent_chunk):
    ctx = [(s, chunk, False) for s in range(0, n_ctx, chunk)]
    if not latent:
        return ctx
    if not window:
        wide = latent_chunk if (t_all - n_ctx) % latent_chunk == 0 else chunk
        return ctx + [(s, wide, False) for s in range(n_ctx, t_all, wide)]
    start = pl.multiple_of(jnp.minimum((i - 1) * Q_BLOCK, t_all - BAND), Q_BLOCK)
    band_chunk = BAND if latent_chunk >= BAND else (chunk if BAND % chunk == 0 else Q_BLOCK)
    return ctx + [(start + s, band_chunk, True) for s in range(0, BAND, band_chunk)]


def _scores(q, k_ref, i, start, size, masked, *, n_ctx):
    s = lax.dot_general(q, k_ref[pl.ds(start, size), :], NT_DIMS, preferred_element_type=F32)
    if masked:
        qpos = (i * Q_BLOCK - n_ctx) + (lax.broadcasted_iota(jnp.int32, (QROWS, size), 0) & (Q_BLOCK - 1))
        kpos = (start - n_ctx) + lax.broadcasted_iota(jnp.int32, (QROWS, size), 1)
        valid = (kpos - qpos <= WINDOW) & (qpos - kpos <= WINDOW) & (kpos >= 0)
        s = jnp.where(valid, s, NEG)
    return s


def _sink_column(sink_ref, j):
    r = lax.broadcasted_iota(jnp.int32, (QROWS, 1), 0)
    s0, s1, s2 = sink_ref[j * N_GROUP], sink_ref[j * N_GROUP + 1], sink_ref[j * N_GROUP + 2]
    return jnp.where(r < Q_BLOCK, s0, jnp.where(r < 2 * Q_BLOCK, s1, s2))


def _attn_specs(Tp, branch):
    nq = Tp // Q_BLOCK
    q_in = pl.BlockSpec((None, Q_BLOCK, Q_WIDTH), lambda b, i: (branch, b * nq + i, 0))
    kv_in = pl.BlockSpec((None, Tp, KV_WIDTH), lambda b, i: (branch, b, 0))
    q_out = pl.BlockSpec((Q_BLOCK, Q_WIDTH), lambda b, i: (b * nq + i, 0))
    kv_out = pl.BlockSpec((Tp, KV_WIDTH), lambda b, i: (b, 0))
    return q_in, kv_in, q_out, kv_out


def _attn_chunk(Tp):
    return 256 if Tp % 256 == 0 else Q_BLOCK


def _attn_fwd(q, k, v, sink, *, branch, B, n_ctx, window, name, rider=None):
    T = q.shape[1]
    Tp = T // B
    nq = Tp // Q_BLOCK
    has_sink = sink is not None
    n_in = 4 if has_sink else 3
    q_in, kv_in, q_out, _ = _attn_specs(Tp, branch)
    lse_spec = pl.BlockSpec((None, N_KV * QROWS, 1), lambda b, i: (b * nq + i, 0, 0))

    def body(*refs):
        refs, finish_ride = _ride(rider, refs, n_in, 3, 0, (B, nq))
        sink_ref = refs.pop(0) if has_sink else None
        q_ref, k_ref, v_ref, o_ref, o32_ref, lse_ref = refs
        i = pl.program_id(1)

        def run(latent):
            outs = []
            for j in range(N_KV):
                qv = (_stack_heads(_lane_blocks(q_ref), j) * SCORE_SCALE).astype(BF16)
                if has_sink:
                    m, l = _sink_column(sink_ref, j), jnp.ones((QROWS, 1), F32)
                else:
                    m, l = jnp.full((QROWS, 1), NEG, F32), jnp.zeros((QROWS, 1), F32)
                acc = jnp.zeros((QROWS, LANES), F32)
                for start, size, masked in _key_chunks(i, latent, n_ctx=n_ctx, t_all=Tp, window=window,
                                                       chunk=_attn_chunk(Tp), latent_chunk=FWD_LATENT_CHUNK):
                    s = _scores(qv, k_ref, i, start, size, masked, n_ctx=n_ctx)
                    m_new = jnp.maximum(m, jnp.max(s, axis=-1, keepdims=True))
                    alpha = jnp.exp(m - m_new)
                    p = jnp.exp(s - m_new)
                    l = l * alpha + jnp.sum(p, axis=-1, keepdims=True)
                    acc = acc * alpha + jnp.dot(p.astype(BF16), v_ref[pl.ds(start, size), :], preferred_element_type=F32)
                    m = m_new
                outs.append(acc * (1.0 / l))
                lse_ref[j * QROWS:(j + 1) * QROWS, :] = m + jnp.log(l)
            _unstack_heads(outs, o_ref)
            _unstack_heads(outs, o32_ref)

        @pl.when(i < n_ctx // Q_BLOCK)
        def _():
            run(False)

        @pl.when(i >= n_ctx // Q_BLOCK)
        def _():
            run(True)

        finish_ride()

    ins, specs = [q, k, v], [q_in, kv_in, kv_in]
    if has_sink:
        ins, specs = [sink] + ins, [pl.BlockSpec(memory_space=pltpu.SMEM)] + specs
    out_specs = [q_out, q_out, lse_spec]
    out_shape = [_sds((T, Q_WIDTH), BF16), _sds((T, Q_WIDTH), F32), _sds((T // Q_BLOCK, N_KV * QROWS, 1), F32)]
    ins, specs, out_specs, out_shape, scratch = _hitch(rider, ins, specs, out_specs, out_shape, [])
    return pl.pallas_call(
        body, name=name, grid=(B, nq), in_specs=specs, out_specs=out_specs, out_shape=out_shape, scratch_shapes=scratch,
        compiler_params=_params(("arbitrary", "arbitrary") if rider is not None else ("parallel", "parallel")),
    )(*ins)


def _attn_bwd(q, k, v, do, o32, lse, sink, *, branch, B, n_ctx, window, name, rider=None):
    T = q.shape[1]
    Tp = T // B
    nq = Tp // Q_BLOCK
    has_sink = sink is not None
    q_in, kv_in, q_out, kv_out = _attn_specs(Tp, branch)
    lse_spec = pl.BlockSpec((None, N_KV * QROWS, 1), lambda b, i: (b * nq + i, 0, 0))
    sink_spec = pl.BlockSpec((None, 8, LANES), lambda b, i: (b, 0, 0))

    def body(*refs):
        refs, finish_ride = _ride(rider, refs, 7 if has_sink else 6, 4 if has_sink else 3, 2, (B, nq))
        if has_sink:
            sink_ref, q_ref, k_ref, v_ref, do_ref, o_ref, lse_ref, dq_ref, dk_ref, dv_ref, ds_ref, dkt_ref, dvt_ref = refs
        else:
            q_ref, k_ref, v_ref, do_ref, o_ref, lse_ref, dq_ref, dk_ref, dv_ref, dkt_ref, dvt_ref = refs
        i = pl.program_id(1)

        @pl.when(i == 0)
        def _():
            dk_ref[...] = jnp.zeros_like(dk_ref)
            dv_ref[...] = jnp.zeros_like(dv_ref)
            if not window:
                dkt_ref[...] = jnp.zeros_like(dkt_ref)
                dvt_ref[...] = jnp.zeros_like(dvt_ref)
            if has_sink:
                ds_ref[...] = jnp.zeros_like(ds_ref)

        def run(latent):
            upd = jnp.zeros((8, LANES), F32)
            do_blocks, o_blocks = _lane_blocks(do_ref), _lane_blocks(o_ref)
            qvs = [(_stack_heads(_lane_blocks(q_ref), j) * SCORE_SCALE).astype(BF16) for j in range(N_KV)]
            dovs = [_stack_heads(do_blocks, j).astype(BF16) for j in range(N_KV)]
            deltas = [jnp.sum(_stack_heads(lambda m: do_blocks(m) * o_blocks(m), j), axis=-1, keepdims=True)
                      for j in range(N_KV)]
            lses = [lse_ref[j * QROWS:(j + 1) * QROWS, :] for j in range(N_KV)]
            q_all, do_all = jnp.concatenate(qvs, axis=0), jnp.concatenate(dovs, axis=0)
            q_all_t, do_all_t = q_all.T, do_all.T
            dqs = [jnp.zeros((QROWS, LANES), F32) for _ in range(N_KV)]
            for start, size, masked in _key_chunks(i, latent, n_ctx=n_ctx, t_all=Tp, window=window,
                                                   chunk=_attn_chunk(Tp), latent_chunk=BWD_LATENT_CHUNK):
                rows = pl.ds(start, size)
                ds_all, p_all = [], []
                for j in range(N_KV):
                    p = jnp.exp(_scores(qvs[j], k_ref, i, start, size, masked, n_ctx=n_ctx) - lses[j])
                    dp = lax.dot_general(dovs[j], v_ref[rows, :], NT_DIMS, preferred_element_type=F32)
                    ds = (p * (dp - deltas[j])).astype(BF16)
                    dqs[j] = dqs[j] + jnp.dot(ds, k_ref[rows, :], preferred_element_type=F32)
                    ds_all.append(ds)
                    p_all.append(p.astype(BF16))
                ds_cat, p_cat = jnp.concatenate(ds_all, axis=0), jnp.concatenate(p_all, axis=0)
                if window:
                    dk_ref[rows, :] += lax.dot_general(ds_cat, q_all, TN_DIMS, preferred_element_type=F32)
                    dv_ref[rows, :] += lax.dot_general(p_cat, do_all, TN_DIMS, preferred_element_type=F32)
                else:
                    dkt_ref[:, start:start + size] += jnp.dot(q_all_t, ds_cat, preferred_element_type=F32)
                    dvt_ref[:, start:start + size] += jnp.dot(do_all_t, p_cat, preferred_element_type=F32)
            dqs = [dq * SCORE_SCALE for dq in dqs]
            for j in range(N_KV):
                if has_sink:
                    contrib = -(jnp.exp(_sink_column(sink_ref, j) - lses[j]) * deltas[j])
                    r = lax.broadcasted_iota(jnp.int32, (QROWS, 1), 0)
                    row8 = lax.broadcasted_iota(jnp.int32, (8, LANES), 0)
                    for h in range(N_GROUP):
                        in_head = (r >= h * Q_BLOCK) & (r < (h + 1) * Q_BLOCK)
                        tot = jnp.sum(jnp.where(in_head, contrib, 0.0), axis=0, keepdims=True)
                        upd = upd + jnp.where(row8 == j * N_GROUP + h, tot, 0.0)
            _unstack_heads(dqs, dq_ref)
            if has_sink:
                ds_ref[...] += upd

        @pl.when(i < n_ctx // Q_BLOCK)
        def _():
            run(False)

        @pl.when(i >= n_ctx // Q_BLOCK)
        def _():
            run(True)

        if not window:
            @pl.when(i == nq - 1)
            def _():
                dk_ref[...] += dkt_ref[...].T
                dv_ref[...] += dvt_ref[...].T

        finish_ride()

    ins, specs = [q, k, v, do, o32, lse], [q_in, kv_in, kv_in, q_out, q_out, lse_spec]
    out_specs = [q_out, kv_out, kv_out]
    out_shape = [_sds((T, Q_WIDTH), F32), _sds((T, KV_WIDTH), F32), _sds((T, KV_WIDTH), F32)]
    if has_sink:
        ins, specs = [sink] + ins, [pl.BlockSpec(memory_space=pltpu.SMEM)] + specs
        out_specs.append(sink_spec)
        out_shape.append(_sds((B, 8, LANES), F32))
    scratch = [pltpu.VMEM((KV_WIDTH, LANES if window else Tp), F32)] * 2
    ins, specs, out_specs, out_shape, scratch = _hitch(rider, ins, specs, out_specs, out_shape, scratch)
    return pl.pallas_call(
        body, name=name, grid=(B, nq), in_specs=specs, out_specs=out_specs, out_shape=out_shape, scratch_shapes=scratch,
        compiler_params=_params(("arbitrary", "arbitrary") if rider is not None else ("parallel", "arbitrary")),
    )(*ins)


def _window_sums(xp):
    n = xp.shape[0]

    def ahead(a, k):
        return pltpu.roll(a, n - k, 0)
    a2 = xp + ahead(xp, 1)
    a4 = a2 + ahead(a2, 2)
    a8 = a4 + ahead(a4, 4)
    a16 = a8 + ahead(a8, 8)
    return (a2, a4, a8, a16)


def _by_group(vals):
    lane = lax.broadcasted_iota(jnp.int32, vals[0].shape, 1)
    return jnp.where(lane < POOL_CH, vals[0], jnp.where(lane < 2 * POOL_CH, vals[1],
                     jnp.where(lane < 3 * POOL_CH, vals[2], vals[3])))


def _pool_counts(n):
    t = lax.broadcasted_iota(jnp.int32, (n, POOL_WIDTH), 0)
    cnts = [(jnp.minimum(t + w // 2, n) - jnp.maximum(t - w // 2, 0)).astype(F32) for w in POOL_WINDOWS]
    return _by_group(cnts)


def _pad_rows(x):
    zeros = jnp.zeros((POOL_PAD, x.shape[1]), x.dtype)
    return jnp.concatenate([zeros, x, zeros], axis=0)


def _pool_stream(u):
    n = u.shape[0]
    sums = _window_sums(_pad_rows(u))
    tots = [pltpu.roll(a, w // 2, 0)[POOL_PAD:POOL_PAD + n] for a, w in zip(sums, POOL_WINDOWS)]
    return _by_group(tots) / _pool_counts(n) - u


def _pool_stream_t(dp):
    n = dp.shape[0]
    sums = _window_sums(_pad_rows(dp / _pool_counts(n)))
    tots = [pltpu.roll(a, w // 2 - 1, 0)[POOL_PAD:POOL_PAD + n] if w > 2 else a[POOL_PAD:POOL_PAD + n]
            for a, w in zip(sums, POOL_WINDOWS)]
    return _by_group(tots) - dp


def _pool_fwd(z, w_bd, scale, *, B, Tp, n_ctx, name):
    T = z.shape[0]
    blk = pl.BlockSpec((Tp, POOL_WIDTH), lambda b: (b, U_COL // POOL_WIDTH))
    out = pl.BlockSpec((Tp, POOL_WIDTH), lambda b: (b, 0))

    def body(u_ref, w_ref, s_ref, p_ref, o_ref):
        for lo, hi in ((0, n_ctx), (n_ctx, Tp)):
            pooled = _pool_stream(u_ref[lo:hi, :]).astype(BF16)
            p_ref[lo:hi, :] = pooled
            mixed = jnp.dot(pooled, w_ref[...], preferred_element_type=F32)
            o_ref[lo:hi, :] = (mixed * s_ref[...]).astype(BF16)

    return pl.pallas_call(
        body, name=name, grid=(B,),
        in_specs=[blk, pl.BlockSpec((POOL_WIDTH, POOL_WIDTH), lambda b: (0, 0)), pl.BlockSpec((1, POOL_WIDTH), lambda b: (0, 0))],
        out_specs=[out, out], out_shape=[_sds((T, POOL_WIDTH), BF16)] * 2, compiler_params=_params(("parallel",)),
    )(z, w_bd, scale)


def _pool_bwd(d_ob, pooled, w_bd, scale, *, B, Tp, n_ctx, name):
    T = d_ob.shape[0]
    blk = pl.BlockSpec((Tp, POOL_WIDTH), lambda b: (b, 0))
    wsp = pl.BlockSpec((POOL_WIDTH, POOL_WIDTH), lambda b: (0, 0))
    ssp = pl.BlockSpec((1, POOL_WIDTH), lambda b: (0, 0))

    def body(d_ref, p_ref, w_ref, s_ref, du_ref, dw_ref, dsc_ref):
        @pl.when(pl.program_id(0) == 0)
        def _():
            dw_ref[...] = jnp.zeros_like(dw_ref)
            dsc_ref[...] = jnp.zeros_like(dsc_ref)

        dv, pv, wv = d_ref[...], p_ref[...], w_ref[...]
        mixed = jnp.dot(pv, wv, preferred_element_type=F32)
        dsc_ref[...] += jnp.sum(dv * mixed, axis=0, keepdims=True)
        dmixed = (dv * s_ref[...]).astype(BF16)
        dw_ref[...] += lax.dot_general(pv, dmixed, TN_DIMS, preferred_element_type=F32)
        dpooled = lax.dot_general(dmixed, wv, NT_DIMS, preferred_element_type=F32)
        for lo, hi in ((0, n_ctx), (n_ctx, Tp)):
            du_ref[lo:hi, :] = _pool_stream_t(dpooled[lo:hi, :]).astype(BF16)

    return pl.pallas_call(
        body, name=name, grid=(B,), in_specs=[blk, blk, wsp, ssp], out_specs=[blk, wsp, ssp],
        out_shape=[_sds((T, POOL_WIDTH), BF16), _sds((POOL_WIDTH, POOL_WIDTH), F32), _sds((1, POOL_WIDTH), F32)],
        compiler_params=_params(("arbitrary",)),
    )(d_ob, pooled, w_bd, scale)


def _merge_specs(z, D, TR, tc, wa, wb, wc):
    def act(width):
        return pl.BlockSpec((TR, width), lambda i, n: (i, 0))

    def gate(part):
        return pl.BlockSpec((TR, tc), lambda i, n: (i, (GATE_COL + part * D) // tc + n))
    w_specs = [w.spec(w.shape[0], tc, lambda i, n: (0, n)) for w in (wa, wb, wc)]
    return [act(Q_WIDTH), act(POOL_WIDTH), act(Q_WIDTH), gate(0), gate(1), gate(2)] + w_specs


def _merge_fwd(oa, ob, oc, z, wa, wb, wc, *, D, TR, name):
    T = oa.shape[0]
    tc = D // N_CHIPS

    def body(oa_ref, ob_ref, oc_ref, ga_ref, gb_ref, gc_ref, wa_ref, wb_ref, wc_ref, y_ref):
        acc = jax.nn.sigmoid(ga_ref[...]) * jnp.dot(oa_ref[...], wa_ref[...], preferred_element_type=F32)
        acc += jax.nn.sigmoid(gb_ref[...]) * jnp.dot(ob_ref[...], wb_ref[...], preferred_element_type=F32)
        acc += jax.nn.sigmoid(gc_ref[...]) * jnp.dot(oc_ref[...], wc_ref[...], preferred_element_type=F32)
        y_ref[...] = acc.astype(BF16)

    return pl.pallas_call(
        body, name=name, grid=(T // TR, D // tc), in_specs=_merge_specs(z, D, TR, tc, wa, wb, wc),
        out_specs=pl.BlockSpec((TR, tc), lambda i, n: (i, n)), out_shape=_sds((T, D), BF16),
        compiler_params=_params(("parallel", "parallel")),
    )(oa, ob, oc, z, z, z, wa.arr, wb.arr, wc.arr)


def _merge_bwd(dy, oa, ob, oc, z, wa, wb, wc, *, D, TR, name):
    T = oa.shape[0]
    tc = D // N_CHIPS
    out = pl.BlockSpec((TR, tc), lambda i, n: (i, n))

    def body(dy_ref, oa_ref, ob_ref, oc_ref, ga_ref, gb_ref, gc_ref, wa_ref, wb_ref, wc_ref,
             dpa_ref, dpb_ref, dpc_ref, dga_ref, dgb_ref, dgc_ref):
        dyv = dy_ref[...]
        for o_ref, g_ref, w_ref, dp_ref, dg_ref in ((oa_ref, ga_ref, wa_ref, dpa_ref, dga_ref),
                                                    (ob_ref, gb_ref, wb_ref, dpb_ref, dgb_ref),
                                                    (oc_ref, gc_ref, wc_ref, dpc_ref, dgc_ref)):
            s = jax.nn.sigmoid(g_ref[...])
            proj = jnp.dot(o_ref[...], w_ref[...], preferred_element_type=F32)
            dp_ref[...] = (dyv * s).astype(BF16)
            dg_ref[...] = (dyv * proj * (s * (1.0 - s))).astype(BF16)

    return pl.pallas_call(
        body, name=name, grid=(T // TR, D // tc), in_specs=[out] + _merge_specs(z, D, TR, tc, wa, wb, wc),
        out_specs=[out] * 6, out_shape=[_sds((T, D), BF16)] * 6, compiler_params=_params(("parallel", "parallel")),
    )(dy, oa, ob, oc, z, z, z, wa.arr, wb.arr, wc.arr)


def _silu_rows(cc, name):
    def body(c_ref, s_ref):
        v = c_ref[...]
        s_ref[...] = (v * jax.nn.sigmoid(v)).astype(BF16)
    return pl.pallas_call(body, name=name, out_shape=_sds(cc.shape, BF16))(cc)


def _ada_bwd_rows(dm, ds, cc, name):
    def body(dm_ref, ds_ref, c_ref, db_ref, dc_ref):
        db_ref[...] = jnp.sum(dm_ref[...], axis=0, keepdims=True)
        v = c_ref[...]
        s = jax.nn.sigmoid(v)
        dc_ref[...] = ds_ref[...] * (s * (1.0 + v * (1.0 - s)))
    return pl.pallas_call(body, name=name, out_shape=[_sds((1, dm.shape[1]), F32), _sds(cc.shape, F32)])(dm, ds, cc)


def _row_tile(rows, cols):
    for t in (512, 256, 128, 64, 32, 16, 8):
        if rows % t == 0 and t * cols * 4 <= (1 << 20):
            return t
    return rows


def _working_rows(tr, C, worker):
    return pl.BlockSpec((tr, C), lambda i, c: (jnp.where(c[0] == worker, i, 0), 0))


def _add_landed(own, landed, core, worker, name):
    R, C = own.shape
    tr = _row_tile(R, C)
    row = _working_rows(tr, C, worker)

    def body(c_ref, a_ref, b_ref, o_ref, o16_ref):
        @pl.when(c_ref[0] == worker)
        def _():
            tot = a_ref[...] + b_ref[...].astype(F32)
            o_ref[...] = tot
            o16_ref[...] = tot.astype(BF16)

    grid_spec = pltpu.PrefetchScalarGridSpec(num_scalar_prefetch=1, grid=(R // tr,), in_specs=[row, row], out_specs=[row, row])
    return pl.pallas_call(body, name=name, grid_spec=grid_spec, out_shape=[_sds((R, C), F32), _sds((R, C), BF16)],
                          compiler_params=_params(("arbitrary",)))(core, own, landed)


def _sum_chips(own, landed, chip, core, worker, name):
    _, R, C = own.shape
    tr = _row_tile(R, C)

    def row(i, c):
        return jnp.where(c[0] == worker, i, 0)

    def body(k_ref, c_ref, a_ref, b_ref, o_ref):
        @pl.when(c_ref[0] == worker)
        def _():
            o_ref[...] = ((a_ref[...] + b_ref[0].astype(F32)) + b_ref[1].astype(F32)) + b_ref[2].astype(F32)

    grid_spec = pltpu.PrefetchScalarGridSpec(
        num_scalar_prefetch=2, grid=(R // tr,),
        in_specs=[pl.BlockSpec((None, tr, C), lambda i, k, c: (k[0], row(i, c), 0)),
                  pl.BlockSpec((3, tr, C), lambda i, k, c: (0, row(i, c), 0))],
        out_specs=pl.BlockSpec((tr, C), lambda i, k, c: (row(i, c), 0)))
    return pl.pallas_call(body, name=name, grid_spec=grid_spec, out_shape=_sds((R, C), F32),
                          compiler_params=_params(("arbitrary",)))(chip, core, own, landed)


def _adam_math(w, g, m, v):
    m = ADAM_B1 * m + (1.0 - ADAM_B1) * g
    v = ADAM_B2 * v + (1.0 - ADAM_B2) * (g * g)
    m_hat = m / (1.0 - ADAM_B1 ** ADAM_STEP)
    v_hat = v / (1.0 - ADAM_B2 ** ADAM_STEP)
    delta = -ADAM_LR * (m_hat / (jnp.sqrt(v_hat) + ADAM_EPS) + ADAM_WD * w)
    return delta, m, v


def _adamw(w, reduced, shared, m, v, core, worker, name):
    L, R, C = w.shape
    tr = _row_tile(R, C)

    def body(c_ref, w_ref, r0_ref, r1_ref, s0_ref, s1_ref, m_ref, v_ref, g_ref, d_ref, mo_ref, vo_ref):
        def step(g):
            d, mn, vn = _adam_math(w_ref[...], g, m_ref[...], v_ref[...])
            g_ref[...] = g
            d_ref[...] = d
            mo_ref[...] = mn
            vo_ref[...] = vn

        layer, here = pl.program_id(0), c_ref[0] == worker
        for l, (r_ref, s_ref) in enumerate(((r0_ref, s0_ref), (r1_ref, s1_ref))):
            @pl.when((layer == l) & here)
            def _(r_ref=r_ref):
                step(r_ref[...])

            @pl.when((layer == l) & jnp.logical_not(here))
            def _(s_ref=s_ref):
                step(s_ref[...])

    lay = pl.BlockSpec((None, tr, C), lambda l, i, c: (l, i, 0))
    row = pl.BlockSpec((tr, C), lambda l, i, c: (i, 0))
    grid_spec = pltpu.PrefetchScalarGridSpec(num_scalar_prefetch=1, grid=(L, R // tr),
                                             in_specs=[lay, row, row, row, row, lay, lay], out_specs=[lay] * 4)
    return pl.pallas_call(body, name=name, grid_spec=grid_spec, out_shape=[_sds((L, R, C), F32)] * 4,
                          compiler_params=_params(("parallel", "parallel")))(core, w, *reduced, *shared, m, v)


def _adamw_small(w, parts, m, v, name):
    R, C = w.shape

    def body(w_ref, p_ref, m_ref, v_ref, g_ref, d_ref, mo_ref, vo_ref):
        g = p_ref[0]
        for dev in range(1, 8):
            g = g + p_ref[dev]
        d, mn, vn = _adam_math(w_ref[...], g, m_ref[...], v_ref[...])
        g_ref[...] = g
        d_ref[...] = d
        mo_ref[...] = mn
        vo_ref[...] = vn

    return pl.pallas_call(body, name=name, out_shape=[_sds((R, C), F32)] * 4)(w, parts, m, v)


def _place():
    return lax.axis_index("x"), lax.axis_index("y"), lax.axis_index("c")


def _other_chips(x, y):
    return [(1 - x, y), (x, 1 - y), (1 - x, 1 - y)]


def _rcopy(src, dst, ssem, rsem, dev):
    return pltpu.make_async_remote_copy(src_ref=src, dst_ref=dst, send_sem=ssem, recv_sem=rsem,
                                        device_id=dev, device_id_type=MESH)


GATHER_SEMS = 7


class _LayerGather:
    def __init__(self, shards, layer):
        self.inputs, self.layer, self.n = list(shards), layer, len(shards)
        load, self.groups = [0, 0], ([], [])
        for w in sorted(range(self.n), key=lambda w: -shards[w][0].size):
            g = 0 if load[0] <= load[1] else 1
            self.groups[g].append(w)
            load[g] += shards[w][0].size
        self.out_shape = [_sds((N_CHIPS,) + s.shape[1:], s.dtype) for s in shards]
        self.scratch = [pltpu.SemaphoreType.DMA((self.n, GATHER_SEMS)), pltpu.SemaphoreType.DMA((self.n, GATHER_SEMS))]

    def _own(self, src, out, send_sems, recv_sems):
        x, y, c = _place()
        return [_rcopy(src[w].at[self.layer], out[w].at[2 * x + y], send_sems.at[w, 6], recv_sems.at[w, 6], (x, y, 1 - c))
                for w in range(self.n)]

    def _to_chips(self, g, src, out, send_sems, recv_sems):
        x, y, c = _place()
        return [_rcopy(src[w].at[self.layer], out[w].at[2 * x + y], send_sems.at[w, j], recv_sems.at[w, j], (*chip, c))
                for w in self.groups[g] for j, chip in enumerate(_other_chips(x, y))]

    def start(self, src, out, send_sems, recv_sems):
        c = lax.axis_index("c")
        for cp in self._own(src, out, send_sems, recv_sems):
            cp.start()
        for g in (0, 1):
            @pl.when(c == g)
            def _(g=g):
                for cp in self._to_chips(g, src, out, send_sems, recv_sems):
                    cp.start()

    def finish(self, src, out, send_sems, recv_sems):
        x, y, c = _place()
        sibling = (x, y, 1 - c)
        chips = _other_chips(x, y)
        for g in (0, 1):
            @pl.when(c == g)
            def _(g=g):
                passed = []
                for w in self.groups[g]:
                    for j, (px, py) in enumerate(chips):
                        landed = out[w].at[2 * px + py]
                        _rcopy(landed, landed, send_sems.at[w, j], recv_sems.at[w, j], (px, py, c)).wait_recv()
                        cp = _rcopy(landed, landed, send_sems.at[w, 3 + j], recv_sems.at[w, 3 + j], sibling)
                        cp.start()
                        passed.append(cp)
                for w in self.groups[1 - g]:
                    for j, (px, py) in enumerate(chips):
                        landed = out[w].at[2 * px + py]
                        _rcopy(landed, landed, send_sems.at[w, 3 + j], recv_sems.at[w, 3 + j], sibling).wait_recv()
                for cp in self._to_chips(g, src, out, send_sems, recv_sems) + passed:
                    cp.wait_send()
        for cp in self._own(src, out, send_sems, recv_sems):
            cp.wait_recv()
            cp.wait_send()


def _on_core(fn):
    for g in (0, 1):
        @pl.when(lax.axis_index("c") == g)
        def _(g=g):
            fn(g)


class _ToSibling:
    def __init__(self, arrays, senders):
        self.inputs, self.senders = list(arrays), list(senders)
        n = len(self.inputs)
        self.out_shape = [_sds(a.shape, a.dtype) for a in self.inputs]
        self.scratch = [pltpu.SemaphoreType.DMA((n,)), pltpu.SemaphoreType.DMA((n,))]

    def _copies(self, sender, src, out, send_sems, recv_sems):
        x, y, c = _place()
        return [_rcopy(src[w], out[w], send_sems.at[w], recv_sems.at[w], (x, y, 1 - c))
                for w in range(len(src)) if self.senders[w] == sender]

    def start(self, *refs):
        def go(g):
            for cp in self._copies(g, *refs):
                cp.start()
        _on_core(go)

    def finish(self, *refs):
        def go(g):
            for cp in self._copies(1 - g, *refs):
                cp.wait_recv()
            for cp in self._copies(g, *refs):
                cp.wait_send()
        _on_core(go)


class _ChipSend:
    def __init__(self, blocked, senders):
        self.inputs, self.senders = list(blocked), list(senders)
        n = len(self.inputs)
        self.out_shape = [_sds((3,) + a.shape[1:], a.dtype) for a in self.inputs]
        self.scratch = [pltpu.SemaphoreType.DMA((n, 3)), pltpu.SemaphoreType.DMA((n, 3))]

    def _copies(self, sender, src, out, send_sems, recv_sems):
        x, y, c = _place()
        return [_rcopy(src[w].at[2 * px + py], out[w].at[j], send_sems.at[w, j], recv_sems.at[w, j], (px, py, c))
                for w in range(len(src)) if self.senders[w] == sender for j, (px, py) in enumerate(_other_chips(x, y))]

    def start(self, *refs):
        def go(g):
            for cp in self._copies(g, *refs):
                cp.start()
        _on_core(go)

    def finish(self, *refs):
        def go(g):
            cps = self._copies(g, *refs)
            for cp in cps:
                cp.wait_recv()
            for cp in cps:
                cp.wait_send()
        _on_core(go)


def _ride_alone(rider, name):
    n_in, n_out = len(rider.inputs), len(rider.out_shape)

    def body(*refs):
        args = (refs[:n_in], refs[n_in:n_in + n_out]) + tuple(refs[n_in + n_out:])
        rider.start(*args)
        rider.finish(*args)

    return pl.pallas_call(body, name=name, in_specs=[ANY] * n_in, out_specs=[ANY] * n_out, out_shape=rider.out_shape,
                          scratch_shapes=rider.scratch)(*rider.inputs)


def _hitch(rider, ins, in_specs, out_specs, out_shape, scratch):
    if rider is None:
        return ins, in_specs, out_specs, out_shape, scratch
    return (list(ins) + rider.inputs, list(in_specs) + [ANY] * len(rider.inputs),
            list(out_specs) + [ANY] * len(rider.out_shape), list(out_shape) + rider.out_shape, list(scratch) + rider.scratch)


def _ride(rider, refs, n_in, n_out, n_scratch, grid):
    if rider is None:
        return list(refs), lambda: None
    r_in, r_out = len(rider.inputs), len(rider.out_shape)
    refs = list(refs)
    own_in, ride_in = refs[:n_in], refs[n_in:n_in + r_in]
    rest = refs[n_in + r_in:]
    own_out, ride_out = rest[:n_out], rest[n_out:n_out + r_out]
    rest = rest[n_out + r_out:]
    own_scratch, sems = rest[:n_scratch], rest[n_scratch:]
    ids = [pl.program_id(a) for a in range(len(grid))]
    first = functools.reduce(jnp.logical_and, [i == 0 for i in ids])
    last = functools.reduce(jnp.logical_and, [i == g - 1 for i, g in zip(ids, grid)])

    @pl.when(first)
    def _():
        rider.start(ride_in, ride_out, *sems)

    def finish():
        @pl.when(last)
        def _():
            rider.finish(ride_in, ride_out, *sems)

    return own_in + own_out + own_scratch, finish


def _gather_small(block, name):
    m_per, n = block.shape

    def body(x_ref, out_ref, send_sems, recv_sems, local_sem):
        x, y, c = _place()
        me, sibling = (x, y, c), (x, y, 1 - c)
        chips = _other_chips(x, y)

        def rows(px, py, pc):
            return out_ref.at[pl.ds((4 * px + 2 * py + pc) * m_per, m_per), :]

        def copy(k, blk, to, src=None):
            return _rcopy(rows(*blk) if src is None else src, rows(*blk), send_sems.at[k], recv_sems.at[k], to)

        mine = pltpu.make_async_copy(x_ref, rows(*me), local_sem)
        mine.start()
        first = [copy(0, me, sibling, src=x_ref)]
        first += [copy(1 + j, me, (*chip, c), src=x_ref) for j, chip in enumerate(chips)]
        for cp in first:
            cp.start()
        passed = [copy(4 + j, (*chip, c), sibling) for j, chip in enumerate(chips)]
        for j, chip in enumerate(chips):
            copy(1 + j, (*chip, c), me).wait_recv()
            passed[j].start()
        copy(0, sibling, me).wait_recv()
        for j, chip in enumerate(chips):
            copy(4 + j, (*chip, 1 - c), me).wait_recv()
        for cp in first + passed:
            cp.wait_send()
        mine.wait()

    return pl.pallas_call(
        body, name=name, out_shape=_sds((8 * m_per, n), block.dtype),
        in_specs=[pl.BlockSpec(memory_space=pltpu.VMEM)], out_specs=pl.BlockSpec(memory_space=pltpu.VMEM),
        scratch_shapes=[pltpu.SemaphoreType.DMA((7,)), pltpu.SemaphoreType.DMA((7,)), pltpu.SemaphoreType.DMA],
    )(block)


def _rope_tables(n_ctx, seq):
    rows = seq // GRID_W
    r = jnp.repeat(jnp.arange(rows, dtype=F32), GRID_W)
    col = jnp.tile(jnp.arange(GRID_W, dtype=F32), rows)
    inv = 1.0 / (ROPE_THETA ** (jnp.arange(0, AXIS_DIM, 2, dtype=F32) / AXIS_DIM))
    ang = jnp.concatenate([r[:, None] * inv, col[:, None] * inv], axis=-1)
    cos = jnp.repeat(jnp.cos(ang), 2, axis=-1)
    sin = jnp.repeat(jnp.sin(ang), 2, axis=-1) * jnp.tile(jnp.array([-1.0, 1.0], F32), HEAD_DIM // 2)
    cos = jnp.concatenate([jnp.ones((n_ctx, HEAD_DIM), F32), cos], axis=0)
    sin = jnp.concatenate([jnp.zeros((n_ctx, HEAD_DIM), F32), sin], axis=0)
    return jnp.tile(cos, (1, 2)), jnp.tile(sin, (1, 2))


def _block_diag(w_pool):
    L, G = w_pool.shape[:2]
    eye = jnp.eye(G, dtype=w_pool.dtype)
    return (w_pool[:, :, :, None, :] * eye[None, :, None, :, None]).reshape(L, POOL_WIDTH, POOL_WIDTH)


def _qk_gains(small):
    qn = jnp.stack([small["q_norm_a"], small["q_norm_c"]], axis=1)[:, :, None, :]
    kn = jnp.stack([small["k_norm_a"], small["k_norm_c"]], axis=1)[:, :, None, :]
    L = qn.shape[0]
    rows = jnp.concatenate([jnp.broadcast_to(qn, (L, 2, N_HEADS, HEAD_DIM)), jnp.broadcast_to(kn, (L, 2, N_KV, HEAD_DIM)),
                            jnp.ones((L, 2, N_KV, HEAD_DIM), F32)], axis=2)
    return rows.reshape(L, 2, 1, QKV_WIDTH)


def _local_step(x, c, ctx, c_ctx, small, gw, target, rider=None, overlap=False):
    gw = list(gw)
    B, S, D = x.shape
    N = ctx.shape[1]
    L = small["norm1"].shape[0]
    Tp = N + S
    T = B * Tp
    TR = N
    P = Tp // N
    rows16 = 16
    assert N % Q_BLOCK == 0 and S % N == 0 and B + 1 <= rows16
    TM = _tile(T, (2304, 1536, 1024, 768, 512, 384, 256, 128))
    TM_WIDE = _tile(T, (1536, 1024, 768, 512, 384, 256, 128))
    TMG = _tile(T, (1024, 768, 512, 384, 256, 128))

    X = jnp.concatenate([ctx, x], axis=1).reshape(T, D)
    cc = jnp.concatenate([c, c_ctx[None], jnp.zeros((rows16 - B - 1, D), F32)], axis=0)
    s_rows = _silu_rows(cc, "silu_rows")
    cos, sin = _rope_tables(N, S)
    all_gains = _qk_gains(small)
    all_w_bd = _block_diag(small["w_pool"]).astype(BF16)

    def weights(l):
        g = gw[l]
        return dict(
            ada=_Opnd(g["w_ada"], "bcols"), w_in=_Opnd(g["w_in"], "bcols"),
            a=_Opnd(g["w_br_a"], "bcols"), b=_Opnd(g["w_br_b"], "bcols"), c=_Opnd(g["w_br_c"], "bcols"),
            out=_Opnd(g["w_out"], "brows"), mlp1=_Opnd(g["w_mlp1"], "bcols"), mlp2=_Opnd(g["w_mlp2"], "brows"))

    IN = weights(0)["w_in"].shape[1]
    DFF = weights(0)["mlp1"].shape[1]
    tn_in = _tile(IN // N_CHIPS, (1152, 768, 512, 384, 256, 128))
    tn_ff = _tile(DFF // N_CHIPS, (1024, 512, 256, 128))
    tn_ada = _tile(6 * D // N_CHIPS, (1536, 768, 512, 256, 128))
    tn_d = D // N_CHIPS
    tk_d = _tile(D, (512,))
    tk_tok = _tile(T, (2304, 1536, 1024, 768, 512, 384, 256))

    saved = []
    xin, pending = X, None
    for l in range(L):
        W = weights(l)
        b_ada = small["b_ada"][l].reshape(1, 6 * D)
        mod = _matmul(s_rows, W["ada"], "nn", tm=rows16, tn=tn_ada, tk=D, name=f"ada_fwd{l}",
                      epilogue=lambda acc, b: (acc + b,), extras=[(b_ada, (1, tn_ada), lambda m, n: (0, n))])
        modtab = mod.reshape(rows16, 1, 6 * D)
        gains = all_gains[l]
        w_bd = all_w_bd[l]
        p_scale = small["pool_scale"][l].reshape(1, POOL_WIDTH)
        sink = small["sink_c"][l]

        x0, h1 = _res_norm(xin, pending, modtab, 0, 1, small["norm1"][l][None], TR=TR, P=P, name=f"norm1_fwd{l}")
        z = _matmul(h1, W["w_in"], "nn", tm=TM, tn=tn_in, tk=D, name=f"in_proj{l}")
        q2, k2, v2 = _qk_prep(z, gains, cos, sin, TR=TR, P=P, name=f"qk_prep{l}")
        riding = rider if l == 0 else None
        oa, oa32, lse_a, *landed = _attn_fwd(q2, k2, v2, None, branch=0, B=B, n_ctx=N, window=False,
                                             name=f"attn_a_fwd{l}", rider=riding)
        if riding is not None:
            gw[riding.layer] = dict(zip(BIG_NAMES, landed))
        oc, oc32, lse_c = _attn_fwd(q2, k2, v2, sink, branch=1, B=B, n_ctx=N, window=True, name=f"attn_c_fwd{l}")
        pooled, ob = _pool_fwd(z, w_bd, p_scale, B=B, Tp=Tp, n_ctx=N, name=f"pool_fwd{l}")
        y = _merge_fwd(oa, ob, oc, z, W["a"], W["b"], W["c"], D=D, TR=TMG, name=f"merge_fwd{l}")
        ao = _matmul(y, W["out"], "nn", tm=TM, tn=D, tk=tn_d, name=f"out_proj{l}")
        x1, h2 = _res_norm(x0, (ao, modtab, 2), modtab, 3, 4, small["norm2"][l][None], TR=TR, P=P, name=f"norm2_fwd{l}")
        a_pre, r_act = _matmul(h2, W["mlp1"], "nn", tm=TM_WIDE, tn=tn_ff, tk=D, name=f"mlp1_fwd{l}", out_dtypes=(F32, BF16),
                               epilogue=lambda acc: (acc, jnp.square(jnp.maximum(acc, 0.0))))
        mo = _matmul(r_act, W["mlp2"], "nn", tm=TM, tn=D, tk=tn_ff, name=f"mlp2_fwd{l}")
        saved.append(dict(modtab=modtab, gains=gains, w_bd=w_bd, p_scale=p_scale, sink=sink, x0=x0, h1=h1, z=z,
                          q2=q2, k2=k2, v2=v2, oa=oa, ob=ob, oc=oc, oa32=oa32, oc32=oc32, lse_a=lse_a, lse_c=lse_c,
                          pooled=pooled, y=y, ao=ao,
                          x1=x1, h2=h2, a_pre=a_pre, r_act=r_act, mo=mo))
        xin, pending = x1, (mo, modtab, 5)

    dxo, loss, d_mo, dg2 = _loss_head(xin, pending[0], pending[1], 5, target.reshape(B * S, D), TR=TR, P=P, name="loss_head")

    big = {k: [None] * L for k in BIG_NAMES}
    big16 = {k: [None] * L for k in BIG_NAMES}
    sm = {k: [None] * L for k in ("b_ada", "norm1", "norm2", "q_norm_a", "k_norm_a", "q_norm_c", "k_norm_c",
                                   "sink_c", "w_pool", "pool_scale")}

    def dw(key, l, a, b, *, tm, tn, name, tk=tk_tok, blocked=True):
        outs = _matmul(a, b, "tn", tm=tm, tn=tn, tk=tk, name=name, out_dtypes=(F32, BF16), out_blocked=blocked)
        if not blocked:
            outs = [o.reshape(N_CHIPS, o.shape[0] // N_CHIPS, o.shape[1]) for o in outs]
        big[key][l], big16[key][l] = outs
    d_cctx = jnp.zeros((D,), F32)
    for l in reversed(range(L)):
        W, sv = weights(l), saved[l]
        modtab = sv["modtab"]
        ride_now = overlap and l == L - 2
        if ride_now:
            early = _LayerReduce(l + 1, [big[k][l + 1] for k in BIG_NAMES], [big16[k][l + 1] for k in BIG_NAMES])
        d_a = _matmul(d_mo, W["mlp2"], "nt", tm=TM_WIDE, tn=tn_ff, tk=D, name=f"mlp2_bwd{l}", out_dtypes=(BF16,),
                      epilogue=lambda acc, a: (acc * (2.0 * jnp.maximum(a, 0.0)),),
                      extras=[(sv["a_pre"], (TM_WIDE, tn_ff), lambda m, n: (m, n))],
                      rider=early.to_worker if ride_now else None)
        if ride_now:
            d_a, landed = d_a
            early_send = early.add(landed)
        dw("w_mlp2", l, sv["r_act"], d_mo, tm=tk_d, tn=D, name=f"mlp2_dw{l}", blocked=False)
        d_h2 = _matmul(d_a, W["mlp1"], "nt", tm=TM, tn=D, tk=tn_ff, name=f"mlp1_bwd{l}")
        dw("w_mlp1", l, sv["h2"], d_a, tm=tk_d, tn=tn_ff, name=f"mlp1_dw{l}")
        dx1, dsh2, dsc2, dn2, d_ao, dg1 = _norm_bwd(sv["x1"], d_h2, dxo, modtab, 4, small["norm2"][l][None],
                                                    (sv["ao"], modtab, 2), TR=TR, P=P, name=f"norm2_bwd{l}")
        d_y = _matmul(d_ao, W["out"], "nt", tm=TM, tn=tn_d, tk=D, name=f"out_bwd{l}")
        dw("w_out", l, sv["y"], d_ao, tm=tk_d, tn=D, name=f"out_dw{l}", blocked=False)
        d_pa, d_pb, d_pc, d_ga, d_gb, d_gc = _merge_bwd(d_y, sv["oa"], sv["ob"], sv["oc"], sv["z"], W["a"], W["b"], W["c"],
                                                        D=D, TR=TMG, name=f"merge_bwd{l}")
        d_oa = _matmul(d_pa, W["a"], "nt", tm=TM, tn=Q_WIDTH, tk=tn_d, name=f"br_a_bwd{l}", out_dtypes=(BF16,))
        d_ob = _matmul(d_pb, W["b"], "nt", tm=TM, tn=POOL_WIDTH, tk=tn_d, name=f"br_b_bwd{l}")
        d_oc = _matmul(d_pc, W["c"], "nt", tm=TM, tn=Q_WIDTH, tk=tn_d, name=f"br_c_bwd{l}", out_dtypes=(BF16,))
        dw("w_br_a", l, sv["oa"], d_pa, tm=Q_WIDTH, tn=tn_d, name=f"br_a_dw{l}")
        dw("w_br_b", l, sv["ob"], d_pb, tm=POOL_WIDTH, tn=tn_d, name=f"br_b_dw{l}")
        dw("w_br_c", l, sv["oc"], d_pc, tm=Q_WIDTH, tn=tn_d, name=f"br_c_dw{l}")
        d_u, d_wbd, d_ps = _pool_bwd(d_ob, sv["pooled"], sv["w_bd"], sv["p_scale"], B=B, Tp=Tp, n_ctx=N, name=f"pool_bwd{l}")
        dqa, dka, dva, *arrived = _attn_bwd(sv["q2"], sv["k2"], sv["v2"], d_oa, sv["oa32"], sv["lse_a"], None, branch=0,
                                            B=B, n_ctx=N, window=False, name=f"attn_a_bwd{l}",
                                            rider=early_send if ride_now else None)
        if ride_now:
            early.from_chips = arrived
        dqc, dkc, dvc, dsink = _attn_bwd(sv["q2"], sv["k2"], sv["v2"], d_oc, sv["oc32"], sv["lse_c"], sv["sink"],
                                         branch=1, B=B, n_ctx=N, window=True, name=f"attn_c_bwd{l}")
        dz_a, dgains_a = _qk_prep_bwd(sv["z"], dqa, dka, dva, sv["gains"], cos, sin, branch=0, TR=TR, P=P,
                                      name=f"qk_prep_a_bwd{l}")
        dz_c, dgains_c = _qk_prep_bwd(sv["z"], dqc, dkc, dvc, sv["gains"], cos, sin, branch=1, TR=TR, P=P,
                                      name=f"qk_prep_c_bwd{l}")
        dz = jnp.concatenate([dz_a, dz_c, d_u, d_ga, d_gb, d_gc], axis=1)
        d_h1 = _matmul(dz, W["w_in"], "nt", tm=TM, tn=D, tk=tn_in, name=f"in_bwd{l}")
        dw("w_in", l, sv["h1"], dz, tm=tk_d, tn=tn_in, name=f"in_dw{l}")
        below = (saved[l - 1]["mo"], saved[l - 1]["modtab"], 5) if l > 0 else None
        dx0, dsh1, dsc1, dn1, *lower = _norm_bwd(sv["x0"], d_h1, dx1, modtab, 1, small["norm1"][l][None], below,
                                                 TR=TR, P=P, name=f"norm1_bwd{l}")
        this_dg2 = dg2
        if l > 0:
            d_mo, dg2 = lower

        dm_groups = jnp.concatenate([dsh1, dsc1, dg1, dsh2, dsc2, this_dg2], axis=-1).reshape(B, 2, 6 * D)
        dm = jnp.concatenate([dm_groups[:, 1], jnp.sum(dm_groups[:, 0], axis=0, keepdims=True),
                              jnp.zeros((rows16 - B - 1, 6 * D), F32)], axis=0)
        dm_bf = dm.astype(BF16)
        d_s = _matmul(dm_bf, W["ada"], "nt", tm=rows16, tn=D, tk=tn_ada, name=f"ada_bwd{l}")
        dw("w_ada", l, s_rows, dm_bf, tm=tk_d, tn=tn_ada, tk=rows16, name=f"ada_dw{l}")
        db_ada, dcc = _ada_bwd_rows(dm, d_s, cc, f"ada_rows_bwd{l}")
        d_cctx = d_cctx + dcc[B]

        sm["b_ada"][l] = db_ada[0]
        sm["norm1"][l] = jnp.sum(dn1, axis=(0, 1))
        sm["norm2"][l] = jnp.sum(dn2, axis=(0, 1))
        dgh = jnp.stack([dgains_a, dgains_c]).reshape(2, QKV_WIDTH // HEAD_DIM, HEAD_DIM)
        sm["q_norm_a"][l] = jnp.sum(dgh[0, :N_HEADS], axis=0)
        sm["k_norm_a"][l] = jnp.sum(dgh[0, N_HEADS:N_HEADS + N_KV], axis=0)
        sm["q_norm_c"][l] = jnp.sum(dgh[1, :N_HEADS], axis=0)
        sm["k_norm_c"][l] = jnp.sum(dgh[1, N_HEADS:N_HEADS + N_KV], axis=0)
        sm["sink_c"][l] = jnp.sum(dsink[:, :N_HEADS, 0], axis=0)
        sm["w_pool"][l] = jnp.stack([d_wbd[g * POOL_CH:(g + 1) * POOL_CH, g * POOL_CH:(g + 1) * POOL_CH]
                                     for g in range(POOL_WIDTH // POOL_CH)])
        sm["pool_scale"][l] = d_ps[0]
        dxo = dx0

    grad_x = dxo.reshape(B, Tp, D)[:, N:]
    small_grads = {k: jnp.stack(v) for k, v in sm.items()}
    small_grads["c_ctx"] = d_cctx
    return loss, grad_x, small_grads, big, big16, (early if overlap else None)


SMALL_NAMES = ("c_ctx", "b_ada", "norm1", "norm2", "q_norm_a", "k_norm_a", "q_norm_c", "k_norm_c", "sink_c",
               "w_pool", "pool_scale")
BIG_NAMES = ("w_ada", "w_in", "w_br_a", "w_br_b", "w_br_c", "w_out", "w_mlp1", "w_mlp2")
WEIGHT_NAMES = ("c_ctx", "w_ada", "b_ada", "norm1", "norm2", "w_in", "q_norm_a", "k_norm_a", "q_norm_c", "k_norm_c",
                "sink_c", "w_pool", "pool_scale", "w_br_a", "w_br_b", "w_br_c", "w_out", "w_mlp1", "w_mlp2")


def _pack(parts, rows):
    flat = jnp.concatenate([p.reshape(-1).astype(F32) for p in parts])
    return jnp.pad(flat, (0, rows * LANES - flat.shape[0])).reshape(rows, LANES)


def _unpack(packed, like):
    flat, out, at = packed.reshape(-1), [], 0
    for p in like:
        out.append(flat[at:at + p.size].reshape(p.shape))
        at += p.size
    return out


def _split_by_bytes(arrays):
    load, owner = [0, 0], [0] * len(arrays)
    for w in sorted(range(len(arrays)), key=lambda w: -arrays[w].size):
        owner[w] = 0 if load[0] <= load[1] else 1
        load[owner[w]] += arrays[w].size
    return owner


class _LayerReduce:
    def __init__(self, layer, partials, partials16):
        self.layer, self.partials = layer, list(partials)
        self.workers = _split_by_bytes(self.partials)
        self.to_worker = _ToSibling([g.reshape(-1, g.shape[-1]) for g in partials16], [1 - wk for wk in self.workers])
        x, y, c = _place()
        self.core = c.astype(jnp.int32).reshape(1)
        self.chip = (2 * x + y).astype(jnp.int32).reshape(1)

    def add(self, landed):
        sums = [_add_landed(g.reshape(-1, g.shape[-1]), r, self.core, wk, f"grads{self.layer}_add_sibling_{k}")
                for k, g, r, wk in zip(BIG_NAMES, self.partials, landed, self.workers)]
        self.in_chip = [h.reshape(g.shape) for g, (h, _) in zip(self.partials, sums)]
        return _ChipSend([h.reshape(g.shape) for g, (_, h) in zip(self.partials, sums)], self.workers)

    def sum(self, from_chips):
        return [_sum_chips(h, r, self.chip, self.core, wk, f"grads{self.layer}_sum_chips_{k}")
                for k, h, r, wk in zip(BIG_NAMES, self.in_chip, from_chips, self.workers)]


def kernel(x, c, ctx, c_ctx, w_ada, b_ada, norm1, norm2, w_in, q_norm_a, k_norm_a, q_norm_c, k_norm_c, sink_c, w_pool, pool_scale, w_br_a, w_br_b, w_br_c, w_out, w_mlp1, w_mlp2, loss_target, m_c_ctx, m_w_ada, m_b_ada, m_norm1, m_norm2, m_w_in, m_q_norm_a, m_k_norm_a, m_q_norm_c, m_k_norm_c, m_sink_c, m_w_pool, m_pool_scale, m_w_br_a, m_w_br_b, m_w_br_c, m_w_out, m_w_mlp1, m_w_mlp2, v_c_ctx, v_w_ada, v_b_ada, v_norm1, v_norm2, v_w_in, v_q_norm_a, v_k_norm_a, v_q_norm_c, v_k_norm_c, v_sink_c, v_w_pool, v_pool_scale, v_w_br_a, v_w_br_b, v_w_br_c, v_w_out, v_w_mlp1, v_w_mlp2):
    given = dict(locals())
    w = {k: given[k] for k in WEIGHT_NAMES}
    m = {k: given["m_" + k] for k in WEIGHT_NAMES}
    v = {k: given["v_" + k] for k in WEIGHT_NAMES}

    shards = [w[k].astype(BF16) for k in BIG_NAMES]
    assert all(s.shape[0] == 2 for s in shards)
    first_layer = dict(zip(BIG_NAMES, _ride_alone(_LayerGather(shards, 0), "gather_weights0")))
    small = {k: w[k] for k in SMALL_NAMES}
    loss_part, grad_x, small_grads, big_grads, big_grads16, early = _local_step(
        x, c, ctx, c_ctx, small, [first_layer, None], loss_target, rider=_LayerGather(shards, 1), overlap=True)

    late = _LayerReduce(0, [big_grads[k][0] for k in BIG_NAMES], [big_grads16[k][0] for k in BIG_NAMES])
    send = late.add(_ride_alone(late.to_worker, "grads0_to_sibling"))
    reduced = [late.sum(_ride_alone(send, "grads0_to_chips")), early.sum(early.from_chips)]
    n_big = len(BIG_NAMES)
    shared = _ride_alone(_ToSibling(reduced[0] + reduced[1], late.workers + early.workers), "grads_share")
    grads, deltas, new_m, new_v = {}, {}, {}, {}
    for i, k in enumerate(BIG_NAMES):
        assert late.workers[i] == early.workers[i]
        grads[k], deltas[k], new_m[k], new_v[k] = _adamw(
            w[k], (reduced[0][i], reduced[1][i]), (shared[i], shared[n_big + i]), m[k], v[k], late.core, late.workers[i],
            f"adamw_{k}")

    sizes = sum(w[k].size for k in SMALL_NAMES) + LANES
    rows = -(-sizes // (8 * LANES)) * 8
    parts = _gather_small(_pack([small_grads[k] for k in SMALL_NAMES] + [loss_part[0]], rows), "gather_small")
    zero = jnp.zeros((LANES,), F32)
    packed = [_pack([t[k] for k in SMALL_NAMES] + [zero], rows) for t in (w, m, v)]
    outs = _adamw_small(packed[0], parts.reshape(8, rows, LANES), packed[1], packed[2], "adamw_small")
    like = [w[k] for k in SMALL_NAMES] + [zero]
    for store, packed_out in zip((grads, deltas, new_m, new_v), outs):
        pieces = _unpack(packed_out, like)
        for k, piece in zip(SMALL_NAMES, pieces):
            store[k] = piece
        if store is grads:
            loss = pieces[-1][0]

    return (loss, grad_x, *[grads[k] for k in WEIGHT_NAMES], *[deltas[k] for k in WEIGHT_NAMES],
            *[new_m[k] for k in WEIGHT_NAMES], *[new_v[k] for k in WEIGHT_NAMES])
```

```python
import functools

import jax
import jax.numpy as jnp
from jax import lax
from jax.experimental import pallas as pl
from jax.experimental.pallas import tpu as pltpu

F32 = jnp.float32
BF16 = jnp.bfloat16

HEAD_DIM = 64
GRID_W = 64
AXIS_DIM = HEAD_DIM // 2
ROPE_THETA = 10000.0
N_HEADS = 6
N_KV = 2
N_GROUP = N_HEADS // N_KV
POOL_CH = 64
POOL_WIDTH = 256
POOL_WINDOWS = (2, 4, 8, 16)
WINDOW = 128
Q_BLOCK = 128
Q_WIDTH = N_HEADS * HEAD_DIM
KV_WIDTH = N_KV * HEAD_DIM
GATE_COL = 2 * (Q_WIDTH + 2 * KV_WIDTH) + POOL_WIDTH
U_COL = 2 * (Q_WIDTH + 2 * KV_WIDTH)
EPS = 1e-6
NEG = -1e30
ADAM_LR = 0.001
ADAM_B1 = 0.9
ADAM_B2 = 0.999
ADAM_EPS = 1e-08
ADAM_WD = 0.01
ADAM_STEP = 10

N_CHIPS = 4
LANES = 128
POOL_PAD = 16
VMEM_LIMIT = 48 * 1024 * 1024
MESH = pl.DeviceIdType.MESH
ANY = pl.BlockSpec(memory_space=pl.ANY)


def _params(sem):
    return pltpu.CompilerParams(dimension_semantics=sem, vmem_limit_bytes=VMEM_LIMIT)


def _sds(shape, dtype):
    return jax.ShapeDtypeStruct(tuple(shape), dtype)


class _Opnd:
    def __init__(self, arr, kind="plain"):
        self.arr, self.kind = arr, kind

    @property
    def shape(self):
        a = self.arr
        if self.kind == "plain":
            return a.shape
        if self.kind == "bcols":
            return (a.shape[1], N_CHIPS * a.shape[2])
        return (N_CHIPS * a.shape[1], a.shape[2])

    def spec(self, tr, tc, fn):
        a = self.arr
        if self.kind == "plain":
            return pl.BlockSpec((tr, tc), lambda *g: fn(*g))
        if self.kind == "bcols":
            assert a.shape[2] % tc == 0, (a.shape, tc)
            per = a.shape[2] // tc

            def im(*g):
                ri, ci = fn(*g)
                return (ci // per, ri, ci % per)
            return pl.BlockSpec((None, tr, tc), im)
        assert a.shape[1] % tr == 0, (a.shape, tr)
        per = a.shape[1] // tr

        def im(*g):
            ri, ci = fn(*g)
            return (ri // per, ri % per, ci)
        return pl.BlockSpec((None, tr, tc), im)


def _matmul(a, b, mode, *, tm, tn, tk, name, out_dtypes=(F32,), epilogue=None, extras=(), out_blocked=False, rider=None):
    if not isinstance(a, _Opnd):
        a = _Opnd(a)
    if not isinstance(b, _Opnd):
        b = _Opnd(b)
    if mode == "nn":
        (M, K), (K2, N) = a.shape, b.shape
        a_spec = a.spec(tm, tk, lambda m, n, k: (m, k))
        b_spec = b.spec(tk, tn, lambda m, n, k: (k, n))
        dims = (((1,), (0,)), ((), ()))
    elif mode == "nt":
        (M, K), (N, K2) = a.shape, b.shape
        a_spec = a.spec(tm, tk, lambda m, n, k: (m, k))
        b_spec = b.spec(tn, tk, lambda m, n, k: (n, k))
        dims = (((1,), (1,)), ((), ()))
    else:
        (K, M), (K2, N) = a.shape, b.shape
        a_spec = a.spec(tk, tm, lambda m, n, k: (k, m))
        b_spec = b.spec(tk, tn, lambda m, n, k: (k, n))
        dims = (((0,), (0,)), ((), ()))
    assert K == K2 and M % tm == 0 and N % tn == 0 and K % tk == 0, (name, M, N, K, K2, tm, tn, tk)
    nk = K // tk
    n_extra = len(extras)
    n_out = len(out_dtypes)
    extra_specs = [pl.BlockSpec(bs, functools.partial(lambda m, n, k, f: f(m, n), f=f)) for (_, bs, f) in extras]
    if out_blocked:
        assert (N // N_CHIPS) % tn == 0
        per = (N // N_CHIPS) // tn
        out_shape = [_sds((N_CHIPS, M, N // N_CHIPS), dt) for dt in out_dtypes]
        out_specs = [pl.BlockSpec((None, tm, tn), lambda m, n, k: (n // per, m, n % per)) for _ in out_dtypes]
    else:
        out_shape = [_sds((M, N), dt) for dt in out_dtypes]
        out_specs = [pl.BlockSpec((tm, tn), lambda m, n, k: (m, n)) for _ in out_dtypes]

    in_place = nk > 1 and epilogue is None and out_dtypes[0] == F32

    grid = (M // tm, N // tn, nk)
    own_scratch = [pltpu.VMEM((tm, tn), F32)] if nk > 1 and not in_place else []

    def body(*refs):
        refs, finish_ride = _ride(rider, refs, 2 + n_extra, n_out, len(own_scratch), grid)
        a_ref, b_ref = refs[0], refs[1]
        extra_refs = refs[2:2 + n_extra]
        out_refs = refs[2 + n_extra:2 + n_extra + n_out]
        acc_ref = out_refs[0] if in_place else (refs[2 + n_extra + n_out] if nk > 1 else None)
        k = pl.program_id(2)
        prod = lax.dot_general(a_ref[...].astype(BF16), b_ref[...].astype(BF16), dims, preferred_element_type=F32)

        def finish(acc):
            outs = epilogue(acc, *[r[...] for r in extra_refs]) if epilogue is not None else (acc,) * n_out
            for o_ref, o in zip(out_refs, outs):
                o_ref[...] = o.astype(o_ref.dtype)

        if nk == 1:
            finish(prod)
        elif in_place:
            @pl.when(k == 0)
            def _():
                acc_ref[...] = prod

            @pl.when(k > 0)
            def _():
                acc_ref[...] += prod

            if n_out > 1:
                @pl.when(k == nk - 1)
                def _():
                    for o_ref in out_refs[1:]:
                        o_ref[...] = acc_ref[...].astype(o_ref.dtype)
        else:
            @pl.when(k == 0)
            def _():
                acc_ref[...] = prod

            @pl.when(k > 0)
            def _():
                acc_ref[...] += prod

            @pl.when(k == nk - 1)
            def _():
                finish(acc_ref[...])

        finish_ride()

    ins, in_specs, out_specs, out_shape, scratch = _hitch(
        rider, [a.arr, b.arr] + [e[0] for e in extras], [a_spec, b_spec] + extra_specs, out_specs, out_shape, own_scratch)
    outs = pl.pallas_call(
        body, name=name, grid=grid, in_specs=in_specs, out_specs=out_specs, out_shape=out_shape, scratch_shapes=scratch,
        compiler_params=_params(("arbitrary",) * 3 if rider is not None else ("parallel", "parallel", "arbitrary")),
    )(*ins)
    if rider is not None:
        return (outs[0] if n_out == 1 else outs[:n_out]), outs[n_out:]
    return outs[0] if n_out == 1 else outs


def _tile(n, cands):
    for t in cands:
        if n % t == 0:
            return t
    return n


def _grp(i, P):
    return 2 * (i // P) + jnp.minimum(i % P, 1)


def _mod_spec(D, P, part, B):
    return pl.BlockSpec((1, 1, D), lambda i: (jnp.where(i % P == 0, B, i // P), 0, part))


def _res_norm(x, pending, modtab, shift_part, scale_part, gain, *, TR, P, name):
    T, D = x.shape
    row = pl.BlockSpec((TR, D), lambda i: (i, 0))
    has_branch = pending is not None
    ins, specs = [x], [row]
    if has_branch:
        branch, gate_tab, gate_part = pending
        ins += [branch, gate_tab]
        specs += [row, _mod_spec(D, P, gate_part, T // (TR * P))]
    ins += [modtab, modtab, gain]
    specs += [_mod_spec(D, P, shift_part, T // (TR * P)), _mod_spec(D, P, scale_part, T // (TR * P)), pl.BlockSpec((1, D), lambda i: (0, 0))]

    def body(*refs):
        if has_branch:
            x_ref, br_ref, g_ref, sh_ref, sc_ref, gn_ref, xo_ref, h_ref = refs
            xv = x_ref[...] + g_ref[0] * br_ref[...]
        else:
            x_ref, sh_ref, sc_ref, gn_ref, xo_ref, h_ref = refs
            xv = x_ref[...]
        xo_ref[...] = xv
        y = xv * lax.rsqrt(jnp.mean(xv * xv, axis=-1, keepdims=True) + EPS) * gn_ref[...]
        h_ref[...] = (y * (1.0 + sc_ref[0]) + sh_ref[0]).astype(BF16)

    return pl.pallas_call(
        body, name=name, grid=(T // TR,), in_specs=specs, out_specs=[row, row],
        out_shape=[_sds((T, D), F32), _sds((T, D), BF16)], compiler_params=_params(("parallel",)),
    )(*ins)


def _norm_bwd(x, dh, dres, modtab, scale_part, gain, below, *, TR, P, name):
    T, D = x.shape
    G = 2 * (T // (TR * P))
    row = pl.BlockSpec((TR, D), lambda i: (i, 0))
    acc = pl.BlockSpec((1, 1, D), lambda i: (_grp(i, P), 0, 0))
    has_below = below is not None

    def body(*refs):
        if has_below:
            x_ref, dh_ref, dres_ref, sc_ref, gn_ref, br_ref, g_ref, dx_ref, dsh_ref, dsc_ref, dgn_ref, db_ref, dg_ref = refs
        else:
            x_ref, dh_ref, dres_ref, sc_ref, gn_ref, dx_ref, dsh_ref, dsc_ref, dgn_ref = refs
        r = pl.program_id(0) % P
        xv, dhv, gn = x_ref[...], dh_ref[...], gn_ref[...]
        rstd = lax.rsqrt(jnp.mean(xv * xv, axis=-1, keepdims=True) + EPS)
        xhat = xv * rstd
        dn = dhv * (1.0 + sc_ref[0])
        dxhat = dn * gn
        dxv = dres_ref[...] + rstd * (dxhat - xhat * jnp.mean(dxhat * xhat, axis=-1, keepdims=True))
        dx_ref[...] = dxv
        parts = [jnp.sum(dhv, axis=0, keepdims=True), jnp.sum(dhv * (xhat * gn), axis=0, keepdims=True),
                 jnp.sum(dn * xhat, axis=0, keepdims=True)]
        outs = [dsh_ref, dsc_ref, dgn_ref]
        if has_below:
            db_ref[...] = (dxv * g_ref[0]).astype(BF16)
            parts.append(jnp.sum(dxv * br_ref[...], axis=0, keepdims=True))
            outs.append(dg_ref)

        @pl.when(r <= 1)
        def _():
            for o_ref, part in zip(outs, parts):
                o_ref[0] = part

        @pl.when(r > 1)
        def _():
            for o_ref, part in zip(outs, parts):
                o_ref[0] += part

    ins = [x, dh, dres, modtab, gain]
    in_specs = [row, row, row, _mod_spec(D, P, scale_part, T // (TR * P)), pl.BlockSpec((1, D), lambda i: (0, 0))]
    out_specs, out_shape = [row, acc, acc, acc], [_sds((T, D), F32)] + [_sds((G, 1, D), F32)] * 3
    if has_below:
        branch, gate_tab, gate_part = below
        ins += [branch, gate_tab]
        in_specs += [row, _mod_spec(D, P, gate_part, T // (TR * P))]
        out_specs += [row, acc]
        out_shape += [_sds((T, D), BF16), _sds((G, 1, D), F32)]
    return pl.pallas_call(body, name=name, grid=(T // TR,), in_specs=in_specs, out_specs=out_specs, out_shape=out_shape,
                          compiler_params=_params(("arbitrary",)))(*ins)


def _loss_head(x, branch, modtab, gate_part, target, *, TR, P, name):
    T, D = x.shape
    row = pl.BlockSpec((TR, D), lambda i: (i, 0))
    tgt = pl.BlockSpec((TR, D), lambda i: ((i // P) * (P - 1) + jnp.maximum(i % P - 1, 0), 0))
    one = pl.BlockSpec((1, LANES), lambda i: (0, 0))

    G = 2 * (T // (TR * P))
    acc = pl.BlockSpec((1, 1, D), lambda i: (_grp(i, P), 0, 0))

    def body(x_ref, br_ref, g_ref, t_ref, dy_ref, loss_ref, db_ref, dg_ref):
        i = pl.program_id(0)
        r = i % P

        @pl.when(i == 0)
        def _():
            loss_ref[...] = jnp.zeros_like(loss_ref)

        @pl.when(r == 0)
        def _():
            dy_ref[...] = jnp.zeros_like(dy_ref)
            db_ref[...] = jnp.zeros_like(db_ref)
            dg_ref[...] = jnp.zeros_like(dg_ref)

        @pl.when(r > 0)
        def _():
            brv, g = br_ref[...], g_ref[0]
            err = x_ref[...] + g * brv - t_ref[...]
            dy = err / D
            dy_ref[...] = dy
            db_ref[...] = (dy * g).astype(BF16)
            part = jnp.sum(dy * brv, axis=0, keepdims=True)
            per_tok = jnp.mean(err * err, axis=-1, keepdims=True)
            loss_ref[...] += 0.5 * jnp.sum(per_tok, axis=0, keepdims=True)

            @pl.when(r == 1)
            def _():
                dg_ref[0] = part

            @pl.when(r > 1)
            def _():
                dg_ref[0] += part

    return pl.pallas_call(
        body, name=name, grid=(T // TR,), in_specs=[row, row, _mod_spec(D, P, gate_part, T // (TR * P)), tgt],
        out_specs=[row, one, row, acc],
        out_shape=[_sds((T, D), F32), _sds((1, LANES), F32), _sds((T, D), BF16), _sds((G, 1, D), F32)],
        compiler_params=_params(("arbitrary",)),
    )(x, branch, modtab, target)


QKV_WIDTH = Q_WIDTH + 2 * KV_WIDTH
QK_NORMED = 4


def _seg_mean(v):
    lane = lax.broadcasted_iota(jnp.int32, v.shape, 1)
    lo = lane < HEAD_DIM
    s0 = jnp.sum(jnp.where(lo, v, 0.0), axis=-1, keepdims=True)
    s1 = jnp.sum(jnp.where(lo, 0.0, v), axis=-1, keepdims=True)
    return jnp.where(lo, s0, s1) * (1.0 / HEAD_DIM)


def _pair_swap(v):
    lane = lax.broadcasted_iota(jnp.int32, v.shape, 1)
    return jnp.where((lane & 1) == 0, pltpu.roll(v, LANES - 1, 1), pltpu.roll(v, 1, 1))


def _chunk(c):
    return slice(c * LANES, (c + 1) * LANES)


def _qk_prep(z, gains, cos, sin, *, TR, P, name):
    T = z.shape[0]

    def body(z_ref, g_ref, c_ref, s_ref, q_ref, k_ref, v_ref):
        cs, sn = c_ref[...], s_ref[...]
        for ch in range(QK_NORMED):
            xv = z_ref[:, _chunk(ch)]
            y = xv * lax.rsqrt(_seg_mean(xv * xv) + EPS) * g_ref[0, :, _chunk(ch)]
            out = (y * cs + _pair_swap(y) * sn).astype(BF16)
            if ch < QK_NORMED - 1:
                q_ref[:, _chunk(ch)] = out
            else:
                k_ref[...] = out
        v_ref[...] = z_ref[:, _chunk(QK_NORMED)].astype(BF16)

    def out(width):
        return pl.BlockSpec((None, TR, width), lambda i, j: (j, i, 0))
    return pl.pallas_call(
        body, name=name, grid=(T // TR, 2),
        in_specs=[pl.BlockSpec((TR, QKV_WIDTH), lambda i, j: (i, j)),
                  pl.BlockSpec((1, 1, QKV_WIDTH), lambda i, j: (j, 0, 0)),
                  pl.BlockSpec((TR, LANES), lambda i, j: (i % P, 0)),
                  pl.BlockSpec((TR, LANES), lambda i, j: (i % P, 0))],
        out_specs=[out(Q_WIDTH), out(KV_WIDTH), out(KV_WIDTH)],
        out_shape=[_sds((2, T, Q_WIDTH), BF16), _sds((2, T, KV_WIDTH), BF16), _sds((2, T, KV_WIDTH), BF16)],
        compiler_params=_params(("parallel", "parallel")),
    )(z, gains, cos, sin)


def _qk_prep_bwd(z, dq, dk, dv, gains, cos, sin, *, branch, TR, P, name):
    T = z.shape[0]
    nt = T // TR

    def body(z_ref, dq_ref, dk_ref, dv_ref, g_ref, c_ref, s_ref, dz_ref, dg_ref):
        i = pl.program_id(0)
        cs, sn = c_ref[...], s_ref[...]
        parts = []
        for ch in range(QK_NORMED):
            xv, g = z_ref[:, _chunk(ch)], g_ref[0, :, _chunk(ch)]
            dout = dq_ref[:, _chunk(ch)] if ch < QK_NORMED - 1 else dk_ref[...]
            dy = dout * cs + _pair_swap(dout * sn)
            rstd = lax.rsqrt(_seg_mean(xv * xv) + EPS)
            xhat = xv * rstd
            dxhat = dy * g
            dz_ref[:, _chunk(ch)] = (rstd * (dxhat - xhat * _seg_mean(dxhat * xhat))).astype(BF16)
            parts.append(jnp.sum(dy * xhat, axis=0, keepdims=True))
        dz_ref[:, _chunk(QK_NORMED)] = dv_ref[...].astype(BF16)
        parts.append(jnp.zeros((1, LANES), F32))
        part = jnp.concatenate(parts, axis=1)

        @pl.when(i == 0)
        def _():
            dg_ref[0] = part

        @pl.when(i > 0)
        def _():
            dg_ref[0] += part

    def rows(width, col=0):
        return pl.BlockSpec((TR, width), lambda i: (i, col))
    return pl.pallas_call(
        body, name=name, grid=(nt,),
        in_specs=[rows(QKV_WIDTH, branch), rows(Q_WIDTH), rows(KV_WIDTH), rows(KV_WIDTH),
                  pl.BlockSpec((1, 1, QKV_WIDTH), lambda i: (branch, 0, 0)),
                  pl.BlockSpec((TR, LANES), lambda i: (i % P, 0)),
                  pl.BlockSpec((TR, LANES), lambda i: (i % P, 0))],
        out_specs=[rows(QKV_WIDTH), pl.BlockSpec((1, 1, QKV_WIDTH), lambda i: (0, 0, 0))],
        out_shape=[_sds((T, QKV_WIDTH), BF16), _sds((1, 1, QKV_WIDTH), F32)],
        compiler_params=_params(("arbitrary",)),
    )(z, dq, dk, dv, gains, cos, sin)


NT_DIMS = (((1,), (1,)), ((), ()))
TN_DIMS = (((0,), (0,)), ((), ()))
QROWS = N_GROUP * Q_BLOCK
SCORE_SCALE = HEAD_DIM ** -0.5
BAND = Q_BLOCK + 2 * WINDOW
FWD_LATENT_CHUNK = 256
BWD_LATENT_CHUNK = 1024


def _move_head(block, half_from, half_to):
    lane = lax.broadcasted_iota(jnp.int32, block.shape, 1)
    src = block if half_from == half_to else pltpu.roll(block, HEAD_DIM, 1)
    keep = (lane < HEAD_DIM) if half_to == 0 else (lane >= HEAD_DIM)
    return jnp.where(keep, src, 0.0)


def _stack_heads(lane_block, j):
    pieces = []
    for h in range(N_GROUP * j, N_GROUP * (j + 1)):
        pieces.append(_move_head(lane_block(h // 2), h % 2, j))
    return jnp.concatenate(pieces, axis=0)


def _lane_blocks(ref):
    return lambda m: ref[:, m * LANES:(m + 1) * LANES].astype(F32)


def _unstack_heads(stacked, ref):
    heads = []
    for h in range(N_HEADS):
        j, r = h // N_GROUP, h % N_GROUP
        heads.append(_move_head(stacked[j][r * Q_BLOCK:(r + 1) * Q_BLOCK], j, h % 2))
    for m in range(N_HEADS // 2):
        ref[:, m * LANES:(m + 1) * LANES] = (heads[2 * m] + heads[2 * m + 1]).astype(ref.dtype)


def _key_chunks(i, latent, *, n_ctx, t_all, window, chunk, latent_chunk):
    ctx = [(s, chunk, False) for s in range(0, n_ctx, chunk)]
    if not latent:
        return ctx
    if not window:
        wide = latent_chunk if (t_all - n_ctx) % latent_chunk == 0 else chunk
        return ctx + [(s, wide, False) for s in range(n_ctx, t_all, wide)]
    start = pl.multiple_of(jnp.minimum((i - 1) * Q_BLOCK, t_all - BAND), Q_BLOCK)
    band_chunk = BAND if latent_chunk >= BAND else (chunk if BAND % chunk == 0 else Q_BLOCK)
    return ctx + [(start + s, band_chunk, True) for s in range(0, BAND, band_chunk)]


def _scores(q, k_ref, i, start, size, masked, *, n_ctx):
    s = lax.dot_general(q, k_ref[pl.ds(start, size), :], NT_DIMS, preferred_element_type=F32)
    if masked:
        qpos = (i * Q_BLOCK - n_ctx) + (lax.broadcasted_iota(jnp.int32, (QROWS, size), 0) & (Q_BLOCK - 1))
        kpos = (start - n_ctx) + lax.broadcasted_iota(jnp.int32, (QROWS, size), 1)
        valid = (kpos - qpos <= WINDOW) & (qpos - kpos <= WINDOW) & (kpos >= 0)
        s = jnp.where(valid, s, NEG)
    return s


def _sink_column(sink_ref, j):
    r = lax.broadcasted_iota(jnp.int32, (QROWS, 1), 0)
    s0, s1, s2 = sink_ref[j * N_GROUP], sink_ref[j * N_GROUP + 1], sink_ref[j * N_GROUP + 2]
    return jnp.where(r < Q_BLOCK, s0, jnp.where(r < 2 * Q_BLOCK, s1, s2))


def _attn_specs(Tp, branch):
    nq = Tp // Q_BLOCK
    q_in = pl.BlockSpec((None, Q_BLOCK, Q_WIDTH), lambda b, i: (branch, b * nq + i, 0))
    kv_in = pl.BlockSpec((None, Tp, KV_WIDTH), lambda b, i: (branch, b, 0))
    q_out = pl.BlockSpec((Q_BLOCK, Q_WIDTH), lambda b, i: (b * nq + i, 0))
    kv_out = pl.BlockSpec((Tp, KV_WIDTH), lambda b, i: (b, 0))
    return q_in, kv_in, q_out, kv_out


def _attn_chunk(Tp):
    return 256 if Tp % 256 == 0 else Q_BLOCK


def _attn_fwd(q, k, v, sink, *, branch, B, n_ctx, window, name, rider=None):
    T = q.shape[1]
    Tp = T // B
    nq = Tp // Q_BLOCK
    has_sink = sink is not None
    n_in = 4 if has_sink else 3
    q_in, kv_in, q_out, _ = _attn_specs(Tp, branch)
    lse_spec = pl.BlockSpec((None, N_KV * QROWS, 1), lambda b, i: (b * nq + i, 0, 0))

    def body(*refs):
        refs, finish_ride = _ride(rider, refs, n_in, 3, 0, (B, nq))
        sink_ref = refs.pop(0) if has_sink else None
        q_ref, k_ref, v_ref, o_ref, o32_ref, lse_ref = refs
        i = pl.program_id(1)

        def run(latent):
            outs = []
            for j in range(N_KV):
                qv = (_stack_heads(_lane_blocks(q_ref), j) * SCORE_SCALE).astype(BF16)
                if has_sink:
                    m, l = _sink_column(sink_ref, j), jnp.ones((QROWS, 1), F32)
                else:
                    m, l = jnp.full((QROWS, 1), NEG, F32), jnp.zeros((QROWS, 1), F32)
                acc = jnp.zeros((QROWS, LANES), F32)
                for start, size, masked in _key_chunks(i, latent, n_ctx=n_ctx, t_all=Tp, window=window,
                                                       chunk=_attn_chunk(Tp), latent_chunk=FWD_LATENT_CHUNK):
                    s = _scores(qv, k_ref, i, start, size, masked, n_ctx=n_ctx)
                    m_new = jnp.maximum(m, jnp.max(s, axis=-1, keepdims=True))
                    alpha = jnp.exp(m - m_new)
                    p = jnp.exp(s - m_new)
                    l = l * alpha + jnp.sum(p, axis=-1, keepdims=True)
                    acc = acc * alpha + jnp.dot(p.astype(BF16), v_ref[pl.ds(start, size), :], preferred_element_type=F32)
                    m = m_new
                outs.append(acc * (1.0 / l))
                lse_ref[j * QROWS:(j + 1) * QROWS, :] = m + jnp.log(l)
            _unstack_heads(outs, o_ref)
            _unstack_heads(outs, o32_ref)

        @pl.when(i < n_ctx // Q_BLOCK)
        def _():
            run(False)

        @pl.when(i >= n_ctx // Q_BLOCK)
        def _():
            run(True)

        finish_ride()

    ins, specs = [q, k, v], [q_in, kv_in, kv_in]
    if has_sink:
        ins, specs = [sink] + ins, [pl.BlockSpec(memory_space=pltpu.SMEM)] + specs
    out_specs = [q_out, q_out, lse_spec]
    out_shape = [_sds((T, Q_WIDTH), BF16), _sds((T, Q_WIDTH), F32), _sds((T // Q_BLOCK, N_KV * QROWS, 1), F32)]
    ins, specs, out_specs, out_shape, scratch = _hitch(rider, ins, specs, out_specs, out_shape, [])
    return pl.pallas_call(
        body, name=name, grid=(B, nq), in_specs=specs, out_specs=out_specs, out_shape=out_shape, scratch_shapes=scratch,
        compiler_params=_params(("arbitrary", "arbitrary") if rider is not None else ("parallel", "parallel")),
    )(*ins)


def _attn_bwd(q, k, v, do, o32, lse, sink, *, branch, B, n_ctx, window, name, rider=None):
    T = q.shape[1]
    Tp = T // B
    nq = Tp // Q_BLOCK
    has_sink = sink is not None
    q_in, kv_in, q_out, kv_out = _attn_specs(Tp, branch)
    lse_spec = pl.BlockSpec((None, N_KV * QROWS, 1), lambda b, i: (b * nq + i, 0, 0))
    sink_spec = pl.BlockSpec((None, 8, LANES), lambda b, i: (b, 0, 0))

    def body(*refs):
        refs, finish_ride = _ride(rider, refs, 7 if has_sink else 6, 4 if has_sink else 3, 2, (B, nq))
        if has_sink:
            sink_ref, q_ref, k_ref, v_ref, do_ref, o_ref, lse_ref, dq_ref, dk_ref, dv_ref, ds_ref, dkt_ref, dvt_ref = refs
        else:
            q_ref, k_ref, v_ref, do_ref, o_ref, lse_ref, dq_ref, dk_ref, dv_ref, dkt_ref, dvt_ref = refs
        i = pl.program_id(1)

        @pl.when(i == 0)
        def _():
            dk_ref[...] = jnp.zeros_like(dk_ref)
            dv_ref[...] = jnp.zeros_like(dv_ref)
            if not window:
                dkt_ref[...] = jnp.zeros_like(dkt_ref)
                dvt_ref[...] = jnp.zeros_like(dvt_ref)
            if has_sink:
                ds_ref[...] = jnp.zeros_like(ds_ref)

        def run(latent):
            upd = jnp.zeros((8, LANES), F32)
            do_blocks, o_blocks = _lane_blocks(do_ref), _lane_blocks(o_ref)
            qvs = [(_stack_heads(_lane_blocks(q_ref), j) * SCORE_SCALE).astype(BF16) for j in range(N_KV)]
            dovs = [_stack_heads(do_blocks, j).astype(BF16) for j in range(N_KV)]
            deltas = [jnp.sum(_stack_heads(lambda m: do_blocks(m) * o_blocks(m), j), axis=-1, keepdims=True)
                      for j in range(N_KV)]
            lses = [lse_ref[j * QROWS:(j + 1) * QROWS, :] for j in range(N_KV)]
            q_all, do_all = jnp.concatenate(qvs, axis=0), jnp.concatenate(dovs, axis=0)
            q_all_t, do_all_t = q_all.T, do_all.T
            dqs = [jnp.zeros((QROWS, LANES), F32) for _ in range(N_KV)]
            for start, size, masked in _key_chunks(i, latent, n_ctx=n_ctx, t_all=Tp, window=window,
                                                   chunk=_attn_chunk(Tp), latent_chunk=BWD_LATENT_CHUNK):
                rows = pl.ds(start, size)
                ds_all, p_all = [], []
                for j in range(N_KV):
                    p = jnp.exp(_scores(qvs[j], k_ref, i, start, size, masked, n_ctx=n_ctx) - lses[j])
                    dp = lax.dot_general(dovs[j], v_ref[rows, :], NT_DIMS, preferred_element_type=F32)
                    ds = (p * (dp - deltas[j])).astype(BF16)
                    dqs[j] = dqs[j] + jnp.dot(ds, k_ref[rows, :], preferred_element_type=F32)
                    ds_all.append(ds)
                    p_all.append(p.astype(BF16))
                ds_cat, p_cat = jnp.concatenate(ds_all, axis=0), jnp.concatenate(p_all, axis=0)
                if window:
                    dk_ref[rows, :] += lax.dot_general(ds_cat, q_all, TN_DIMS, preferred_element_type=F32)
                    dv_ref[rows, :] += lax.dot_general(p_cat, do_all, TN_DIMS, preferred_element_type=F32)
                else:
                    dkt_ref[:, start:start + size] += jnp.dot(q_all_t, ds_cat, preferred_element_type=F32)
                    dvt_ref[:, start:start + size] += jnp.dot(do_all_t, p_cat, preferred_element_type=F32)
            dqs = [dq * SCORE_SCALE for dq in dqs]
            for j in range(N_KV):
                if has_sink:
                    contrib = -(jnp.exp(_sink_column(sink_ref, j) - lses[j]) * deltas[j])
                    r = lax.broadcasted_iota(jnp.int32, (QROWS, 1), 0)
                    row8 = lax.broadcasted_iota(jnp.int32, (8, LANES), 0)
                    for h in range(N_GROUP):
                        in_head = (r >= h * Q_BLOCK) & (r < (h + 1) * Q_BLOCK)
                        tot = jnp.sum(jnp.where(in_head, contrib, 0.0), axis=0, keepdims=True)
                        upd = upd + jnp.where(row8 == j * N_GROUP + h, tot, 0.0)
            _unstack_heads(dqs, dq_ref)
            if has_sink:
                ds_ref[...] += upd

        @pl.when(i < n_ctx // Q_BLOCK)
        def _():
            run(False)

        @pl.when(i >= n_ctx // Q_BLOCK)
        def _():
            run(True)

        if not window:
            @pl.when(i == nq - 1)
            def _():
                dk_ref[...] += dkt_ref[...].T
                dv_ref[...] += dvt_ref[...].T

        finish_ride()

    ins, specs = [q, k, v, do, o32, lse], [q_in, kv_in, kv_in, q_out, q_out, lse_spec]
    out_specs = [q_out, kv_out, kv_out]
    out_shape = [_sds((T, Q_WIDTH), F32), _sds((T, KV_WIDTH), F32), _sds((T, KV_WIDTH), F32)]
    if has_sink:
        ins, specs = [sink] + ins, [pl.BlockSpec(memory_space=pltpu.SMEM)] + specs
        out_specs.append(sink_spec)
        out_shape.append(_sds((B, 8, LANES), F32))
    scratch = [pltpu.VMEM((KV_WIDTH, LANES if window else Tp), F32)] * 2
    ins, specs, out_specs, out_shape, scratch = _hitch(rider, ins, specs, out_specs, out_shape, scratch)
    return pl.pallas_call(
        body, name=name, grid=(B, nq), in_specs=specs, out_specs=out_specs, out_shape=out_shape, scratch_shapes=scratch,
        compiler_params=_params(("arbitrary", "arbitrary") if rider is not None else ("parallel", "arbitrary")),
    )(*ins)


def _window_sums(xp):
    n = xp.shape[0]

    def ahead(a, k):
        return pltpu.roll(a, n - k, 0)
    a2 = xp + ahead(xp, 1)
    a4 = a2 + ahead(a2, 2)
    a8 = a4 + ahead(a4, 4)
    a16 = a8 + ahead(a8, 8)
    return (a2, a4, a8, a16)


def _by_group(vals):
    lane = lax.broadcasted_iota(jnp.int32, vals[0].shape, 1)
    return jnp.where(lane < POOL_CH, vals[0], jnp.where(lane < 2 * POOL_CH, vals[1],
                     jnp.where(lane < 3 * POOL_CH, vals[2], vals[3])))


def _pool_counts(n):
    t = lax.broadcasted_iota(jnp.int32, (n, POOL_WIDTH), 0)
    cnts = [(jnp.minimum(t + w // 2, n) - jnp.maximum(t - w // 2, 0)).astype(F32) for w in POOL_WINDOWS]
    return _by_group(cnts)


def _pad_rows(x):
    zeros = jnp.zeros((POOL_PAD, x.shape[1]), x.dtype)
    return jnp.concatenate([zeros, x, zeros], axis=0)


def _pool_stream(u):
    n = u.shape[0]
    sums = _window_sums(_pad_rows(u))
    tots = [pltpu.roll(a, w // 2, 0)[POOL_PAD:POOL_PAD + n] for a, w in zip(sums, POOL_WINDOWS)]
    return _by_group(tots) / _pool_counts(n) - u


def _pool_stream_t(dp):
    n = dp.shape[0]
    sums = _window_sums(_pad_rows(dp / _pool_counts(n)))
    tots = [pltpu.roll(a, w // 2 - 1, 0)[POOL_PAD:POOL_PAD + n] if w > 2 else a[POOL_PAD:POOL_PAD + n]
            for a, w in zip(sums, POOL_WINDOWS)]
    return _by_group(tots) - dp


def _pool_fwd(z, w_bd, scale, *, B, Tp, n_ctx, name):
    T = z.shape[0]
    blk = pl.BlockSpec((Tp, POOL_WIDTH), lambda b: (b, U_COL // POOL_WIDTH))
    out = pl.BlockSpec((Tp, POOL_WIDTH), lambda b: (b, 0))

    def body(u_ref, w_ref, s_ref, p_ref, o_ref):
        for lo, hi in ((0, n_ctx), (n_ctx, Tp)):
            pooled = _pool_stream(u_ref[lo:hi, :]).astype(BF16)
            p_ref[lo:hi, :] = pooled
            mixed = jnp.dot(pooled, w_ref[...], preferred_element_type=F32)
            o_ref[lo:hi, :] = (mixed * s_ref[...]).astype(BF16)

    return pl.pallas_call(
        body, name=name, grid=(B,),
        in_specs=[blk, pl.BlockSpec((POOL_WIDTH, POOL_WIDTH), lambda b: (0, 0)), pl.BlockSpec((1, POOL_WIDTH), lambda b: (0, 0))],
        out_specs=[out, out], out_shape=[_sds((T, POOL_WIDTH), BF16)] * 2, compiler_params=_params(("parallel",)),
    )(z, w_bd, scale)


def _pool_bwd(d_ob, pooled, w_bd, scale, *, B, Tp, n_ctx, name):
    T = d_ob.shape[0]
    blk = pl.BlockSpec((Tp, POOL_WIDTH), lambda b: (b, 0))
    wsp = pl.BlockSpec((POOL_WIDTH, POOL_WIDTH), lambda b: (0, 0))
    ssp = pl.BlockSpec((1, POOL_WIDTH), lambda b: (0, 0))

    def body(d_ref, p_ref, w_ref, s_ref, du_ref, dw_ref, dsc_ref):
        @pl.when(pl.program_id(0) == 0)
        def _():
            dw_ref[...] = jnp.zeros_like(dw_ref)
            dsc_ref[...] = jnp.zeros_like(dsc_ref)

        dv, pv, wv = d_ref[...], p_ref[...], w_ref[...]
        mixed = jnp.dot(pv, wv, preferred_element_type=F32)
        dsc_ref[...] += jnp.sum(dv * mixed, axis=0, keepdims=True)
        dmixed = (dv * s_ref[...]).astype(BF16)
        dw_ref[...] += lax.dot_general(pv, dmixed, TN_DIMS, preferred_element_type=F32)
        dpooled = lax.dot_general(dmixed, wv, NT_DIMS, preferred_element_type=F32)
        for lo, hi in ((0, n_ctx), (n_ctx, Tp)):
            du_ref[lo:hi, :] = _pool_stream_t(dpooled[lo:hi, :]).astype(BF16)

    return pl.pallas_call(
        body, name=name, grid=(B,), in_specs=[blk, blk, wsp, ssp], out_specs=[blk, wsp, ssp],
        out_shape=[_sds((T, POOL_WIDTH), BF16), _sds((POOL_WIDTH, POOL_WIDTH), F32), _sds((1, POOL_WIDTH), F32)],
        compiler_params=_params(("arbitrary",)),
    )(d_ob, pooled, w_bd, scale)


def _merge_specs(z, D, TR, tc, wa, wb, wc):
    def act(width):
        return pl.BlockSpec((TR, width), lambda i, n: (i, 0))

    def gate(part):
        return pl.BlockSpec((TR, tc), lambda i, n: (i, (GATE_COL + part * D) // tc + n))
    w_specs = [w.spec(w.shape[0], tc, lambda i, n: (0, n)) for w in (wa, wb, wc)]
    return [act(Q_WIDTH), act(POOL_WIDTH), act(Q_WIDTH), gate(0), gate(1), gate(2)] + w_specs


def _merge_fwd(oa, ob, oc, z, wa, wb, wc, *, D, TR, name):
    T = oa.shape[0]
    tc = D // N_CHIPS

    def body(oa_ref, ob_ref, oc_ref, ga_ref, gb_ref, gc_ref, wa_ref, wb_ref, wc_ref, y_ref):
        acc = jax.nn.sigmoid(ga_ref[...]) * jnp.dot(oa_ref[...], wa_ref[...], preferred_element_type=F32)
        acc += jax.nn.sigmoid(gb_ref[...]) * jnp.dot(ob_ref[...], wb_ref[...], preferred_element_type=F32)
        acc += jax.nn.sigmoid(gc_ref[...]) * jnp.dot(oc_ref[...], wc_ref[...], preferred_element_type=F32)
        y_ref[...] = acc.astype(BF16)

    return pl.pallas_call(
        body, name=name, grid=(T // TR, D // tc), in_specs=_merge_specs(z, D, TR, tc, wa, wb, wc),
        out_specs=pl.BlockSpec((TR, tc), lambda i, n: (i, n)), out_shape=_sds((T, D), BF16),
        compiler_params=_params(("parallel", "parallel")),
    )(oa, ob, oc, z, z, z, wa.arr, wb.arr, wc.arr)


def _merge_bwd(dy, oa, ob, oc, z, wa, wb, wc, *, D, TR, name):
    T = oa.shape[0]
    tc = D // N_CHIPS
    out = pl.BlockSpec((TR, tc), lambda i, n: (i, n))

    def body(dy_ref, oa_ref, ob_ref, oc_ref, ga_ref, gb_ref, gc_ref, wa_ref, wb_ref, wc_ref,
             dpa_ref, dpb_ref, dpc_ref, dga_ref, dgb_ref, dgc_ref):
        dyv = dy_ref[...]
        for o_ref, g_ref, w_ref, dp_ref, dg_ref in ((oa_ref, ga_ref, wa_ref, dpa_ref, dga_ref),
                                                    (ob_ref, gb_ref, wb_ref, dpb_ref, dgb_ref),
                                                    (oc_ref, gc_ref, wc_ref, dpc_ref, dgc_ref)):
            s = jax.nn.sigmoid(g_ref[...])
            proj = jnp.dot(o_ref[...], w_ref[...], preferred_element_type=F32)
            dp_ref[...] = (dyv * s).astype(BF16)
            dg_ref[...] = (dyv * proj * (s * (1.0 - s))).astype(BF16)

    return pl.pallas_call(
        body, name=name, grid=(T // TR, D // tc), in_specs=[out] + _merge_specs(z, D, TR, tc, wa, wb, wc),
        out_specs=[out] * 6, out_shape=[_sds((T, D), BF16)] * 6, compiler_params=_params(("parallel", "parallel")),
    )(dy, oa, ob, oc, z, z, z, wa.arr, wb.arr, wc.arr)


def _silu_rows(cc, name):
    def body(c_ref, s_ref):
        v = c_ref[...]
        s_ref[...] = (v * jax.nn.sigmoid(v)).astype(BF16)
    return pl.pallas_call(body, name=name, out_shape=_sds(cc.shape, BF16))(cc)


def _ada_bwd_rows(dm, ds, cc, name):
    def body(dm_ref, ds_ref, c_ref, db_ref, dc_ref):
        db_ref[...] = jnp.sum(dm_ref[...], axis=0, keepdims=True)
        v = c_ref[...]
        s = jax.nn.sigmoid(v)
        dc_ref[...] = ds_ref[...] * (s * (1.0 + v * (1.0 - s)))
    return pl.pallas_call(body, name=name, out_shape=[_sds((1, dm.shape[1]), F32), _sds(cc.shape, F32)])(dm, ds, cc)


def _row_tile(rows, cols):
    for t in (512, 256, 128, 64, 32, 16, 8):
        if rows % t == 0 and t * cols * 4 <= (1 << 20):
            return t
    return rows


def _working_rows(tr, C, worker):
    return pl.BlockSpec((tr, C), lambda i, c: (jnp.where(c[0] == worker, i, 0), 0))


def _add_landed(own, landed, core, worker, name):
    R, C = own.shape
    tr = _row_tile(R, C)
    row = _working_rows(tr, C, worker)

    def body(c_ref, a_ref, b_ref, o_ref, o16_ref):
        @pl.when(c_ref[0] == worker)
        def _():
            tot = a_ref[...] + b_ref[...].astype(F32)
            o_ref[...] = tot
            o16_ref[...] = tot.astype(BF16)

    grid_spec = pltpu.PrefetchScalarGridSpec(num_scalar_prefetch=1, grid=(R // tr,), in_specs=[row, row], out_specs=[row, row])
    return pl.pallas_call(body, name=name, grid_spec=grid_spec, out_shape=[_sds((R, C), F32), _sds((R, C), BF16)],
                          compiler_params=_params(("arbitrary",)))(core, own, landed)


def _sum_chips(own, landed, chip, core, worker, name):
    _, R, C = own.shape
    tr = _row_tile(R, C)

    def row(i, c):
        return jnp.where(c[0] == worker, i, 0)

    def body(k_ref, c_ref, a_ref, b_ref, o_ref):
        @pl.when(c_ref[0] == worker)
        def _():
            o_ref[...] = ((a_ref[...] + b_ref[0].astype(F32)) + b_ref[1].astype(F32)) + b_ref[2].astype(F32)

    grid_spec = pltpu.PrefetchScalarGridSpec(
        num_scalar_prefetch=2, grid=(R // tr,),
        in_specs=[pl.BlockSpec((None, tr, C), lambda i, k, c: (k[0], row(i, c), 0)),
                  pl.BlockSpec((3, tr, C), lambda i, k, c: (0, row(i, c), 0))],
        out_specs=pl.BlockSpec((tr, C), lambda i, k, c: (row(i, c), 0)))
    return pl.pallas_call(body, name=name, grid_spec=grid_spec, out_shape=_sds((R, C), F32),
                          compiler_params=_params(("arbitrary",)))(chip, core, own, landed)


def _adam_math(w, g, m, v):
    m = ADAM_B1 * m + (1.0 - ADAM_B1) * g
    v = ADAM_B2 * v + (1.0 - ADAM_B2) * (g * g)
    m_hat = m / (1.0 - ADAM_B1 ** ADAM_STEP)
    v_hat = v / (1.0 - ADAM_B2 ** ADAM_STEP)
    delta = -ADAM_LR * (m_hat / (jnp.sqrt(v_hat) + ADAM_EPS) + ADAM_WD * w)
    return delta, m, v


def _adamw(w, reduced, shared, m, v, core, worker, name):
    L, R, C = w.shape
    tr = _row_tile(R, C)

    def body(c_ref, w_ref, r0_ref, r1_ref, s0_ref, s1_ref, m_ref, v_ref, g_ref, d_ref, mo_ref, vo_ref):
        def step(g):
            d, mn, vn = _adam_math(w_ref[...], g, m_ref[...], v_ref[...])
            g_ref[...] = g
            d_ref[...] = d
            mo_ref[...] = mn
            vo_ref[...] = vn

        layer, here = pl.program_id(0), c_ref[0] == worker
        for l, (r_ref, s_ref) in enumerate(((r0_ref, s0_ref), (r1_ref, s1_ref))):
            @pl.when((layer == l) & here)
            def _(r_ref=r_ref):
                step(r_ref[...])

            @pl.when((layer == l) & jnp.logical_not(here))
            def _(s_ref=s_ref):
                step(s_ref[...])

    lay = pl.BlockSpec((None, tr, C), lambda l, i, c: (l, i, 0))
    row = pl.BlockSpec((tr, C), lambda l, i, c: (i, 0))
    grid_spec = pltpu.PrefetchScalarGridSpec(num_scalar_prefetch=1, grid=(L, R // tr),
                                             in_specs=[lay, row, row, row, row, lay, lay], out_specs=[lay] * 4)
    return pl.pallas_call(body, name=name, grid_spec=grid_spec, out_shape=[_sds((L, R, C), F32)] * 4,
                          compiler_params=_params(("parallel", "parallel")))(core, w, *reduced, *shared, m, v)


def _adamw_small(w, parts, m, v, name):
    R, C = w.shape

    def body(w_ref, p_ref, m_ref, v_ref, g_ref, d_ref, mo_ref, vo_ref):
        g = p_ref[0]
        for dev in range(1, 8):
            g = g + p_ref[dev]
        d, mn, vn = _adam_math(w_ref[...], g, m_ref[...], v_ref[...])
        g_ref[...] = g
        d_ref[...] = d
        mo_ref[...] = mn
        vo_ref[...] = vn

    return pl.pallas_call(body, name=name, out_shape=[_sds((R, C), F32)] * 4)(w, parts, m, v)


def _place():
    return lax.axis_index("x"), lax.axis_index("y"), lax.axis_index("c")


def _other_chips(x, y):
    return [(1 - x, y), (x, 1 - y), (1 - x, 1 - y)]


def _rcopy(src, dst, ssem, rsem, dev):
    return pltpu.make_async_remote_copy(src_ref=src, dst_ref=dst, send_sem=ssem, recv_sem=rsem,
                                        device_id=dev, device_id_type=MESH)


GATHER_SEMS = 7


class _LayerGather:
    def __init__(self, shards, layer):
        self.inputs, self.layer, self.n = list(shards), layer, len(shards)
        load, self.groups = [0, 0], ([], [])
        for w in sorted(range(self.n), key=lambda w: -shards[w][0].size):
            g = 0 if load[0] <= load[1] else 1
            self.groups[g].append(w)
            load[g] += shards[w][0].size
        self.out_shape = [_sds((N_CHIPS,) + s.shape[1:], s.dtype) for s in shards]
        self.scratch = [pltpu.SemaphoreType.DMA((self.n, GATHER_SEMS)), pltpu.SemaphoreType.DMA((self.n, GATHER_SEMS))]

    def _own(self, src, out, send_sems, recv_sems):
        x, y, c = _place()
        return [_rcopy(src[w].at[self.layer], out[w].at[2 * x + y], send_sems.at[w, 6], recv_sems.at[w, 6], (x, y, 1 - c))
                for w in range(self.n)]

    def _to_chips(self, g, src, out, send_sems, recv_sems):
        x, y, c = _place()
        return [_rcopy(src[w].at[self.layer], out[w].at[2 * x + y], send_sems.at[w, j], recv_sems.at[w, j], (*chip, c))
                for w in self.groups[g] for j, chip in enumerate(_other_chips(x, y))]

    def start(self, src, out, send_sems, recv_sems):
        c = lax.axis_index("c")
        for cp in self._own(src, out, send_sems, recv_sems):
            cp.start()
        for g in (0, 1):
            @pl.when(c == g)
            def _(g=g):
                for cp in self._to_chips(g, src, out, send_sems, recv_sems):
                    cp.start()

    def finish(self, src, out, send_sems, recv_sems):
        x, y, c = _place()
        sibling = (x, y, 1 - c)
        chips = _other_chips(x, y)
        for g in (0, 1):
            @pl.when(c == g)
            def _(g=g):
                passed = []
                for w in self.groups[g]:
                    for j, (px, py) in enumerate(chips):
                        landed = out[w].at[2 * px + py]
                        _rcopy(landed, landed, send_sems.at[w, j], recv_sems.at[w, j], (px, py, c)).wait_recv()
                        cp = _rcopy(landed, landed, send_sems.at[w, 3 + j], recv_sems.at[w, 3 + j], sibling)
                        cp.start()
                        passed.append(cp)
                for w in self.groups[1 - g]:
                    for j, (px, py) in enumerate(chips):
                        landed = out[w].at[2 * px + py]
                        _rcopy(landed, landed, send_sems.at[w, 3 + j], recv_sems.at[w, 3 + j], sibling).wait_recv()
                for cp in self._to_chips(g, src, out, send_sems, recv_sems) + passed:
                    cp.wait_send()
        for cp in self._own(src, out, send_sems, recv_sems):
            cp.wait_recv()
            cp.wait_send()


def _on_core(fn):
    for g in (0, 1):
        @pl.when(lax.axis_index("c") == g)
        def _(g=g):
            fn(g)


class _ToSibling:
    def __init__(self, arrays, senders):
        self.inputs, self.senders = list(arrays), list(senders)
        n = len(self.inputs)
        self.out_shape = [_sds(a.shape, a.dtype) for a in self.inputs]
        self.scratch = [pltpu.SemaphoreType.DMA((n,)), pltpu.SemaphoreType.DMA((n,))]

    def _copies(self, sender, src, out, send_sems, recv_sems):
        x, y, c = _place()
        return [_rcopy(src[w], out[w], send_sems.at[w], recv_sems.at[w], (x, y, 1 - c))
                for w in range(len(src)) if self.senders[w] == sender]

    def start(self, *refs):
        def go(g):
            for cp in self._copies(g, *refs):
                cp.start()
        _on_core(go)

    def finish(self, *refs):
        def go(g):
            for cp in self._copies(1 - g, *refs):
                cp.wait_recv()
            for cp in self._copies(g, *refs):
                cp.wait_send()
        _on_core(go)


class _ChipSend:
    def __init__(self, blocked, senders):
        self.inputs, self.senders = list(blocked), list(senders)
        n = len(self.inputs)
        self.out_shape = [_sds((3,) + a.shape[1:], a.dtype) for a in self.inputs]
        self.scratch = [pltpu.SemaphoreType.DMA((n, 3)), pltpu.SemaphoreType.DMA((n, 3))]

    def _copies(self, sender, src, out, send_sems, recv_sems):
        x, y, c = _place()
        return [_rcopy(src[w].at[2 * px + py], out[w].at[j], send_sems.at[w, j], recv_sems.at[w, j], (px, py, c))
                for w in range(len(src)) if self.senders[w] == sender for j, (px, py) in enumerate(_other_chips(x, y))]

    def start(self, *refs):
        def go(g):
            for cp in self._copies(g, *refs):
                cp.start()
        _on_core(go)

    def finish(self, *refs):
        def go(g):
            cps = self._copies(g, *refs)
            for cp in cps:
                cp.wait_recv()
            for cp in cps:
                cp.wait_send()
        _on_core(go)


def _ride_alone(rider, name):
    n_in, n_out = len(rider.inputs), len(rider.out_shape)

    def body(*refs):
        args = (refs[:n_in], refs[n_in:n_in + n_out]) + tuple(refs[n_in + n_out:])
        rider.start(*args)
        rider.finish(*args)

    return pl.pallas_call(body, name=name, in_specs=[ANY] * n_in, out_specs=[ANY] * n_out, out_shape=rider.out_shape,
                          scratch_shapes=rider.scratch)(*rider.inputs)


def _hitch(rider, ins, in_specs, out_specs, out_shape, scratch):
    if rider is None:
        return ins, in_specs, out_specs, out_shape, scratch
    return (list(ins) + rider.inputs, list(in_specs) + [ANY] * len(rider.inputs),
            list(out_specs) + [ANY] * len(rider.out_shape), list(out_shape) + rider.out_shape, list(scratch) + rider.scratch)


def _ride(rider, refs, n_in, n_out, n_scratch, grid):
    if rider is None:
        return list(refs), lambda: None
    r_in, r_out = len(rider.inputs), len(rider.out_shape)
    refs = list(refs)
    own_in, ride_in = refs[:n_in], refs[n_in:n_in + r_in]
    rest = refs[n_in + r_in:]
    own_out, ride_out = rest[:n_out], rest[n_out:n_out + r_out]
    rest = rest[n_out + r_out:]
    own_scratch, sems = rest[:n_scratch], rest[n_scratch:]
    ids = [pl.program_id(a) for a in range(len(grid))]
    first = functools.reduce(jnp.logical_and, [i == 0 for i in ids])
    last = functools.reduce(jnp.logical_and, [i == g - 1 for i, g in zip(ids, grid)])

    @pl.when(first)
    def _():
        rider.start(ride_in, ride_out, *sems)

    def finish():
        @pl.when(last)
        def _():
            rider.finish(ride_in, ride_out, *sems)

    return own_in + own_out + own_scratch, finish


def _gather_small(block, name):
    m_per, n = block.shape

    def body(x_ref, out_ref, send_sems, recv_sems, local_sem):
        x, y, c = _place()
        me, sibling = (x, y, c), (x, y, 1 - c)
        chips = _other_chips(x, y)

        def rows(px, py, pc):
            return out_ref.at[pl.ds((4 * px + 2 * py + pc) * m_per, m_per), :]

        def copy(k, blk, to, src=None):
            return _rcopy(rows(*blk) if src is None else src, rows(*blk), send_sems.at[k], recv_sems.at[k], to)

        mine = pltpu.make_async_copy(x_ref, rows(*me), local_sem)
        mine.start()
        first = [copy(0, me, sibling, src=x_ref)]
        first += [copy(1 + j, me, (*chip, c), src=x_ref) for j, chip in enumerate(chips)]
        for cp in first:
            cp.start()
        passed = [copy(4 + j, (*chip, c), sibling) for j, chip in enumerate(chips)]
        for j, chip in enumerate(chips):
            copy(1 + j, (*chip, c), me).wait_recv()
            passed[j].start()
        copy(0, sibling, me).wait_recv()
        for j, chip in enumerate(chips):
            copy(4 + j, (*chip, 1 - c), me).wait_recv()
        for cp in first + passed:
            cp.wait_send()
        mine.wait()

    return pl.pallas_call(
        body, name=name, out_shape=_sds((8 * m_per, n), block.dtype),
        in_specs=[pl.BlockSpec(memory_space=pltpu.VMEM)], out_specs=pl.BlockSpec(memory_space=pltpu.VMEM),
        scratch_shapes=[pltpu.SemaphoreType.DMA((7,)), pltpu.SemaphoreType.DMA((7,)), pltpu.SemaphoreType.DMA],
    )(block)


def _rope_tables(n_ctx, seq):
    rows = seq // GRID_W
    r = jnp.repeat(jnp.arange(rows, dtype=F32), GRID_W)
    col = jnp.tile(jnp.arange(GRID_W, dtype=F32), rows)
    inv = 1.0 / (ROPE_THETA ** (jnp.arange(0, AXIS_DIM, 2, dtype=F32) / AXIS_DIM))
    ang = jnp.concatenate([r[:, None] * inv, col[:, None] * inv], axis=-1)
    cos = jnp.repeat(jnp.cos(ang), 2, axis=-1)
    sin = jnp.repeat(jnp.sin(ang), 2, axis=-1) * jnp.tile(jnp.array([-1.0, 1.0], F32), HEAD_DIM // 2)
    cos = jnp.concatenate([jnp.ones((n_ctx, HEAD_DIM), F32), cos], axis=0)
    sin = jnp.concatenate([jnp.zeros((n_ctx, HEAD_DIM), F32), sin], axis=0)
    return jnp.tile(cos, (1, 2)), jnp.tile(sin, (1, 2))


def _block_diag(w_pool):
    L, G = w_pool.shape[:2]
    eye = jnp.eye(G, dtype=w_pool.dtype)
    return (w_pool[:, :, :, None, :] * eye[None, :, None, :, None]).reshape(L, POOL_WIDTH, POOL_WIDTH)


def _qk_gains(small):
    qn = jnp.stack([small["q_norm_a"], small["q_norm_c"]], axis=1)[:, :, None, :]
    kn = jnp.stack([small["k_norm_a"], small["k_norm_c"]], axis=1)[:, :, None, :]
    L = qn.shape[0]
    rows = jnp.concatenate([jnp.broadcast_to(qn, (L, 2, N_HEADS, HEAD_DIM)), jnp.broadcast_to(kn, (L, 2, N_KV, HEAD_DIM)),
                            jnp.ones((L, 2, N_KV, HEAD_DIM), F32)], axis=2)
    return rows.reshape(L, 2, 1, QKV_WIDTH)


def _local_step(x, c, ctx, c_ctx, small, gw, target, rider=None, overlap=False):
    gw = list(gw)
    B, S, D = x.shape
    N = ctx.shape[1]
    L = small["norm1"].shape[0]
    Tp = N + S
    T = B * Tp
    TR = N
    P = Tp // N
    rows16 = 16
    assert N % Q_BLOCK == 0 and S % N == 0 and B + 1 <= rows16
    TM = _tile(T, (2304, 1536, 1024, 768, 512, 384, 256, 128))
    TM_WIDE = _tile(T, (1536, 1024, 768, 512, 384, 256, 128))
    TMG = _tile(T, (1024, 768, 512, 384, 256, 128))

    X = jnp.concatenate([ctx, x], axis=1).reshape(T, D)
    cc = jnp.concatenate([c, c_ctx[None], jnp.zeros((rows16 - B - 1, D), F32)], axis=0)
    s_rows = _silu_rows(cc, "silu_rows")
    cos, sin = _rope_tables(N, S)
    all_gains = _qk_gains(small)
    all_w_bd = _block_diag(small["w_pool"]).astype(BF16)

    def weights(l):
        g = gw[l]
        return dict(
            ada=_Opnd(g["w_ada"], "bcols"), w_in=_Opnd(g["w_in"], "bcols"),
            a=_Opnd(g["w_br_a"], "bcols"), b=_Opnd(g["w_br_b"], "bcols"), c=_Opnd(g["w_br_c"], "bcols"),
            out=_Opnd(g["w_out"], "brows"), mlp1=_Opnd(g["w_mlp1"], "bcols"), mlp2=_Opnd(g["w_mlp2"], "brows"))

    IN = weights(0)["w_in"].shape[1]
    DFF = weights(0)["mlp1"].shape[1]
    tn_in = _tile(IN // N_CHIPS, (1152, 768, 512, 384, 256, 128))
    tn_ff = _tile(DFF // N_CHIPS, (1024, 512, 256, 128))
    tn_ada = _tile(6 * D // N_CHIPS, (1536, 768, 512, 256, 128))
    tn_d = D // N_CHIPS
    tk_d = _tile(D, (1024, 512))
    tk_tok = _tile(T, (2304, 1536, 1024, 768, 512, 384, 256))

    saved = []
    xin, pending = X, None
    for l in range(L):
        W = weights(l)
        b_ada = small["b_ada"][l].reshape(1, 6 * D)
        mod = _matmul(s_rows, W["ada"], "nn", tm=rows16, tn=tn_ada, tk=D, name=f"ada_fwd{l}",
                      epilogue=lambda acc, b: (acc + b,), extras=[(b_ada, (1, tn_ada), lambda m, n: (0, n))])
        modtab = mod.reshape(rows16, 1, 6 * D)
        gains = all_gains[l]
        w_bd = all_w_bd[l]
        p_scale = small["pool_scale"][l].reshape(1, POOL_WIDTH)
        sink = small["sink_c"][l]

        x0, h1 = _res_norm(xin, pending, modtab, 0, 1, small["norm1"][l][None], TR=TR, P=P, name=f"norm1_fwd{l}")
        z = _matmul(h1, W["w_in"], "nn", tm=TM, tn=tn_in, tk=D, name=f"in_proj{l}")
        q2, k2, v2 = _qk_prep(z, gains, cos, sin, TR=TR, P=P, name=f"qk_prep{l}")
        riding = rider if l == 0 else None
        oa, oa32, lse_a, *landed = _attn_fwd(q2, k2, v2, None, branch=0, B=B, n_ctx=N, window=False,
                                             name=f"attn_a_fwd{l}", rider=riding)
        if riding is not None:
            gw[riding.layer] = dict(zip(BIG_NAMES, landed))
        oc, oc32, lse_c = _attn_fwd(q2, k2, v2, sink, branch=1, B=B, n_ctx=N, window=True, name=f"attn_c_fwd{l}")
        pooled, ob = _pool_fwd(z, w_bd, p_scale, B=B, Tp=Tp, n_ctx=N, name=f"pool_fwd{l}")
        y = _merge_fwd(oa, ob, oc, z, W["a"], W["b"], W["c"], D=D, TR=TMG, name=f"merge_fwd{l}")
        ao = _matmul(y, W["out"], "nn", tm=TM, tn=D, tk=tn_d, name=f"out_proj{l}")
        x1, h2 = _res_norm(x0, (ao, modtab, 2), modtab, 3, 4, small["norm2"][l][None], TR=TR, P=P, name=f"norm2_fwd{l}")
        a_pre, r_act = _matmul(h2, W["mlp1"], "nn", tm=TM_WIDE, tn=tn_ff, tk=D, name=f"mlp1_fwd{l}", out_dtypes=(F32, BF16),
                               epilogue=lambda acc: (acc, jnp.square(jnp.maximum(acc, 0.0))))
        mo = _matmul(r_act, W["mlp2"], "nn", tm=TM, tn=D, tk=tn_ff, name=f"mlp2_fwd{l}")
        saved.append(dict(modtab=modtab, gains=gains, w_bd=w_bd, p_scale=p_scale, sink=sink, x0=x0, h1=h1, z=z,
                          q2=q2, k2=k2, v2=v2, oa=oa, ob=ob, oc=oc, oa32=oa32, oc32=oc32, lse_a=lse_a, lse_c=lse_c,
                          pooled=pooled, y=y, ao=ao,
                          x1=x1, h2=h2, a_pre=a_pre, r_act=r_act, mo=mo))
        xin, pending = x1, (mo, modtab, 5)

    dxo, loss, d_mo, dg2 = _loss_head(xin, pending[0], pending[1], 5, target.reshape(B * S, D), TR=TR, P=P, name="loss_head")

    big = {k: [None] * L for k in BIG_NAMES}
    big16 = {k: [None] * L for k in BIG_NAMES}
    sm = {k: [None] * L for k in ("b_ada", "norm1", "norm2", "q_norm_a", "k_norm_a", "q_norm_c", "k_norm_c",
                                   "sink_c", "w_pool", "pool_scale")}

    def dw(key, l, a, b, *, tm, tn, name, tk=tk_tok, blocked=True):
        outs = _matmul(a, b, "tn", tm=tm, tn=tn, tk=tk, name=name, out_dtypes=(F32, BF16), out_blocked=blocked)
        if not blocked:
            outs = [o.reshape(N_CHIPS, o.shape[0] // N_CHIPS, o.shape[1]) for o in outs]
        big[key][l], big16[key][l] = outs
    d_cctx = jnp.zeros((D,), F32)
    for l in reversed(range(L)):
        W, sv = weights(l), saved[l]
        modtab = sv["modtab"]
        ride_now = overlap and l == L - 2
        if ride_now:
            early = _LayerReduce(l + 1, [big[k][l + 1] for k in BIG_NAMES], [big16[k][l + 1] for k in BIG_NAMES])
        d_a = _matmul(d_mo, W["mlp2"], "nt", tm=TM_WIDE, tn=tn_ff, tk=D, name=f"mlp2_bwd{l}", out_dtypes=(BF16,),
                      epilogue=lambda acc, a: (acc * (2.0 * jnp.maximum(a, 0.0)),),
                      extras=[(sv["a_pre"], (TM_WIDE, tn_ff), lambda m, n: (m, n))],
                      rider=early.to_worker if ride_now else None)
        if ride_now:
            d_a, landed = d_a
            early_send = early.add(landed)
        dw("w_mlp2", l, sv["r_act"], d_mo, tm=tk_d, tn=D, name=f"mlp2_dw{l}", blocked=False)
        d_h2 = _matmul(d_a, W["mlp1"], "nt", tm=TM, tn=D, tk=tn_ff, name=f"mlp1_bwd{l}")
        dw("w_mlp1", l, sv["h2"], d_a, tm=tk_d, tn=tn_ff, name=f"mlp1_dw{l}")
        dx1, dsh2, dsc2, dn2, d_ao, dg1 = _norm_bwd(sv["x1"], d_h2, dxo, modtab, 4, small["norm2"][l][None],
                                                    (sv["ao"], modtab, 2), TR=TR, P=P, name=f"norm2_bwd{l}")
        d_y = _matmul(d_ao, W["out"], "nt", tm=TM, tn=tn_d, tk=D, name=f"out_bwd{l}")
        dw("w_out", l, sv["y"], d_ao, tm=tk_d, tn=D, name=f"out_dw{l}", blocked=False)
        d_pa, d_pb, d_pc, d_ga, d_gb, d_gc = _merge_bwd(d_y, sv["oa"], sv["ob"], sv["oc"], sv["z"], W["a"], W["b"], W["c"],
                                                        D=D, TR=TMG, name=f"merge_bwd{l}")
        d_oa = _matmul(d_pa, W["a"], "nt", tm=TM, tn=Q_WIDTH, tk=tn_d, name=f"br_a_bwd{l}", out_dtypes=(BF16,))
        d_ob = _matmul(d_pb, W["b"], "nt", tm=TM, tn=POOL_WIDTH, tk=tn_d, name=f"br_b_bwd{l}")
        d_oc = _matmul(d_pc, W["c"], "nt", tm=TM, tn=Q_WIDTH, tk=tn_d, name=f"br_c_bwd{l}", out_dtypes=(BF16,))
        dw("w_br_a", l, sv["oa"], d_pa, tm=Q_WIDTH, tn=tn_d, name=f"br_a_dw{l}")
        dw("w_br_b", l, sv["ob"], d_pb, tm=POOL_WIDTH, tn=tn_d, name=f"br_b_dw{l}")
        dw("w_br_c", l, sv["oc"], d_pc, tm=Q_WIDTH, tn=tn_d, name=f"br_c_dw{l}")
        d_u, d_wbd, d_ps = _pool_bwd(d_ob, sv["pooled"], sv["w_bd"], sv["p_scale"], B=B, Tp=Tp, n_ctx=N, name=f"pool_bwd{l}")
        dqa, dka, dva, *arrived = _attn_bwd(sv["q2"], sv["k2"], sv["v2"], d_oa, sv["oa32"], sv["lse_a"], None, branch=0,
                                            B=B, n_ctx=N, window=False, name=f"attn_a_bwd{l}",
                                            rider=early_send if ride_now else None)
        if ride_now:
            early.from_chips = arrived
        dqc, dkc, dvc, dsink = _attn_bwd(sv["q2"], sv["k2"], sv["v2"], d_oc, sv["oc32"], sv["lse_c"], sv["sink"],
                                         branch=1, B=B, n_ctx=N, window=True, name=f"attn_c_bwd{l}")
        dz_a, dgains_a = _qk_prep_bwd(sv["z"], dqa, dka, dva, sv["gains"], cos, sin, branch=0, TR=TR, P=P,
                                      name=f"qk_prep_a_bwd{l}")
        dz_c, dgains_c = _qk_prep_bwd(sv["z"], dqc, dkc, dvc, sv["gains"], cos, sin, branch=1, TR=TR, P=P,
                                      name=f"qk_prep_c_bwd{l}")
        dz = jnp.concatenate([dz_a, dz_c, d_u, d_ga, d_gb, d_gc], axis=1)
        d_h1 = _matmul(dz, W["w_in"], "nt", tm=TM, tn=D, tk=tn_in, name=f"in_bwd{l}")
        dw("w_in", l, sv["h1"], dz, tm=tk_d, tn=tn_in, name=f"in_dw{l}")
        below = (saved[l - 1]["mo"], saved[l - 1]["modtab"], 5) if l > 0 else None
        dx0, dsh1, dsc1, dn1, *lower = _norm_bwd(sv["x0"], d_h1, dx1, modtab, 1, small["norm1"][l][None], below,
                                                 TR=TR, P=P, name=f"norm1_bwd{l}")
        this_dg2 = dg2
        if l > 0:
            d_mo, dg2 = lower

        dm_groups = jnp.concatenate([dsh1, dsc1, dg1, dsh2, dsc2, this_dg2], axis=-1).reshape(B, 2, 6 * D)
        dm = jnp.concatenate([dm_groups[:, 1], jnp.sum(dm_groups[:, 0], axis=0, keepdims=True),
                              jnp.zeros((rows16 - B - 1, 6 * D), F32)], axis=0)
        dm_bf = dm.astype(BF16)
        d_s = _matmul(dm_bf, W["ada"], "nt", tm=rows16, tn=D, tk=tn_ada, name=f"ada_bwd{l}")
        dw("w_ada", l, s_rows, dm_bf, tm=tk_d, tn=tn_ada, tk=rows16, name=f"ada_dw{l}")
        db_ada, dcc = _ada_bwd_rows(dm, d_s, cc, f"ada_rows_bwd{l}")
        d_cctx = d_cctx + dcc[B]

        sm["b_ada"][l] = db_ada[0]
        sm["norm1"][l] = jnp.sum(dn1, axis=(0, 1))
        sm["norm2"][l] = jnp.sum(dn2, axis=(0, 1))
        dgh = jnp.stack([dgains_a, dgains_c]).reshape(2, QKV_WIDTH // HEAD_DIM, HEAD_DIM)
        sm["q_norm_a"][l] = jnp.sum(dgh[0, :N_HEADS], axis=0)
        sm["k_norm_a"][l] = jnp.sum(dgh[0, N_HEADS:N_HEADS + N_KV], axis=0)
        sm["q_norm_c"][l] = jnp.sum(dgh[1, :N_HEADS], axis=0)
        sm["k_norm_c"][l] = jnp.sum(dgh[1, N_HEADS:N_HEADS + N_KV], axis=0)
        sm["sink_c"][l] = jnp.sum(dsink[:, :N_HEADS, 0], axis=0)
        sm["w_pool"][l] = jnp.stack([d_wbd[g * POOL_CH:(g + 1) * POOL_CH, g * POOL_CH:(g + 1) * POOL_CH]
                                     for g in range(POOL_WIDTH // POOL_CH)])
        sm["pool_scale"][l] = d_ps[0]
        dxo = dx0

    grad_x = dxo.reshape(B, Tp, D)[:, N:]
    small_grads = {k: jnp.stack(v) for k, v in sm.items()}
    small_grads["c_ctx"] = d_cctx
    return loss, grad_x, small_grads, big, big16, (early if overlap else None)


SMALL_NAMES = ("c_ctx", "b_ada", "norm1", "norm2", "q_norm_a", "k_norm_a", "q_norm_c", "k_norm_c", "sink_c",
               "w_pool", "pool_scale")
BIG_NAMES = ("w_ada", "w_in", "w_br_a", "w_br_b", "w_br_c", "w_out", "w_mlp1", "w_mlp2")
WEIGHT_NAMES = ("c_ctx", "w_ada", "b_ada", "norm1", "norm2", "w_in", "q_norm_a", "k_norm_a", "q_norm_c", "k_norm_c",
                "sink_c", "w_pool", "pool_scale", "w_br_a", "w_br_b", "w_br_c", "w_out", "w_mlp1", "w_mlp2")


def _pack(parts, rows):
    flat = jnp.concatenate([p.reshape(-1).astype(F32) for p in parts])
    return jnp.pad(flat, (0, rows * LANES - flat.shape[0])).reshape(rows, LANES)


def _unpack(packed, like):
    flat, out, at = packed.reshape(-1), [], 0
    for p in like:
        out.append(flat[at:at + p.size].reshape(p.shape))
        at += p.size
    return out


def _split_by_bytes(arrays):
    load, owner = [0, 0], [0] * len(arrays)
    for w in sorted(range(len(arrays)), key=lambda w: -arrays[w].size):
        owner[w] = 0 if load[0] <= load[1] else 1
        load[owner[w]] += arrays[w].size
    return owner


class _LayerReduce:
    def __init__(self, layer, partials, partials16):
        self.layer, self.partials = layer, list(partials)
        self.workers = _split_by_bytes(self.partials)
        self.to_worker = _ToSibling([g.reshape(-1, g.shape[-1]) for g in partials16], [1 - wk for wk in self.workers])
        x, y, c = _place()
        self.core = c.astype(jnp.int32).reshape(1)
        self.chip = (2 * x + y).astype(jnp.int32).reshape(1)

    def add(self, landed):
        sums = [_add_landed(g.reshape(-1, g.shape[-1]), r, self.core, wk, f"grads{self.layer}_add_sibling_{k}")
                for k, g, r, wk in zip(BIG_NAMES, self.partials, landed, self.workers)]
        self.in_chip = [h.reshape(g.shape) for g, (h, _) in zip(self.partials, sums)]
        return _ChipSend([h.reshape(g.shape) for g, (_, h) in zip(self.partials, sums)], self.workers)

    def sum(self, from_chips):
        return [_sum_chips(h, r, self.chip, self.core, wk, f"grads{self.layer}_sum_chips_{k}")
                for k, h, r, wk in zip(BIG_NAMES, self.in_chip, from_chips, self.workers)]


def kernel(x, c, ctx, c_ctx, w_ada, b_ada, norm1, norm2, w_in, q_norm_a, k_norm_a, q_norm_c, k_norm_c, sink_c, w_pool, pool_scale, w_br_a, w_br_b, w_br_c, w_out, w_mlp1, w_mlp2, loss_target, m_c_ctx, m_w_ada, m_b_ada, m_norm1, m_norm2, m_w_in, m_q_norm_a, m_k_norm_a, m_q_norm_c, m_k_norm_c, m_sink_c, m_w_pool, m_pool_scale, m_w_br_a, m_w_br_b, m_w_br_c, m_w_out, m_w_mlp1, m_w_mlp2, v_c_ctx, v_w_ada, v_b_ada, v_norm1, v_norm2, v_w_in, v_q_norm_a, v_k_norm_a, v_q_norm_c, v_k_norm_c, v_sink_c, v_w_pool, v_pool_scale, v_w_br_a, v_w_br_b, v_w_br_c, v_w_out, v_w_mlp1, v_w_mlp2):
    given = dict(locals())
    w = {k: given[k] for k in WEIGHT_NAMES}
    m = {k: given["m_" + k] for k in WEIGHT_NAMES}
    v = {k: given["v_" + k] for k in WEIGHT_NAMES}

    shards = [w[k].astype(BF16) for k in BIG_NAMES]
    assert all(s.shape[0] == 2 for s in shards)
    first_layer = dict(zip(BIG_NAMES, _ride_alone(_LayerGather(shards, 0), "gather_weights0")))
    small = {k: w[k] for k in SMALL_NAMES}
    loss_part, grad_x, small_grads, big_grads, big_grads16, early = _local_step(
        x, c, ctx, c_ctx, small, [first_layer, None], loss_target, rider=_LayerGather(shards, 1), overlap=True)

    late = _LayerReduce(0, [big_grads[k][0] for k in BIG_NAMES], [big_grads16[k][0] for k in BIG_NAMES])
    send = late.add(_ride_alone(late.to_worker, "grads0_to_sibling"))
    reduced = [late.sum(_ride_alone(send, "grads0_to_chips")), early.sum(early.from_chips)]
    n_big = len(BIG_NAMES)
    shared = _ride_alone(_ToSibling(reduced[0] + reduced[1], late.workers + early.workers), "grads_share")
    grads, deltas, new_m, new_v = {}, {}, {}, {}
    for i, k in enumerate(BIG_NAMES):
        assert late.workers[i] == early.workers[i]
        grads[k], deltas[k], new_m[k], new_v[k] = _adamw(
            w[k], (reduced[0][i], reduced[1][i]), (shared[i], shared[n_big + i]), m[k], v[k], late.core, late.workers[i],
            f"adamw_{k}")

    sizes = sum(w[k].size for k in SMALL_NAMES) + LANES
    rows = -(-sizes // (8 * LANES)) * 8
    parts = _gather_small(_pack([small_grads[k] for k in SMALL_NAMES] + [loss_part[0]], rows), "gather_small")
    zero = jnp.zeros((LANES,), F32)
    packed = [_pack([t[k] for k in SMALL_NAMES] + [zero], rows) for t in (w, m, v)]
    outs = _adamw_small(packed[0], parts.reshape(8, rows, LANES), packed[1], packed[2], "adamw_small")
    like = [w[k] for k in SMALL_NAMES] + [zero]
    for store, packed_out in zip((grads, deltas, new_m, new_v), outs):
        pieces = _unpack(packed_out, like)
        for k, piece in zip(SMALL_NAMES, pieces):
            store[k] = piece
        if store is grads:
            loss = pieces[-1][0]

    return (loss, grad_x, *[grads[k] for k in WEIGHT_NAMES], *[deltas[k] for k in WEIGHT_NAMES],
            *[new_m[k] for k in WEIGHT_NAMES], *[new_v[k] for k in WEIGHT_NAMES])
```
